```python
import jax, jax.numpy as jnp
from jax import lax
import numpy as np

D_MODEL = 1024
BATCH = 8
SEQ = 8192
DEPTH = 1

PLE_DIM = 256
D_FF = 2816
SB_HEADS = 8
SB_HEAD_DIM = 64
SB_BLOCK = 128
GDN_HEADS = 8
GDN_DK = 64
GDN_DV = 64
GDN_CHUNK = 64
CONV_K = 4
SB_WIDTH = SB_HEADS * SB_HEAD_DIM
GDN_KW = GDN_HEADS * GDN_DK
GDN_VW = GDN_HEADS * GDN_DV
GDN_CONV_CH = 2 * GDN_KW + GDN_VW
SPLITS = (SB_WIDTH, SB_WIDTH, SB_WIDTH, GDN_KW, GDN_KW, GDN_VW, GDN_VW, GDN_HEADS, GDN_HEADS, D_MODEL, D_MODEL)
N_IN = 3 * SB_WIDTH + 2 * GDN_KW + 2 * GDN_VW + 2 * GDN_HEADS + 2 * D_MODEL
DEEPNORM_ALPHA = (2 * DEPTH) ** 0.25
DEEPNORM_BETA = (8 * DEPTH) ** -0.25
LN_EPS = 1e-5
RMS_EPS = 1e-6

kernel_name = "hybrid_stickbreak_gdn_macaron_deepnorm"


def layer_norm(x, g, b):
    x32 = x.astype(jnp.float32)
    mu = jnp.mean(x32, axis=-1, keepdims=True)
    xc = x32 - mu
    var = jnp.mean(xc * xc, axis=-1, keepdims=True)
    return (xc * lax.rsqrt(var + LN_EPS) * g.astype(jnp.float32) + b.astype(jnp.float32)).astype(x.dtype)


def swiglu(h, w_in, w_out):
    gate, up = jnp.split(h @ w_in, 2, axis=-1)
    return (jax.nn.silu(gate) * up) @ w_out


def causal_depthwise_conv(x, w):
    ch = x.shape[-1]
    return lax.conv_general_dilated(x, w[:, None, :].astype(x.dtype), window_strides=(1,), padding=[(CONV_K - 1, 0)], dimension_numbers=("NWC", "WIO", "NWC"), feature_group_count=ch)


def l2norm(t):
    return t * lax.rsqrt(jnp.sum(t * t, axis=-1, keepdims=True) + RMS_EPS)


def stick_breaking_attention(q, k, v):
    S = q.shape[1]
    scale = q.shape[-1] ** -0.5
    outs = []
    for blk in range(S // SB_BLOCK):
        t0 = blk * SB_BLOCK
        t1 = t0 + SB_BLOCK
        z = jnp.einsum("bqhd,bkhd->bhqk", q[:, t0:t1], k[:, :t1]).astype(jnp.float32) * scale
        qpos = t0 + jnp.arange(SB_BLOCK)[:, None]
        kpos = jnp.arange(t1)[None, :]
        mask = kpos < qpos
        log_fail = jnp.where(mask, -jax.nn.softplus(z), 0.0)
        after = lax.cumsum(log_fail, axis=3, reverse=True) - log_fail
        weights = jnp.where(mask, jnp.exp(jax.nn.log_sigmoid(z) + after), 0.0)
        outs.append(jnp.einsum("bhqk,bkhd->bqhd", weights.astype(v.dtype), v[:, :t1]))
    return jnp.concatenate(outs, axis=1)


def gated_delta_rule(q, k, v, beta, g):
    B, S, H, dk = q.shape
    dv = v.shape[-1]
    C = GDN_CHUNK
    n = S // C
    q = l2norm(q) * dk ** -0.5
    k = l2norm(k)

    def chunks(t):
        return t.reshape(B, n, C, H, -1).transpose(0, 3, 1, 2, 4)

    q, k, v = chunks(q), chunks(k), chunks(v)
    beta = beta.reshape(B, n, C, H).transpose(0, 3, 1, 2)
    g = jnp.cumsum(g.reshape(B, n, C, H).transpose(0, 3, 1, 2), axis=-1)
    causal = jnp.tril(jnp.ones((C, C), dtype=bool))
    strict = jnp.tril(jnp.ones((C, C), dtype=bool), k=-1)
    decay = jnp.exp(jnp.where(causal, g[..., :, None] - g[..., None, :], -jnp.inf))
    kk = jnp.einsum("bhncd,bhnmd->bhncm", k, k)
    lower = jnp.where(strict, beta[..., :, None] * kk * decay, 0.0)
    rhs = jnp.concatenate([v * beta[..., None], k * (beta * jnp.exp(g))[..., None]], axis=-1)
    sol = lax.linalg.triangular_solve(lower, rhs, left_side=True, lower=True, unit_diagonal=True)
    u, w = sol[..., :dv], sol[..., dv:]
    qk = jnp.where(causal, jnp.einsum("bhncd,bhnmd->bhncm", q, k) * decay, 0.0)
    g_last = g[..., -1:]
    q_dec = q * jnp.exp(g)[..., None]
    k_dec = k * jnp.exp(g_last - g)[..., None]
    chunk_decay = jnp.exp(g_last[..., 0])
    xs = (jnp.moveaxis(qk, 2, 0), jnp.moveaxis(u, 2, 0), jnp.moveaxis(w, 2, 0), jnp.moveaxis(q_dec, 2, 0), jnp.moveaxis(k_dec, 2, 0), jnp.moveaxis(chunk_decay, 2, 0))

    def step(state, inp):
        qk_c, u_c, w_c, qd_c, kd_c, dec_c = inp
        v_new = u_c - jnp.einsum("bhck,bhkv->bhcv", w_c, state)
        o = jnp.einsum("bhck,bhkv->bhcv", qd_c, state) + jnp.einsum("bhcm,bhmv->bhcv", qk_c, v_new)
        state = state * dec_c[..., None, None] + jnp.einsum("bhck,bhcv->bhkv", kd_c, v_new)
        return state, o

    state0 = jnp.zeros((B, H, dk, dv), jnp.float32)
    _, o = lax.scan(step, state0, xs)
    return o.transpose(1, 0, 3, 2, 4).reshape(B, S, H, dv)


def hybrid_mixer(h, w_in, b_gate, conv_w, a_log, dt_bias, gdn_norm_w, w_branch_sb, w_branch_gdn, w_mix_out):
    B, S, _ = h.shape
    idx = np.cumsum(np.array(SPLITS))[:-1].tolist()
    sb_q, sb_k, sb_v, gq, gk, gv, gz, gb, ga, gate_sb, gate_gdn = jnp.split(h @ w_in, idx, axis=-1)
    y_sb = stick_breaking_attention(sb_q.reshape(B, S, SB_HEADS, SB_HEAD_DIM), sb_k.reshape(B, S, SB_HEADS, SB_HEAD_DIM), sb_v.reshape(B, S, SB_HEADS, SB_HEAD_DIM))
    y_sb = y_sb.reshape(B, S, SB_WIDTH) @ w_branch_sb
    qkv = jax.nn.silu(causal_depthwise_conv(jnp.concatenate([gq, gk, gv], axis=-1), conv_w))
    cq, ck, cv = jnp.split(qkv.astype(jnp.float32), [GDN_KW, 2 * GDN_KW], axis=-1)
    beta = jax.nn.sigmoid(gb.astype(jnp.float32))
    g = -jnp.exp(a_log.astype(jnp.float32)) * jax.nn.softplus(ga.astype(jnp.float32) + dt_bias.astype(jnp.float32))
    o = gated_delta_rule(cq.reshape(B, S, GDN_HEADS, GDN_DK), ck.reshape(B, S, GDN_HEADS, GDN_DK), cv.reshape(B, S, GDN_HEADS, GDN_DV), beta, g)
    o = o * lax.rsqrt(jnp.mean(o * o, axis=-1, keepdims=True) + RMS_EPS) * gdn_norm_w.astype(jnp.float32)
    o = o * jax.nn.silu(gz.astype(jnp.float32).reshape(B, S, GDN_HEADS, GDN_DV))
    y_gdn = o.reshape(B, S, GDN_VW).astype(h.dtype) @ w_branch_gdn
    g_sb = jax.nn.sigmoid(gate_sb + b_gate[:D_MODEL])
    g_gdn = jax.nn.sigmoid(gate_gdn + b_gate[D_MODEL:])
    return (g_sb * y_sb + g_gdn * y_gdn) @ w_mix_out


def _fwd_setup_inputs(seed: int = 0) -> dict:
    key = jax.random.key(seed)
    ks = jax.random.split(key, 32)
    f32 = jnp.float32

    def nrm(k, shape, scale):
        return jax.random.normal(k, shape, f32) * scale

    dt = jnp.exp(jax.random.uniform(ks[14], (DEPTH, GDN_HEADS), f32, minval=np.log(1e-3), maxval=np.log(0.1)))
    return {
        "x": nrm(ks[0], (BATCH, SEQ, D_MODEL), 1.0),
        "p": nrm(ks[1], (DEPTH, BATCH, SEQ, PLE_DIM), 1.0),
        "ffn1_w_in": nrm(ks[2], (DEPTH, D_MODEL, 2 * D_FF), D_MODEL ** -0.5),
        "ffn1_w_out": nrm(ks[3], (DEPTH, D_FF, D_MODEL), D_FF ** -0.5 * DEEPNORM_BETA),
        "ln1_g": 1.0 + nrm(ks[4], (DEPTH, D_MODEL), 0.02),
        "ln1_b": nrm(ks[5], (DEPTH, D_MODEL), 0.02),
        "w_mix_in": nrm(ks[6], (DEPTH, D_MODEL, N_IN), D_MODEL ** -0.5),
        "b_gate": nrm(ks[7], (DEPTH, 2 * D_MODEL), 0.1),
        "conv_w": nrm(ks[8], (DEPTH, CONV_K, GDN_CONV_CH), CONV_K ** -0.5),
        "a_log": jnp.log(jax.random.uniform(ks[9], (DEPTH, GDN_HEADS), f32, minval=1.0, maxval=16.0)),
        "dt_bias": dt + jnp.log(-jnp.expm1(-dt)),
        "gdn_norm_w": 1.0 + nrm(ks[10], (DEPTH, GDN_DV), 0.02),
        "w_branch_sb": nrm(ks[11], (DEPTH, SB_WIDTH, D_MODEL), SB_WIDTH ** -0.5),
        "w_branch_gdn": nrm(ks[12], (DEPTH, GDN_VW, D_MODEL), GDN_VW ** -0.5),
        "w_mix_out": nrm(ks[13], (DEPTH, D_MODEL, D_MODEL), D_MODEL ** -0.5 * DEEPNORM_BETA),
        "ln2_g": 1.0 + nrm(ks[15], (DEPTH, D_MODEL), 0.02),
        "ln2_b": nrm(ks[16], (DEPTH, D_MODEL), 0.02),
        "ffn2_w_in": nrm(ks[17], (DEPTH, D_MODEL, 2 * D_FF), D_MODEL ** -0.5),
        "ffn2_w_out": nrm(ks[18], (DEPTH, D_FF, D_MODEL), D_FF ** -0.5 * DEEPNORM_BETA),
        "ln3_g": 1.0 + nrm(ks[19], (DEPTH, D_MODEL), 0.02),
        "ln3_b": nrm(ks[20], (DEPTH, D_MODEL), 0.02),
        "w_ple_gate": nrm(ks[21], (DEPTH, D_MODEL, D_MODEL), D_MODEL ** -0.5),
        "b_ple_gate": nrm(ks[22], (DEPTH, D_MODEL), 0.1),
        "w_ple": nrm(ks[23], (DEPTH, PLE_DIM, D_MODEL), PLE_DIM ** -0.5 * DEEPNORM_BETA),
        "ln4_g": 1.0 + nrm(ks[24], (DEPTH, D_MODEL), 0.02),
        "ln4_b": nrm(ks[25], (DEPTH, D_MODEL), 0.02),
    }


def _fwd_reference(x, p, ffn1_w_in, ffn1_w_out, ln1_g, ln1_b, w_mix_in, b_gate, conv_w, a_log, dt_bias, gdn_norm_w, w_branch_sb, w_branch_gdn, w_mix_out, ln2_g, ln2_b, ffn2_w_in, ffn2_w_out, ln3_g, ln3_b, w_ple_gate, b_ple_gate, w_ple, ln4_g, ln4_b):
    h = x
    for i in range(DEPTH):
        h = layer_norm(DEEPNORM_ALPHA * h + 0.5 * swiglu(h, ffn1_w_in[i], ffn1_w_out[i]), ln1_g[i], ln1_b[i])
        mix = hybrid_mixer(h, w_mix_in[i], b_gate[i], conv_w[i], a_log[i], dt_bias[i], gdn_norm_w[i], w_branch_sb[i], w_branch_gdn[i], w_mix_out[i])
        h = layer_norm(DEEPNORM_ALPHA * h + mix, ln2_g[i], ln2_b[i])
        h = layer_norm(DEEPNORM_ALPHA * h + 0.5 * swiglu(h, ffn2_w_in[i], ffn2_w_out[i]), ln3_g[i], ln3_b[i])
        ple = jax.nn.sigmoid(h @ w_ple_gate[i] + b_ple_gate[i]) * (p[i] @ w_ple[i])
        h = layer_norm(DEEPNORM_ALPHA * h + ple, ln4_g[i], ln4_b[i])
    return h


import jax as _jax
import jax.numpy as _jnp

TWIN_FORMAT = 'train_step'
FWD_PARAMS = ['x', 'p', 'ffn1_w_in', 'ffn1_w_out', 'ln1_g', 'ln1_b', 'w_mix_in', 'b_gate', 'conv_w', 'a_log', 'dt_bias', 'gdn_norm_w', 'w_branch_sb', 'w_branch_gdn', 'w_mix_out', 'ln2_g', 'ln2_b', 'ffn2_w_in', 'ffn2_w_out', 'ln3_g', 'ln3_b', 'w_ple_gate', 'b_ple_gate', 'w_ple', 'ln4_g', 'ln4_b']
TWIN_WEIGHTS = ['ffn1_w_in', 'ffn1_w_out', 'ln1_g', 'ln1_b', 'w_mix_in', 'b_gate', 'conv_w', 'a_log', 'dt_bias', 'gdn_norm_w', 'w_branch_sb', 'w_branch_gdn', 'w_mix_out', 'ln2_g', 'ln2_b', 'ffn2_w_in', 'ffn2_w_out', 'ln3_g', 'ln3_b', 'w_ple_gate', 'b_ple_gate', 'w_ple', 'ln4_g', 'ln4_b']
TWIN_DIFF_INPUT = 'x'
TWIN_INPUTS = ['x', 'p', 'ffn1_w_in', 'ffn1_w_out', 'ln1_g', 'ln1_b', 'w_mix_in', 'b_gate', 'conv_w', 'a_log', 'dt_bias', 'gdn_norm_w', 'w_branch_sb', 'w_branch_gdn', 'w_mix_out', 'ln2_g', 'ln2_b', 'ffn2_w_in', 'ffn2_w_out', 'ln3_g', 'ln3_b', 'w_ple_gate', 'b_ple_gate', 'w_ple', 'ln4_g', 'ln4_b', 'loss_target', 'm_ffn1_w_in', 'm_ffn1_w_out', 'm_ln1_g', 'm_ln1_b', 'm_w_mix_in', 'm_b_gate', 'm_conv_w', 'm_a_log', 'm_dt_bias', 'm_gdn_norm_w', 'm_w_branch_sb', 'm_w_branch_gdn', 'm_w_mix_out', 'm_ln2_g', 'm_ln2_b', 'm_ffn2_w_in', 'm_ffn2_w_out', 'm_ln3_g', 'm_ln3_b', 'm_w_ple_gate', 'm_b_ple_gate', 'm_w_ple', 'm_ln4_g', 'm_ln4_b', 'v_ffn1_w_in', 'v_ffn1_w_out', 'v_ln1_g', 'v_ln1_b', 'v_w_mix_in', 'v_b_gate', 'v_conv_w', 'v_a_log', 'v_dt_bias', 'v_gdn_norm_w', 'v_w_branch_sb', 'v_w_branch_gdn', 'v_w_mix_out', 'v_ln2_g', 'v_ln2_b', 'v_ffn2_w_in', 'v_ffn2_w_out', 'v_ln3_g', 'v_ln3_b', 'v_w_ple_gate', 'v_b_ple_gate', 'v_w_ple', 'v_ln4_g', 'v_ln4_b']
TWIN_OUTPUTS = ['loss', 'grad_x', 'grad_ffn1_w_in', 'grad_ffn1_w_out', 'grad_ln1_g', 'grad_ln1_b', 'grad_w_mix_in', 'grad_b_gate', 'grad_conv_w', 'grad_a_log', 'grad_dt_bias', 'grad_gdn_norm_w', 'grad_w_branch_sb', 'grad_w_branch_gdn', 'grad_w_mix_out', 'grad_ln2_g', 'grad_ln2_b', 'grad_ffn2_w_in', 'grad_ffn2_w_out', 'grad_ln3_g', 'grad_ln3_b', 'grad_w_ple_gate', 'grad_b_ple_gate', 'grad_w_ple', 'grad_ln4_g', 'grad_ln4_b', 'delta_ffn1_w_in', 'delta_ffn1_w_out', 'delta_ln1_g', 'delta_ln1_b', 'delta_w_mix_in', 'delta_b_gate', 'delta_conv_w', 'delta_a_log', 'delta_dt_bias', 'delta_gdn_norm_w', 'delta_w_branch_sb', 'delta_w_branch_gdn', 'delta_w_mix_out', 'delta_ln2_g', 'delta_ln2_b', 'delta_ffn2_w_in', 'delta_ffn2_w_out', 'delta_ln3_g', 'delta_ln3_b', 'delta_w_ple_gate', 'delta_b_ple_gate', 'delta_w_ple', 'delta_ln4_g', 'delta_ln4_b', 'new_m_ffn1_w_in', 'new_m_ffn1_w_out', 'new_m_ln1_g', 'new_m_ln1_b', 'new_m_w_mix_in', 'new_m_b_gate', 'new_m_conv_w', 'new_m_a_log', 'new_m_dt_bias', 'new_m_gdn_norm_w', 'new_m_w_branch_sb', 'new_m_w_branch_gdn', 'new_m_w_mix_out', 'new_m_ln2_g', 'new_m_ln2_b', 'new_m_ffn2_w_in', 'new_m_ffn2_w_out', 'new_m_ln3_g', 'new_m_ln3_b', 'new_m_w_ple_gate', 'new_m_b_ple_gate', 'new_m_w_ple', 'new_m_ln4_g', 'new_m_ln4_b', 'new_v_ffn1_w_in', 'new_v_ffn1_w_out', 'new_v_ln1_g', 'new_v_ln1_b', 'new_v_w_mix_in', 'new_v_b_gate', 'new_v_conv_w', 'new_v_a_log', 'new_v_dt_bias', 'new_v_gdn_norm_w', 'new_v_w_branch_sb', 'new_v_w_branch_gdn', 'new_v_w_mix_out', 'new_v_ln2_g', 'new_v_ln2_b', 'new_v_ffn2_w_in', 'new_v_ffn2_w_out', 'new_v_ln3_g', 'new_v_ln3_b', 'new_v_w_ple_gate', 'new_v_b_ple_gate', 'new_v_w_ple', 'new_v_ln4_g', 'new_v_ln4_b']
TWIN_LEAF_KINDS = {'loss': 'loss', 'grad_x': 'grad_x', 'grad_ffn1_w_in': 'grad_w', 'grad_ffn1_w_out': 'grad_w', 'grad_ln1_g': 'grad_w', 'grad_ln1_b': 'grad_w', 'grad_w_mix_in': 'grad_w', 'grad_b_gate': 'grad_w', 'grad_conv_w': 'grad_w', 'grad_a_log': 'grad_w', 'grad_dt_bias': 'grad_w', 'grad_gdn_norm_w': 'grad_w', 'grad_w_branch_sb': 'grad_w', 'grad_w_branch_gdn': 'grad_w', 'grad_w_mix_out': 'grad_w', 'grad_ln2_g': 'grad_w', 'grad_ln2_b': 'grad_w', 'grad_ffn2_w_in': 'grad_w', 'grad_ffn2_w_out': 'grad_w', 'grad_ln3_g': 'grad_w', 'grad_ln3_b': 'grad_w', 'grad_w_ple_gate': 'grad_w', 'grad_b_ple_gate': 'grad_w', 'grad_w_ple': 'grad_w', 'grad_ln4_g': 'grad_w', 'grad_ln4_b': 'grad_w', 'delta_ffn1_w_in': 'delta_w', 'delta_ffn1_w_out': 'delta_w', 'delta_ln1_g': 'delta_w', 'delta_ln1_b': 'delta_w', 'delta_w_mix_in': 'delta_w', 'delta_b_gate': 'delta_w', 'delta_conv_w': 'delta_w', 'delta_a_log': 'delta_w', 'delta_dt_bias': 'delta_w', 'delta_gdn_norm_w': 'delta_w', 'delta_w_branch_sb': 'delta_w', 'delta_w_branch_gdn': 'delta_w', 'delta_w_mix_out': 'delta_w', 'delta_ln2_g': 'delta_w', 'delta_ln2_b': 'delta_w', 'delta_ffn2_w_in': 'delta_w', 'delta_ffn2_w_out': 'delta_w', 'delta_ln3_g': 'delta_w', 'delta_ln3_b': 'delta_w', 'delta_w_ple_gate': 'delta_w', 'delta_b_ple_gate': 'delta_w', 'delta_w_ple': 'delta_w', 'delta_ln4_g': 'delta_w', 'delta_ln4_b': 'delta_w', 'new_m_ffn1_w_in': 'new_m', 'new_m_ffn1_w_out': 'new_m', 'new_m_ln1_g': 'new_m', 'new_m_ln1_b': 'new_m', 'new_m_w_mix_in': 'new_m', 'new_m_b_gate': 'new_m', 'new_m_conv_w': 'new_m', 'new_m_a_log': 'new_m', 'new_m_dt_bias': 'new_m', 'new_m_gdn_norm_w': 'new_m', 'new_m_w_branch_sb': 'new_m', 'new_m_w_branch_gdn': 'new_m', 'new_m_w_mix_out': 'new_m', 'new_m_ln2_g': 'new_m', 'new_m_ln2_b': 'new_m', 'new_m_ffn2_w_in': 'new_m', 'new_m_ffn2_w_out': 'new_m', 'new_m_ln3_g': 'new_m', 'new_m_ln3_b': 'new_m', 'new_m_w_ple_gate': 'new_m', 'new_m_b_ple_gate': 'new_m', 'new_m_w_ple': 'new_m', 'new_m_ln4_g': 'new_m', 'new_m_ln4_b': 'new_m', 'new_v_ffn1_w_in': 'new_v', 'new_v_ffn1_w_out': 'new_v', 'new_v_ln1_g': 'new_v', 'new_v_ln1_b': 'new_v', 'new_v_w_mix_in': 'new_v', 'new_v_b_gate': 'new_v', 'new_v_conv_w': 'new_v', 'new_v_a_log': 'new_v', 'new_v_dt_bias': 'new_v', 'new_v_gdn_norm_w': 'new_v', 'new_v_w_branch_sb': 'new_v', 'new_v_w_branch_gdn': 'new_v', 'new_v_w_mix_out': 'new_v', 'new_v_ln2_g': 'new_v', 'new_v_ln2_b': 'new_v', 'new_v_ffn2_w_in': 'new_v', 'new_v_ffn2_w_out': 'new_v', 'new_v_ln3_g': 'new_v', 'new_v_ln3_b': 'new_v', 'new_v_w_ple_gate': 'new_v', 'new_v_b_ple_gate': 'new_v', 'new_v_w_ple': 'new_v', 'new_v_ln4_g': 'new_v', 'new_v_ln4_b': 'new_v'}


def _forward(args):
    return _fwd_reference(*[args[k] for k in FWD_PARAMS])


def _output_shape():
    def fwd():
        inp = _fwd_setup_inputs(0)
        return _fwd_reference(*[inp[k] for k in FWD_PARAMS])
    out = _jax.eval_shape(fwd)
    return out.shape, out.dtype

N_MICROBATCH = 1
ADAM_LR = 0.001
ADAM_B1 = 0.9
ADAM_B2 = 0.999
ADAM_EPS = 1e-08
ADAM_WD = 0.01
ADAM_STEP = 10
PER_EXAMPLE_BATCH_AXIS = {'x': 0, 'p': 1, 'loss_target': 0}
SHARED_INPUTS = []
_WEIGHT_DTYPES = {'ffn1_w_in': _jnp.float32, 'ffn1_w_out': _jnp.float32, 'ln1_g': _jnp.float32, 'ln1_b': _jnp.float32, 'w_mix_in': _jnp.float32, 'b_gate': _jnp.float32, 'conv_w': _jnp.float32, 'a_log': _jnp.float32, 'dt_bias': _jnp.float32, 'gdn_norm_w': _jnp.float32, 'w_branch_sb': _jnp.float32, 'w_branch_gdn': _jnp.float32, 'w_mix_out': _jnp.float32, 'ln2_g': _jnp.float32, 'ln2_b': _jnp.float32, 'ffn2_w_in': _jnp.float32, 'ffn2_w_out': _jnp.float32, 'ln3_g': _jnp.float32, 'ln3_b': _jnp.float32, 'w_ple_gate': _jnp.float32, 'b_ple_gate': _jnp.float32, 'w_ple': _jnp.float32, 'ln4_g': _jnp.float32, 'ln4_b': _jnp.float32}
MOMENT_SCALE = {'ffn1_w_in': 2.316012e-02, 'ffn1_w_out': 6.354659e-02, 'ln1_g': 1.858962e+00, 'ln1_b': 1.034207e+00, 'w_mix_in': 3.894500e-02, 'b_gate': 1.692086e-02, 'conv_w': 5.249862e-02, 'a_log': 1.346013e-01, 'dt_bias': 1.334955e-01, 'gdn_norm_w': 1.818568e-01, 'w_branch_sb': 4.169611e-02, 'w_branch_gdn': 4.240962e-02, 'w_mix_out': 1.007387e-01, 'ln2_g': 2.017076e+00, 'ln2_b': 1.001725e+00, 'ffn2_w_in': 2.200150e-02, 'ffn2_w_out': 6.049586e-02, 'ln3_g': 2.073310e+00, 'ln3_b': 1.009568e+00, 'w_ple_gate': 2.558623e-02, 'b_ple_gate': 4.528967e-02, 'w_ple': 1.102321e-01, 'ln4_g': 6.415183e+01, 'ln4_b': 2.662437e+00}


def _to_microbatches(a, axis):
    t = _jnp.moveaxis(a, axis, 0)
    t = t.reshape((N_MICROBATCH, t.shape[0] // N_MICROBATCH) + t.shape[1:])
    return _jnp.moveaxis(t, 1, axis + 1)


def setup_inputs(seed: int = 0) -> dict:
    inp = _fwd_setup_inputs(seed)
    key = _jax.random.fold_in(_jax.random.key(seed), 7919)
    shape, _ = _output_shape()
    out = dict(inp)
    out["loss_target"] = _jax.random.normal(_jax.random.fold_in(key, 0), shape, _jnp.float32)
    for i, name in enumerate(TWIN_WEIGHTS):
        w = inp[name].astype(_jnp.float32)
        if MOMENT_SCALE is None:
            s = _jnp.sqrt(_jnp.mean(_jnp.square(w)) + 1e-30)
        else:
            s = MOMENT_SCALE[name]
        km, kv = _jax.random.split(_jax.random.fold_in(key, i + 1))
        out[name] = w
        out["m_" + name] = s * _jax.random.normal(km, w.shape, _jnp.float32)
        out["v_" + name] = (s * s) * _jax.random.uniform(kv, w.shape, _jnp.float32, 0.5, 1.5)
    if N_MICROBATCH > 1:
        for name, axis in PER_EXAMPLE_BATCH_AXIS.items():
            out[name] = _to_microbatches(out[name], axis)
    return {'x': out['x'], 'p': out['p'], 'ffn1_w_in': out['ffn1_w_in'], 'ffn1_w_out': out['ffn1_w_out'], 'ln1_g': out['ln1_g'], 'ln1_b': out['ln1_b'], 'w_mix_in': out['w_mix_in'], 'b_gate': out['b_gate'], 'conv_w': out['conv_w'], 'a_log': out['a_log'], 'dt_bias': out['dt_bias'], 'gdn_norm_w': out['gdn_norm_w'], 'w_branch_sb': out['w_branch_sb'], 'w_branch_gdn': out['w_branch_gdn'], 'w_mix_out': out['w_mix_out'], 'ln2_g': out['ln2_g'], 'ln2_b': out['ln2_b'], 'ffn2_w_in': out['ffn2_w_in'], 'ffn2_w_out': out['ffn2_w_out'], 'ln3_g': out['ln3_g'], 'ln3_b': out['ln3_b'], 'w_ple_gate': out['w_ple_gate'], 'b_ple_gate': out['b_ple_gate'], 'w_ple': out['w_ple'], 'ln4_g': out['ln4_g'], 'ln4_b': out['ln4_b'], 'loss_target': out['loss_target'], 'm_ffn1_w_in': out['m_ffn1_w_in'], 'm_ffn1_w_out': out['m_ffn1_w_out'], 'm_ln1_g': out['m_ln1_g'], 'm_ln1_b': out['m_ln1_b'], 'm_w_mix_in': out['m_w_mix_in'], 'm_b_gate': out['m_b_gate'], 'm_conv_w': out['m_conv_w'], 'm_a_log': out['m_a_log'], 'm_dt_bias': out['m_dt_bias'], 'm_gdn_norm_w': out['m_gdn_norm_w'], 'm_w_branch_sb': out['m_w_branch_sb'], 'm_w_branch_gdn': out['m_w_branch_gdn'], 'm_w_mix_out': out['m_w_mix_out'], 'm_ln2_g': out['m_ln2_g'], 'm_ln2_b': out['m_ln2_b'], 'm_ffn2_w_in': out['m_ffn2_w_in'], 'm_ffn2_w_out': out['m_ffn2_w_out'], 'm_ln3_g': out['m_ln3_g'], 'm_ln3_b': out['m_ln3_b'], 'm_w_ple_gate': out['m_w_ple_gate'], 'm_b_ple_gate': out['m_b_ple_gate'], 'm_w_ple': out['m_w_ple'], 'm_ln4_g': out['m_ln4_g'], 'm_ln4_b': out['m_ln4_b'], 'v_ffn1_w_in': out['v_ffn1_w_in'], 'v_ffn1_w_out': out['v_ffn1_w_out'], 'v_ln1_g': out['v_ln1_g'], 'v_ln1_b': out['v_ln1_b'], 'v_w_mix_in': out['v_w_mix_in'], 'v_b_gate': out['v_b_gate'], 'v_conv_w': out['v_conv_w'], 'v_a_log': out['v_a_log'], 'v_dt_bias': out['v_dt_bias'], 'v_gdn_norm_w': out['v_gdn_norm_w'], 'v_w_branch_sb': out['v_w_branch_sb'], 'v_w_branch_gdn': out['v_w_branch_gdn'], 'v_w_mix_out': out['v_w_mix_out'], 'v_ln2_g': out['v_ln2_g'], 'v_ln2_b': out['v_ln2_b'], 'v_ffn2_w_in': out['v_ffn2_w_in'], 'v_ffn2_w_out': out['v_ffn2_w_out'], 'v_ln3_g': out['v_ln3_g'], 'v_ln3_b': out['v_ln3_b'], 'v_w_ple_gate': out['v_w_ple_gate'], 'v_b_ple_gate': out['v_b_ple_gate'], 'v_w_ple': out['v_w_ple'], 'v_ln4_g': out['v_ln4_g'], 'v_ln4_b': out['v_ln4_b']}


def _loss(weights, diff, rest, loss_target):
    with _jax.named_scope("forward"):
        args = {**rest, TWIN_DIFF_INPUT: diff, **{k: w.astype(_WEIGHT_DTYPES[k]) for k, w in weights.items()}}
        y = _forward(args)
    with _jax.named_scope("loss_head"):
        err = _jnp.square(y.astype(_jnp.float32) - loss_target)
        return 0.5 * _jnp.sum(_jnp.mean(err, axis=-1)) if err.ndim else 0.5 * err


def _adamw(w, g, m, v):
    m = ADAM_B1 * m + (1.0 - ADAM_B1) * g
    v = ADAM_B2 * v + (1.0 - ADAM_B2) * _jnp.square(g)
    m_hat = m / (1.0 - ADAM_B1 ** ADAM_STEP)
    v_hat = v / (1.0 - ADAM_B2 ** ADAM_STEP)
    delta = -ADAM_LR * (m_hat / (_jnp.sqrt(v_hat) + ADAM_EPS) + ADAM_WD * w)
    return delta, m, v


def reference(x, p, ffn1_w_in, ffn1_w_out, ln1_g, ln1_b, w_mix_in, b_gate, conv_w, a_log, dt_bias, gdn_norm_w, w_branch_sb, w_branch_gdn, w_mix_out, ln2_g, ln2_b, ffn2_w_in, ffn2_w_out, ln3_g, ln3_b, w_ple_gate, b_ple_gate, w_ple, ln4_g, ln4_b, loss_target, m_ffn1_w_in, m_ffn1_w_out, m_ln1_g, m_ln1_b, m_w_mix_in, m_b_gate, m_conv_w, m_a_log, m_dt_bias, m_gdn_norm_w, m_w_branch_sb, m_w_branch_gdn, m_w_mix_out, m_ln2_g, m_ln2_b, m_ffn2_w_in, m_ffn2_w_out, m_ln3_g, m_ln3_b, m_w_ple_gate, m_b_ple_gate, m_w_ple, m_ln4_g, m_ln4_b, v_ffn1_w_in, v_ffn1_w_out, v_ln1_g, v_ln1_b, v_w_mix_in, v_b_gate, v_conv_w, v_a_log, v_dt_bias, v_gdn_norm_w, v_w_branch_sb, v_w_branch_gdn, v_w_mix_out, v_ln2_g, v_ln2_b, v_ffn2_w_in, v_ffn2_w_out, v_ln3_g, v_ln3_b, v_w_ple_gate, v_b_ple_gate, v_w_ple, v_ln4_g, v_ln4_b):
    given = dict(x=x, p=p, ffn1_w_in=ffn1_w_in, ffn1_w_out=ffn1_w_out, ln1_g=ln1_g, ln1_b=ln1_b, w_mix_in=w_mix_in, b_gate=b_gate, conv_w=conv_w, a_log=a_log, dt_bias=dt_bias, gdn_norm_w=gdn_norm_w, w_branch_sb=w_branch_sb, w_branch_gdn=w_branch_gdn, w_mix_out=w_mix_out, ln2_g=ln2_g, ln2_b=ln2_b, ffn2_w_in=ffn2_w_in, ffn2_w_out=ffn2_w_out, ln3_g=ln3_g, ln3_b=ln3_b, w_ple_gate=w_ple_gate, b_ple_gate=b_ple_gate, w_ple=w_ple, ln4_g=ln4_g, ln4_b=ln4_b, loss_target=loss_target, m_ffn1_w_in=m_ffn1_w_in, m_ffn1_w_out=m_ffn1_w_out, m_ln1_g=m_ln1_g, m_ln1_b=m_ln1_b, m_w_mix_in=m_w_mix_in, m_b_gate=m_b_gate, m_conv_w=m_conv_w, m_a_log=m_a_log, m_dt_bias=m_dt_bias, m_gdn_norm_w=m_gdn_norm_w, m_w_branch_sb=m_w_branch_sb, m_w_branch_gdn=m_w_branch_gdn, m_w_mix_out=m_w_mix_out, m_ln2_g=m_ln2_g, m_ln2_b=m_ln2_b, m_ffn2_w_in=m_ffn2_w_in, m_ffn2_w_out=m_ffn2_w_out, m_ln3_g=m_ln3_g, m_ln3_b=m_ln3_b, m_w_ple_gate=m_w_ple_gate, m_b_ple_gate=m_b_ple_gate, m_w_ple=m_w_ple, m_ln4_g=m_ln4_g, m_ln4_b=m_ln4_b, v_ffn1_w_in=v_ffn1_w_in, v_ffn1_w_out=v_ffn1_w_out, v_ln1_g=v_ln1_g, v_ln1_b=v_ln1_b, v_w_mix_in=v_w_mix_in, v_b_gate=v_b_gate, v_conv_w=v_conv_w, v_a_log=v_a_log, v_dt_bias=v_dt_bias, v_gdn_norm_w=v_gdn_norm_w, v_w_branch_sb=v_w_branch_sb, v_w_branch_gdn=v_w_branch_gdn, v_w_mix_out=v_w_mix_out, v_ln2_g=v_ln2_g, v_ln2_b=v_ln2_b, v_ffn2_w_in=v_ffn2_w_in, v_ffn2_w_out=v_ffn2_w_out, v_ln3_g=v_ln3_g, v_ln3_b=v_ln3_b, v_w_ple_gate=v_w_ple_gate, v_b_ple_gate=v_b_ple_gate, v_w_ple=v_w_ple, v_ln4_g=v_ln4_g, v_ln4_b=v_ln4_b)
    weights = {n: given[n] for n in TWIN_WEIGHTS}
    shared = {n: given[n] for n in SHARED_INPUTS}
    per_example = {n: given[n] for n in ['x', 'p']}
    grad_fn = _jax.value_and_grad(_loss, argnums=(0, 1))

    def one_microbatch(ex, loss_target):
        ex = dict(ex)
        diff = ex.pop(TWIN_DIFF_INPUT)
        return grad_fn(weights, diff, {**shared, **ex}, loss_target)

    if N_MICROBATCH == 1:
        loss, (grad_w, grad_x) = one_microbatch(per_example, given["loss_target"])
    else:
        def body(carry, xs):
            loss_sum, grad_sum = carry
            l_k, (gw_k, gx_k) = one_microbatch(xs[0], xs[1])
            with _jax.named_scope("update"):
                return (loss_sum + l_k, _jax.tree.map(_jnp.add, grad_sum, gw_k)), gx_k

        init = (_jnp.zeros((), _jnp.float32), _jax.tree.map(_jnp.zeros_like, weights))
        (loss, grad_w), grad_x = _jax.lax.scan(body, init, (per_example, given["loss_target"]))
    with _jax.named_scope("update"):
        delta_w, new_m, new_v = {}, {}, {}
        for n in TWIN_WEIGHTS:
            delta_w[n], new_m[n], new_v[n] = _adamw(weights[n], grad_w[n], given["m_" + n], given["v_" + n])
    return (loss, grad_x, *[grad_w[n] for n in TWIN_WEIGHTS], *[delta_w[n] for n in TWIN_WEIGHTS],
            *[new_m[n] for n in TWIN_WEIGHTS], *[new_v[n] for n in TWIN_WEIGHTS])
```

```python
import functools

import jax
import jax.numpy as jnp
from jax import lax
from jax.experimental import pallas as pl
from jax.experimental.pallas import tpu as pltpu

F32 = jnp.float32
BF = jnp.bfloat16
I32 = jnp.int32
HI = lax.Precision.HIGHEST
MESH = pl.DeviceIdType.MESH

D = 1024
DFF = 2816
NSH = 4
FSH = 2 * DFF // NSH
NIN = 5648
MSH = NIN // NSH
NP = 6144
CUT = 3600
GATE0 = 4096
HEADS = 8
HD = 64
CH = 64
KB = 128
ALPHA = 2.0 ** 0.25
LN_EPS = 1e-5
RMS_EPS = 1e-6
B1, B2, LR, EPS, WD, STEP = 0.9, 0.999, 0.001, 1e-08, 0.01, 10


def _sigmoid(x):
    return 0.5 * jnp.tanh(0.5 * x) + 0.5


def _softplus(x):
    return jnp.maximum(x, 0.0) + jnp.log1p(jnp.exp(-jnp.abs(x)))


def _layer_norm(r, g, b):
    mu = jnp.mean(r, axis=-1, keepdims=True)
    xc = r - mu
    var = jnp.mean(xc * xc, axis=-1, keepdims=True)
    return xc * lax.rsqrt(var + LN_EPS) * g + b


def _layer_norm_bwd(r, g, dh):
    mu = jnp.mean(r, axis=-1, keepdims=True)
    xc = r - mu
    var = jnp.mean(xc * xc, axis=-1, keepdims=True)
    xhat = xc * lax.rsqrt(var + LN_EPS)
    dxh = dh * g
    dr = lax.rsqrt(var + LN_EPS) * (dxh - jnp.mean(dxh, axis=-1, keepdims=True) - xhat * jnp.mean(dxh * xhat, axis=-1, keepdims=True))
    return dr, jnp.sum(dh * xhat, axis=0, keepdims=True), jnp.sum(dh, axis=0, keepdims=True)


def _pick(n, cap):
    if n <= cap:
        return n
    for t in range(cap - cap % 8, 7, -8):
        if n % t == 0:
            return t
    raise ValueError((n, cap))


def _mm(name, a, b, M, N, K, *, tm, tn, tk, ta=False, tb=False, a_spec=None, b_spec=None, order="ij",
        extras=(), epilogue=None, outs, n_acc=0):
    ni, nj, nk = M // tm, N // tn, K // tk
    assert M % tm == 0 and N % tn == 0 and K % tk == 0, (name, M, N, K, tm, tn, tk)
    assert n_acc == 0 or nj == 1

    def wrap(fn):
        if order == "ij":
            return lambda g0, g1, g2: fn(g0, g1, g2)
        return lambda g0, g1, g2: fn(g1, g0, g2)

    if a_spec is None:
        a_spec = ((tk, tm), lambda i, j, k: (k, i)) if ta else ((tm, tk), lambda i, j, k: (i, k))
    if b_spec is None:
        b_spec = ((tn, tk), lambda i, j, k: (j, k)) if tb else ((tk, tn), lambda i, j, k: (k, j))
    dims = (((0 if ta else 1,), (1 if tb else 0,)), ((), ()))
    ne, no = len(extras), len(outs)
    grid = (ni, nj, nk) if order == "ij" else (nj, ni, nk)

    def body(*refs):
        a_ref, b_ref = refs[0], refs[1]
        ex = refs[2:2 + ne]
        o = refs[2 + ne:2 + ne + no]
        g0, g1, k = pl.program_id(0), pl.program_id(1), pl.program_id(2)
        first = jnp.logical_and(g0 == 0, g1 == 0)
        p = lax.dot_general(a_ref[...].astype(BF), b_ref[...].astype(BF), dims, preferred_element_type=F32)

        def finish(acc):
            vals = (acc,) if epilogue is None else epilogue(acc, *[e[...] for e in ex])
            for idx, (ref, val) in enumerate(zip(o, vals)):
                if idx < no - n_acc:
                    ref[...] = val.astype(ref.dtype)
                else:
                    @pl.when(first)
                    def _(ref=ref, val=val):
                        ref[...] = val

                    @pl.when(jnp.logical_not(first))
                    def _(ref=ref, val=val):
                        ref[...] += val

        if nk == 1:
            finish(p)
        else:
            acc_ref = refs[-1]

            @pl.when(k == 0)
            def _():
                acc_ref[...] = p

            @pl.when(k > 0)
            def _():
                acc_ref[...] += p

            @pl.when(k == nk - 1)
            def _():
                finish(acc_ref[...])

    in_specs = [pl.BlockSpec(a_spec[0], wrap(a_spec[1])), pl.BlockSpec(b_spec[0], wrap(b_spec[1]))]
    in_specs += [pl.BlockSpec(blk, wrap(fn)) for _, blk, fn in extras]
    res = pl.pallas_call(
        body, name=name, grid=grid, in_specs=in_specs,
        out_specs=[pl.BlockSpec(blk, wrap(fn)) for _, _, blk, fn in outs],
        out_shape=[jax.ShapeDtypeStruct(shape, dt) for shape, dt, _, _ in outs],
        scratch_shapes=[pltpu.VMEM((tm, tn), F32)] if nk > 1 else [],
    )(a, b, *[e[0] for e in extras])
    return res


def _row(i, j, k):
    return (i, 0)


def _tile(i, j, k):
    return (i, j)


def _const(i, j, k):
    return (0, 0)


def _rows(name, fn, n_steps, ins, outs, n_acc=0):
    ni, no = len(ins), len(outs)

    def body(*refs):
        i = pl.program_id(0)
        vals = fn(*[r[...] for r in refs[:ni]])
        for idx, (ref, val) in enumerate(zip(refs[ni:ni + no], vals)):
            if idx < no - n_acc:
                ref[...] = val.astype(ref.dtype)
            else:
                @pl.when(i == 0)
                def _(ref=ref, val=val):
                    ref[...] = val

                @pl.when(i > 0)
                def _(ref=ref, val=val):
                    ref[...] += val

    return pl.pallas_call(
        body, name=name, grid=(n_steps,),
        in_specs=[pl.BlockSpec(blk, fn_) for _, blk, fn_ in ins],
        out_specs=[pl.BlockSpec(blk, fn_) for _, _, blk, fn_ in outs],
        out_shape=[jax.ShapeDtypeStruct(shape, dt) for shape, dt, _, _ in outs],
    )(*[a for a, _, _ in ins])


def _ffn_fwd(tag, x, w_in, w_out, g, b):
    S = x.shape[0]
    tm = min(512, S)
    gate, = _mm(f"{tag}_gate", x, w_in, S, DFF, D, tm=tm, tn=FSH, tk=D, order="ji",
                b_spec=((None, D, FSH), lambda i, j, k: (j, 0, 0)),
                outs=[((S, DFF), F32, (tm, FSH), _tile)])

    def up_epi(acc, gt):
        return acc, gt * _sigmoid(gt) * acc

    up, s = _mm(f"{tag}_up", x, w_in, S, DFF, D, tm=tm, tn=FSH, tk=D, order="ji",
                b_spec=((None, D, FSH), lambda i, j, k: (j + 2, 0, 0)),
                extras=[(gate, (tm, FSH), _tile)], epilogue=up_epi,
                outs=[((S, DFF), F32, (tm, FSH), _tile), ((S, DFF), BF, (tm, FSH), _tile)])

    def out_epi(acc, xin, gg, bb):
        r = ALPHA * xin + 0.5 * acc
        return _layer_norm(r, gg, bb), r

    h, r = _mm(f"{tag}_out", s, w_out, S, D, DFF, tm=tm, tn=D, tk=DFF,
               extras=[(x, (tm, D), _row), (g, (1, D), _const), (b, (1, D), _const)], epilogue=out_epi,
               outs=[((S, D), F32, (tm, D), _row), ((S, D), F32, (tm, D), _row)])
    return h, (x, gate, up, s, r)


def _ln_bwd(tag, r, g, dh):
    S = r.shape[0]
    tm = min(512, S)
    return _rows(f"{tag}_lnbwd", _layer_norm_bwd, S // tm,
                 [(r, (tm, D), lambda i: (i, 0)), (g, (1, D), lambda i: (0, 0)), (dh, (tm, D), lambda i: (i, 0))],
                 [((S, D), F32, (tm, D), lambda i: (i, 0)), ((1, D), F32, (1, D), lambda i: (0, 0)),
                  ((1, D), F32, (1, D), lambda i: (0, 0))], n_acc=2)


def _ffn_bwd(tag, saved, w_in, w_out, g, dh):
    x, gate, up, s, r = saved
    S = x.shape[0]
    tm = min(512, S)
    dr, dg, db = _ln_bwd(tag, r, g, dh)

    def act_epi(acc, gt, u):
        ds = 0.5 * acc
        sg = _sigmoid(gt)
        return (jnp.stack([ds * u * (sg * (1.0 + gt * (1.0 - sg))), ds * (gt * sg)]),)

    da, = _mm(f"{tag}_dact", dr, w_out, S, DFF, D, tm=tm, tn=FSH, tk=D, tb=True, order="ji",
              extras=[(gate, (tm, FSH), _tile), (up, (tm, FSH), _tile)], epilogue=act_epi,
              outs=[((2, S, DFF), BF, (2, tm, FSH), lambda i, j, k: (0, i, j))])
    d_w_out, = _mm(f"{tag}_dwout", s, dr, DFF, D, S, tm=FSH, tn=D, tk=tm, ta=True,
                   epilogue=lambda acc: (0.5 * acc,), outs=[((DFF, D), F32, (FSH, D), _tile)])
    d_in, = _mm(f"{tag}_dx", da, w_in, S, D, 2 * DFF, tm=tm, tn=D, tk=FSH, tb=True,
                a_spec=((None, tm, FSH), lambda i, j, k: (k // 2, i, k % 2)),
                b_spec=((None, D, FSH), lambda i, j, k: (k, 0, 0)),
                extras=[(dr, (tm, D), _row)], epilogue=lambda acc, d: (acc + ALPHA * d,),
                outs=[((S, D), F32, (tm, D), _row)])
    d_w_in, = _mm(f"{tag}_dwin", x, da, D, 2 * DFF, S, tm=D, tn=FSH, tk=tm, ta=True, order="ji",
                  b_spec=((None, tm, FSH), lambda i, j, k: (j // 2, k, j % 2)),
                  outs=[((NSH, D, FSH), F32, (None, D, FSH), lambda i, j, k: (j, 0, 0))])
    return d_in, d_w_in, d_w_out, dg, db


def _cum(vals, tri):
    hi = vals.astype(BF)
    lo = (vals - hi.astype(F32)).astype(BF)
    return jnp.dot(hi, tri, preferred_element_type=F32) + jnp.dot(lo, tri, preferred_element_type=F32)


def _nt(a, b):
    return lax.dot_general(a, b, (((1,), (1,)), ((), ())), preferred_element_type=F32)


def _tn(a, b):
    return lax.dot_general(a, b, (((0,), (0,)), ((), ())), preferred_element_type=F32)


def _attn_fwd(m):
    S = m.shape[0]
    tq = min(256, S)
    assert S // KB <= 128

    def body(q_ref, k_ref, v_ref, o_ref, r_ref, kb, vb):
        i = pl.program_id(1)

        @pl.when(i == 0)
        def _():
            kb[...] = k_ref[...].astype(BF)
            vb[...] = v_ref[...].astype(BF)

        qs = (q_ref[...] * 0.125).astype(BF)
        row = lax.broadcasted_iota(I32, (tq, KB), 0) + i * tq
        col = lax.broadcasted_iota(I32, (tq, KB), 1)
        tri = (lax.broadcasted_iota(I32, (KB, KB), 0) >= lax.broadcasted_iota(I32, (KB, KB), 1)).astype(BF)
        nb = (i + 1) * (tq // KB)
        r_ref[...] = jnp.zeros_like(r_ref)

        def step(t, carry):
            j = nb - 1 - t
            off = pl.multiple_of(j * KB, KB)
            mask = col + j * KB < row
            kblk = kb[pl.ds(off, KB), :]
            vblk = vb[pl.ds(off, KB), :]
            new = []
            for hh in range(2):
                run, acc = carry[2 * hh], carry[2 * hh + 1]
                sl = slice(hh * HD, (hh + 1) * HD)
                z = _nt(qs[:, sl], kblk[:, sl])
                cs = _cum(jnp.where(mask, _softplus(z), 0.0), tri)
                a = jnp.where(mask, jnp.exp(z - (run + cs)), 0.0)
                r_ref[hh] = jnp.where(col == j, run, r_ref[hh])
                new += [run + cs[:, :1], acc + jnp.dot(a.astype(BF), vblk[:, sl], preferred_element_type=F32)]
            return tuple(new)

        init = (jnp.zeros((tq, 1), F32), jnp.zeros((tq, HD), F32)) * 2
        res = lax.fori_loop(0, nb, step, init)
        o_ref[...] = jnp.concatenate([res[1], res[3]], axis=1)

    return pl.pallas_call(
        body, name="attn_fwd", grid=(HEADS // 2, S // tq),
        in_specs=[pl.BlockSpec((tq, KB), lambda h, i: (i, h)), pl.BlockSpec((S, KB), lambda h, i: (0, 4 + h)),
                  pl.BlockSpec((S, KB), lambda h, i: (0, 8 + h))],
        out_specs=[pl.BlockSpec((tq, KB), lambda h, i: (i, h)), pl.BlockSpec((2, tq, KB), lambda h, i: (h, i, 0))],
        out_shape=[jax.ShapeDtypeStruct((S, HEADS * HD), F32), jax.ShapeDtypeStruct((HEADS, S, KB), F32)],
        scratch_shapes=[pltpu.VMEM((S, KB), BF), pltpu.VMEM((S, KB), BF)],
    )(m, m, m)


def _attn_bwd(m, runs, dy):
    S = m.shape[0]
    tq = min(256, S)

    def body(q_ref, k_ref, v_ref, r_ref, dy_ref, dq_ref, dk_ref, dv_ref, kb, vb):
        i = pl.program_id(1)

        @pl.when(i == 0)
        def _():
            kb[...] = k_ref[...].astype(BF)
            vb[...] = v_ref[...].astype(BF)
            dk_ref[...] = jnp.zeros_like(dk_ref)
            dv_ref[...] = jnp.zeros_like(dv_ref)

        qs = (q_ref[...] * 0.125).astype(BF)
        dyb = dy_ref[...].astype(BF)
        rs = [r_ref[0], r_ref[1]]
        row = lax.broadcasted_iota(I32, (tq, KB), 0) + i * tq
        col = lax.broadcasted_iota(I32, (tq, KB), 1)
        jj = lax.broadcasted_iota(I32, (KB, KB), 0)
        ss = lax.broadcasted_iota(I32, (KB, KB), 1)
        tri_rev = (jj >= ss).astype(BF)
        tri_fwd = (jj <= ss).astype(BF)
        nb = (i + 1) * (tq // KB)

        def step(j, carry):
            off = pl.multiple_of(j * KB, KB)
            mask = col + j * KB < row
            kblk = kb[pl.ds(off, KB), :]
            vblk = vb[pl.ds(off, KB), :]
            new, dks, dvs = [], [], []
            for hh in range(2):
                pre, dq = carry[2 * hh], carry[2 * hh + 1]
                sl = slice(hh * HD, (hh + 1) * HD)
                run = jnp.sum(jnp.where(col == j, rs[hh], 0.0), axis=1, keepdims=True)
                z = _nt(qs[:, sl], kblk[:, sl])
                e = jnp.exp(-jnp.abs(z))
                sp = jnp.maximum(z, 0.0) + jnp.log1p(e)
                sig = jnp.where(z >= 0.0, 1.0, e) / (1.0 + e)
                cs = _cum(jnp.where(mask, sp, 0.0), tri_rev)
                a = jnp.where(mask, jnp.exp(z - (run + cs)), 0.0)
                gmat = a * _nt(dyb[:, sl], vblk[:, sl])
                pg = _cum(gmat, tri_fwd)
                dz = jnp.where(mask, gmat - sig * (pre + pg), 0.0).astype(BF)
                dks.append(_tn(dz, qs[:, sl]))
                dvs.append(_tn(a.astype(BF), dyb[:, sl]))
                new += [pre + pg[:, KB - 1:], dq + jnp.dot(dz, kblk[:, sl], preferred_element_type=F32)]
            dk_ref[pl.ds(off, KB), :] += jnp.concatenate(dks, axis=1)
            dv_ref[pl.ds(off, KB), :] += jnp.concatenate(dvs, axis=1)
            return tuple(new)

        init = (jnp.zeros((tq, 1), F32), jnp.zeros((tq, HD), F32)) * 2
        res = lax.fori_loop(0, nb, step, init)
        dq_ref[...] = (jnp.concatenate([res[1], res[3]], axis=1) * 0.125).astype(dq_ref.dtype)

    n = HEADS * HD
    return pl.pallas_call(
        body, name="attn_bwd", grid=(HEADS // 2, S // tq),
        in_specs=[pl.BlockSpec((tq, KB), lambda h, i: (i, h)), pl.BlockSpec((S, KB), lambda h, i: (0, 4 + h)),
                  pl.BlockSpec((S, KB), lambda h, i: (0, 8 + h)), pl.BlockSpec((2, tq, KB), lambda h, i: (h, i, 0)),
                  pl.BlockSpec((tq, KB), lambda h, i: (i, h))],
        out_specs=[pl.BlockSpec((tq, KB), lambda h, i: (i, h)), pl.BlockSpec((S, KB), lambda h, i: (0, h)),
                   pl.BlockSpec((S, KB), lambda h, i: (0, h))],
        out_shape=[jax.ShapeDtypeStruct((S, n), BF), jax.ShapeDtypeStruct((S, n), F32), jax.ShapeDtypeStruct((S, n), F32)],
        scratch_shapes=[pltpu.VMEM((S, KB), BF), pltpu.VMEM((S, KB), BF)],
    )(m, m, m, runs, dy)


CW = 3 * HEADS * HD


def _shift_down(cur, prev8, s):
    if s == 0:
        return cur
    r = pltpu.roll(cur, s, 0)
    first = jnp.where(lax.broadcasted_iota(I32, (8, cur.shape[1]), 0) < s, pltpu.roll(prev8, s, 0), r[:8])
    return jnp.concatenate([first, r[8:]], axis=0)


def _shift_up(cur, next8, s):
    if s == 0:
        return cur
    n = cur.shape[0]
    r = pltpu.roll(cur, n - s, 0)
    last = jnp.where(lax.broadcasted_iota(I32, (8, cur.shape[1]), 0) >= 8 - s, pltpu.roll(next8, 8 - s, 0), r[n - 8:])
    return jnp.concatenate([r[:n - 8], last], axis=0)


def _conv_fwd(m, conv_w):
    S = m.shape[0]
    tm = min(512, S)
    hb = tm // 8

    def body(x_ref, p_ref, w_ref, o_ref):
        i = pl.program_id(0)
        cur = x_ref[...]
        prev = jnp.where(i > 0, p_ref[...], 0.0)
        w = w_ref[...]
        acc = cur * w[3:4]
        for jk in range(3):
            acc = acc + _shift_down(cur, prev, 3 - jk) * w[jk:jk + 1]
        o_ref[...] = acc

    return pl.pallas_call(
        body, name="conv_fwd", grid=(S // tm,),
        in_specs=[pl.BlockSpec((tm, CW), lambda i: (i, 1)), pl.BlockSpec((8, CW), lambda i: (jnp.maximum(i * hb - 1, 0), 1)),
                  pl.BlockSpec((4, CW), lambda i: (0, 0))],
        out_specs=pl.BlockSpec((tm, CW), lambda i: (i, 0)),
        out_shape=jax.ShapeDtypeStruct((S, CW), F32),
    )(m, m, conv_w)


def _conv_bwd(m, dyc, conv_w):
    S = m.shape[0]
    tm = min(512, S)
    hb = tm // 8
    nt = S // tm

    def body(x_ref, p_ref, d_ref, n_ref, w_ref, dx_ref, dw_ref):
        i = pl.program_id(0)
        cur = x_ref[...]
        prev = jnp.where(i > 0, p_ref[...], 0.0)
        d = d_ref[...]
        nxt = jnp.where(i < nt - 1, n_ref[...], 0.0)
        w = w_ref[...]
        acc = d * w[3:4]
        dws = []
        for jk in range(3):
            acc = acc + _shift_up(d, nxt, 3 - jk) * w[jk:jk + 1]
            dws.append(jnp.sum(d * _shift_down(cur, prev, 3 - jk), axis=0, keepdims=True))
        dws.append(jnp.sum(d * cur, axis=0, keepdims=True))
        dx_ref[...] = acc.astype(dx_ref.dtype)
        dw = jnp.concatenate(dws, axis=0)

        @pl.when(i == 0)
        def _():
            dw_ref[...] = dw

        @pl.when(i > 0)
        def _():
            dw_ref[...] += dw

    return pl.pallas_call(
        body, name="conv_bwd", grid=(nt,),
        in_specs=[pl.BlockSpec((tm, CW), lambda i: (i, 1)), pl.BlockSpec((8, CW), lambda i: (jnp.maximum(i * hb - 1, 0), 1)),
                  pl.BlockSpec((tm, CW), lambda i: (i, 0)),
                  pl.BlockSpec((8, CW), lambda i: (jnp.minimum((i + 1) * hb, S // 8 - 1), 0)),
                  pl.BlockSpec((4, CW), lambda i: (0, 0))],
        out_specs=[pl.BlockSpec((tm, CW), lambda i: (i, 0)), pl.BlockSpec((4, CW), lambda i: (0, 0))],
        out_shape=[jax.ShapeDtypeStruct((S, CW), BF), jax.ShapeDtypeStruct((4, CW), F32)],
    )(m, m, dyc, dyc, conv_w)


def _bmm(spec, a, b):
    return jnp.einsum(spec, a, b, precision=HI, preferred_element_type=F32)


def _gdn_chunk(state, yc, gz, gba, alog, dtb, normw):
    def heads(t, off):
        return jnp.stack([t[:, off + h * HD: off + (h + 1) * HD] for h in range(HEADS)])

    def cols(t, off):
        return jnp.stack([jnp.broadcast_to(t[:, off + h: off + h + 1], (CH, CH)) for h in range(HEADS)])

    c = yc * _sigmoid(yc)
    q, k, v, zg = heads(c, 0), heads(c, HEADS * HD), heads(c, 2 * HEADS * HD), heads(gz, 0)
    q = q * lax.rsqrt(jnp.sum(q * q, axis=-1, keepdims=True) + RMS_EPS) * (HD ** -0.5)
    k = k * lax.rsqrt(jnp.sum(k * k, axis=-1, keepdims=True) + RMS_EPS)
    beta = cols(_sigmoid(gba), 0)
    g = cols(-jnp.exp(alog) * _softplus(gba + dtb), HEADS)
    ri = lax.broadcasted_iota(I32, (HEADS, CH, CH), 1)
    ci = lax.broadcasted_iota(I32, (HEADS, CH, CH), 2)
    causal, strict = ri >= ci, ri > ci
    eye = (ri == ci).astype(F32)
    gc = _bmm("hcj,hjl->hcl", causal.astype(F32), g)
    gr = _bmm("hcj,hmj->hcm", jnp.full((HEADS, CH, CH), 1.0 / CH, F32), gc)
    decay = jnp.where(causal, jnp.exp(jnp.where(causal, gc - gr, 0.0)), 0.0)
    lower = jnp.where(strict, beta * _bmm("hcd,hmd->hcm", k, k) * decay, 0.0)
    pw = -lower
    inv = eye + pw
    for _ in range(5):
        pw = _bmm("hij,hjk->hik", pw, pw)
        inv = inv + _bmm("hij,hjk->hik", inv, pw)
    eg = jnp.exp(gc)
    u = _bmm("hij,hjk->hik", inv, v * beta)
    w = _bmm("hij,hjk->hik", inv, k * (beta * eg))
    qk = jnp.where(causal, _bmm("hcd,hmd->hcm", q, k) * decay, 0.0)
    g_last = gc[:, CH - 1:CH, :]
    v_new = u - _bmm("hck,hkv->hcv", w, state)
    o = _bmm("hck,hkv->hcv", q * eg, state) + _bmm("hcm,hmv->hcv", qk, v_new)
    new_state = state * jnp.exp(g_last) + _bmm("hck,hcv->hkv", k * jnp.exp(g_last - gc), v_new)
    o = o * lax.rsqrt(jnp.mean(o * o, axis=-1, keepdims=True) + RMS_EPS) * normw
    o = o * (zg * _sigmoid(zg))
    return jnp.concatenate([o[h] for h in range(HEADS)], axis=1), new_state


def _gdn_fwd(yc, m, alog, dtb, normw):
    S = yc.shape[0]
    nch = S // CH

    def body(y_ref, gz_ref, gba_ref, al_ref, dt_ref, nw_ref, o_ref, st_ref, st):
        @pl.when(pl.program_id(0) == 0)
        def _():
            st[...] = jnp.zeros_like(st)

        cur = st[...]
        st_ref[0] = cur
        o, new = _gdn_chunk(cur, y_ref[...], gz_ref[...], gba_ref[...], al_ref[...], dt_ref[...], nw_ref[...])
        o_ref[...] = o
        st[...] = new

    return pl.pallas_call(
        body, name="gdn_fwd", grid=(nch,),
        in_specs=[pl.BlockSpec((CH, CW), lambda n: (n, 0)), pl.BlockSpec((CH, HEADS * HD), lambda n: (n, 6)),
                  pl.BlockSpec((CH, 128), lambda n: (n, 28)), pl.BlockSpec((1, 128), lambda n: (0, 0)),
                  pl.BlockSpec((1, 128), lambda n: (0, 0)), pl.BlockSpec((1, HD), lambda n: (0, 0))],
        out_specs=[pl.BlockSpec((CH, HEADS * HD), lambda n: (n, 0)), pl.BlockSpec((1, HEADS, HD, HD), lambda n: (n, 0, 0, 0))],
        out_shape=[jax.ShapeDtypeStruct((S, HEADS * HD), F32), jax.ShapeDtypeStruct((nch, HEADS, HD, HD), F32)],
        scratch_shapes=[pltpu.VMEM((HEADS, HD, HD), F32)],
    )(yc, m, m, alog, dtb, normw)


def _gdn_bwd(yc, m, alog, dtb, normw, states, dog):
    S = yc.shape[0]
    nch = S // CH

    def body(y_ref, gz_ref, gba_ref, al_ref, dt_ref, nw_ref, st_ref, do_ref, dy_ref, dgz_ref, dgba_ref, dal_ref, ddt_ref, dnw_ref, dst):
        n = pl.program_id(0)

        @pl.when(n == 0)
        def _():
            dst[...] = jnp.zeros_like(dst)

        _, vjp = jax.vjp(_gdn_chunk, st_ref[0], y_ref[...], gz_ref[...], gba_ref[...], al_ref[...], dt_ref[...], nw_ref[...])
        d_state, d_y, d_gz, d_gba, d_al, d_dt, d_nw = vjp((do_ref[...], dst[...]))
        dst[...] = d_state
        dy_ref[...] = d_y
        dgz_ref[...] = d_gz.astype(dgz_ref.dtype)
        dgba_ref[...] = d_gba.astype(dgba_ref.dtype)
        for ref, val in ((dal_ref, d_al), (ddt_ref, d_dt), (dnw_ref, d_nw)):
            @pl.when(n == 0)
            def _(ref=ref, val=val):
                ref[...] = val

            @pl.when(n > 0)
            def _(ref=ref, val=val):
                ref[...] += val

    rev = lambda n: nch - 1 - n
    return pl.pallas_call(
        body, name="gdn_bwd", grid=(nch,),
        in_specs=[pl.BlockSpec((CH, CW), lambda n: (rev(n), 0)), pl.BlockSpec((CH, HEADS * HD), lambda n: (rev(n), 6)),
                  pl.BlockSpec((CH, 128), lambda n: (rev(n), 28)), pl.BlockSpec((1, 128), lambda n: (0, 0)),
                  pl.BlockSpec((1, 128), lambda n: (0, 0)), pl.BlockSpec((1, HD), lambda n: (0, 0)),
                  pl.BlockSpec((1, HEADS, HD, HD), lambda n: (rev(n), 0, 0, 0)),
                  pl.BlockSpec((CH, HEADS * HD), lambda n: (rev(n), 0))],
        out_specs=[pl.BlockSpec((CH, CW), lambda n: (rev(n), 0)), pl.BlockSpec((CH, HEADS * HD), lambda n: (rev(n), 0)),
                   pl.BlockSpec((CH, 128), lambda n: (rev(n), 0)), pl.BlockSpec((1, 128), lambda n: (0, 0)),
                   pl.BlockSpec((1, 128), lambda n: (0, 0)), pl.BlockSpec((1, HD), lambda n: (0, 0))],
        out_shape=[jax.ShapeDtypeStruct((S, CW), F32), jax.ShapeDtypeStruct((S, HEADS * HD), BF),
                   jax.ShapeDtypeStruct((S, 128), BF), jax.ShapeDtypeStruct((1, 128), F32),
                   jax.ShapeDtypeStruct((1, 128), F32), jax.ShapeDtypeStruct((1, HD), F32)],
        scratch_shapes=[pltpu.VMEM((HEADS, HD, HD), F32)],
    )(yc, m, m, alog, dtb, normw, states, dog)


def _mixer_fwd(h1, wp, b_gate, conv_w, alog, dtb, normw, w_sb, w_gdn, w_mo, g, b):
    S = h1.shape[0]
    tm = min(512, S)
    n = HEADS * HD
    m, = _mm("mix_in", h1, wp, S, NP, D, tm=tm, tn=1536, tk=D, order="ji", outs=[((S, NP), F32, (tm, 1536), _tile)])
    ya, runs = _attn_fwd(m)
    yc = _conv_fwd(m, conv_w)
    og, states = _gdn_fwd(yc, m, alog, dtb, normw)
    ysb, = _mm("mix_sb", ya, w_sb, S, D, n, tm=tm, tn=D, tk=n, outs=[((S, D), F32, (tm, D), _row)])

    def merge_epi(acc, ys, gs, gg, bg):
        return _sigmoid(gs + bg[:, :D]) * ys + _sigmoid(gg + bg[:, D:]) * acc, acc

    u, ygdn = _mm("mix_gdn", og, w_gdn, S, D, n, tm=tm, tn=D, tk=n,
                  extras=[(ysb, (tm, D), _row), (m, (tm, D), lambda i, j, k: (i, GATE0 // D)),
                          (m, (tm, D), lambda i, j, k: (i, GATE0 // D + 1)), (b_gate, (1, 2 * D), _const)],
                  epilogue=merge_epi, outs=[((S, D), BF, (tm, D), _row), ((S, D), F32, (tm, D), _row)])

    def out_epi(acc, xin, gg, bb):
        r = ALPHA * xin + acc
        return _layer_norm(r, gg, bb), r

    h2, r2 = _mm("mix_out", u, w_mo, S, D, D, tm=tm, tn=D, tk=D,
                 extras=[(h1, (tm, D), _row), (g, (1, D), _const), (b, (1, D), _const)], epilogue=out_epi,
                 outs=[((S, D), F32, (tm, D), _row), ((S, D), F32, (tm, D), _row)])
    return h2, (h1, m, ya, runs, yc, og, states, ysb, ygdn, u, r2)


def _mixer_bwd(saved, wp, b_gate, conv_w, alog, dtb, normw, w_sb, w_gdn, w_mo, g, dh):
    h1, m, ya, runs, yc, og, states, ysb, ygdn, u, r2 = saved
    S = h1.shape[0]
    tm = min(512, S)
    n = HEADS * HD
    dr, dg, db = _ln_bwd("mix", r2, g, dh)

    def merge_epi(du, ys, yg, gs, gg, bg):
        s1, s2 = _sigmoid(gs + bg[:, :D]), _sigmoid(gg + bg[:, D:])
        dgate = jnp.concatenate([du * ys * s1 * (1.0 - s1), du * yg * s2 * (1.0 - s2)], axis=1)
        return du * s1, du * s2, dgate, jnp.sum(dgate, axis=0, keepdims=True)

    dysb, dygdn, dgate, d_bg = _mm(
        "mix_dmerge", dr, w_mo, S, D, D, tm=tm, tn=D, tk=D, tb=True,
        extras=[(ysb, (tm, D), _row), (ygdn, (tm, D), _row), (m, (tm, D), lambda i, j, k: (i, GATE0 // D)),
                (m, (tm, D), lambda i, j, k: (i, GATE0 // D + 1)), (b_gate, (1, 2 * D), _const)],
        epilogue=merge_epi, n_acc=1,
        outs=[((S, D), BF, (tm, D), _row), ((S, D), BF, (tm, D), _row), ((S, 2 * D), BF, (tm, 2 * D), _row),
              ((1, 2 * D), F32, (1, 2 * D), _const)])
    d_w_mo, = _mm("mix_dwmo", u, dr, D, D, S, tm=D, tn=D, tk=tm, ta=True, outs=[((D, D), F32, (D, D), _tile)])
    dya, = _mm("mix_dya", dysb, w_sb, S, n, D, tm=tm, tn=n, tk=D, tb=True, outs=[((S, n), F32, (tm, n), _row)])
    col_shards = [((NSH, n, D // NSH), F32, (None, n, D // NSH), lambda i, j, k: (j, 0, 0))]
    d_w_sb, = _mm("mix_dwsb", ya, dysb, n, D, S, tm=n, tn=D // NSH, tk=tm, ta=True, order="ji", outs=col_shards)
    dog, = _mm("mix_dog", dygdn, w_gdn, S, n, D, tm=tm, tn=n, tk=D, tb=True, outs=[((S, n), F32, (tm, n), _row)])
    d_w_gdn, = _mm("mix_dwgdn", og, dygdn, n, D, S, tm=n, tn=D // NSH, tk=tm, ta=True, order="ji", outs=col_shards)
    dq, dk, dv = _attn_bwd(m, runs, dya)
    dyc, dgz, dgba, d_alog, d_dtb, d_normw = _gdn_bwd(yc, m, alog, dtb, normw, states, dog)
    dxc, d_conv = _conv_bwd(m, dyc, conv_w)
    dm = jnp.concatenate([dq, dk.astype(BF), dv.astype(BF), dxc, dgz, dgba, jnp.zeros((S, GATE0 - 3712), BF), dgate], axis=1)
    d_h1, = _mm("mix_dh", dm, wp, S, D, NP, tm=tm, tn=D, tk=1536, tb=True,
                extras=[(dr, (tm, D), _row)], epilogue=lambda acc, d: (acc + ALPHA * d,),
                outs=[((S, D), F32, (tm, D), _row)])
    d_wp, = _mm("mix_dwp", h1, dm, D, NP, S, tm=D, tn=1536, tk=tm, ta=True, order="ji",
                outs=[((D, NP), F32, (D, 1536), _tile)])
    return d_h1, dict(wp=d_wp, b_gate=d_bg, conv_w=d_conv, alog=d_alog, dtb=d_dtb, normw=d_normw,
                      w_sb=d_w_sb, w_gdn=d_w_gdn, w_mo=d_w_mo, g=dg, b=db)


def _ple_fwd(h3, p, w_pg, b_pg, w_ple, g, b, target):
    S = h3.shape[0]
    tm = min(512, S)
    pd = p.shape[1]
    pe, = _mm("ple_emb", p, w_ple, S, D, pd, tm=tm, tn=D, tk=pd, outs=[((S, D), F32, (tm, D), _row)])

    def epi(acc, e, xin, tgt, bp, gg, bb):
        gt = _sigmoid(acc + bp)
        r = ALPHA * xin + gt * e
        diff = _layer_norm(r, gg, bb) - tgt
        return gt, r, diff * (1.0 / D), jnp.sum(diff * diff, axis=0, keepdims=True)

    gt, r4, dh4, loss_row = _mm(
        "ple_gate", h3, w_pg, S, D, D, tm=tm, tn=D, tk=D,
        extras=[(pe, (tm, D), _row), (h3, (tm, D), _row), (target, (tm, D), _row), (b_pg, (1, D), _const),
                (g, (1, D), _const), (b, (1, D), _const)], epilogue=epi, n_acc=1,
        outs=[((S, D), F32, (tm, D), _row), ((S, D), F32, (tm, D), _row), ((S, D), F32, (tm, D), _row),
              ((1, D), F32, (1, D), _const)])
    return dh4, loss_row, (h3, p, pe, gt, r4)


def _ple_bwd(saved, w_pg, g, dh4):
    h3, p, pe, gt, r4 = saved
    S = h3.shape[0]
    tm = min(512, S)
    pd = p.shape[1]

    def fn(r, gg, dh, e, t):
        dr, dg, db = _layer_norm_bwd(r, gg, dh)
        dpre = dr * e * t * (1.0 - t)
        return dr, dpre, dr * t, dg, db, jnp.sum(dpre, axis=0, keepdims=True)

    row, one = (lambda i: (i, 0)), (lambda i: (0, 0))
    dr, dpre, dpe, dg, db, d_bpg = _rows(
        "ple_lnbwd", fn, S // tm,
        [(r4, (tm, D), row), (g, (1, D), one), (dh4, (tm, D), row), (pe, (tm, D), row), (gt, (tm, D), row)],
        [((S, D), F32, (tm, D), row), ((S, D), BF, (tm, D), row), ((S, D), BF, (tm, D), row),
         ((1, D), F32, (1, D), one), ((1, D), F32, (1, D), one), ((1, D), F32, (1, D), one)], n_acc=3)
    d_w_pg, = _mm("ple_dwpg", h3, dpre, D, D, S, tm=D, tn=D, tk=tm, ta=True, outs=[((D, D), F32, (D, D), _tile)])
    d_w_ple, = _mm("ple_dwple", p, dpe, pd, D, S, tm=pd, tn=D // NSH, tk=tm, ta=True, order="ji",
                   outs=[((NSH, pd, D // NSH), F32, (None, pd, D // NSH), lambda i, j, k: (j, 0, 0))])
    d_h3, = _mm("ple_dh", dpre, w_pg, S, D, D, tm=tm, tn=D, tk=D, tb=True,
                extras=[(dr, (tm, D), _row)], epilogue=lambda acc, d: (acc + ALPHA * d,),
                outs=[((S, D), F32, (tm, D), _row)])
    return d_h3, d_w_pg, d_bpg, d_w_ple, dg, db


def _local_step(x, p, target, W):
    h1, sv1 = _ffn_fwd("ffn1", x, W["ffn1_in"], W["ffn1_out"], W["ln1_g"], W["ln1_b"])
    h2, sv2 = _mixer_fwd(h1, W["wp"], W["b_gate"], W["conv_w"], W["alog"], W["dtb"], W["normw"],
                         W["w_sb"], W["w_gdn"], W["w_mo"], W["ln2_g"], W["ln2_b"])
    h3, sv3 = _ffn_fwd("ffn2", h2, W["ffn2_in"], W["ffn2_out"], W["ln3_g"], W["ln3_b"])
    dh4, loss_row, sv4 = _ple_fwd(h3, p, W["w_pg"], W["b_pg"], W["w_ple"], W["ln4_g"], W["ln4_b"], target)
    G = {}
    dh3, G["w_pg"], G["b_pg"], G["w_ple"], G["ln4_g"], G["ln4_b"] = _ple_bwd(sv4, W["w_pg"], W["ln4_g"], dh4)
    dh2, G["ffn2_in"], G["ffn2_out"], G["ln3_g"], G["ln3_b"] = _ffn_bwd("ffn2", sv3, W["ffn2_in"], W["ffn2_out"], W["ln3_g"], dh3)
    dh1, gm = _mixer_bwd(sv2, W["wp"], W["b_gate"], W["conv_w"], W["alog"], W["dtb"], W["normw"],
                         W["w_sb"], W["w_gdn"], W["w_mo"], W["ln2_g"], dh2)
    G.update({k: v for k, v in gm.items() if k not in ("g", "b")})
    G["ln2_g"], G["ln2_b"] = gm["g"], gm["b"]
    dx, G["ffn1_in"], G["ffn1_out"], G["ln1_g"], G["ln1_b"] = _ffn_bwd("ffn1", sv1, W["ffn1_in"], W["ffn1_out"], W["ln1_g"], dh1)
    return loss_row, dx, G


S2 = CUT - 2 * MSH


def _pack_wp(w4):
    tr = 256

    def fn(w):
        s = [w[j].astype(F32) for j in range(NSH)]
        full = jnp.concatenate([s[0], s[1], s[2][:, :S2], jnp.zeros((tr, GATE0 - CUT), F32), s[2][:, S2:], s[3]], axis=1)
        return (full,)

    return _rows("pack_wp", fn, D // tr, [(w4, (NSH, tr, MSH), lambda i: (0, i, 0))],
                 [((D, NP), BF, (tr, NP), lambda i: (i, 0))])[0]


def _unpack_wp(d):
    tr = 256
    g2 = GATE0 + MSH - S2

    def fn(v):
        return (jnp.stack([v[:, :MSH], v[:, MSH:2 * MSH], jnp.concatenate([v[:, 2 * MSH:CUT], v[:, GATE0:g2]], axis=1), v[:, g2:]]),)

    return _rows("unpack_wp", fn, D // tr, [(d, (tr, NP), lambda i: (i, 0))],
                 [((NSH, D, MSH), F32, (NSH, tr, MSH), lambda i: (0, i, 0))])[0]


def _cast_bf16(tag, w):
    r, c = w.shape
    tr = _pick(r, 256)
    return _rows(f"cast_{tag}", lambda v: (v,), r // tr, [(w, (tr, c), lambda i: (i, 0))],
                 [((r, c), BF, (tr, c), lambda i: (i, 0))])[0]


ANY = pl.BlockSpec(memory_space=pl.ANY)


def _place():
    return lax.axis_index("x"), lax.axis_index("y"), lax.axis_index("c")


def _gather_shards(shards):
    n = len(shards)

    def body(*refs):
        ins, outs = refs[:n], refs[n:2 * n]
        send, recv, loc = refs[2 * n:]
        x, y, c = _place()
        me = 2 * x + y
        chips = [(1 - x, y), (x, 1 - y), (1 - x, 1 - y)]
        started = []
        for t in range(n):
            lc = pltpu.make_async_copy(ins[t], outs[t].at[me], loc.at[t])
            lc.start()
            started.append(lc)
            for q, (px, py) in enumerate(chips):
                cp = pltpu.make_async_remote_copy(src_ref=ins[t], dst_ref=outs[t].at[me], send_sem=send.at[3 * t + q],
                                                  recv_sem=recv.at[3 * t + q], device_id=(px, py, c), device_id_type=MESH)
                cp.start()
                started.append(cp)
        for t in range(n):
            for q, (px, py) in enumerate(chips):
                pltpu.make_async_remote_copy(src_ref=ins[t], dst_ref=outs[t].at[2 * px + py], send_sem=send.at[3 * t + q],
                                             recv_sem=recv.at[3 * t + q], device_id=(px, py, c), device_id_type=MESH).wait_recv()
        for t in range(n):
            started[4 * t].wait()
            for q in range(3):
                started[4 * t + 1 + q].wait_send()

    return pl.pallas_call(
        body, name="gather_weights", in_specs=[ANY] * n, out_specs=[ANY] * n,
        out_shape=[jax.ShapeDtypeStruct((NSH,) + s.shape, s.dtype) for s in shards],
        scratch_shapes=[pltpu.SemaphoreType.DMA((3 * n,)), pltpu.SemaphoreType.DMA((3 * n,)), pltpu.SemaphoreType.DMA((n,))],
    )(*shards)


def _reduce_sibling(gs):
    n = len(gs)

    def body(*refs):
        ins, mine, theirs = refs[:n], refs[n:2 * n], refs[2 * n:3 * n]
        send, recv, loc = refs[3 * n:]
        x, y, c = _place()
        started = []
        for t in range(n):
            h = gs[t].shape[1] // 2
            lc = pltpu.make_async_copy(ins[t].at[:, pl.ds(pl.multiple_of(c * h, 8), h), :], mine[t], loc.at[t])
            cp = pltpu.make_async_remote_copy(src_ref=ins[t].at[:, pl.ds(pl.multiple_of((1 - c) * h, 8), h), :], dst_ref=theirs[t],
                                              send_sem=send.at[t], recv_sem=recv.at[t], device_id=(x, y, 1 - c), device_id_type=MESH)
            lc.start()
            cp.start()
            started += [lc, cp]
        for t in range(n):
            started[2 * t].wait()
            started[2 * t + 1].wait()

    half = [jax.ShapeDtypeStruct((NSH, g.shape[1] // 2, g.shape[2]), g.dtype) for g in gs]
    res = pl.pallas_call(
        body, name="reduce_sibling", in_specs=[ANY] * n, out_specs=[ANY] * (2 * n), out_shape=half + half,
        scratch_shapes=[pltpu.SemaphoreType.DMA((n,)), pltpu.SemaphoreType.DMA((n,)), pltpu.SemaphoreType.DMA((n,))],
    )(*gs)
    return res[:n], res[n:]


def _reduce_chips(ps):
    n = len(ps)

    def body(*refs):
        ins, outs = refs[:n], refs[n:2 * n]
        send, recv, loc = refs[2 * n:]
        x, y, c = _place()
        me = 2 * x + y
        chips = [(1 - x, y), (x, 1 - y), (1 - x, 1 - y)]
        started = []
        for t in range(n):
            lc = pltpu.make_async_copy(ins[t].at[me], outs[t].at[me], loc.at[t])
            lc.start()
            started.append(lc)
            for q, (px, py) in enumerate(chips):
                cp = pltpu.make_async_remote_copy(src_ref=ins[t].at[2 * px + py], dst_ref=outs[t].at[me], send_sem=send.at[3 * t + q],
                                                  recv_sem=recv.at[3 * t + q], device_id=(px, py, c), device_id_type=MESH)
                cp.start()
                started.append(cp)
        for t in range(n):
            for q, (px, py) in enumerate(chips):
                pltpu.make_async_remote_copy(src_ref=ins[t].at[me], dst_ref=outs[t].at[2 * px + py], send_sem=send.at[3 * t + q],
                                             recv_sem=recv.at[3 * t + q], device_id=(px, py, c), device_id_type=MESH).wait_recv()
        for t in range(n):
            started[4 * t].wait()
            for q in range(3):
                started[4 * t + 1 + q].wait_send()

    return pl.pallas_call(
        body, name="reduce_chips", in_specs=[ANY] * n, out_specs=[ANY] * n,
        out_shape=[jax.ShapeDtypeStruct(p_.shape, p_.dtype) for p_ in ps],
        scratch_shapes=[pltpu.SemaphoreType.DMA((3 * n,)), pltpu.SemaphoreType.DMA((3 * n,)), pltpu.SemaphoreType.DMA((n,))],
    )(*ps)


def _share_sibling(fs):
    n = len(fs)

    def body(*refs):
        ins, outs = refs[:n], refs[n:2 * n]
        send, recv, loc = refs[2 * n:]
        x, y, c = _place()
        started = []
        for t in range(n):
            lc = pltpu.make_async_copy(ins[t], outs[t].at[c], loc.at[t])
            cp = pltpu.make_async_remote_copy(src_ref=ins[t], dst_ref=outs[t].at[c], send_sem=send.at[t], recv_sem=recv.at[t],
                                              device_id=(x, y, 1 - c), device_id_type=MESH)
            lc.start()
            cp.start()
            started += [lc, cp]
        for t in range(n):
            started[2 * t].wait()
            started[2 * t + 1].wait_send()
            pltpu.make_async_remote_copy(src_ref=ins[t], dst_ref=outs[t].at[1 - c], send_sem=send.at[t], recv_sem=recv.at[t],
                                         device_id=(x, y, 1 - c), device_id_type=MESH).wait_recv()

    return pl.pallas_call(
        body, name="share_sibling", in_specs=[ANY] * n, out_specs=[ANY] * n,
        out_shape=[jax.ShapeDtypeStruct((2,) + f.shape, f.dtype) for f in fs],
        scratch_shapes=[pltpu.SemaphoreType.DMA((n,)), pltpu.SemaphoreType.DMA((n,)), pltpu.SemaphoreType.DMA((n,))],
    )(*fs)


NDEV = 8


def _allreduce_small(pack):
    r, w = pack.shape
    rel = [(dx, dy, dc) for dx in (0, 1) for dy in (0, 1) for dc in (0, 1) if (dx, dy, dc) != (0, 0, 0)]

    def body(in_ref, out_ref, buf, send, recv):
        x, y, c = _place()
        me = 4 * x + 2 * y + c
        buf[me] = in_ref[...]
        peers = [((x + dx) % 2, (y + dy) % 2, (c + dc) % 2) for dx, dy, dc in rel]
        sent = []
        for k, peer in enumerate(peers):
            cp = pltpu.make_async_remote_copy(src_ref=in_ref, dst_ref=buf.at[me], send_sem=send.at[k], recv_sem=recv.at[k],
                                              device_id=peer, device_id_type=MESH)
            cp.start()
            sent.append(cp)
        for k, (px, py, pc) in enumerate(peers):
            pltpu.make_async_remote_copy(src_ref=in_ref, dst_ref=buf.at[4 * px + 2 * py + pc], send_sem=send.at[k], recv_sem=recv.at[k],
                                         device_id=(px, py, pc), device_id_type=MESH).wait_recv()
        for cp in sent:
            cp.wait_send()
        acc = buf[0]
        for k in range(1, NDEV):
            acc = acc + buf[k]
        out_ref[...] = acc

    vm = pl.BlockSpec(memory_space=pltpu.VMEM)
    return pl.pallas_call(
        body, name="allreduce_small", in_specs=[vm], out_specs=vm, out_shape=jax.ShapeDtypeStruct((r, w), F32),
        scratch_shapes=[pltpu.VMEM((NDEV, r, w), F32), pltpu.SemaphoreType.DMA((NDEV - 1,)), pltpu.SemaphoreType.DMA((NDEV - 1,))],
    )(pack)


def _add2(tag, a, b):
    _, h, c = a.shape
    tr = _pick(h, max(8, 262144 // c // 8 * 8))
    spec = ((None, tr, c), lambda i: (i // (h // tr), i % (h // tr), 0))
    return _rows(f"add2_{tag}", lambda u, v: (u + v,), NSH * (h // tr), [(a,) + spec, (b,) + spec], [(a.shape, F32) + spec])[0]


def _add4(tag, a):
    _, h, c = a.shape
    tr = _pick(h, max(8, 131072 // c // 8 * 8))
    return _rows(f"add4_{tag}", lambda v: (((v[0] + v[1]) + v[2]) + v[3],), h // tr, [(a, (NSH, tr, c), lambda i: (0, i, 0))],
                 [((h, c), F32, (tr, c), lambda i: (i, 0))])[0]


def _adamw(tag, w, g, m, v):
    r, c = w.shape
    tr = _pick(r, max(8, 262144 // c // 8 * 8))

    def fn(w_, g_, m_, v_):
        m2 = B1 * m_ + (1.0 - B1) * g_
        v2 = B2 * v_ + (1.0 - B2) * (g_ * g_)
        m_hat = m2 / (1.0 - B1 ** STEP)
        v_hat = v2 / (1.0 - B2 ** STEP)
        return -LR * (m_hat / (jnp.sqrt(v_hat) + EPS) + WD * w_), m2, v2

    spec = ((tr, c), lambda i: (i, 0))
    return _rows(f"adamw_{tag}", fn, r // tr, [(a,) + spec for a in (w, g, m, v)], [((r, c), F32) + spec] * 3)


BIG = ("ffn1_w_in", "ffn1_w_out", "w_mix_in", "w_branch_sb", "w_branch_gdn", "w_mix_out", "ffn2_w_in", "ffn2_w_out",
       "w_ple_gate", "w_ple")
SMALL = ("ln1_g", "ln1_b", "b_gate", "conv_w", "a_log", "dt_bias", "gdn_norm_w", "ln2_g", "ln2_b", "ln3_g", "ln3_b",
         "b_ple_gate", "ln4_g", "ln4_b")
ORDER = ("ffn1_w_in", "ffn1_w_out", "ln1_g", "ln1_b", "w_mix_in", "b_gate", "conv_w", "a_log", "dt_bias", "gdn_norm_w",
         "w_branch_sb", "w_branch_gdn", "w_mix_out", "ln2_g", "ln2_b", "ffn2_w_in", "ffn2_w_out", "ln3_g", "ln3_b",
         "w_ple_gate", "b_ple_gate", "w_ple", "ln4_g", "ln4_b")
PACK_W = 2304


def _lane_row(v, lanes=128, at=HEADS):
    return jnp.pad(v[None, :], ((0, 0), (at, lanes - at - v.shape[0])))


def _col_join(w4):
    return jnp.transpose(w4, (1, 0, 2)).reshape(w4.shape[1], NSH * w4.shape[2])


def kernel(x, p, ffn1_w_in, ffn1_w_out, ln1_g, ln1_b, w_mix_in, b_gate, conv_w, a_log, dt_bias, gdn_norm_w, w_branch_sb, w_branch_gdn, w_mix_out, ln2_g, ln2_b, ffn2_w_in, ffn2_w_out, ln3_g, ln3_b, w_ple_gate, b_ple_gate, w_ple, ln4_g, ln4_b, loss_target, m_ffn1_w_in, m_ffn1_w_out, m_ln1_g, m_ln1_b, m_w_mix_in, m_b_gate, m_conv_w, m_a_log, m_dt_bias, m_gdn_norm_w, m_w_branch_sb, m_w_branch_gdn, m_w_mix_out, m_ln2_g, m_ln2_b, m_ffn2_w_in, m_ffn2_w_out, m_ln3_g, m_ln3_b, m_w_ple_gate, m_b_ple_gate, m_w_ple, m_ln4_g, m_ln4_b, v_ffn1_w_in, v_ffn1_w_out, v_ln1_g, v_ln1_b, v_w_mix_in, v_b_gate, v_conv_w, v_a_log, v_dt_bias, v_gdn_norm_w, v_w_branch_sb, v_w_branch_gdn, v_w_mix_out, v_ln2_g, v_ln2_b, v_ffn2_w_in, v_ffn2_w_out, v_ln3_g, v_ln3_b, v_w_ple_gate, v_b_ple_gate, v_w_ple, v_ln4_g, v_ln4_b):
    args = dict(locals())
    w = {n: args[n][0] for n in ORDER}
    mom = {n: args["m_" + n][0] for n in ORDER}
    var = {n: args["v_" + n][0] for n in ORDER}

    shards = [_cast_bf16(n, w[n]) for n in BIG] + [w["conv_w"]]
    full = dict(zip(BIG + ("conv_w",), _gather_shards(shards)))
    W = dict(
        ffn1_in=full["ffn1_w_in"], ffn1_out=full["ffn1_w_out"].reshape(DFF, D),
        ffn2_in=full["ffn2_w_in"], ffn2_out=full["ffn2_w_out"].reshape(DFF, D),
        wp=_pack_wp(full["w_mix_in"]), w_sb=_col_join(full["w_branch_sb"]), w_gdn=_col_join(full["w_branch_gdn"]),
        w_mo=full["w_mix_out"].reshape(D, D), w_pg=full["w_ple_gate"].reshape(D, D), w_ple=_col_join(full["w_ple"]),
        conv_w=_col_join(full["conv_w"]), b_gate=w["b_gate"][None], alog=_lane_row(w["a_log"]), dtb=_lane_row(w["dt_bias"]),
        normw=w["gdn_norm_w"][None], b_pg=w["b_ple_gate"][None],
        **{f"ln{i}_{s}": w[f"ln{i}_{s}"][None] for i in (1, 2, 3, 4) for s in ("g", "b")},
    )

    loss_row, grad_x, G = _local_step(x[0], p[0, 0], loss_target[0], W)
    loss = lax.psum(0.5 * jnp.sum(loss_row) / D, ("x", "y", "c"))

    big = dict(
        ffn1_w_in=G["ffn1_in"], ffn1_w_out=G["ffn1_out"].reshape(NSH, DFF // NSH, D), w_mix_in=_unpack_wp(G["wp"]),
        w_branch_sb=G["w_sb"], w_branch_gdn=G["w_gdn"], w_mix_out=G["w_mo"].reshape(NSH, D // NSH, D),
        ffn2_w_in=G["ffn2_in"], ffn2_w_out=G["ffn2_out"].reshape(NSH, DFF // NSH, D),
        w_ple_gate=G["w_pg"].reshape(NSH, D // NSH, D), w_ple=G["w_ple"],
    )
    mine, theirs = _reduce_sibling([big[n] for n in BIG])
    chip_sums = _reduce_chips([_add2(n, a, b) for n, a, b in zip(BIG, mine, theirs)])
    halves = _share_sibling([_add4(n, a) for n, a in zip(BIG, chip_sums)])
    grad = {n: h.reshape(w[n].shape) for n, h in zip(BIG, halves)}

    pieces = [G["ln1_g"], G["ln1_b"], G["b_gate"], G["conv_w"].reshape(1, 4 * CW), G["alog"], G["dtb"], G["normw"],
              G["ln2_g"], G["ln2_b"], G["ln3_g"], G["ln3_b"], G["b_pg"], G["ln4_g"], G["ln4_b"]]
    flat = jnp.concatenate(pieces, axis=1)
    flat = jnp.pad(flat, ((0, 0), (0, NDEV * PACK_W - flat.shape[1])))
    total = _allreduce_small(flat.reshape(NDEV, PACK_W)).reshape(1, NDEV * PACK_W)
    off = 0
    for n, piece in zip(SMALL, pieces):
        grad[n] = total[0, off:off + piece.shape[1]]
        off += piece.shape[1]
    chip = 2 * lax.axis_index("x") + lax.axis_index("y")
    grad["conv_w"] = lax.dynamic_slice_in_dim(grad["conv_w"].reshape(4, CW), chip * (CW // NSH), CW // NSH, axis=1)
    grad["a_log"] = grad["a_log"][HEADS:2 * HEADS]
    grad["dt_bias"] = grad["dt_bias"][HEADS:2 * HEADS]

    delta, new_m, new_v = {}, {}, {}
    for n in ORDER:
        shape2 = w[n].shape if w[n].ndim == 2 else (1, w[n].shape[0])
        d_, m_, v_ = _adamw(n, *[a.reshape(shape2) for a in (w[n], grad[n], mom[n], var[n])])
        delta[n], new_m[n], new_v[n] = (a.reshape(args[n].shape) for a in (d_, m_, v_))
    outs = [loss, grad_x[None]]
    outs += [grad[n].reshape(args[n].shape) for n in ORDER]
    for group in (delta, new_m, new_v):
        outs += [group[n] for n in ORDER]
    return tuple(outs)
```

```python
import functools

import jax
import jax.numpy as jnp
from jax import lax
from jax.experimental import pallas as pl
from jax.experimental.pallas import tpu as pltpu

F32 = jnp.float32
BF = jnp.bfloat16
I32 = jnp.int32
HI = lax.Precision.HIGHEST
MESH = pl.DeviceIdType.MESH

D = 1024
DFF = 2816
NSH = 4
FSH = 2 * DFF // NSH
NIN = 5648
MSH = NIN // NSH
NP = 6144
CUT = 3600
GATE0 = 4096
HEADS = 8
HD = 64
CH = 64
KB = 128
ALPHA = 2.0 ** 0.25
LN_EPS = 1e-5
RMS_EPS = 1e-6
B1, B2, LR, EPS, WD, STEP = 0.9, 0.999, 0.001, 1e-08, 0.01, 10


def _sigmoid(x):
    return 0.5 * jnp.tanh(0.5 * x) + 0.5


def _softplus(x):
    return jnp.maximum(x, 0.0) + jnp.log1p(jnp.exp(-jnp.abs(x)))


def _layer_norm(r, g, b):
    mu = jnp.mean(r, axis=-1, keepdims=True)
    xc = r - mu
    var = jnp.mean(xc * xc, axis=-1, keepdims=True)
    return xc * lax.rsqrt(var + LN_EPS) * g + b


def _layer_norm_bwd(r, g, dh):
    mu = jnp.mean(r, axis=-1, keepdims=True)
    xc = r - mu
    var = jnp.mean(xc * xc, axis=-1, keepdims=True)
    xhat = xc * lax.rsqrt(var + LN_EPS)
    dxh = dh * g
    dr = lax.rsqrt(var + LN_EPS) * (dxh - jnp.mean(dxh, axis=-1, keepdims=True) - xhat * jnp.mean(dxh * xhat, axis=-1, keepdims=True))
    return dr, jnp.sum(dh * xhat, axis=0, keepdims=True), jnp.sum(dh, axis=0, keepdims=True)


def _pick(n, cap):
    if n <= cap:
        return n
    for t in range(cap - cap % 8, 7, -8):
        if n % t == 0:
            return t
    raise ValueError((n, cap))


def _mm(name, a, b, M, N, K, *, tm, tn, tk, ta=False, tb=False, a_spec=None, b_spec=None, order="ij",
        extras=(), epilogue=None, outs, n_acc=0):
    ni, nj, nk = M // tm, N // tn, K // tk
    assert M % tm == 0 and N % tn == 0 and K % tk == 0, (name, M, N, K, tm, tn, tk)
    assert n_acc == 0 or nj == 1

    def wrap(fn):
        if order == "ij":
            return lambda g0, g1, g2: fn(g0, g1, g2)
        return lambda g0, g1, g2: fn(g1, g0, g2)

    if a_spec is None:
        a_spec = ((tk, tm), lambda i, j, k: (k, i)) if ta else ((tm, tk), lambda i, j, k: (i, k))
    if b_spec is None:
        b_spec = ((tn, tk), lambda i, j, k: (j, k)) if tb else ((tk, tn), lambda i, j, k: (k, j))
    dims = (((0 if ta else 1,), (1 if tb else 0,)), ((), ()))
    ne, no = len(extras), len(outs)
    grid = (ni, nj, nk) if order == "ij" else (nj, ni, nk)

    def body(*refs):
        a_ref, b_ref = refs[0], refs[1]
        ex = refs[2:2 + ne]
        o = refs[2 + ne:2 + ne + no]
        g0, g1, k = pl.program_id(0), pl.program_id(1), pl.program_id(2)
        first = jnp.logical_and(g0 == 0, g1 == 0)
        p = lax.dot_general(a_ref[...].astype(BF), b_ref[...].astype(BF), dims, preferred_element_type=F32)

        def finish(acc):
            vals = (acc,) if epilogue is None else epilogue(acc, *[e[...] for e in ex])
            for idx, (ref, val) in enumerate(zip(o, vals)):
                if idx < no - n_acc:
                    ref[...] = val.astype(ref.dtype)
                else:
                    @pl.when(first)
                    def _(ref=ref, val=val):
                        ref[...] = val

                    @pl.when(jnp.logical_not(first))
                    def _(ref=ref, val=val):
                        ref[...] += val

        if nk == 1:
            finish(p)
        else:
            acc_ref = refs[-1]

            @pl.when(k == 0)
            def _():
                acc_ref[...] = p

            @pl.when(k > 0)
            def _():
                acc_ref[...] += p

            @pl.when(k == nk - 1)
            def _():
                finish(acc_ref[...])

    in_specs = [pl.BlockSpec(a_spec[0], wrap(a_spec[1])), pl.BlockSpec(b_spec[0], wrap(b_spec[1]))]
    in_specs += [pl.BlockSpec(blk, wrap(fn)) for _, blk, fn in extras]
    res = pl.pallas_call(
        body, name=name, grid=grid, in_specs=in_specs,
        out_specs=[pl.BlockSpec(blk, wrap(fn)) for _, _, blk, fn in outs],
        out_shape=[jax.ShapeDtypeStruct(shape, dt) for shape, dt, _, _ in outs],
        scratch_shapes=[pltpu.VMEM((tm, tn), F32)] if nk > 1 else [],
    )(a, b, *[e[0] for e in extras])
    return res


def _row(i, j, k):
    return (i, 0)


def _tile(i, j, k):
    return (i, j)


def _const(i, j, k):
    return (0, 0)


def _rows(name, fn, n_steps, ins, outs, n_acc=0):
    ni, no = len(ins), len(outs)

    def body(*refs):
        i = pl.program_id(0)
        vals = fn(*[r[...] for r in refs[:ni]])
        for idx, (ref, val) in enumerate(zip(refs[ni:ni + no], vals)):
            if idx < no - n_acc:
                ref[...] = val.astype(ref.dtype)
            else:
                @pl.when(i == 0)
                def _(ref=ref, val=val):
                    ref[...] = val

                @pl.when(i > 0)
                def _(ref=ref, val=val):
                    ref[...] += val

    return pl.pallas_call(
        body, name=name, grid=(n_steps,),
        in_specs=[pl.BlockSpec(blk, fn_) for _, blk, fn_ in ins],
        out_specs=[pl.BlockSpec(blk, fn_) for _, _, blk, fn_ in outs],
        out_shape=[jax.ShapeDtypeStruct(shape, dt) for shape, dt, _, _ in outs],
    )(*[a for a, _, _ in ins])


def _ffn_fwd(tag, x, w_in, w_out, g, b):
    S = x.shape[0]
    tm = min(512, S)
    gate, = _mm(f"{tag}_gate", x, w_in, S, DFF, D, tm=tm, tn=FSH, tk=D, order="ji",
                b_spec=((None, D, FSH), lambda i, j, k: (j, 0, 0)),
                outs=[((S, DFF), F32, (tm, FSH), _tile)])

    def up_epi(acc, gt):
        return acc, gt * _sigmoid(gt) * acc

    up, s = _mm(f"{tag}_up", x, w_in, S, DFF, D, tm=tm, tn=FSH, tk=D, order="ji",
                b_spec=((None, D, FSH), lambda i, j, k: (j + 2, 0, 0)),
                extras=[(gate, (tm, FSH), _tile)], epilogue=up_epi,
                outs=[((S, DFF), F32, (tm, FSH), _tile), ((S, DFF), BF, (tm, FSH), _tile)])

    def out_epi(acc, xin, gg, bb):
        r = ALPHA * xin + 0.5 * acc
        return _layer_norm(r, gg, bb), r

    h, r = _mm(f"{tag}_out", s, w_out, S, D, DFF, tm=tm, tn=D, tk=DFF,
               extras=[(x, (tm, D), _row), (g, (1, D), _const), (b, (1, D), _const)], epilogue=out_epi,
               outs=[((S, D), F32, (tm, D), _row), ((S, D), F32, (tm, D), _row)])
    return h, (x, gate, up, s, r)


def _ln_bwd(tag, r, g, dh):
    S = r.shape[0]
    tm = min(512, S)
    return _rows(f"{tag}_lnbwd", _layer_norm_bwd, S // tm,
                 [(r, (tm, D), lambda i: (i, 0)), (g, (1, D), lambda i: (0, 0)), (dh, (tm, D), lambda i: (i, 0))],
                 [((S, D), F32, (tm, D), lambda i: (i, 0)), ((1, D), F32, (1, D), lambda i: (0, 0)),
                  ((1, D), F32, (1, D), lambda i: (0, 0))], n_acc=2)


def _ffn_bwd(tag, saved, w_in, w_out, g, dh):
    x, gate, up, s, r = saved
    S = x.shape[0]
    tm = min(512, S)
    dr, dg, db = _ln_bwd(tag, r, g, dh)

    def act_epi(acc, gt, u):
        ds = 0.5 * acc
        sg = _sigmoid(gt)
        return (jnp.stack([ds * u * (sg * (1.0 + gt * (1.0 - sg))), ds * (gt * sg)]),)

    da, = _mm(f"{tag}_dact", dr, w_out, S, DFF, D, tm=tm, tn=FSH, tk=D, tb=True, order="ji",
              extras=[(gate, (tm, FSH), _tile), (up, (tm, FSH), _tile)], epilogue=act_epi,
              outs=[((2, S, DFF), BF, (2, tm, FSH), lambda i, j, k: (0, i, j))])
    d_w_out, = _mm(f"{tag}_dwout", s, dr, DFF, D, S, tm=FSH, tn=D, tk=tm, ta=True,
                   epilogue=lambda acc: (0.5 * acc,), outs=[((DFF, D), F32, (FSH, D), _tile)])
    d_in, = _mm(f"{tag}_dx", da, w_in, S, D, 2 * DFF, tm=tm, tn=D, tk=FSH, tb=True,
                a_spec=((None, tm, FSH), lambda i, j, k: (k // 2, i, k % 2)),
                b_spec=((None, D, FSH), lambda i, j, k: (k, 0, 0)),
                extras=[(dr, (tm, D), _row)], epilogue=lambda acc, d: (acc + ALPHA * d,),
                outs=[((S, D), F32, (tm, D), _row)])
    d_w_in, = _mm(f"{tag}_dwin", x, da, D, 2 * DFF, S, tm=D, tn=FSH, tk=tm, ta=True, order="ji",
                  b_spec=((None, tm, FSH), lambda i, j, k: (j // 2, k, j % 2)),
                  outs=[((NSH, D, FSH), F32, (None, D, FSH), lambda i, j, k: (j, 0, 0))])
    return d_in, d_w_in, d_w_out, dg, db


def _cum(vals, tri):
    hi = vals.astype(BF)
    lo = (vals - hi.astype(F32)).astype(BF)
    return jnp.dot(hi, tri, preferred_element_type=F32) + jnp.dot(lo, tri, preferred_element_type=F32)


def _nt(a, b):
    return lax.dot_general(a, b, (((1,), (1,)), ((), ())), preferred_element_type=F32)


def _tn(a, b):
    return lax.dot_general(a, b, (((0,), (0,)), ((), ())), preferred_element_type=F32)


def _attn_fwd(m):
    S = m.shape[0]
    tq = min(256, S)
    assert S // KB <= 128

    def body(q_ref, k_ref, v_ref, o_ref, r_ref, kb, vb):
        i = pl.program_id(1)

        @pl.when(i == 0)
        def _():
            kb[...] = k_ref[...].astype(BF)
            vb[...] = v_ref[...].astype(BF)

        qs = (q_ref[...] * 0.125).astype(BF)
        row = lax.broadcasted_iota(I32, (tq, KB), 0) + i * tq
        col = lax.broadcasted_iota(I32, (tq, KB), 1)
        tri = (lax.broadcasted_iota(I32, (KB, KB), 0) >= lax.broadcasted_iota(I32, (KB, KB), 1)).astype(BF)
        nb = (i + 1) * (tq // KB)
        r_ref[...] = jnp.zeros_like(r_ref)

        def step(t, carry):
            j = nb - 1 - t
            off = pl.multiple_of(j * KB, KB)
            mask = col + j * KB < row
            kblk = kb[pl.ds(off, KB), :]
            vblk = vb[pl.ds(off, KB), :]
            new = []
            for hh in range(2):
                run, acc = carry[2 * hh], carry[2 * hh + 1]
                sl = slice(hh * HD, (hh + 1) * HD)
                z = _nt(qs[:, sl], kblk[:, sl])
                cs = _cum(jnp.where(mask, _softplus(z), 0.0), tri)
                a = jnp.where(mask, jnp.exp(z - (run + cs)), 0.0)
                r_ref[hh] = jnp.where(col == j, run, r_ref[hh])
                new += [run + cs[:, :1], acc + jnp.dot(a.astype(BF), vblk[:, sl], preferred_element_type=F32)]
            return tuple(new)

        init = (jnp.zeros((tq, 1), F32), jnp.zeros((tq, HD), F32)) * 2
        res = lax.fori_loop(0, nb, step, init)
        o_ref[...] = jnp.concatenate([res[1], res[3]], axis=1)

    return pl.pallas_call(
        body, name="attn_fwd", grid=(HEADS // 2, S // tq),
        in_specs=[pl.BlockSpec((tq, KB), lambda h, i: (i, h)), pl.BlockSpec((S, KB), lambda h, i: (0, 4 + h)),
                  pl.BlockSpec((S, KB), lambda h, i: (0, 8 + h))],
        out_specs=[pl.BlockSpec((tq, KB), lambda h, i: (i, h)), pl.BlockSpec((2, tq, KB), lambda h, i: (h, i, 0))],
        out_shape=[jax.ShapeDtypeStruct((S, HEADS * HD), F32), jax.ShapeDtypeStruct((HEADS, S, KB), F32)],
        scratch_shapes=[pltpu.VMEM((S, KB), BF), pltpu.VMEM((S, KB), BF)],
    )(m, m, m)


def _attn_bwd(m, runs, dy):
    S = m.shape[0]
    tq = min(256, S)

    def body(q_ref, k_ref, v_ref, r_ref, dy_ref, dq_ref, dk_ref, dv_ref, kb, vb):
        i = pl.program_id(1)

        @pl.when(i == 0)
        def _():
            kb[...] = k_ref[...].astype(BF)
            vb[...] = v_ref[...].astype(BF)
            dk_ref[...] = jnp.zeros_like(dk_ref)
            dv_ref[...] = jnp.zeros_like(dv_ref)

        qs = (q_ref[...] * 0.125).astype(BF)
        dyb = dy_ref[...].astype(BF)
        rs = [r_ref[0], r_ref[1]]
        row = lax.broadcasted_iota(I32, (tq, KB), 0) + i * tq
        col = lax.broadcasted_iota(I32, (tq, KB), 1)
        jj = lax.broadcasted_iota(I32, (KB, KB), 0)
        ss = lax.broadcasted_iota(I32, (KB, KB), 1)
        tri_rev = (jj >= ss).astype(BF)
        tri_fwd = (jj <= ss).astype(BF)
        nb = (i + 1) * (tq // KB)

        def step(j, carry):
            off = pl.multiple_of(j * KB, KB)
            mask = col + j * KB < row
            kblk = kb[pl.ds(off, KB), :]
            vblk = vb[pl.ds(off, KB), :]
            new, dks, dvs = [], [], []
            for hh in range(2):
                pre, dq = carry[2 * hh], carry[2 * hh + 1]
                sl = slice(hh * HD, (hh + 1) * HD)
                run = jnp.sum(jnp.where(col == j, rs[hh], 0.0), axis=1, keepdims=True)
                z = _nt(qs[:, sl], kblk[:, sl])
                e = jnp.exp(-jnp.abs(z))
                sp = jnp.maximum(z, 0.0) + jnp.log1p(e)
                sig = jnp.where(z >= 0.0, 1.0, e) / (1.0 + e)
                cs = _cum(jnp.where(mask, sp, 0.0), tri_rev)
                a = jnp.where(mask, jnp.exp(z - (run + cs)), 0.0)
                gmat = a * _nt(dyb[:, sl], vblk[:, sl])
                pg = _cum(gmat, tri_fwd)
                dz = jnp.where(mask, gmat - sig * (pre + pg), 0.0).astype(BF)
                dks.append(_tn(dz, qs[:, sl]))
                dvs.append(_tn(a.astype(BF), dyb[:, sl]))
                new += [pre + pg[:, KB - 1:], dq + jnp.dot(dz, kblk[:, sl], preferred_element_type=F32)]
            dk_ref[pl.ds(off, KB), :] += jnp.concatenate(dks, axis=1)
            dv_ref[pl.ds(off, KB), :] += jnp.concatenate(dvs, axis=1)
            return tuple(new)

        init = (jnp.zeros((tq, 1), F32), jnp.zeros((tq, HD), F32)) * 2
        res = lax.fori_loop(0, nb, step, init)
        dq_ref[...] = (jnp.concatenate([res[1], res[3]], axis=1) * 0.125).astype(dq_ref.dtype)

    n = HEADS * HD
    return pl.pallas_call(
        body, name="attn_bwd", grid=(HEADS // 2, S // tq),
        in_specs=[pl.BlockSpec((tq, KB), lambda h, i: (i, h)), pl.BlockSpec((S, KB), lambda h, i: (0, 4 + h)),
                  pl.BlockSpec((S, KB), lambda h, i: (0, 8 + h)), pl.BlockSpec((2, tq, KB), lambda h, i: (h, i, 0)),
                  pl.BlockSpec((tq, KB), lambda h, i: (i, h))],
        out_specs=[pl.BlockSpec((tq, KB), lambda h, i: (i, h)), pl.BlockSpec((S, KB), lambda h, i: (0, h)),
                   pl.BlockSpec((S, KB), lambda h, i: (0, h))],
        out_shape=[jax.ShapeDtypeStruct((S, n), BF), jax.ShapeDtypeStruct((S, n), F32), jax.ShapeDtypeStruct((S, n), F32)],
        scratch_shapes=[pltpu.VMEM((S, KB), BF), pltpu.VMEM((S, KB), BF)],
    )(m, m, m, runs, dy)


SB = 256
NKC = SB // KB
HPS = 4
GW = HPS * HD
LOG2E = 1.4426950408889634
ANY = pl.BlockSpec(memory_space=pl.ANY)


def _split(vals):
    hi = vals.astype(BF)
    return hi, (vals - hi.astype(F32)).astype(BF)


def _chunk_sums(tri2, vals):
    hi, lo = _split(vals)
    return [jnp.dot(tri2, jnp.concatenate([hi[c * KB:(c + 1) * KB], lo[c * KB:(c + 1) * KB]], axis=0), preferred_element_type=F32)
            for c in range(NKC)]


def _head_halves(t, axis):
    idx = lax.broadcasted_iota(I32, t.shape, axis)
    return [jnp.where(idx < HD, t, 0.0).astype(BF), jnp.where(idx >= HD, t, 0.0).astype(BF)]


def _softplus2(z):
    return jnp.maximum(z, 0.0) + jnp.log2(1.0 + jnp.exp2(jnp.minimum(z, -z)))


def _attn_prep(m):
    S = m.shape[0]
    tm = min(512, S)
    n = HEADS * HD
    return _rows("attn_prep", lambda k, v: (k, v, k.T, v.T), S // tm,
                 [(m, (tm, n), lambda i: (i, 1)), (m, (tm, n), lambda i: (i, 2))],
                 [((S, n), BF, (tm, n), lambda i: (i, 0)), ((S, n), BF, (tm, n), lambda i: (i, 0)),
                  ((n, S), BF, (n, tm), lambda i: (0, i)), ((n, S), BF, (n, tm), lambda i: (0, i))])


def _attn_fwd_t(m, kb_all, vt_all):
    S = m.shape[0]
    assert S % SB == 0
    nkc = S // KB

    def body(q_ref, kb_hbm, vt_hbm, o_ref, r_ref, kb, vt, acc):
        h, i = pl.program_id(0), pl.program_id(1)

        @pl.when(i == 0)
        def _():
            cols = pl.ds(pl.multiple_of(h * GW, GW), GW)
            pltpu.sync_copy(kb_hbm.at[:, cols], kb)
            pltpu.sync_copy(vt_hbm.at[cols, :], vt)

        qt = (q_ref[...] * (0.125 * LOG2E)).T
        qtm = [t for g in range(HPS // 2) for t in _head_halves(qt[g * KB:(g + 1) * KB], 0)]
        dmask = lax.broadcasted_iota(I32, (SB, SB), 0) < lax.broadcasted_iota(I32, (SB, SB), 1)
        upper = (lax.broadcasted_iota(I32, (KB, KB), 1) >= lax.broadcasted_iota(I32, (KB, KB), 0)).astype(BF)
        tri2 = jnp.concatenate([upper, upper], axis=1)
        acc[...] = jnp.zeros_like(acc)
        r_ref[...] = jnp.zeros_like(r_ref)

        def block(jb, runs, masked):
            off = pl.multiple_of(jb * SB, SB)
            groups = [slice(g * KB, (g + 1) * KB) for g in range(HPS // 2)]
            kblk = [kb[pl.ds(off, SB), s] for s in groups]
            vtb = [vt[s, pl.ds(off, SB)] for s in groups]
            old = acc[...]
            zs = [jnp.dot(kblk[hh // 2], qtm[hh], preferred_element_type=F32) for hh in range(HPS)]
            sps = [_softplus2(z) for z in zs]
            if masked:
                sps = [jnp.where(dmask, sp, 0.0) for sp in sps]
            css = [_chunk_sums(tri2, sp) for sp in sps]
            run0s = [run + cs[1][0:1, :] for run, cs in zip(runs, css)]
            aa = [jnp.exp2(z - jnp.concatenate([run0 + cs[0], run + cs[1]], axis=0)) for z, run, run0, cs in zip(zs, runs, run0s, css)]
            if masked:
                aa = [jnp.where(dmask, a, 0.0) for a in aa]
            parts = [jnp.dot(vtb[hh // 2], aa[hh].astype(BF), preferred_element_type=F32) for hh in range(HPS)]
            upd = jnp.concatenate([parts[hh][(hh % 2) * HD:(hh % 2 + 1) * HD, :] for hh in range(HPS)], axis=0)
            for hh in range(HPS):
                r_ref[hh, pl.ds(NKC * jb, 1), :] = run0s[hh]
                r_ref[hh, pl.ds(NKC * jb + 1, 1), :] = runs[hh]
            acc[...] = old + upd
            return tuple(run0 + cs[0][0:1, :] for run0, cs in zip(run0s, css))

        zero = jnp.zeros((1, SB), F32)
        runs = block(i, (zero,) * HPS, True)
        lax.fori_loop(0, i, lambda t, c: block(i - 1 - t, c, False), runs)
        o_ref[...] = acc[...].T

    return pl.pallas_call(
        body, name="attn_fwd", grid=(HEADS // HPS, S // SB),
        in_specs=[pl.BlockSpec((SB, GW), lambda h, i: (i, h)), ANY, ANY],
        out_specs=[pl.BlockSpec((SB, GW), lambda h, i: (i, h)), pl.BlockSpec((HPS, nkc, SB), lambda h, i: (h, 0, i))],
        out_shape=[jax.ShapeDtypeStruct((S, HEADS * HD), F32), jax.ShapeDtypeStruct((HEADS, nkc, S), F32)],
        scratch_shapes=[pltpu.VMEM((S, GW), BF), pltpu.VMEM((GW, S), BF), pltpu.VMEM((GW, SB), F32)],
    )(m, kb_all, vt_all)


def _attn_bwd_t(m, kb_all, vb_all, kt_all, runs, dy):
    S = m.shape[0]
    nkc = S // KB

    def body(q_ref, kb_hbm, vb_hbm, kt_hbm, r_ref, dy_ref, dq_ref, dk_hbm, dv_hbm, kb, vb, kt, dqt, dka, dva):
        h, i = pl.program_id(0), pl.program_id(1)
        cols = pl.ds(pl.multiple_of(h * GW, GW), GW)

        @pl.when(i == 0)
        def _():
            pltpu.sync_copy(kb_hbm.at[:, cols], kb)
            pltpu.sync_copy(vb_hbm.at[:, cols], vb)
            pltpu.sync_copy(kt_hbm.at[cols, :], kt)
            dka[...] = jnp.zeros_like(dka)
            dva[...] = jnp.zeros_like(dva)

        q8 = q_ref[...] * 0.125
        dyf = dy_ref[...]
        q8t, dyt = (q8 * LOG2E).T, dyf.T
        groups = [slice(g * KB, (g + 1) * KB) for g in range(HPS // 2)]
        qtm = [t for s in groups for t in _head_halves(q8t[s], 0)]
        dytm = [t for s in groups for t in _head_halves(dyt[s], 0)]
        qlm = [t for s in groups for t in _head_halves(q8[:, s], 1)]
        dylm = [t for s in groups for t in _head_halves(dyf[:, s], 1)]
        dmask = lax.broadcasted_iota(I32, (SB, SB), 0) < lax.broadcasted_iota(I32, (SB, SB), 1)
        ri, ci = lax.broadcasted_iota(I32, (KB, KB), 0), lax.broadcasted_iota(I32, (KB, KB), 1)
        upper, lower = (ci >= ri).astype(BF), (ci <= ri).astype(BF)
        rev2, fwd2 = jnp.concatenate([upper, upper], axis=1), jnp.concatenate([lower, lower], axis=1)
        dqt[...] = jnp.zeros_like(dqt)

        def block(jb, pres, masked):
            off = pl.multiple_of(jb * SB, SB)
            heads = range(HPS)
            kblk = [kb[pl.ds(off, SB), s] for s in groups]
            vblk = [vb[pl.ds(off, SB), s] for s in groups]
            ktb = [kt[s, pl.ds(off, SB)] for s in groups]
            run0s = [r_ref[hh, pl.ds(NKC * jb, 1), :] for hh in heads]
            run1s = [r_ref[hh, pl.ds(NKC * jb + 1, 1), :] for hh in heads]
            old_dq, old_dk, old_dv = dqt[...], dka[pl.ds(off, SB), :], dva[pl.ds(off, SB), :]
            zs = [jnp.dot(kblk[hh // 2], qtm[hh], preferred_element_type=F32) for hh in heads]
            das = [jnp.dot(vblk[hh // 2], dytm[hh], preferred_element_type=F32) for hh in heads]
            sps = [_softplus2(z) for z in zs]
            sigs = [jnp.exp2(z - sp) for z, sp in zip(zs, sps)]
            if masked:
                sps = [jnp.where(dmask, sp, 0.0) for sp in sps]
            css = [_chunk_sums(rev2, sp) for sp in sps]
            aa = [jnp.exp2(z - jnp.concatenate([r0 + cs[0], r1 + cs[1]], axis=0)) for z, r0, r1, cs in zip(zs, run0s, run1s, css)]
            if masked:
                aa = [jnp.where(dmask, a, 0.0) for a in aa]
            gs = [a * da for a, da in zip(aa, das)]
            pgs = [_chunk_sums(fwd2, g) for g in gs]
            pre1s = [pre + pg[0][KB - 1:KB, :] for pre, pg in zip(pres, pgs)]
            dzs = [g - sig * jnp.concatenate([pre + pg[0], pre1 + pg[1]], axis=0)
                   for g, sig, pre, pre1, pg in zip(gs, sigs, pres, pre1s, pgs)]
            if masked:
                dzs = [jnp.where(dmask, dz, 0.0) for dz in dzs]
            dzb, ab = [dz.astype(BF) for dz in dzs], [a.astype(BF) for a in aa]
            dks = [jnp.dot(dzb[hh], qlm[hh], preferred_element_type=F32) for hh in heads]
            dvs = [jnp.dot(ab[hh], dylm[hh], preferred_element_type=F32) for hh in heads]
            parts = [jnp.dot(ktb[hh // 2], dzb[hh], preferred_element_type=F32) for hh in heads]
            dqt[...] = old_dq + jnp.concatenate([parts[hh][(hh % 2) * HD:(hh % 2 + 1) * HD, :] for hh in heads], axis=0)
            dka[pl.ds(off, SB), :] = old_dk + jnp.concatenate([dks[2 * g] + dks[2 * g + 1] for g in range(HPS // 2)], axis=1)
            dva[pl.ds(off, SB), :] = old_dv + jnp.concatenate([dvs[2 * g] + dvs[2 * g + 1] for g in range(HPS // 2)], axis=1)
            return tuple(pre1 + pg[1][KB - 1:KB, :] for pre1, pg in zip(pre1s, pgs))

        zero = jnp.zeros((1, SB), F32)
        pres = lax.fori_loop(0, i, lambda jb, c: block(jb, c, False), (zero,) * HPS)
        block(i, pres, True)
        dq_ref[...] = (dqt[...].T * 0.125).astype(dq_ref.dtype)

        @pl.when(i == pl.num_programs(1) - 1)
        def _():
            pltpu.sync_copy(dka, dk_hbm.at[:, cols])
            pltpu.sync_copy(dva, dv_hbm.at[:, cols])

    n = HEADS * HD
    return pl.pallas_call(
        body, name="attn_bwd", grid=(HEADS // HPS, S // SB),
        in_specs=[pl.BlockSpec((SB, GW), lambda h, i: (i, h)), ANY, ANY, ANY,
                  pl.BlockSpec((HPS, nkc, SB), lambda h, i: (h, 0, i)), pl.BlockSpec((SB, GW), lambda h, i: (i, h))],
        out_specs=[pl.BlockSpec((SB, GW), lambda h, i: (i, h)), ANY, ANY],
        out_shape=[jax.ShapeDtypeStruct((S, n), BF), jax.ShapeDtypeStruct((S, n), F32), jax.ShapeDtypeStruct((S, n), F32)],
        scratch_shapes=[pltpu.VMEM((S, GW), BF), pltpu.VMEM((S, GW), BF), pltpu.VMEM((GW, S), BF), pltpu.VMEM((GW, SB), F32),
                        pltpu.VMEM((S, GW), F32), pltpu.VMEM((S, GW), F32)],
    )(m, kb_all, vb_all, kt_all, runs, dy)


CW = 3 * HEADS * HD


def _shift_down(cur, prev8, s):
    if s == 0:
        return cur
    r = pltpu.roll(cur, s, 0)
    first = jnp.where(lax.broadcasted_iota(I32, (8, cur.shape[1]), 0) < s, pltpu.roll(prev8, s, 0), r[:8])
    return jnp.concatenate([first, r[8:]], axis=0)


def _shift_up(cur, next8, s):
    if s == 0:
        return cur
    n = cur.shape[0]
    r = pltpu.roll(cur, n - s, 0)
    last = jnp.where(lax.broadcasted_iota(I32, (8, cur.shape[1]), 0) >= 8 - s, pltpu.roll(next8, 8 - s, 0), r[n - 8:])
    return jnp.concatenate([r[:n - 8], last], axis=0)


def _conv_fwd(m, conv_w):
    S = m.shape[0]
    tm = min(512, S)
    hb = tm // 8

    def body(x_ref, p_ref, w_ref, o_ref):
        i = pl.program_id(0)
        cur = x_ref[...]
        prev = jnp.where(i > 0, p_ref[...], 0.0)
        w = w_ref[...]
        acc = cur * w[3:4]
        for jk in range(3):
            acc = acc + _shift_down(cur, prev, 3 - jk) * w[jk:jk + 1]
        o_ref[...] = acc

    return pl.pallas_call(
        body, name="conv_fwd", grid=(S // tm,),
        in_specs=[pl.BlockSpec((tm, CW), lambda i: (i, 1)), pl.BlockSpec((8, CW), lambda i: (jnp.maximum(i * hb - 1, 0), 1)),
                  pl.BlockSpec((4, CW), lambda i: (0, 0))],
        out_specs=pl.BlockSpec((tm, CW), lambda i: (i, 0)),
        out_shape=jax.ShapeDtypeStruct((S, CW), F32),
    )(m, m, conv_w)


def _conv_bwd(m, dyc, conv_w):
    S = m.shape[0]
    tm = min(512, S)
    hb = tm // 8
    nt = S // tm

    def body(x_ref, p_ref, d_ref, n_ref, w_ref, dx_ref, dw_ref):
        i = pl.program_id(0)
        cur = x_ref[...]
        prev = jnp.where(i > 0, p_ref[...], 0.0)
        d = d_ref[...]
        nxt = jnp.where(i < nt - 1, n_ref[...], 0.0)
        w = w_ref[...]
        acc = d * w[3:4]
        dws = []
        for jk in range(3):
            acc = acc + _shift_up(d, nxt, 3 - jk) * w[jk:jk + 1]
            dws.append(jnp.sum(d * _shift_down(cur, prev, 3 - jk), axis=0, keepdims=True))
        dws.append(jnp.sum(d * cur, axis=0, keepdims=True))
        dx_ref[...] = acc.astype(dx_ref.dtype)
        dw = jnp.concatenate(dws, axis=0)

        @pl.when(i == 0)
        def _():
            dw_ref[...] = dw

        @pl.when(i > 0)
        def _():
            dw_ref[...] += dw

    return pl.pallas_call(
        body, name="conv_bwd", grid=(nt,),
        in_specs=[pl.BlockSpec((tm, CW), lambda i: (i, 1)), pl.BlockSpec((8, CW), lambda i: (jnp.maximum(i * hb - 1, 0), 1)),
                  pl.BlockSpec((tm, CW), lambda i: (i, 0)),
                  pl.BlockSpec((8, CW), lambda i: (jnp.minimum((i + 1) * hb, S // 8 - 1), 0)),
                  pl.BlockSpec((4, CW), lambda i: (0, 0))],
        out_specs=[pl.BlockSpec((tm, CW), lambda i: (i, 0)), pl.BlockSpec((4, CW), lambda i: (0, 0))],
        out_shape=[jax.ShapeDtypeStruct((S, CW), BF), jax.ShapeDtypeStruct((4, CW), F32)],
    )(m, m, dyc, dyc, conv_w)


def _t(a):
    return jnp.swapaxes(a, 1, 2)


def _bdot(a, b):
    return jnp.einsum("hik,hkj->hij", a, b, preferred_element_type=F32)


@jax.custom_vjp
def _mm1(a, b):
    return _bdot(a.astype(BF), b.astype(BF))


_mm1.defvjp(lambda a, b: (_mm1(a, b), (a, b)), lambda res, dc: (_mm1(dc, _t(res[1])), _mm1(_t(res[0]), dc)))


@jax.custom_vjp
def _mm3(a, b):
    (ah, al), (bh, bl) = _split(a), _split(b)
    return _bdot(jnp.concatenate([ah, ah, al], axis=-1), jnp.concatenate([bh, bl, bh], axis=-2))


_mm3.defvjp(lambda a, b: (_mm3(a, b), (a, b)), lambda res, dc: (_mm3(dc, _t(res[1])), _mm3(_t(res[0]), dc)))


def _mm_exact(c3, b):
    hi, lo = _split(b)
    lo2 = (b - hi.astype(F32) - lo.astype(F32)).astype(BF)
    return _bdot(c3, jnp.concatenate([hi, lo, lo2], axis=-2))


@jax.custom_vjp
def _cumsum_rows(b):
    return _mm_exact(_tri3(True), b)


def _tri3(lower):
    ri = lax.broadcasted_iota(I32, (HEADS, CH, CH), 1)
    ci = lax.broadcasted_iota(I32, (HEADS, CH, CH), 2)
    tri = (ri >= ci if lower else ri <= ci).astype(BF)
    return jnp.concatenate([tri, tri, tri], axis=-1)


_cumsum_rows.defvjp(lambda b: (_cumsum_rows(b), None), lambda _, dc: (_mm_exact(_tri3(False), dc),))


def _gdn_chunk(state, yc, gz, gba, alog, dtb, normw):
    def heads(t, off):
        return jnp.stack([t[:, off + h * HD: off + (h + 1) * HD] for h in range(HEADS)])

    def cols(t, off):
        return jnp.stack([jnp.broadcast_to(t[:, off + h: off + h + 1], (CH, CH)) for h in range(HEADS)])

    c = yc * _sigmoid(yc)
    q, k, v, zg = heads(c, 0), heads(c, HEADS * HD), heads(c, 2 * HEADS * HD), heads(gz, 0)
    q = q * lax.rsqrt(jnp.sum(q * q, axis=-1, keepdims=True) + RMS_EPS) * (HD ** -0.5)
    k = k * lax.rsqrt(jnp.sum(k * k, axis=-1, keepdims=True) + RMS_EPS)
    beta = cols(_sigmoid(gba), 0)
    g = cols(-jnp.exp(alog) * _softplus(gba + dtb), HEADS)
    ri = lax.broadcasted_iota(I32, (HEADS, CH, CH), 1)
    ci = lax.broadcasted_iota(I32, (HEADS, CH, CH), 2)
    causal, strict = ri >= ci, ri > ci
    eye = (ri == ci).astype(F32)
    gc = _cumsum_rows(g)
    gr = _t(gc)
    decay = jnp.where(causal, jnp.exp(jnp.where(causal, gc - gr, 0.0)), 0.0)
    kt = _t(k)
    lower = jnp.where(strict, beta * _mm1(k, kt) * decay, 0.0)
    pw = -lower
    inv = eye + pw
    pw = _mm3(pw, pw)
    for _ in range(4):
        both = _mm3(jnp.concatenate([inv, pw], axis=1), pw)
        inv, pw = inv + both[:, :CH], both[:, CH:]
    inv = inv + _mm3(inv, pw)
    eg = jnp.exp(gc)
    uw = _mm3(inv, jnp.concatenate([v * beta, k * (beta * eg)], axis=2))
    u, w = uw[:, :, :HD], uw[:, :, HD:]
    qk = jnp.where(causal, _mm1(q, kt) * decay, 0.0)
    g_last = gc[:, CH - 1:CH, :]
    ws = _mm1(jnp.concatenate([w, q * eg], axis=1), state)
    v_new = u - ws[:, :CH]
    o = ws[:, CH:] + _mm1(qk, v_new)
    new_state = state * jnp.exp(g_last) + _mm1(_t(k * jnp.exp(g_last - gc)), v_new)
    o = o * lax.rsqrt(jnp.mean(o * o, axis=-1, keepdims=True) + RMS_EPS) * normw
    o = o * (zg * _sigmoid(zg))
    return jnp.concatenate([o[h] for h in range(HEADS)], axis=1), new_state


def _gdn_fwd(yc, m, alog, dtb, normw):
    S = yc.shape[0]
    nch = S // CH

    def body(y_ref, gz_ref, gba_ref, al_ref, dt_ref, nw_ref, o_ref, st_ref, st):
        @pl.when(pl.program_id(0) == 0)
        def _():
            st[...] = jnp.zeros_like(st)

        cur = st[...]
        st_ref[0] = cur
        o, new = _gdn_chunk(cur, y_ref[...], gz_ref[...], gba_ref[...], al_ref[...], dt_ref[...], nw_ref[...])
        o_ref[...] = o
        st[...] = new

    return pl.pallas_call(
        body, name="gdn_fwd", grid=(nch,),
        in_specs=[pl.BlockSpec((CH, CW), lambda n: (n, 0)), pl.BlockSpec((CH, HEADS * HD), lambda n: (n, 6)),
                  pl.BlockSpec((CH, 128), lambda n: (n, 28)), pl.BlockSpec((1, 128), lambda n: (0, 0)),
                  pl.BlockSpec((1, 128), lambda n: (0, 0)), pl.BlockSpec((1, HD), lambda n: (0, 0))],
        out_specs=[pl.BlockSpec((CH, HEADS * HD), lambda n: (n, 0)), pl.BlockSpec((1, HEADS, HD, HD), lambda n: (n, 0, 0, 0))],
        out_shape=[jax.ShapeDtypeStruct((S, HEADS * HD), F32), jax.ShapeDtypeStruct((nch, HEADS, HD, HD), F32)],
        scratch_shapes=[pltpu.VMEM((HEADS, HD, HD), F32)],
    )(yc, m, m, alog, dtb, normw)


def _gdn_bwd(yc, m, alog, dtb, normw, states, dog):
    S = yc.shape[0]
    nch = S // CH

    def body(y_ref, gz_ref, gba_ref, al_ref, dt_ref, nw_ref, st_ref, do_ref, dy_ref, dgz_ref, dgba_ref, dal_ref, ddt_ref, dnw_ref, dst):
        n = pl.program_id(0)

        @pl.when(n == 0)
        def _():
            dst[...] = jnp.zeros_like(dst)

        _, vjp = jax.vjp(_gdn_chunk, st_ref[0], y_ref[...], gz_ref[...], gba_ref[...], al_ref[...], dt_ref[...], nw_ref[...])
        d_state, d_y, d_gz, d_gba, d_al, d_dt, d_nw = vjp((do_ref[...], dst[...]))
        dst[...] = d_state
        dy_ref[...] = d_y
        dgz_ref[...] = d_gz.astype(dgz_ref.dtype)
        dgba_ref[...] = d_gba.astype(dgba_ref.dtype)
        for ref, val in ((dal_ref, d_al), (ddt_ref, d_dt), (dnw_ref, d_nw)):
            @pl.when(n == 0)
            def _(ref=ref, val=val):
                ref[...] = val

            @pl.when(n > 0)
            def _(ref=ref, val=val):
                ref[...] += val

    rev = lambda n: nch - 1 - n
    return pl.pallas_call(
        body, name="gdn_bwd", grid=(nch,),
        in_specs=[pl.BlockSpec((CH, CW), lambda n: (rev(n), 0)), pl.BlockSpec((CH, HEADS * HD), lambda n: (rev(n), 6)),
                  pl.BlockSpec((CH, 128), lambda n: (rev(n), 28)), pl.BlockSpec((1, 128), lambda n: (0, 0)),
                  pl.BlockSpec((1, 128), lambda n: (0, 0)), pl.BlockSpec((1, HD), lambda n: (0, 0)),
                  pl.BlockSpec((1, HEADS, HD, HD), lambda n: (rev(n), 0, 0, 0)),
                  pl.BlockSpec((CH, HEADS * HD), lambda n: (rev(n), 0))],
        out_specs=[pl.BlockSpec((CH, CW), lambda n: (rev(n), 0)), pl.BlockSpec((CH, HEADS * HD), lambda n: (rev(n), 0)),
                   pl.BlockSpec((CH, 128), lambda n: (rev(n), 0)), pl.BlockSpec((1, 128), lambda n: (0, 0)),
                   pl.BlockSpec((1, 128), lambda n: (0, 0)), pl.BlockSpec((1, HD), lambda n: (0, 0))],
        out_shape=[jax.ShapeDtypeStruct((S, CW), F32), jax.ShapeDtypeStruct((S, HEADS * HD), BF),
                   jax.ShapeDtypeStruct((S, 128), BF), jax.ShapeDtypeStruct((1, 128), F32),
                   jax.ShapeDtypeStruct((1, 128), F32), jax.ShapeDtypeStruct((1, HD), F32)],
        scratch_shapes=[pltpu.VMEM((HEADS, HD, HD), F32)],
    )(yc, m, m, alog, dtb, normw, states, dog)


def _mixer_fwd(h1, wp, b_gate, conv_w, alog, dtb, normw, w_sb, w_gdn, w_mo, g, b):
    S = h1.shape[0]
    tm = min(512, S)
    n = HEADS * HD
    m, = _mm("mix_in", h1, wp, S, NP, D, tm=tm, tn=1536, tk=D, order="ji", outs=[((S, NP), F32, (tm, 1536), _tile)])
    kb_all, vb_all, kt_all, vt_all = _attn_prep(m)
    ya, runs = _attn_fwd_t(m, kb_all, vt_all)
    runs = (kb_all, vb_all, kt_all, runs)
    yc = _conv_fwd(m, conv_w)
    og, states = _gdn_fwd(yc, m, alog, dtb, normw)
    ysb, = _mm("mix_sb", ya, w_sb, S, D, n, tm=tm, tn=D, tk=n, outs=[((S, D), F32, (tm, D), _row)])

    def merge_epi(acc, ys, gs, gg, bg):
        return _sigmoid(gs + bg[:, :D]) * ys + _sigmoid(gg + bg[:, D:]) * acc, acc

    u, ygdn = _mm("mix_gdn", og, w_gdn, S, D, n, tm=tm, tn=D, tk=n,
                  extras=[(ysb, (tm, D), _row), (m, (tm, D), lambda i, j, k: (i, GATE0 // D)),
                          (m, (tm, D), lambda i, j, k: (i, GATE0 // D + 1)), (b_gate, (1, 2 * D), _const)],
                  epilogue=merge_epi, outs=[((S, D), BF, (tm, D), _row), ((S, D), F32, (tm, D), _row)])

    def out_epi(acc, xin, gg, bb):
        r = ALPHA * xin + acc
        return _layer_norm(r, gg, bb), r

    h2, r2 = _mm("mix_out", u, w_mo, S, D, D, tm=tm, tn=D, tk=D,
                 extras=[(h1, (tm, D), _row), (g, (1, D), _const), (b, (1, D), _const)], epilogue=out_epi,
                 outs=[((S, D), F32, (tm, D), _row), ((S, D), F32, (tm, D), _row)])
    return h2, (h1, m, ya, runs, yc, og, states, ysb, ygdn, u, r2)


def _mixer_bwd(saved, wp, b_gate, conv_w, alog, dtb, normw, w_sb, w_gdn, w_mo, g, dh):
    h1, m, ya, runs, yc, og, states, ysb, ygdn, u, r2 = saved
    S = h1.shape[0]
    tm = min(512, S)
    n = HEADS * HD
    dr, dg, db = _ln_bwd("mix", r2, g, dh)

    def merge_epi(du, ys, yg, gs, gg, bg):
        s1, s2 = _sigmoid(gs + bg[:, :D]), _sigmoid(gg + bg[:, D:])
        dgate = jnp.concatenate([du * ys * s1 * (1.0 - s1), du * yg * s2 * (1.0 - s2)], axis=1)
        return du * s1, du * s2, dgate, jnp.sum(dgate, axis=0, keepdims=True)

    dysb, dygdn, dgate, d_bg = _mm(
        "mix_dmerge", dr, w_mo, S, D, D, tm=tm, tn=D, tk=D, tb=True,
        extras=[(ysb, (tm, D), _row), (ygdn, (tm, D), _row), (m, (tm, D), lambda i, j, k: (i, GATE0 // D)),
                (m, (tm, D), lambda i, j, k: (i, GATE0 // D + 1)), (b_gate, (1, 2 * D), _const)],
        epilogue=merge_epi, n_acc=1,
        outs=[((S, D), BF, (tm, D), _row), ((S, D), BF, (tm, D), _row), ((S, 2 * D), BF, (tm, 2 * D), _row),
              ((1, 2 * D), F32, (1, 2 * D), _const)])
    d_w_mo, = _mm("mix_dwmo", u, dr, D, D, S, tm=D, tn=D, tk=tm, ta=True, outs=[((D, D), F32, (D, D), _tile)])
    dya, = _mm("mix_dya", dysb, w_sb, S, n, D, tm=tm, tn=n, tk=D, tb=True, outs=[((S, n), F32, (tm, n), _row)])
    col_shards = [((NSH, n, D // NSH), F32, (None, n, D // NSH), lambda i, j, k: (j, 0, 0))]
    d_w_sb, = _mm("mix_dwsb", ya, dysb, n, D, S, tm=n, tn=D // NSH, tk=tm, ta=True, order="ji", outs=col_shards)
    dog, = _mm("mix_dog", dygdn, w_gdn, S, n, D, tm=tm, tn=n, tk=D, tb=True, outs=[((S, n), F32, (tm, n), _row)])
    d_w_gdn, = _mm("mix_dwgdn", og, dygdn, n, D, S, tm=n, tn=D // NSH, tk=tm, ta=True, order="ji", outs=col_shards)
    dq, dk, dv = _attn_bwd_t(m, *runs, dya)
    dyc, dgz, dgba, d_alog, d_dtb, d_normw = _gdn_bwd(yc, m, alog, dtb, normw, states, dog)
    dxc, d_conv = _conv_bwd(m, dyc, conv_w)
    dm = jnp.concatenate([dq, dk.astype(BF), dv.astype(BF), dxc, dgz, dgba, jnp.zeros((S, GATE0 - 3712), BF), dgate], axis=1)
    d_h1, = _mm("mix_dh", dm, wp, S, D, NP, tm=tm, tn=D, tk=1536, tb=True,
                extras=[(dr, (tm, D), _row)], epilogue=lambda acc, d: (acc + ALPHA * d,),
                outs=[((S, D), F32, (tm, D), _row)])
    d_wp, = _mm("mix_dwp", h1, dm, D, NP, S, tm=D, tn=1536, tk=tm, ta=True, order="ji",
                outs=[((D, NP), F32, (D, 1536), _tile)])
    return d_h1, dict(wp=d_wp, b_gate=d_bg, conv_w=d_conv, alog=d_alog, dtb=d_dtb, normw=d_normw,
                      w_sb=d_w_sb, w_gdn=d_w_gdn, w_mo=d_w_mo, g=dg, b=db)


def _ple_fwd(h3, p, w_pg, b_pg, w_ple, g, b, target):
    S = h3.shape[0]
    tm = min(512, S)
    pd = p.shape[1]
    pe, = _mm("ple_emb", p, w_ple, S, D, pd, tm=tm, tn=D, tk=pd, outs=[((S, D), F32, (tm, D), _row)])

    def epi(acc, e, xin, tgt, bp, gg, bb):
        gt = _sigmoid(acc + bp)
        r = ALPHA * xin + gt * e
        diff = _layer_norm(r, gg, bb) - tgt
        return gt, r, diff * (1.0 / D), jnp.sum(diff * diff, axis=0, keepdims=True)

    gt, r4, dh4, loss_row = _mm(
        "ple_gate", h3, w_pg, S, D, D, tm=tm, tn=D, tk=D,
        extras=[(pe, (tm, D), _row), (h3, (tm, D), _row), (target, (tm, D), _row), (b_pg, (1, D), _const),
                (g, (1, D), _const), (b, (1, D), _const)], epilogue=epi, n_acc=1,
        outs=[((S, D), F32, (tm, D), _row), ((S, D), F32, (tm, D), _row), ((S, D), F32, (tm, D), _row),
              ((1, D), F32, (1, D), _const)])
    return dh4, loss_row, (h3, p, pe, gt, r4)


def _ple_bwd(saved, w_pg, g, dh4):
    h3, p, pe, gt, r4 = saved
    S = h3.shape[0]
    tm = min(512, S)
    pd = p.shape[1]

    def fn(r, gg, dh, e, t):
        dr, dg, db = _layer_norm_bwd(r, gg, dh)
        dpre = dr * e * t * (1.0 - t)
        return dr, dpre, dr * t, dg, db, jnp.sum(dpre, axis=0, keepdims=True)

    row, one = (lambda i: (i, 0)), (lambda i: (0, 0))
    dr, dpre, dpe, dg, db, d_bpg = _rows(
        "ple_lnbwd", fn, S // tm,
        [(r4, (tm, D), row), (g, (1, D), one), (dh4, (tm, D), row), (pe, (tm, D), row), (gt, (tm, D), row)],
        [((S, D), F32, (tm, D), row), ((S, D), BF, (tm, D), row), ((S, D), BF, (tm, D), row),
         ((1, D), F32, (1, D), one), ((1, D), F32, (1, D), one), ((1, D), F32, (1, D), one)], n_acc=3)
    d_w_pg, = _mm("ple_dwpg", h3, dpre, D, D, S, tm=D, tn=D, tk=tm, ta=True, outs=[((D, D), F32, (D, D), _tile)])
    d_w_ple, = _mm("ple_dwple", p, dpe, pd, D, S, tm=pd, tn=D // NSH, tk=tm, ta=True, order="ji",
                   outs=[((NSH, pd, D // NSH), F32, (None, pd, D // NSH), lambda i, j, k: (j, 0, 0))])
    d_h3, = _mm("ple_dh", dpre, w_pg, S, D, D, tm=tm, tn=D, tk=D, tb=True,
                extras=[(dr, (tm, D), _row)], epilogue=lambda acc, d: (acc + ALPHA * d,),
                outs=[((S, D), F32, (tm, D), _row)])
    return d_h3, d_w_pg, d_bpg, d_w_ple, dg, db


def _local_step(x, p, target, W):
    h1, sv1 = _ffn_fwd("ffn1", x, W["ffn1_in"], W["ffn1_out"], W["ln1_g"], W["ln1_b"])
    h2, sv2 = _mixer_fwd(h1, W["wp"], W["b_gate"], W["conv_w"], W["alog"], W["dtb"], W["normw"],
                         W["w_sb"], W["w_gdn"], W["w_mo"], W["ln2_g"], W["ln2_b"])
    h3, sv3 = _ffn_fwd("ffn2", h2, W["ffn2_in"], W["ffn2_out"], W["ln3_g"], W["ln3_b"])
    dh4, loss_row, sv4 = _ple_fwd(h3, p, W["w_pg"], W["b_pg"], W["w_ple"], W["ln4_g"], W["ln4_b"], target)
    G = {}
    dh3, G["w_pg"], G["b_pg"], G["w_ple"], G["ln4_g"], G["ln4_b"] = _ple_bwd(sv4, W["w_pg"], W["ln4_g"], dh4)
    dh2, G["ffn2_in"], G["ffn2_out"], G["ln3_g"], G["ln3_b"] = _ffn_bwd("ffn2", sv3, W["ffn2_in"], W["ffn2_out"], W["ln3_g"], dh3)
    dh1, gm = _mixer_bwd(sv2, W["wp"], W["b_gate"], W["conv_w"], W["alog"], W["dtb"], W["normw"],
                         W["w_sb"], W["w_gdn"], W["w_mo"], W["ln2_g"], dh2)
    G.update({k: v for k, v in gm.items() if k not in ("g", "b")})
    G["ln2_g"], G["ln2_b"] = gm["g"], gm["b"]
    dx, G["ffn1_in"], G["ffn1_out"], G["ln1_g"], G["ln1_b"] = _ffn_bwd("ffn1", sv1, W["ffn1_in"], W["ffn1_out"], W["ln1_g"], dh1)
    return loss_row, dx, G


S2 = CUT - 2 * MSH


def _pack_wp(w4):
    tr = 256

    def fn(w):
        s = [w[j].astype(F32) for j in range(NSH)]
        full = jnp.concatenate([s[0], s[1], s[2][:, :S2], jnp.zeros((tr, GATE0 - CUT), F32), s[2][:, S2:], s[3]], axis=1)
        return (full,)

    return _rows("pack_wp", fn, D // tr, [(w4, (NSH, tr, MSH), lambda i: (0, i, 0))],
                 [((D, NP), BF, (tr, NP), lambda i: (i, 0))])[0]


def _unpack_wp(d):
    tr = 256
    g2 = GATE0 + MSH - S2

    def fn(v):
        return (jnp.stack([v[:, :MSH], v[:, MSH:2 * MSH], jnp.concatenate([v[:, 2 * MSH:CUT], v[:, GATE0:g2]], axis=1), v[:, g2:]]),)

    return _rows("unpack_wp", fn, D // tr, [(d, (tr, NP), lambda i: (i, 0))],
                 [((NSH, D, MSH), F32, (NSH, tr, MSH), lambda i: (0, i, 0))])[0]


def _cast_bf16(tag, w):
    r, c = w.shape
    tr = _pick(r, 256)
    return _rows(f"cast_{tag}", lambda v: (v,), r // tr, [(w, (tr, c), lambda i: (i, 0))],
                 [((r, c), BF, (tr, c), lambda i: (i, 0))])[0]


ANY = pl.BlockSpec(memory_space=pl.ANY)


def _place():
    return lax.axis_index("x"), lax.axis_index("y"), lax.axis_index("c")


def _gather_shards(shards):
    n = len(shards)

    def body(*refs):
        ins, outs = refs[:n], refs[n:2 * n]
        send, recv, loc = refs[2 * n:]
        x, y, c = _place()
        me = 2 * x + y
        chips = [(1 - x, y), (x, 1 - y), (1 - x, 1 - y)]
        started = []
        for t in range(n):
            lc = pltpu.make_async_copy(ins[t], outs[t].at[me], loc.at[t])
            lc.start()
            started.append(lc)
            for q, (px, py) in enumerate(chips):
                cp = pltpu.make_async_remote_copy(src_ref=ins[t], dst_ref=outs[t].at[me], send_sem=send.at[3 * t + q],
                                                  recv_sem=recv.at[3 * t + q], device_id=(px, py, c), device_id_type=MESH)
                cp.start()
                started.append(cp)
        for t in range(n):
            for q, (px, py) in enumerate(chips):
                pltpu.make_async_remote_copy(src_ref=ins[t], dst_ref=outs[t].at[2 * px + py], send_sem=send.at[3 * t + q],
                                             recv_sem=recv.at[3 * t + q], device_id=(px, py, c), device_id_type=MESH).wait_recv()
        for t in range(n):
            started[4 * t].wait()
            for q in range(3):
                started[4 * t + 1 + q].wait_send()

    return pl.pallas_call(
        body, name="gather_weights", in_specs=[ANY] * n, out_specs=[ANY] * n,
        out_shape=[jax.ShapeDtypeStruct((NSH,) + s.shape, s.dtype) for s in shards],
        scratch_shapes=[pltpu.SemaphoreType.DMA((3 * n,)), pltpu.SemaphoreType.DMA((3 * n,)), pltpu.SemaphoreType.DMA((n,))],
    )(*shards)


def _reduce_sibling(gs):
    n = len(gs)

    def body(*refs):
        ins, mine, theirs = refs[:n], refs[n:2 * n], refs[2 * n:3 * n]
        send, recv, loc = refs[3 * n:]
        x, y, c = _place()
        started = []
        for t in range(n):
            h = gs[t].shape[1] // 2
            lc = pltpu.make_async_copy(ins[t].at[:, pl.ds(pl.multiple_of(c * h, 8), h), :], mine[t], loc.at[t])
            cp = pltpu.make_async_remote_copy(src_ref=ins[t].at[:, pl.ds(pl.multiple_of((1 - c) * h, 8), h), :], dst_ref=theirs[t],
                                              send_sem=send.at[t], recv_sem=recv.at[t], device_id=(x, y, 1 - c), device_id_type=MESH)
            lc.start()
            cp.start()
            started += [lc, cp]
        for t in range(n):
            started[2 * t].wait()
            started[2 * t + 1].wait()

    half = [jax.ShapeDtypeStruct((NSH, g.shape[1] // 2, g.shape[2]), g.dtype) for g in gs]
    res = pl.pallas_call(
        body, name="reduce_sibling", in_specs=[ANY] * n, out_specs=[ANY] * (2 * n), out_shape=half + half,
        scratch_shapes=[pltpu.SemaphoreType.DMA((n,)), pltpu.SemaphoreType.DMA((n,)), pltpu.SemaphoreType.DMA((n,))],
    )(*gs)
    return res[:n], res[n:]


def _reduce_chips(ps):
    n = len(ps)

    def body(*refs):
        ins, outs = refs[:n], refs[n:2 * n]
        send, recv, loc = refs[2 * n:]
        x, y, c = _place()
        me = 2 * x + y
        chips = [(1 - x, y), (x, 1 - y), (1 - x, 1 - y)]
        started = []
        for t in range(n):
            lc = pltpu.make_async_copy(ins[t].at[me], outs[t].at[me], loc.at[t])
            lc.start()
            started.append(lc)
            for q, (px, py) in enumerate(chips):
                cp = pltpu.make_async_remote_copy(src_ref=ins[t].at[2 * px + py], dst_ref=outs[t].at[me], send_sem=send.at[3 * t + q],
                                                  recv_sem=recv.at[3 * t + q], device_id=(px, py, c), device_id_type=MESH)
                cp.start()
                started.append(cp)
        for t in range(n):
            for q, (px, py) in enumerate(chips):
                pltpu.make_async_remote_copy(src_ref=ins[t].at[me], dst_ref=outs[t].at[2 * px + py], send_sem=send.at[3 * t + q],
                                             recv_sem=recv.at[3 * t + q], device_id=(px, py, c), device_id_type=MESH).wait_recv()
        for t in range(n):
            started[4 * t].wait()
            for q in range(3):
                started[4 * t + 1 + q].wait_send()

    return pl.pallas_call(
        body, name="reduce_chips", in_specs=[ANY] * n, out_specs=[ANY] * n,
        out_shape=[jax.ShapeDtypeStruct(p_.shape, p_.dtype) for p_ in ps],
        scratch_shapes=[pltpu.SemaphoreType.DMA((3 * n,)), pltpu.SemaphoreType.DMA((3 * n,)), pltpu.SemaphoreType.DMA((n,))],
    )(*ps)


def _share_sibling(fs):
    n = len(fs)

    def body(*refs):
        ins, outs = refs[:n], refs[n:2 * n]
        send, recv, loc = refs[2 * n:]
        x, y, c = _place()
        started = []
        for t in range(n):
            lc = pltpu.make_async_copy(ins[t], outs[t].at[c], loc.at[t])
            cp = pltpu.make_async_remote_copy(src_ref=ins[t], dst_ref=outs[t].at[c], send_sem=send.at[t], recv_sem=recv.at[t],
                                              device_id=(x, y, 1 - c), device_id_type=MESH)
            lc.start()
            cp.start()
            started += [lc, cp]
        for t in range(n):
            started[2 * t].wait()
            started[2 * t + 1].wait_send()
            pltpu.make_async_remote_copy(src_ref=ins[t], dst_ref=outs[t].at[1 - c], send_sem=send.at[t], recv_sem=recv.at[t],
                                         device_id=(x, y, 1 - c), device_id_type=MESH).wait_recv()

    return pl.pallas_call(
        body, name="share_sibling", in_specs=[ANY] * n, out_specs=[ANY] * n,
        out_shape=[jax.ShapeDtypeStruct((2,) + f.shape, f.dtype) for f in fs],
        scratch_shapes=[pltpu.SemaphoreType.DMA((n,)), pltpu.SemaphoreType.DMA((n,)), pltpu.SemaphoreType.DMA((n,))],
    )(*fs)


def _chunk_rows(h, c):
    return _pick(h, max(8, 262144 // c // 8 * 8))


def _sibling_sum(tag, g):
    _, r, c = g.shape
    h = r // 2
    tr = _chunk_rows(h, c)
    nch = h // tr
    steps = NSH * nch

    def body(top_ref, bot_ref, out_ref, land, send, recv):
        s = pl.program_id(0)
        x, y, core = _place()

        def exchange(keep_ref, give_ref):
            cp = pltpu.make_async_remote_copy(src_ref=give_ref.at[0], dst_ref=land.at[s], send_sem=send.at[s], recv_sem=recv.at[s],
                                              device_id=(x, y, 1 - core), device_id_type=MESH)
            cp.start()
            cp.wait()
            out_ref[0] = keep_ref[0] + land[s]

        @pl.when(core == 0)
        def _():
            exchange(top_ref, bot_ref)

        @pl.when(core == 1)
        def _():
            exchange(bot_ref, top_ref)

    return pl.pallas_call(
        body, name=f"sibling_sum_{tag}", grid=(steps,),
        in_specs=[pl.BlockSpec((1, tr, c), lambda s: (s // nch, s % nch, 0)),
                  pl.BlockSpec((1, tr, c), lambda s: (s // nch, nch + s % nch, 0))],
        out_specs=pl.BlockSpec((1, tr, c), lambda s: (s // nch, s % nch, 0)),
        out_shape=jax.ShapeDtypeStruct((NSH, h, c), F32),
        scratch_shapes=[pltpu.VMEM((steps, tr, c), F32), pltpu.SemaphoreType.DMA((steps,)), pltpu.SemaphoreType.DMA((steps,))],
    )(g, g)


def _chip_sum_share(tag, b):
    _, h, c = b.shape
    tr = _chunk_rows(h, c)
    steps = h // tr

    def body(b_ref, out_ref, stage, land, send, recv):
        s = pl.program_id(0)
        x, y, core = _place()
        v = b_ref[...]
        total = ((v[0] + v[1]) + v[2]) + v[3]
        stage[...] = total
        cp = pltpu.make_async_remote_copy(src_ref=stage, dst_ref=land.at[s], send_sem=send.at[s], recv_sem=recv.at[s],
                                          device_id=(x, y, 1 - core), device_id_type=MESH)
        cp.start()
        cp.wait()
        out_ref[core] = total
        out_ref[1 - core] = land[s]

    return pl.pallas_call(
        body, name=f"chip_sum_share_{tag}", grid=(steps,),
        in_specs=[pl.BlockSpec((NSH, tr, c), lambda s: (0, s, 0))],
        out_specs=pl.BlockSpec((2, tr, c), lambda s: (0, s, 0)),
        out_shape=jax.ShapeDtypeStruct((2, h, c), F32),
        scratch_shapes=[pltpu.VMEM((tr, c), F32), pltpu.VMEM((steps, tr, c), F32), pltpu.SemaphoreType.DMA((steps,)),
                        pltpu.SemaphoreType.DMA((steps,))],
    )(b)


NDEV = 8


def _allreduce_small(pack):
    r, w = pack.shape
    rel = [(dx, dy, dc) for dx in (0, 1) for dy in (0, 1) for dc in (0, 1) if (dx, dy, dc) != (0, 0, 0)]

    def body(in_ref, out_ref, buf, send, recv):
        x, y, c = _place()
        me = 4 * x + 2 * y + c
        buf[me] = in_ref[...]
        peers = [((x + dx) % 2, (y + dy) % 2, (c + dc) % 2) for dx, dy, dc in rel]
        sent = []
        for k, peer in enumerate(peers):
            cp = pltpu.make_async_remote_copy(src_ref=in_ref, dst_ref=buf.at[me], send_sem=send.at[k], recv_sem=recv.at[k],
                                              device_id=peer, device_id_type=MESH)
            cp.start()
            sent.append(cp)
        for k, (px, py, pc) in enumerate(peers):
            pltpu.make_async_remote_copy(src_ref=in_ref, dst_ref=buf.at[4 * px + 2 * py + pc], send_sem=send.at[k], recv_sem=recv.at[k],
                                         device_id=(px, py, pc), device_id_type=MESH).wait_recv()
        for cp in sent:
            cp.wait_send()
        acc = buf[0]
        for k in range(1, NDEV):
            acc = acc + buf[k]
        out_ref[...] = acc

    vm = pl.BlockSpec(memory_space=pltpu.VMEM)
    return pl.pallas_call(
        body, name="allreduce_small", in_specs=[vm], out_specs=vm, out_shape=jax.ShapeDtypeStruct((r, w), F32),
        scratch_shapes=[pltpu.VMEM((NDEV, r, w), F32), pltpu.SemaphoreType.DMA((NDEV - 1,)), pltpu.SemaphoreType.DMA((NDEV - 1,))],
    )(pack)


def _add2(tag, a, b):
    _, h, c = a.shape
    tr = _pick(h, max(8, 262144 // c // 8 * 8))
    spec = ((None, tr, c), lambda i: (i // (h // tr), i % (h // tr), 0))
    return _rows(f"add2_{tag}", lambda u, v: (u + v,), NSH * (h // tr), [(a,) + spec, (b,) + spec], [(a.shape, F32) + spec])[0]


def _add4(tag, a):
    _, h, c = a.shape
    tr = _pick(h, max(8, 131072 // c // 8 * 8))
    return _rows(f"add4_{tag}", lambda v: (((v[0] + v[1]) + v[2]) + v[3],), h // tr, [(a, (NSH, tr, c), lambda i: (0, i, 0))],
                 [((h, c), F32, (tr, c), lambda i: (i, 0))])[0]


def _adamw(tag, w, g, m, v):
    r, c = w.shape
    tr = _pick(r, max(8, 262144 // c // 8 * 8))

    def fn(w_, g_, m_, v_):
        m2 = B1 * m_ + (1.0 - B1) * g_
        v2 = B2 * v_ + (1.0 - B2) * (g_ * g_)
        m_hat = m2 / (1.0 - B1 ** STEP)
        v_hat = v2 / (1.0 - B2 ** STEP)
        return -LR * (m_hat / (jnp.sqrt(v_hat) + EPS) + WD * w_), m2, v2

    spec = ((tr, c), lambda i: (i, 0))
    return _rows(f"adamw_{tag}", fn, r // tr, [(a,) + spec for a in (w, g, m, v)], [((r, c), F32) + spec] * 3)


BIG = ("ffn1_w_in", "ffn1_w_out", "w_mix_in", "w_branch_sb", "w_branch_gdn", "w_mix_out", "ffn2_w_in", "ffn2_w_out",
       "w_ple_gate", "w_ple")
SMALL = ("ln1_g", "ln1_b", "b_gate", "conv_w", "a_log", "dt_bias", "gdn_norm_w", "ln2_g", "ln2_b", "ln3_g", "ln3_b",
         "b_ple_gate", "ln4_g", "ln4_b")
ORDER = ("ffn1_w_in", "ffn1_w_out", "ln1_g", "ln1_b", "w_mix_in", "b_gate", "conv_w", "a_log", "dt_bias", "gdn_norm_w",
         "w_branch_sb", "w_branch_gdn", "w_mix_out", "ln2_g", "ln2_b", "ffn2_w_in", "ffn2_w_out", "ln3_g", "ln3_b",
         "w_ple_gate", "b_ple_gate", "w_ple", "ln4_g", "ln4_b")
PACK_W = 2304


def _lane_row(v, lanes=128, at=HEADS):
    return jnp.pad(v[None, :], ((0, 0), (at, lanes - at - v.shape[0])))


def _col_join(w4):
    return jnp.transpose(w4, (1, 0, 2)).reshape(w4.shape[1], NSH * w4.shape[2])


def kernel(x, p, ffn1_w_in, ffn1_w_out, ln1_g, ln1_b, w_mix_in, b_gate, conv_w, a_log, dt_bias, gdn_norm_w, w_branch_sb, w_branch_gdn, w_mix_out, ln2_g, ln2_b, ffn2_w_in, ffn2_w_out, ln3_g, ln3_b, w_ple_gate, b_ple_gate, w_ple, ln4_g, ln4_b, loss_target, m_ffn1_w_in, m_ffn1_w_out, m_ln1_g, m_ln1_b, m_w_mix_in, m_b_gate, m_conv_w, m_a_log, m_dt_bias, m_gdn_norm_w, m_w_branch_sb, m_w_branch_gdn, m_w_mix_out, m_ln2_g, m_ln2_b, m_ffn2_w_in, m_ffn2_w_out, m_ln3_g, m_ln3_b, m_w_ple_gate, m_b_ple_gate, m_w_ple, m_ln4_g, m_ln4_b, v_ffn1_w_in, v_ffn1_w_out, v_ln1_g, v_ln1_b, v_w_mix_in, v_b_gate, v_conv_w, v_a_log, v_dt_bias, v_gdn_norm_w, v_w_branch_sb, v_w_branch_gdn, v_w_mix_out, v_ln2_g, v_ln2_b, v_ffn2_w_in, v_ffn2_w_out, v_ln3_g, v_ln3_b, v_w_ple_gate, v_b_ple_gate, v_w_ple, v_ln4_g, v_ln4_b):
    args = dict(locals())
    w = {n: args[n][0] for n in ORDER}
    mom = {n: args["m_" + n][0] for n in ORDER}
    var = {n: args["v_" + n][0] for n in ORDER}

    shards = [_cast_bf16(n, w[n]) for n in BIG] + [w["conv_w"]]
    full = dict(zip(BIG + ("conv_w",), _gather_shards(shards)))
    W = dict(
        ffn1_in=full["ffn1_w_in"], ffn1_out=full["ffn1_w_out"].reshape(DFF, D),
        ffn2_in=full["ffn2_w_in"], ffn2_out=full["ffn2_w_out"].reshape(DFF, D),
        wp=_pack_wp(full["w_mix_in"]), w_sb=_col_join(full["w_branch_sb"]), w_gdn=_col_join(full["w_branch_gdn"]),
        w_mo=full["w_mix_out"].reshape(D, D), w_pg=full["w_ple_gate"].reshape(D, D), w_ple=_col_join(full["w_ple"]),
        conv_w=_col_join(full["conv_w"]), b_gate=w["b_gate"][None], alog=_lane_row(w["a_log"]), dtb=_lane_row(w["dt_bias"]),
        normw=w["gdn_norm_w"][None], b_pg=w["b_ple_gate"][None],
        **{f"ln{i}_{s}": w[f"ln{i}_{s}"][None] for i in (1, 2, 3, 4) for s in ("g", "b")},
    )

    loss_row, grad_x, G = _local_step(x[0], p[0, 0], loss_target[0], W)
    loss = lax.psum(0.5 * jnp.sum(loss_row) / D, ("x", "y", "c"))

    big = dict(
        ffn1_w_in=G["ffn1_in"], ffn1_w_out=G["ffn1_out"].reshape(NSH, DFF // NSH, D), w_mix_in=_unpack_wp(G["wp"]),
        w_branch_sb=G["w_sb"], w_branch_gdn=G["w_gdn"], w_mix_out=G["w_mo"].reshape(NSH, D // NSH, D),
        ffn2_w_in=G["ffn2_in"], ffn2_w_out=G["ffn2_out"].reshape(NSH, DFF // NSH, D),
        w_ple_gate=G["w_pg"].reshape(NSH, D // NSH, D), w_ple=G["w_ple"],
    )
    chip_sums = _reduce_chips([_sibling_sum(n, big[n]) for n in BIG])
    grad = {n: _chip_sum_share(n, a).reshape(w[n].shape) for n, a in zip(BIG, chip_sums)}

    pieces = [G["ln1_g"], G["ln1_b"], G["b_gate"], G["conv_w"].reshape(1, 4 * CW), G["alog"], G["dtb"], G["normw"],
              G["ln2_g"], G["ln2_b"], G["ln3_g"], G["ln3_b"], G["b_pg"], G["ln4_g"], G["ln4_b"]]
    flat = jnp.concatenate(pieces, axis=1)
    flat = jnp.pad(flat, ((0, 0), (0, NDEV * PACK_W - flat.shape[1])))
    total = _allreduce_small(flat.reshape(NDEV, PACK_W)).reshape(1, NDEV * PACK_W)
    off = 0
    for n, piece in zip(SMALL, pieces):
        grad[n] = total[0, off:off + piece.shape[1]]
        off += piece.shape[1]
    chip = 2 * lax.axis_index("x") + lax.axis_index("y")
    grad["conv_w"] = lax.dynamic_slice_in_dim(grad["conv_w"].reshape(4, CW), chip * (CW // NSH), CW // NSH, axis=1)
    grad["a_log"] = grad["a_log"][HEADS:2 * HEADS]
    grad["dt_bias"] = grad["dt_bias"][HEADS:2 * HEADS]

    delta, new_m, new_v = {}, {}, {}
    for n in ORDER:
        shape2 = w[n].shape if w[n].ndim == 2 else (1, w[n].shape[0])
        d_, m_, v_ = _adamw(n, *[a.reshape(shape2) for a in (w[n], grad[n], mom[n], var[n])])
        delta[n], new_m[n], new_v[n] = (a.reshape(args[n].shape) for a in (d_, m_, v_))
    outs = [loss, grad_x[None]]
    outs += [grad[n].reshape(args[n].shape) for n in ORDER]
    for group in (delta, new_m, new_v):
        outs += [group[n] for n in ORDER]
    return tuple(outs)
```

```python
import functools

import jax
import jax.numpy as jnp
from jax import lax
from jax.experimental import pallas as pl
from jax.experimental.pallas import tpu as pltpu

F32 = jnp.float32
BF = jnp.bfloat16
I32 = jnp.int32
HI = lax.Precision.HIGHEST
MESH = pl.DeviceIdType.MESH

D = 1024
DFF = 2816
NSH = 4
FSH = 2 * DFF // NSH
NIN = 5648
MSH = NIN // NSH
NP = 6144
CUT = 3600
GATE0 = 4096
HEADS = 8
HD = 64
CH = 64
KB = 128
ALPHA = 2.0 ** 0.25
LN_EPS = 1e-5
RMS_EPS = 1e-6
B1, B2, LR, EPS, WD, STEP = 0.9, 0.999, 0.001, 1e-08, 0.01, 10


def _sigmoid(x):
    return 0.5 * jnp.tanh(0.5 * x) + 0.5


def _softplus(x):
    return jnp.maximum(x, 0.0) + jnp.log1p(jnp.exp(-jnp.abs(x)))


def _layer_norm(r, g, b):
    mu = jnp.mean(r, axis=-1, keepdims=True)
    xc = r - mu
    var = jnp.mean(xc * xc, axis=-1, keepdims=True)
    return xc * lax.rsqrt(var + LN_EPS) * g + b


def _layer_norm_bwd(r, g, dh):
    mu = jnp.mean(r, axis=-1, keepdims=True)
    xc = r - mu
    var = jnp.mean(xc * xc, axis=-1, keepdims=True)
    xhat = xc * lax.rsqrt(var + LN_EPS)
    dxh = dh * g
    dr = lax.rsqrt(var + LN_EPS) * (dxh - jnp.mean(dxh, axis=-1, keepdims=True) - xhat * jnp.mean(dxh * xhat, axis=-1, keepdims=True))
    return dr, jnp.sum(dh * xhat, axis=0, keepdims=True), jnp.sum(dh, axis=0, keepdims=True)


def _pick(n, cap):
    if n <= cap:
        return n
    for t in range(cap - cap % 8, 7, -8):
        if n % t == 0:
            return t
    raise ValueError((n, cap))


def _mm(name, a, b, M, N, K, *, tm, tn, tk, ta=False, tb=False, a_spec=None, b_spec=None, order="ij",
        extras=(), epilogue=None, outs, n_acc=0):
    ni, nj, nk = M // tm, N // tn, K // tk
    assert M % tm == 0 and N % tn == 0 and K % tk == 0, (name, M, N, K, tm, tn, tk)
    assert n_acc == 0 or nj == 1

    def wrap(fn):
        if order == "ij":
            return lambda g0, g1, g2: fn(g0, g1, g2)
        return lambda g0, g1, g2: fn(g1, g0, g2)

    if a_spec is None:
        a_spec = ((tk, tm), lambda i, j, k: (k, i)) if ta else ((tm, tk), lambda i, j, k: (i, k))
    if b_spec is None:
        b_spec = ((tn, tk), lambda i, j, k: (j, k)) if tb else ((tk, tn), lambda i, j, k: (k, j))
    dims = (((0 if ta else 1,), (1 if tb else 0,)), ((), ()))
    ne, no = len(extras), len(outs)
    grid = (ni, nj, nk) if order == "ij" else (nj, ni, nk)

    def body(*refs):
        a_ref, b_ref = refs[0], refs[1]
        ex = refs[2:2 + ne]
        o = refs[2 + ne:2 + ne + no]
        g0, g1, k = pl.program_id(0), pl.program_id(1), pl.program_id(2)
        first = jnp.logical_and(g0 == 0, g1 == 0)
        p = lax.dot_general(a_ref[...].astype(BF), b_ref[...].astype(BF), dims, preferred_element_type=F32)

        def finish(acc):
            vals = (acc,) if epilogue is None else epilogue(acc, *[e[...] for e in ex])
            for idx, (ref, val) in enumerate(zip(o, vals)):
                if idx < no - n_acc:
                    ref[...] = val.astype(ref.dtype)
                else:
                    @pl.when(first)
                    def _(ref=ref, val=val):
                        ref[...] = val

                    @pl.when(jnp.logical_not(first))
                    def _(ref=ref, val=val):
                        ref[...] += val

        if nk == 1:
            finish(p)
        else:
            acc_ref = refs[-1]

            @pl.when(k == 0)
            def _():
                acc_ref[...] = p

            @pl.when(k > 0)
            def _():
                acc_ref[...] += p

            @pl.when(k == nk - 1)
            def _():
                finish(acc_ref[...])

    in_specs = [pl.BlockSpec(a_spec[0], wrap(a_spec[1])), pl.BlockSpec(b_spec[0], wrap(b_spec[1]))]
    in_specs += [pl.BlockSpec(blk, wrap(fn)) for _, blk, fn in extras]
    res = pl.pallas_call(
        body, name=name, grid=grid, in_specs=in_specs,
        out_specs=[pl.BlockSpec(blk, wrap(fn)) for _, _, blk, fn in outs],
        out_shape=[jax.ShapeDtypeStruct(shape, dt) for shape, dt, _, _ in outs],
        scratch_shapes=[pltpu.VMEM((tm, tn), F32)] if nk > 1 else [],
    )(a, b, *[e[0] for e in extras])
    return res


def _row(i, j, k):
    return (i, 0)


def _tile(i, j, k):
    return (i, j)


def _const(i, j, k):
    return (0, 0)


def _rows(name, fn, n_steps, ins, outs, n_acc=0):
    ni, no = len(ins), len(outs)

    def body(*refs):
        i = pl.program_id(0)
        vals = fn(*[r[...] for r in refs[:ni]])
        for idx, (ref, val) in enumerate(zip(refs[ni:ni + no], vals)):
            if idx < no - n_acc:
                ref[...] = val.astype(ref.dtype)
            else:
                @pl.when(i == 0)
                def _(ref=ref, val=val):
                    ref[...] = val

                @pl.when(i > 0)
                def _(ref=ref, val=val):
                    ref[...] += val

    return pl.pallas_call(
        body, name=name, grid=(n_steps,),
        in_specs=[pl.BlockSpec(blk, fn_) for _, blk, fn_ in ins],
        out_specs=[pl.BlockSpec(blk, fn_) for _, _, blk, fn_ in outs],
        out_shape=[jax.ShapeDtypeStruct(shape, dt) for shape, dt, _, _ in outs],
    )(*[a for a, _, _ in ins])


def _ffn_fwd(tag, x, w_in, w_out, g, b):
    S = x.shape[0]
    tm = min(512, S)
    gate, = _mm(f"{tag}_gate", x, w_in, S, DFF, D, tm=tm, tn=FSH, tk=D, order="ji",
                b_spec=((None, D, FSH), lambda i, j, k: (j, 0, 0)),
                outs=[((S, DFF), F32, (tm, FSH), _tile)])

    def up_epi(acc, gt):
        return acc, gt * _sigmoid(gt) * acc

    up, s = _mm(f"{tag}_up", x, w_in, S, DFF, D, tm=tm, tn=FSH, tk=D, order="ji",
                b_spec=((None, D, FSH), lambda i, j, k: (j + 2, 0, 0)),
                extras=[(gate, (tm, FSH), _tile)], epilogue=up_epi,
                outs=[((S, DFF), F32, (tm, FSH), _tile), ((S, DFF), BF, (tm, FSH), _tile)])

    def out_epi(acc, xin, gg, bb):
        r = ALPHA * xin + 0.5 * acc
        return _layer_norm(r, gg, bb), r

    h, r = _mm(f"{tag}_out", s, w_out, S, D, DFF, tm=tm, tn=D, tk=DFF,
               extras=[(x, (tm, D), _row), (g, (1, D), _const), (b, (1, D), _const)], epilogue=out_epi,
               outs=[((S, D), F32, (tm, D), _row), ((S, D), F32, (tm, D), _row)])
    return h, (x, gate, up, s, r)


def _ln_bwd(tag, r, g, dh):
    S = r.shape[0]
    tm = min(512, S)
    return _rows(f"{tag}_lnbwd", _layer_norm_bwd, S // tm,
                 [(r, (tm, D), lambda i: (i, 0)), (g, (1, D), lambda i: (0, 0)), (dh, (tm, D), lambda i: (i, 0))],
                 [((S, D), F32, (tm, D), lambda i: (i, 0)), ((1, D), F32, (1, D), lambda i: (0, 0)),
                  ((1, D), F32, (1, D), lambda i: (0, 0))], n_acc=2)


def _ffn_bwd(tag, saved, w_in, w_out, g, dh):
    x, gate, up, s, r = saved
    S = x.shape[0]
    tm = min(512, S)
    dr, dg, db = _ln_bwd(tag, r, g, dh)

    def act_epi(acc, gt, u):
        ds = 0.5 * acc
        sg = _sigmoid(gt)
        return (jnp.stack([ds * u * (sg * (1.0 + gt * (1.0 - sg))), ds * (gt * sg)]),)

    da, = _mm(f"{tag}_dact", dr, w_out, S, DFF, D, tm=tm, tn=FSH, tk=D, tb=True, order="ji",
              extras=[(gate, (tm, FSH), _tile), (up, (tm, FSH), _tile)], epilogue=act_epi,
              outs=[((2, S, DFF), BF, (2, tm, FSH), lambda i, j, k: (0, i, j))])
    d_w_out, = _mm(f"{tag}_dwout", s, dr, DFF, D, S, tm=FSH, tn=D, tk=tm, ta=True,
                   epilogue=lambda acc: (0.5 * acc,), outs=[((DFF, D), F32, (FSH, D), _tile)])
    tb_ = min(1024, S)
    d_in, = _mm(f"{tag}_dx", da, w_in, S, D, 2 * DFF, tm=tb_, tn=D, tk=FSH, tb=True,
                a_spec=((None, tb_, FSH), lambda i, j, k: (k // 2, i, k % 2)),
                b_spec=((None, D, FSH), lambda i, j, k: (k, 0, 0)),
                extras=[(dr, (tb_, D), _row)], epilogue=lambda acc, d: (acc + ALPHA * d,),
                outs=[((S, D), F32, (tb_, D), _row)])
    d_w_in, = _mm(f"{tag}_dwin", x, da, D, 2 * DFF, S, tm=D, tn=FSH, tk=tb_, ta=True, order="ji",
                  b_spec=((None, tb_, FSH), lambda i, j, k: (j // 2, k, j % 2)),
                  outs=[((NSH, D, FSH), F32, (None, D, FSH), lambda i, j, k: (j, 0, 0))])
    return d_in, d_w_in, d_w_out, dg, db


def _cum(vals, tri):
    hi = vals.astype(BF)
    lo = (vals - hi.astype(F32)).astype(BF)
    return jnp.dot(hi, tri, preferred_element_type=F32) + jnp.dot(lo, tri, preferred_element_type=F32)


def _nt(a, b):
    return lax.dot_general(a, b, (((1,), (1,)), ((), ())), preferred_element_type=F32)


def _tn(a, b):
    return lax.dot_general(a, b, (((0,), (0,)), ((), ())), preferred_element_type=F32)


def _attn_fwd(m):
    S = m.shape[0]
    tq = min(256, S)
    assert S // KB <= 128

    def body(q_ref, k_ref, v_ref, o_ref, r_ref, kb, vb):
        i = pl.program_id(1)

        @pl.when(i == 0)
        def _():
            kb[...] = k_ref[...].astype(BF)
            vb[...] = v_ref[...].astype(BF)

        qs = (q_ref[...] * 0.125).astype(BF)
        row = lax.broadcasted_iota(I32, (tq, KB), 0) + i * tq
        col = lax.broadcasted_iota(I32, (tq, KB), 1)
        tri = (lax.broadcasted_iota(I32, (KB, KB), 0) >= lax.broadcasted_iota(I32, (KB, KB), 1)).astype(BF)
        nb = (i + 1) * (tq // KB)
        r_ref[...] = jnp.zeros_like(r_ref)

        def step(t, carry):
            j = nb - 1 - t
            off = pl.multiple_of(j * KB, KB)
            mask = col + j * KB < row
            kblk = kb[pl.ds(off, KB), :]
            vblk = vb[pl.ds(off, KB), :]
            new = []
            for hh in range(2):
                run, acc = carry[2 * hh], carry[2 * hh + 1]
                sl = slice(hh * HD, (hh + 1) * HD)
                z = _nt(qs[:, sl], kblk[:, sl])
                cs = _cum(jnp.where(mask, _softplus(z), 0.0), tri)
                a = jnp.where(mask, jnp.exp(z - (run + cs)), 0.0)
                r_ref[hh] = jnp.where(col == j, run, r_ref[hh])
                new += [run + cs[:, :1], acc + jnp.dot(a.astype(BF), vblk[:, sl], preferred_element_type=F32)]
            return tuple(new)

        init = (jnp.zeros((tq, 1), F32), jnp.zeros((tq, HD), F32)) * 2
        res = lax.fori_loop(0, nb, step, init)
        o_ref[...] = jnp.concatenate([res[1], res[3]], axis=1)

    return pl.pallas_call(
        body, name="attn_fwd", grid=(HEADS // 2, S // tq),
        in_specs=[pl.BlockSpec((tq, KB), lambda h, i: (i, h)), pl.BlockSpec((S, KB), lambda h, i: (0, 4 + h)),
                  pl.BlockSpec((S, KB), lambda h, i: (0, 8 + h))],
        out_specs=[pl.BlockSpec((tq, KB), lambda h, i: (i, h)), pl.BlockSpec((2, tq, KB), lambda h, i: (h, i, 0))],
        out_shape=[jax.ShapeDtypeStruct((S, HEADS * HD), F32), jax.ShapeDtypeStruct((HEADS, S, KB), F32)],
        scratch_shapes=[pltpu.VMEM((S, KB), BF), pltpu.VMEM((S, KB), BF)],
    )(m, m, m)


def _attn_bwd(m, runs, dy):
    S = m.shape[0]
    tq = min(256, S)

    def body(q_ref, k_ref, v_ref, r_ref, dy_ref, dq_ref, dk_ref, dv_ref, kb, vb):
        i = pl.program_id(1)

        @pl.when(i == 0)
        def _():
            kb[...] = k_ref[...].astype(BF)
            vb[...] = v_ref[...].astype(BF)
            dk_ref[...] = jnp.zeros_like(dk_ref)
            dv_ref[...] = jnp.zeros_like(dv_ref)

        qs = (q_ref[...] * 0.125).astype(BF)
        dyb = dy_ref[...].astype(BF)
        rs = [r_ref[0], r_ref[1]]
        row = lax.broadcasted_iota(I32, (tq, KB), 0) + i * tq
        col = lax.broadcasted_iota(I32, (tq, KB), 1)
        jj = lax.broadcasted_iota(I32, (KB, KB), 0)
        ss = lax.broadcasted_iota(I32, (KB, KB), 1)
        tri_rev = (jj >= ss).astype(BF)
        tri_fwd = (jj <= ss).astype(BF)
        nb = (i + 1) * (tq // KB)

        def step(j, carry):
            off = pl.multiple_of(j * KB, KB)
            mask = col + j * KB < row
            kblk = kb[pl.ds(off, KB), :]
            vblk = vb[pl.ds(off, KB), :]
            new, dks, dvs = [], [], []
            for hh in range(2):
                pre, dq = carry[2 * hh], carry[2 * hh + 1]
                sl = slice(hh * HD, (hh + 1) * HD)
                run = jnp.sum(jnp.where(col == j, rs[hh], 0.0), axis=1, keepdims=True)
                z = _nt(qs[:, sl], kblk[:, sl])
                e = jnp.exp(-jnp.abs(z))
                sp = jnp.maximum(z, 0.0) + jnp.log1p(e)
                sig = jnp.where(z >= 0.0, 1.0, e) / (1.0 + e)
                cs = _cum(jnp.where(mask, sp, 0.0), tri_rev)
                a = jnp.where(mask, jnp.exp(z - (run + cs)), 0.0)
                gmat = a * _nt(dyb[:, sl], vblk[:, sl])
                pg = _cum(gmat, tri_fwd)
                dz = jnp.where(mask, gmat - sig * (pre + pg), 0.0).astype(BF)
                dks.append(_tn(dz, qs[:, sl]))
                dvs.append(_tn(a.astype(BF), dyb[:, sl]))
                new += [pre + pg[:, KB - 1:], dq + jnp.dot(dz, kblk[:, sl], preferred_element_type=F32)]
            dk_ref[pl.ds(off, KB), :] += jnp.concatenate(dks, axis=1)
            dv_ref[pl.ds(off, KB), :] += jnp.concatenate(dvs, axis=1)
            return tuple(new)

        init = (jnp.zeros((tq, 1), F32), jnp.zeros((tq, HD), F32)) * 2
        res = lax.fori_loop(0, nb, step, init)
        dq_ref[...] = (jnp.concatenate([res[1], res[3]], axis=1) * 0.125).astype(dq_ref.dtype)

    n = HEADS * HD
    return pl.pallas_call(
        body, name="attn_bwd", grid=(HEADS // 2, S // tq),
        in_specs=[pl.BlockSpec((tq, KB), lambda h, i: (i, h)), pl.BlockSpec((S, KB), lambda h, i: (0, 4 + h)),
                  pl.BlockSpec((S, KB), lambda h, i: (0, 8 + h)), pl.BlockSpec((2, tq, KB), lambda h, i: (h, i, 0)),
                  pl.BlockSpec((tq, KB), lambda h, i: (i, h))],
        out_specs=[pl.BlockSpec((tq, KB), lambda h, i: (i, h)), pl.BlockSpec((S, KB), lambda h, i: (0, h)),
                   pl.BlockSpec((S, KB), lambda h, i: (0, h))],
        out_shape=[jax.ShapeDtypeStruct((S, n), BF), jax.ShapeDtypeStruct((S, n), F32), jax.ShapeDtypeStruct((S, n), F32)],
        scratch_shapes=[pltpu.VMEM((S, KB), BF), pltpu.VMEM((S, KB), BF)],
    )(m, m, m, runs, dy)


SB = 256
NKC = SB // KB
HPS = 4
GW = HPS * HD
LOG2E = 1.4426950408889634
ANY = pl.BlockSpec(memory_space=pl.ANY)


def _split(vals):
    hi = vals.astype(BF)
    return hi, (vals - hi.astype(F32)).astype(BF)


def _chunk_sums(tri2, vals):
    hi, lo = _split(vals)
    return [jnp.dot(tri2, jnp.concatenate([hi[c * KB:(c + 1) * KB], lo[c * KB:(c + 1) * KB]], axis=0), preferred_element_type=F32)
            for c in range(NKC)]


def _head_halves(t, axis):
    idx = lax.broadcasted_iota(I32, t.shape, axis)
    return [jnp.where(idx < HD, t, 0.0).astype(BF), jnp.where(idx >= HD, t, 0.0).astype(BF)]


def _softplus2(z):
    return jnp.maximum(z, 0.0) + jnp.log2(1.0 + jnp.exp2(jnp.minimum(z, -z)))


def _attn_prep(m):
    S = m.shape[0]
    tm = min(512, S)
    n = HEADS * HD
    return _rows("attn_prep", lambda k, v: (k, v, k.T, v.T), S // tm,
                 [(m, (tm, n), lambda i: (i, 1)), (m, (tm, n), lambda i: (i, 2))],
                 [((S, n), BF, (tm, n), lambda i: (i, 0)), ((S, n), BF, (tm, n), lambda i: (i, 0)),
                  ((n, S), BF, (n, tm), lambda i: (0, i)), ((n, S), BF, (n, tm), lambda i: (0, i))])


def _attn_fwd_t(m, kb_all, vt_all):
    S = m.shape[0]
    assert S % SB == 0
    nkc = S // KB

    def body(q_ref, kb_hbm, vt_hbm, o_ref, r_ref, kb, vt, acc):
        h, i = pl.program_id(0), pl.program_id(1)

        @pl.when(i == 0)
        def _():
            cols = pl.ds(pl.multiple_of(h * GW, GW), GW)
            pltpu.sync_copy(kb_hbm.at[:, cols], kb)
            pltpu.sync_copy(vt_hbm.at[cols, :], vt)

        qt = (q_ref[...] * (0.125 * LOG2E)).T
        qtm = [t for g in range(HPS // 2) for t in _head_halves(qt[g * KB:(g + 1) * KB], 0)]
        dmask = lax.broadcasted_iota(I32, (SB, SB), 0) < lax.broadcasted_iota(I32, (SB, SB), 1)
        upper = (lax.broadcasted_iota(I32, (KB, KB), 1) >= lax.broadcasted_iota(I32, (KB, KB), 0)).astype(BF)
        tri2 = jnp.concatenate([upper, upper], axis=1)
        acc[...] = jnp.zeros_like(acc)
        r_ref[...] = jnp.zeros_like(r_ref)

        def block(jb, runs, masked):
            off = pl.multiple_of(jb * SB, SB)
            groups = [slice(g * KB, (g + 1) * KB) for g in range(HPS // 2)]
            kblk = [kb[pl.ds(off, SB), s] for s in groups]
            vtb = [vt[s, pl.ds(off, SB)] for s in groups]
            old = acc[...]
            zs = [jnp.dot(kblk[hh // 2], qtm[hh], preferred_element_type=F32) for hh in range(HPS)]
            sps = [_softplus2(z) for z in zs]
            if masked:
                sps = [jnp.where(dmask, sp, 0.0) for sp in sps]
            css = [_chunk_sums(tri2, sp) for sp in sps]
            run0s = [run + cs[1][0:1, :] for run, cs in zip(runs, css)]
            aa = [jnp.exp2(z - jnp.concatenate([run0 + cs[0], run + cs[1]], axis=0)) for z, run, run0, cs in zip(zs, runs, run0s, css)]
            if masked:
                aa = [jnp.where(dmask, a, 0.0) for a in aa]
            parts = [jnp.dot(vtb[hh // 2], aa[hh].astype(BF), preferred_element_type=F32) for hh in range(HPS)]
            upd = jnp.concatenate([parts[hh][(hh % 2) * HD:(hh % 2 + 1) * HD, :] for hh in range(HPS)], axis=0)
            for hh in range(HPS):
                r_ref[hh, pl.ds(NKC * jb, 1), :] = run0s[hh]
                r_ref[hh, pl.ds(NKC * jb + 1, 1), :] = runs[hh]
            acc[...] = old + upd
            return tuple(run0 + cs[0][0:1, :] for run0, cs in zip(run0s, css))

        zero = jnp.zeros((1, SB), F32)
        runs = block(i, (zero,) * HPS, True)
        lax.fori_loop(0, i, lambda t, c: block(i - 1 - t, c, False), runs)
        o_ref[...] = acc[...].T

    return pl.pallas_call(
        body, name="attn_fwd", grid=(HEADS // HPS, S // SB),
        in_specs=[pl.BlockSpec((SB, GW), lambda h, i: (i, h)), ANY, ANY],
        out_specs=[pl.BlockSpec((SB, GW), lambda h, i: (i, h)), pl.BlockSpec((HPS, nkc, SB), lambda h, i: (h, 0, i))],
        out_shape=[jax.ShapeDtypeStruct((S, HEADS * HD), F32), jax.ShapeDtypeStruct((HEADS, nkc, S), F32)],
        scratch_shapes=[pltpu.VMEM((S, GW), BF), pltpu.VMEM((GW, S), BF), pltpu.VMEM((GW, SB), F32)],
    )(m, kb_all, vt_all)


def _attn_bwd_t(m, kb_all, vb_all, kt_all, runs, dy):
    S = m.shape[0]
    nkc = S // KB

    def body(q_ref, kb_hbm, vb_hbm, kt_hbm, r_ref, dy_ref, dq_ref, dk_hbm, dv_hbm, kb, vb, kt, dqt, dka, dva):
        h, i = pl.program_id(0), pl.program_id(1)
        cols = pl.ds(pl.multiple_of(h * GW, GW), GW)

        @pl.when(i == 0)
        def _():
            pltpu.sync_copy(kb_hbm.at[:, cols], kb)
            pltpu.sync_copy(vb_hbm.at[:, cols], vb)
            pltpu.sync_copy(kt_hbm.at[cols, :], kt)
            dka[...] = jnp.zeros_like(dka)
            dva[...] = jnp.zeros_like(dva)

        q8 = q_ref[...] * 0.125
        dyf = dy_ref[...]
        q8t, dyt = (q8 * LOG2E).T, dyf.T
        groups = [slice(g * KB, (g + 1) * KB) for g in range(HPS // 2)]
        qtm = [t for s in groups for t in _head_halves(q8t[s], 0)]
        dytm = [t for s in groups for t in _head_halves(dyt[s], 0)]
        qlm = [t for s in groups for t in _head_halves(q8[:, s], 1)]
        dylm = [t for s in groups for t in _head_halves(dyf[:, s], 1)]
        dmask = lax.broadcasted_iota(I32, (SB, SB), 0) < lax.broadcasted_iota(I32, (SB, SB), 1)
        ri, ci = lax.broadcasted_iota(I32, (KB, KB), 0), lax.broadcasted_iota(I32, (KB, KB), 1)
        upper, lower = (ci >= ri).astype(BF), (ci <= ri).astype(BF)
        rev2, fwd2 = jnp.concatenate([upper, upper], axis=1), jnp.concatenate([lower, lower], axis=1)
        dqt[...] = jnp.zeros_like(dqt)

        def block(jb, pres, masked):
            off = pl.multiple_of(jb * SB, SB)
            heads = range(HPS)
            kblk = [kb[pl.ds(off, SB), s] for s in groups]
            vblk = [vb[pl.ds(off, SB), s] for s in groups]
            ktb = [kt[s, pl.ds(off, SB)] for s in groups]
            run0s = [r_ref[hh, pl.ds(NKC * jb, 1), :] for hh in heads]
            run1s = [r_ref[hh, pl.ds(NKC * jb + 1, 1), :] for hh in heads]
            old_dq, old_dk, old_dv = dqt[...], dka[pl.ds(off, SB), :], dva[pl.ds(off, SB), :]
            zs = [jnp.dot(kblk[hh // 2], qtm[hh], preferred_element_type=F32) for hh in heads]
            das = [jnp.dot(vblk[hh // 2], dytm[hh], preferred_element_type=F32) for hh in heads]
            sps = [_softplus2(z) for z in zs]
            sigs = [jnp.exp2(z - sp) for z, sp in zip(zs, sps)]
            if masked:
                sps = [jnp.where(dmask, sp, 0.0) for sp in sps]
            css = [_chunk_sums(rev2, sp) for sp in sps]
            aa = [jnp.exp2(z - jnp.concatenate([r0 + cs[0], r1 + cs[1]], axis=0)) for z, r0, r1, cs in zip(zs, run0s, run1s, css)]
            if masked:
                aa = [jnp.where(dmask, a, 0.0) for a in aa]
            gs = [a * da for a, da in zip(aa, das)]
            pgs = [_chunk_sums(fwd2, g) for g in gs]
            pre1s = [pre + pg[0][KB - 1:KB, :] for pre, pg in zip(pres, pgs)]
            dzs = [g - sig * jnp.concatenate([pre + pg[0], pre1 + pg[1]], axis=0)
                   for g, sig, pre, pre1, pg in zip(gs, sigs, pres, pre1s, pgs)]
            if masked:
                dzs = [jnp.where(dmask, dz, 0.0) for dz in dzs]
            dzb, ab = [dz.astype(BF) for dz in dzs], [a.astype(BF) for a in aa]
            dks = [jnp.dot(dzb[hh], qlm[hh], preferred_element_type=F32) for hh in heads]
            dvs = [jnp.dot(ab[hh], dylm[hh], preferred_element_type=F32) for hh in heads]
            parts = [jnp.dot(ktb[hh // 2], dzb[hh], preferred_element_type=F32) for hh in heads]
            dqt[...] = old_dq + jnp.concatenate([parts[hh][(hh % 2) * HD:(hh % 2 + 1) * HD, :] for hh in heads], axis=0)
            dka[pl.ds(off, SB), :] = old_dk + jnp.concatenate([dks[2 * g] + dks[2 * g + 1] for g in range(HPS // 2)], axis=1)
            dva[pl.ds(off, SB), :] = old_dv + jnp.concatenate([dvs[2 * g] + dvs[2 * g + 1] for g in range(HPS // 2)], axis=1)
            return tuple(pre1 + pg[1][KB - 1:KB, :] for pre1, pg in zip(pre1s, pgs))

        zero = jnp.zeros((1, SB), F32)
        pres = lax.fori_loop(0, i, lambda jb, c: block(jb, c, False), (zero,) * HPS)
        block(i, pres, True)
        dq_ref[...] = (dqt[...].T * 0.125).astype(dq_ref.dtype)

        @pl.when(i == pl.num_programs(1) - 1)
        def _():
            pltpu.sync_copy(dka, dk_hbm.at[:, cols])
            pltpu.sync_copy(dva, dv_hbm.at[:, cols])

    n = HEADS * HD
    return pl.pallas_call(
        body, name="attn_bwd", grid=(HEADS // HPS, S // SB),
        in_specs=[pl.BlockSpec((SB, GW), lambda h, i: (i, h)), ANY, ANY, ANY,
                  pl.BlockSpec((HPS, nkc, SB), lambda h, i: (h, 0, i)), pl.BlockSpec((SB, GW), lambda h, i: (i, h))],
        out_specs=[pl.BlockSpec((SB, GW), lambda h, i: (i, h)), ANY, ANY],
        out_shape=[jax.ShapeDtypeStruct((S, n), BF), jax.ShapeDtypeStruct((S, n), F32), jax.ShapeDtypeStruct((S, n), F32)],
        scratch_shapes=[pltpu.VMEM((S, GW), BF), pltpu.VMEM((S, GW), BF), pltpu.VMEM((GW, S), BF), pltpu.VMEM((GW, SB), F32),
                        pltpu.VMEM((S, GW), F32), pltpu.VMEM((S, GW), F32)],
    )(m, kb_all, vb_all, kt_all, runs, dy)


CW = 3 * HEADS * HD


def _shift_down(cur, prev8, s):
    if s == 0:
        return cur
    r = pltpu.roll(cur, s, 0)
    first = jnp.where(lax.broadcasted_iota(I32, (8, cur.shape[1]), 0) < s, pltpu.roll(prev8, s, 0), r[:8])
    return jnp.concatenate([first, r[8:]], axis=0)


def _shift_up(cur, next8, s):
    if s == 0:
        return cur
    n = cur.shape[0]
    r = pltpu.roll(cur, n - s, 0)
    last = jnp.where(lax.broadcasted_iota(I32, (8, cur.shape[1]), 0) >= 8 - s, pltpu.roll(next8, 8 - s, 0), r[n - 8:])
    return jnp.concatenate([r[:n - 8], last], axis=0)


def _conv_fwd(m, conv_w):
    S = m.shape[0]
    tm = min(512, S)
    hb = tm // 8

    def body(x_ref, p_ref, w_ref, o_ref):
        i = pl.program_id(0)
        cur = x_ref[...]
        prev = jnp.where(i > 0, p_ref[...], 0.0)
        w = w_ref[...]
        acc = cur * w[3:4]
        for jk in range(3):
            acc = acc + _shift_down(cur, prev, 3 - jk) * w[jk:jk + 1]
        o_ref[...] = acc

    return pl.pallas_call(
        body, name="conv_fwd", grid=(S // tm,),
        in_specs=[pl.BlockSpec((tm, CW), lambda i: (i, 1)), pl.BlockSpec((8, CW), lambda i: (jnp.maximum(i * hb - 1, 0), 1)),
                  pl.BlockSpec((4, CW), lambda i: (0, 0))],
        out_specs=pl.BlockSpec((tm, CW), lambda i: (i, 0)),
        out_shape=jax.ShapeDtypeStruct((S, CW), F32),
    )(m, m, conv_w)


def _conv_bwd(m, dyc, conv_w):
    S = m.shape[0]
    tm = min(512, S)
    hb = tm // 8
    nt = S // tm

    def body(x_ref, p_ref, d_ref, n_ref, w_ref, dx_ref, dw_ref):
        i = pl.program_id(0)
        cur = x_ref[...]
        prev = jnp.where(i > 0, p_ref[...], 0.0)
        d = d_ref[...]
        nxt = jnp.where(i < nt - 1, n_ref[...], 0.0)
        w = w_ref[...]
        acc = d * w[3:4]
        dws = []
        for jk in range(3):
            acc = acc + _shift_up(d, nxt, 3 - jk) * w[jk:jk + 1]
            dws.append(jnp.sum(d * _shift_down(cur, prev, 3 - jk), axis=0, keepdims=True))
        dws.append(jnp.sum(d * cur, axis=0, keepdims=True))
        dx_ref[...] = acc.astype(dx_ref.dtype)
        dw = jnp.concatenate(dws, axis=0)

        @pl.when(i == 0)
        def _():
            dw_ref[...] = dw

        @pl.when(i > 0)
        def _():
            dw_ref[...] += dw

    return pl.pallas_call(
        body, name="conv_bwd", grid=(nt,),
        in_specs=[pl.BlockSpec((tm, CW), lambda i: (i, 1)), pl.BlockSpec((8, CW), lambda i: (jnp.maximum(i * hb - 1, 0), 1)),
                  pl.BlockSpec((tm, CW), lambda i: (i, 0)),
                  pl.BlockSpec((8, CW), lambda i: (jnp.minimum((i + 1) * hb, S // 8 - 1), 0)),
                  pl.BlockSpec((4, CW), lambda i: (0, 0))],
        out_specs=[pl.BlockSpec((tm, CW), lambda i: (i, 0)), pl.BlockSpec((4, CW), lambda i: (0, 0))],
        out_shape=[jax.ShapeDtypeStruct((S, CW), BF), jax.ShapeDtypeStruct((4, CW), F32)],
    )(m, m, dyc, dyc, conv_w)


def _t(a):
    return jnp.swapaxes(a, 1, 2)


def _bdot(a, b):
    return jnp.einsum("hik,hkj->hij", a, b, preferred_element_type=F32)


@jax.custom_vjp
def _mm1(a, b):
    return _bdot(a.astype(BF), b.astype(BF))


def _bdot_nt(a, b):
    return jnp.einsum("hij,hkj->hik", a, b, preferred_element_type=F32)


def _bdot_tn(a, b):
    return jnp.einsum("hki,hkj->hij", a, b, preferred_element_type=F32)


_mm1.defvjp(lambda a, b: (_mm1(a, b), (a.astype(BF), b.astype(BF))),
            lambda res, dc: (_bdot_nt(dc.astype(BF), res[1]), _bdot_tn(res[0], dc.astype(BF))))


@jax.custom_vjp
def _mm1_nt(a, b):
    return _bdot_nt(a.astype(BF), b.astype(BF))


_mm1_nt.defvjp(lambda a, b: (_mm1_nt(a, b), (a.astype(BF), b.astype(BF))),
               lambda res, dc: (_bdot(dc.astype(BF), res[1]), _bdot_tn(dc.astype(BF), res[0])))


@jax.custom_vjp
def _mm1_tn(a, b):
    return _bdot_tn(a.astype(BF), b.astype(BF))


_mm1_tn.defvjp(lambda a, b: (_mm1_tn(a, b), (a.astype(BF), b.astype(BF))),
               lambda res, dc: (_bdot_nt(res[1], dc.astype(BF)), _bdot(res[0], dc.astype(BF))))


def _stack_rows(hi, lo):
    return jnp.concatenate([hi, lo], axis=1)


@jax.custom_vjp
def _mm3(a, b):
    (ah, al), (bh, bl) = _split(a), _split(b)
    n = a.shape[1]
    two = _bdot(_stack_rows(ah, al), bh)
    return two[:, :n] + two[:, n:] + _bdot(ah, bl)


def _mm3_fwd(a, b):
    return _mm3(a, b), (_split(a), _split(b))


def _mm3_bwd(res, dc):
    (ah, al), (bh, bl) = res
    dh, dl = _split(dc)
    n = dc.shape[1]
    two = _bdot_nt(_stack_rows(dh, dl), bh)
    da = two[:, :n] + two[:, n:] + _bdot_nt(dh, bl)
    db = _bdot_tn(jnp.concatenate([ah, ah, al], axis=1), jnp.concatenate([dh, dl, dh], axis=1))
    return da, db


_mm3.defvjp(_mm3_fwd, _mm3_bwd)


def _mm_exact(c3, b):
    hi, lo = _split(b)
    lo2 = (b - hi.astype(F32) - lo.astype(F32)).astype(BF)
    return _bdot(c3, jnp.concatenate([hi, lo, lo2], axis=-2))


@jax.custom_vjp
def _cumsum_rows(b):
    return _mm_exact(_tri3(True), b)


def _tri3(lower):
    ri = lax.broadcasted_iota(I32, (HEADS, CH, CH), 1)
    ci = lax.broadcasted_iota(I32, (HEADS, CH, CH), 2)
    tri = (ri >= ci if lower else ri <= ci).astype(BF)
    return jnp.concatenate([tri, tri, tri], axis=-1)


_cumsum_rows.defvjp(lambda b: (_cumsum_rows(b), None), lambda _, dc: (_mm_exact(_tri3(False), dc),))


CPS = 2


def _gdn_chunk(yc, gz, gba, alog, dtb):
    def heads(t, off):
        return jnp.stack([t[:, off + h * HD: off + (h + 1) * HD] for h in range(HEADS)])

    def cols(t, off):
        return jnp.stack([jnp.broadcast_to(t[:, off + h: off + h + 1], (CH, CH)) for h in range(HEADS)])

    c = yc * _sigmoid(yc)
    q, k, v, zg = heads(c, 0), heads(c, HEADS * HD), heads(c, 2 * HEADS * HD), heads(gz, 0)
    q = q * lax.rsqrt(jnp.sum(q * q, axis=-1, keepdims=True) + RMS_EPS) * (HD ** -0.5)
    k = k * lax.rsqrt(jnp.sum(k * k, axis=-1, keepdims=True) + RMS_EPS)
    beta = cols(_sigmoid(gba), 0)
    g = cols(-jnp.exp(alog) * _softplus(gba + dtb), HEADS)
    ri = lax.broadcasted_iota(I32, (HEADS, CH, CH), 1)
    ci = lax.broadcasted_iota(I32, (HEADS, CH, CH), 2)
    causal, strict = ri >= ci, ri > ci
    eye = (ri == ci).astype(F32)
    gc = _cumsum_rows(g)
    gr = _t(gc)
    decay = jnp.where(causal, jnp.exp(jnp.where(causal, gc - gr, 0.0)), 0.0)
    lower = jnp.where(strict, beta * _mm1_nt(k, k) * decay, 0.0)
    pw = -lower
    inv = eye + pw
    pw = _mm3(pw, pw)
    for _ in range(4):
        both = _mm3(jnp.concatenate([inv, pw], axis=1), pw)
        inv, pw = inv + both[:, :CH], both[:, CH:]
    inv = inv + _mm3(inv, pw)
    eg = jnp.exp(gc)
    uw = _mm3(inv, jnp.concatenate([v * beta, k * (beta * eg)], axis=2))
    u, w = uw[:, :, :HD], uw[:, :, HD:]
    qk = jnp.where(causal, _mm1_nt(q, k) * decay, 0.0)
    g_last = gc[:, CH - 1:CH, :]
    return u, jnp.concatenate([w, q * eg], axis=1), qk, k * jnp.exp(g_last - gc), jnp.exp(g_last), zg * _sigmoid(zg)


def _gdn_advance(state, pre, normw):
    u, wq, qk, kd, last, gate = pre
    ws = _mm1(wq, state)
    v_new = u - ws[:, :CH]
    o = ws[:, CH:] + _mm1(qk, v_new)
    new_state = state * last + _mm1_tn(kd, v_new)
    o = o * lax.rsqrt(jnp.mean(o * o, axis=-1, keepdims=True) + RMS_EPS) * normw * gate
    return jnp.concatenate([o[h] for h in range(HEADS)], axis=1), new_state


def _gdn_chunks(state, yc, gz, gba, alog, dtb, normw):
    rows = [slice(c * CH, (c + 1) * CH) for c in range(yc.shape[0] // CH)]
    pres = [_gdn_chunk(yc[r], gz[r], gba[r], alog, dtb) for r in rows]
    outs = []
    for pre in pres:
        o, state = _gdn_advance(state, pre, normw)
        outs.append(o)
    return jnp.concatenate(outs, axis=0), state


def _gdn_fwd(yc, m, alog, dtb, normw):
    S = yc.shape[0]
    RS = CPS * CH
    nch = S // RS

    def body(y_ref, gz_ref, gba_ref, al_ref, dt_ref, nw_ref, o_ref, st_ref, st):
        @pl.when(pl.program_id(0) == 0)
        def _():
            st[...] = jnp.zeros_like(st)

        cur = st[...]
        st_ref[0] = cur
        o, new = _gdn_chunks(cur, y_ref[...], gz_ref[...], gba_ref[...], al_ref[...], dt_ref[...], nw_ref[...])
        o_ref[...] = o
        st[...] = new

    return pl.pallas_call(
        body, name="gdn_fwd", grid=(nch,),
        in_specs=[pl.BlockSpec((RS, CW), lambda n: (n, 0)), pl.BlockSpec((RS, HEADS * HD), lambda n: (n, 6)),
                  pl.BlockSpec((RS, 128), lambda n: (n, 28)), pl.BlockSpec((1, 128), lambda n: (0, 0)),
                  pl.BlockSpec((1, 128), lambda n: (0, 0)), pl.BlockSpec((1, HD), lambda n: (0, 0))],
        out_specs=[pl.BlockSpec((RS, HEADS * HD), lambda n: (n, 0)), pl.BlockSpec((1, HEADS, HD, HD), lambda n: (n, 0, 0, 0))],
        out_shape=[jax.ShapeDtypeStruct((S, HEADS * HD), F32), jax.ShapeDtypeStruct((nch, HEADS, HD, HD), F32)],
        scratch_shapes=[pltpu.VMEM((HEADS, HD, HD), F32)],
    )(yc, m, m, alog, dtb, normw)


def _gdn_bwd(yc, m, alog, dtb, normw, states, dog):
    S = yc.shape[0]
    RS = CPS * CH
    nch = S // RS

    def body(y_ref, gz_ref, gba_ref, al_ref, dt_ref, nw_ref, st_ref, do_ref, dy_ref, dgz_ref, dgba_ref, dal_ref, ddt_ref, dnw_ref, dst):
        n = pl.program_id(0)

        @pl.when(n == 0)
        def _():
            dst[...] = jnp.zeros_like(dst)

        _, vjp = jax.vjp(_gdn_chunks, st_ref[0], y_ref[...], gz_ref[...], gba_ref[...], al_ref[...], dt_ref[...], nw_ref[...])
        d_state, d_y, d_gz, d_gba, d_al, d_dt, d_nw = vjp((do_ref[...], dst[...]))
        dst[...] = d_state
        dy_ref[...] = d_y
        dgz_ref[...] = d_gz.astype(dgz_ref.dtype)
        dgba_ref[...] = d_gba.astype(dgba_ref.dtype)
        for ref, val in ((dal_ref, d_al), (ddt_ref, d_dt), (dnw_ref, d_nw)):
            @pl.when(n == 0)
            def _(ref=ref, val=val):
                ref[...] = val

            @pl.when(n > 0)
            def _(ref=ref, val=val):
                ref[...] += val

    rev = lambda n: nch - 1 - n
    return pl.pallas_call(
        body, name="gdn_bwd", grid=(nch,),
        in_specs=[pl.BlockSpec((RS, CW), lambda n: (rev(n), 0)), pl.BlockSpec((RS, HEADS * HD), lambda n: (rev(n), 6)),
                  pl.BlockSpec((RS, 128), lambda n: (rev(n), 28)), pl.BlockSpec((1, 128), lambda n: (0, 0)),
                  pl.BlockSpec((1, 128), lambda n: (0, 0)), pl.BlockSpec((1, HD), lambda n: (0, 0)),
                  pl.BlockSpec((1, HEADS, HD, HD), lambda n: (rev(n), 0, 0, 0)),
                  pl.BlockSpec((RS, HEADS * HD), lambda n: (rev(n), 0))],
        out_specs=[pl.BlockSpec((RS, CW), lambda n: (rev(n), 0)), pl.BlockSpec((RS, HEADS * HD), lambda n: (rev(n), 0)),
                   pl.BlockSpec((RS, 128), lambda n: (rev(n), 0)), pl.BlockSpec((1, 128), lambda n: (0, 0)),
                   pl.BlockSpec((1, 128), lambda n: (0, 0)), pl.BlockSpec((1, HD), lambda n: (0, 0))],
        out_shape=[jax.ShapeDtypeStruct((S, CW), F32), jax.ShapeDtypeStruct((S, HEADS * HD), BF),
                   jax.ShapeDtypeStruct((S, 128), BF), jax.ShapeDtypeStruct((1, 128), F32),
                   jax.ShapeDtypeStruct((1, 128), F32), jax.ShapeDtypeStruct((1, HD), F32)],
        scratch_shapes=[pltpu.VMEM((HEADS, HD, HD), F32)],
    )(yc, m, m, alog, dtb, normw, states, dog)


def _mixer_fwd(h1, wp, b_gate, conv_w, alog, dtb, normw, w_sb, w_gdn, w_mo, g, b):
    S = h1.shape[0]
    tm = min(512, S)
    n = HEADS * HD
    m, = _mm("mix_in", h1, wp, S, NP, D, tm=tm, tn=1536, tk=D, order="ji", outs=[((S, NP), F32, (tm, 1536), _tile)])
    kb_all, vb_all, kt_all, vt_all = _attn_prep(m)
    ya, runs = _attn_fwd_t(m, kb_all, vt_all)
    runs = (kb_all, vb_all, kt_all, runs)
    yc = _conv_fwd(m, conv_w)
    og, states = _gdn_fwd(yc, m, alog, dtb, normw)
    ysb, = _mm("mix_sb", ya, w_sb, S, D, n, tm=tm, tn=D, tk=n, outs=[((S, D), F32, (tm, D), _row)])

    def merge_epi(acc, ys, gs, gg, bg):
        return _sigmoid(gs + bg[:, :D]) * ys + _sigmoid(gg + bg[:, D:]) * acc, acc

    u, ygdn = _mm("mix_gdn", og, w_gdn, S, D, n, tm=tm, tn=D, tk=n,
                  extras=[(ysb, (tm, D), _row), (m, (tm, D), lambda i, j, k: (i, GATE0 // D)),
                          (m, (tm, D), lambda i, j, k: (i, GATE0 // D + 1)), (b_gate, (1, 2 * D), _const)],
                  epilogue=merge_epi, outs=[((S, D), BF, (tm, D), _row), ((S, D), F32, (tm, D), _row)])

    def out_epi(acc, xin, gg, bb):
        r = ALPHA * xin + acc
        return _layer_norm(r, gg, bb), r

    h2, r2 = _mm("mix_out", u, w_mo, S, D, D, tm=tm, tn=D, tk=D,
                 extras=[(h1, (tm, D), _row), (g, (1, D), _const), (b, (1, D), _const)], epilogue=out_epi,
                 outs=[((S, D), F32, (tm, D), _row), ((S, D), F32, (tm, D), _row)])
    return h2, (h1, m, ya, runs, yc, og, states, ysb, ygdn, u, r2)


def _mixer_bwd(saved, wp, b_gate, conv_w, alog, dtb, normw, w_sb, w_gdn, w_mo, g, dh):
    h1, m, ya, runs, yc, og, states, ysb, ygdn, u, r2 = saved
    S = h1.shape[0]
    tm = min(512, S)
    n = HEADS * HD
    dr, dg, db = _ln_bwd("mix", r2, g, dh)

    def merge_epi(du, ys, yg, gs, gg, bg):
        s1, s2 = _sigmoid(gs + bg[:, :D]), _sigmoid(gg + bg[:, D:])
        dgate = jnp.concatenate([du * ys * s1 * (1.0 - s1), du * yg * s2 * (1.0 - s2)], axis=1)
        return du * s1, du * s2, dgate, jnp.sum(dgate, axis=0, keepdims=True)

    dysb, dygdn, dgate, d_bg = _mm(
        "mix_dmerge", dr, w_mo, S, D, D, tm=tm, tn=D, tk=D, tb=True,
        extras=[(ysb, (tm, D), _row), (ygdn, (tm, D), _row), (m, (tm, D), lambda i, j, k: (i, GATE0 // D)),
                (m, (tm, D), lambda i, j, k: (i, GATE0 // D + 1)), (b_gate, (1, 2 * D), _const)],
        epilogue=merge_epi, n_acc=1,
        outs=[((S, D), BF, (tm, D), _row), ((S, D), BF, (tm, D), _row), ((S, 2 * D), BF, (tm, 2 * D), _row),
              ((1, 2 * D), F32, (1, 2 * D), _const)])
    d_w_mo, = _mm("mix_dwmo", u, dr, D, D, S, tm=D, tn=D, tk=tm, ta=True, outs=[((D, D), F32, (D, D), _tile)])
    dya, = _mm("mix_dya", dysb, w_sb, S, n, D, tm=tm, tn=n, tk=D, tb=True, outs=[((S, n), F32, (tm, n), _row)])
    col_shards = [((NSH, n, D // NSH), F32, (None, n, D // NSH), lambda i, j, k: (j, 0, 0))]
    d_w_sb, = _mm("mix_dwsb", ya, dysb, n, D, S, tm=n, tn=D // NSH, tk=tm, ta=True, order="ji", outs=col_shards)
    dog, = _mm("mix_dog", dygdn, w_gdn, S, n, D, tm=tm, tn=n, tk=D, tb=True, outs=[((S, n), F32, (tm, n), _row)])
    d_w_gdn, = _mm("mix_dwgdn", og, dygdn, n, D, S, tm=n, tn=D // NSH, tk=tm, ta=True, order="ji", outs=col_shards)
    dq, dk, dv = _attn_bwd_t(m, *runs, dya)
    dyc, dgz, dgba, d_alog, d_dtb, d_normw = _gdn_bwd(yc, m, alog, dtb, normw, states, dog)
    dxc, d_conv = _conv_bwd(m, dyc, conv_w)
    dm = jnp.concatenate([dq, dk.astype(BF), dv.astype(BF), dxc, dgz, dgba, jnp.zeros((S, GATE0 - 3712), BF), dgate], axis=1)
    d_h1, = _mm("mix_dh", dm, wp, S, D, NP, tm=tm, tn=D, tk=1536, tb=True,
                extras=[(dr, (tm, D), _row)], epilogue=lambda acc, d: (acc + ALPHA * d,),
                outs=[((S, D), F32, (tm, D), _row)])
    d_wp, = _mm("mix_dwp", h1, dm, D, NP, S, tm=D, tn=1536, tk=tm, ta=True, order="ji",
                outs=[((D, NP), F32, (D, 1536), _tile)])
    return d_h1, dict(wp=d_wp, b_gate=d_bg, conv_w=d_conv, alog=d_alog, dtb=d_dtb, normw=d_normw,
                      w_sb=d_w_sb, w_gdn=d_w_gdn, w_mo=d_w_mo, g=dg, b=db)


def _ple_fwd(h3, p, w_pg, b_pg, w_ple, g, b, target):
    S = h3.shape[0]
    tm = min(512, S)
    pd = p.shape[1]
    pe, = _mm("ple_emb", p, w_ple, S, D, pd, tm=tm, tn=D, tk=pd, outs=[((S, D), F32, (tm, D), _row)])

    def epi(acc, e, xin, tgt, bp, gg, bb):
        gt = _sigmoid(acc + bp)
        r = ALPHA * xin + gt * e
        diff = _layer_norm(r, gg, bb) - tgt
        return gt, r, diff * (1.0 / D), jnp.sum(diff * diff, axis=0, keepdims=True)

    gt, r4, dh4, loss_row = _mm(
        "ple_gate", h3, w_pg, S, D, D, tm=tm, tn=D, tk=D,
        extras=[(pe, (tm, D), _row), (h3, (tm, D), _row), (target, (tm, D), _row), (b_pg, (1, D), _const),
                (g, (1, D), _const), (b, (1, D), _const)], epilogue=epi, n_acc=1,
        outs=[((S, D), F32, (tm, D), _row), ((S, D), F32, (tm, D), _row), ((S, D), F32, (tm, D), _row),
              ((1, D), F32, (1, D), _const)])
    return dh4, loss_row, (h3, p, pe, gt, r4)


def _ple_bwd(saved, w_pg, g, dh4):
    h3, p, pe, gt, r4 = saved
    S = h3.shape[0]
    tm = min(512, S)
    pd = p.shape[1]

    def fn(r, gg, dh, e, t):
        dr, dg, db = _layer_norm_bwd(r, gg, dh)
        dpre = dr * e * t * (1.0 - t)
        return dr, dpre, dr * t, dg, db, jnp.sum(dpre, axis=0, keepdims=True)

    row, one = (lambda i: (i, 0)), (lambda i: (0, 0))
    dr, dpre, dpe, dg, db, d_bpg = _rows(
        "ple_lnbwd", fn, S // tm,
        [(r4, (tm, D), row), (g, (1, D), one), (dh4, (tm, D), row), (pe, (tm, D), row), (gt, (tm, D), row)],
        [((S, D), F32, (tm, D), row), ((S, D), BF, (tm, D), row), ((S, D), BF, (tm, D), row),
         ((1, D), F32, (1, D), one), ((1, D), F32, (1, D), one), ((1, D), F32, (1, D), one)], n_acc=3)
    d_w_pg, = _mm("ple_dwpg", h3, dpre, D, D, S, tm=D, tn=D, tk=tm, ta=True, outs=[((D, D), F32, (D, D), _tile)])
    d_w_ple, = _mm("ple_dwple", p, dpe, pd, D, S, tm=pd, tn=D // NSH, tk=tm, ta=True, order="ji",
                   outs=[((NSH, pd, D // NSH), F32, (None, pd, D // NSH), lambda i, j, k: (j, 0, 0))])
    d_h3, = _mm("ple_dh", dpre, w_pg, S, D, D, tm=tm, tn=D, tk=D, tb=True,
                extras=[(dr, (tm, D), _row)], epilogue=lambda acc, d: (acc + ALPHA * d,),
                outs=[((S, D), F32, (tm, D), _row)])
    return d_h3, d_w_pg, d_bpg, d_w_ple, dg, db


def _local_step(x, p, target, W):
    h1, sv1 = _ffn_fwd("ffn1", x, W["ffn1_in"], W["ffn1_out"], W["ln1_g"], W["ln1_b"])
    h2, sv2 = _mixer_fwd(h1, W["wp"], W["b_gate"], W["conv_w"], W["alog"], W["dtb"], W["normw"],
                         W["w_sb"], W["w_gdn"], W["w_mo"], W["ln2_g"], W["ln2_b"])
    h3, sv3 = _ffn_fwd("ffn2", h2, W["ffn2_in"], W["ffn2_out"], W["ln3_g"], W["ln3_b"])
    dh4, loss_row, sv4 = _ple_fwd(h3, p, W["w_pg"], W["b_pg"], W["w_ple"], W["ln4_g"], W["ln4_b"], target)
    G = {}
    dh3, G["w_pg"], G["b_pg"], G["w_ple"], G["ln4_g"], G["ln4_b"] = _ple_bwd(sv4, W["w_pg"], W["ln4_g"], dh4)
    dh2, G["ffn2_in"], G["ffn2_out"], G["ln3_g"], G["ln3_b"] = _ffn_bwd("ffn2", sv3, W["ffn2_in"], W["ffn2_out"], W["ln3_g"], dh3)
    dh1, gm = _mixer_bwd(sv2, W["wp"], W["b_gate"], W["conv_w"], W["alog"], W["dtb"], W["normw"],
                         W["w_sb"], W["w_gdn"], W["w_mo"], W["ln2_g"], dh2)
    G.update({k: v for k, v in gm.items() if k not in ("g", "b")})
    G["ln2_g"], G["ln2_b"] = gm["g"], gm["b"]
    dx, G["ffn1_in"], G["ffn1_out"], G["ln1_g"], G["ln1_b"] = _ffn_bwd("ffn1", sv1, W["ffn1_in"], W["ffn1_out"], W["ln1_g"], dh1)
    return loss_row, dx, G


S2 = CUT - 2 * MSH


def _pack_wp(w4):
    tr = 256

    def fn(w):
        s = [w[j].astype(F32) for j in range(NSH)]
        full = jnp.concatenate([s[0], s[1], s[2][:, :S2], jnp.zeros((tr, GATE0 - CUT), F32), s[2][:, S2:], s[3]], axis=1)
        return (full,)

    return _rows("pack_wp", fn, D // tr, [(w4, (NSH, tr, MSH), lambda i: (0, i, 0))],
                 [((D, NP), BF, (tr, NP), lambda i: (i, 0))])[0]


def _unpack_wp(d):
    tr = 256
    g2 = GATE0 + MSH - S2

    def fn(v):
        return (jnp.stack([v[:, :MSH], v[:, MSH:2 * MSH], jnp.concatenate([v[:, 2 * MSH:CUT], v[:, GATE0:g2]], axis=1), v[:, g2:]]),)

    return _rows("unpack_wp", fn, D // tr, [(d, (tr, NP), lambda i: (i, 0))],
                 [((NSH, D, MSH), F32, (NSH, tr, MSH), lambda i: (0, i, 0))])[0]


def _cast_bf16(tag, w):
    r, c = w.shape
    tr = _pick(r, 256)
    return _rows(f"cast_{tag}", lambda v: (v,), r // tr, [(w, (tr, c), lambda i: (i, 0))],
                 [((r, c), BF, (tr, c), lambda i: (i, 0))])[0]


ANY = pl.BlockSpec(memory_space=pl.ANY)


def _place():
    return lax.axis_index("x"), lax.axis_index("y"), lax.axis_index("c")


def _gather_shards(shards):
    n = len(shards)

    def body(*refs):
        ins, outs = refs[:n], refs[n:2 * n]
        send, recv, loc = refs[2 * n:]
        x, y, c = _place()
        me = 2 * x + y
        chips = [(1 - x, y), (x, 1 - y), (1 - x, 1 - y)]
        started = []
        for t in range(n):
            lc = pltpu.make_async_copy(ins[t], outs[t].at[me], loc.at[t])
            lc.start()
            started.append(lc)
            for q, (px, py) in enumerate(chips):
                cp = pltpu.make_async_remote_copy(src_ref=ins[t], dst_ref=outs[t].at[me], send_sem=send.at[3 * t + q],
                                                  recv_sem=recv.at[3 * t + q], device_id=(px, py, c), device_id_type=MESH)
                cp.start()
                started.append(cp)
        for t in range(n):
            for q, (px, py) in enumerate(chips):
                pltpu.make_async_remote_copy(src_ref=ins[t], dst_ref=outs[t].at[2 * px + py], send_sem=send.at[3 * t + q],
                                             recv_sem=recv.at[3 * t + q], device_id=(px, py, c), device_id_type=MESH).wait_recv()
        for t in range(n):
            started[4 * t].wait()
            for q in range(3):
                started[4 * t + 1 + q].wait_send()

    return pl.pallas_call(
        body, name="gather_weights", in_specs=[ANY] * n, out_specs=[ANY] * n,
        out_shape=[jax.ShapeDtypeStruct((NSH,) + s.shape, s.dtype) for s in shards],
        scratch_shapes=[pltpu.SemaphoreType.DMA((3 * n,)), pltpu.SemaphoreType.DMA((3 * n,)), pltpu.SemaphoreType.DMA((n,))],
    )(*shards)


def _reduce_sibling(gs):
    n = len(gs)

    def body(*refs):
        ins, mine, theirs = refs[:n], refs[n:2 * n], refs[2 * n:3 * n]
        send, recv, loc = refs[3 * n:]
        x, y, c = _place()
        started = []
        for t in range(n):
            h = gs[t].shape[1] // 2
            lc = pltpu.make_async_copy(ins[t].at[:, pl.ds(pl.multiple_of(c * h, 8), h), :], mine[t], loc.at[t])
            cp = pltpu.make_async_remote_copy(src_ref=ins[t].at[:, pl.ds(pl.multiple_of((1 - c) * h, 8), h), :], dst_ref=theirs[t],
                                              send_sem=send.at[t], recv_sem=recv.at[t], device_id=(x, y, 1 - c), device_id_type=MESH)
            lc.start()
            cp.start()
            started += [lc, cp]
        for t in range(n):
            started[2 * t].wait()
            started[2 * t + 1].wait()

    half = [jax.ShapeDtypeStruct((NSH, g.shape[1] // 2, g.shape[2]), g.dtype) for g in gs]
    res = pl.pallas_call(
        body, name="reduce_sibling", in_specs=[ANY] * n, out_specs=[ANY] * (2 * n), out_shape=half + half,
        scratch_shapes=[pltpu.SemaphoreType.DMA((n,)), pltpu.SemaphoreType.DMA((n,)), pltpu.SemaphoreType.DMA((n,))],
    )(*gs)
    return res[:n], res[n:]


def _reduce_chips(ps):
    n = len(ps)

    def body(*refs):
        ins, outs = refs[:n], refs[n:2 * n]
        send, recv, loc = refs[2 * n:]
        x, y, c = _place()
        me = 2 * x + y
        chips = [(1 - x, y), (x, 1 - y), (1 - x, 1 - y)]
        started = []
        for t in range(n):
            lc = pltpu.make_async_copy(ins[t].at[me], outs[t].at[me], loc.at[t])
            lc.start()
            started.append(lc)
            for q, (px, py) in enumerate(chips):
                cp = pltpu.make_async_remote_copy(src_ref=ins[t].at[2 * px + py], dst_ref=outs[t].at[me], send_sem=send.at[3 * t + q],
                                                  recv_sem=recv.at[3 * t + q], device_id=(px, py, c), device_id_type=MESH)
                cp.start()
                started.append(cp)
        for t in range(n):
            for q, (px, py) in enumerate(chips):
                pltpu.make_async_remote_copy(src_ref=ins[t].at[me], dst_ref=outs[t].at[2 * px + py], send_sem=send.at[3 * t + q],
                                             recv_sem=recv.at[3 * t + q], device_id=(px, py, c), device_id_type=MESH).wait_recv()
        for t in range(n):
            started[4 * t].wait()
            for q in range(3):
                started[4 * t + 1 + q].wait_send()

    return pl.pallas_call(
        body, name="reduce_chips", in_specs=[ANY] * n, out_specs=[ANY] * n,
        out_shape=[jax.ShapeDtypeStruct(p_.shape, p_.dtype) for p_ in ps],
        scratch_shapes=[pltpu.SemaphoreType.DMA((3 * n,)), pltpu.SemaphoreType.DMA((3 * n,)), pltpu.SemaphoreType.DMA((n,))],
    )(*ps)


def _share_sibling(fs):
    n = len(fs)

    def body(*refs):
        ins, outs = refs[:n], refs[n:2 * n]
        send, recv, loc = refs[2 * n:]
        x, y, c = _place()
        started = []
        for t in range(n):
            lc = pltpu.make_async_copy(ins[t], outs[t].at[c], loc.at[t])
            cp = pltpu.make_async_remote_copy(src_ref=ins[t], dst_ref=outs[t].at[c], send_sem=send.at[t], recv_sem=recv.at[t],
                                              device_id=(x, y, 1 - c), device_id_type=MESH)
            lc.start()
            cp.start()
            started += [lc, cp]
        for t in range(n):
            started[2 * t].wait()
            started[2 * t + 1].wait_send()
            pltpu.make_async_remote_copy(src_ref=ins[t], dst_ref=outs[t].at[1 - c], send_sem=send.at[t], recv_sem=recv.at[t],
                                         device_id=(x, y, 1 - c), device_id_type=MESH).wait_recv()

    return pl.pallas_call(
        body, name="share_sibling", in_specs=[ANY] * n, out_specs=[ANY] * n,
        out_shape=[jax.ShapeDtypeStruct((2,) + f.shape, f.dtype) for f in fs],
        scratch_shapes=[pltpu.SemaphoreType.DMA((n,)), pltpu.SemaphoreType.DMA((n,)), pltpu.SemaphoreType.DMA((n,))],
    )(*fs)


def _chunk_rows(h, c):
    return _pick(h, max(8, 262144 // c // 8 * 8))


def _sibling_sum(tag, g):
    _, r, c = g.shape
    h = r // 2
    tr = _chunk_rows(h, c)
    nch = h // tr
    steps = NSH * nch

    def body(top_ref, bot_ref, out_ref, narrow_ref, land, send, recv):
        s = pl.program_id(0)
        x, y, core = _place()

        def exchange(keep_ref, give_ref):
            cp = pltpu.make_async_remote_copy(src_ref=give_ref.at[0], dst_ref=land.at[s], send_sem=send.at[s], recv_sem=recv.at[s],
                                              device_id=(x, y, 1 - core), device_id_type=MESH)
            cp.start()
            cp.wait()
            total = keep_ref[0] + land[s]
            out_ref[0] = total
            narrow_ref[0] = total.astype(BF)

        @pl.when(core == 0)
        def _():
            exchange(top_ref, bot_ref)

        @pl.when(core == 1)
        def _():
            exchange(bot_ref, top_ref)

    return pl.pallas_call(
        body, name=f"sibling_sum_{tag}", grid=(steps,),
        in_specs=[pl.BlockSpec((1, tr, c), lambda s: (s // nch, s % nch, 0)),
                  pl.BlockSpec((1, tr, c), lambda s: (s // nch, nch + s % nch, 0))],
        out_specs=[pl.BlockSpec((1, tr, c), lambda s: (s // nch, s % nch, 0))] * 2,
        out_shape=[jax.ShapeDtypeStruct((NSH, h, c), F32), jax.ShapeDtypeStruct((NSH, h, c), BF)],
        scratch_shapes=[pltpu.VMEM((steps, tr, c), F32), pltpu.SemaphoreType.DMA((steps,)), pltpu.SemaphoreType.DMA((steps,))],
    )(g, g)


def _chip_sum_share(tag, b, own):
    _, h, c = b.shape
    tr = _chunk_rows(h, c)
    steps = h // tr

    def body(b_ref, own_ref, out_ref, stage, land, send, recv):
        s = pl.program_id(0)
        x, y, core = _place()
        me = 2 * x + y
        v = [jnp.where(me == j, own_ref[j], b_ref[j].astype(F32)) for j in range(NSH)]
        total = ((v[0] + v[1]) + v[2]) + v[3]
        stage[...] = total
        cp = pltpu.make_async_remote_copy(src_ref=stage, dst_ref=land.at[s], send_sem=send.at[s], recv_sem=recv.at[s],
                                          device_id=(x, y, 1 - core), device_id_type=MESH)
        cp.start()
        cp.wait()
        out_ref[core] = total
        out_ref[1 - core] = land[s]

    return pl.pallas_call(
        body, name=f"chip_sum_share_{tag}", grid=(steps,),
        in_specs=[pl.BlockSpec((NSH, tr, c), lambda s: (0, s, 0))] * 2,
        out_specs=pl.BlockSpec((2, tr, c), lambda s: (0, s, 0)),
        out_shape=jax.ShapeDtypeStruct((2, h, c), F32),
        scratch_shapes=[pltpu.VMEM((tr, c), F32), pltpu.VMEM((steps, tr, c), F32), pltpu.SemaphoreType.DMA((steps,)),
                        pltpu.SemaphoreType.DMA((steps,))],
    )(b, own)


NDEV = 8


def _allreduce_small(pack):
    r, w = pack.shape
    rel = [(dx, dy, dc) for dx in (0, 1) for dy in (0, 1) for dc in (0, 1) if (dx, dy, dc) != (0, 0, 0)]

    def body(in_ref, out_ref, buf, send, recv):
        x, y, c = _place()
        me = 4 * x + 2 * y + c
        buf[me] = in_ref[...]
        peers = [((x + dx) % 2, (y + dy) % 2, (c + dc) % 2) for dx, dy, dc in rel]
        sent = []
        for k, peer in enumerate(peers):
            cp = pltpu.make_async_remote_copy(src_ref=in_ref, dst_ref=buf.at[me], send_sem=send.at[k], recv_sem=recv.at[k],
                                              device_id=peer, device_id_type=MESH)
            cp.start()
            sent.append(cp)
        for k, (px, py, pc) in enumerate(peers):
            pltpu.make_async_remote_copy(src_ref=in_ref, dst_ref=buf.at[4 * px + 2 * py + pc], send_sem=send.at[k], recv_sem=recv.at[k],
                                         device_id=(px, py, pc), device_id_type=MESH).wait_recv()
        for cp in sent:
            cp.wait_send()
        acc = buf[0]
        for k in range(1, NDEV):
            acc = acc + buf[k]
        out_ref[...] = acc

    vm = pl.BlockSpec(memory_space=pltpu.VMEM)
    return pl.pallas_call(
        body, name="allreduce_small", in_specs=[vm], out_specs=vm, out_shape=jax.ShapeDtypeStruct((r, w), F32),
        scratch_shapes=[pltpu.VMEM((NDEV, r, w), F32), pltpu.SemaphoreType.DMA((NDEV - 1,)), pltpu.SemaphoreType.DMA((NDEV - 1,))],
    )(pack)


def _add2(tag, a, b):
    _, h, c = a.shape
    tr = _pick(h, max(8, 262144 // c // 8 * 8))
    spec = ((None, tr, c), lambda i: (i // (h // tr), i % (h // tr), 0))
    return _rows(f"add2_{tag}", lambda u, v: (u + v,), NSH * (h // tr), [(a,) + spec, (b,) + spec], [(a.shape, F32) + spec])[0]


def _add4(tag, a):
    _, h, c = a.shape
    tr = _pick(h, max(8, 131072 // c // 8 * 8))
    return _rows(f"add4_{tag}", lambda v: (((v[0] + v[1]) + v[2]) + v[3],), h // tr, [(a, (NSH, tr, c), lambda i: (0, i, 0))],
                 [((h, c), F32, (tr, c), lambda i: (i, 0))])[0]


def _adamw(tag, w, g, m, v):
    r, c = w.shape
    tr = _pick(r, max(8, 262144 // c // 8 * 8))

    def fn(w_, g_, m_, v_):
        m2 = B1 * m_ + (1.0 - B1) * g_
        v2 = B2 * v_ + (1.0 - B2) * (g_ * g_)
        m_hat = m2 / (1.0 - B1 ** STEP)
        v_hat = v2 / (1.0 - B2 ** STEP)
        return -LR * (m_hat / (jnp.sqrt(v_hat) + EPS) + WD * w_), m2, v2

    spec = ((tr, c), lambda i: (i, 0))
    return _rows(f"adamw_{tag}", fn, r // tr, [(a,) + spec for a in (w, g, m, v)], [((r, c), F32) + spec] * 3)


BIG = ("ffn1_w_in", "ffn1_w_out", "w_mix_in", "w_branch_sb", "w_branch_gdn", "w_mix_out", "ffn2_w_in", "ffn2_w_out",
       "w_ple_gate", "w_ple")
SMALL = ("ln1_g", "ln1_b", "b_gate", "conv_w", "a_log", "dt_bias", "gdn_norm_w", "ln2_g", "ln2_b", "ln3_g", "ln3_b",
         "b_ple_gate", "ln4_g", "ln4_b")
ORDER = ("ffn1_w_in", "ffn1_w_out", "ln1_g", "ln1_b", "w_mix_in", "b_gate", "conv_w", "a_log", "dt_bias", "gdn_norm_w",
         "w_branch_sb", "w_branch_gdn", "w_mix_out", "ln2_g", "ln2_b", "ffn2_w_in", "ffn2_w_out", "ln3_g", "ln3_b",
         "w_ple_gate", "b_ple_gate", "w_ple", "ln4_g", "ln4_b")
PACK_W = 2304


def _lane_row(v, lanes=128, at=HEADS):
    return jnp.pad(v[None, :], ((0, 0), (at, lanes - at - v.shape[0])))


def _col_join(w4):
    return jnp.transpose(w4, (1, 0, 2)).reshape(w4.shape[1], NSH * w4.shape[2])


def kernel(x, p, ffn1_w_in, ffn1_w_out, ln1_g, ln1_b, w_mix_in, b_gate, conv_w, a_log, dt_bias, gdn_norm_w, w_branch_sb, w_branch_gdn, w_mix_out, ln2_g, ln2_b, ffn2_w_in, ffn2_w_out, ln3_g, ln3_b, w_ple_gate, b_ple_gate, w_ple, ln4_g, ln4_b, loss_target, m_ffn1_w_in, m_ffn1_w_out, m_ln1_g, m_ln1_b, m_w_mix_in, m_b_gate, m_conv_w, m_a_log, m_dt_bias, m_gdn_norm_w, m_w_branch_sb, m_w_branch_gdn, m_w_mix_out, m_ln2_g, m_ln2_b, m_ffn2_w_in, m_ffn2_w_out, m_ln3_g, m_ln3_b, m_w_ple_gate, m_b_ple_gate, m_w_ple, m_ln4_g, m_ln4_b, v_ffn1_w_in, v_ffn1_w_out, v_ln1_g, v_ln1_b, v_w_mix_in, v_b_gate, v_conv_w, v_a_log, v_dt_bias, v_gdn_norm_w, v_w_branch_sb, v_w_branch_gdn, v_w_mix_out, v_ln2_g, v_ln2_b, v_ffn2_w_in, v_ffn2_w_out, v_ln3_g, v_ln3_b, v_w_ple_gate, v_b_ple_gate, v_w_ple, v_ln4_g, v_ln4_b):
    args = dict(locals())
    w = {n: args[n][0] for n in ORDER}
    mom = {n: args["m_" + n][0] for n in ORDER}
    var = {n: args["v_" + n][0] for n in ORDER}

    shards = [_cast_bf16(n, w[n]) for n in BIG] + [w["conv_w"]]
    full = dict(zip(BIG + ("conv_w",), _gather_shards(shards)))
    W = dict(
        ffn1_in=full["ffn1_w_in"], ffn1_out=full["ffn1_w_out"].reshape(DFF, D),
        ffn2_in=full["ffn2_w_in"], ffn2_out=full["ffn2_w_out"].reshape(DFF, D),
        wp=_pack_wp(full["w_mix_in"]), w_sb=_col_join(full["w_branch_sb"]), w_gdn=_col_join(full["w_branch_gdn"]),
        w_mo=full["w_mix_out"].reshape(D, D), w_pg=full["w_ple_gate"].reshape(D, D), w_ple=_col_join(full["w_ple"]),
        conv_w=_col_join(full["conv_w"]), b_gate=w["b_gate"][None], alog=_lane_row(w["a_log"]), dtb=_lane_row(w["dt_bias"]),
        normw=w["gdn_norm_w"][None], b_pg=w["b_ple_gate"][None],
        **{f"ln{i}_{s}": w[f"ln{i}_{s}"][None] for i in (1, 2, 3, 4) for s in ("g", "b")},
    )

    loss_row, grad_x, G = _local_step(x[0], p[0, 0], loss_target[0], W)
    loss = lax.psum(0.5 * jnp.sum(loss_row) / D, ("x", "y", "c"))

    big = dict(
        ffn1_w_in=G["ffn1_in"], ffn1_w_out=G["ffn1_out"].reshape(NSH, DFF // NSH, D), w_mix_in=_unpack_wp(G["wp"]),
        w_branch_sb=G["w_sb"], w_branch_gdn=G["w_gdn"], w_mix_out=G["w_mo"].reshape(NSH, D // NSH, D),
        ffn2_w_in=G["ffn2_in"], ffn2_w_out=G["ffn2_out"].reshape(NSH, DFF // NSH, D),
        w_ple_gate=G["w_pg"].reshape(NSH, D // NSH, D), w_ple=G["w_ple"],
    )
    sums = [_sibling_sum(n, big[n]) for n in BIG]
    landed = _reduce_chips([narrow for _, narrow in sums])
    grad = {n: _chip_sum_share(n, b, own).reshape(w[n].shape) for n, b, (own, _) in zip(BIG, landed, sums)}

    pieces = [G["ln1_g"], G["ln1_b"], G["b_gate"], G["conv_w"].reshape(1, 4 * CW), G["alog"], G["dtb"], G["normw"],
              G["ln2_g"], G["ln2_b"], G["ln3_g"], G["ln3_b"], G["b_pg"], G["ln4_g"], G["ln4_b"]]
    flat = jnp.concatenate(pieces, axis=1)
    flat = jnp.pad(flat, ((0, 0), (0, NDEV * PACK_W - flat.shape[1])))
    total = _allreduce_small(flat.reshape(NDEV, PACK_W)).reshape(1, NDEV * PACK_W)
    off = 0
    for n, piece in zip(SMALL, pieces):
        grad[n] = total[0, off:off + piece.shape[1]]
        off += piece.shape[1]
    chip = 2 * lax.axis_index("x") + lax.axis_index("y")
    grad["conv_w"] = lax.dynamic_slice_in_dim(grad["conv_w"].reshape(4, CW), chip * (CW // NSH), CW // NSH, axis=1)
    grad["a_log"] = grad["a_log"][HEADS:2 * HEADS]
    grad["dt_bias"] = grad["dt_bias"][HEADS:2 * HEADS]

    delta, new_m, new_v = {}, {}, {}
    for n in ORDER:
        shape2 = w[n].shape if w[n].ndim == 2 else (1, w[n].shape[0])
        d_, m_, v_ = _adamw(n, *[a.reshape(shape2) for a in (w[n], grad[n], mom[n], var[n])])
        delta[n], new_m[n], new_v[n] = (a.reshape(args[n].shape) for a in (d_, m_, v_))
    outs = [loss, grad_x[None]]
    outs += [grad[n].reshape(args[n].shape) for n in ORDER]
    for group in (delta, new_m, new_v):
        outs += [group[n] for n in ORDER]
    return tuple(outs)
```

```python
import functools

import jax
import jax.numpy as jnp
from jax import lax
from jax.experimental import pallas as pl
from jax.experimental.pallas import tpu as pltpu

F32 = jnp.float32
BF = jnp.bfloat16
I32 = jnp.int32
HI = lax.Precision.HIGHEST
MESH = pl.DeviceIdType.MESH

D = 1024
DFF = 2816
NSH = 4
FSH = 2 * DFF // NSH
NIN = 5648
MSH = NIN // NSH
NP = 6144
CUT = 3600
GATE0 = 4096
HEADS = 8
HD = 64
CH = 64
KB = 128
BIG_ROWS = 1024
ALPHA = 2.0 ** 0.25
LN_EPS = 1e-5
RMS_EPS = 1e-6
B1, B2, LR, EPS, WD, STEP = 0.9, 0.999, 0.001, 1e-08, 0.01, 10


def _sigmoid(x):
    return 0.5 * jnp.tanh(0.5 * x) + 0.5


def _softplus(x):
    return jnp.maximum(x, 0.0) + jnp.log1p(jnp.exp(-jnp.abs(x)))


def _layer_norm(r, g, b):
    mu = jnp.mean(r, axis=-1, keepdims=True)
    xc = r - mu
    var = jnp.mean(xc * xc, axis=-1, keepdims=True)
    return xc * lax.rsqrt(var + LN_EPS) * g + b


def _layer_norm_bwd(r, g, dh):
    mu = jnp.mean(r, axis=-1, keepdims=True)
    xc = r - mu
    var = jnp.mean(xc * xc, axis=-1, keepdims=True)
    xhat = xc * lax.rsqrt(var + LN_EPS)
    dxh = dh * g
    dr = lax.rsqrt(var + LN_EPS) * (dxh - jnp.mean(dxh, axis=-1, keepdims=True) - xhat * jnp.mean(dxh * xhat, axis=-1, keepdims=True))
    return dr, jnp.sum(dh * xhat, axis=0, keepdims=True), jnp.sum(dh, axis=0, keepdims=True)


def _pick(n, cap):
    if n <= cap:
        return n
    for t in range(cap - cap % 8, 7, -8):
        if n % t == 0:
            return t
    raise ValueError((n, cap))


def _mm(name, a, b, M, N, K, *, tm, tn, tk, ta=False, tb=False, a_spec=None, b_spec=None, order="ij",
        extras=(), epilogue=None, outs, n_acc=0):
    ni, nj, nk = M // tm, N // tn, K // tk
    assert M % tm == 0 and N % tn == 0 and K % tk == 0, (name, M, N, K, tm, tn, tk)
    assert n_acc == 0 or nj == 1

    def wrap(fn):
        if order == "ij":
            return lambda g0, g1, g2: fn(g0, g1, g2)
        return lambda g0, g1, g2: fn(g1, g0, g2)

    if a_spec is None:
        a_spec = ((tk, tm), lambda i, j, k: (k, i)) if ta else ((tm, tk), lambda i, j, k: (i, k))
    if b_spec is None:
        b_spec = ((tn, tk), lambda i, j, k: (j, k)) if tb else ((tk, tn), lambda i, j, k: (k, j))
    dims = (((0 if ta else 1,), (1 if tb else 0,)), ((), ()))
    ne, no = len(extras), len(outs)
    grid = (ni, nj, nk) if order == "ij" else (nj, ni, nk)

    def body(*refs):
        a_ref, b_ref = refs[0], refs[1]
        ex = refs[2:2 + ne]
        o = refs[2 + ne:2 + ne + no]
        g0, g1, k = pl.program_id(0), pl.program_id(1), pl.program_id(2)
        first = jnp.logical_and(g0 == 0, g1 == 0)
        p = lax.dot_general(a_ref[...].astype(BF), b_ref[...].astype(BF), dims, preferred_element_type=F32)

        def finish(acc):
            vals = (acc,) if epilogue is None else epilogue(acc, *[e[...] for e in ex])
            for idx, (ref, val) in enumerate(zip(o, vals)):
                if idx < no - n_acc:
                    ref[...] = val.astype(ref.dtype)
                else:
                    @pl.when(first)
                    def _(ref=ref, val=val):
                        ref[...] = val

                    @pl.when(jnp.logical_not(first))
                    def _(ref=ref, val=val):
                        ref[...] += val

        if nk == 1:
            finish(p)
        else:
            acc_ref = refs[-1]

            @pl.when(k == 0)
            def _():
                acc_ref[...] = p

            @pl.when(k > 0)
            def _():
                acc_ref[...] += p

            @pl.when(k == nk - 1)
            def _():
                finish(acc_ref[...])

    in_specs = [pl.BlockSpec(a_spec[0], wrap(a_spec[1])), pl.BlockSpec(b_spec[0], wrap(b_spec[1]))]
    in_specs += [pl.BlockSpec(blk, wrap(fn)) for _, blk, fn in extras]
    res = pl.pallas_call(
        body, name=name, grid=grid, in_specs=in_specs,
        out_specs=[pl.BlockSpec(blk, wrap(fn)) for _, _, blk, fn in outs],
        out_shape=[jax.ShapeDtypeStruct(shape, dt) for shape, dt, _, _ in outs],
        scratch_shapes=[pltpu.VMEM((tm, tn), F32)] if nk > 1 else [],
    )(a, b, *[e[0] for e in extras])
    return res


def _row(i, j, k):
    return (i, 0)


def _tile(i, j, k):
    return (i, j)


def _const(i, j, k):
    return (0, 0)


def _rows(name, fn, n_steps, ins, outs, n_acc=0):
    ni, no = len(ins), len(outs)

    def body(*refs):
        i = pl.program_id(0)
        vals = fn(*[r[...] for r in refs[:ni]])
        for idx, (ref, val) in enumerate(zip(refs[ni:ni + no], vals)):
            if idx < no - n_acc:
                ref[...] = val.astype(ref.dtype)
            else:
                @pl.when(i == 0)
                def _(ref=ref, val=val):
                    ref[...] = val

                @pl.when(i > 0)
                def _(ref=ref, val=val):
                    ref[...] += val

    return pl.pallas_call(
        body, name=name, grid=(n_steps,),
        in_specs=[pl.BlockSpec(blk, fn_) for _, blk, fn_ in ins],
        out_specs=[pl.BlockSpec(blk, fn_) for _, _, blk, fn_ in outs],
        out_shape=[jax.ShapeDtypeStruct(shape, dt) for shape, dt, _, _ in outs],
    )(*[a for a, _, _ in ins])


def _ffn_fwd(tag, x, w_in, w_out, g, b):
    S = x.shape[0]
    tm = min(BIG_ROWS, S)
    gate, = _mm(f"{tag}_gate", x, w_in, S, DFF, D, tm=tm, tn=FSH, tk=D, order="ji",
                b_spec=((None, D, FSH), lambda i, j, k: (j, 0, 0)),
                outs=[((S, DFF), F32, (tm, FSH), _tile)])

    def up_epi(acc, gt):
        return acc, gt * _sigmoid(gt) * acc

    up, s = _mm(f"{tag}_up", x, w_in, S, DFF, D, tm=tm, tn=FSH, tk=D, order="ji",
                b_spec=((None, D, FSH), lambda i, j, k: (j + 2, 0, 0)),
                extras=[(gate, (tm, FSH), _tile)], epilogue=up_epi,
                outs=[((S, DFF), F32, (tm, FSH), _tile), ((S, DFF), BF, (tm, FSH), _tile)])

    def out_epi(acc, xin, gg, bb):
        r = ALPHA * xin + 0.5 * acc
        return _layer_norm(r, gg, bb), r

    h, r = _mm(f"{tag}_out", s, w_out, S, D, DFF, tm=tm, tn=D, tk=DFF,
               extras=[(x, (tm, D), _row), (g, (1, D), _const), (b, (1, D), _const)], epilogue=out_epi,
               outs=[((S, D), F32, (tm, D), _row), ((S, D), F32, (tm, D), _row)])
    return h, (x, gate, up, s, r)


def _ln_bwd(tag, r, g, dh):
    S = r.shape[0]
    tm = min(512, S)
    return _rows(f"{tag}_lnbwd", _layer_norm_bwd, S // tm,
                 [(r, (tm, D), lambda i: (i, 0)), (g, (1, D), lambda i: (0, 0)), (dh, (tm, D), lambda i: (i, 0))],
                 [((S, D), F32, (tm, D), lambda i: (i, 0)), ((1, D), F32, (1, D), lambda i: (0, 0)),
                  ((1, D), F32, (1, D), lambda i: (0, 0))], n_acc=2)


def _ffn_bwd(tag, saved, w_in, w_out, g, dh):
    x, gate, up, s, r = saved
    S = x.shape[0]
    tm = min(BIG_ROWS, S)
    dr, dg, db = _ln_bwd(tag, r, g, dh)

    def act_epi(acc, gt, u):
        ds = 0.5 * acc
        sg = _sigmoid(gt)
        return (jnp.stack([ds * u * (sg * (1.0 + gt * (1.0 - sg))), ds * (gt * sg)]),)

    da, = _mm(f"{tag}_dact", dr, w_out, S, DFF, D, tm=tm, tn=FSH, tk=D, tb=True, order="ji",
              extras=[(gate, (tm, FSH), _tile), (up, (tm, FSH), _tile)], epilogue=act_epi,
              outs=[((2, S, DFF), BF, (2, tm, FSH), lambda i, j, k: (0, i, j))])
    d_w_out, = _mm(f"{tag}_dwout", s, dr, DFF, D, S, tm=FSH, tn=D, tk=tm, ta=True,
                   epilogue=lambda acc: (0.5 * acc,), outs=[((DFF, D), F32, (FSH, D), _tile)])
    tb_ = min(BIG_ROWS, S)
    d_in, = _mm(f"{tag}_dx", da, w_in, S, D, 2 * DFF, tm=tb_, tn=D, tk=FSH, tb=True,
                a_spec=((None, tb_, FSH), lambda i, j, k: (k // 2, i, k % 2)),
                b_spec=((None, D, FSH), lambda i, j, k: (k, 0, 0)),
                extras=[(dr, (tb_, D), _row)], epilogue=lambda acc, d: (acc + ALPHA * d,),
                outs=[((S, D), F32, (tb_, D), _row)])
    d_w_in, = _mm(f"{tag}_dwin", x, da, D, 2 * DFF, S, tm=D, tn=FSH, tk=tb_, ta=True, order="ji",
                  b_spec=((None, tb_, FSH), lambda i, j, k: (j // 2, k, j % 2)),
                  outs=[((NSH, D, FSH), F32, (None, D, FSH), lambda i, j, k: (j, 0, 0))])
    return d_in, d_w_in, d_w_out, dg, db


def _cum(vals, tri):
    hi = vals.astype(BF)
    lo = (vals - hi.astype(F32)).astype(BF)
    return jnp.dot(hi, tri, preferred_element_type=F32) + jnp.dot(lo, tri, preferred_element_type=F32)


def _nt(a, b):
    return lax.dot_general(a, b, (((1,), (1,)), ((), ())), preferred_element_type=F32)


def _tn(a, b):
    return lax.dot_general(a, b, (((0,), (0,)), ((), ())), preferred_element_type=F32)


def _attn_fwd(m):
    S = m.shape[0]
    tq = min(256, S)
    assert S // KB <= 128

    def body(q_ref, k_ref, v_ref, o_ref, r_ref, kb, vb):
        i = pl.program_id(1)

        @pl.when(i == 0)
        def _():
            kb[...] = k_ref[...].astype(BF)
            vb[...] = v_ref[...].astype(BF)

        qs = (q_ref[...] * 0.125).astype(BF)
        row = lax.broadcasted_iota(I32, (tq, KB), 0) + i * tq
        col = lax.broadcasted_iota(I32, (tq, KB), 1)
        tri = (lax.broadcasted_iota(I32, (KB, KB), 0) >= lax.broadcasted_iota(I32, (KB, KB), 1)).astype(BF)
        nb = (i + 1) * (tq // KB)
        r_ref[...] = jnp.zeros_like(r_ref)

        def step(t, carry):
            j = nb - 1 - t
            off = pl.multiple_of(j * KB, KB)
            mask = col + j * KB < row
            kblk = kb[pl.ds(off, KB), :]
            vblk = vb[pl.ds(off, KB), :]
            new = []
            for hh in range(2):
                run, acc = carry[2 * hh], carry[2 * hh + 1]
                sl = slice(hh * HD, (hh + 1) * HD)
                z = _nt(qs[:, sl], kblk[:, sl])
                cs = _cum(jnp.where(mask, _softplus(z), 0.0), tri)
                a = jnp.where(mask, jnp.exp(z - (run + cs)), 0.0)
                r_ref[hh] = jnp.where(col == j, run, r_ref[hh])
                new += [run + cs[:, :1], acc + jnp.dot(a.astype(BF), vblk[:, sl], preferred_element_type=F32)]
            return tuple(new)

        init = (jnp.zeros((tq, 1), F32), jnp.zeros((tq, HD), F32)) * 2
        res = lax.fori_loop(0, nb, step, init)
        o_ref[...] = jnp.concatenate([res[1], res[3]], axis=1)

    return pl.pallas_call(
        body, name="attn_fwd", grid=(HEADS // 2, S // tq),
        in_specs=[pl.BlockSpec((tq, KB), lambda h, i: (i, h)), pl.BlockSpec((S, KB), lambda h, i: (0, 4 + h)),
                  pl.BlockSpec((S, KB), lambda h, i: (0, 8 + h))],
        out_specs=[pl.BlockSpec((tq, KB), lambda h, i: (i, h)), pl.BlockSpec((2, tq, KB), lambda h, i: (h, i, 0))],
        out_shape=[jax.ShapeDtypeStruct((S, HEADS * HD), F32), jax.ShapeDtypeStruct((HEADS, S, KB), F32)],
        scratch_shapes=[pltpu.VMEM((S, KB), BF), pltpu.VMEM((S, KB), BF)],
    )(m, m, m)


def _attn_bwd(m, runs, dy):
    S = m.shape[0]
    tq = min(256, S)

    def body(q_ref, k_ref, v_ref, r_ref, dy_ref, dq_ref, dk_ref, dv_ref, kb, vb):
        i = pl.program_id(1)

        @pl.when(i == 0)
        def _():
            kb[...] = k_ref[...].astype(BF)
            vb[...] = v_ref[...].astype(BF)
            dk_ref[...] = jnp.zeros_like(dk_ref)
            dv_ref[...] = jnp.zeros_like(dv_ref)

        qs = (q_ref[...] * 0.125).astype(BF)
        dyb = dy_ref[...].astype(BF)
        rs = [r_ref[0], r_ref[1]]
        row = lax.broadcasted_iota(I32, (tq, KB), 0) + i * tq
        col = lax.broadcasted_iota(I32, (tq, KB), 1)
        jj = lax.broadcasted_iota(I32, (KB, KB), 0)
        ss = lax.broadcasted_iota(I32, (KB, KB), 1)
        tri_rev = (jj >= ss).astype(BF)
        tri_fwd = (jj <= ss).astype(BF)
        nb = (i + 1) * (tq // KB)

        def step(j, carry):
            off = pl.multiple_of(j * KB, KB)
            mask = col + j * KB < row
            kblk = kb[pl.ds(off, KB), :]
            vblk = vb[pl.ds(off, KB), :]
            new, dks, dvs = [], [], []
            for hh in range(2):
                pre, dq = carry[2 * hh], carry[2 * hh + 1]
                sl = slice(hh * HD, (hh + 1) * HD)
                run = jnp.sum(jnp.where(col == j, rs[hh], 0.0), axis=1, keepdims=True)
                z = _nt(qs[:, sl], kblk[:, sl])
                e = jnp.exp(-jnp.abs(z))
                sp = jnp.maximum(z, 0.0) + jnp.log1p(e)
                sig = jnp.where(z >= 0.0, 1.0, e) / (1.0 + e)
                cs = _cum(jnp.where(mask, sp, 0.0), tri_rev)
                a = jnp.where(mask, jnp.exp(z - (run + cs)), 0.0)
                gmat = a * _nt(dyb[:, sl], vblk[:, sl])
                pg = _cum(gmat, tri_fwd)
                dz = jnp.where(mask, gmat - sig * (pre + pg), 0.0).astype(BF)
                dks.append(_tn(dz, qs[:, sl]))
                dvs.append(_tn(a.astype(BF), dyb[:, sl]))
                new += [pre + pg[:, KB - 1:], dq + jnp.dot(dz, kblk[:, sl], preferred_element_type=F32)]
            dk_ref[pl.ds(off, KB), :] += jnp.concatenate(dks, axis=1)
            dv_ref[pl.ds(off, KB), :] += jnp.concatenate(dvs, axis=1)
            return tuple(new)

        init = (jnp.zeros((tq, 1), F32), jnp.zeros((tq, HD), F32)) * 2
        res = lax.fori_loop(0, nb, step, init)
        dq_ref[...] = (jnp.concatenate([res[1], res[3]], axis=1) * 0.125).astype(dq_ref.dtype)

    n = HEADS * HD
    return pl.pallas_call(
        body, name="attn_bwd", grid=(HEADS // 2, S // tq),
        in_specs=[pl.BlockSpec((tq, KB), lambda h, i: (i, h)), pl.BlockSpec((S, KB), lambda h, i: (0, 4 + h)),
                  pl.BlockSpec((S, KB), lambda h, i: (0, 8 + h)), pl.BlockSpec((2, tq, KB), lambda h, i: (h, i, 0)),
                  pl.BlockSpec((tq, KB), lambda h, i: (i, h))],
        out_specs=[pl.BlockSpec((tq, KB), lambda h, i: (i, h)), pl.BlockSpec((S, KB), lambda h, i: (0, h)),
                   pl.BlockSpec((S, KB), lambda h, i: (0, h))],
        out_shape=[jax.ShapeDtypeStruct((S, n), BF), jax.ShapeDtypeStruct((S, n), F32), jax.ShapeDtypeStruct((S, n), F32)],
        scratch_shapes=[pltpu.VMEM((S, KB), BF), pltpu.VMEM((S, KB), BF)],
    )(m, m, m, runs, dy)


SB = 256
NKC = SB // KB
HPS = 4
GW = HPS * HD
LOG2E = 1.4426950408889634
ANY = pl.BlockSpec(memory_space=pl.ANY)


def _split(vals):
    hi = vals.astype(BF)
    return hi, (vals - hi.astype(F32)).astype(BF)


def _chunk_sums(tri2, vals):
    hi, lo = _split(vals)
    return [jnp.dot(tri2, jnp.concatenate([hi[c * KB:(c + 1) * KB], lo[c * KB:(c + 1) * KB]], axis=0), preferred_element_type=F32)
            for c in range(NKC)]


def _head_halves(t, axis):
    idx = lax.broadcasted_iota(I32, t.shape, axis)
    return [jnp.where(idx < HD, t, 0.0).astype(BF), jnp.where(idx >= HD, t, 0.0).astype(BF)]


QT = 512


def _diag_masks(sq):
    krow, qcol = lax.broadcasted_iota(I32, (SB, sq), 0), lax.broadcasted_iota(I32, (SB, sq), 1)
    return [krow + d * SB < qcol for d in range(sq // SB)]


def _softplus2(z):
    return jnp.maximum(z, 0.0) + jnp.log2(1.0 + jnp.exp2(jnp.minimum(z, -z)))


def _attn_prep(m):
    S = m.shape[0]
    tm = min(512, S)
    n = HEADS * HD
    return _rows("attn_prep", lambda k, v: (k, v, k.T, v.T), S // tm,
                 [(m, (tm, n), lambda i: (i, 1)), (m, (tm, n), lambda i: (i, 2))],
                 [((S, n), BF, (tm, n), lambda i: (i, 0)), ((S, n), BF, (tm, n), lambda i: (i, 0)),
                  ((n, S), BF, (n, tm), lambda i: (0, i)), ((n, S), BF, (n, tm), lambda i: (0, i))])


def _attn_fwd_t(m, kb_all, vt_all):
    S = m.shape[0]
    SQ = min(QT, S)
    assert S % SQ == 0 and SQ % SB == 0
    nkc, nqb = S // KB, SQ // SB

    def body(q_ref, kb_hbm, vt_hbm, o_ref, r_ref, kb, vt, acc):
        h, i = pl.program_id(0), pl.program_id(1)

        @pl.when(i == 0)
        def _():
            cols = pl.ds(pl.multiple_of(h * GW, GW), GW)
            pltpu.sync_copy(kb_hbm.at[:, cols], kb)
            pltpu.sync_copy(vt_hbm.at[cols, :], vt)

        qt = (q_ref[...] * (0.125 * LOG2E)).T
        qtm = [t for g in range(HPS // 2) for t in _head_halves(qt[g * KB:(g + 1) * KB], 0)]
        dmasks = _diag_masks(SQ)
        upper = (lax.broadcasted_iota(I32, (KB, KB), 1) >= lax.broadcasted_iota(I32, (KB, KB), 0)).astype(BF)
        tri2 = jnp.concatenate([upper, upper], axis=1)
        acc[...] = jnp.zeros_like(acc)
        r_ref[...] = jnp.zeros_like(r_ref)

        def block(jb, runs, dmask):
            masked = dmask is not None
            off = pl.multiple_of(jb * SB, SB)
            groups = [slice(g * KB, (g + 1) * KB) for g in range(HPS // 2)]
            kblk = [kb[pl.ds(off, SB), s] for s in groups]
            vtb = [vt[s, pl.ds(off, SB)] for s in groups]
            old = acc[...]
            zs = [jnp.dot(kblk[hh // 2], qtm[hh], preferred_element_type=F32) for hh in range(HPS)]
            sps = [_softplus2(z) for z in zs]
            if masked:
                sps = [jnp.where(dmask, sp, 0.0) for sp in sps]
            css = [_chunk_sums(tri2, sp) for sp in sps]
            run0s = [run + cs[1][0:1, :] for run, cs in zip(runs, css)]
            aa = [jnp.exp2(z - jnp.concatenate([run0 + cs[0], run + cs[1]], axis=0)) for z, run, run0, cs in zip(zs, runs, run0s, css)]
            if masked:
                aa = [jnp.where(dmask, a, 0.0) for a in aa]
            parts = [jnp.dot(vtb[hh // 2], aa[hh].astype(BF), preferred_element_type=F32) for hh in range(HPS)]
            upd = jnp.concatenate([parts[hh][(hh % 2) * HD:(hh % 2 + 1) * HD, :] for hh in range(HPS)], axis=0)
            for hh in range(HPS):
                r_ref[hh, pl.ds(NKC * jb, 1), :] = run0s[hh]
                r_ref[hh, pl.ds(NKC * jb + 1, 1), :] = runs[hh]
            acc[...] = old + upd
            return tuple(run0 + cs[0][0:1, :] for run0, cs in zip(run0s, css))

        runs = (jnp.zeros((1, SQ), F32),) * HPS
        for d in reversed(range(nqb)):
            runs = block(i * nqb + d, runs, dmasks[d])
        lax.fori_loop(0, i * nqb, lambda t, c: block(i * nqb - 1 - t, c, None), runs)
        o_ref[...] = acc[...].T

    return pl.pallas_call(
        body, name="attn_fwd", grid=(HEADS // HPS, S // SQ),
        in_specs=[pl.BlockSpec((SQ, GW), lambda h, i: (i, h)), ANY, ANY],
        out_specs=[pl.BlockSpec((SQ, GW), lambda h, i: (i, h)), pl.BlockSpec((HPS, nkc, SQ), lambda h, i: (h, 0, i))],
        out_shape=[jax.ShapeDtypeStruct((S, HEADS * HD), F32), jax.ShapeDtypeStruct((HEADS, nkc, S), F32)],
        scratch_shapes=[pltpu.VMEM((S, GW), BF), pltpu.VMEM((GW, S), BF), pltpu.VMEM((GW, SQ), F32)],
    )(m, kb_all, vt_all)


def _attn_bwd_t(m, kb_all, vb_all, kt_all, runs, dy):
    S = m.shape[0]
    SQ = min(QT, S)
    nkc, nqb, nq = S // KB, SQ // SB, S // SQ

    def body(q_ref, kb_hbm, vb_hbm, kt_hbm, r_ref, dy_ref, dq_ref, dk_hbm, dv_hbm, kb, vb, kt, dqt, dka, dva):
        h, i = pl.program_id(0), pl.program_id(1)
        cols = pl.ds(pl.multiple_of(h * GW, GW), GW)

        @pl.when(i == 0)
        def _():
            pltpu.sync_copy(kb_hbm.at[:, cols], kb)
            pltpu.sync_copy(vb_hbm.at[:, cols], vb)
            pltpu.sync_copy(kt_hbm.at[cols, :], kt)
            dka[...] = jnp.zeros_like(dka)
            dva[...] = jnp.zeros_like(dva)

        q8 = q_ref[...] * 0.125
        dyf = dy_ref[...]
        q8t, dyt = (q8 * LOG2E).T, dyf.T
        groups = [slice(g * KB, (g + 1) * KB) for g in range(HPS // 2)]
        qtm = [t for s in groups for t in _head_halves(q8t[s], 0)]
        dytm = [t for s in groups for t in _head_halves(dyt[s], 0)]
        qlm = [t for s in groups for t in _head_halves(q8[:, s], 1)]
        dylm = [t for s in groups for t in _head_halves(dyf[:, s], 1)]
        dmasks = _diag_masks(SQ)
        ri, ci = lax.broadcasted_iota(I32, (KB, KB), 0), lax.broadcasted_iota(I32, (KB, KB), 1)
        upper, lower = (ci >= ri).astype(BF), (ci <= ri).astype(BF)
        rev2, fwd2 = jnp.concatenate([upper, upper], axis=1), jnp.concatenate([lower, lower], axis=1)
        dqt[...] = jnp.zeros_like(dqt)

        def block(jb, pres, dmask):
            masked = dmask is not None
            off = pl.multiple_of(jb * SB, SB)
            heads = range(HPS)
            kblk = [kb[pl.ds(off, SB), s] for s in groups]
            vblk = [vb[pl.ds(off, SB), s] for s in groups]
            ktb = [kt[s, pl.ds(off, SB)] for s in groups]
            run0s = [r_ref[hh, pl.ds(NKC * jb, 1), :] for hh in heads]
            run1s = [r_ref[hh, pl.ds(NKC * jb + 1, 1), :] for hh in heads]
            old_dq, old_dk, old_dv = dqt[...], dka[pl.ds(off, SB), :], dva[pl.ds(off, SB), :]
            dks, dvs, parts, new = [], [], [], []
            for g in range(HPS // 2):
                hs = (2 * g, 2 * g + 1)
                zs = [jnp.dot(kblk[g], qtm[hh], preferred_element_type=F32) for hh in hs]
                das = [jnp.dot(vblk[g], dytm[hh], preferred_element_type=F32) for hh in hs]
                sps = [_softplus2(z) for z in zs]
                sigs = [jnp.exp2(z - sp) for z, sp in zip(zs, sps)]
                if masked:
                    sps = [jnp.where(dmask, sp, 0.0) for sp in sps]
                css = [_chunk_sums(rev2, sp) for sp in sps]
                aa = [jnp.exp2(z - jnp.concatenate([run0s[hh] + cs[0], run1s[hh] + cs[1]], axis=0)) for z, hh, cs in zip(zs, hs, css)]
                if masked:
                    aa = [jnp.where(dmask, a, 0.0) for a in aa]
                gs = [a * da for a, da in zip(aa, das)]
                pgs = [_chunk_sums(fwd2, gg) for gg in gs]
                pre1s = [pres[hh] + pg[0][KB - 1:KB, :] for hh, pg in zip(hs, pgs)]
                dzs = [gg - sig * jnp.concatenate([pres[hh] + pg[0], pre1 + pg[1]], axis=0)
                       for gg, sig, hh, pre1, pg in zip(gs, sigs, hs, pre1s, pgs)]
                if masked:
                    dzs = [jnp.where(dmask, dz, 0.0) for dz in dzs]
                dzb, ab = [dz.astype(BF) for dz in dzs], [a.astype(BF) for a in aa]
                dks.append(sum(jnp.dot(dzb[t], qlm[hh], preferred_element_type=F32) for t, hh in enumerate(hs)))
                dvs.append(sum(jnp.dot(ab[t], dylm[hh], preferred_element_type=F32) for t, hh in enumerate(hs)))
                parts += [jnp.dot(ktb[g], dzb[t], preferred_element_type=F32)[t * HD:(t + 1) * HD, :] for t in range(2)]
                new += [pre1 + pg[1][KB - 1:KB, :] for pre1, pg in zip(pre1s, pgs)]
            dqt[...] = old_dq + jnp.concatenate(parts, axis=0)
            dka[pl.ds(off, SB), :] = old_dk + jnp.concatenate(dks, axis=1)
            dva[pl.ds(off, SB), :] = old_dv + jnp.concatenate(dvs, axis=1)
            return tuple(new)

        pres = lax.fori_loop(0, i * nqb, lambda jb, c: block(jb, c, None), (jnp.zeros((1, SQ), F32),) * HPS)
        for d in range(nqb):
            pres = block(i * nqb + d, pres, dmasks[d])
        dq_ref[...] = (dqt[...].T * 0.125).astype(dq_ref.dtype)

        @pl.when(i == nq - 1)
        def _():
            pltpu.sync_copy(dka, dk_hbm.at[:, cols])
            pltpu.sync_copy(dva, dv_hbm.at[:, cols])

    n = HEADS * HD
    return pl.pallas_call(
        body, name="attn_bwd", grid=(HEADS // HPS, nq),
        in_specs=[pl.BlockSpec((SQ, GW), lambda h, i: (i, h)), ANY, ANY, ANY,
                  pl.BlockSpec((HPS, nkc, SQ), lambda h, i: (h, 0, i)), pl.BlockSpec((SQ, GW), lambda h, i: (i, h))],
        out_specs=[pl.BlockSpec((SQ, GW), lambda h, i: (i, h)), ANY, ANY],
        out_shape=[jax.ShapeDtypeStruct((S, n), BF), jax.ShapeDtypeStruct((S, n), F32), jax.ShapeDtypeStruct((S, n), F32)],
        scratch_shapes=[pltpu.VMEM((S, GW), BF), pltpu.VMEM((S, GW), BF), pltpu.VMEM((GW, S), BF), pltpu.VMEM((GW, SQ), F32),
                        pltpu.VMEM((S, GW), F32), pltpu.VMEM((S, GW), F32)],
    )(m, kb_all, vb_all, kt_all, runs, dy)


CW = 3 * HEADS * HD


def _shift_down(cur, prev8, s):
    if s == 0:
        return cur
    r = pltpu.roll(cur, s, 0)
    first = jnp.where(lax.broadcasted_iota(I32, (8, cur.shape[1]), 0) < s, pltpu.roll(prev8, s, 0), r[:8])
    return jnp.concatenate([first, r[8:]], axis=0)


def _shift_up(cur, next8, s):
    if s == 0:
        return cur
    n = cur.shape[0]
    r = pltpu.roll(cur, n - s, 0)
    last = jnp.where(lax.broadcasted_iota(I32, (8, cur.shape[1]), 0) >= 8 - s, pltpu.roll(next8, 8 - s, 0), r[n - 8:])
    return jnp.concatenate([r[:n - 8], last], axis=0)


def _conv_fwd(m, conv_w):
    S = m.shape[0]
    tm = min(512, S)
    hb = tm // 8

    def body(x_ref, p_ref, w_ref, o_ref):
        i = pl.program_id(0)
        cur = x_ref[...]
        prev = jnp.where(i > 0, p_ref[...], 0.0)
        w = w_ref[...]
        acc = cur * w[3:4]
        for jk in range(3):
            acc = acc + _shift_down(cur, prev, 3 - jk) * w[jk:jk + 1]
        o_ref[...] = acc

    return pl.pallas_call(
        body, name="conv_fwd", grid=(S // tm,),
        in_specs=[pl.BlockSpec((tm, CW), lambda i: (i, 1)), pl.BlockSpec((8, CW), lambda i: (jnp.maximum(i * hb - 1, 0), 1)),
                  pl.BlockSpec((4, CW), lambda i: (0, 0))],
        out_specs=pl.BlockSpec((tm, CW), lambda i: (i, 0)),
        out_shape=jax.ShapeDtypeStruct((S, CW), F32),
    )(m, m, conv_w)


def _conv_bwd(m, dyc, conv_w):
    S = m.shape[0]
    tm = min(512, S)
    hb = tm // 8
    nt = S // tm

    def body(x_ref, p_ref, d_ref, n_ref, w_ref, dx_ref, dw_ref):
        i = pl.program_id(0)
        cur = x_ref[...]
        prev = jnp.where(i > 0, p_ref[...], 0.0)
        d = d_ref[...]
        nxt = jnp.where(i < nt - 1, n_ref[...], 0.0)
        w = w_ref[...]
        acc = d * w[3:4]
        dws = []
        for jk in range(3):
            acc = acc + _shift_up(d, nxt, 3 - jk) * w[jk:jk + 1]
            dws.append(jnp.sum(d * _shift_down(cur, prev, 3 - jk), axis=0, keepdims=True))
        dws.append(jnp.sum(d * cur, axis=0, keepdims=True))
        dx_ref[...] = acc.astype(dx_ref.dtype)
        dw = jnp.concatenate(dws, axis=0)

        @pl.when(i == 0)
        def _():
            dw_ref[...] = dw

        @pl.when(i > 0)
        def _():
            dw_ref[...] += dw

    return pl.pallas_call(
        body, name="conv_bwd", grid=(nt,),
        in_specs=[pl.BlockSpec((tm, CW), lambda i: (i, 1)), pl.BlockSpec((8, CW), lambda i: (jnp.maximum(i * hb - 1, 0), 1)),
                  pl.BlockSpec((tm, CW), lambda i: (i, 0)),
                  pl.BlockSpec((8, CW), lambda i: (jnp.minimum((i + 1) * hb, S // 8 - 1), 0)),
                  pl.BlockSpec((4, CW), lambda i: (0, 0))],
        out_specs=[pl.BlockSpec((tm, CW), lambda i: (i, 0)), pl.BlockSpec((4, CW), lambda i: (0, 0))],
        out_shape=[jax.ShapeDtypeStruct((S, CW), BF), jax.ShapeDtypeStruct((4, CW), F32)],
    )(m, m, dyc, dyc, conv_w)


def _t(a):
    return jnp.swapaxes(a, 1, 2)


def _bdot(a, b):
    return jnp.einsum("hik,hkj->hij", a, b, preferred_element_type=F32)


@jax.custom_vjp
def _mm1(a, b):
    return _bdot(a.astype(BF), b.astype(BF))


def _bdot_nt(a, b):
    return jnp.einsum("hij,hkj->hik", a, b, preferred_element_type=F32)


def _bdot_tn(a, b):
    return jnp.einsum("hki,hkj->hij", a, b, preferred_element_type=F32)


_mm1.defvjp(lambda a, b: (_mm1(a, b), (a.astype(BF), b.astype(BF))),
            lambda res, dc: (_bdot_nt(dc.astype(BF), res[1]), _bdot_tn(res[0], dc.astype(BF))))


@jax.custom_vjp
def _mm1_nt(a, b):
    return _bdot_nt(a.astype(BF), b.astype(BF))


_mm1_nt.defvjp(lambda a, b: (_mm1_nt(a, b), (a.astype(BF), b.astype(BF))),
               lambda res, dc: (_bdot(dc.astype(BF), res[1]), _bdot_tn(dc.astype(BF), res[0])))


@jax.custom_vjp
def _mm1_tn(a, b):
    return _bdot_tn(a.astype(BF), b.astype(BF))


_mm1_tn.defvjp(lambda a, b: (_mm1_tn(a, b), (a.astype(BF), b.astype(BF))),
               lambda res, dc: (_bdot_nt(res[1], dc.astype(BF)), _bdot(res[0], dc.astype(BF))))


def _stack_rows(hi, lo):
    return jnp.concatenate([hi, lo], axis=1)


@jax.custom_vjp
def _mm3(a, b):
    (ah, al), (bh, bl) = _split(a), _split(b)
    n = a.shape[1]
    two = _bdot(_stack_rows(ah, al), bh)
    return two[:, :n] + two[:, n:] + _bdot(ah, bl)


def _mm3_fwd(a, b):
    return _mm3(a, b), (_split(a), _split(b))


def _mm3_bwd(res, dc):
    (ah, al), (bh, bl) = res
    dh, dl = _split(dc)
    n = dc.shape[1]
    two = _bdot_nt(_stack_rows(dh, dl), bh)
    da = two[:, :n] + two[:, n:] + _bdot_nt(dh, bl)
    db = _bdot_tn(jnp.concatenate([ah, ah, al], axis=1), jnp.concatenate([dh, dl, dh], axis=1))
    return da, db


_mm3.defvjp(_mm3_fwd, _mm3_bwd)


def _mm_exact(c3, b):
    hi, lo = _split(b)
    lo2 = (b - hi.astype(F32) - lo.astype(F32)).astype(BF)
    return _bdot(c3, jnp.concatenate([hi, lo, lo2], axis=-2))


@jax.custom_vjp
def _cumsum_rows(b):
    return _mm_exact(_tri3(True), b)


def _tri3(lower):
    ri = lax.broadcasted_iota(I32, (HEADS, CH, CH), 1)
    ci = lax.broadcasted_iota(I32, (HEADS, CH, CH), 2)
    tri = (ri >= ci if lower else ri <= ci).astype(BF)
    return jnp.concatenate([tri, tri, tri], axis=-1)


_cumsum_rows.defvjp(lambda b: (_cumsum_rows(b), None), lambda _, dc: (_mm_exact(_tri3(False), dc),))


CPS = 2


def _gdn_chunk(yc, gz, gba, alog, dtb):
    def heads(t, off):
        return jnp.stack([t[:, off + h * HD: off + (h + 1) * HD] for h in range(HEADS)])

    def cols(t, off):
        return jnp.stack([jnp.broadcast_to(t[:, off + h: off + h + 1], (CH, CH)) for h in range(HEADS)])

    c = yc * _sigmoid(yc)
    q, k, v, zg = heads(c, 0), heads(c, HEADS * HD), heads(c, 2 * HEADS * HD), heads(gz, 0)
    q = q * lax.rsqrt(jnp.sum(q * q, axis=-1, keepdims=True) + RMS_EPS) * (HD ** -0.5)
    k = k * lax.rsqrt(jnp.sum(k * k, axis=-1, keepdims=True) + RMS_EPS)
    beta = cols(_sigmoid(gba), 0)
    g = cols(-jnp.exp(alog) * _softplus(gba + dtb), HEADS)
    ri = lax.broadcasted_iota(I32, (HEADS, CH, CH), 1)
    ci = lax.broadcasted_iota(I32, (HEADS, CH, CH), 2)
    causal, strict = ri >= ci, ri > ci
    eye = (ri == ci).astype(F32)
    gc = _cumsum_rows(g)
    gr = _t(gc)
    decay = jnp.where(causal, jnp.exp(jnp.where(causal, gc - gr, 0.0)), 0.0)
    lower = jnp.where(strict, beta * _mm1_nt(k, k) * decay, 0.0)
    pw = -lower
    inv = eye + pw
    pw = _mm3(pw, pw)
    for _ in range(4):
        both = _mm3(jnp.concatenate([inv, pw], axis=1), pw)
        inv, pw = inv + both[:, :CH], both[:, CH:]
    inv = inv + _mm3(inv, pw)
    eg = jnp.exp(gc)
    uw = _mm3(inv, jnp.concatenate([v * beta, k * (beta * eg)], axis=2))
    u, w = uw[:, :, :HD], uw[:, :, HD:]
    qk = jnp.where(causal, _mm1_nt(q, k) * decay, 0.0)
    g_last = gc[:, CH - 1:CH, :]
    return u, jnp.concatenate([w, q * eg], axis=1), qk, k * jnp.exp(g_last - gc), jnp.exp(g_last), zg * _sigmoid(zg)


def _gdn_advance(state, pre, normw):
    u, wq, qk, kd, last, gate = pre
    ws = _mm1(wq, state)
    v_new = u - ws[:, :CH]
    o = ws[:, CH:] + _mm1(qk, v_new)
    new_state = state * last + _mm1_tn(kd, v_new)
    o = o * lax.rsqrt(jnp.mean(o * o, axis=-1, keepdims=True) + RMS_EPS) * normw * gate
    return jnp.concatenate([o[h] for h in range(HEADS)], axis=1), new_state


def _gdn_chunks(state, yc, gz, gba, alog, dtb, normw):
    rows = [slice(c * CH, (c + 1) * CH) for c in range(yc.shape[0] // CH)]
    pres = [_gdn_chunk(yc[r], gz[r], gba[r], alog, dtb) for r in rows]
    outs = []
    for pre in pres:
        o, state = _gdn_advance(state, pre, normw)
        outs.append(o)
    return jnp.concatenate(outs, axis=0), state


def _gdn_fwd(yc, m, alog, dtb, normw):
    S = yc.shape[0]
    RS = CPS * CH
    nch = S // RS

    def body(y_ref, gz_ref, gba_ref, al_ref, dt_ref, nw_ref, o_ref, st_ref, st):
        @pl.when(pl.program_id(0) == 0)
        def _():
            st[...] = jnp.zeros_like(st)

        cur = st[...]
        st_ref[0] = cur
        o, new = _gdn_chunks(cur, y_ref[...], gz_ref[...], gba_ref[...], al_ref[...], dt_ref[...], nw_ref[...])
        o_ref[...] = o
        st[...] = new

    return pl.pallas_call(
        body, name="gdn_fwd", grid=(nch,),
        in_specs=[pl.BlockSpec((RS, CW), lambda n: (n, 0)), pl.BlockSpec((RS, HEADS * HD), lambda n: (n, 6)),
                  pl.BlockSpec((RS, 128), lambda n: (n, 28)), pl.BlockSpec((1, 128), lambda n: (0, 0)),
                  pl.BlockSpec((1, 128), lambda n: (0, 0)), pl.BlockSpec((1, HD), lambda n: (0, 0))],
        out_specs=[pl.BlockSpec((RS, HEADS * HD), lambda n: (n, 0)), pl.BlockSpec((1, HEADS, HD, HD), lambda n: (n, 0, 0, 0))],
        out_shape=[jax.ShapeDtypeStruct((S, HEADS * HD), F32), jax.ShapeDtypeStruct((nch, HEADS, HD, HD), F32)],
        scratch_shapes=[pltpu.VMEM((HEADS, HD, HD), F32)],
    )(yc, m, m, alog, dtb, normw)


def _gdn_bwd(yc, m, alog, dtb, normw, states, dog):
    S = yc.shape[0]
    RS = CPS * CH
    nch = S // RS

    def body(y_ref, gz_ref, gba_ref, al_ref, dt_ref, nw_ref, st_ref, do_ref, dy_ref, dgz_ref, dgba_ref, dal_ref, ddt_ref, dnw_ref, dst):
        n = pl.program_id(0)

        @pl.when(n == 0)
        def _():
            dst[...] = jnp.zeros_like(dst)

        _, vjp = jax.vjp(_gdn_chunks, st_ref[0], y_ref[...], gz_ref[...], gba_ref[...], al_ref[...], dt_ref[...], nw_ref[...])
        d_state, d_y, d_gz, d_gba, d_al, d_dt, d_nw = vjp((do_ref[...], dst[...]))
        dst[...] = d_state
        dy_ref[...] = d_y
        dgz_ref[...] = d_gz.astype(dgz_ref.dtype)
        dgba_ref[...] = d_gba.astype(dgba_ref.dtype)
        for ref, val in ((dal_ref, d_al), (ddt_ref, d_dt), (dnw_ref, d_nw)):
            @pl.when(n == 0)
            def _(ref=ref, val=val):
                ref[...] = val

            @pl.when(n > 0)
            def _(ref=ref, val=val):
                ref[...] += val

    rev = lambda n: nch - 1 - n
    return pl.pallas_call(
        body, name="gdn_bwd", grid=(nch,),
        in_specs=[pl.BlockSpec((RS, CW), lambda n: (rev(n), 0)), pl.BlockSpec((RS, HEADS * HD), lambda n: (rev(n), 6)),
                  pl.BlockSpec((RS, 128), lambda n: (rev(n), 28)), pl.BlockSpec((1, 128), lambda n: (0, 0)),
                  pl.BlockSpec((1, 128), lambda n: (0, 0)), pl.BlockSpec((1, HD), lambda n: (0, 0)),
                  pl.BlockSpec((1, HEADS, HD, HD), lambda n: (rev(n), 0, 0, 0)),
                  pl.BlockSpec((RS, HEADS * HD), lambda n: (rev(n), 0))],
        out_specs=[pl.BlockSpec((RS, CW), lambda n: (rev(n), 0)), pl.BlockSpec((RS, HEADS * HD), lambda n: (rev(n), 0)),
                   pl.BlockSpec((RS, 128), lambda n: (rev(n), 0)), pl.BlockSpec((1, 128), lambda n: (0, 0)),
                   pl.BlockSpec((1, 128), lambda n: (0, 0)), pl.BlockSpec((1, HD), lambda n: (0, 0))],
        out_shape=[jax.ShapeDtypeStruct((S, CW), F32), jax.ShapeDtypeStruct((S, HEADS * HD), BF),
                   jax.ShapeDtypeStruct((S, 128), BF), jax.ShapeDtypeStruct((1, 128), F32),
                   jax.ShapeDtypeStruct((1, 128), F32), jax.ShapeDtypeStruct((1, HD), F32)],
        scratch_shapes=[pltpu.VMEM((HEADS, HD, HD), F32)],
    )(yc, m, m, alog, dtb, normw, states, dog)


def _mixer_fwd(h1, wp, b_gate, conv_w, alog, dtb, normw, w_sb, w_gdn, w_mo, g, b):
    S = h1.shape[0]
    tm = min(512, S)
    n = HEADS * HD
    tb_ = min(BIG_ROWS, S)
    m, = _mm("mix_in", h1, wp, S, NP, D, tm=tb_, tn=1536, tk=D, order="ji", outs=[((S, NP), F32, (tb_, 1536), _tile)])
    kb_all, vb_all, kt_all, vt_all = _attn_prep(m)
    ya, runs = _attn_fwd_t(m, kb_all, vt_all)
    runs = (kb_all, vb_all, kt_all, runs)
    yc = _conv_fwd(m, conv_w)
    og, states = _gdn_fwd(yc, m, alog, dtb, normw)
    ysb, = _mm("mix_sb", ya, w_sb, S, D, n, tm=tm, tn=D, tk=n, outs=[((S, D), F32, (tm, D), _row)])

    def merge_epi(acc, ys, gs, gg, bg):
        return _sigmoid(gs + bg[:, :D]) * ys + _sigmoid(gg + bg[:, D:]) * acc, acc

    u, ygdn = _mm("mix_gdn", og, w_gdn, S, D, n, tm=tm, tn=D, tk=n,
                  extras=[(ysb, (tm, D), _row), (m, (tm, D), lambda i, j, k: (i, GATE0 // D)),
                          (m, (tm, D), lambda i, j, k: (i, GATE0 // D + 1)), (b_gate, (1, 2 * D), _const)],
                  epilogue=merge_epi, outs=[((S, D), BF, (tm, D), _row), ((S, D), F32, (tm, D), _row)])

    def out_epi(acc, xin, gg, bb):
        r = ALPHA * xin + acc
        return _layer_norm(r, gg, bb), r

    h2, r2 = _mm("mix_out", u, w_mo, S, D, D, tm=tm, tn=D, tk=D,
                 extras=[(h1, (tm, D), _row), (g, (1, D), _const), (b, (1, D), _const)], epilogue=out_epi,
                 outs=[((S, D), F32, (tm, D), _row), ((S, D), F32, (tm, D), _row)])
    return h2, (h1, m, ya, runs, yc, og, states, ysb, ygdn, u, r2)


def _mixer_bwd(saved, wp, b_gate, conv_w, alog, dtb, normw, w_sb, w_gdn, w_mo, g, dh):
    h1, m, ya, runs, yc, og, states, ysb, ygdn, u, r2 = saved
    S = h1.shape[0]
    tm = min(512, S)
    n = HEADS * HD
    dr, dg, db = _ln_bwd("mix", r2, g, dh)

    def merge_epi(du, ys, yg, gs, gg, bg):
        s1, s2 = _sigmoid(gs + bg[:, :D]), _sigmoid(gg + bg[:, D:])
        dgate = jnp.concatenate([du * ys * s1 * (1.0 - s1), du * yg * s2 * (1.0 - s2)], axis=1)
        return du * s1, du * s2, dgate, jnp.sum(dgate, axis=0, keepdims=True)

    dysb, dygdn, dgate, d_bg = _mm(
        "mix_dmerge", dr, w_mo, S, D, D, tm=tm, tn=D, tk=D, tb=True,
        extras=[(ysb, (tm, D), _row), (ygdn, (tm, D), _row), (m, (tm, D), lambda i, j, k: (i, GATE0 // D)),
                (m, (tm, D), lambda i, j, k: (i, GATE0 // D + 1)), (b_gate, (1, 2 * D), _const)],
        epilogue=merge_epi, n_acc=1,
        outs=[((S, D), BF, (tm, D), _row), ((S, D), BF, (tm, D), _row), ((S, 2 * D), BF, (tm, 2 * D), _row),
              ((1, 2 * D), F32, (1, 2 * D), _const)])
    d_w_mo, = _mm("mix_dwmo", u, dr, D, D, S, tm=D, tn=D, tk=min(BIG_ROWS, S), ta=True, outs=[((D, D), F32, (D, D), _tile)])
    dya, = _mm("mix_dya", dysb, w_sb, S, n, D, tm=tm, tn=n, tk=D, tb=True, outs=[((S, n), F32, (tm, n), _row)])
    col_shards = [((NSH, n, D // NSH), F32, (None, n, D // NSH), lambda i, j, k: (j, 0, 0))]
    d_w_sb, = _mm("mix_dwsb", ya, dysb, n, D, S, tm=n, tn=D // NSH, tk=min(BIG_ROWS, S), ta=True, order="ji", outs=col_shards)
    dog, = _mm("mix_dog", dygdn, w_gdn, S, n, D, tm=tm, tn=n, tk=D, tb=True, outs=[((S, n), F32, (tm, n), _row)])
    d_w_gdn, = _mm("mix_dwgdn", og, dygdn, n, D, S, tm=n, tn=D // NSH, tk=min(BIG_ROWS, S), ta=True, order="ji", outs=col_shards)
    dq, dk, dv = _attn_bwd_t(m, *runs, dya)
    dyc, dgz, dgba, d_alog, d_dtb, d_normw = _gdn_bwd(yc, m, alog, dtb, normw, states, dog)
    dxc, d_conv = _conv_bwd(m, dyc, conv_w)
    dm = jnp.concatenate([dq, dk.astype(BF), dv.astype(BF), dxc, dgz, dgba, jnp.zeros((S, GATE0 - 3712), BF), dgate], axis=1)
    tb_ = min(BIG_ROWS, S)
    d_h1, = _mm("mix_dh", dm, wp, S, D, NP, tm=tb_, tn=D, tk=1536, tb=True,
                extras=[(dr, (tb_, D), _row)], epilogue=lambda acc, d: (acc + ALPHA * d,),
                outs=[((S, D), F32, (tb_, D), _row)])
    d_wp, = _mm("mix_dwp", h1, dm, D, NP, S, tm=D, tn=1536, tk=tb_, ta=True, order="ji",
                outs=[((D, NP), F32, (D, 1536), _tile)])
    return d_h1, dict(wp=d_wp, b_gate=d_bg, conv_w=d_conv, alog=d_alog, dtb=d_dtb, normw=d_normw,
                      w_sb=d_w_sb, w_gdn=d_w_gdn, w_mo=d_w_mo, g=dg, b=db)


def _ple_fwd(h3, p, w_pg, b_pg, w_ple, g, b, target):
    S = h3.shape[0]
    tm = min(512, S)
    pd = p.shape[1]
    pe, = _mm("ple_emb", p, w_ple, S, D, pd, tm=tm, tn=D, tk=pd, outs=[((S, D), F32, (tm, D), _row)])

    def epi(acc, e, xin, tgt, bp, gg, bb):
        gt = _sigmoid(acc + bp)
        r = ALPHA * xin + gt * e
        diff = _layer_norm(r, gg, bb) - tgt
        return gt, r, diff * (1.0 / D), jnp.sum(diff * diff, axis=0, keepdims=True)

    gt, r4, dh4, loss_row = _mm(
        "ple_gate", h3, w_pg, S, D, D, tm=tm, tn=D, tk=D,
        extras=[(pe, (tm, D), _row), (h3, (tm, D), _row), (target, (tm, D), _row), (b_pg, (1, D), _const),
                (g, (1, D), _const), (b, (1, D), _const)], epilogue=epi, n_acc=1,
        outs=[((S, D), F32, (tm, D), _row), ((S, D), F32, (tm, D), _row), ((S, D), F32, (tm, D), _row),
              ((1, D), F32, (1, D), _const)])
    return dh4, loss_row, (h3, p, pe, gt, r4)


def _ple_bwd(saved, w_pg, g, dh4):
    h3, p, pe, gt, r4 = saved
    S = h3.shape[0]
    tm = min(512, S)
    pd = p.shape[1]

    def fn(r, gg, dh, e, t):
        dr, dg, db = _layer_norm_bwd(r, gg, dh)
        dpre = dr * e * t * (1.0 - t)
        return dr, dpre, dr * t, dg, db, jnp.sum(dpre, axis=0, keepdims=True)

    row, one = (lambda i: (i, 0)), (lambda i: (0, 0))
    dr, dpre, dpe, dg, db, d_bpg = _rows(
        "ple_lnbwd", fn, S // tm,
        [(r4, (tm, D), row), (g, (1, D), one), (dh4, (tm, D), row), (pe, (tm, D), row), (gt, (tm, D), row)],
        [((S, D), F32, (tm, D), row), ((S, D), BF, (tm, D), row), ((S, D), BF, (tm, D), row),
         ((1, D), F32, (1, D), one), ((1, D), F32, (1, D), one), ((1, D), F32, (1, D), one)], n_acc=3)
    d_w_pg, = _mm("ple_dwpg", h3, dpre, D, D, S, tm=D, tn=D, tk=min(BIG_ROWS, S), ta=True, outs=[((D, D), F32, (D, D), _tile)])
    d_w_ple, = _mm("ple_dwple", p, dpe, pd, D, S, tm=pd, tn=D // NSH, tk=min(BIG_ROWS, S), ta=True, order="ji",
                   outs=[((NSH, pd, D // NSH), F32, (None, pd, D // NSH), lambda i, j, k: (j, 0, 0))])
    d_h3, = _mm("ple_dh", dpre, w_pg, S, D, D, tm=tm, tn=D, tk=D, tb=True,
                extras=[(dr, (tm, D), _row)], epilogue=lambda acc, d: (acc + ALPHA * d,),
                outs=[((S, D), F32, (tm, D), _row)])
    return d_h3, d_w_pg, d_bpg, d_w_ple, dg, db


def _local_step(x, p, target, W):
    h1, sv1 = _ffn_fwd("ffn1", x, W["ffn1_in"], W["ffn1_out"], W["ln1_g"], W["ln1_b"])
    h2, sv2 = _mixer_fwd(h1, W["wp"], W["b_gate"], W["conv_w"], W["alog"], W["dtb"], W["normw"],
                         W["w_sb"], W["w_gdn"], W["w_mo"], W["ln2_g"], W["ln2_b"])
    h3, sv3 = _ffn_fwd("ffn2", h2, W["ffn2_in"], W["ffn2_out"], W["ln3_g"], W["ln3_b"])
    dh4, loss_row, sv4 = _ple_fwd(h3, p, W["w_pg"], W["b_pg"], W["w_ple"], W["ln4_g"], W["ln4_b"], target)
    G = {}
    dh3, G["w_pg"], G["b_pg"], G["w_ple"], G["ln4_g"], G["ln4_b"] = _ple_bwd(sv4, W["w_pg"], W["ln4_g"], dh4)
    dh2, G["ffn2_in"], G["ffn2_out"], G["ln3_g"], G["ln3_b"] = _ffn_bwd("ffn2", sv3, W["ffn2_in"], W["ffn2_out"], W["ln3_g"], dh3)
    dh1, gm = _mixer_bwd(sv2, W["wp"], W["b_gate"], W["conv_w"], W["alog"], W["dtb"], W["normw"],
                         W["w_sb"], W["w_gdn"], W["w_mo"], W["ln2_g"], dh2)
    G.update({k: v for k, v in gm.items() if k not in ("g", "b")})
    G["ln2_g"], G["ln2_b"] = gm["g"], gm["b"]
    dx, G["ffn1_in"], G["ffn1_out"], G["ln1_g"], G["ln1_b"] = _ffn_bwd("ffn1", sv1, W["ffn1_in"], W["ffn1_out"], W["ln1_g"], dh1)
    return loss_row, dx, G


S2 = CUT - 2 * MSH


def _pack_wp(w4):
    tr = 256

    def fn(w):
        s = [w[j].astype(F32) for j in range(NSH)]
        full = jnp.concatenate([s[0], s[1], s[2][:, :S2], jnp.zeros((tr, GATE0 - CUT), F32), s[2][:, S2:], s[3]], axis=1)
        return (full,)

    return _rows("pack_wp", fn, D // tr, [(w4, (NSH, tr, MSH), lambda i: (0, i, 0))],
                 [((D, NP), BF, (tr, NP), lambda i: (i, 0))])[0]


def _unpack_wp(d):
    tr = 256
    g2 = GATE0 + MSH - S2

    def fn(v):
        return (jnp.stack([v[:, :MSH], v[:, MSH:2 * MSH], jnp.concatenate([v[:, 2 * MSH:CUT], v[:, GATE0:g2]], axis=1), v[:, g2:]]),)

    return _rows("unpack_wp", fn, D // tr, [(d, (tr, NP), lambda i: (i, 0))],
                 [((NSH, D, MSH), F32, (NSH, tr, MSH), lambda i: (0, i, 0))])[0]


def _cast_bf16(tag, w):
    r, c = w.shape
    tr = _pick(r, 256)
    return _rows(f"cast_{tag}", lambda v: (v,), r // tr, [(w, (tr, c), lambda i: (i, 0))],
                 [((r, c), BF, (tr, c), lambda i: (i, 0))])[0]


ANY = pl.BlockSpec(memory_space=pl.ANY)


def _place():
    return lax.axis_index("x"), lax.axis_index("y"), lax.axis_index("c")


def _gather_shards(shards):
    n = len(shards)

    def body(*refs):
        ins, outs = refs[:n], refs[n:2 * n]
        send, recv, loc = refs[2 * n:]
        x, y, c = _place()
        me = 2 * x + y
        chips = [(1 - x, y), (x, 1 - y), (1 - x, 1 - y)]
        started = []
        for t in range(n):
            lc = pltpu.make_async_copy(ins[t], outs[t].at[me], loc.at[t])
            lc.start()
            started.append(lc)
            for q, (px, py) in enumerate(chips):
                cp = pltpu.make_async_remote_copy(src_ref=ins[t], dst_ref=outs[t].at[me], send_sem=send.at[3 * t + q],
                                                  recv_sem=recv.at[3 * t + q], device_id=(px, py, c), device_id_type=MESH)
                cp.start()
                started.append(cp)
        for t in range(n):
            for q, (px, py) in enumerate(chips):
                pltpu.make_async_remote_copy(src_ref=ins[t], dst_ref=outs[t].at[2 * px + py], send_sem=send.at[3 * t + q],
                                             recv_sem=recv.at[3 * t + q], device_id=(px, py, c), device_id_type=MESH).wait_recv()
        for t in range(n):
            started[4 * t].wait()
            for q in range(3):
                started[4 * t + 1 + q].wait_send()

    return pl.pallas_call(
        body, name="gather_weights", in_specs=[ANY] * n, out_specs=[ANY] * n,
        out_shape=[jax.ShapeDtypeStruct((NSH,) + s.shape, s.dtype) for s in shards],
        scratch_shapes=[pltpu.SemaphoreType.DMA((3 * n,)), pltpu.SemaphoreType.DMA((3 * n,)), pltpu.SemaphoreType.DMA((n,))],
    )(*shards)


def _reduce_sibling(gs):
    n = len(gs)

    def body(*refs):
        ins, mine, theirs = refs[:n], refs[n:2 * n], refs[2 * n:3 * n]
        send, recv, loc = refs[3 * n:]
        x, y, c = _place()
        started = []
        for t in range(n):
            h = gs[t].shape[1] // 2
            lc = pltpu.make_async_copy(ins[t].at[:, pl.ds(pl.multiple_of(c * h, 8), h), :], mine[t], loc.at[t])
            cp = pltpu.make_async_remote_copy(src_ref=ins[t].at[:, pl.ds(pl.multiple_of((1 - c) * h, 8), h), :], dst_ref=theirs[t],
                                              send_sem=send.at[t], recv_sem=recv.at[t], device_id=(x, y, 1 - c), device_id_type=MESH)
            lc.start()
            cp.start()
            started += [lc, cp]
        for t in range(n):
            started[2 * t].wait()
            started[2 * t + 1].wait()

    half = [jax.ShapeDtypeStruct((NSH, g.shape[1] // 2, g.shape[2]), g.dtype) for g in gs]
    res = pl.pallas_call(
        body, name="reduce_sibling", in_specs=[ANY] * n, out_specs=[ANY] * (2 * n), out_shape=half + half,
        scratch_shapes=[pltpu.SemaphoreType.DMA((n,)), pltpu.SemaphoreType.DMA((n,)), pltpu.SemaphoreType.DMA((n,))],
    )(*gs)
    return res[:n], res[n:]


def _reduce_chips(ps):
    n = len(ps)

    def body(*refs):
        ins, outs = refs[:n], refs[n:2 * n]
        send, recv, loc = refs[2 * n:]
        x, y, c = _place()
        me = 2 * x + y
        chips = [(1 - x, y), (x, 1 - y), (1 - x, 1 - y)]
        started = []
        for t in range(n):
            lc = pltpu.make_async_copy(ins[t].at[me], outs[t].at[me], loc.at[t])
            lc.start()
            started.append(lc)
            for q, (px, py) in enumerate(chips):
                cp = pltpu.make_async_remote_copy(src_ref=ins[t].at[2 * px + py], dst_ref=outs[t].at[me], send_sem=send.at[3 * t + q],
                                                  recv_sem=recv.at[3 * t + q], device_id=(px, py, c), device_id_type=MESH)
                cp.start()
                started.append(cp)
        for t in range(n):
            for q, (px, py) in enumerate(chips):
                pltpu.make_async_remote_copy(src_ref=ins[t].at[me], dst_ref=outs[t].at[2 * px + py], send_sem=send.at[3 * t + q],
                                             recv_sem=recv.at[3 * t + q], device_id=(px, py, c), device_id_type=MESH).wait_recv()
        for t in range(n):
            started[4 * t].wait()
            for q in range(3):
                started[4 * t + 1 + q].wait_send()

    return pl.pallas_call(
        body, name="reduce_chips", in_specs=[ANY] * n, out_specs=[ANY] * n,
        out_shape=[jax.ShapeDtypeStruct(p_.shape, p_.dtype) for p_ in ps],
        scratch_shapes=[pltpu.SemaphoreType.DMA((3 * n,)), pltpu.SemaphoreType.DMA((3 * n,)), pltpu.SemaphoreType.DMA((n,))],
    )(*ps)


def _share_sibling(fs):
    n = len(fs)

    def body(*refs):
        ins, outs = refs[:n], refs[n:2 * n]
        send, recv, loc = refs[2 * n:]
        x, y, c = _place()
        started = []
        for t in range(n):
            lc = pltpu.make_async_copy(ins[t], outs[t].at[c], loc.at[t])
            cp = pltpu.make_async_remote_copy(src_ref=ins[t], dst_ref=outs[t].at[c], send_sem=send.at[t], recv_sem=recv.at[t],
                                              device_id=(x, y, 1 - c), device_id_type=MESH)
            lc.start()
            cp.start()
            started += [lc, cp]
        for t in range(n):
            started[2 * t].wait()
            started[2 * t + 1].wait_send()
            pltpu.make_async_remote_copy(src_ref=ins[t], dst_ref=outs[t].at[1 - c], send_sem=send.at[t], recv_sem=recv.at[t],
                                         device_id=(x, y, 1 - c), device_id_type=MESH).wait_recv()

    return pl.pallas_call(
        body, name="share_sibling", in_specs=[ANY] * n, out_specs=[ANY] * n,
        out_shape=[jax.ShapeDtypeStruct((2,) + f.shape, f.dtype) for f in fs],
        scratch_shapes=[pltpu.SemaphoreType.DMA((n,)), pltpu.SemaphoreType.DMA((n,)), pltpu.SemaphoreType.DMA((n,))],
    )(*fs)


def _chunk_rows(h, c):
    return _pick(h, max(8, 262144 // c // 8 * 8))


def _sibling_sum(tag, g):
    _, r, c = g.shape
    h = r // 2
    tr = _chunk_rows(h, c)
    nch = h // tr
    steps = NSH * nch

    def body(top_ref, bot_ref, out_ref, narrow_ref, land, send, recv):
        s = pl.program_id(0)
        x, y, core = _place()

        def exchange(keep_ref, give_ref):
            cp = pltpu.make_async_remote_copy(src_ref=give_ref.at[0], dst_ref=land.at[s], send_sem=send.at[s], recv_sem=recv.at[s],
                                              device_id=(x, y, 1 - core), device_id_type=MESH)
            cp.start()
            cp.wait()
            total = keep_ref[0] + land[s]
            out_ref[0] = total
            narrow_ref[0] = total.astype(BF)

        @pl.when(core == 0)
        def _():
            exchange(top_ref, bot_ref)

        @pl.when(core == 1)
        def _():
            exchange(bot_ref, top_ref)

    return pl.pallas_call(
        body, name=f"sibling_sum_{tag}", grid=(steps,),
        in_specs=[pl.BlockSpec((1, tr, c), lambda s: (s // nch, s % nch, 0)),
                  pl.BlockSpec((1, tr, c), lambda s: (s // nch, nch + s % nch, 0))],
        out_specs=[pl.BlockSpec((1, tr, c), lambda s: (s // nch, s % nch, 0))] * 2,
        out_shape=[jax.ShapeDtypeStruct((NSH, h, c), F32), jax.ShapeDtypeStruct((NSH, h, c), BF)],
        scratch_shapes=[pltpu.VMEM((steps, tr, c), F32), pltpu.SemaphoreType.DMA((steps,)), pltpu.SemaphoreType.DMA((steps,))],
    )(g, g)


def _chip_sum_share(tag, b, own):
    _, h, c = b.shape
    tr = _chunk_rows(h, c)
    steps = h // tr

    def body(b_ref, own_ref, out_ref, stage, land, send, recv):
        s = pl.program_id(0)
        x, y, core = _place()
        me = 2 * x + y
        v = [jnp.where(me == j, own_ref[j], b_ref[j].astype(F32)) for j in range(NSH)]
        total = ((v[0] + v[1]) + v[2]) + v[3]
        stage[...] = total
        cp = pltpu.make_async_remote_copy(src_ref=stage, dst_ref=land.at[s], send_sem=send.at[s], recv_sem=recv.at[s],
                                          device_id=(x, y, 1 - core), device_id_type=MESH)
        cp.start()
        cp.wait()
        out_ref[core] = total
        out_ref[1 - core] = land[s]

    return pl.pallas_call(
        body, name=f"chip_sum_share_{tag}", grid=(steps,),
        in_specs=[pl.BlockSpec((NSH, tr, c), lambda s: (0, s, 0))] * 2,
        out_specs=pl.BlockSpec((2, tr, c), lambda s: (0, s, 0)),
        out_shape=jax.ShapeDtypeStruct((2, h, c), F32),
        scratch_shapes=[pltpu.VMEM((tr, c), F32), pltpu.VMEM((steps, tr, c), F32), pltpu.SemaphoreType.DMA((steps,)),
                        pltpu.SemaphoreType.DMA((steps,))],
    )(b, own)


NDEV = 8


def _allreduce_small(pack):
    r, w = pack.shape
    rel = [(dx, dy, dc) for dx in (0, 1) for dy in (0, 1) for dc in (0, 1) if (dx, dy, dc) != (0, 0, 0)]

    def body(in_ref, out_ref, buf, send, recv):
        x, y, c = _place()
        me = 4 * x + 2 * y + c
        buf[me] = in_ref[...]
        peers = [((x + dx) % 2, (y + dy) % 2, (c + dc) % 2) for dx, dy, dc in rel]
        sent = []
        for k, peer in enumerate(peers):
            cp = pltpu.make_async_remote_copy(src_ref=in_ref, dst_ref=buf.at[me], send_sem=send.at[k], recv_sem=recv.at[k],
                                              device_id=peer, device_id_type=MESH)
            cp.start()
            sent.append(cp)
        for k, (px, py, pc) in enumerate(peers):
            pltpu.make_async_remote_copy(src_ref=in_ref, dst_ref=buf.at[4 * px + 2 * py + pc], send_sem=send.at[k], recv_sem=recv.at[k],
                                         device_id=(px, py, pc), device_id_type=MESH).wait_recv()
        for cp in sent:
            cp.wait_send()
        acc = buf[0]
        for k in range(1, NDEV):
            acc = acc + buf[k]
        out_ref[...] = acc

    vm = pl.BlockSpec(memory_space=pltpu.VMEM)
    return pl.pallas_call(
        body, name="allreduce_small", in_specs=[vm], out_specs=vm, out_shape=jax.ShapeDtypeStruct((r, w), F32),
        scratch_shapes=[pltpu.VMEM((NDEV, r, w), F32), pltpu.SemaphoreType.DMA((NDEV - 1,)), pltpu.SemaphoreType.DMA((NDEV - 1,))],
    )(pack)


def _add2(tag, a, b):
    _, h, c = a.shape
    tr = _pick(h, max(8, 262144 // c // 8 * 8))
    spec = ((None, tr, c), lambda i: (i // (h // tr), i % (h // tr), 0))
    return _rows(f"add2_{tag}", lambda u, v: (u + v,), NSH * (h // tr), [(a,) + spec, (b,) + spec], [(a.shape, F32) + spec])[0]


def _add4(tag, a):
    _, h, c = a.shape
    tr = _pick(h, max(8, 131072 // c // 8 * 8))
    return _rows(f"add4_{tag}", lambda v: (((v[0] + v[1]) + v[2]) + v[3],), h // tr, [(a, (NSH, tr, c), lambda i: (0, i, 0))],
                 [((h, c), F32, (tr, c), lambda i: (i, 0))])[0]


def _adamw(tag, w, g, m, v):
    r, c = w.shape
    tr = _pick(r, max(8, 262144 // c // 8 * 8))

    def fn(w_, g_, m_, v_):
        m2 = B1 * m_ + (1.0 - B1) * g_
        v2 = B2 * v_ + (1.0 - B2) * (g_ * g_)
        m_hat = m2 / (1.0 - B1 ** STEP)
        v_hat = v2 / (1.0 - B2 ** STEP)
        return -LR * (m_hat / (jnp.sqrt(v_hat) + EPS) + WD * w_), m2, v2

    spec = ((tr, c), lambda i: (i, 0))
    return _rows(f"adamw_{tag}", fn, r // tr, [(a,) + spec for a in (w, g, m, v)], [((r, c), F32) + spec] * 3)


BIG = ("ffn1_w_in", "ffn1_w_out", "w_mix_in", "w_branch_sb", "w_branch_gdn", "w_mix_out", "ffn2_w_in", "ffn2_w_out",
       "w_ple_gate", "w_ple")
SMALL = ("ln1_g", "ln1_b", "b_gate", "conv_w", "a_log", "dt_bias", "gdn_norm_w", "ln2_g", "ln2_b", "ln3_g", "ln3_b",
         "b_ple_gate", "ln4_g", "ln4_b")
ORDER = ("ffn1_w_in", "ffn1_w_out", "ln1_g", "ln1_b", "w_mix_in", "b_gate", "conv_w", "a_log", "dt_bias", "gdn_norm_w",
         "w_branch_sb", "w_branch_gdn", "w_mix_out", "ln2_g", "ln2_b", "ffn2_w_in", "ffn2_w_out", "ln3_g", "ln3_b",
         "w_ple_gate", "b_ple_gate", "w_ple", "ln4_g", "ln4_b")
PACK_W = 2304


def _lane_row(v, lanes=128, at=HEADS):
    return jnp.pad(v[None, :], ((0, 0), (at, lanes - at - v.shape[0])))


def _col_join(w4):
    return jnp.transpose(w4, (1, 0, 2)).reshape(w4.shape[1], NSH * w4.shape[2])


def kernel(x, p, ffn1_w_in, ffn1_w_out, ln1_g, ln1_b, w_mix_in, b_gate, conv_w, a_log, dt_bias, gdn_norm_w, w_branch_sb, w_branch_gdn, w_mix_out, ln2_g, ln2_b, ffn2_w_in, ffn2_w_out, ln3_g, ln3_b, w_ple_gate, b_ple_gate, w_ple, ln4_g, ln4_b, loss_target, m_ffn1_w_in, m_ffn1_w_out, m_ln1_g, m_ln1_b, m_w_mix_in, m_b_gate, m_conv_w, m_a_log, m_dt_bias, m_gdn_norm_w, m_w_branch_sb, m_w_branch_gdn, m_w_mix_out, m_ln2_g, m_ln2_b, m_ffn2_w_in, m_ffn2_w_out, m_ln3_g, m_ln3_b, m_w_ple_gate, m_b_ple_gate, m_w_ple, m_ln4_g, m_ln4_b, v_ffn1_w_in, v_ffn1_w_out, v_ln1_g, v_ln1_b, v_w_mix_in, v_b_gate, v_conv_w, v_a_log, v_dt_bias, v_gdn_norm_w, v_w_branch_sb, v_w_branch_gdn, v_w_mix_out, v_ln2_g, v_ln2_b, v_ffn2_w_in, v_ffn2_w_out, v_ln3_g, v_ln3_b, v_w_ple_gate, v_b_ple_gate, v_w_ple, v_ln4_g, v_ln4_b):
    args = dict(locals())
    w = {n: args[n][0] for n in ORDER}
    mom = {n: args["m_" + n][0] for n in ORDER}
    var = {n: args["v_" + n][0] for n in ORDER}

    shards = [_cast_bf16(n, w[n]) for n in BIG] + [w["conv_w"]]
    full = dict(zip(BIG + ("conv_w",), _gather_shards(shards)))
    W = dict(
        ffn1_in=full["ffn1_w_in"], ffn1_out=full["ffn1_w_out"].reshape(DFF, D),
        ffn2_in=full["ffn2_w_in"], ffn2_out=full["ffn2_w_out"].reshape(DFF, D),
        wp=_pack_wp(full["w_mix_in"]), w_sb=_col_join(full["w_branch_sb"]), w_gdn=_col_join(full["w_branch_gdn"]),
        w_mo=full["w_mix_out"].reshape(D, D), w_pg=full["w_ple_gate"].reshape(D, D), w_ple=_col_join(full["w_ple"]),
        conv_w=_col_join(full["conv_w"]), b_gate=w["b_gate"][None], alog=_lane_row(w["a_log"]), dtb=_lane_row(w["dt_bias"]),
        normw=w["gdn_norm_w"][None], b_pg=w["b_ple_gate"][None],
        **{f"ln{i}_{s}": w[f"ln{i}_{s}"][None] for i in (1, 2, 3, 4) for s in ("g", "b")},
    )

    loss_row, grad_x, G = _local_step(x[0], p[0, 0], loss_target[0], W)
    loss = lax.psum(0.5 * jnp.sum(loss_row) / D, ("x", "y", "c"))

    big = dict(
        ffn1_w_in=G["ffn1_in"], ffn1_w_out=G["ffn1_out"].reshape(NSH, DFF // NSH, D), w_mix_in=_unpack_wp(G["wp"]),
        w_branch_sb=G["w_sb"], w_branch_gdn=G["w_gdn"], w_mix_out=G["w_mo"].reshape(NSH, D // NSH, D),
        ffn2_w_in=G["ffn2_in"], ffn2_w_out=G["ffn2_out"].reshape(NSH, DFF // NSH, D),
        w_ple_gate=G["w_pg"].reshape(NSH, D // NSH, D), w_ple=G["w_ple"],
    )
    sums = [_sibling_sum(n, big[n]) for n in BIG]
    landed = _reduce_chips([narrow for _, narrow in sums])
    grad = {n: _chip_sum_share(n, b, own).reshape(w[n].shape) for n, b, (own, _) in zip(BIG, landed, sums)}

    pieces = [G["ln1_g"], G["ln1_b"], G["b_gate"], G["conv_w"].reshape(1, 4 * CW), G["alog"], G["dtb"], G["normw"],
              G["ln2_g"], G["ln2_b"], G["ln3_g"], G["ln3_b"], G["b_pg"], G["ln4_g"], G["ln4_b"]]
    flat = jnp.concatenate(pieces, axis=1)
    flat = jnp.pad(flat, ((0, 0), (0, NDEV * PACK_W - flat.shape[1])))
    total = _allreduce_small(flat.reshape(NDEV, PACK_W)).reshape(1, NDEV * PACK_W)
    off = 0
    for n, piece in zip(SMALL, pieces):
        grad[n] = total[0, off:off + piece.shape[1]]
        off += piece.shape[1]
    chip = 2 * lax.axis_index("x") + lax.axis_index("y")
    grad["conv_w"] = lax.dynamic_slice_in_dim(grad["conv_w"].reshape(4, CW), chip * (CW // NSH), CW // NSH, axis=1)
    grad["a_log"] = grad["a_log"][HEADS:2 * HEADS]
    grad["dt_bias"] = grad["dt_bias"][HEADS:2 * HEADS]

    delta, new_m, new_v = {}, {}, {}
    for n in ORDER:
        shape2 = w[n].shape if w[n].ndim == 2 else (1, w[n].shape[0])
        d_, m_, v_ = _adamw(n, *[a.reshape(shape2) for a in (w[n], grad[n], mom[n], var[n])])
        delta[n], new_m[n], new_v[n] = (a.reshape(args[n].shape) for a in (d_, m_, v_))
    outs = [loss, grad_x[None]]
    outs += [grad[n].reshape(args[n].shape) for n in ORDER]
    for group in (delta, new_m, new_v):
        outs += [group[n] for n in ORDER]
    return tuple(outs)
```

```python
import functools

import jax
import jax.numpy as jnp
from jax import lax
from jax.experimental import pallas as pl
from jax.experimental.pallas import tpu as pltpu

F32 = jnp.float32
BF = jnp.bfloat16
I32 = jnp.int32
HI = lax.Precision.HIGHEST
MESH = pl.DeviceIdType.MESH

D = 1024
DFF = 2816
NSH = 4
FSH = 2 * DFF // NSH
NIN = 5648
MSH = NIN // NSH
NP = 6144
CUT = 3600
GATE0 = 4096
HEADS = 8
HD = 64
CH = 64
KB = 128
BIG_ROWS = 1024
ALPHA = 2.0 ** 0.25
LN_EPS = 1e-5
RMS_EPS = 1e-6
B1, B2, LR, EPS, WD, STEP = 0.9, 0.999, 0.001, 1e-08, 0.01, 10


def _sigmoid(x):
    return 0.5 * jnp.tanh(0.5 * x) + 0.5


def _softplus(x):
    return jnp.maximum(x, 0.0) + jnp.log1p(jnp.exp(-jnp.abs(x)))


def _layer_norm(r, g, b):
    mu = jnp.mean(r, axis=-1, keepdims=True)
    xc = r - mu
    var = jnp.mean(xc * xc, axis=-1, keepdims=True)
    return xc * lax.rsqrt(var + LN_EPS) * g + b


def _layer_norm_bwd(r, g, dh):
    mu = jnp.mean(r, axis=-1, keepdims=True)
    xc = r - mu
    var = jnp.mean(xc * xc, axis=-1, keepdims=True)
    xhat = xc * lax.rsqrt(var + LN_EPS)
    dxh = dh * g
    dr = lax.rsqrt(var + LN_EPS) * (dxh - jnp.mean(dxh, axis=-1, keepdims=True) - xhat * jnp.mean(dxh * xhat, axis=-1, keepdims=True))
    return dr, jnp.sum(dh * xhat, axis=0, keepdims=True), jnp.sum(dh, axis=0, keepdims=True)


def _pick(n, cap):
    if n <= cap:
        return n
    for t in range(cap - cap % 8, 7, -8):
        if n % t == 0:
            return t
    raise ValueError((n, cap))


def _mm(name, a, b, M, N, K, *, tm, tn, tk, ta=False, tb=False, a_spec=None, b_spec=None, order="ij",
        extras=(), epilogue=None, outs, n_acc=0):
    ni, nj, nk = M // tm, N // tn, K // tk
    assert M % tm == 0 and N % tn == 0 and K % tk == 0, (name, M, N, K, tm, tn, tk)
    assert n_acc == 0 or nj == 1

    def wrap(fn):
        if order == "ij":
            return lambda g0, g1, g2: fn(g0, g1, g2)
        return lambda g0, g1, g2: fn(g1, g0, g2)

    if a_spec is None:
        a_spec = ((tk, tm), lambda i, j, k: (k, i)) if ta else ((tm, tk), lambda i, j, k: (i, k))
    if b_spec is None:
        b_spec = ((tn, tk), lambda i, j, k: (j, k)) if tb else ((tk, tn), lambda i, j, k: (k, j))
    dims = (((0 if ta else 1,), (1 if tb else 0,)), ((), ()))
    ne, no = len(extras), len(outs)
    grid = (ni, nj, nk) if order == "ij" else (nj, ni, nk)

    def body(*refs):
        a_ref, b_ref = refs[0], refs[1]
        ex = refs[2:2 + ne]
        o = refs[2 + ne:2 + ne + no]
        g0, g1, k = pl.program_id(0), pl.program_id(1), pl.program_id(2)
        first = jnp.logical_and(g0 == 0, g1 == 0)
        p = lax.dot_general(a_ref[...].astype(BF), b_ref[...].astype(BF), dims, preferred_element_type=F32)

        def finish(acc):
            vals = (acc,) if epilogue is None else epilogue(acc, *[e[...] for e in ex])
            for idx, (ref, val) in enumerate(zip(o, vals)):
                if idx < no - n_acc:
                    ref[...] = val.astype(ref.dtype)
                else:
                    @pl.when(first)
                    def _(ref=ref, val=val):
                        ref[...] = val

                    @pl.when(jnp.logical_not(first))
                    def _(ref=ref, val=val):
                        ref[...] += val

        if nk == 1:
            finish(p)
        else:
            acc_ref = refs[-1]

            @pl.when(k == 0)
            def _():
                acc_ref[...] = p

            @pl.when(k > 0)
            def _():
                acc_ref[...] += p

            @pl.when(k == nk - 1)
            def _():
                finish(acc_ref[...])

    in_specs = [pl.BlockSpec(a_spec[0], wrap(a_spec[1])), pl.BlockSpec(b_spec[0], wrap(b_spec[1]))]
    in_specs += [pl.BlockSpec(blk, wrap(fn)) for _, blk, fn in extras]
    res = pl.pallas_call(
        body, name=name, grid=grid, in_specs=in_specs,
        out_specs=[pl.BlockSpec(blk, wrap(fn)) for _, _, blk, fn in outs],
        out_shape=[jax.ShapeDtypeStruct(shape, dt) for shape, dt, _, _ in outs],
        scratch_shapes=[pltpu.VMEM((tm, tn), F32)] if nk > 1 else [],
    )(a, b, *[e[0] for e in extras])
    return res


def _row(i, j, k):
    return (i, 0)


def _tile(i, j, k):
    return (i, j)


def _const(i, j, k):
    return (0, 0)


def _rows(name, fn, n_steps, ins, outs, n_acc=0):
    ni, no = len(ins), len(outs)

    def body(*refs):
        i = pl.program_id(0)
        vals = fn(*[r[...] for r in refs[:ni]])
        for idx, (ref, val) in enumerate(zip(refs[ni:ni + no], vals)):
            if idx < no - n_acc:
                ref[...] = val.astype(ref.dtype)
            else:
                @pl.when(i == 0)
                def _(ref=ref, val=val):
                    ref[...] = val

                @pl.when(i > 0)
                def _(ref=ref, val=val):
                    ref[...] += val

    return pl.pallas_call(
        body, name=name, grid=(n_steps,),
        in_specs=[pl.BlockSpec(blk, fn_) for _, blk, fn_ in ins],
        out_specs=[pl.BlockSpec(blk, fn_) for _, _, blk, fn_ in outs],
        out_shape=[jax.ShapeDtypeStruct(shape, dt) for shape, dt, _, _ in outs],
    )(*[a for a, _, _ in ins])


def _ffn_fwd(tag, x, w_in, w_out, g, b):
    S = x.shape[0]
    tm = min(BIG_ROWS, S)
    gate, = _mm(f"{tag}_gate", x, w_in, S, DFF, D, tm=tm, tn=FSH, tk=D, order="ji",
                b_spec=((None, D, FSH), lambda i, j, k: (j, 0, 0)),
                outs=[((S, DFF), F32, (tm, FSH), _tile)])

    def up_epi(acc, gt):
        return acc, gt * _sigmoid(gt) * acc

    up, s = _mm(f"{tag}_up", x, w_in, S, DFF, D, tm=tm, tn=FSH, tk=D, order="ji",
                b_spec=((None, D, FSH), lambda i, j, k: (j + 2, 0, 0)),
                extras=[(gate, (tm, FSH), _tile)], epilogue=up_epi,
                outs=[((S, DFF), F32, (tm, FSH), _tile), ((S, DFF), BF, (tm, FSH), _tile)])

    def out_epi(acc, xin, gg, bb):
        r = ALPHA * xin + 0.5 * acc
        return _layer_norm(r, gg, bb), r

    h, r = _mm(f"{tag}_out", s, w_out, S, D, DFF, tm=tm, tn=D, tk=DFF,
               extras=[(x, (tm, D), _row), (g, (1, D), _const), (b, (1, D), _const)], epilogue=out_epi,
               outs=[((S, D), F32, (tm, D), _row), ((S, D), F32, (tm, D), _row)])
    return h, (x, gate, up, s, r)


def _ln_bwd(tag, r, g, dh):
    S = r.shape[0]
    tm = min(512, S)
    return _rows(f"{tag}_lnbwd", _layer_norm_bwd, S // tm,
                 [(r, (tm, D), lambda i: (i, 0)), (g, (1, D), lambda i: (0, 0)), (dh, (tm, D), lambda i: (i, 0))],
                 [((S, D), F32, (tm, D), lambda i: (i, 0)), ((1, D), F32, (1, D), lambda i: (0, 0)),
                  ((1, D), F32, (1, D), lambda i: (0, 0))], n_acc=2)


def _ffn_bwd(tag, saved, w_in, w_out, g, dh):
    x, gate, up, s, r = saved
    S = x.shape[0]
    tm = min(BIG_ROWS, S)
    dr, dg, db = _ln_bwd(tag, r, g, dh)

    def act_epi(acc, gt, u):
        ds = 0.5 * acc
        sg = _sigmoid(gt)
        return (jnp.stack([ds * u * (sg * (1.0 + gt * (1.0 - sg))), ds * (gt * sg)]),)

    da, = _mm(f"{tag}_dact", dr, w_out, S, DFF, D, tm=tm, tn=FSH, tk=D, tb=True, order="ji",
              extras=[(gate, (tm, FSH), _tile), (up, (tm, FSH), _tile)], epilogue=act_epi,
              outs=[((2, S, DFF), BF, (2, tm, FSH), lambda i, j, k: (0, i, j))])
    d_w_out, = _mm(f"{tag}_dwout", s, dr, DFF, D, S, tm=FSH, tn=D, tk=tm, ta=True,
                   epilogue=lambda acc: (0.5 * acc,), outs=[((DFF, D), F32, (FSH, D), _tile)])
    tb_ = min(BIG_ROWS, S)
    d_in, = _mm(f"{tag}_dx", da, w_in, S, D, 2 * DFF, tm=tb_, tn=D, tk=FSH, tb=True,
                a_spec=((None, tb_, FSH), lambda i, j, k: (k // 2, i, k % 2)),
                b_spec=((None, D, FSH), lambda i, j, k: (k, 0, 0)),
                extras=[(dr, (tb_, D), _row)], epilogue=lambda acc, d: (acc + ALPHA * d,),
                outs=[((S, D), F32, (tb_, D), _row)])
    d_w_in, = _mm(f"{tag}_dwin", x, da, D, 2 * DFF, S, tm=D, tn=FSH, tk=tb_, ta=True, order="ji",
                  b_spec=((None, tb_, FSH), lambda i, j, k: (j // 2, k, j % 2)),
                  outs=[((NSH, D, FSH), F32, (None, D, FSH), lambda i, j, k: (j, 0, 0))])
    return d_in, d_w_in, d_w_out, dg, db


def _cum(vals, tri):
    hi = vals.astype(BF)
    lo = (vals - hi.astype(F32)).astype(BF)
    return jnp.dot(hi, tri, preferred_element_type=F32) + jnp.dot(lo, tri, preferred_element_type=F32)


def _nt(a, b):
    return lax.dot_general(a, b, (((1,), (1,)), ((), ())), preferred_element_type=F32)


def _tn(a, b):
    return lax.dot_general(a, b, (((0,), (0,)), ((), ())), preferred_element_type=F32)


def _attn_fwd(m):
    S = m.shape[0]
    tq = min(256, S)
    assert S // KB <= 128

    def body(q_ref, k_ref, v_ref, o_ref, r_ref, kb, vb):
        i = pl.program_id(1)

        @pl.when(i == 0)
        def _():
            kb[...] = k_ref[...].astype(BF)
            vb[...] = v_ref[...].astype(BF)

        qs = (q_ref[...] * 0.125).astype(BF)
        row = lax.broadcasted_iota(I32, (tq, KB), 0) + i * tq
        col = lax.broadcasted_iota(I32, (tq, KB), 1)
        tri = (lax.broadcasted_iota(I32, (KB, KB), 0) >= lax.broadcasted_iota(I32, (KB, KB), 1)).astype(BF)
        nb = (i + 1) * (tq // KB)
        r_ref[...] = jnp.zeros_like(r_ref)

        def step(t, carry):
            j = nb - 1 - t
            off = pl.multiple_of(j * KB, KB)
            mask = col + j * KB < row
            kblk = kb[pl.ds(off, KB), :]
            vblk = vb[pl.ds(off, KB), :]
            new = []
            for hh in range(2):
                run, acc = carry[2 * hh], carry[2 * hh + 1]
                sl = slice(hh * HD, (hh + 1) * HD)
                z = _nt(qs[:, sl], kblk[:, sl])
                cs = _cum(jnp.where(mask, _softplus(z), 0.0), tri)
                a = jnp.where(mask, jnp.exp(z - (run + cs)), 0.0)
                r_ref[hh] = jnp.where(col == j, run, r_ref[hh])
                new += [run + cs[:, :1], acc + jnp.dot(a.astype(BF), vblk[:, sl], preferred_element_type=F32)]
            return tuple(new)

        init = (jnp.zeros((tq, 1), F32), jnp.zeros((tq, HD), F32)) * 2
        res = lax.fori_loop(0, nb, step, init)
        o_ref[...] = jnp.concatenate([res[1], res[3]], axis=1)

    return pl.pallas_call(
        body, name="attn_fwd", grid=(HEADS // 2, S // tq),
        in_specs=[pl.BlockSpec((tq, KB), lambda h, i: (i, h)), pl.BlockSpec((S, KB), lambda h, i: (0, 4 + h)),
                  pl.BlockSpec((S, KB), lambda h, i: (0, 8 + h))],
        out_specs=[pl.BlockSpec((tq, KB), lambda h, i: (i, h)), pl.BlockSpec((2, tq, KB), lambda h, i: (h, i, 0))],
        out_shape=[jax.ShapeDtypeStruct((S, HEADS * HD), F32), jax.ShapeDtypeStruct((HEADS, S, KB), F32)],
        scratch_shapes=[pltpu.VMEM((S, KB), BF), pltpu.VMEM((S, KB), BF)],
    )(m, m, m)


def _attn_bwd(m, runs, dy):
    S = m.shape[0]
    tq = min(256, S)

    def body(q_ref, k_ref, v_ref, r_ref, dy_ref, dq_ref, dk_ref, dv_ref, kb, vb):
        i = pl.program_id(1)

        @pl.when(i == 0)
        def _():
            kb[...] = k_ref[...].astype(BF)
            vb[...] = v_ref[...].astype(BF)
            dk_ref[...] = jnp.zeros_like(dk_ref)
            dv_ref[...] = jnp.zeros_like(dv_ref)

        qs = (q_ref[...] * 0.125).astype(BF)
        dyb = dy_ref[...].astype(BF)
        rs = [r_ref[0], r_ref[1]]
        row = lax.broadcasted_iota(I32, (tq, KB), 0) + i * tq
        col = lax.broadcasted_iota(I32, (tq, KB), 1)
        jj = lax.broadcasted_iota(I32, (KB, KB), 0)
        ss = lax.broadcasted_iota(I32, (KB, KB), 1)
        tri_rev = (jj >= ss).astype(BF)
        tri_fwd = (jj <= ss).astype(BF)
        nb = (i + 1) * (tq // KB)

        def step(j, carry):
            off = pl.multiple_of(j * KB, KB)
            mask = col + j * KB < row
            kblk = kb[pl.ds(off, KB), :]
            vblk = vb[pl.ds(off, KB), :]
            new, dks, dvs = [], [], []
            for hh in range(2):
                pre, dq = carry[2 * hh], carry[2 * hh + 1]
                sl = slice(hh * HD, (hh + 1) * HD)
                run = jnp.sum(jnp.where(col == j, rs[hh], 0.0), axis=1, keepdims=True)
                z = _nt(qs[:, sl], kblk[:, sl])
                e = jnp.exp(-jnp.abs(z))
                sp = jnp.maximum(z, 0.0) + jnp.log1p(e)
                sig = jnp.where(z >= 0.0, 1.0, e) / (1.0 + e)
                cs = _cum(jnp.where(mask, sp, 0.0), tri_rev)
                a = jnp.where(mask, jnp.exp(z - (run + cs)), 0.0)
                gmat = a * _nt(dyb[:, sl], vblk[:, sl])
                pg = _cum(gmat, tri_fwd)
                dz = jnp.where(mask, gmat - sig * (pre + pg), 0.0).astype(BF)
                dks.append(_tn(dz, qs[:, sl]))
                dvs.append(_tn(a.astype(BF), dyb[:, sl]))
                new += [pre + pg[:, KB - 1:], dq + jnp.dot(dz, kblk[:, sl], preferred_element_type=F32)]
            dk_ref[pl.ds(off, KB), :] += jnp.concatenate(dks, axis=1)
            dv_ref[pl.ds(off, KB), :] += jnp.concatenate(dvs, axis=1)
            return tuple(new)

        init = (jnp.zeros((tq, 1), F32), jnp.zeros((tq, HD), F32)) * 2
        res = lax.fori_loop(0, nb, step, init)
        dq_ref[...] = (jnp.concatenate([res[1], res[3]], axis=1) * 0.125).astype(dq_ref.dtype)

    n = HEADS * HD
    return pl.pallas_call(
        body, name="attn_bwd", grid=(HEADS // 2, S // tq),
        in_specs=[pl.BlockSpec((tq, KB), lambda h, i: (i, h)), pl.BlockSpec((S, KB), lambda h, i: (0, 4 + h)),
                  pl.BlockSpec((S, KB), lambda h, i: (0, 8 + h)), pl.BlockSpec((2, tq, KB), lambda h, i: (h, i, 0)),
                  pl.BlockSpec((tq, KB), lambda h, i: (i, h))],
        out_specs=[pl.BlockSpec((tq, KB), lambda h, i: (i, h)), pl.BlockSpec((S, KB), lambda h, i: (0, h)),
                   pl.BlockSpec((S, KB), lambda h, i: (0, h))],
        out_shape=[jax.ShapeDtypeStruct((S, n), BF), jax.ShapeDtypeStruct((S, n), F32), jax.ShapeDtypeStruct((S, n), F32)],
        scratch_shapes=[pltpu.VMEM((S, KB), BF), pltpu.VMEM((S, KB), BF)],
    )(m, m, m, runs, dy)


SB = 256
NKC = SB // KB
HPS = 4
GW = HPS * HD
LOG2E = 1.4426950408889634
ANY = pl.BlockSpec(memory_space=pl.ANY)


def _split(vals):
    hi = vals.astype(BF)
    return hi, (vals - hi.astype(F32)).astype(BF)


def _chunk_sums(tri2, vals):
    hi, lo = _split(vals)
    return [jnp.dot(tri2, jnp.concatenate([hi[c * KB:(c + 1) * KB], lo[c * KB:(c + 1) * KB]], axis=0), preferred_element_type=F32)
            for c in range(NKC)]


def _head_halves(t, axis):
    idx = lax.broadcasted_iota(I32, t.shape, axis)
    return [jnp.where(idx < HD, t, 0.0).astype(BF), jnp.where(idx >= HD, t, 0.0).astype(BF)]


QT = 512


def _diag_masks(sq):
    krow, qcol = lax.broadcasted_iota(I32, (SB, sq), 0), lax.broadcasted_iota(I32, (SB, sq), 1)
    return [krow + d * SB < qcol for d in range(sq // SB)]


def _softplus2(z):
    return jnp.maximum(z, 0.0) + jnp.log2(1.0 + jnp.exp2(jnp.minimum(z, -z)))


def _attn_prep(m):
    S = m.shape[0]
    tm = min(512, S)
    n = HEADS * HD
    return _rows("attn_prep", lambda k, v: (k, v, k.T, v.T), S // tm,
                 [(m, (tm, n), lambda i: (i, 1)), (m, (tm, n), lambda i: (i, 2))],
                 [((S, n), BF, (tm, n), lambda i: (i, 0)), ((S, n), BF, (tm, n), lambda i: (i, 0)),
                  ((n, S), BF, (n, tm), lambda i: (0, i)), ((n, S), BF, (n, tm), lambda i: (0, i))])


def _attn_fwd_t(m, kb_all, vt_all, ride=None):
    S = m.shape[0]
    SQ = min(QT, S)
    assert S % SQ == 0 and SQ % SB == 0
    nkc, nqb, nh, nq = S // KB, SQ // SB, HEADS // HPS, S // SQ
    nr = ride.n if ride else 0

    def body(*refs):
        q_ref, kb_hbm, vt_hbm = refs[:3]
        ride_in, (o_ref, r_ref), ride_out = refs[3:3 + nr], refs[3 + nr:5 + nr], refs[5 + nr:5 + 2 * nr]
        kb, vt, acc = refs[5 + 2 * nr:8 + 2 * nr]
        sems = refs[8 + 2 * nr:]
        h, i = pl.program_id(0), pl.program_id(1)
        if ride:
            @pl.when(jnp.logical_and(h == 0, i == 0))
            def _():
                ride.run("start", ride_in, ride_out, sems)

        @pl.when(i == 0)
        def _():
            cols = pl.ds(pl.multiple_of(h * GW, GW), GW)
            pltpu.sync_copy(kb_hbm.at[:, cols], kb)
            pltpu.sync_copy(vt_hbm.at[cols, :], vt)

        qt = (q_ref[...] * (0.125 * LOG2E)).T
        qtm = [t for g in range(HPS // 2) for t in _head_halves(qt[g * KB:(g + 1) * KB], 0)]
        dmasks = _diag_masks(SQ)
        upper = (lax.broadcasted_iota(I32, (KB, KB), 1) >= lax.broadcasted_iota(I32, (KB, KB), 0)).astype(BF)
        tri2 = jnp.concatenate([upper, upper], axis=1)
        acc[...] = jnp.zeros_like(acc)
        r_ref[...] = jnp.zeros_like(r_ref)

        def block(jb, runs, dmask):
            masked = dmask is not None
            off = pl.multiple_of(jb * SB, SB)
            groups = [slice(g * KB, (g + 1) * KB) for g in range(HPS // 2)]
            kblk = [kb[pl.ds(off, SB), s] for s in groups]
            vtb = [vt[s, pl.ds(off, SB)] for s in groups]
            old = acc[...]
            zs = [jnp.dot(kblk[hh // 2], qtm[hh], preferred_element_type=F32) for hh in range(HPS)]
            sps = [_softplus2(z) for z in zs]
            if masked:
                sps = [jnp.where(dmask, sp, 0.0) for sp in sps]
            css = [_chunk_sums(tri2, sp) for sp in sps]
            run0s = [run + cs[1][0:1, :] for run, cs in zip(runs, css)]
            aa = [jnp.exp2(z - jnp.concatenate([run0 + cs[0], run + cs[1]], axis=0)) for z, run, run0, cs in zip(zs, runs, run0s, css)]
            if masked:
                aa = [jnp.where(dmask, a, 0.0) for a in aa]
            parts = [jnp.dot(vtb[hh // 2], aa[hh].astype(BF), preferred_element_type=F32) for hh in range(HPS)]
            upd = jnp.concatenate([parts[hh][(hh % 2) * HD:(hh % 2 + 1) * HD, :] for hh in range(HPS)], axis=0)
            for hh in range(HPS):
                r_ref[hh, pl.ds(NKC * jb, 1), :] = run0s[hh]
                r_ref[hh, pl.ds(NKC * jb + 1, 1), :] = runs[hh]
            acc[...] = old + upd
            return tuple(run0 + cs[0][0:1, :] for run0, cs in zip(run0s, css))

        runs = (jnp.zeros((1, SQ), F32),) * HPS
        for d in reversed(range(nqb)):
            runs = block(i * nqb + d, runs, dmasks[d])
        lax.fori_loop(0, i * nqb, lambda t, c: block(i * nqb - 1 - t, c, None), runs)
        o_ref[...] = acc[...].T
        if ride:
            @pl.when(jnp.logical_and(h == nh - 1, i == nq - 1))
            def _():
                ride.run("wait", ride_in, ride_out, sems)

    return pl.pallas_call(
        body, name="attn_fwd", grid=(nh, nq),
        in_specs=[pl.BlockSpec((SQ, GW), lambda h, i: (i, h)), ANY, ANY] + [ANY] * nr,
        out_specs=[pl.BlockSpec((SQ, GW), lambda h, i: (i, h)), pl.BlockSpec((HPS, nkc, SQ), lambda h, i: (h, 0, i))] + [ANY] * nr,
        out_shape=[jax.ShapeDtypeStruct((S, HEADS * HD), F32), jax.ShapeDtypeStruct((HEADS, nkc, S), F32)]
        + (ride.out_shape if ride else []),
        scratch_shapes=[pltpu.VMEM((S, GW), BF), pltpu.VMEM((GW, S), BF), pltpu.VMEM((GW, SQ), F32)]
        + (_exchange_sems(nr) if ride else []),
    )(m, kb_all, vt_all, *(ride.arrays if ride else []))


def _attn_bwd_t(m, kb_all, vb_all, kt_all, runs, dy, ride=None):
    S = m.shape[0]
    SQ = min(QT, S)
    nkc, nqb, nh, nq = S // KB, SQ // SB, HEADS // HPS, S // SQ
    nr = ride.n if ride else 0

    def body(*refs):
        q_ref, kb_hbm, vb_hbm, kt_hbm, r_ref, dy_ref = refs[:6]
        ride_in, (dq_ref, dk_hbm, dv_hbm), ride_out = refs[6:6 + nr], refs[6 + nr:9 + nr], refs[9 + nr:9 + 2 * nr]
        kb, vb, kt, dqt, dka, dva = refs[9 + 2 * nr:15 + 2 * nr]
        sems = refs[15 + 2 * nr:]
        h, i = pl.program_id(0), pl.program_id(1)
        cols = pl.ds(pl.multiple_of(h * GW, GW), GW)
        if ride:
            @pl.when(jnp.logical_and(h == 0, i == 0))
            def _():
                ride.run("start", ride_in, ride_out, sems)

        @pl.when(i == 0)
        def _():
            pltpu.sync_copy(kb_hbm.at[:, cols], kb)
            pltpu.sync_copy(vb_hbm.at[:, cols], vb)
            pltpu.sync_copy(kt_hbm.at[cols, :], kt)
            dka[...] = jnp.zeros_like(dka)
            dva[...] = jnp.zeros_like(dva)

        q8 = q_ref[...] * 0.125
        dyf = dy_ref[...]
        q8t, dyt = (q8 * LOG2E).T, dyf.T
        groups = [slice(g * KB, (g + 1) * KB) for g in range(HPS // 2)]
        qtm = [t for s in groups for t in _head_halves(q8t[s], 0)]
        dytm = [t for s in groups for t in _head_halves(dyt[s], 0)]
        qlm = [t for s in groups for t in _head_halves(q8[:, s], 1)]
        dylm = [t for s in groups for t in _head_halves(dyf[:, s], 1)]
        dmasks = _diag_masks(SQ)
        ri, ci = lax.broadcasted_iota(I32, (KB, KB), 0), lax.broadcasted_iota(I32, (KB, KB), 1)
        upper, lower = (ci >= ri).astype(BF), (ci <= ri).astype(BF)
        rev2 = jnp.concatenate([upper, upper], axis=1)
        dqt[...] = jnp.zeros_like(dqt)

        def block(jb, pres, dmask):
            masked = dmask is not None
            off = pl.multiple_of(jb * SB, SB)
            heads = range(HPS)
            kblk = [kb[pl.ds(off, SB), s] for s in groups]
            vblk = [vb[pl.ds(off, SB), s] for s in groups]
            ktb = [kt[s, pl.ds(off, SB)] for s in groups]
            run0s = [r_ref[hh, pl.ds(NKC * jb, 1), :] for hh in heads]
            run1s = [r_ref[hh, pl.ds(NKC * jb + 1, 1), :] for hh in heads]
            old_dq, old_dk, old_dv = dqt[...], dka[pl.ds(off, SB), :], dva[pl.ds(off, SB), :]
            dks, dvs, parts, new = [], [], [], []
            for g in range(HPS // 2):
                hs = (2 * g, 2 * g + 1)
                zs = [jnp.dot(kblk[g], qtm[hh], preferred_element_type=F32) for hh in hs]
                das = [jnp.dot(vblk[g], dytm[hh], preferred_element_type=F32) for hh in hs]
                sps = [_softplus2(z) for z in zs]
                sigs = [jnp.exp2(z - sp) for z, sp in zip(zs, sps)]
                if masked:
                    sps = [jnp.where(dmask, sp, 0.0) for sp in sps]
                css = [_chunk_sums(rev2, sp) for sp in sps]
                aa = [jnp.exp2(z - jnp.concatenate([run0s[hh] + cs[0], run1s[hh] + cs[1]], axis=0)) for z, hh, cs in zip(zs, hs, css)]
                if masked:
                    aa = [jnp.where(dmask, a, 0.0) for a in aa]
                gs = [a * da for a, da in zip(aa, das)]
                pgs = [[jnp.dot(lower, gg[c * KB:(c + 1) * KB].astype(BF), preferred_element_type=F32) for c in range(NKC)] for gg in gs]
                pre1s = [pres[hh] + pg[0][KB - 1:KB, :] for hh, pg in zip(hs, pgs)]
                dzs = [gg - sig * jnp.concatenate([pres[hh] + pg[0], pre1 + pg[1]], axis=0)
                       for gg, sig, hh, pre1, pg in zip(gs, sigs, hs, pre1s, pgs)]
                if masked:
                    dzs = [jnp.where(dmask, dz, 0.0) for dz in dzs]
                dzb, ab = [dz.astype(BF) for dz in dzs], [a.astype(BF) for a in aa]
                dks.append(sum(jnp.dot(dzb[t], qlm[hh], preferred_element_type=F32) for t, hh in enumerate(hs)))
                dvs.append(sum(jnp.dot(ab[t], dylm[hh], preferred_element_type=F32) for t, hh in enumerate(hs)))
                parts += [jnp.dot(ktb[g], dzb[t], preferred_element_type=F32)[t * HD:(t + 1) * HD, :] for t in range(2)]
                new += [pre1 + pg[1][KB - 1:KB, :] for pre1, pg in zip(pre1s, pgs)]
            dqt[...] = old_dq + jnp.concatenate(parts, axis=0)
            dka[pl.ds(off, SB), :] = old_dk + jnp.concatenate(dks, axis=1)
            dva[pl.ds(off, SB), :] = old_dv + jnp.concatenate(dvs, axis=1)
            return tuple(new)

        pres = lax.fori_loop(0, i * nqb, lambda jb, c: block(jb, c, None), (jnp.zeros((1, SQ), F32),) * HPS)
        for d in range(nqb):
            pres = block(i * nqb + d, pres, dmasks[d])
        dq_ref[...] = (dqt[...].T * 0.125).astype(dq_ref.dtype)

        @pl.when(i == nq - 1)
        def _():
            pltpu.sync_copy(dka, dk_hbm.at[:, cols])
            pltpu.sync_copy(dva, dv_hbm.at[:, cols])

        if ride:
            @pl.when(jnp.logical_and(h == nh - 1, i == nq - 1))
            def _():
                ride.run("wait", ride_in, ride_out, sems)

    n = HEADS * HD
    return pl.pallas_call(
        body, name="attn_bwd", grid=(nh, nq),
        in_specs=[pl.BlockSpec((SQ, GW), lambda h, i: (i, h)), ANY, ANY, ANY,
                  pl.BlockSpec((HPS, nkc, SQ), lambda h, i: (h, 0, i)), pl.BlockSpec((SQ, GW), lambda h, i: (i, h))] + [ANY] * nr,
        out_specs=[pl.BlockSpec((SQ, GW), lambda h, i: (i, h)), ANY, ANY] + [ANY] * nr,
        out_shape=[jax.ShapeDtypeStruct((S, n), BF), jax.ShapeDtypeStruct((S, n), F32), jax.ShapeDtypeStruct((S, n), F32)]
        + (ride.out_shape if ride else []),
        scratch_shapes=[pltpu.VMEM((S, GW), BF), pltpu.VMEM((S, GW), BF), pltpu.VMEM((GW, S), BF), pltpu.VMEM((GW, SQ), F32),
                        pltpu.VMEM((S, GW), F32), pltpu.VMEM((S, GW), F32)] + (_exchange_sems(nr) if ride else []),
    )(m, kb_all, vb_all, kt_all, runs, dy, *(ride.arrays if ride else []))


CW = 3 * HEADS * HD


def _shift_down(cur, prev8, s):
    if s == 0:
        return cur
    r = pltpu.roll(cur, s, 0)
    first = jnp.where(lax.broadcasted_iota(I32, (8, cur.shape[1]), 0) < s, pltpu.roll(prev8, s, 0), r[:8])
    return jnp.concatenate([first, r[8:]], axis=0)


def _shift_up(cur, next8, s):
    if s == 0:
        return cur
    n = cur.shape[0]
    r = pltpu.roll(cur, n - s, 0)
    last = jnp.where(lax.broadcasted_iota(I32, (8, cur.shape[1]), 0) >= 8 - s, pltpu.roll(next8, 8 - s, 0), r[n - 8:])
    return jnp.concatenate([r[:n - 8], last], axis=0)


def _conv_fwd(m, conv_w):
    S = m.shape[0]
    tm = min(512, S)
    hb = tm // 8

    def body(x_ref, p_ref, w_ref, o_ref):
        i = pl.program_id(0)
        cur = x_ref[...]
        prev = jnp.where(i > 0, p_ref[...], 0.0)
        w = w_ref[...]
        acc = cur * w[3:4]
        for jk in range(3):
            acc = acc + _shift_down(cur, prev, 3 - jk) * w[jk:jk + 1]
        o_ref[...] = acc

    return pl.pallas_call(
        body, name="conv_fwd", grid=(S // tm,),
        in_specs=[pl.BlockSpec((tm, CW), lambda i: (i, 1)), pl.BlockSpec((8, CW), lambda i: (jnp.maximum(i * hb - 1, 0), 1)),
                  pl.BlockSpec((4, CW), lambda i: (0, 0))],
        out_specs=pl.BlockSpec((tm, CW), lambda i: (i, 0)),
        out_shape=jax.ShapeDtypeStruct((S, CW), F32),
    )(m, m, conv_w)


def _conv_bwd(m, dyc, conv_w):
    S = m.shape[0]
    tm = min(512, S)
    hb = tm // 8
    nt = S // tm

    def body(x_ref, p_ref, d_ref, n_ref, w_ref, dx_ref, dw_ref):
        i = pl.program_id(0)
        cur = x_ref[...]
        prev = jnp.where(i > 0, p_ref[...], 0.0)
        d = d_ref[...]
        nxt = jnp.where(i < nt - 1, n_ref[...], 0.0)
        w = w_ref[...]
        acc = d * w[3:4]
        dws = []
        for jk in range(3):
            acc = acc + _shift_up(d, nxt, 3 - jk) * w[jk:jk + 1]
            dws.append(jnp.sum(d * _shift_down(cur, prev, 3 - jk), axis=0, keepdims=True))
        dws.append(jnp.sum(d * cur, axis=0, keepdims=True))
        dx_ref[...] = acc.astype(dx_ref.dtype)
        dw = jnp.concatenate(dws, axis=0)

        @pl.when(i == 0)
        def _():
            dw_ref[...] = dw

        @pl.when(i > 0)
        def _():
            dw_ref[...] += dw

    return pl.pallas_call(
        body, name="conv_bwd", grid=(nt,),
        in_specs=[pl.BlockSpec((tm, CW), lambda i: (i, 1)), pl.BlockSpec((8, CW), lambda i: (jnp.maximum(i * hb - 1, 0), 1)),
                  pl.BlockSpec((tm, CW), lambda i: (i, 0)),
                  pl.BlockSpec((8, CW), lambda i: (jnp.minimum((i + 1) * hb, S // 8 - 1), 0)),
                  pl.BlockSpec((4, CW), lambda i: (0, 0))],
        out_specs=[pl.BlockSpec((tm, CW), lambda i: (i, 0)), pl.BlockSpec((4, CW), lambda i: (0, 0))],
        out_shape=[jax.ShapeDtypeStruct((S, CW), BF), jax.ShapeDtypeStruct((4, CW), F32)],
    )(m, m, dyc, dyc, conv_w)


def _t(a):
    return jnp.swapaxes(a, 1, 2)


def _bdot(a, b):
    return jnp.einsum("hik,hkj->hij", a, b, preferred_element_type=F32)


@jax.custom_vjp
def _mm1(a, b):
    return _bdot(a.astype(BF), b.astype(BF))


def _bdot_nt(a, b):
    return jnp.einsum("hij,hkj->hik", a, b, preferred_element_type=F32)


def _bdot_tn(a, b):
    return jnp.einsum("hki,hkj->hij", a, b, preferred_element_type=F32)


_mm1.defvjp(lambda a, b: (_mm1(a, b), (a.astype(BF), b.astype(BF))),
            lambda res, dc: (_bdot_nt(dc.astype(BF), res[1]), _bdot_tn(res[0], dc.astype(BF))))


@jax.custom_vjp
def _mm1_nt(a, b):
    return _bdot_nt(a.astype(BF), b.astype(BF))


_mm1_nt.defvjp(lambda a, b: (_mm1_nt(a, b), (a.astype(BF), b.astype(BF))),
               lambda res, dc: (_bdot(dc.astype(BF), res[1]), _bdot_tn(dc.astype(BF), res[0])))


@jax.custom_vjp
def _mm1_tn(a, b):
    return _bdot_tn(a.astype(BF), b.astype(BF))


_mm1_tn.defvjp(lambda a, b: (_mm1_tn(a, b), (a.astype(BF), b.astype(BF))),
               lambda res, dc: (_bdot_nt(res[1], dc.astype(BF)), _bdot(res[0], dc.astype(BF))))


def _stack_rows(hi, lo):
    return jnp.concatenate([hi, lo], axis=1)


@jax.custom_vjp
def _mm3(a, b):
    (ah, al), (bh, bl) = _split(a), _split(b)
    n = a.shape[1]
    two = _bdot(_stack_rows(ah, al), bh)
    return two[:, :n] + two[:, n:] + _bdot(ah, bl)


def _mm3_fwd(a, b):
    return _mm3(a, b), (_split(a), _split(b))


def _mm3_bwd(res, dc):
    (ah, al), (bh, bl) = res
    dh, dl = _split(dc)
    n = dc.shape[1]
    two = _bdot_nt(_stack_rows(dh, dl), bh)
    da = two[:, :n] + two[:, n:] + _bdot_nt(dh, bl)
    db = _bdot_tn(jnp.concatenate([ah, ah, al], axis=1), jnp.concatenate([dh, dl, dh], axis=1))
    return da, db


_mm3.defvjp(_mm3_fwd, _mm3_bwd)


def _mm_exact(c3, b):
    hi, lo = _split(b)
    lo2 = (b - hi.astype(F32) - lo.astype(F32)).astype(BF)
    return _bdot(c3, jnp.concatenate([hi, lo, lo2], axis=-2))


@jax.custom_vjp
def _cumsum_rows(b):
    return _mm_exact(_tri3(True), b)


def _tri3(lower):
    ri = lax.broadcasted_iota(I32, (HEADS, CH, CH), 1)
    ci = lax.broadcasted_iota(I32, (HEADS, CH, CH), 2)
    tri = (ri >= ci if lower else ri <= ci).astype(BF)
    return jnp.concatenate([tri, tri, tri], axis=-1)


_cumsum_rows.defvjp(lambda b: (_cumsum_rows(b), None), lambda _, dc: (_mm_exact(_tri3(False), dc),))


CPS = 2


def _gdn_chunk(yc, gz, gba, alog, dtb):
    def heads(t, off):
        return jnp.stack([t[:, off + h * HD: off + (h + 1) * HD] for h in range(HEADS)])

    def cols(t, off):
        return jnp.stack([jnp.broadcast_to(t[:, off + h: off + h + 1], (CH, CH)) for h in range(HEADS)])

    c = yc * _sigmoid(yc)
    q, k, v, zg = heads(c, 0), heads(c, HEADS * HD), heads(c, 2 * HEADS * HD), heads(gz, 0)
    q = q * lax.rsqrt(jnp.sum(q * q, axis=-1, keepdims=True) + RMS_EPS) * (HD ** -0.5)
    k = k * lax.rsqrt(jnp.sum(k * k, axis=-1, keepdims=True) + RMS_EPS)
    beta = cols(_sigmoid(gba), 0)
    g = cols(-jnp.exp(alog) * _softplus(gba + dtb), HEADS)
    ri = lax.broadcasted_iota(I32, (HEADS, CH, CH), 1)
    ci = lax.broadcasted_iota(I32, (HEADS, CH, CH), 2)
    causal, strict = ri >= ci, ri > ci
    eye = (ri == ci).astype(F32)
    gc = _cumsum_rows(g)
    gr = _t(gc)
    decay = jnp.where(causal, jnp.exp(jnp.where(causal, gc - gr, 0.0)), 0.0)
    lower = jnp.where(strict, beta * _mm1_nt(k, k) * decay, 0.0)
    pw = -lower
    inv = eye + pw
    pw = _mm3(pw, pw)
    for _ in range(4):
        both = _mm3(jnp.concatenate([inv, pw], axis=1), pw)
        inv, pw = inv + both[:, :CH], both[:, CH:]
    inv = inv + _mm3(inv, pw)
    eg = jnp.exp(gc)
    uw = _mm3(inv, jnp.concatenate([v * beta, k * (beta * eg)], axis=2))
    u, w = uw[:, :, :HD], uw[:, :, HD:]
    qk = jnp.where(causal, _mm1_nt(q, k) * decay, 0.0)
    g_last = gc[:, CH - 1:CH, :]
    return u, jnp.concatenate([w, q * eg], axis=1), qk, k * jnp.exp(g_last - gc), jnp.exp(g_last), zg * _sigmoid(zg)


def _gdn_advance(state, pre, normw):
    u, wq, qk, kd, last, gate = pre
    ws = _mm1(wq, state)
    v_new = u - ws[:, :CH]
    o = ws[:, CH:] + _mm1(qk, v_new)
    new_state = state * last + _mm1_tn(kd, v_new)
    o = o * lax.rsqrt(jnp.mean(o * o, axis=-1, keepdims=True) + RMS_EPS) * normw * gate
    return jnp.concatenate([o[h] for h in range(HEADS)], axis=1), new_state


def _gdn_chunks(state, yc, gz, gba, alog, dtb, normw):
    rows = [slice(c * CH, (c + 1) * CH) for c in range(yc.shape[0] // CH)]
    pres = [_gdn_chunk(yc[r], gz[r], gba[r], alog, dtb) for r in rows]
    outs = []
    for pre in pres:
        o, state = _gdn_advance(state, pre, normw)
        outs.append(o)
    return jnp.concatenate(outs, axis=0), state


def _gdn_fwd(yc, m, alog, dtb, normw):
    S = yc.shape[0]
    RS = CPS * CH
    nch = S // RS

    def body(y_ref, gz_ref, gba_ref, al_ref, dt_ref, nw_ref, o_ref, st_ref, st):
        @pl.when(pl.program_id(0) == 0)
        def _():
            st[...] = jnp.zeros_like(st)

        cur = st[...]
        st_ref[0] = cur
        o, new = _gdn_chunks(cur, y_ref[...], gz_ref[...], gba_ref[...], al_ref[...], dt_ref[...], nw_ref[...])
        o_ref[...] = o
        st[...] = new

    return pl.pallas_call(
        body, name="gdn_fwd", grid=(nch,),
        in_specs=[pl.BlockSpec((RS, CW), lambda n: (n, 0)), pl.BlockSpec((RS, HEADS * HD), lambda n: (n, 6)),
                  pl.BlockSpec((RS, 128), lambda n: (n, 28)), pl.BlockSpec((1, 128), lambda n: (0, 0)),
                  pl.BlockSpec((1, 128), lambda n: (0, 0)), pl.BlockSpec((1, HD), lambda n: (0, 0))],
        out_specs=[pl.BlockSpec((RS, HEADS * HD), lambda n: (n, 0)), pl.BlockSpec((1, HEADS, HD, HD), lambda n: (n, 0, 0, 0))],
        out_shape=[jax.ShapeDtypeStruct((S, HEADS * HD), F32), jax.ShapeDtypeStruct((nch, HEADS, HD, HD), F32)],
        scratch_shapes=[pltpu.VMEM((HEADS, HD, HD), F32)],
    )(yc, m, m, alog, dtb, normw)


def _gdn_bwd(yc, m, alog, dtb, normw, states, dog):
    S = yc.shape[0]
    RS = CPS * CH
    nch = S // RS

    def body(y_ref, gz_ref, gba_ref, al_ref, dt_ref, nw_ref, st_ref, do_ref, dy_ref, dgz_ref, dgba_ref, dal_ref, ddt_ref, dnw_ref, dst):
        n = pl.program_id(0)

        @pl.when(n == 0)
        def _():
            dst[...] = jnp.zeros_like(dst)

        _, vjp = jax.vjp(_gdn_chunks, st_ref[0], y_ref[...], gz_ref[...], gba_ref[...], al_ref[...], dt_ref[...], nw_ref[...])
        d_state, d_y, d_gz, d_gba, d_al, d_dt, d_nw = vjp((do_ref[...], dst[...]))
        dst[...] = d_state
        dy_ref[...] = d_y
        dgz_ref[...] = d_gz.astype(dgz_ref.dtype)
        dgba_ref[...] = d_gba.astype(dgba_ref.dtype)
        for ref, val in ((dal_ref, d_al), (ddt_ref, d_dt), (dnw_ref, d_nw)):
            @pl.when(n == 0)
            def _(ref=ref, val=val):
                ref[...] = val

            @pl.when(n > 0)
            def _(ref=ref, val=val):
                ref[...] += val

    rev = lambda n: nch - 1 - n
    return pl.pallas_call(
        body, name="gdn_bwd", grid=(nch,),
        in_specs=[pl.BlockSpec((RS, CW), lambda n: (rev(n), 0)), pl.BlockSpec((RS, HEADS * HD), lambda n: (rev(n), 6)),
                  pl.BlockSpec((RS, 128), lambda n: (rev(n), 28)), pl.BlockSpec((1, 128), lambda n: (0, 0)),
                  pl.BlockSpec((1, 128), lambda n: (0, 0)), pl.BlockSpec((1, HD), lambda n: (0, 0)),
                  pl.BlockSpec((1, HEADS, HD, HD), lambda n: (rev(n), 0, 0, 0)),
                  pl.BlockSpec((RS, HEADS * HD), lambda n: (rev(n), 0))],
        out_specs=[pl.BlockSpec((RS, CW), lambda n: (rev(n), 0)), pl.BlockSpec((RS, HEADS * HD), lambda n: (rev(n), 0)),
                   pl.BlockSpec((RS, 128), lambda n: (rev(n), 0)), pl.BlockSpec((1, 128), lambda n: (0, 0)),
                   pl.BlockSpec((1, 128), lambda n: (0, 0)), pl.BlockSpec((1, HD), lambda n: (0, 0))],
        out_shape=[jax.ShapeDtypeStruct((S, CW), F32), jax.ShapeDtypeStruct((S, HEADS * HD), BF),
                   jax.ShapeDtypeStruct((S, 128), BF), jax.ShapeDtypeStruct((1, 128), F32),
                   jax.ShapeDtypeStruct((1, 128), F32), jax.ShapeDtypeStruct((1, HD), F32)],
        scratch_shapes=[pltpu.VMEM((HEADS, HD, HD), F32)],
    )(yc, m, m, alog, dtb, normw, states, dog)


def _mixer_fwd(h1, W, late=None):
    S = h1.shape[0]
    tm = min(512, S)
    n = HEADS * HD
    tb_ = min(BIG_ROWS, S)
    m, = _mm("mix_in", h1, W["wp"], S, NP, D, tm=tb_, tn=1536, tk=D, order="ji", outs=[((S, NP), F32, (tb_, 1536), _tile)])
    kb_all, vb_all, kt_all, vt_all = _attn_prep(m)
    ya, runs, *arrived = _attn_fwd_t(m, kb_all, vt_all, ride=late[0] if late else None)
    if late:
        late[1](W, arrived)
    runs = (kb_all, vb_all, kt_all, runs)
    b_gate, conv_w, alog, dtb, normw, w_sb, w_gdn, w_mo, g, b = (
        W[k] for k in ("b_gate", "conv_w", "alog", "dtb", "normw", "w_sb", "w_gdn", "w_mo", "ln2_g", "ln2_b"))
    yc = _conv_fwd(m, conv_w)
    og, states = _gdn_fwd(yc, m, alog, dtb, normw)
    ysb, = _mm("mix_sb", ya, w_sb, S, D, n, tm=tm, tn=D, tk=n, outs=[((S, D), F32, (tm, D), _row)])

    def merge_epi(acc, ys, gs, gg, bg):
        return _sigmoid(gs + bg[:, :D]) * ys + _sigmoid(gg + bg[:, D:]) * acc, acc

    u, ygdn = _mm("mix_gdn", og, w_gdn, S, D, n, tm=tm, tn=D, tk=n,
                  extras=[(ysb, (tm, D), _row), (m, (tm, D), lambda i, j, k: (i, GATE0 // D)),
                          (m, (tm, D), lambda i, j, k: (i, GATE0 // D + 1)), (b_gate, (1, 2 * D), _const)],
                  epilogue=merge_epi, outs=[((S, D), BF, (tm, D), _row), ((S, D), F32, (tm, D), _row)])

    def out_epi(acc, xin, gg, bb):
        r = ALPHA * xin + acc
        return _layer_norm(r, gg, bb), r

    h2, r2 = _mm("mix_out", u, w_mo, S, D, D, tm=tm, tn=D, tk=D,
                 extras=[(h1, (tm, D), _row), (g, (1, D), _const), (b, (1, D), _const)], epilogue=out_epi,
                 outs=[((S, D), F32, (tm, D), _row), ((S, D), F32, (tm, D), _row)])
    return h2, (h1, m, ya, runs, yc, og, states, ysb, ygdn, u, r2)


def _mixer_bwd(saved, W, dh, ride=None):
    h1, m, ya, runs, yc, og, states, ysb, ygdn, u, r2 = saved
    wp, b_gate, conv_w, alog, dtb, normw, w_sb, w_gdn, w_mo, g = (
        W[k] for k in ("wp", "b_gate", "conv_w", "alog", "dtb", "normw", "w_sb", "w_gdn", "w_mo", "ln2_g"))
    S = h1.shape[0]
    tm = min(512, S)
    n = HEADS * HD
    dr, dg, db = _ln_bwd("mix", r2, g, dh)

    def merge_epi(du, ys, yg, gs, gg, bg):
        s1, s2 = _sigmoid(gs + bg[:, :D]), _sigmoid(gg + bg[:, D:])
        dgate = jnp.concatenate([du * ys * s1 * (1.0 - s1), du * yg * s2 * (1.0 - s2)], axis=1)
        return du * s1, du * s2, dgate, jnp.sum(dgate, axis=0, keepdims=True)

    dysb, dygdn, dgate, d_bg = _mm(
        "mix_dmerge", dr, w_mo, S, D, D, tm=tm, tn=D, tk=D, tb=True,
        extras=[(ysb, (tm, D), _row), (ygdn, (tm, D), _row), (m, (tm, D), lambda i, j, k: (i, GATE0 // D)),
                (m, (tm, D), lambda i, j, k: (i, GATE0 // D + 1)), (b_gate, (1, 2 * D), _const)],
        epilogue=merge_epi, n_acc=1,
        outs=[((S, D), BF, (tm, D), _row), ((S, D), BF, (tm, D), _row), ((S, 2 * D), BF, (tm, 2 * D), _row),
              ((1, 2 * D), F32, (1, 2 * D), _const)])
    d_w_mo, = _mm("mix_dwmo", u, dr, D, D, S, tm=D, tn=D, tk=min(BIG_ROWS, S), ta=True, outs=[((D, D), F32, (D, D), _tile)])
    dya, = _mm("mix_dya", dysb, w_sb, S, n, D, tm=tm, tn=n, tk=D, tb=True, outs=[((S, n), F32, (tm, n), _row)])
    col_shards = [((NSH, n, D // NSH), F32, (None, n, D // NSH), lambda i, j, k: (j, 0, 0))]
    d_w_sb, = _mm("mix_dwsb", ya, dysb, n, D, S, tm=n, tn=D // NSH, tk=min(BIG_ROWS, S), ta=True, order="ji", outs=col_shards)
    dog, = _mm("mix_dog", dygdn, w_gdn, S, n, D, tm=tm, tn=n, tk=D, tb=True, outs=[((S, n), F32, (tm, n), _row)])
    d_w_gdn, = _mm("mix_dwgdn", og, dygdn, n, D, S, tm=n, tn=D // NSH, tk=min(BIG_ROWS, S), ta=True, order="ji", outs=col_shards)
    dq, dk, dv, *arrived = _attn_bwd_t(m, *runs, dya, ride=ride)
    dyc, dgz, dgba, d_alog, d_dtb, d_normw = _gdn_bwd(yc, m, alog, dtb, normw, states, dog)
    dxc, d_conv = _conv_bwd(m, dyc, conv_w)
    dm = jnp.concatenate([dq, dk.astype(BF), dv.astype(BF), dxc, dgz, dgba, jnp.zeros((S, GATE0 - 3712), BF), dgate], axis=1)
    tb_ = min(BIG_ROWS, S)
    d_h1, = _mm("mix_dh", dm, wp, S, D, NP, tm=tb_, tn=D, tk=1536, tb=True,
                extras=[(dr, (tb_, D), _row)], epilogue=lambda acc, d: (acc + ALPHA * d,),
                outs=[((S, D), F32, (tb_, D), _row)])
    d_wp, = _mm("mix_dwp", h1, dm, D, NP, S, tm=D, tn=1536, tk=tb_, ta=True, order="ji",
                outs=[((D, NP), F32, (D, 1536), _tile)])
    return d_h1, dict(wp=d_wp, b_gate=d_bg, conv_w=d_conv, alog=d_alog, dtb=d_dtb, normw=d_normw,
                      w_sb=d_w_sb, w_gdn=d_w_gdn, w_mo=d_w_mo, g=dg, b=db), arrived


def _ple_fwd(h3, p, w_pg, b_pg, w_ple, g, b, target):
    S = h3.shape[0]
    tm = min(512, S)
    pd = p.shape[1]
    pe, = _mm("ple_emb", p, w_ple, S, D, pd, tm=tm, tn=D, tk=pd, outs=[((S, D), F32, (tm, D), _row)])

    def epi(acc, e, xin, tgt, bp, gg, bb):
        gt = _sigmoid(acc + bp)
        r = ALPHA * xin + gt * e
        diff = _layer_norm(r, gg, bb) - tgt
        return gt, r, diff * (1.0 / D), jnp.sum(diff * diff, axis=0, keepdims=True)

    gt, r4, dh4, loss_row = _mm(
        "ple_gate", h3, w_pg, S, D, D, tm=tm, tn=D, tk=D,
        extras=[(pe, (tm, D), _row), (h3, (tm, D), _row), (target, (tm, D), _row), (b_pg, (1, D), _const),
                (g, (1, D), _const), (b, (1, D), _const)], epilogue=epi, n_acc=1,
        outs=[((S, D), F32, (tm, D), _row), ((S, D), F32, (tm, D), _row), ((S, D), F32, (tm, D), _row),
              ((1, D), F32, (1, D), _const)])
    return dh4, loss_row, (h3, p, pe, gt, r4)


def _ple_bwd(saved, w_pg, g, dh4):
    h3, p, pe, gt, r4 = saved
    S = h3.shape[0]
    tm = min(512, S)
    pd = p.shape[1]

    def fn(r, gg, dh, e, t):
        dr, dg, db = _layer_norm_bwd(r, gg, dh)
        dpre = dr * e * t * (1.0 - t)
        return dr, dpre, dr * t, dg, db, jnp.sum(dpre, axis=0, keepdims=True)

    row, one = (lambda i: (i, 0)), (lambda i: (0, 0))
    dr, dpre, dpe, dg, db, d_bpg = _rows(
        "ple_lnbwd", fn, S // tm,
        [(r4, (tm, D), row), (g, (1, D), one), (dh4, (tm, D), row), (pe, (tm, D), row), (gt, (tm, D), row)],
        [((S, D), F32, (tm, D), row), ((S, D), BF, (tm, D), row), ((S, D), BF, (tm, D), row),
         ((1, D), F32, (1, D), one), ((1, D), F32, (1, D), one), ((1, D), F32, (1, D), one)], n_acc=3)
    d_w_pg, = _mm("ple_dwpg", h3, dpre, D, D, S, tm=D, tn=D, tk=min(BIG_ROWS, S), ta=True, outs=[((D, D), F32, (D, D), _tile)])
    d_w_ple, = _mm("ple_dwple", p, dpe, pd, D, S, tm=pd, tn=D // NSH, tk=min(BIG_ROWS, S), ta=True, order="ji",
                   outs=[((NSH, pd, D // NSH), F32, (None, pd, D // NSH), lambda i, j, k: (j, 0, 0))])
    d_h3, = _mm("ple_dh", dpre, w_pg, S, D, D, tm=tm, tn=D, tk=D, tb=True,
                extras=[(dr, (tm, D), _row)], epilogue=lambda acc, d: (acc + ALPHA * d,),
                outs=[((S, D), F32, (tm, D), _row)])
    return d_h3, d_w_pg, d_bpg, d_w_ple, dg, db


def _local_step(x, p, target, W, late=None, early=None):
    W = dict(W)
    h1, sv1 = _ffn_fwd("ffn1", x, W["ffn1_in"], W["ffn1_out"], W["ln1_g"], W["ln1_b"])
    h2, sv2 = _mixer_fwd(h1, W, late)
    h3, sv3 = _ffn_fwd("ffn2", h2, W["ffn2_in"], W["ffn2_out"], W["ln3_g"], W["ln3_b"])
    dh4, loss_row, sv4 = _ple_fwd(h3, p, W["w_pg"], W["b_pg"], W["w_ple"], W["ln4_g"], W["ln4_b"], target)
    G = {}
    dh3, G["w_pg"], G["b_pg"], G["w_ple"], G["ln4_g"], G["ln4_b"] = _ple_bwd(sv4, W["w_pg"], W["ln4_g"], dh4)
    dh2, G["ffn2_in"], G["ffn2_out"], G["ln3_g"], G["ln3_b"] = _ffn_bwd("ffn2", sv3, W["ffn2_in"], W["ffn2_out"], W["ln3_g"], dh3)
    dh1, gm, G["early"] = _mixer_bwd(sv2, W, dh2, ride=early(G) if early else None)
    G.update({k: v for k, v in gm.items() if k not in ("g", "b")})
    G["ln2_g"], G["ln2_b"] = gm["g"], gm["b"]
    dx, G["ffn1_in"], G["ffn1_out"], G["ln1_g"], G["ln1_b"] = _ffn_bwd("ffn1", sv1, W["ffn1_in"], W["ffn1_out"], W["ln1_g"], dh1)
    return loss_row, dx, G


S2 = CUT - 2 * MSH


def _pack_wp(w4):
    tr = 256

    def fn(w):
        s = [w[j].astype(F32) for j in range(NSH)]
        full = jnp.concatenate([s[0], s[1], s[2][:, :S2], jnp.zeros((tr, GATE0 - CUT), F32), s[2][:, S2:], s[3]], axis=1)
        return (full,)

    return _rows("pack_wp", fn, D // tr, [(w4, (NSH, tr, MSH), lambda i: (0, i, 0))],
                 [((D, NP), BF, (tr, NP), lambda i: (i, 0))])[0]


def _unpack_wp(d):
    tr = 256
    g2 = GATE0 + MSH - S2

    def fn(v):
        return (jnp.stack([v[:, :MSH], v[:, MSH:2 * MSH], jnp.concatenate([v[:, 2 * MSH:CUT], v[:, GATE0:g2]], axis=1), v[:, g2:]]),)

    return _rows("unpack_wp", fn, D // tr, [(d, (tr, NP), lambda i: (i, 0))],
                 [((NSH, D, MSH), F32, (NSH, tr, MSH), lambda i: (0, i, 0))])[0]


def _cast_bf16(tag, w):
    r, c = w.shape
    tr = _pick(r, 256)
    return _rows(f"cast_{tag}", lambda v: (v,), r // tr, [(w, (tr, c), lambda i: (i, 0))],
                 [((r, c), BF, (tr, c), lambda i: (i, 0))])[0]


ANY = pl.BlockSpec(memory_space=pl.ANY)


def _place():
    return lax.axis_index("x"), lax.axis_index("y"), lax.axis_index("c")


def _chip_exchange(phase, scatter, ins, outs, send, recv, loc):
    x, y, c = _place()
    me = 2 * x + y
    chips = [(1 - x, y), (x, 1 - y), (1 - x, 1 - y)]
    for t in range(len(ins)):
        own = pltpu.make_async_copy(ins[t].at[me] if scatter else ins[t], outs[t].at[me], loc.at[t])
        out_going, in_coming = [], []
        for q, (px, py) in enumerate(chips):
            src = ins[t].at[2 * px + py] if scatter else ins[t]
            sems = dict(send_sem=send.at[3 * t + q], recv_sem=recv.at[3 * t + q], device_id=(px, py, c), device_id_type=MESH)
            out_going.append(pltpu.make_async_remote_copy(src_ref=src, dst_ref=outs[t].at[me], **sems))
            in_coming.append(pltpu.make_async_remote_copy(src_ref=src, dst_ref=outs[t].at[2 * px + py], **sems))
        if phase == "start":
            own.start()
            for cp in out_going:
                cp.start()
        else:
            for cp in in_coming:
                cp.wait_recv()
            own.wait()
            for cp in out_going:
                cp.wait_send()


def _exchange_sems(n):
    return [pltpu.SemaphoreType.DMA((3 * n,)), pltpu.SemaphoreType.DMA((3 * n,)), pltpu.SemaphoreType.DMA((n,))]


class _Ride:
    def __init__(self, scatter, arrays):
        self.scatter, self.arrays, self.n = scatter, list(arrays), len(arrays)
        self.out_shape = [jax.ShapeDtypeStruct(a.shape if scatter else (NSH,) + a.shape, a.dtype) for a in self.arrays]

    def run(self, phase, refs_in, refs_out, sems):
        _chip_exchange(phase, self.scatter, refs_in, refs_out, *sems)


def _gather_shards(shards):
    n = len(shards)

    def body(*refs):
        _chip_exchange("start", False, refs[:n], refs[n:2 * n], *refs[2 * n:])
        _chip_exchange("wait", False, refs[:n], refs[n:2 * n], *refs[2 * n:])

    return pl.pallas_call(
        body, name="gather_weights", in_specs=[ANY] * n, out_specs=[ANY] * n,
        out_shape=[jax.ShapeDtypeStruct((NSH,) + s.shape, s.dtype) for s in shards], scratch_shapes=_exchange_sems(n),
    )(*shards)


def _gather_shards_v1(shards):
    n = len(shards)

    def body(*refs):
        ins, outs = refs[:n], refs[n:2 * n]
        send, recv, loc = refs[2 * n:]
        x, y, c = _place()
        me = 2 * x + y
        chips = [(1 - x, y), (x, 1 - y), (1 - x, 1 - y)]
        started = []
        for t in range(n):
            lc = pltpu.make_async_copy(ins[t], outs[t].at[me], loc.at[t])
            lc.start()
            started.append(lc)
            for q, (px, py) in enumerate(chips):
                cp = pltpu.make_async_remote_copy(src_ref=ins[t], dst_ref=outs[t].at[me], send_sem=send.at[3 * t + q],
                                                  recv_sem=recv.at[3 * t + q], device_id=(px, py, c), device_id_type=MESH)
                cp.start()
                started.append(cp)
        for t in range(n):
            for q, (px, py) in enumerate(chips):
                pltpu.make_async_remote_copy(src_ref=ins[t], dst_ref=outs[t].at[2 * px + py], send_sem=send.at[3 * t + q],
                                             recv_sem=recv.at[3 * t + q], device_id=(px, py, c), device_id_type=MESH).wait_recv()
        for t in range(n):
            started[4 * t].wait()
            for q in range(3):
                started[4 * t + 1 + q].wait_send()

    return pl.pallas_call(
        body, name="gather_weights", in_specs=[ANY] * n, out_specs=[ANY] * n,
        out_shape=[jax.ShapeDtypeStruct((NSH,) + s.shape, s.dtype) for s in shards],
        scratch_shapes=[pltpu.SemaphoreType.DMA((3 * n,)), pltpu.SemaphoreType.DMA((3 * n,)), pltpu.SemaphoreType.DMA((n,))],
    )(*shards)


def _reduce_sibling(gs):
    n = len(gs)

    def body(*refs):
        ins, mine, theirs = refs[:n], refs[n:2 * n], refs[2 * n:3 * n]
        send, recv, loc = refs[3 * n:]
        x, y, c = _place()
        started = []
        for t in range(n):
            h = gs[t].shape[1] // 2
            lc = pltpu.make_async_copy(ins[t].at[:, pl.ds(pl.multiple_of(c * h, 8), h), :], mine[t], loc.at[t])
            cp = pltpu.make_async_remote_copy(src_ref=ins[t].at[:, pl.ds(pl.multiple_of((1 - c) * h, 8), h), :], dst_ref=theirs[t],
                                              send_sem=send.at[t], recv_sem=recv.at[t], device_id=(x, y, 1 - c), device_id_type=MESH)
            lc.start()
            cp.start()
            started += [lc, cp]
        for t in range(n):
            started[2 * t].wait()
            started[2 * t + 1].wait()

    half = [jax.ShapeDtypeStruct((NSH, g.shape[1] // 2, g.shape[2]), g.dtype) for g in gs]
    res = pl.pallas_call(
        body, name="reduce_sibling", in_specs=[ANY] * n, out_specs=[ANY] * (2 * n), out_shape=half + half,
        scratch_shapes=[pltpu.SemaphoreType.DMA((n,)), pltpu.SemaphoreType.DMA((n,)), pltpu.SemaphoreType.DMA((n,))],
    )(*gs)
    return res[:n], res[n:]


def _reduce_chips(ps):
    n = len(ps)

    def body(*refs):
        _chip_exchange("start", True, refs[:n], refs[n:2 * n], *refs[2 * n:])
        _chip_exchange("wait", True, refs[:n], refs[n:2 * n], *refs[2 * n:])

    return pl.pallas_call(
        body, name="reduce_chips", in_specs=[ANY] * n, out_specs=[ANY] * n,
        out_shape=[jax.ShapeDtypeStruct(p_.shape, p_.dtype) for p_ in ps], scratch_shapes=_exchange_sems(n),
    )(*ps)


def _reduce_chips_v1(ps):
    n = len(ps)

    def body(*refs):
        ins, outs = refs[:n], refs[n:2 * n]
        send, recv, loc = refs[2 * n:]
        x, y, c = _place()
        me = 2 * x + y
        chips = [(1 - x, y), (x, 1 - y), (1 - x, 1 - y)]
        started = []
        for t in range(n):
            lc = pltpu.make_async_copy(ins[t].at[me], outs[t].at[me], loc.at[t])
            lc.start()
            started.append(lc)
            for q, (px, py) in enumerate(chips):
                cp = pltpu.make_async_remote_copy(src_ref=ins[t].at[2 * px + py], dst_ref=outs[t].at[me], send_sem=send.at[3 * t + q],
                                                  recv_sem=recv.at[3 * t + q], device_id=(px, py, c), device_id_type=MESH)
                cp.start()
                started.append(cp)
        for t in range(n):
            for q, (px, py) in enumerate(chips):
                pltpu.make_async_remote_copy(src_ref=ins[t].at[me], dst_ref=outs[t].at[2 * px + py], send_sem=send.at[3 * t + q],
                                             recv_sem=recv.at[3 * t + q], device_id=(px, py, c), device_id_type=MESH).wait_recv()
        for t in range(n):
            started[4 * t].wait()
            for q in range(3):
                started[4 * t + 1 + q].wait_send()

    return pl.pallas_call(
        body, name="reduce_chips", in_specs=[ANY] * n, out_specs=[ANY] * n,
        out_shape=[jax.ShapeDtypeStruct(p_.shape, p_.dtype) for p_ in ps],
        scratch_shapes=[pltpu.SemaphoreType.DMA((3 * n,)), pltpu.SemaphoreType.DMA((3 * n,)), pltpu.SemaphoreType.DMA((n,))],
    )(*ps)


def _share_sibling(fs):
    n = len(fs)

    def body(*refs):
        ins, outs = refs[:n], refs[n:2 * n]
        send, recv, loc = refs[2 * n:]
        x, y, c = _place()
        started = []
        for t in range(n):
            lc = pltpu.make_async_copy(ins[t], outs[t].at[c], loc.at[t])
            cp = pltpu.make_async_remote_copy(src_ref=ins[t], dst_ref=outs[t].at[c], send_sem=send.at[t], recv_sem=recv.at[t],
                                              device_id=(x, y, 1 - c), device_id_type=MESH)
            lc.start()
            cp.start()
            started += [lc, cp]
        for t in range(n):
            started[2 * t].wait()
            started[2 * t + 1].wait_send()
            pltpu.make_async_remote_copy(src_ref=ins[t], dst_ref=outs[t].at[1 - c], send_sem=send.at[t], recv_sem=recv.at[t],
                                         device_id=(x, y, 1 - c), device_id_type=MESH).wait_recv()

    return pl.pallas_call(
        body, name="share_sibling", in_specs=[ANY] * n, out_specs=[ANY] * n,
        out_shape=[jax.ShapeDtypeStruct((2,) + f.shape, f.dtype) for f in fs],
        scratch_shapes=[pltpu.SemaphoreType.DMA((n,)), pltpu.SemaphoreType.DMA((n,)), pltpu.SemaphoreType.DMA((n,))],
    )(*fs)


def _chunk_rows(h, c):
    return _pick(h, max(8, 262144 // c // 8 * 8))


def _sibling_sum(tag, g):
    _, r, c = g.shape
    h = r // 2
    tr = _chunk_rows(h, c)
    nch = h // tr
    steps = NSH * nch

    def body(top_ref, bot_ref, out_ref, narrow_ref, land, send, recv):
        s = pl.program_id(0)
        x, y, core = _place()

        def exchange(keep_ref, give_ref):
            cp = pltpu.make_async_remote_copy(src_ref=give_ref.at[0], dst_ref=land.at[s], send_sem=send.at[s], recv_sem=recv.at[s],
                                              device_id=(x, y, 1 - core), device_id_type=MESH)
            cp.start()
            cp.wait()
            total = keep_ref[0] + land[s]
            out_ref[0] = total
            narrow_ref[0] = total.astype(BF)

        @pl.when(core == 0)
        def _():
            exchange(top_ref, bot_ref)

        @pl.when(core == 1)
        def _():
            exchange(bot_ref, top_ref)

    return pl.pallas_call(
        body, name=f"sibling_sum_{tag}", grid=(steps,),
        in_specs=[pl.BlockSpec((1, tr, c), lambda s: (s // nch, s % nch, 0)),
                  pl.BlockSpec((1, tr, c), lambda s: (s // nch, nch + s % nch, 0))],
        out_specs=[pl.BlockSpec((1, tr, c), lambda s: (s // nch, s % nch, 0))] * 2,
        out_shape=[jax.ShapeDtypeStruct((NSH, h, c), F32), jax.ShapeDtypeStruct((NSH, h, c), BF)],
        scratch_shapes=[pltpu.VMEM((steps, tr, c), F32), pltpu.SemaphoreType.DMA((steps,)), pltpu.SemaphoreType.DMA((steps,))],
    )(g, g)


def _chip_sum_share(tag, b, own):
    _, h, c = b.shape
    tr = _chunk_rows(h, c)
    steps = h // tr

    def body(b_ref, own_ref, out_ref, stage, land, send, recv):
        s = pl.program_id(0)
        x, y, core = _place()
        me = 2 * x + y
        v = [jnp.where(me == j, own_ref[j], b_ref[j].astype(F32)) for j in range(NSH)]
        total = ((v[0] + v[1]) + v[2]) + v[3]
        stage[...] = total
        cp = pltpu.make_async_remote_copy(src_ref=stage, dst_ref=land.at[s], send_sem=send.at[s], recv_sem=recv.at[s],
                                          device_id=(x, y, 1 - core), device_id_type=MESH)
        cp.start()
        cp.wait()
        out_ref[core] = total
        out_ref[1 - core] = land[s]

    return pl.pallas_call(
        body, name=f"chip_sum_share_{tag}", grid=(steps,),
        in_specs=[pl.BlockSpec((NSH, tr, c), lambda s: (0, s, 0))] * 2,
        out_specs=pl.BlockSpec((2, tr, c), lambda s: (0, s, 0)),
        out_shape=jax.ShapeDtypeStruct((2, h, c), F32),
        scratch_shapes=[pltpu.VMEM((tr, c), F32), pltpu.VMEM((steps, tr, c), F32), pltpu.SemaphoreType.DMA((steps,)),
                        pltpu.SemaphoreType.DMA((steps,))],
    )(b, own)


NDEV = 8


def _allreduce_small(pack):
    r, w = pack.shape
    rel = [(dx, dy, dc) for dx in (0, 1) for dy in (0, 1) for dc in (0, 1) if (dx, dy, dc) != (0, 0, 0)]

    def body(in_ref, out_ref, buf, send, recv):
        x, y, c = _place()
        me = 4 * x + 2 * y + c
        buf[me] = in_ref[...]
        peers = [((x + dx) % 2, (y + dy) % 2, (c + dc) % 2) for dx, dy, dc in rel]
        sent = []
        for k, peer in enumerate(peers):
            cp = pltpu.make_async_remote_copy(src_ref=in_ref, dst_ref=buf.at[me], send_sem=send.at[k], recv_sem=recv.at[k],
                                              device_id=peer, device_id_type=MESH)
            cp.start()
            sent.append(cp)
        for k, (px, py, pc) in enumerate(peers):
            pltpu.make_async_remote_copy(src_ref=in_ref, dst_ref=buf.at[4 * px + 2 * py + pc], send_sem=send.at[k], recv_sem=recv.at[k],
                                         device_id=(px, py, pc), device_id_type=MESH).wait_recv()
        for cp in sent:
            cp.wait_send()
        acc = buf[0]
        for k in range(1, NDEV):
            acc = acc + buf[k]
        out_ref[...] = acc

    vm = pl.BlockSpec(memory_space=pltpu.VMEM)
    return pl.pallas_call(
        body, name="allreduce_small", in_specs=[vm], out_specs=vm, out_shape=jax.ShapeDtypeStruct((r, w), F32),
        scratch_shapes=[pltpu.VMEM((NDEV, r, w), F32), pltpu.SemaphoreType.DMA((NDEV - 1,)), pltpu.SemaphoreType.DMA((NDEV - 1,))],
    )(pack)


def _add2(tag, a, b):
    _, h, c = a.shape
    tr = _pick(h, max(8, 262144 // c // 8 * 8))
    spec = ((None, tr, c), lambda i: (i // (h // tr), i % (h // tr), 0))
    return _rows(f"add2_{tag}", lambda u, v: (u + v,), NSH * (h // tr), [(a,) + spec, (b,) + spec], [(a.shape, F32) + spec])[0]


def _add4(tag, a):
    _, h, c = a.shape
    tr = _pick(h, max(8, 131072 // c // 8 * 8))
    return _rows(f"add4_{tag}", lambda v: (((v[0] + v[1]) + v[2]) + v[3],), h // tr, [(a, (NSH, tr, c), lambda i: (0, i, 0))],
                 [((h, c), F32, (tr, c), lambda i: (i, 0))])[0]


def _adamw(tag, w, g, m, v):
    r, c = w.shape
    tr = _pick(r, max(8, 262144 // c // 8 * 8))

    def fn(w_, g_, m_, v_):
        m2 = B1 * m_ + (1.0 - B1) * g_
        v2 = B2 * v_ + (1.0 - B2) * (g_ * g_)
        m_hat = m2 / (1.0 - B1 ** STEP)
        v_hat = v2 / (1.0 - B2 ** STEP)
        return -LR * (m_hat / (jnp.sqrt(v_hat) + EPS) + WD * w_), m2, v2

    spec = ((tr, c), lambda i: (i, 0))
    return _rows(f"adamw_{tag}", fn, r // tr, [(a,) + spec for a in (w, g, m, v)], [((r, c), F32) + spec] * 3)


BIG = ("ffn1_w_in", "ffn1_w_out", "w_mix_in", "w_branch_sb", "w_branch_gdn", "w_mix_out", "ffn2_w_in", "ffn2_w_out",
       "w_ple_gate", "w_ple")
FIRST = ("ffn1_w_in", "ffn1_w_out", "w_mix_in")
LATER = tuple(n for n in BIG if n not in FIRST)
EARLY = ("ffn2_w_in", "ffn2_w_out", "w_ple_gate", "w_ple")
REST = tuple(n for n in BIG if n not in EARLY)
SMALL = ("ln1_g", "ln1_b", "b_gate", "conv_w", "a_log", "dt_bias", "gdn_norm_w", "ln2_g", "ln2_b", "ln3_g", "ln3_b",
         "b_ple_gate", "ln4_g", "ln4_b")
ORDER = ("ffn1_w_in", "ffn1_w_out", "ln1_g", "ln1_b", "w_mix_in", "b_gate", "conv_w", "a_log", "dt_bias", "gdn_norm_w",
         "w_branch_sb", "w_branch_gdn", "w_mix_out", "ln2_g", "ln2_b", "ffn2_w_in", "ffn2_w_out", "ln3_g", "ln3_b",
         "w_ple_gate", "b_ple_gate", "w_ple", "ln4_g", "ln4_b")
PACK_W = 2304


def _lane_row(v, lanes=128, at=HEADS):
    return jnp.pad(v[None, :], ((0, 0), (at, lanes - at - v.shape[0])))


def _col_join(w4):
    return jnp.transpose(w4, (1, 0, 2)).reshape(w4.shape[1], NSH * w4.shape[2])


def kernel(x, p, ffn1_w_in, ffn1_w_out, ln1_g, ln1_b, w_mix_in, b_gate, conv_w, a_log, dt_bias, gdn_norm_w, w_branch_sb, w_branch_gdn, w_mix_out, ln2_g, ln2_b, ffn2_w_in, ffn2_w_out, ln3_g, ln3_b, w_ple_gate, b_ple_gate, w_ple, ln4_g, ln4_b, loss_target, m_ffn1_w_in, m_ffn1_w_out, m_ln1_g, m_ln1_b, m_w_mix_in, m_b_gate, m_conv_w, m_a_log, m_dt_bias, m_gdn_norm_w, m_w_branch_sb, m_w_branch_gdn, m_w_mix_out, m_ln2_g, m_ln2_b, m_ffn2_w_in, m_ffn2_w_out, m_ln3_g, m_ln3_b, m_w_ple_gate, m_b_ple_gate, m_w_ple, m_ln4_g, m_ln4_b, v_ffn1_w_in, v_ffn1_w_out, v_ln1_g, v_ln1_b, v_w_mix_in, v_b_gate, v_conv_w, v_a_log, v_dt_bias, v_gdn_norm_w, v_w_branch_sb, v_w_branch_gdn, v_w_mix_out, v_ln2_g, v_ln2_b, v_ffn2_w_in, v_ffn2_w_out, v_ln3_g, v_ln3_b, v_w_ple_gate, v_b_ple_gate, v_w_ple, v_ln4_g, v_ln4_b):
    args = dict(locals())
    w = {n: args[n][0] for n in ORDER}
    mom = {n: args["m_" + n][0] for n in ORDER}
    var = {n: args["v_" + n][0] for n in ORDER}

    cast = {n: _cast_bf16(n, w[n]) for n in BIG}
    full = dict(zip(FIRST + ("conv_w",), _gather_shards([cast[n] for n in FIRST] + [w["conv_w"]])))
    W = dict(
        ffn1_in=full["ffn1_w_in"], ffn1_out=full["ffn1_w_out"].reshape(DFF, D), wp=_pack_wp(full["w_mix_in"]),
        conv_w=_col_join(full["conv_w"]), b_gate=w["b_gate"][None], alog=_lane_row(w["a_log"]), dtb=_lane_row(w["dt_bias"]),
        normw=w["gdn_norm_w"][None], b_pg=w["b_ple_gate"][None],
        **{f"ln{i}_{s}": w[f"ln{i}_{s}"][None] for i in (1, 2, 3, 4) for s in ("g", "b")},
    )

    def fill(W_, arrived):
        got = dict(zip(LATER, arrived))
        W_.update(w_sb=_col_join(got["w_branch_sb"]), w_gdn=_col_join(got["w_branch_gdn"]), w_mo=got["w_mix_out"].reshape(D, D),
                  ffn2_in=got["ffn2_w_in"], ffn2_out=got["ffn2_w_out"].reshape(DFF, D),
                  w_pg=got["w_ple_gate"].reshape(D, D), w_ple=_col_join(got["w_ple"]))

    def by_shard(G_, names):
        forms = dict(
            ffn1_w_in=lambda: G_["ffn1_in"], ffn1_w_out=lambda: G_["ffn1_out"].reshape(NSH, DFF // NSH, D),
            w_mix_in=lambda: _unpack_wp(G_["wp"]), w_branch_sb=lambda: G_["w_sb"], w_branch_gdn=lambda: G_["w_gdn"],
            w_mix_out=lambda: G_["w_mo"].reshape(NSH, D // NSH, D), ffn2_w_in=lambda: G_["ffn2_in"],
            ffn2_w_out=lambda: G_["ffn2_out"].reshape(NSH, DFF // NSH, D),
            w_ple_gate=lambda: G_["w_pg"].reshape(NSH, D // NSH, D), w_ple=lambda: G_["w_ple"])
        return [_sibling_sum(n, forms[n]()) for n in names]

    early_sums = []

    def early(G_):
        early_sums.extend(by_shard(G_, EARLY))
        return _Ride(True, [narrow for _, narrow in early_sums])

    loss_row, grad_x, G = _local_step(x[0], p[0, 0], loss_target[0], W, late=(_Ride(False, [cast[n] for n in LATER]), fill),
                                      early=early)
    loss = lax.psum(0.5 * jnp.sum(loss_row) / D, ("x", "y", "c"))

    rest_sums = by_shard(G, REST)
    landed = list(G["early"]) + list(_reduce_chips([narrow for _, narrow in rest_sums]))
    grad = {n: _chip_sum_share(n, b, own).reshape(w[n].shape)
            for n, b, (own, _) in zip(EARLY + REST, landed, early_sums + rest_sums)}

    pieces = [G["ln1_g"], G["ln1_b"], G["b_gate"], G["conv_w"].reshape(1, 4 * CW), G["alog"], G["dtb"], G["normw"],
              G["ln2_g"], G["ln2_b"], G["ln3_g"], G["ln3_b"], G["b_pg"], G["ln4_g"], G["ln4_b"]]
    flat = jnp.concatenate(pieces, axis=1)
    flat = jnp.pad(flat, ((0, 0), (0, NDEV * PACK_W - flat.shape[1])))
    total = _allreduce_small(flat.reshape(NDEV, PACK_W)).reshape(1, NDEV * PACK_W)
    off = 0
    for n, piece in zip(SMALL, pieces):
        grad[n] = total[0, off:off + piece.shape[1]]
        off += piece.shape[1]
    chip = 2 * lax.axis_index("x") + lax.axis_index("y")
    grad["conv_w"] = lax.dynamic_slice_in_dim(grad["conv_w"].reshape(4, CW), chip * (CW // NSH), CW // NSH, axis=1)
    grad["a_log"] = grad["a_log"][HEADS:2 * HEADS]
    grad["dt_bias"] = grad["dt_bias"][HEADS:2 * HEADS]

    delta, new_m, new_v = {}, {}, {}
    for n in ORDER:
        shape2 = w[n].shape if w[n].ndim == 2 else (1, w[n].shape[0])
        d_, m_, v_ = _adamw(n, *[a.reshape(shape2) for a in (w[n], grad[n], mom[n], var[n])])
        delta[n], new_m[n], new_v[n] = (a.reshape(args[n].shape) for a in (d_, m_, v_))
    outs = [loss, grad_x[None]]
    outs += [grad[n].reshape(args[n].shape) for n in ORDER]
    for group in (delta, new_m, new_v):
        outs += [group[n] for n in ORDER]
    return tuple(outs)
```

```python
import functools

import jax
import jax.numpy as jnp
from jax import lax
from jax.experimental import pallas as pl
from jax.experimental.pallas import tpu as pltpu

F32 = jnp.float32
BF = jnp.bfloat16
I32 = jnp.int32
HI = lax.Precision.HIGHEST
MESH = pl.DeviceIdType.MESH

D = 1024
DFF = 2816
NSH = 4
FSH = 2 * DFF // NSH
NIN = 5648
MSH = NIN // NSH
NP = 6144
CUT = 3600
GATE0 = 4096
HEADS = 8
HD = 64
CH = 64
KB = 128
BIG_ROWS = 1024
ALPHA = 2.0 ** 0.25
LN_EPS = 1e-5
RMS_EPS = 1e-6
B1, B2, LR, EPS, WD, STEP = 0.9, 0.999, 0.001, 1e-08, 0.01, 10


def _sigmoid(x):
    return 0.5 * jnp.tanh(0.5 * x) + 0.5


def _softplus(x):
    return jnp.maximum(x, 0.0) + jnp.log1p(jnp.exp(-jnp.abs(x)))


def _layer_norm(r, g, b):
    mu = jnp.mean(r, axis=-1, keepdims=True)
    xc = r - mu
    var = jnp.mean(xc * xc, axis=-1, keepdims=True)
    return xc * lax.rsqrt(var + LN_EPS) * g + b


def _layer_norm_bwd(r, g, dh):
    mu = jnp.mean(r, axis=-1, keepdims=True)
    xc = r - mu
    var = jnp.mean(xc * xc, axis=-1, keepdims=True)
    xhat = xc * lax.rsqrt(var + LN_EPS)
    dxh = dh * g
    dr = lax.rsqrt(var + LN_EPS) * (dxh - jnp.mean(dxh, axis=-1, keepdims=True) - xhat * jnp.mean(dxh * xhat, axis=-1, keepdims=True))
    return dr, jnp.sum(dh * xhat, axis=0, keepdims=True), jnp.sum(dh, axis=0, keepdims=True)


def _pick(n, cap):
    if n <= cap:
        return n
    for t in range(cap - cap % 8, 7, -8):
        if n % t == 0:
            return t
    raise ValueError((n, cap))


def _mm(name, a, b, M, N, K, *, tm, tn, tk, ta=False, tb=False, a_spec=None, b_spec=None, order="ij",
        extras=(), epilogue=None, outs, n_acc=0, ride=None):
    ni, nj, nk = M // tm, N // tn, K // tk
    nr = ride.n if ride else 0
    assert M % tm == 0 and N % tn == 0 and K % tk == 0, (name, M, N, K, tm, tn, tk)
    assert n_acc == 0 or nj == 1

    def wrap(fn):
        if order == "ij":
            return lambda g0, g1, g2: fn(g0, g1, g2)
        return lambda g0, g1, g2: fn(g1, g0, g2)

    if a_spec is None:
        a_spec = ((tk, tm), lambda i, j, k: (k, i)) if ta else ((tm, tk), lambda i, j, k: (i, k))
    if b_spec is None:
        b_spec = ((tn, tk), lambda i, j, k: (j, k)) if tb else ((tk, tn), lambda i, j, k: (k, j))
    dims = (((0 if ta else 1,), (1 if tb else 0,)), ((), ()))
    ne, no = len(extras), len(outs)
    grid = (ni, nj, nk) if order == "ij" else (nj, ni, nk)

    def body(*refs):
        a_ref, b_ref = refs[0], refs[1]
        ex = refs[2:2 + ne]
        ride_in = refs[2 + ne:2 + ne + nr]
        o = refs[2 + ne + nr:2 + ne + nr + no]
        ride_out = refs[2 + ne + nr + no:2 + ne + 2 * nr + no]
        scratch = refs[2 + ne + 2 * nr + no:]
        g0, g1, k = pl.program_id(0), pl.program_id(1), pl.program_id(2)
        first = jnp.logical_and(g0 == 0, g1 == 0)
        if ride:
            @pl.when(jnp.logical_and(first, k == 0))
            def _():
                ride.run("start", ride_in, ride_out, scratch[-3:])

        p = lax.dot_general(a_ref[...].astype(BF), b_ref[...].astype(BF), dims, preferred_element_type=F32)

        def finish(acc):
            vals = (acc,) if epilogue is None else epilogue(acc, *[e[...] for e in ex])
            for idx, (ref, val) in enumerate(zip(o, vals)):
                if idx < no - n_acc:
                    ref[...] = val.astype(ref.dtype)
                else:
                    @pl.when(first)
                    def _(ref=ref, val=val):
                        ref[...] = val

                    @pl.when(jnp.logical_not(first))
                    def _(ref=ref, val=val):
                        ref[...] += val

        if nk == 1:
            finish(p)
        else:
            acc_ref = scratch[0]

            @pl.when(k == 0)
            def _():
                acc_ref[...] = p

            @pl.when(k > 0)
            def _():
                acc_ref[...] += p

            @pl.when(k == nk - 1)
            def _():
                finish(acc_ref[...])

        if ride:
            @pl.when(jnp.logical_and(jnp.logical_and(g0 == grid[0] - 1, g1 == grid[1] - 1), k == nk - 1))
            def _():
                ride.run("wait", ride_in, ride_out, scratch[-3:])

    in_specs = [pl.BlockSpec(a_spec[0], wrap(a_spec[1])), pl.BlockSpec(b_spec[0], wrap(b_spec[1]))]
    in_specs += [pl.BlockSpec(blk, wrap(fn)) for _, blk, fn in extras] + [ANY] * nr
    res = pl.pallas_call(
        body, name=name, grid=grid, in_specs=in_specs,
        out_specs=[pl.BlockSpec(blk, wrap(fn)) for _, _, blk, fn in outs] + [ANY] * nr,
        out_shape=[jax.ShapeDtypeStruct(shape, dt) for shape, dt, _, _ in outs] + (ride.out_shape if ride else []),
        scratch_shapes=([pltpu.VMEM((tm, tn), F32)] if nk > 1 else []) + (_exchange_sems(nr) if ride else []),
    )(a, b, *[e[0] for e in extras], *(ride.arrays if ride else []))
    return res


def _row(i, j, k):
    return (i, 0)


def _tile(i, j, k):
    return (i, j)


def _const(i, j, k):
    return (0, 0)


def _rows(name, fn, n_steps, ins, outs, n_acc=0):
    ni, no = len(ins), len(outs)

    def body(*refs):
        i = pl.program_id(0)
        vals = fn(*[r[...] for r in refs[:ni]])
        for idx, (ref, val) in enumerate(zip(refs[ni:ni + no], vals)):
            if idx < no - n_acc:
                ref[...] = val.astype(ref.dtype)
            else:
                @pl.when(i == 0)
                def _(ref=ref, val=val):
                    ref[...] = val

                @pl.when(i > 0)
                def _(ref=ref, val=val):
                    ref[...] += val

    return pl.pallas_call(
        body, name=name, grid=(n_steps,),
        in_specs=[pl.BlockSpec(blk, fn_) for _, blk, fn_ in ins],
        out_specs=[pl.BlockSpec(blk, fn_) for _, _, blk, fn_ in outs],
        out_shape=[jax.ShapeDtypeStruct(shape, dt) for shape, dt, _, _ in outs],
    )(*[a for a, _, _ in ins])


def _ffn_fwd(tag, x, w_in, w_out, g, b, ride=None):
    S = x.shape[0]
    tm = min(BIG_ROWS, S)
    gate, = _mm(f"{tag}_gate", x, w_in, S, DFF, D, tm=tm, tn=FSH, tk=D, order="ji",
                b_spec=((None, D, FSH), lambda i, j, k: (j, 0, 0)),
                outs=[((S, DFF), F32, (tm, FSH), _tile)])

    def up_epi(acc, gt):
        return acc, gt * _sigmoid(gt) * acc

    up, s, *arrived = _mm(f"{tag}_up", x, w_in, S, DFF, D, tm=tm, tn=FSH, tk=D, order="ji",
                          b_spec=((None, D, FSH), lambda i, j, k: (j + 2, 0, 0)),
                          extras=[(gate, (tm, FSH), _tile)], epilogue=up_epi, ride=ride,
                          outs=[((S, DFF), F32, (tm, FSH), _tile), ((S, DFF), BF, (tm, FSH), _tile)])

    def out_epi(acc, xin, gg, bb):
        r = ALPHA * xin + 0.5 * acc
        return _layer_norm(r, gg, bb), r

    h, r = _mm(f"{tag}_out", s, w_out, S, D, DFF, tm=tm, tn=D, tk=DFF,
               extras=[(x, (tm, D), _row), (g, (1, D), _const), (b, (1, D), _const)], epilogue=out_epi,
               outs=[((S, D), F32, (tm, D), _row), ((S, D), F32, (tm, D), _row)])
    return h, (x, gate, up, s, r), arrived


def _ln_bwd(tag, r, g, dh):
    S = r.shape[0]
    tm = min(512, S)
    return _rows(f"{tag}_lnbwd", _layer_norm_bwd, S // tm,
                 [(r, (tm, D), lambda i: (i, 0)), (g, (1, D), lambda i: (0, 0)), (dh, (tm, D), lambda i: (i, 0))],
                 [((S, D), F32, (tm, D), lambda i: (i, 0)), ((1, D), F32, (1, D), lambda i: (0, 0)),
                  ((1, D), F32, (1, D), lambda i: (0, 0))], n_acc=2)


def _ffn_bwd(tag, saved, w_in, w_out, g, dh, ride=None):
    x, gate, up, s, r = saved
    S = x.shape[0]
    tm = min(BIG_ROWS, S)
    dr, dg, db = _ln_bwd(tag, r, g, dh)

    def act_epi(acc, gt, u):
        ds = 0.5 * acc
        sg = _sigmoid(gt)
        return (jnp.stack([ds * u * (sg * (1.0 + gt * (1.0 - sg))), ds * (gt * sg)]),)

    da, = _mm(f"{tag}_dact", dr, w_out, S, DFF, D, tm=tm, tn=FSH, tk=D, tb=True, order="ji",
              extras=[(gate, (tm, FSH), _tile), (up, (tm, FSH), _tile)], epilogue=act_epi,
              outs=[((2, S, DFF), BF, (2, tm, FSH), lambda i, j, k: (0, i, j))])
    d_w_out, = _mm(f"{tag}_dwout", s, dr, DFF, D, S, tm=FSH, tn=D, tk=tm, ta=True,
                   epilogue=lambda acc: (0.5 * acc,), outs=[((DFF, D), F32, (FSH, D), _tile)])
    tb_ = min(BIG_ROWS, S)
    d_in, *arrived = _mm(f"{tag}_dx", da, w_in, S, D, 2 * DFF, tm=tb_, tn=D, tk=FSH, tb=True,
                         a_spec=((None, tb_, FSH), lambda i, j, k: (k // 2, i, k % 2)),
                         b_spec=((None, D, FSH), lambda i, j, k: (k, 0, 0)),
                         extras=[(dr, (tb_, D), _row)], epilogue=lambda acc, d: (acc + ALPHA * d,), ride=ride,
                         outs=[((S, D), F32, (tb_, D), _row)])
    d_w_in, = _mm(f"{tag}_dwin", x, da, D, 2 * DFF, S, tm=D, tn=FSH, tk=tb_, ta=True, order="ji",
                  b_spec=((None, tb_, FSH), lambda i, j, k: (j // 2, k, j % 2)),
                  outs=[((NSH, D, FSH), F32, (None, D, FSH), lambda i, j, k: (j, 0, 0))])
    return d_in, d_w_in, d_w_out, dg, db, arrived


def _cum(vals, tri):
    hi = vals.astype(BF)
    lo = (vals - hi.astype(F32)).astype(BF)
    return jnp.dot(hi, tri, preferred_element_type=F32) + jnp.dot(lo, tri, preferred_element_type=F32)


def _nt(a, b):
    return lax.dot_general(a, b, (((1,), (1,)), ((), ())), preferred_element_type=F32)


def _tn(a, b):
    return lax.dot_general(a, b, (((0,), (0,)), ((), ())), preferred_element_type=F32)


def _attn_fwd(m):
    S = m.shape[0]
    tq = min(256, S)
    assert S // KB <= 128

    def body(q_ref, k_ref, v_ref, o_ref, r_ref, kb, vb):
        i = pl.program_id(1)

        @pl.when(i == 0)
        def _():
            kb[...] = k_ref[...].astype(BF)
            vb[...] = v_ref[...].astype(BF)

        qs = (q_ref[...] * 0.125).astype(BF)
        row = lax.broadcasted_iota(I32, (tq, KB), 0) + i * tq
        col = lax.broadcasted_iota(I32, (tq, KB), 1)
        tri = (lax.broadcasted_iota(I32, (KB, KB), 0) >= lax.broadcasted_iota(I32, (KB, KB), 1)).astype(BF)
        nb = (i + 1) * (tq // KB)
        r_ref[...] = jnp.zeros_like(r_ref)

        def step(t, carry):
            j = nb - 1 - t
            off = pl.multiple_of(j * KB, KB)
            mask = col + j * KB < row
            kblk = kb[pl.ds(off, KB), :]
            vblk = vb[pl.ds(off, KB), :]
            new = []
            for hh in range(2):
                run, acc = carry[2 * hh], carry[2 * hh + 1]
                sl = slice(hh * HD, (hh + 1) * HD)
                z = _nt(qs[:, sl], kblk[:, sl])
                cs = _cum(jnp.where(mask, _softplus(z), 0.0), tri)
                a = jnp.where(mask, jnp.exp(z - (run + cs)), 0.0)
                r_ref[hh] = jnp.where(col == j, run, r_ref[hh])
                new += [run + cs[:, :1], acc + jnp.dot(a.astype(BF), vblk[:, sl], preferred_element_type=F32)]
            return tuple(new)

        init = (jnp.zeros((tq, 1), F32), jnp.zeros((tq, HD), F32)) * 2
        res = lax.fori_loop(0, nb, step, init)
        o_ref[...] = jnp.concatenate([res[1], res[3]], axis=1)

    return pl.pallas_call(
        body, name="attn_fwd", grid=(HEADS // 2, S // tq),
        in_specs=[pl.BlockSpec((tq, KB), lambda h, i: (i, h)), pl.BlockSpec((S, KB), lambda h, i: (0, 4 + h)),
                  pl.BlockSpec((S, KB), lambda h, i: (0, 8 + h))],
        out_specs=[pl.BlockSpec((tq, KB), lambda h, i: (i, h)), pl.BlockSpec((2, tq, KB), lambda h, i: (h, i, 0))],
        out_shape=[jax.ShapeDtypeStruct((S, HEADS * HD), F32), jax.ShapeDtypeStruct((HEADS, S, KB), F32)],
        scratch_shapes=[pltpu.VMEM((S, KB), BF), pltpu.VMEM((S, KB), BF)],
    )(m, m, m)


def _attn_bwd(m, runs, dy):
    S = m.shape[0]
    tq = min(256, S)

    def body(q_ref, k_ref, v_ref, r_ref, dy_ref, dq_ref, dk_ref, dv_ref, kb, vb):
        i = pl.program_id(1)

        @pl.when(i == 0)
        def _():
            kb[...] = k_ref[...].astype(BF)
            vb[...] = v_ref[...].astype(BF)
            dk_ref[...] = jnp.zeros_like(dk_ref)
            dv_ref[...] = jnp.zeros_like(dv_ref)

        qs = (q_ref[...] * 0.125).astype(BF)
        dyb = dy_ref[...].astype(BF)
        rs = [r_ref[0], r_ref[1]]
        row = lax.broadcasted_iota(I32, (tq, KB), 0) + i * tq
        col = lax.broadcasted_iota(I32, (tq, KB), 1)
        jj = lax.broadcasted_iota(I32, (KB, KB), 0)
        ss = lax.broadcasted_iota(I32, (KB, KB), 1)
        tri_rev = (jj >= ss).astype(BF)
        tri_fwd = (jj <= ss).astype(BF)
        nb = (i + 1) * (tq // KB)

        def step(j, carry):
            off = pl.multiple_of(j * KB, KB)
            mask = col + j * KB < row
            kblk = kb[pl.ds(off, KB), :]
            vblk = vb[pl.ds(off, KB), :]
            new, dks, dvs = [], [], []
            for hh in range(2):
                pre, dq = carry[2 * hh], carry[2 * hh + 1]
                sl = slice(hh * HD, (hh + 1) * HD)
                run = jnp.sum(jnp.where(col == j, rs[hh], 0.0), axis=1, keepdims=True)
                z = _nt(qs[:, sl], kblk[:, sl])
                e = jnp.exp(-jnp.abs(z))
                sp = jnp.maximum(z, 0.0) + jnp.log1p(e)
                sig = jnp.where(z >= 0.0, 1.0, e) / (1.0 + e)
                cs = _cum(jnp.where(mask, sp, 0.0), tri_rev)
                a = jnp.where(mask, jnp.exp(z - (run + cs)), 0.0)
                gmat = a * _nt(dyb[:, sl], vblk[:, sl])
                pg = _cum(gmat, tri_fwd)
                dz = jnp.where(mask, gmat - sig * (pre + pg), 0.0).astype(BF)
                dks.append(_tn(dz, qs[:, sl]))
                dvs.append(_tn(a.astype(BF), dyb[:, sl]))
                new += [pre + pg[:, KB - 1:], dq + jnp.dot(dz, kblk[:, sl], preferred_element_type=F32)]
            dk_ref[pl.ds(off, KB), :] += jnp.concatenate(dks, axis=1)
            dv_ref[pl.ds(off, KB), :] += jnp.concatenate(dvs, axis=1)
            return tuple(new)

        init = (jnp.zeros((tq, 1), F32), jnp.zeros((tq, HD), F32)) * 2
        res = lax.fori_loop(0, nb, step, init)
        dq_ref[...] = (jnp.concatenate([res[1], res[3]], axis=1) * 0.125).astype(dq_ref.dtype)

    n = HEADS * HD
    return pl.pallas_call(
        body, name="attn_bwd", grid=(HEADS // 2, S // tq),
        in_specs=[pl.BlockSpec((tq, KB), lambda h, i: (i, h)), pl.BlockSpec((S, KB), lambda h, i: (0, 4 + h)),
                  pl.BlockSpec((S, KB), lambda h, i: (0, 8 + h)), pl.BlockSpec((2, tq, KB), lambda h, i: (h, i, 0)),
                  pl.BlockSpec((tq, KB), lambda h, i: (i, h))],
        out_specs=[pl.BlockSpec((tq, KB), lambda h, i: (i, h)), pl.BlockSpec((S, KB), lambda h, i: (0, h)),
                   pl.BlockSpec((S, KB), lambda h, i: (0, h))],
        out_shape=[jax.ShapeDtypeStruct((S, n), BF), jax.ShapeDtypeStruct((S, n), F32), jax.ShapeDtypeStruct((S, n), F32)],
        scratch_shapes=[pltpu.VMEM((S, KB), BF), pltpu.VMEM((S, KB), BF)],
    )(m, m, m, runs, dy)


SB = 256
NKC = SB // KB
HPS = 4
GW = HPS * HD
LOG2E = 1.4426950408889634
ANY = pl.BlockSpec(memory_space=pl.ANY)


def _split(vals):
    hi = vals.astype(BF)
    return hi, (vals - hi.astype(F32)).astype(BF)


def _chunk_sums(tri2, vals):
    hi, lo = _split(vals)
    return [jnp.dot(tri2, jnp.concatenate([hi[c * KB:(c + 1) * KB], lo[c * KB:(c + 1) * KB]], axis=0), preferred_element_type=F32)
            for c in range(NKC)]


def _head_halves(t, axis):
    idx = lax.broadcasted_iota(I32, t.shape, axis)
    return [jnp.where(idx < HD, t, 0.0).astype(BF), jnp.where(idx >= HD, t, 0.0).astype(BF)]


QT = 512


def _diag_masks(sq):
    krow, qcol = lax.broadcasted_iota(I32, (SB, sq), 0), lax.broadcasted_iota(I32, (SB, sq), 1)
    return [krow + d * SB < qcol for d in range(sq // SB)]


def _softplus2(z):
    return jnp.maximum(z, 0.0) + jnp.log2(1.0 + jnp.exp2(jnp.minimum(z, -z)))


def _attn_prep(m):
    S = m.shape[0]
    tm = min(512, S)
    n = HEADS * HD
    return _rows("attn_prep", lambda k, v: (k, v, k.T, v.T), S // tm,
                 [(m, (tm, n), lambda i: (i, 1)), (m, (tm, n), lambda i: (i, 2))],
                 [((S, n), BF, (tm, n), lambda i: (i, 0)), ((S, n), BF, (tm, n), lambda i: (i, 0)),
                  ((n, S), BF, (n, tm), lambda i: (0, i)), ((n, S), BF, (n, tm), lambda i: (0, i))])


def _attn_fwd_t(m, kb_all, vt_all, ride=None):
    S = m.shape[0]
    SQ = min(QT, S)
    assert S % SQ == 0 and SQ % SB == 0
    nkc, nqb, nh, nq = S // KB, SQ // SB, HEADS // HPS, S // SQ
    nr = ride.n if ride else 0

    def body(*refs):
        q_ref, kb_hbm, vt_hbm = refs[:3]
        ride_in, (o_ref, r_ref), ride_out = refs[3:3 + nr], refs[3 + nr:5 + nr], refs[5 + nr:5 + 2 * nr]
        kb, vt, acc = refs[5 + 2 * nr:8 + 2 * nr]
        sems = refs[8 + 2 * nr:]
        h, i = pl.program_id(0), pl.program_id(1)
        if ride:
            @pl.when(jnp.logical_and(h == 0, i == 0))
            def _():
                ride.run("start", ride_in, ride_out, sems)

        @pl.when(i == 0)
        def _():
            cols = pl.ds(pl.multiple_of(h * GW, GW), GW)
            pltpu.sync_copy(kb_hbm.at[:, cols], kb)
            pltpu.sync_copy(vt_hbm.at[cols, :], vt)

        qt = (q_ref[...] * (0.125 * LOG2E)).T
        qtm = [t for g in range(HPS // 2) for t in _head_halves(qt[g * KB:(g + 1) * KB], 0)]
        dmasks = _diag_masks(SQ)
        upper = (lax.broadcasted_iota(I32, (KB, KB), 1) >= lax.broadcasted_iota(I32, (KB, KB), 0)).astype(BF)
        tri2 = jnp.concatenate([upper, upper], axis=1)
        acc[...] = jnp.zeros_like(acc)
        r_ref[...] = jnp.zeros_like(r_ref)

        def block(jb, runs, dmask):
            masked = dmask is not None
            off = pl.multiple_of(jb * SB, SB)
            groups = [slice(g * KB, (g + 1) * KB) for g in range(HPS // 2)]
            kblk = [kb[pl.ds(off, SB), s] for s in groups]
            vtb = [vt[s, pl.ds(off, SB)] for s in groups]
            old = acc[...]
            zs = [jnp.dot(kblk[hh // 2], qtm[hh], preferred_element_type=F32) for hh in range(HPS)]
            sps = [_softplus2(z) for z in zs]
            if masked:
                sps = [jnp.where(dmask, sp, 0.0) for sp in sps]
            css = [_chunk_sums(tri2, sp) for sp in sps]
            run0s = [run + cs[1][0:1, :] for run, cs in zip(runs, css)]
            aa = [jnp.exp2(z - jnp.concatenate([run0 + cs[0], run + cs[1]], axis=0)) for z, run, run0, cs in zip(zs, runs, run0s, css)]
            if masked:
                aa = [jnp.where(dmask, a, 0.0) for a in aa]
            parts = [jnp.dot(vtb[hh // 2], aa[hh].astype(BF), preferred_element_type=F32) for hh in range(HPS)]
            upd = jnp.concatenate([parts[hh][(hh % 2) * HD:(hh % 2 + 1) * HD, :] for hh in range(HPS)], axis=0)
            for hh in range(HPS):
                r_ref[hh, pl.ds(NKC * jb, 1), :] = run0s[hh]
                r_ref[hh, pl.ds(NKC * jb + 1, 1), :] = runs[hh]
            acc[...] = old + upd
            return tuple(run0 + cs[0][0:1, :] for run0, cs in zip(run0s, css))

        runs = (jnp.zeros((1, SQ), F32),) * HPS
        for d in reversed(range(nqb)):
            runs = block(i * nqb + d, runs, dmasks[d])
        lax.fori_loop(0, i * nqb, lambda t, c: block(i * nqb - 1 - t, c, None), runs)
        o_ref[...] = acc[...].T
        if ride:
            @pl.when(jnp.logical_and(h == nh - 1, i == nq - 1))
            def _():
                ride.run("wait", ride_in, ride_out, sems)

    return pl.pallas_call(
        body, name="attn_fwd", grid=(nh, nq),
        in_specs=[pl.BlockSpec((SQ, GW), lambda h, i: (i, h)), ANY, ANY] + [ANY] * nr,
        out_specs=[pl.BlockSpec((SQ, GW), lambda h, i: (i, h)), pl.BlockSpec((HPS, nkc, SQ), lambda h, i: (h, 0, i))] + [ANY] * nr,
        out_shape=[jax.ShapeDtypeStruct((S, HEADS * HD), F32), jax.ShapeDtypeStruct((HEADS, nkc, S), F32)]
        + (ride.out_shape if ride else []),
        scratch_shapes=[pltpu.VMEM((S, GW), BF), pltpu.VMEM((GW, S), BF), pltpu.VMEM((GW, SQ), F32)]
        + (_exchange_sems(nr) if ride else []),
    )(m, kb_all, vt_all, *(ride.arrays if ride else []))


def _attn_bwd_t(m, kb_all, vb_all, kt_all, runs, dy, ride=None):
    S = m.shape[0]
    SQ = min(QT, S)
    nkc, nqb, nh, nq = S // KB, SQ // SB, HEADS // HPS, S // SQ
    nr = ride.n if ride else 0

    def body(*refs):
        q_ref, kb_hbm, vb_hbm, kt_hbm, r_ref, dy_ref = refs[:6]
        ride_in, (dq_ref, dk_hbm, dv_hbm), ride_out = refs[6:6 + nr], refs[6 + nr:9 + nr], refs[9 + nr:9 + 2 * nr]
        kb, vb, kt, dqt, dka, dva = refs[9 + 2 * nr:15 + 2 * nr]
        sems = refs[15 + 2 * nr:]
        h, i = pl.program_id(0), pl.program_id(1)
        cols = pl.ds(pl.multiple_of(h * GW, GW), GW)
        if ride:
            @pl.when(jnp.logical_and(h == 0, i == 0))
            def _():
                ride.run("start", ride_in, ride_out, sems)

        @pl.when(i == 0)
        def _():
            pltpu.sync_copy(kb_hbm.at[:, cols], kb)
            pltpu.sync_copy(vb_hbm.at[:, cols], vb)
            pltpu.sync_copy(kt_hbm.at[cols, :], kt)
            dka[...] = jnp.zeros_like(dka)
            dva[...] = jnp.zeros_like(dva)

        q8 = q_ref[...] * 0.125
        dyf = dy_ref[...]
        q8t, dyt = (q8 * LOG2E).T, dyf.T
        groups = [slice(g * KB, (g + 1) * KB) for g in range(HPS // 2)]
        qtm = [t for s in groups for t in _head_halves(q8t[s], 0)]
        dytm = [t for s in groups for t in _head_halves(dyt[s], 0)]
        qlm = [t for s in groups for t in _head_halves(q8[:, s], 1)]
        dylm = [t for s in groups for t in _head_halves(dyf[:, s], 1)]
        dmasks = _diag_masks(SQ)
        ri, ci = lax.broadcasted_iota(I32, (KB, KB), 0), lax.broadcasted_iota(I32, (KB, KB), 1)
        upper, lower = (ci >= ri).astype(BF), (ci <= ri).astype(BF)
        rev2 = jnp.concatenate([upper, upper], axis=1)
        dqt[...] = jnp.zeros_like(dqt)

        def block(jb, pres, dmask):
            masked = dmask is not None
            off = pl.multiple_of(jb * SB, SB)
            heads = range(HPS)
            kblk = [kb[pl.ds(off, SB), s] for s in groups]
            vblk = [vb[pl.ds(off, SB), s] for s in groups]
            ktb = [kt[s, pl.ds(off, SB)] for s in groups]
            run0s = [r_ref[hh, pl.ds(NKC * jb, 1), :] for hh in heads]
            run1s = [r_ref[hh, pl.ds(NKC * jb + 1, 1), :] for hh in heads]
            old_dq, old_dk, old_dv = dqt[...], dka[pl.ds(off, SB), :], dva[pl.ds(off, SB), :]
            dks, dvs, parts, new = [], [], [], []
            for g in range(HPS // 2):
                hs = (2 * g, 2 * g + 1)
                zs = [jnp.dot(kblk[g], qtm[hh], preferred_element_type=F32) for hh in hs]
                das = [jnp.dot(vblk[g], dytm[hh], preferred_element_type=F32) for hh in hs]
                sps = [_softplus2(z) for z in zs]
                sigs = [jnp.exp2(z - sp) for z, sp in zip(zs, sps)]
                if masked:
                    sps = [jnp.where(dmask, sp, 0.0) for sp in sps]
                css = [_chunk_sums(rev2, sp) for sp in sps]
                aa = [jnp.exp2(z - jnp.concatenate([run0s[hh] + cs[0], run1s[hh] + cs[1]], axis=0)) for z, hh, cs in zip(zs, hs, css)]
                if masked:
                    aa = [jnp.where(dmask, a, 0.0) for a in aa]
                gs = [a * da for a, da in zip(aa, das)]
                pgs = [[jnp.dot(lower, gg[c * KB:(c + 1) * KB].astype(BF), preferred_element_type=F32) for c in range(NKC)] for gg in gs]
                pre1s = [pres[hh] + pg[0][KB - 1:KB, :] for hh, pg in zip(hs, pgs)]
                dzs = [gg - sig * jnp.concatenate([pres[hh] + pg[0], pre1 + pg[1]], axis=0)
                       for gg, sig, hh, pre1, pg in zip(gs, sigs, hs, pre1s, pgs)]
                if masked:
                    dzs = [jnp.where(dmask, dz, 0.0) for dz in dzs]
                dzb, ab = [dz.astype(BF) for dz in dzs], [a.astype(BF) for a in aa]
                dks.append(sum(jnp.dot(dzb[t], qlm[hh], preferred_element_type=F32) for t, hh in enumerate(hs)))
                dvs.append(sum(jnp.dot(ab[t], dylm[hh], preferred_element_type=F32) for t, hh in enumerate(hs)))
                parts += [jnp.dot(ktb[g], dzb[t], preferred_element_type=F32)[t * HD:(t + 1) * HD, :] for t in range(2)]
                new += [pre1 + pg[1][KB - 1:KB, :] for pre1, pg in zip(pre1s, pgs)]
            dqt[...] = old_dq + jnp.concatenate(parts, axis=0)
            dka[pl.ds(off, SB), :] = old_dk + jnp.concatenate(dks, axis=1)
            dva[pl.ds(off, SB), :] = old_dv + jnp.concatenate(dvs, axis=1)
            return tuple(new)

        pres = lax.fori_loop(0, i * nqb, lambda jb, c: block(jb, c, None), (jnp.zeros((1, SQ), F32),) * HPS)
        for d in range(nqb):
            pres = block(i * nqb + d, pres, dmasks[d])
        dq_ref[...] = (dqt[...].T * 0.125).astype(dq_ref.dtype)

        @pl.when(i == nq - 1)
        def _():
            pltpu.sync_copy(dka, dk_hbm.at[:, cols])
            pltpu.sync_copy(dva, dv_hbm.at[:, cols])

        if ride:
            @pl.when(jnp.logical_and(h == nh - 1, i == nq - 1))
            def _():
                ride.run("wait", ride_in, ride_out, sems)

    n = HEADS * HD
    return pl.pallas_call(
        body, name="attn_bwd", grid=(nh, nq),
        in_specs=[pl.BlockSpec((SQ, GW), lambda h, i: (i, h)), ANY, ANY, ANY,
                  pl.BlockSpec((HPS, nkc, SQ), lambda h, i: (h, 0, i)), pl.BlockSpec((SQ, GW), lambda h, i: (i, h))] + [ANY] * nr,
        out_specs=[pl.BlockSpec((SQ, GW), lambda h, i: (i, h)), ANY, ANY] + [ANY] * nr,
        out_shape=[jax.ShapeDtypeStruct((S, n), BF), jax.ShapeDtypeStruct((S, n), F32), jax.ShapeDtypeStruct((S, n), F32)]
        + (ride.out_shape if ride else []),
        scratch_shapes=[pltpu.VMEM((S, GW), BF), pltpu.VMEM((S, GW), BF), pltpu.VMEM((GW, S), BF), pltpu.VMEM((GW, SQ), F32),
                        pltpu.VMEM((S, GW), F32), pltpu.VMEM((S, GW), F32)] + (_exchange_sems(nr) if ride else []),
    )(m, kb_all, vb_all, kt_all, runs, dy, *(ride.arrays if ride else []))


CW = 3 * HEADS * HD


def _shift_down(cur, prev8, s):
    if s == 0:
        return cur
    r = pltpu.roll(cur, s, 0)
    first = jnp.where(lax.broadcasted_iota(I32, (8, cur.shape[1]), 0) < s, pltpu.roll(prev8, s, 0), r[:8])
    return jnp.concatenate([first, r[8:]], axis=0)


def _shift_up(cur, next8, s):
    if s == 0:
        return cur
    n = cur.shape[0]
    r = pltpu.roll(cur, n - s, 0)
    last = jnp.where(lax.broadcasted_iota(I32, (8, cur.shape[1]), 0) >= 8 - s, pltpu.roll(next8, 8 - s, 0), r[n - 8:])
    return jnp.concatenate([r[:n - 8], last], axis=0)


def _conv_fwd(m, conv_w):
    S = m.shape[0]
    tm = min(512, S)
    hb = tm // 8

    def body(x_ref, p_ref, w_ref, o_ref):
        i = pl.program_id(0)
        cur = x_ref[...]
        prev = jnp.where(i > 0, p_ref[...], 0.0)
        w = w_ref[...]
        acc = cur * w[3:4]
        for jk in range(3):
            acc = acc + _shift_down(cur, prev, 3 - jk) * w[jk:jk + 1]
        o_ref[...] = acc

    return pl.pallas_call(
        body, name="conv_fwd", grid=(S // tm,),
        in_specs=[pl.BlockSpec((tm, CW), lambda i: (i, 1)), pl.BlockSpec((8, CW), lambda i: (jnp.maximum(i * hb - 1, 0), 1)),
                  pl.BlockSpec((4, CW), lambda i: (0, 0))],
        out_specs=pl.BlockSpec((tm, CW), lambda i: (i, 0)),
        out_shape=jax.ShapeDtypeStruct((S, CW), F32),
    )(m, m, conv_w)


def _conv_bwd(m, dyc, conv_w):
    S = m.shape[0]
    tm = min(512, S)
    hb = tm // 8
    nt = S // tm

    def body(x_ref, p_ref, d_ref, n_ref, w_ref, dx_ref, dw_ref):
        i = pl.program_id(0)
        cur = x_ref[...]
        prev = jnp.where(i > 0, p_ref[...], 0.0)
        d = d_ref[...]
        nxt = jnp.where(i < nt - 1, n_ref[...], 0.0)
        w = w_ref[...]
        acc = d * w[3:4]
        dws = []
        for jk in range(3):
            acc = acc + _shift_up(d, nxt, 3 - jk) * w[jk:jk + 1]
            dws.append(jnp.sum(d * _shift_down(cur, prev, 3 - jk), axis=0, keepdims=True))
        dws.append(jnp.sum(d * cur, axis=0, keepdims=True))
        dx_ref[...] = acc.astype(dx_ref.dtype)
        dw = jnp.concatenate(dws, axis=0)

        @pl.when(i == 0)
        def _():
            dw_ref[...] = dw

        @pl.when(i > 0)
        def _():
            dw_ref[...] += dw

    return pl.pallas_call(
        body, name="conv_bwd", grid=(nt,),
        in_specs=[pl.BlockSpec((tm, CW), lambda i: (i, 1)), pl.BlockSpec((8, CW), lambda i: (jnp.maximum(i * hb - 1, 0), 1)),
                  pl.BlockSpec((tm, CW), lambda i: (i, 0)),
                  pl.BlockSpec((8, CW), lambda i: (jnp.minimum((i + 1) * hb, S // 8 - 1), 0)),
                  pl.BlockSpec((4, CW), lambda i: (0, 0))],
        out_specs=[pl.BlockSpec((tm, CW), lambda i: (i, 0)), pl.BlockSpec((4, CW), lambda i: (0, 0))],
        out_shape=[jax.ShapeDtypeStruct((S, CW), BF), jax.ShapeDtypeStruct((4, CW), F32)],
    )(m, m, dyc, dyc, conv_w)


def _t(a):
    return jnp.swapaxes(a, 1, 2)


def _bdot(a, b):
    return jnp.einsum("hik,hkj->hij", a, b, preferred_element_type=F32)


@jax.custom_vjp
def _mm1(a, b):
    return _bdot(a.astype(BF), b.astype(BF))


def _bdot_nt(a, b):
    return jnp.einsum("hij,hkj->hik", a, b, preferred_element_type=F32)


def _bdot_tn(a, b):
    return jnp.einsum("hki,hkj->hij", a, b, preferred_element_type=F32)


_mm1.defvjp(lambda a, b: (_mm1(a, b), (a.astype(BF), b.astype(BF))),
            lambda res, dc: (_bdot_nt(dc.astype(BF), res[1]), _bdot_tn(res[0], dc.astype(BF))))


@jax.custom_vjp
def _mm1_nt(a, b):
    return _bdot_nt(a.astype(BF), b.astype(BF))


_mm1_nt.defvjp(lambda a, b: (_mm1_nt(a, b), (a.astype(BF), b.astype(BF))),
               lambda res, dc: (_bdot(dc.astype(BF), res[1]), _bdot_tn(dc.astype(BF), res[0])))


@jax.custom_vjp
def _mm1_tn(a, b):
    return _bdot_tn(a.astype(BF), b.astype(BF))


_mm1_tn.defvjp(lambda a, b: (_mm1_tn(a, b), (a.astype(BF), b.astype(BF))),
               lambda res, dc: (_bdot_nt(res[1], dc.astype(BF)), _bdot(res[0], dc.astype(BF))))


def _stack_rows(hi, lo):
    return jnp.concatenate([hi, lo], axis=1)


@jax.custom_vjp
def _mm3(a, b):
    (ah, al), (bh, bl) = _split(a), _split(b)
    n = a.shape[1]
    two = _bdot(_stack_rows(ah, al), bh)
    return two[:, :n] + two[:, n:] + _bdot(ah, bl)


def _mm3_fwd(a, b):
    return _mm3(a, b), (_split(a), _split(b))


def _mm3_bwd(res, dc):
    (ah, al), (bh, bl) = res
    dh, dl = _split(dc)
    n = dc.shape[1]
    two = _bdot_nt(_stack_rows(dh, dl), bh)
    da = two[:, :n] + two[:, n:] + _bdot_nt(dh, bl)
    db = _bdot_tn(jnp.concatenate([ah, ah, al], axis=1), jnp.concatenate([dh, dl, dh], axis=1))
    return da, db


_mm3.defvjp(_mm3_fwd, _mm3_bwd)


def _mm_exact(c3, b):
    hi, lo = _split(b)
    lo2 = (b - hi.astype(F32) - lo.astype(F32)).astype(BF)
    return _bdot(c3, jnp.concatenate([hi, lo, lo2], axis=-2))


@jax.custom_vjp
def _cumsum_rows(b):
    return _mm_exact(_tri3(True), b)


def _tri3(lower):
    ri = lax.broadcasted_iota(I32, (HEADS, CH, CH), 1)
    ci = lax.broadcasted_iota(I32, (HEADS, CH, CH), 2)
    tri = (ri >= ci if lower else ri <= ci).astype(BF)
    return jnp.concatenate([tri, tri, tri], axis=-1)


_cumsum_rows.defvjp(lambda b: (_cumsum_rows(b), None), lambda _, dc: (_mm_exact(_tri3(False), dc),))


CPS = 2


def _gdn_chunk(yc, gz, gba, alog, dtb):
    def heads(t, off):
        return jnp.stack([t[:, off + h * HD: off + (h + 1) * HD] for h in range(HEADS)])

    def cols(t, off):
        return jnp.stack([jnp.broadcast_to(t[:, off + h: off + h + 1], (CH, CH)) for h in range(HEADS)])

    c = yc * _sigmoid(yc)
    q, k, v, zg = heads(c, 0), heads(c, HEADS * HD), heads(c, 2 * HEADS * HD), heads(gz, 0)
    q = q * lax.rsqrt(jnp.sum(q * q, axis=-1, keepdims=True) + RMS_EPS) * (HD ** -0.5)
    k = k * lax.rsqrt(jnp.sum(k * k, axis=-1, keepdims=True) + RMS_EPS)
    beta = cols(_sigmoid(gba), 0)
    g = cols(-jnp.exp(alog) * _softplus(gba + dtb), HEADS)
    ri = lax.broadcasted_iota(I32, (HEADS, CH, CH), 1)
    ci = lax.broadcasted_iota(I32, (HEADS, CH, CH), 2)
    causal, strict = ri >= ci, ri > ci
    eye = (ri == ci).astype(F32)
    gc = _cumsum_rows(g)
    gr = _t(gc)
    decay = jnp.where(causal, jnp.exp(jnp.where(causal, gc - gr, 0.0)), 0.0)
    lower = jnp.where(strict, beta * _mm1_nt(k, k) * decay, 0.0)
    pw = -lower
    inv = eye + pw
    pw = _mm3(pw, pw)
    for _ in range(4):
        both = _mm3(jnp.concatenate([inv, pw], axis=1), pw)
        inv, pw = inv + both[:, :CH], both[:, CH:]
    inv = inv + _mm3(inv, pw)
    eg = jnp.exp(gc)
    uw = _mm3(inv, jnp.concatenate([v * beta, k * (beta * eg)], axis=2))
    u, w = uw[:, :, :HD], uw[:, :, HD:]
    qk = jnp.where(causal, _mm1_nt(q, k) * decay, 0.0)
    g_last = gc[:, CH - 1:CH, :]
    return u, jnp.concatenate([w, q * eg], axis=1), qk, k * jnp.exp(g_last - gc), jnp.exp(g_last), zg * _sigmoid(zg)


def _gdn_advance(state, pre, normw):
    u, wq, qk, kd, last, gate = pre
    ws = _mm1(wq, state)
    v_new = u - ws[:, :CH]
    o = ws[:, CH:] + _mm1(qk, v_new)
    new_state = state * last + _mm1_tn(kd, v_new)
    o = o * lax.rsqrt(jnp.mean(o * o, axis=-1, keepdims=True) + RMS_EPS) * normw * gate
    return jnp.concatenate([o[h] for h in range(HEADS)], axis=1), new_state


def _gdn_chunks(state, yc, gz, gba, alog, dtb, normw):
    rows = [slice(c * CH, (c + 1) * CH) for c in range(yc.shape[0] // CH)]
    pres = [_gdn_chunk(yc[r], gz[r], gba[r], alog, dtb) for r in rows]
    outs = []
    for pre in pres:
        o, state = _gdn_advance(state, pre, normw)
        outs.append(o)
    return jnp.concatenate(outs, axis=0), state


def _gdn_fwd(yc, m, alog, dtb, normw):
    S = yc.shape[0]
    RS = CPS * CH
    nch = S // RS

    def body(y_ref, gz_ref, gba_ref, al_ref, dt_ref, nw_ref, o_ref, st_ref, st):
        @pl.when(pl.program_id(0) == 0)
        def _():
            st[...] = jnp.zeros_like(st)

        cur = st[...]
        st_ref[0] = cur
        o, new = _gdn_chunks(cur, y_ref[...], gz_ref[...], gba_ref[...], al_ref[...], dt_ref[...], nw_ref[...])
        o_ref[...] = o
        st[...] = new

    return pl.pallas_call(
        body, name="gdn_fwd", grid=(nch,),
        in_specs=[pl.BlockSpec((RS, CW), lambda n: (n, 0)), pl.BlockSpec((RS, HEADS * HD), lambda n: (n, 6)),
                  pl.BlockSpec((RS, 128), lambda n: (n, 28)), pl.BlockSpec((1, 128), lambda n: (0, 0)),
                  pl.BlockSpec((1, 128), lambda n: (0, 0)), pl.BlockSpec((1, HD), lambda n: (0, 0))],
        out_specs=[pl.BlockSpec((RS, HEADS * HD), lambda n: (n, 0)), pl.BlockSpec((1, HEADS, HD, HD), lambda n: (n, 0, 0, 0))],
        out_shape=[jax.ShapeDtypeStruct((S, HEADS * HD), F32), jax.ShapeDtypeStruct((nch, HEADS, HD, HD), F32)],
        scratch_shapes=[pltpu.VMEM((HEADS, HD, HD), F32)],
    )(yc, m, m, alog, dtb, normw)


def _gdn_bwd(yc, m, alog, dtb, normw, states, dog):
    S = yc.shape[0]
    RS = CPS * CH
    nch = S // RS

    def body(y_ref, gz_ref, gba_ref, al_ref, dt_ref, nw_ref, st_ref, do_ref, dy_ref, dgz_ref, dgba_ref, dal_ref, ddt_ref, dnw_ref, dst):
        n = pl.program_id(0)

        @pl.when(n == 0)
        def _():
            dst[...] = jnp.zeros_like(dst)

        _, vjp = jax.vjp(_gdn_chunks, st_ref[0], y_ref[...], gz_ref[...], gba_ref[...], al_ref[...], dt_ref[...], nw_ref[...])
        d_state, d_y, d_gz, d_gba, d_al, d_dt, d_nw = vjp((do_ref[...], dst[...]))
        dst[...] = d_state
        dy_ref[...] = d_y
        dgz_ref[...] = d_gz.astype(dgz_ref.dtype)
        dgba_ref[...] = d_gba.astype(dgba_ref.dtype)
        for ref, val in ((dal_ref, d_al), (ddt_ref, d_dt), (dnw_ref, d_nw)):
            @pl.when(n == 0)
            def _(ref=ref, val=val):
                ref[...] = val

            @pl.when(n > 0)
            def _(ref=ref, val=val):
                ref[...] += val

    rev = lambda n: nch - 1 - n
    return pl.pallas_call(
        body, name="gdn_bwd", grid=(nch,),
        in_specs=[pl.BlockSpec((RS, CW), lambda n: (rev(n), 0)), pl.BlockSpec((RS, HEADS * HD), lambda n: (rev(n), 6)),
                  pl.BlockSpec((RS, 128), lambda n: (rev(n), 28)), pl.BlockSpec((1, 128), lambda n: (0, 0)),
                  pl.BlockSpec((1, 128), lambda n: (0, 0)), pl.BlockSpec((1, HD), lambda n: (0, 0)),
                  pl.BlockSpec((1, HEADS, HD, HD), lambda n: (rev(n), 0, 0, 0)),
                  pl.BlockSpec((RS, HEADS * HD), lambda n: (rev(n), 0))],
        out_specs=[pl.BlockSpec((RS, CW), lambda n: (rev(n), 0)), pl.BlockSpec((RS, HEADS * HD), lambda n: (rev(n), 0)),
                   pl.BlockSpec((RS, 128), lambda n: (rev(n), 0)), pl.BlockSpec((1, 128), lambda n: (0, 0)),
                   pl.BlockSpec((1, 128), lambda n: (0, 0)), pl.BlockSpec((1, HD), lambda n: (0, 0))],
        out_shape=[jax.ShapeDtypeStruct((S, CW), F32), jax.ShapeDtypeStruct((S, HEADS * HD), BF),
                   jax.ShapeDtypeStruct((S, 128), BF), jax.ShapeDtypeStruct((1, 128), F32),
                   jax.ShapeDtypeStruct((1, 128), F32), jax.ShapeDtypeStruct((1, HD), F32)],
        scratch_shapes=[pltpu.VMEM((HEADS, HD, HD), F32)],
    )(yc, m, m, alog, dtb, normw, states, dog)


def _mixer_fwd(h1, W, late=None):
    S = h1.shape[0]
    tm = min(512, S)
    n = HEADS * HD
    tb_ = min(BIG_ROWS, S)
    m, = _mm("mix_in", h1, W["wp"], S, NP, D, tm=tb_, tn=1536, tk=D, order="ji", outs=[((S, NP), F32, (tb_, 1536), _tile)])
    kb_all, vb_all, kt_all, vt_all = _attn_prep(m)
    ya, runs, *arrived = _attn_fwd_t(m, kb_all, vt_all, ride=late[0] if late else None)
    if late:
        late[1](W, arrived)
    runs = (kb_all, vb_all, kt_all, runs)
    b_gate, conv_w, alog, dtb, normw, w_sb, w_gdn, w_mo, g, b = (
        W[k] for k in ("b_gate", "conv_w", "alog", "dtb", "normw", "w_sb", "w_gdn", "w_mo", "ln2_g", "ln2_b"))
    yc = _conv_fwd(m, conv_w)
    og, states = _gdn_fwd(yc, m, alog, dtb, normw)
    ysb, = _mm("mix_sb", ya, w_sb, S, D, n, tm=tm, tn=D, tk=n, outs=[((S, D), F32, (tm, D), _row)])

    def merge_epi(acc, ys, gs, gg, bg):
        return _sigmoid(gs + bg[:, :D]) * ys + _sigmoid(gg + bg[:, D:]) * acc, acc

    u, ygdn = _mm("mix_gdn", og, w_gdn, S, D, n, tm=tm, tn=D, tk=n,
                  extras=[(ysb, (tm, D), _row), (m, (tm, D), lambda i, j, k: (i, GATE0 // D)),
                          (m, (tm, D), lambda i, j, k: (i, GATE0 // D + 1)), (b_gate, (1, 2 * D), _const)],
                  epilogue=merge_epi, outs=[((S, D), BF, (tm, D), _row), ((S, D), F32, (tm, D), _row)])

    def out_epi(acc, xin, gg, bb):
        r = ALPHA * xin + acc
        return _layer_norm(r, gg, bb), r

    h2, r2 = _mm("mix_out", u, w_mo, S, D, D, tm=tm, tn=D, tk=D,
                 extras=[(h1, (tm, D), _row), (g, (1, D), _const), (b, (1, D), _const)], epilogue=out_epi,
                 outs=[((S, D), F32, (tm, D), _row), ((S, D), F32, (tm, D), _row)])
    return h2, (h1, m, ya, runs, yc, og, states, ysb, ygdn, u, r2)


def _mixer_bwd(saved, W, dh, ride=None):
    h1, m, ya, runs, yc, og, states, ysb, ygdn, u, r2 = saved
    wp, b_gate, conv_w, alog, dtb, normw, w_sb, w_gdn, w_mo, g = (
        W[k] for k in ("wp", "b_gate", "conv_w", "alog", "dtb", "normw", "w_sb", "w_gdn", "w_mo", "ln2_g"))
    S = h1.shape[0]
    tm = min(512, S)
    n = HEADS * HD
    dr, dg, db = _ln_bwd("mix", r2, g, dh)

    def merge_epi(du, ys, yg, gs, gg, bg):
        s1, s2 = _sigmoid(gs + bg[:, :D]), _sigmoid(gg + bg[:, D:])
        dgate = jnp.concatenate([du * ys * s1 * (1.0 - s1), du * yg * s2 * (1.0 - s2)], axis=1)
        return du * s1, du * s2, dgate, jnp.sum(dgate, axis=0, keepdims=True)

    dysb, dygdn, dgate, d_bg = _mm(
        "mix_dmerge", dr, w_mo, S, D, D, tm=tm, tn=D, tk=D, tb=True,
        extras=[(ysb, (tm, D), _row), (ygdn, (tm, D), _row), (m, (tm, D), lambda i, j, k: (i, GATE0 // D)),
                (m, (tm, D), lambda i, j, k: (i, GATE0 // D + 1)), (b_gate, (1, 2 * D), _const)],
        epilogue=merge_epi, n_acc=1,
        outs=[((S, D), BF, (tm, D), _row), ((S, D), BF, (tm, D), _row), ((S, 2 * D), BF, (tm, 2 * D), _row),
              ((1, 2 * D), F32, (1, 2 * D), _const)])
    d_w_mo, = _mm("mix_dwmo", u, dr, D, D, S, tm=D, tn=D, tk=min(BIG_ROWS, S), ta=True, outs=[((D, D), F32, (D, D), _tile)])
    dya, = _mm("mix_dya", dysb, w_sb, S, n, D, tm=tm, tn=n, tk=D, tb=True, outs=[((S, n), F32, (tm, n), _row)])
    col_shards = [((NSH, n, D // NSH), F32, (None, n, D // NSH), lambda i, j, k: (j, 0, 0))]
    d_w_sb, = _mm("mix_dwsb", ya, dysb, n, D, S, tm=n, tn=D // NSH, tk=min(BIG_ROWS, S), ta=True, order="ji", outs=col_shards)
    dog, = _mm("mix_dog", dygdn, w_gdn, S, n, D, tm=tm, tn=n, tk=D, tb=True, outs=[((S, n), F32, (tm, n), _row)])
    d_w_gdn, = _mm("mix_dwgdn", og, dygdn, n, D, S, tm=n, tn=D // NSH, tk=min(BIG_ROWS, S), ta=True, order="ji", outs=col_shards)
    dq, dk, dv, *arrived = _attn_bwd_t(m, *runs, dya, ride=ride)
    dyc, dgz, dgba, d_alog, d_dtb, d_normw = _gdn_bwd(yc, m, alog, dtb, normw, states, dog)
    dxc, d_conv = _conv_bwd(m, dyc, conv_w)
    dm = jnp.concatenate([dq, dk.astype(BF), dv.astype(BF), dxc, dgz, dgba, jnp.zeros((S, GATE0 - 3712), BF), dgate], axis=1)
    tb_ = min(BIG_ROWS, S)
    d_h1, = _mm("mix_dh", dm, wp, S, D, NP, tm=tb_, tn=D, tk=1536, tb=True,
                extras=[(dr, (tb_, D), _row)], epilogue=lambda acc, d: (acc + ALPHA * d,),
                outs=[((S, D), F32, (tb_, D), _row)])
    d_wp, = _mm("mix_dwp", h1, dm, D, NP, S, tm=D, tn=1536, tk=tb_, ta=True, order="ji",
                outs=[((D, NP), F32, (D, 1536), _tile)])
    return d_h1, dict(wp=d_wp, b_gate=d_bg, conv_w=d_conv, alog=d_alog, dtb=d_dtb, normw=d_normw,
                      w_sb=d_w_sb, w_gdn=d_w_gdn, w_mo=d_w_mo, g=dg, b=db), arrived


def _ple_fwd(h3, p, w_pg, b_pg, w_ple, g, b, target):
    S = h3.shape[0]
    tm = min(512, S)
    pd = p.shape[1]
    pe, = _mm("ple_emb", p, w_ple, S, D, pd, tm=tm, tn=D, tk=pd, outs=[((S, D), F32, (tm, D), _row)])

    def epi(acc, e, xin, tgt, bp, gg, bb):
        gt = _sigmoid(acc + bp)
        r = ALPHA * xin + gt * e
        diff = _layer_norm(r, gg, bb) - tgt
        return gt, r, diff * (1.0 / D), jnp.sum(diff * diff, axis=0, keepdims=True)

    gt, r4, dh4, loss_row = _mm(
        "ple_gate", h3, w_pg, S, D, D, tm=tm, tn=D, tk=D,
        extras=[(pe, (tm, D), _row), (h3, (tm, D), _row), (target, (tm, D), _row), (b_pg, (1, D), _const),
                (g, (1, D), _const), (b, (1, D), _const)], epilogue=epi, n_acc=1,
        outs=[((S, D), F32, (tm, D), _row), ((S, D), F32, (tm, D), _row), ((S, D), F32, (tm, D), _row),
              ((1, D), F32, (1, D), _const)])
    return dh4, loss_row, (h3, p, pe, gt, r4)


def _ple_bwd(saved, w_pg, g, dh4):
    h3, p, pe, gt, r4 = saved
    S = h3.shape[0]
    tm = min(512, S)
    pd = p.shape[1]

    def fn(r, gg, dh, e, t):
        dr, dg, db = _layer_norm_bwd(r, gg, dh)
        dpre = dr * e * t * (1.0 - t)
        return dr, dpre, dr * t, dg, db, jnp.sum(dpre, axis=0, keepdims=True)

    row, one = (lambda i: (i, 0)), (lambda i: (0, 0))
    dr, dpre, dpe, dg, db, d_bpg = _rows(
        "ple_lnbwd", fn, S // tm,
        [(r4, (tm, D), row), (g, (1, D), one), (dh4, (tm, D), row), (pe, (tm, D), row), (gt, (tm, D), row)],
        [((S, D), F32, (tm, D), row), ((S, D), BF, (tm, D), row), ((S, D), BF, (tm, D), row),
         ((1, D), F32, (1, D), one), ((1, D), F32, (1, D), one), ((1, D), F32, (1, D), one)], n_acc=3)
    d_w_pg, = _mm("ple_dwpg", h3, dpre, D, D, S, tm=D, tn=D, tk=min(BIG_ROWS, S), ta=True, outs=[((D, D), F32, (D, D), _tile)])
    d_w_ple, = _mm("ple_dwple", p, dpe, pd, D, S, tm=pd, tn=D // NSH, tk=min(BIG_ROWS, S), ta=True, order="ji",
                   outs=[((NSH, pd, D // NSH), F32, (None, pd, D // NSH), lambda i, j, k: (j, 0, 0))])
    d_h3, = _mm("ple_dh", dpre, w_pg, S, D, D, tm=tm, tn=D, tk=D, tb=True,
                extras=[(dr, (tm, D), _row)], epilogue=lambda acc, d: (acc + ALPHA * d,),
                outs=[((S, D), F32, (tm, D), _row)])
    return d_h3, d_w_pg, d_bpg, d_w_ple, dg, db


def _local_step(x, p, target, W, soon=None, late=None, early=None, mid=None):
    W = dict(W)
    h1, sv1, arrived = _ffn_fwd("ffn1", x, W["ffn1_in"], W["ffn1_out"], W["ln1_g"], W["ln1_b"], ride=soon[0] if soon else None)
    if soon:
        soon[1](W, arrived)
    h2, sv2 = _mixer_fwd(h1, W, late)
    h3, sv3, _ = _ffn_fwd("ffn2", h2, W["ffn2_in"], W["ffn2_out"], W["ln3_g"], W["ln3_b"])
    dh4, loss_row, sv4 = _ple_fwd(h3, p, W["w_pg"], W["b_pg"], W["w_ple"], W["ln4_g"], W["ln4_b"], target)
    G = {}
    dh3, G["w_pg"], G["b_pg"], G["w_ple"], G["ln4_g"], G["ln4_b"] = _ple_bwd(sv4, W["w_pg"], W["ln4_g"], dh4)
    dh2, G["ffn2_in"], G["ffn2_out"], G["ln3_g"], G["ln3_b"], _ = _ffn_bwd("ffn2", sv3, W["ffn2_in"], W["ffn2_out"], W["ln3_g"], dh3)
    dh1, gm, G["early"] = _mixer_bwd(sv2, W, dh2, ride=early(G) if early else None)
    G.update({k: v for k, v in gm.items() if k not in ("g", "b")})
    G["ln2_g"], G["ln2_b"] = gm["g"], gm["b"]
    dx, G["ffn1_in"], G["ffn1_out"], G["ln1_g"], G["ln1_b"], G["mid"] = _ffn_bwd(
        "ffn1", sv1, W["ffn1_in"], W["ffn1_out"], W["ln1_g"], dh1, ride=mid(G) if mid else None)
    return loss_row, dx, G


S2 = CUT - 2 * MSH


def _pack_wp(w4):
    tr = 256

    def fn(w):
        s = [w[j].astype(F32) for j in range(NSH)]
        full = jnp.concatenate([s[0], s[1], s[2][:, :S2], jnp.zeros((tr, GATE0 - CUT), F32), s[2][:, S2:], s[3]], axis=1)
        return (full,)

    return _rows("pack_wp", fn, D // tr, [(w4, (NSH, tr, MSH), lambda i: (0, i, 0))],
                 [((D, NP), BF, (tr, NP), lambda i: (i, 0))])[0]


def _unpack_wp(d):
    tr = 256
    g2 = GATE0 + MSH - S2

    def fn(v):
        return (jnp.stack([v[:, :MSH], v[:, MSH:2 * MSH], jnp.concatenate([v[:, 2 * MSH:CUT], v[:, GATE0:g2]], axis=1), v[:, g2:]]),)

    return _rows("unpack_wp", fn, D // tr, [(d, (tr, NP), lambda i: (i, 0))],
                 [((NSH, D, MSH), F32, (NSH, tr, MSH), lambda i: (0, i, 0))])[0]


def _cast_bf16(tag, w):
    r, c = w.shape
    tr = _pick(r, 256)
    return _rows(f"cast_{tag}", lambda v: (v,), r // tr, [(w, (tr, c), lambda i: (i, 0))],
                 [((r, c), BF, (tr, c), lambda i: (i, 0))])[0]


ANY = pl.BlockSpec(memory_space=pl.ANY)


def _place():
    return lax.axis_index("x"), lax.axis_index("y"), lax.axis_index("c")


def _chip_exchange(phase, scatter, ins, outs, send, recv, loc):
    x, y, c = _place()
    me = 2 * x + y
    chips = [(1 - x, y), (x, 1 - y), (1 - x, 1 - y)]
    for t in range(len(ins)):
        own = pltpu.make_async_copy(ins[t].at[me] if scatter else ins[t], outs[t].at[me], loc.at[t])
        out_going, in_coming = [], []
        for q, (px, py) in enumerate(chips):
            src = ins[t].at[2 * px + py] if scatter else ins[t]
            sems = dict(send_sem=send.at[3 * t + q], recv_sem=recv.at[3 * t + q], device_id=(px, py, c), device_id_type=MESH)
            out_going.append(pltpu.make_async_remote_copy(src_ref=src, dst_ref=outs[t].at[me], **sems))
            in_coming.append(pltpu.make_async_remote_copy(src_ref=src, dst_ref=outs[t].at[2 * px + py], **sems))
        if phase == "start":
            own.start()
            for cp in out_going:
                cp.start()
        else:
            for cp in in_coming:
                cp.wait_recv()
            own.wait()
            for cp in out_going:
                cp.wait_send()


def _exchange_sems(n):
    return [pltpu.SemaphoreType.DMA((3 * n,)), pltpu.SemaphoreType.DMA((3 * n,)), pltpu.SemaphoreType.DMA((n,))]


class _Ride:
    def __init__(self, scatter, arrays):
        self.scatter, self.arrays, self.n = scatter, list(arrays), len(arrays)
        self.out_shape = [jax.ShapeDtypeStruct(a.shape if scatter else (NSH,) + a.shape, a.dtype) for a in self.arrays]

    def run(self, phase, refs_in, refs_out, sems):
        _chip_exchange(phase, self.scatter, refs_in, refs_out, *sems)


def _gather_shards(shards):
    n = len(shards)

    def body(*refs):
        _chip_exchange("start", False, refs[:n], refs[n:2 * n], *refs[2 * n:])
        _chip_exchange("wait", False, refs[:n], refs[n:2 * n], *refs[2 * n:])

    return pl.pallas_call(
        body, name="gather_weights", in_specs=[ANY] * n, out_specs=[ANY] * n,
        out_shape=[jax.ShapeDtypeStruct((NSH,) + s.shape, s.dtype) for s in shards], scratch_shapes=_exchange_sems(n),
    )(*shards)


def _gather_shards_v1(shards):
    n = len(shards)

    def body(*refs):
        ins, outs = refs[:n], refs[n:2 * n]
        send, recv, loc = refs[2 * n:]
        x, y, c = _place()
        me = 2 * x + y
        chips = [(1 - x, y), (x, 1 - y), (1 - x, 1 - y)]
        started = []
        for t in range(n):
            lc = pltpu.make_async_copy(ins[t], outs[t].at[me], loc.at[t])
            lc.start()
            started.append(lc)
            for q, (px, py) in enumerate(chips):
                cp = pltpu.make_async_remote_copy(src_ref=ins[t], dst_ref=outs[t].at[me], send_sem=send.at[3 * t + q],
                                                  recv_sem=recv.at[3 * t + q], device_id=(px, py, c), device_id_type=MESH)
                cp.start()
                started.append(cp)
        for t in range(n):
            for q, (px, py) in enumerate(chips):
                pltpu.make_async_remote_copy(src_ref=ins[t], dst_ref=outs[t].at[2 * px + py], send_sem=send.at[3 * t + q],
                                             recv_sem=recv.at[3 * t + q], device_id=(px, py, c), device_id_type=MESH).wait_recv()
        for t in range(n):
            started[4 * t].wait()
            for q in range(3):
                started[4 * t + 1 + q].wait_send()

    return pl.pallas_call(
        body, name="gather_weights", in_specs=[ANY] * n, out_specs=[ANY] * n,
        out_shape=[jax.ShapeDtypeStruct((NSH,) + s.shape, s.dtype) for s in shards],
        scratch_shapes=[pltpu.SemaphoreType.DMA((3 * n,)), pltpu.SemaphoreType.DMA((3 * n,)), pltpu.SemaphoreType.DMA((n,))],
    )(*shards)


def _reduce_sibling(gs):
    n = len(gs)

    def body(*refs):
        ins, mine, theirs = refs[:n], refs[n:2 * n], refs[2 * n:3 * n]
        send, recv, loc = refs[3 * n:]
        x, y, c = _place()
        started = []
        for t in range(n):
            h = gs[t].shape[1] // 2
            lc = pltpu.make_async_copy(ins[t].at[:, pl.ds(pl.multiple_of(c * h, 8), h), :], mine[t], loc.at[t])
            cp = pltpu.make_async_remote_copy(src_ref=ins[t].at[:, pl.ds(pl.multiple_of((1 - c) * h, 8), h), :], dst_ref=theirs[t],
                                              send_sem=send.at[t], recv_sem=recv.at[t], device_id=(x, y, 1 - c), device_id_type=MESH)
            lc.start()
            cp.start()
            started += [lc, cp]
        for t in range(n):
            started[2 * t].wait()
            started[2 * t + 1].wait()

    half = [jax.ShapeDtypeStruct((NSH, g.shape[1] // 2, g.shape[2]), g.dtype) for g in gs]
    res = pl.pallas_call(
        body, name="reduce_sibling", in_specs=[ANY] * n, out_specs=[ANY] * (2 * n), out_shape=half + half,
        scratch_shapes=[pltpu.SemaphoreType.DMA((n,)), pltpu.SemaphoreType.DMA((n,)), pltpu.SemaphoreType.DMA((n,))],
    )(*gs)
    return res[:n], res[n:]


def _reduce_chips(ps):
    n = len(ps)

    def body(*refs):
        _chip_exchange("start", True, refs[:n], refs[n:2 * n], *refs[2 * n:])
        _chip_exchange("wait", True, refs[:n], refs[n:2 * n], *refs[2 * n:])

    return pl.pallas_call(
        body, name="reduce_chips", in_specs=[ANY] * n, out_specs=[ANY] * n,
        out_shape=[jax.ShapeDtypeStruct(p_.shape, p_.dtype) for p_ in ps], scratch_shapes=_exchange_sems(n),
    )(*ps)


def _reduce_chips_v1(ps):
    n = len(ps)

    def body(*refs):
        ins, outs = refs[:n], refs[n:2 * n]
        send, recv, loc = refs[2 * n:]
        x, y, c = _place()
        me = 2 * x + y
        chips = [(1 - x, y), (x, 1 - y), (1 - x, 1 - y)]
        started = []
        for t in range(n):
            lc = pltpu.make_async_copy(ins[t].at[me], outs[t].at[me], loc.at[t])
            lc.start()
            started.append(lc)
            for q, (px, py) in enumerate(chips):
                cp = pltpu.make_async_remote_copy(src_ref=ins[t].at[2 * px + py], dst_ref=outs[t].at[me], send_sem=send.at[3 * t + q],
                                                  recv_sem=recv.at[3 * t + q], device_id=(px, py, c), device_id_type=MESH)
                cp.start()
                started.append(cp)
        for t in range(n):
            for q, (px, py) in enumerate(chips):
                pltpu.make_async_remote_copy(src_ref=ins[t].at[me], dst_ref=outs[t].at[2 * px + py], send_sem=send.at[3 * t + q],
                                             recv_sem=recv.at[3 * t + q], device_id=(px, py, c), device_id_type=MESH).wait_recv()
        for t in range(n):
            started[4 * t].wait()
            for q in range(3):
                started[4 * t + 1 + q].wait_send()

    return pl.pallas_call(
        body, name="reduce_chips", in_specs=[ANY] * n, out_specs=[ANY] * n,
        out_shape=[jax.ShapeDtypeStruct(p_.shape, p_.dtype) for p_ in ps],
        scratch_shapes=[pltpu.SemaphoreType.DMA((3 * n,)), pltpu.SemaphoreType.DMA((3 * n,)), pltpu.SemaphoreType.DMA((n,))],
    )(*ps)


def _share_sibling(fs):
    n = len(fs)

    def body(*refs):
        ins, outs = refs[:n], refs[n:2 * n]
        send, recv, loc = refs[2 * n:]
        x, y, c = _place()
        started = []
        for t in range(n):
            lc = pltpu.make_async_copy(ins[t], outs[t].at[c], loc.at[t])
            cp = pltpu.make_async_remote_copy(src_ref=ins[t], dst_ref=outs[t].at[c], send_sem=send.at[t], recv_sem=recv.at[t],
                                              device_id=(x, y, 1 - c), device_id_type=MESH)
            lc.start()
            cp.start()
            started += [lc, cp]
        for t in range(n):
            started[2 * t].wait()
            started[2 * t + 1].wait_send()
            pltpu.make_async_remote_copy(src_ref=ins[t], dst_ref=outs[t].at[1 - c], send_sem=send.at[t], recv_sem=recv.at[t],
                                         device_id=(x, y, 1 - c), device_id_type=MESH).wait_recv()

    return pl.pallas_call(
        body, name="share_sibling", in_specs=[ANY] * n, out_specs=[ANY] * n,
        out_shape=[jax.ShapeDtypeStruct((2,) + f.shape, f.dtype) for f in fs],
        scratch_shapes=[pltpu.SemaphoreType.DMA((n,)), pltpu.SemaphoreType.DMA((n,)), pltpu.SemaphoreType.DMA((n,))],
    )(*fs)


def _chunk_rows(h, c):
    return _pick(h, max(8, 262144 // c // 8 * 8))


def _sibling_sum(tag, g):
    _, r, c = g.shape
    h = r // 2
    tr = _chunk_rows(h, c)
    nch = h // tr
    steps = NSH * nch

    def body(top_ref, bot_ref, out_ref, narrow_ref, land, send, recv):
        s = pl.program_id(0)
        x, y, core = _place()

        def exchange(keep_ref, give_ref):
            cp = pltpu.make_async_remote_copy(src_ref=give_ref.at[0], dst_ref=land.at[s], send_sem=send.at[s], recv_sem=recv.at[s],
                                              device_id=(x, y, 1 - core), device_id_type=MESH)
            cp.start()
            cp.wait()
            total = keep_ref[0] + land[s]
            out_ref[0] = total
            narrow_ref[0] = total.astype(BF)

        @pl.when(core == 0)
        def _():
            exchange(top_ref, bot_ref)

        @pl.when(core == 1)
        def _():
            exchange(bot_ref, top_ref)

    return pl.pallas_call(
        body, name=f"sibling_sum_{tag}", grid=(steps,),
        in_specs=[pl.BlockSpec((1, tr, c), lambda s: (s // nch, s % nch, 0)),
                  pl.BlockSpec((1, tr, c), lambda s: (s // nch, nch + s % nch, 0))],
        out_specs=[pl.BlockSpec((1, tr, c), lambda s: (s // nch, s % nch, 0))] * 2,
        out_shape=[jax.ShapeDtypeStruct((NSH, h, c), F32), jax.ShapeDtypeStruct((NSH, h, c), BF)],
        scratch_shapes=[pltpu.VMEM((steps, tr, c), F32), pltpu.SemaphoreType.DMA((steps,)), pltpu.SemaphoreType.DMA((steps,))],
    )(g, g)


def _chip_sum_share(tag, b, own):
    _, h, c = b.shape
    tr = _chunk_rows(h, c)
    steps = h // tr

    def body(b_ref, own_ref, out_ref, stage, land, send, recv):
        s = pl.program_id(0)
        x, y, core = _place()
        me = 2 * x + y
        v = [jnp.where(me == j, own_ref[j], b_ref[j].astype(F32)) for j in range(NSH)]
        total = ((v[0] + v[1]) + v[2]) + v[3]
        stage[...] = total
        cp = pltpu.make_async_remote_copy(src_ref=stage, dst_ref=land.at[s], send_sem=send.at[s], recv_sem=recv.at[s],
                                          device_id=(x, y, 1 - core), device_id_type=MESH)
        cp.start()
        cp.wait()
        out_ref[core] = total
        out_ref[1 - core] = land[s]

    return pl.pallas_call(
        body, name=f"chip_sum_share_{tag}", grid=(steps,),
        in_specs=[pl.BlockSpec((NSH, tr, c), lambda s: (0, s, 0))] * 2,
        out_specs=pl.BlockSpec((2, tr, c), lambda s: (0, s, 0)),
        out_shape=jax.ShapeDtypeStruct((2, h, c), F32),
        scratch_shapes=[pltpu.VMEM((tr, c), F32), pltpu.VMEM((steps, tr, c), F32), pltpu.SemaphoreType.DMA((steps,)),
                        pltpu.SemaphoreType.DMA((steps,))],
    )(b, own)


NDEV = 8


def _allreduce_small(pack):
    r, w = pack.shape
    rel = [(dx, dy, dc) for dx in (0, 1) for dy in (0, 1) for dc in (0, 1) if (dx, dy, dc) != (0, 0, 0)]

    def body(in_ref, out_ref, buf, send, recv):
        x, y, c = _place()
        me = 4 * x + 2 * y + c
        buf[me] = in_ref[...]
        peers = [((x + dx) % 2, (y + dy) % 2, (c + dc) % 2) for dx, dy, dc in rel]
        sent = []
        for k, peer in enumerate(peers):
            cp = pltpu.make_async_remote_copy(src_ref=in_ref, dst_ref=buf.at[me], send_sem=send.at[k], recv_sem=recv.at[k],
                                              device_id=peer, device_id_type=MESH)
            cp.start()
            sent.append(cp)
        for k, (px, py, pc) in enumerate(peers):
            pltpu.make_async_remote_copy(src_ref=in_ref, dst_ref=buf.at[4 * px + 2 * py + pc], send_sem=send.at[k], recv_sem=recv.at[k],
                                         device_id=(px, py, pc), device_id_type=MESH).wait_recv()
        for cp in sent:
            cp.wait_send()
        acc = buf[0]
        for k in range(1, NDEV):
            acc = acc + buf[k]
        out_ref[...] = acc

    vm = pl.BlockSpec(memory_space=pltpu.VMEM)
    return pl.pallas_call(
        body, name="allreduce_small", in_specs=[vm], out_specs=vm, out_shape=jax.ShapeDtypeStruct((r, w), F32),
        scratch_shapes=[pltpu.VMEM((NDEV, r, w), F32), pltpu.SemaphoreType.DMA((NDEV - 1,)), pltpu.SemaphoreType.DMA((NDEV - 1,))],
    )(pack)


def _add2(tag, a, b):
    _, h, c = a.shape
    tr = _pick(h, max(8, 262144 // c // 8 * 8))
    spec = ((None, tr, c), lambda i: (i // (h // tr), i % (h // tr), 0))
    return _rows(f"add2_{tag}", lambda u, v: (u + v,), NSH * (h // tr), [(a,) + spec, (b,) + spec], [(a.shape, F32) + spec])[0]


def _add4(tag, a):
    _, h, c = a.shape
    tr = _pick(h, max(8, 131072 // c // 8 * 8))
    return _rows(f"add4_{tag}", lambda v: (((v[0] + v[1]) + v[2]) + v[3],), h // tr, [(a, (NSH, tr, c), lambda i: (0, i, 0))],
                 [((h, c), F32, (tr, c), lambda i: (i, 0))])[0]


def _adamw(tag, w, g, m, v):
    r, c = w.shape
    tr = _pick(r, max(8, 262144 // c // 8 * 8))

    def fn(w_, g_, m_, v_):
        m2 = B1 * m_ + (1.0 - B1) * g_
        v2 = B2 * v_ + (1.0 - B2) * (g_ * g_)
        m_hat = m2 / (1.0 - B1 ** STEP)
        v_hat = v2 / (1.0 - B2 ** STEP)
        return -LR * (m_hat / (jnp.sqrt(v_hat) + EPS) + WD * w_), m2, v2

    spec = ((tr, c), lambda i: (i, 0))
    return _rows(f"adamw_{tag}", fn, r // tr, [(a,) + spec for a in (w, g, m, v)], [((r, c), F32) + spec] * 3)


BIG = ("ffn1_w_in", "ffn1_w_out", "w_mix_in", "w_branch_sb", "w_branch_gdn", "w_mix_out", "ffn2_w_in", "ffn2_w_out",
       "w_ple_gate", "w_ple")
FIRST = ("ffn1_w_in", "ffn1_w_out")
SOON = ("w_mix_in",)
LATER = tuple(n for n in BIG if n not in FIRST + SOON)
EARLY = ("ffn2_w_in", "ffn2_w_out", "w_ple_gate", "w_ple")
MID = ("w_mix_in", "w_branch_sb", "w_branch_gdn", "w_mix_out")
LAST = ("ffn1_w_in", "ffn1_w_out")
SMALL = ("ln1_g", "ln1_b", "b_gate", "conv_w", "a_log", "dt_bias", "gdn_norm_w", "ln2_g", "ln2_b", "ln3_g", "ln3_b",
         "b_ple_gate", "ln4_g", "ln4_b")
ORDER = ("ffn1_w_in", "ffn1_w_out", "ln1_g", "ln1_b", "w_mix_in", "b_gate", "conv_w", "a_log", "dt_bias", "gdn_norm_w",
         "w_branch_sb", "w_branch_gdn", "w_mix_out", "ln2_g", "ln2_b", "ffn2_w_in", "ffn2_w_out", "ln3_g", "ln3_b",
         "w_ple_gate", "b_ple_gate", "w_ple", "ln4_g", "ln4_b")
PACK_W = 2304


def _lane_row(v, lanes=128, at=HEADS):
    return jnp.pad(v[None, :], ((0, 0), (at, lanes - at - v.shape[0])))


def _col_join(w4):
    return jnp.transpose(w4, (1, 0, 2)).reshape(w4.shape[1], NSH * w4.shape[2])


def kernel(x, p, ffn1_w_in, ffn1_w_out, ln1_g, ln1_b, w_mix_in, b_gate, conv_w, a_log, dt_bias, gdn_norm_w, w_branch_sb, w_branch_gdn, w_mix_out, ln2_g, ln2_b, ffn2_w_in, ffn2_w_out, ln3_g, ln3_b, w_ple_gate, b_ple_gate, w_ple, ln4_g, ln4_b, loss_target, m_ffn1_w_in, m_ffn1_w_out, m_ln1_g, m_ln1_b, m_w_mix_in, m_b_gate, m_conv_w, m_a_log, m_dt_bias, m_gdn_norm_w, m_w_branch_sb, m_w_branch_gdn, m_w_mix_out, m_ln2_g, m_ln2_b, m_ffn2_w_in, m_ffn2_w_out, m_ln3_g, m_ln3_b, m_w_ple_gate, m_b_ple_gate, m_w_ple, m_ln4_g, m_ln4_b, v_ffn1_w_in, v_ffn1_w_out, v_ln1_g, v_ln1_b, v_w_mix_in, v_b_gate, v_conv_w, v_a_log, v_dt_bias, v_gdn_norm_w, v_w_branch_sb, v_w_branch_gdn, v_w_mix_out, v_ln2_g, v_ln2_b, v_ffn2_w_in, v_ffn2_w_out, v_ln3_g, v_ln3_b, v_w_ple_gate, v_b_ple_gate, v_w_ple, v_ln4_g, v_ln4_b):
    args = dict(locals())
    w = {n: args[n][0] for n in ORDER}
    mom = {n: args["m_" + n][0] for n in ORDER}
    var = {n: args["v_" + n][0] for n in ORDER}

    cast = {n: _cast_bf16(n, w[n]) for n in BIG}
    full = dict(zip(FIRST + ("conv_w",), _gather_shards([cast[n] for n in FIRST] + [w["conv_w"]])))
    W = dict(
        ffn1_in=full["ffn1_w_in"], ffn1_out=full["ffn1_w_out"].reshape(DFF, D),
        conv_w=_col_join(full["conv_w"]), b_gate=w["b_gate"][None], alog=_lane_row(w["a_log"]), dtb=_lane_row(w["dt_bias"]),
        normw=w["gdn_norm_w"][None], b_pg=w["b_ple_gate"][None],
        **{f"ln{i}_{s}": w[f"ln{i}_{s}"][None] for i in (1, 2, 3, 4) for s in ("g", "b")},
    )

    def fill(W_, arrived):
        got = dict(zip(LATER, arrived))
        W_.update(w_sb=_col_join(got["w_branch_sb"]), w_gdn=_col_join(got["w_branch_gdn"]), w_mo=got["w_mix_out"].reshape(D, D),
                  ffn2_in=got["ffn2_w_in"], ffn2_out=got["ffn2_w_out"].reshape(DFF, D),
                  w_pg=got["w_ple_gate"].reshape(D, D), w_ple=_col_join(got["w_ple"]))

    def by_shard(G_, names):
        forms = dict(
            ffn1_w_in=lambda: G_["ffn1_in"], ffn1_w_out=lambda: G_["ffn1_out"].reshape(NSH, DFF // NSH, D),
            w_mix_in=lambda: _unpack_wp(G_["wp"]), w_branch_sb=lambda: G_["w_sb"], w_branch_gdn=lambda: G_["w_gdn"],
            w_mix_out=lambda: G_["w_mo"].reshape(NSH, D // NSH, D), ffn2_w_in=lambda: G_["ffn2_in"],
            ffn2_w_out=lambda: G_["ffn2_out"].reshape(NSH, DFF // NSH, D),
            w_ple_gate=lambda: G_["w_pg"].reshape(NSH, D // NSH, D), w_ple=lambda: G_["w_ple"])
        return [_sibling_sum(n, forms[n]()) for n in names]

    early_sums, mid_sums = [], []

    def early(G_):
        early_sums.extend(by_shard(G_, EARLY))
        return _Ride(True, [narrow for _, narrow in early_sums])

    def mid(G_):
        mid_sums.extend(by_shard(G_, MID))
        return _Ride(True, [narrow for _, narrow in mid_sums])

    loss_row, grad_x, G = _local_step(
        x[0], p[0, 0], loss_target[0], W,
        soon=(_Ride(False, [cast[n] for n in SOON]), lambda W_, arrived: W_.update(wp=_pack_wp(arrived[0]))),
        late=(_Ride(False, [cast[n] for n in LATER]), fill), early=early, mid=mid)
    loss = lax.psum(0.5 * jnp.sum(loss_row) / D, ("x", "y", "c"))

    last_sums = by_shard(G, LAST)
    landed = list(G["early"]) + list(G["mid"]) + list(_reduce_chips([narrow for _, narrow in last_sums]))
    grad = {n: _chip_sum_share(n, b, own).reshape(w[n].shape)
            for n, b, (own, _) in zip(EARLY + MID + LAST, landed, early_sums + mid_sums + last_sums)}

    pieces = [G["ln1_g"], G["ln1_b"], G["b_gate"], G["conv_w"].reshape(1, 4 * CW), G["alog"], G["dtb"], G["normw"],
              G["ln2_g"], G["ln2_b"], G["ln3_g"], G["ln3_b"], G["b_pg"], G["ln4_g"], G["ln4_b"]]
    flat = jnp.concatenate(pieces, axis=1)
    flat = jnp.pad(flat, ((0, 0), (0, NDEV * PACK_W - flat.shape[1])))
    total = _allreduce_small(flat.reshape(NDEV, PACK_W)).reshape(1, NDEV * PACK_W)
    off = 0
    for n, piece in zip(SMALL, pieces):
        grad[n] = total[0, off:off + piece.shape[1]]
        off += piece.shape[1]
    chip = 2 * lax.axis_index("x") + lax.axis_index("y")
    grad["conv_w"] = lax.dynamic_slice_in_dim(grad["conv_w"].reshape(4, CW), chip * (CW // NSH), CW // NSH, axis=1)
    grad["a_log"] = grad["a_log"][HEADS:2 * HEADS]
    grad["dt_bias"] = grad["dt_bias"][HEADS:2 * HEADS]

    delta, new_m, new_v = {}, {}, {}
    for n in ORDER:
        shape2 = w[n].shape if w[n].ndim == 2 else (1, w[n].shape[0])
        d_, m_, v_ = _adamw(n, *[a.reshape(shape2) for a in (w[n], grad[n], mom[n], var[n])])
        delta[n], new_m[n], new_v[n] = (a.reshape(args[n].shape) for a in (d_, m_, v_))
    outs = [loss, grad_x[None]]
    outs += [grad[n].reshape(args[n].shape) for n in ORDER]
    for group in (delta, new_m, new_v):
        outs += [group[n] for n in ORDER]
    return tuple(outs)
```

```python
import jax
import jax.numpy as jnp
from jax import lax
from jax.experimental import pallas as pl
from jax.experimental.pallas import tpu as pltpu

F32 = jnp.float32
BF = jnp.bfloat16
I32 = jnp.int32
MESH = pl.DeviceIdType.MESH
ANY = pl.BlockSpec(memory_space=pl.ANY)

D = 1024
DFF = 2816
NSH = 4
FSH = 2 * DFF // NSH
NIN = 5648
MSH = NIN // NSH
NP = 6144
CUT = 3600
GATE0 = 4096
HEADS = 8
HD = 64
CH = 64
KB = 128
BIG_ROWS = 1024
ALPHA = 2.0 ** 0.25
LN_EPS = 1e-5
RMS_EPS = 1e-6
B1, B2, LR, EPS, WD, STEP = 0.9, 0.999, 0.001, 1e-08, 0.01, 10


def _sigmoid(x):
    return 0.5 * jnp.tanh(0.5 * x) + 0.5


def _softplus(x):
    return jnp.maximum(x, 0.0) + jnp.log1p(jnp.exp(-jnp.abs(x)))


def _layer_norm(r, g, b):
    mu = jnp.mean(r, axis=-1, keepdims=True)
    xc = r - mu
    var = jnp.mean(xc * xc, axis=-1, keepdims=True)
    return xc * lax.rsqrt(var + LN_EPS) * g + b


def _layer_norm_bwd(r, g, dh):
    mu = jnp.mean(r, axis=-1, keepdims=True)
    xc = r - mu
    var = jnp.mean(xc * xc, axis=-1, keepdims=True)
    xhat = xc * lax.rsqrt(var + LN_EPS)
    dxh = dh * g
    dr = lax.rsqrt(var + LN_EPS) * (dxh - jnp.mean(dxh, axis=-1, keepdims=True) - xhat * jnp.mean(dxh * xhat, axis=-1, keepdims=True))
    return dr, jnp.sum(dh * xhat, axis=0, keepdims=True), jnp.sum(dh, axis=0, keepdims=True)


def _pick(n, cap):
    if n <= cap:
        return n
    for t in range(cap - cap % 8, 7, -8):
        if n % t == 0:
            return t
    raise ValueError((n, cap))


def _mm(name, a, b, M, N, K, *, tm, tn, tk, ta=False, tb=False, a_spec=None, b_spec=None, order="ij",
        extras=(), epilogue=None, outs, n_acc=0, ride=None):
    ni, nj, nk = M // tm, N // tn, K // tk
    nr = ride.n if ride else 0
    assert M % tm == 0 and N % tn == 0 and K % tk == 0, (name, M, N, K, tm, tn, tk)
    assert n_acc == 0 or nj == 1

    def wrap(fn):
        if order == "ij":
            return lambda g0, g1, g2: fn(g0, g1, g2)
        return lambda g0, g1, g2: fn(g1, g0, g2)

    if a_spec is None:
        a_spec = ((tk, tm), lambda i, j, k: (k, i)) if ta else ((tm, tk), lambda i, j, k: (i, k))
    if b_spec is None:
        b_spec = ((tn, tk), lambda i, j, k: (j, k)) if tb else ((tk, tn), lambda i, j, k: (k, j))
    dims = (((0 if ta else 1,), (1 if tb else 0,)), ((), ()))
    ne, no = len(extras), len(outs)
    grid = (ni, nj, nk) if order == "ij" else (nj, ni, nk)

    def body(*refs):
        a_ref, b_ref = refs[0], refs[1]
        ex = refs[2:2 + ne]
        ride_in = refs[2 + ne:2 + ne + nr]
        o = refs[2 + ne + nr:2 + ne + nr + no]
        ride_out = refs[2 + ne + nr + no:2 + ne + 2 * nr + no]
        scratch = refs[2 + ne + 2 * nr + no:]
        g0, g1, k = pl.program_id(0), pl.program_id(1), pl.program_id(2)
        first = jnp.logical_and(g0 == 0, g1 == 0)
        if ride:
            @pl.when(jnp.logical_and(first, k == 0))
            def _():
                ride.run("start", ride_in, ride_out, scratch[-3:])

        p = lax.dot_general(a_ref[...].astype(BF), b_ref[...].astype(BF), dims, preferred_element_type=F32)

        def finish(acc):
            vals = (acc,) if epilogue is None else epilogue(acc, *[e[...] for e in ex])
            for idx, (ref, val) in enumerate(zip(o, vals)):
                if idx < no - n_acc:
                    ref[...] = val.astype(ref.dtype)
                else:
                    @pl.when(first)
                    def _(ref=ref, val=val):
                        ref[...] = val

                    @pl.when(jnp.logical_not(first))
                    def _(ref=ref, val=val):
                        ref[...] += val

        if nk == 1:
            finish(p)
        else:
            acc_ref = scratch[0]

            @pl.when(k == 0)
            def _():
                acc_ref[...] = p

            @pl.when(k > 0)
            def _():
                acc_ref[...] += p

            @pl.when(k == nk - 1)
            def _():
                finish(acc_ref[...])

        if ride:
            @pl.when(jnp.logical_and(jnp.logical_and(g0 == grid[0] - 1, g1 == grid[1] - 1), k == nk - 1))
            def _():
                ride.run("wait", ride_in, ride_out, scratch[-3:])

    in_specs = [pl.BlockSpec(a_spec[0], wrap(a_spec[1])), pl.BlockSpec(b_spec[0], wrap(b_spec[1]))]
    in_specs += [pl.BlockSpec(blk, wrap(fn)) for _, blk, fn in extras] + [ANY] * nr
    res = pl.pallas_call(
        body, name=name, grid=grid, in_specs=in_specs,
        out_specs=[pl.BlockSpec(blk, wrap(fn)) for _, _, blk, fn in outs] + [ANY] * nr,
        out_shape=[jax.ShapeDtypeStruct(shape, dt) for shape, dt, _, _ in outs] + (ride.out_shape if ride else []),
        scratch_shapes=([pltpu.VMEM((tm, tn), F32)] if nk > 1 else []) + (_exchange_sems(nr) if ride else []),
    )(a, b, *[e[0] for e in extras], *(ride.arrays if ride else []))
    return res


def _row(i, j, k):
    return (i, 0)


def _tile(i, j, k):
    return (i, j)


def _const(i, j, k):
    return (0, 0)


def _rows(name, fn, n_steps, ins, outs, n_acc=0):
    ni, no = len(ins), len(outs)

    def body(*refs):
        i = pl.program_id(0)
        vals = fn(*[r[...] for r in refs[:ni]])
        for idx, (ref, val) in enumerate(zip(refs[ni:ni + no], vals)):
            if idx < no - n_acc:
                ref[...] = val.astype(ref.dtype)
            else:
                @pl.when(i == 0)
                def _(ref=ref, val=val):
                    ref[...] = val

                @pl.when(i > 0)
                def _(ref=ref, val=val):
                    ref[...] += val

    return pl.pallas_call(
        body, name=name, grid=(n_steps,),
        in_specs=[pl.BlockSpec(blk, fn_) for _, blk, fn_ in ins],
        out_specs=[pl.BlockSpec(blk, fn_) for _, _, blk, fn_ in outs],
        out_shape=[jax.ShapeDtypeStruct(shape, dt) for shape, dt, _, _ in outs],
    )(*[a for a, _, _ in ins])


def _ffn_fwd(tag, x, w_in, w_out, g, b, ride=None):
    S = x.shape[0]
    tm = min(BIG_ROWS, S)
    gate, = _mm(f"{tag}_gate", x, w_in, S, DFF, D, tm=tm, tn=FSH, tk=D, order="ji",
                b_spec=((None, D, FSH), lambda i, j, k: (j, 0, 0)),
                outs=[((S, DFF), F32, (tm, FSH), _tile)])

    def up_epi(acc, gt):
        return acc, gt * _sigmoid(gt) * acc

    up, s, *arrived = _mm(f"{tag}_up", x, w_in, S, DFF, D, tm=tm, tn=FSH, tk=D, order="ji",
                          b_spec=((None, D, FSH), lambda i, j, k: (j + 2, 0, 0)),
                          extras=[(gate, (tm, FSH), _tile)], epilogue=up_epi, ride=ride,
                          outs=[((S, DFF), F32, (tm, FSH), _tile), ((S, DFF), BF, (tm, FSH), _tile)])

    def out_epi(acc, xin, gg, bb):
        r = ALPHA * xin + 0.5 * acc
        return _layer_norm(r, gg, bb), r

    h, r = _mm(f"{tag}_out", s, w_out, S, D, DFF, tm=tm, tn=D, tk=DFF,
               extras=[(x, (tm, D), _row), (g, (1, D), _const), (b, (1, D), _const)], epilogue=out_epi,
               outs=[((S, D), F32, (tm, D), _row), ((S, D), F32, (tm, D), _row)])
    return h, (x, gate, up, s, r), arrived


def _ln_bwd(tag, r, g, dh):
    S = r.shape[0]
    tm = min(512, S)
    return _rows(f"{tag}_lnbwd", _layer_norm_bwd, S // tm,
                 [(r, (tm, D), lambda i: (i, 0)), (g, (1, D), lambda i: (0, 0)), (dh, (tm, D), lambda i: (i, 0))],
                 [((S, D), F32, (tm, D), lambda i: (i, 0)), ((1, D), F32, (1, D), lambda i: (0, 0)),
                  ((1, D), F32, (1, D), lambda i: (0, 0))], n_acc=2)


def _ffn_bwd(tag, saved, w_in, w_out, g, dh, ride=None):
    x, gate, up, s, r = saved
    S = x.shape[0]
    tm = min(BIG_ROWS, S)
    dr, dg, db = _ln_bwd(tag, r, g, dh)

    def act_epi(acc, gt, u):
        ds = 0.5 * acc
        sg = _sigmoid(gt)
        return (jnp.stack([ds * u * (sg * (1.0 + gt * (1.0 - sg))), ds * (gt * sg)]),)

    da, = _mm(f"{tag}_dact", dr, w_out, S, DFF, D, tm=tm, tn=FSH, tk=D, tb=True, order="ji",
              extras=[(gate, (tm, FSH), _tile), (up, (tm, FSH), _tile)], epilogue=act_epi,
              outs=[((2, S, DFF), BF, (2, tm, FSH), lambda i, j, k: (0, i, j))])
    d_w_out, = _mm(f"{tag}_dwout", s, dr, DFF, D, S, tm=FSH, tn=D, tk=tm, ta=True,
                   epilogue=lambda acc: (0.5 * acc,), outs=[((DFF, D), F32, (FSH, D), _tile)])
    tb_ = min(BIG_ROWS, S)
    d_in, *arrived = _mm(f"{tag}_dx", da, w_in, S, D, 2 * DFF, tm=tb_, tn=D, tk=FSH, tb=True,
                         a_spec=((None, tb_, FSH), lambda i, j, k: (k // 2, i, k % 2)),
                         b_spec=((None, D, FSH), lambda i, j, k: (k, 0, 0)),
                         extras=[(dr, (tb_, D), _row)], epilogue=lambda acc, d: (acc + ALPHA * d,), ride=ride,
                         outs=[((S, D), F32, (tb_, D), _row)])
    d_w_in, = _mm(f"{tag}_dwin", x, da, D, 2 * DFF, S, tm=D, tn=FSH, tk=tb_, ta=True, order="ji",
                  b_spec=((None, tb_, FSH), lambda i, j, k: (j // 2, k, j % 2)),
                  outs=[((NSH, D, FSH), F32, (None, D, FSH), lambda i, j, k: (j, 0, 0))])
    return d_in, d_w_in, d_w_out, dg, db, arrived


SB = 256
NKC = SB // KB
HPS = 4
GW = HPS * HD
LOG2E = 1.4426950408889634


def _split(vals):
    hi = vals.astype(BF)
    return hi, (vals - hi.astype(F32)).astype(BF)


def _chunk_sums(tri2, vals):
    hi, lo = _split(vals)
    return [jnp.dot(tri2, jnp.concatenate([hi[c * KB:(c + 1) * KB], lo[c * KB:(c + 1) * KB]], axis=0), preferred_element_type=F32)
            for c in range(NKC)]


def _head_halves(t, axis):
    idx = lax.broadcasted_iota(I32, t.shape, axis)
    return [jnp.where(idx < HD, t, 0.0).astype(BF), jnp.where(idx >= HD, t, 0.0).astype(BF)]


QT = 512


def _diag_masks(sq):
    krow, qcol = lax.broadcasted_iota(I32, (SB, sq), 0), lax.broadcasted_iota(I32, (SB, sq), 1)
    return [krow + d * SB < qcol for d in range(sq // SB)]


def _softplus2(z):
    return jnp.maximum(z, 0.0) + jnp.log2(1.0 + jnp.exp2(jnp.minimum(z, -z)))


def _attn_prep(m):
    S = m.shape[0]
    tm = min(512, S)
    n = HEADS * HD
    return _rows("attn_prep", lambda k, v: (k, v, k.T, v.T), S // tm,
                 [(m, (tm, n), lambda i: (i, 1)), (m, (tm, n), lambda i: (i, 2))],
                 [((S, n), BF, (tm, n), lambda i: (i, 0)), ((S, n), BF, (tm, n), lambda i: (i, 0)),
                  ((n, S), BF, (n, tm), lambda i: (0, i)), ((n, S), BF, (n, tm), lambda i: (0, i))])


def _attn_fwd_t(m, kb_all, vt_all, ride=None):
    S = m.shape[0]
    SQ = min(QT, S)
    assert S % SQ == 0 and SQ % SB == 0
    nkc, nqb, nh, nq = S // KB, SQ // SB, HEADS // HPS, S // SQ
    nr = ride.n if ride else 0

    def body(*refs):
        q_ref, kb_hbm, vt_hbm = refs[:3]
        ride_in, (o_ref, r_ref), ride_out = refs[3:3 + nr], refs[3 + nr:5 + nr], refs[5 + nr:5 + 2 * nr]
        kb, vt, acc = refs[5 + 2 * nr:8 + 2 * nr]
        sems = refs[8 + 2 * nr:]
        h, i = pl.program_id(0), pl.program_id(1)
        if ride:
            @pl.when(jnp.logical_and(h == 0, i == 0))
            def _():
                ride.run("start", ride_in, ride_out, sems)

        @pl.when(i == 0)
        def _():
            cols = pl.ds(pl.multiple_of(h * GW, GW), GW)
            pltpu.sync_copy(kb_hbm.at[:, cols], kb)
            pltpu.sync_copy(vt_hbm.at[cols, :], vt)

        qt = (q_ref[...] * (0.125 * LOG2E)).T
        qtm = [t for g in range(HPS // 2) for t in _head_halves(qt[g * KB:(g + 1) * KB], 0)]
        dmasks = _diag_masks(SQ)
        upper = (lax.broadcasted_iota(I32, (KB, KB), 1) >= lax.broadcasted_iota(I32, (KB, KB), 0)).astype(BF)
        tri2 = jnp.concatenate([upper, upper], axis=1)
        acc[...] = jnp.zeros_like(acc)
        r_ref[...] = jnp.zeros_like(r_ref)

        def block(jb, runs, dmask):
            masked = dmask is not None
            off = pl.multiple_of(jb * SB, SB)
            groups = [slice(g * KB, (g + 1) * KB) for g in range(HPS // 2)]
            kblk = [kb[pl.ds(off, SB), s] for s in groups]
            vtb = [vt[s, pl.ds(off, SB)] for s in groups]
            old = acc[...]
            zs = [jnp.dot(kblk[hh // 2], qtm[hh], preferred_element_type=F32) for hh in range(HPS)]
            sps = [_softplus2(z) for z in zs]
            if masked:
                sps = [jnp.where(dmask, sp, 0.0) for sp in sps]
            css = [_chunk_sums(tri2, sp) for sp in sps]
            run0s = [run + cs[1][0:1, :] for run, cs in zip(runs, css)]
            aa = [jnp.exp2(z - jnp.concatenate([run0 + cs[0], run + cs[1]], axis=0)) for z, run, run0, cs in zip(zs, runs, run0s, css)]
            if masked:
                aa = [jnp.where(dmask, a, 0.0) for a in aa]
            parts = [jnp.dot(vtb[hh // 2], aa[hh].astype(BF), preferred_element_type=F32) for hh in range(HPS)]
            upd = jnp.concatenate([parts[hh][(hh % 2) * HD:(hh % 2 + 1) * HD, :] for hh in range(HPS)], axis=0)
            for hh in range(HPS):
                r_ref[hh, pl.ds(NKC * jb, 1), :] = run0s[hh]
                r_ref[hh, pl.ds(NKC * jb + 1, 1), :] = runs[hh]
            acc[...] = old + upd
            return tuple(run0 + cs[0][0:1, :] for run0, cs in zip(run0s, css))

        runs = (jnp.zeros((1, SQ), F32),) * HPS
        for d in reversed(range(nqb)):
            runs = block(i * nqb + d, runs, dmasks[d])
        lax.fori_loop(0, i * nqb, lambda t, c: block(i * nqb - 1 - t, c, None), runs)
        o_ref[...] = acc[...].T
        if ride:
            @pl.when(jnp.logical_and(h == nh - 1, i == nq - 1))
            def _():
                ride.run("wait", ride_in, ride_out, sems)

    return pl.pallas_call(
        body, name="attn_fwd", grid=(nh, nq),
        in_specs=[pl.BlockSpec((SQ, GW), lambda h, i: (i, h)), ANY, ANY] + [ANY] * nr,
        out_specs=[pl.BlockSpec((SQ, GW), lambda h, i: (i, h)), pl.BlockSpec((HPS, nkc, SQ), lambda h, i: (h, 0, i))] + [ANY] * nr,
        out_shape=[jax.ShapeDtypeStruct((S, HEADS * HD), F32), jax.ShapeDtypeStruct((HEADS, nkc, S), F32)]
        + (ride.out_shape if ride else []),
        scratch_shapes=[pltpu.VMEM((S, GW), BF), pltpu.VMEM((GW, S), BF), pltpu.VMEM((GW, SQ), F32)]
        + (_exchange_sems(nr) if ride else []),
    )(m, kb_all, vt_all, *(ride.arrays if ride else []))


def _attn_bwd_t(m, kb_all, vb_all, kt_all, runs, dy, ride=None):
    S = m.shape[0]
    SQ = min(QT, S)
    nkc, nqb, nh, nq = S // KB, SQ // SB, HEADS // HPS, S // SQ
    nr = ride.n if ride else 0

    def body(*refs):
        q_ref, kb_hbm, vb_hbm, kt_hbm, r_ref, dy_ref = refs[:6]
        ride_in, (dq_ref, dk_hbm, dv_hbm), ride_out = refs[6:6 + nr], refs[6 + nr:9 + nr], refs[9 + nr:9 + 2 * nr]
        kb, vb, kt, dqt, dka, dva = refs[9 + 2 * nr:15 + 2 * nr]
        sems = refs[15 + 2 * nr:]
        h, i = pl.program_id(0), pl.program_id(1)
        cols = pl.ds(pl.multiple_of(h * GW, GW), GW)
        if ride:
            @pl.when(jnp.logical_and(h == 0, i == 0))
            def _():
                ride.run("start", ride_in, ride_out, sems)

        @pl.when(i == 0)
        def _():
            pltpu.sync_copy(kb_hbm.at[:, cols], kb)
            pltpu.sync_copy(vb_hbm.at[:, cols], vb)
            pltpu.sync_copy(kt_hbm.at[cols, :], kt)
            dka[...] = jnp.zeros_like(dka)
            dva[...] = jnp.zeros_like(dva)

        q8 = q_ref[...] * 0.125
        dyf = dy_ref[...]
        q8t, dyt = (q8 * LOG2E).T, dyf.T
        groups = [slice(g * KB, (g + 1) * KB) for g in range(HPS // 2)]
        qtm = [t for s in groups for t in _head_halves(q8t[s], 0)]
        dytm = [t for s in groups for t in _head_halves(dyt[s], 0)]
        qlm = [t for s in groups for t in _head_halves(q8[:, s], 1)]
        dylm = [t for s in groups for t in _head_halves(dyf[:, s], 1)]
        dmasks = _diag_masks(SQ)
        ri, ci = lax.broadcasted_iota(I32, (KB, KB), 0), lax.broadcasted_iota(I32, (KB, KB), 1)
        upper, lower = (ci >= ri).astype(BF), (ci <= ri).astype(BF)
        rev2 = jnp.concatenate([upper, upper], axis=1)
        dqt[...] = jnp.zeros_like(dqt)

        def block(jb, pres, dmask):
            masked = dmask is not None
            off = pl.multiple_of(jb * SB, SB)
            heads = range(HPS)
            kblk = [kb[pl.ds(off, SB), s] for s in groups]
            vblk = [vb[pl.ds(off, SB), s] for s in groups]
            ktb = [kt[s, pl.ds(off, SB)] for s in groups]
            run0s = [r_ref[hh, pl.ds(NKC * jb, 1), :] for hh in heads]
            run1s = [r_ref[hh, pl.ds(NKC * jb + 1, 1), :] for hh in heads]
            old_dq, old_dk, old_dv = dqt[...], dka[pl.ds(off, SB), :], dva[pl.ds(off, SB), :]
            dks, dvs, parts, new = [], [], [], []
            for g in range(HPS // 2):
                hs = (2 * g, 2 * g + 1)
                zs = [jnp.dot(kblk[g], qtm[hh], preferred_element_type=F32) for hh in hs]
                das = [jnp.dot(vblk[g], dytm[hh], preferred_element_type=F32) for hh in hs]
                sps = [_softplus2(z) for z in zs]
                sigs = [jnp.exp2(z - sp) for z, sp in zip(zs, sps)]
                if masked:
                    sps = [jnp.where(dmask, sp, 0.0) for sp in sps]
                css = [_chunk_sums(rev2, sp) for sp in sps]
                aa = [jnp.exp2(z - jnp.concatenate([run0s[hh] + cs[0], run1s[hh] + cs[1]], axis=0)) for z, hh, cs in zip(zs, hs, css)]
                if masked:
                    aa = [jnp.where(dmask, a, 0.0) for a in aa]
                gs = [a * da for a, da in zip(aa, das)]
                pgs = [[jnp.dot(lower, gg[c * KB:(c + 1) * KB].astype(BF), preferred_element_type=F32) for c in range(NKC)] for gg in gs]
                pre1s = [pres[hh] + pg[0][KB - 1:KB, :] for hh, pg in zip(hs, pgs)]
                dzs = [gg - sig * jnp.concatenate([pres[hh] + pg[0], pre1 + pg[1]], axis=0)
                       for gg, sig, hh, pre1, pg in zip(gs, sigs, hs, pre1s, pgs)]
                if masked:
                    dzs = [jnp.where(dmask, dz, 0.0) for dz in dzs]
                dzb, ab = [dz.astype(BF) for dz in dzs], [a.astype(BF) for a in aa]
                dks.append(sum(jnp.dot(dzb[t], qlm[hh], preferred_element_type=F32) for t, hh in enumerate(hs)))
                dvs.append(sum(jnp.dot(ab[t], dylm[hh], preferred_element_type=F32) for t, hh in enumerate(hs)))
                parts += [jnp.dot(ktb[g], dzb[t], preferred_element_type=F32)[t * HD:(t + 1) * HD, :] for t in range(2)]
                new += [pre1 + pg[1][KB - 1:KB, :] for pre1, pg in zip(pre1s, pgs)]
            dqt[...] = old_dq + jnp.concatenate(parts, axis=0)
            dka[pl.ds(off, SB), :] = old_dk + jnp.concatenate(dks, axis=1)
            dva[pl.ds(off, SB), :] = old_dv + jnp.concatenate(dvs, axis=1)
            return tuple(new)

        pres = lax.fori_loop(0, i * nqb, lambda jb, c: block(jb, c, None), (jnp.zeros((1, SQ), F32),) * HPS)
        for d in range(nqb):
            pres = block(i * nqb + d, pres, dmasks[d])
        dq_ref[...] = (dqt[...].T * 0.125).astype(dq_ref.dtype)

        @pl.when(i == nq - 1)
        def _():
            pltpu.sync_copy(dka, dk_hbm.at[:, cols])
            pltpu.sync_copy(dva, dv_hbm.at[:, cols])

        if ride:
            @pl.when(jnp.logical_and(h == nh - 1, i == nq - 1))
            def _():
                ride.run("wait", ride_in, ride_out, sems)

    n = HEADS * HD
    return pl.pallas_call(
        body, name="attn_bwd", grid=(nh, nq),
        in_specs=[pl.BlockSpec((SQ, GW), lambda h, i: (i, h)), ANY, ANY, ANY,
                  pl.BlockSpec((HPS, nkc, SQ), lambda h, i: (h, 0, i)), pl.BlockSpec((SQ, GW), lambda h, i: (i, h))] + [ANY] * nr,
        out_specs=[pl.BlockSpec((SQ, GW), lambda h, i: (i, h)), ANY, ANY] + [ANY] * nr,
        out_shape=[jax.ShapeDtypeStruct((S, n), BF), jax.ShapeDtypeStruct((S, n), F32), jax.ShapeDtypeStruct((S, n), F32)]
        + (ride.out_shape if ride else []),
        scratch_shapes=[pltpu.VMEM((S, GW), BF), pltpu.VMEM((S, GW), BF), pltpu.VMEM((GW, S), BF), pltpu.VMEM((GW, SQ), F32),
                        pltpu.VMEM((S, GW), F32), pltpu.VMEM((S, GW), F32)] + (_exchange_sems(nr) if ride else []),
    )(m, kb_all, vb_all, kt_all, runs, dy, *(ride.arrays if ride else []))


CW = 3 * HEADS * HD


def _shift_down(cur, prev8, s):
    if s == 0:
        return cur
    r = pltpu.roll(cur, s, 0)
    first = jnp.where(lax.broadcasted_iota(I32, (8, cur.shape[1]), 0) < s, pltpu.roll(prev8, s, 0), r[:8])
    return jnp.concatenate([first, r[8:]], axis=0)


def _shift_up(cur, next8, s):
    if s == 0:
        return cur
    n = cur.shape[0]
    r = pltpu.roll(cur, n - s, 0)
    last = jnp.where(lax.broadcasted_iota(I32, (8, cur.shape[1]), 0) >= 8 - s, pltpu.roll(next8, 8 - s, 0), r[n - 8:])
    return jnp.concatenate([r[:n - 8], last], axis=0)


def _conv_fwd(m, conv_w):
    S = m.shape[0]
    tm = min(512, S)
    hb = tm // 8

    def body(x_ref, p_ref, w_ref, o_ref):
        i = pl.program_id(0)
        cur = x_ref[...]
        prev = jnp.where(i > 0, p_ref[...], 0.0)
        w = w_ref[...]
        acc = cur * w[3:4]
        for jk in range(3):
            acc = acc + _shift_down(cur, prev, 3 - jk) * w[jk:jk + 1]
        o_ref[...] = acc

    return pl.pallas_call(
        body, name="conv_fwd", grid=(S // tm,),
        in_specs=[pl.BlockSpec((tm, CW), lambda i: (i, 1)), pl.BlockSpec((8, CW), lambda i: (jnp.maximum(i * hb - 1, 0), 1)),
                  pl.BlockSpec((4, CW), lambda i: (0, 0))],
        out_specs=pl.BlockSpec((tm, CW), lambda i: (i, 0)),
        out_shape=jax.ShapeDtypeStruct((S, CW), F32),
    )(m, m, conv_w)


def _conv_bwd(m, dyc, conv_w):
    S = m.shape[0]
    tm = min(512, S)
    hb = tm // 8
    nt = S // tm

    def body(x_ref, p_ref, d_ref, n_ref, w_ref, dx_ref, dw_ref):
        i = pl.program_id(0)
        cur = x_ref[...]
        prev = jnp.where(i > 0, p_ref[...], 0.0)
        d = d_ref[...]
        nxt = jnp.where(i < nt - 1, n_ref[...], 0.0)
        w = w_ref[...]
        acc = d * w[3:4]
        dws = []
        for jk in range(3):
            acc = acc + _shift_up(d, nxt, 3 - jk) * w[jk:jk + 1]
            dws.append(jnp.sum(d * _shift_down(cur, prev, 3 - jk), axis=0, keepdims=True))
        dws.append(jnp.sum(d * cur, axis=0, keepdims=True))
        dx_ref[...] = acc.astype(dx_ref.dtype)
        dw = jnp.concatenate(dws, axis=0)

        @pl.when(i == 0)
        def _():
            dw_ref[...] = dw

        @pl.when(i > 0)
        def _():
            dw_ref[...] += dw

    return pl.pallas_call(
        body, name="conv_bwd", grid=(nt,),
        in_specs=[pl.BlockSpec((tm, CW), lambda i: (i, 1)), pl.BlockSpec((8, CW), lambda i: (jnp.maximum(i * hb - 1, 0), 1)),
                  pl.BlockSpec((tm, CW), lambda i: (i, 0)),
                  pl.BlockSpec((8, CW), lambda i: (jnp.minimum((i + 1) * hb, S // 8 - 1), 0)),
                  pl.BlockSpec((4, CW), lambda i: (0, 0))],
        out_specs=[pl.BlockSpec((tm, CW), lambda i: (i, 0)), pl.BlockSpec((4, CW), lambda i: (0, 0))],
        out_shape=[jax.ShapeDtypeStruct((S, CW), BF), jax.ShapeDtypeStruct((4, CW), F32)],
    )(m, m, dyc, dyc, conv_w)


def _t(a):
    return jnp.swapaxes(a, 1, 2)


def _bdot(a, b):
    return jnp.einsum("hik,hkj->hij", a, b, preferred_element_type=F32)


@jax.custom_vjp
def _mm1(a, b):
    return _bdot(a.astype(BF), b.astype(BF))


def _bdot_nt(a, b):
    return jnp.einsum("hij,hkj->hik", a, b, preferred_element_type=F32)


def _bdot_tn(a, b):
    return jnp.einsum("hki,hkj->hij", a, b, preferred_element_type=F32)


_mm1.defvjp(lambda a, b: (_mm1(a, b), (a.astype(BF), b.astype(BF))),
            lambda res, dc: (_bdot_nt(dc.astype(BF), res[1]), _bdot_tn(res[0], dc.astype(BF))))


@jax.custom_vjp
def _mm1_nt(a, b):
    return _bdot_nt(a.astype(BF), b.astype(BF))


_mm1_nt.defvjp(lambda a, b: (_mm1_nt(a, b), (a.astype(BF), b.astype(BF))),
               lambda res, dc: (_bdot(dc.astype(BF), res[1]), _bdot_tn(dc.astype(BF), res[0])))


@jax.custom_vjp
def _mm1_tn(a, b):
    return _bdot_tn(a.astype(BF), b.astype(BF))


_mm1_tn.defvjp(lambda a, b: (_mm1_tn(a, b), (a.astype(BF), b.astype(BF))),
               lambda res, dc: (_bdot_nt(res[1], dc.astype(BF)), _bdot(res[0], dc.astype(BF))))


def _stack_rows(hi, lo):
    return jnp.concatenate([hi, lo], axis=1)


@jax.custom_vjp
def _mm3(a, b):
    (ah, al), (bh, bl) = _split(a), _split(b)
    n = a.shape[1]
    two = _bdot(_stack_rows(ah, al), bh)
    return two[:, :n] + two[:, n:] + _bdot(ah, bl)


def _mm3_fwd(a, b):
    return _mm3(a, b), (_split(a), _split(b))


def _mm3_bwd(res, dc):
    (ah, al), (bh, bl) = res
    dh, dl = _split(dc)
    n = dc.shape[1]
    two = _bdot_nt(_stack_rows(dh, dl), bh)
    da = two[:, :n] + two[:, n:] + _bdot_nt(dh, bl)
    db = _bdot_tn(jnp.concatenate([ah, ah, al], axis=1), jnp.concatenate([dh, dl, dh], axis=1))
    return da, db


_mm3.defvjp(_mm3_fwd, _mm3_bwd)


def _mm_exact(c3, b):
    hi, lo = _split(b)
    lo2 = (b - hi.astype(F32) - lo.astype(F32)).astype(BF)
    return _bdot(c3, jnp.concatenate([hi, lo, lo2], axis=-2))


@jax.custom_vjp
def _cumsum_rows(b):
    return _mm_exact(_tri3(True), b)


def _tri3(lower):
    ri = lax.broadcasted_iota(I32, (HEADS, CH, CH), 1)
    ci = lax.broadcasted_iota(I32, (HEADS, CH, CH), 2)
    tri = (ri >= ci if lower else ri <= ci).astype(BF)
    return jnp.concatenate([tri, tri, tri], axis=-1)


_cumsum_rows.defvjp(lambda b: (_cumsum_rows(b), None), lambda _, dc: (_mm_exact(_tri3(False), dc),))


CPS = 4


def _gdn_chunk(yc, gz, gba, alog, dtb):
    def heads(t, off):
        return jnp.stack([t[:, off + h * HD: off + (h + 1) * HD] for h in range(HEADS)])

    def cols(t, off):
        return jnp.stack([jnp.broadcast_to(t[:, off + h: off + h + 1], (CH, CH)) for h in range(HEADS)])

    c = yc * _sigmoid(yc)
    q, k, v, zg = heads(c, 0), heads(c, HEADS * HD), heads(c, 2 * HEADS * HD), heads(gz, 0)
    q = q * lax.rsqrt(jnp.sum(q * q, axis=-1, keepdims=True) + RMS_EPS) * (HD ** -0.5)
    k = k * lax.rsqrt(jnp.sum(k * k, axis=-1, keepdims=True) + RMS_EPS)
    beta = cols(_sigmoid(gba), 0)
    g = cols(-jnp.exp(alog) * _softplus(gba + dtb), HEADS)
    ri = lax.broadcasted_iota(I32, (HEADS, CH, CH), 1)
    ci = lax.broadcasted_iota(I32, (HEADS, CH, CH), 2)
    causal, strict = ri >= ci, ri > ci
    eye = (ri == ci).astype(F32)
    gc = _cumsum_rows(g)
    gr = _t(gc)
    decay = jnp.where(causal, jnp.exp(jnp.where(causal, gc - gr, 0.0)), 0.0)
    lower = jnp.where(strict, beta * _mm1_nt(k, k) * decay, 0.0)
    pw = -lower
    inv = eye + pw
    pw = _mm3(pw, pw)
    for _ in range(4):
        both = _mm3(jnp.concatenate([inv, pw], axis=1), pw)
        inv, pw = inv + both[:, :CH], both[:, CH:]
    inv = inv + _mm3(inv, pw)
    eg = jnp.exp(gc)
    uw = _mm3(inv, jnp.concatenate([v * beta, k * (beta * eg)], axis=2))
    u, w = uw[:, :, :HD], uw[:, :, HD:]
    qk = jnp.where(causal, _mm1_nt(q, k) * decay, 0.0)
    g_last = gc[:, CH - 1:CH, :]
    return u, jnp.concatenate([w, q * eg], axis=1), qk, k * jnp.exp(g_last - gc), jnp.exp(g_last), zg * _sigmoid(zg)


def _gdn_advance(state, pre, normw):
    u, wq, qk, kd, last, gate = pre
    ws = _mm1(wq, state)
    v_new = u - ws[:, :CH]
    o = ws[:, CH:] + _mm1(qk, v_new)
    new_state = state * last + _mm1_tn(kd, v_new)
    o = o * lax.rsqrt(jnp.mean(o * o, axis=-1, keepdims=True) + RMS_EPS) * normw * gate
    return jnp.concatenate([o[h] for h in range(HEADS)], axis=1), new_state


def _gdn_chunks(state, yc, gz, gba, alog, dtb, normw):
    rows = [slice(c * CH, (c + 1) * CH) for c in range(yc.shape[0] // CH)]
    pres = [_gdn_chunk(yc[r], gz[r], gba[r], alog, dtb) for r in rows]
    outs = []
    for pre in pres:
        o, state = _gdn_advance(state, pre, normw)
        outs.append(o)
    return jnp.concatenate(outs, axis=0), state


def _gdn_fwd(yc, m, alog, dtb, normw):
    S = yc.shape[0]
    RS = CPS * CH
    nch = S // RS

    def body(y_ref, gz_ref, gba_ref, al_ref, dt_ref, nw_ref, o_ref, st_ref, st):
        @pl.when(pl.program_id(0) == 0)
        def _():
            st[...] = jnp.zeros_like(st)

        cur = st[...]
        st_ref[0] = cur
        o, new = _gdn_chunks(cur, y_ref[...], gz_ref[...], gba_ref[...], al_ref[...], dt_ref[...], nw_ref[...])
        o_ref[...] = o
        st[...] = new

    return pl.pallas_call(
        body, name="gdn_fwd", grid=(nch,),
        in_specs=[pl.BlockSpec((RS, CW), lambda n: (n, 0)), pl.BlockSpec((RS, HEADS * HD), lambda n: (n, 6)),
                  pl.BlockSpec((RS, 128), lambda n: (n, 28)), pl.BlockSpec((1, 128), lambda n: (0, 0)),
                  pl.BlockSpec((1, 128), lambda n: (0, 0)), pl.BlockSpec((1, HD), lambda n: (0, 0))],
        out_specs=[pl.BlockSpec((RS, HEADS * HD), lambda n: (n, 0)), pl.BlockSpec((1, HEADS, HD, HD), lambda n: (n, 0, 0, 0))],
        out_shape=[jax.ShapeDtypeStruct((S, HEADS * HD), F32), jax.ShapeDtypeStruct((nch, HEADS, HD, HD), F32)],
        scratch_shapes=[pltpu.VMEM((HEADS, HD, HD), F32)],
    )(yc, m, m, alog, dtb, normw)


def _gdn_bwd(yc, m, alog, dtb, normw, states, dog):
    S = yc.shape[0]
    RS = CPS * CH
    nch = S // RS

    def body(y_ref, gz_ref, gba_ref, al_ref, dt_ref, nw_ref, st_ref, do_ref, dy_ref, dgz_ref, dgba_ref, dal_ref, ddt_ref, dnw_ref, dst):
        n = pl.program_id(0)

        @pl.when(n == 0)
        def _():
            dst[...] = jnp.zeros_like(dst)

        _, vjp = jax.vjp(_gdn_chunks, st_ref[0], y_ref[...], gz_ref[...], gba_ref[...], al_ref[...], dt_ref[...], nw_ref[...])
        d_state, d_y, d_gz, d_gba, d_al, d_dt, d_nw = vjp((do_ref[...], dst[...]))
        dst[...] = d_state
        dy_ref[...] = d_y
        dgz_ref[...] = d_gz.astype(dgz_ref.dtype)
        dgba_ref[...] = d_gba.astype(dgba_ref.dtype)
        for ref, val in ((dal_ref, d_al), (ddt_ref, d_dt), (dnw_ref, d_nw)):
            @pl.when(n == 0)
            def _(ref=ref, val=val):
                ref[...] = val

            @pl.when(n > 0)
            def _(ref=ref, val=val):
                ref[...] += val

    rev = lambda n: nch - 1 - n
    return pl.pallas_call(
        body, name="gdn_bwd", grid=(nch,),
        in_specs=[pl.BlockSpec((RS, CW), lambda n: (rev(n), 0)), pl.BlockSpec((RS, HEADS * HD), lambda n: (rev(n), 6)),
                  pl.BlockSpec((RS, 128), lambda n: (rev(n), 28)), pl.BlockSpec((1, 128), lambda n: (0, 0)),
                  pl.BlockSpec((1, 128), lambda n: (0, 0)), pl.BlockSpec((1, HD), lambda n: (0, 0)),
                  pl.BlockSpec((1, HEADS, HD, HD), lambda n: (rev(n), 0, 0, 0)),
                  pl.BlockSpec((RS, HEADS * HD), lambda n: (rev(n), 0))],
        out_specs=[pl.BlockSpec((RS, CW), lambda n: (rev(n), 0)), pl.BlockSpec((RS, HEADS * HD), lambda n: (rev(n), 0)),
                   pl.BlockSpec((RS, 128), lambda n: (rev(n), 0)), pl.BlockSpec((1, 128), lambda n: (0, 0)),
                   pl.BlockSpec((1, 128), lambda n: (0, 0)), pl.BlockSpec((1, HD), lambda n: (0, 0))],
        out_shape=[jax.ShapeDtypeStruct((S, CW), F32), jax.ShapeDtypeStruct((S, HEADS * HD), BF),
                   jax.ShapeDtypeStruct((S, 128), BF), jax.ShapeDtypeStruct((1, 128), F32),
                   jax.ShapeDtypeStruct((1, 128), F32), jax.ShapeDtypeStruct((1, HD), F32)],
        scratch_shapes=[pltpu.VMEM((HEADS, HD, HD), F32)],
    )(yc, m, m, alog, dtb, normw, states, dog)


def _mixer_fwd(h1, W, late=None):
    S = h1.shape[0]
    tm = min(512, S)
    n = HEADS * HD
    tb_ = min(BIG_ROWS, S)
    m, = _mm("mix_in", h1, W["wp"], S, NP, D, tm=tb_, tn=1536, tk=D, order="ji", outs=[((S, NP), F32, (tb_, 1536), _tile)])
    kb_all, vb_all, kt_all, vt_all = _attn_prep(m)
    ya, runs, *arrived = _attn_fwd_t(m, kb_all, vt_all, ride=late[0] if late else None)
    if late:
        late[1](W, arrived)
    runs = (kb_all, vb_all, kt_all, runs)
    b_gate, conv_w, alog, dtb, normw, w_sb, w_gdn, w_mo, g, b = (
        W[k] for k in ("b_gate", "conv_w", "alog", "dtb", "normw", "w_sb", "w_gdn", "w_mo", "ln2_g", "ln2_b"))
    yc = _conv_fwd(m, conv_w)
    og, states = _gdn_fwd(yc, m, alog, dtb, normw)
    ysb, = _mm("mix_sb", ya, w_sb, S, D, n, tm=tm, tn=D, tk=n, outs=[((S, D), F32, (tm, D), _row)])

    def merge_epi(acc, ys, gs, gg, bg):
        return _sigmoid(gs + bg[:, :D]) * ys + _sigmoid(gg + bg[:, D:]) * acc, acc

    u, ygdn = _mm("mix_gdn", og, w_gdn, S, D, n, tm=tm, tn=D, tk=n,
                  extras=[(ysb, (tm, D), _row), (m, (tm, D), lambda i, j, k: (i, GATE0 // D)),
                          (m, (tm, D), lambda i, j, k: (i, GATE0 // D + 1)), (b_gate, (1, 2 * D), _const)],
                  epilogue=merge_epi, outs=[((S, D), BF, (tm, D), _row), ((S, D), F32, (tm, D), _row)])

    def out_epi(acc, xin, gg, bb):
        r = ALPHA * xin + acc
        return _layer_norm(r, gg, bb), r

    h2, r2 = _mm("mix_out", u, w_mo, S, D, D, tm=tm, tn=D, tk=D,
                 extras=[(h1, (tm, D), _row), (g, (1, D), _const), (b, (1, D), _const)], epilogue=out_epi,
                 outs=[((S, D), F32, (tm, D), _row), ((S, D), F32, (tm, D), _row)])
    return h2, (h1, m, ya, runs, yc, og, states, ysb, ygdn, u, r2)


def _mixer_bwd(saved, W, dh, ride=None):
    h1, m, ya, runs, yc, og, states, ysb, ygdn, u, r2 = saved
    wp, b_gate, conv_w, alog, dtb, normw, w_sb, w_gdn, w_mo, g = (
        W[k] for k in ("wp", "b_gate", "conv_w", "alog", "dtb", "normw", "w_sb", "w_gdn", "w_mo", "ln2_g"))
    S = h1.shape[0]
    tm = min(512, S)
    n = HEADS * HD
    dr, dg, db = _ln_bwd("mix", r2, g, dh)

    def merge_epi(du, ys, yg, gs, gg, bg):
        s1, s2 = _sigmoid(gs + bg[:, :D]), _sigmoid(gg + bg[:, D:])
        dgate = jnp.concatenate([du * ys * s1 * (1.0 - s1), du * yg * s2 * (1.0 - s2)], axis=1)
        return du * s1, du * s2, dgate, jnp.sum(dgate, axis=0, keepdims=True)

    dysb, dygdn, dgate, d_bg = _mm(
        "mix_dmerge", dr, w_mo, S, D, D, tm=tm, tn=D, tk=D, tb=True,
        extras=[(ysb, (tm, D), _row), (ygdn, (tm, D), _row), (m, (tm, D), lambda i, j, k: (i, GATE0 // D)),
                (m, (tm, D), lambda i, j, k: (i, GATE0 // D + 1)), (b_gate, (1, 2 * D), _const)],
        epilogue=merge_epi, n_acc=1,
        outs=[((S, D), BF, (tm, D), _row), ((S, D), BF, (tm, D), _row), ((S, 2 * D), BF, (tm, 2 * D), _row),
              ((1, 2 * D), F32, (1, 2 * D), _const)])
    d_w_mo, = _mm("mix_dwmo", u, dr, D, D, S, tm=D, tn=D, tk=min(BIG_ROWS, S), ta=True, outs=[((D, D), F32, (D, D), _tile)])
    dya, = _mm("mix_dya", dysb, w_sb, S, n, D, tm=tm, tn=n, tk=D, tb=True, outs=[((S, n), F32, (tm, n), _row)])
    col_shards = [((NSH, n, D // NSH), F32, (None, n, D // NSH), lambda i, j, k: (j, 0, 0))]
    d_w_sb, = _mm("mix_dwsb", ya, dysb, n, D, S, tm=n, tn=D // NSH, tk=min(BIG_ROWS, S), ta=True, order="ji", outs=col_shards)
    dog, = _mm("mix_dog", dygdn, w_gdn, S, n, D, tm=tm, tn=n, tk=D, tb=True, outs=[((S, n), F32, (tm, n), _row)])
    d_w_gdn, = _mm("mix_dwgdn", og, dygdn, n, D, S, tm=n, tn=D // NSH, tk=min(BIG_ROWS, S), ta=True, order="ji", outs=col_shards)
    dq, dk, dv, *arrived = _attn_bwd_t(m, *runs, dya, ride=ride)
    dyc, dgz, dgba, d_alog, d_dtb, d_normw = _gdn_bwd(yc, m, alog, dtb, normw, states, dog)
    dxc, d_conv = _conv_bwd(m, dyc, conv_w)
    dm = jnp.concatenate([dq, dk.astype(BF), dv.astype(BF), dxc, dgz, dgba, jnp.zeros((S, GATE0 - 3712), BF), dgate], axis=1)
    tb_ = min(BIG_ROWS, S)
    d_h1, = _mm("mix_dh", dm, wp, S, D, NP, tm=tb_, tn=D, tk=1536, tb=True,
                extras=[(dr, (tb_, D), _row)], epilogue=lambda acc, d: (acc + ALPHA * d,),
                outs=[((S, D), F32, (tb_, D), _row)])
    d_wp, = _mm("mix_dwp", h1, dm, D, NP, S, tm=D, tn=1536, tk=tb_, ta=True, order="ji",
                outs=[((D, NP), F32, (D, 1536), _tile)])
    return d_h1, dict(wp=d_wp, b_gate=d_bg, conv_w=d_conv, alog=d_alog, dtb=d_dtb, normw=d_normw,
                      w_sb=d_w_sb, w_gdn=d_w_gdn, w_mo=d_w_mo, g=dg, b=db), arrived


def _ple_fwd(h3, p, w_pg, b_pg, w_ple, g, b, target):
    S = h3.shape[0]
    tm = min(512, S)
    pd = p.shape[1]
    pe, = _mm("ple_emb", p, w_ple, S, D, pd, tm=tm, tn=D, tk=pd, outs=[((S, D), F32, (tm, D), _row)])

    def epi(acc, e, xin, tgt, bp, gg, bb):
        gt = _sigmoid(acc + bp)
        r = ALPHA * xin + gt * e
        diff = _layer_norm(r, gg, bb) - tgt
        return gt, r, diff * (1.0 / D), jnp.sum(diff * diff, axis=0, keepdims=True)

    gt, r4, dh4, loss_row = _mm(
        "ple_gate", h3, w_pg, S, D, D, tm=tm, tn=D, tk=D,
        extras=[(pe, (tm, D), _row), (h3, (tm, D), _row), (target, (tm, D), _row), (b_pg, (1, D), _const),
                (g, (1, D), _const), (b, (1, D), _const)], epilogue=epi, n_acc=1,
        outs=[((S, D), F32, (tm, D), _row), ((S, D), F32, (tm, D), _row), ((S, D), F32, (tm, D), _row),
              ((1, D), F32, (1, D), _const)])
    return dh4, loss_row, (h3, p, pe, gt, r4)


def _ple_bwd(saved, w_pg, g, dh4):
    h3, p, pe, gt, r4 = saved
    S = h3.shape[0]
    tm = min(512, S)
    pd = p.shape[1]

    def fn(r, gg, dh, e, t):
        dr, dg, db = _layer_norm_bwd(r, gg, dh)
        dpre = dr * e * t * (1.0 - t)
        return dr, dpre, dr * t, dg, db, jnp.sum(dpre, axis=0, keepdims=True)

    row, one = (lambda i: (i, 0)), (lambda i: (0, 0))
    dr, dpre, dpe, dg, db, d_bpg = _rows(
        "ple_lnbwd", fn, S // tm,
        [(r4, (tm, D), row), (g, (1, D), one), (dh4, (tm, D), row), (pe, (tm, D), row), (gt, (tm, D), row)],
        [((S, D), F32, (tm, D), row), ((S, D), BF, (tm, D), row), ((S, D), BF, (tm, D), row),
         ((1, D), F32, (1, D), one), ((1, D), F32, (1, D), one), ((1, D), F32, (1, D), one)], n_acc=3)
    d_w_pg, = _mm("ple_dwpg", h3, dpre, D, D, S, tm=D, tn=D, tk=min(BIG_ROWS, S), ta=True, outs=[((D, D), F32, (D, D), _tile)])
    d_w_ple, = _mm("ple_dwple", p, dpe, pd, D, S, tm=pd, tn=D // NSH, tk=min(BIG_ROWS, S), ta=True, order="ji",
                   outs=[((NSH, pd, D // NSH), F32, (None, pd, D // NSH), lambda i, j, k: (j, 0, 0))])
    d_h3, = _mm("ple_dh", dpre, w_pg, S, D, D, tm=tm, tn=D, tk=D, tb=True,
                extras=[(dr, (tm, D), _row)], epilogue=lambda acc, d: (acc + ALPHA * d,),
                outs=[((S, D), F32, (tm, D), _row)])
    return d_h3, d_w_pg, d_bpg, d_w_ple, dg, db


def _local_step(x, p, target, W, soon=None, late=None, early=None, mid=None):
    W = dict(W)
    h1, sv1, arrived = _ffn_fwd("ffn1", x, W["ffn1_in"], W["ffn1_out"], W["ln1_g"], W["ln1_b"], ride=soon[0] if soon else None)
    if soon:
        soon[1](W, arrived)
    h2, sv2 = _mixer_fwd(h1, W, late)
    h3, sv3, _ = _ffn_fwd("ffn2", h2, W["ffn2_in"], W["ffn2_out"], W["ln3_g"], W["ln3_b"])
    dh4, loss_row, sv4 = _ple_fwd(h3, p, W["w_pg"], W["b_pg"], W["w_ple"], W["ln4_g"], W["ln4_b"], target)
    G = {}
    dh3, G["w_pg"], G["b_pg"], G["w_ple"], G["ln4_g"], G["ln4_b"] = _ple_bwd(sv4, W["w_pg"], W["ln4_g"], dh4)
    dh2, G["ffn2_in"], G["ffn2_out"], G["ln3_g"], G["ln3_b"], _ = _ffn_bwd("ffn2", sv3, W["ffn2_in"], W["ffn2_out"], W["ln3_g"], dh3)
    dh1, gm, G["early"] = _mixer_bwd(sv2, W, dh2, ride=early(G) if early else None)
    G.update({k: v for k, v in gm.items() if k not in ("g", "b")})
    G["ln2_g"], G["ln2_b"] = gm["g"], gm["b"]
    dx, G["ffn1_in"], G["ffn1_out"], G["ln1_g"], G["ln1_b"], G["mid"] = _ffn_bwd(
        "ffn1", sv1, W["ffn1_in"], W["ffn1_out"], W["ln1_g"], dh1, ride=mid(G) if mid else None)
    return loss_row, dx, G


S2 = CUT - 2 * MSH


def _pack_wp(w4):
    tr = 256

    def fn(w):
        s = [w[j].astype(F32) for j in range(NSH)]
        full = jnp.concatenate([s[0], s[1], s[2][:, :S2], jnp.zeros((tr, GATE0 - CUT), F32), s[2][:, S2:], s[3]], axis=1)
        return (full,)

    return _rows("pack_wp", fn, D // tr, [(w4, (NSH, tr, MSH), lambda i: (0, i, 0))],
                 [((D, NP), BF, (tr, NP), lambda i: (i, 0))])[0]


def _unpack_wp(d):
    tr = 256
    g2 = GATE0 + MSH - S2

    def fn(v):
        return (jnp.stack([v[:, :MSH], v[:, MSH:2 * MSH], jnp.concatenate([v[:, 2 * MSH:CUT], v[:, GATE0:g2]], axis=1), v[:, g2:]]),)

    return _rows("unpack_wp", fn, D // tr, [(d, (tr, NP), lambda i: (i, 0))],
                 [((NSH, D, MSH), F32, (NSH, tr, MSH), lambda i: (0, i, 0))])[0]


def _cast_bf16(tag, w):
    r, c = w.shape
    tr = _pick(r, 256)
    return _rows(f"cast_{tag}", lambda v: (v,), r // tr, [(w, (tr, c), lambda i: (i, 0))],
                 [((r, c), BF, (tr, c), lambda i: (i, 0))])[0]


def _place():
    return lax.axis_index("x"), lax.axis_index("y"), lax.axis_index("c")


def _chip_exchange(phase, scatter, ins, outs, send, recv, loc):
    x, y, c = _place()
    me = 2 * x + y
    chips = [(1 - x, y), (x, 1 - y), (1 - x, 1 - y)]
    for t in range(len(ins)):
        own = pltpu.make_async_copy(ins[t].at[me] if scatter else ins[t], outs[t].at[me], loc.at[t])
        out_going, in_coming = [], []
        for q, (px, py) in enumerate(chips):
            src = ins[t].at[2 * px + py] if scatter else ins[t]
            sems = dict(send_sem=send.at[3 * t + q], recv_sem=recv.at[3 * t + q], device_id=(px, py, c), device_id_type=MESH)
            out_going.append(pltpu.make_async_remote_copy(src_ref=src, dst_ref=outs[t].at[me], **sems))
            in_coming.append(pltpu.make_async_remote_copy(src_ref=src, dst_ref=outs[t].at[2 * px + py], **sems))
        if phase == "start":
            own.start()
            for cp in out_going:
                cp.start()
        else:
            for cp in in_coming:
                cp.wait_recv()
            own.wait()
            for cp in out_going:
                cp.wait_send()


def _exchange_sems(n):
    return [pltpu.SemaphoreType.DMA((3 * n,)), pltpu.SemaphoreType.DMA((3 * n,)), pltpu.SemaphoreType.DMA((n,))]


class _Ride:
    def __init__(self, scatter, arrays):
        self.scatter, self.arrays, self.n = scatter, list(arrays), len(arrays)
        self.out_shape = [jax.ShapeDtypeStruct(a.shape if scatter else (NSH,) + a.shape, a.dtype) for a in self.arrays]

    def run(self, phase, refs_in, refs_out, sems):
        _chip_exchange(phase, self.scatter, refs_in, refs_out, *sems)


def _gather_shards(shards):
    n = len(shards)

    def body(*refs):
        _chip_exchange("start", False, refs[:n], refs[n:2 * n], *refs[2 * n:])
        _chip_exchange("wait", False, refs[:n], refs[n:2 * n], *refs[2 * n:])

    return pl.pallas_call(
        body, name="gather_weights", in_specs=[ANY] * n, out_specs=[ANY] * n,
        out_shape=[jax.ShapeDtypeStruct((NSH,) + s.shape, s.dtype) for s in shards], scratch_shapes=_exchange_sems(n),
    )(*shards)


def _reduce_chips(ps):
    n = len(ps)

    def body(*refs):
        _chip_exchange("start", True, refs[:n], refs[n:2 * n], *refs[2 * n:])
        _chip_exchange("wait", True, refs[:n], refs[n:2 * n], *refs[2 * n:])

    return pl.pallas_call(
        body, name="reduce_chips", in_specs=[ANY] * n, out_specs=[ANY] * n,
        out_shape=[jax.ShapeDtypeStruct(p_.shape, p_.dtype) for p_ in ps], scratch_shapes=_exchange_sems(n),
    )(*ps)


def _chunk_rows(h, c):
    return _pick(h, max(8, 524288 // c // 8 * 8))


def _sibling_sum(tag, g):
    _, r, c = g.shape
    h = r // 2
    tr = _chunk_rows(h, c)
    nch = h // tr
    steps = NSH * nch

    def body(top_ref, bot_ref, out_ref, narrow_ref, land, send, recv):
        s = pl.program_id(0)
        x, y, core = _place()

        def exchange(keep_ref, give_ref):
            cp = pltpu.make_async_remote_copy(src_ref=give_ref.at[0], dst_ref=land.at[s], send_sem=send.at[s], recv_sem=recv.at[s],
                                              device_id=(x, y, 1 - core), device_id_type=MESH)
            cp.start()
            cp.wait()
            total = keep_ref[0] + land[s]
            out_ref[0] = total
            narrow_ref[0] = total.astype(BF)

        @pl.when(core == 0)
        def _():
            exchange(top_ref, bot_ref)

        @pl.when(core == 1)
        def _():
            exchange(bot_ref, top_ref)

    return pl.pallas_call(
        body, name=f"sibling_sum_{tag}", grid=(steps,),
        in_specs=[pl.BlockSpec((1, tr, c), lambda s: (s // nch, s % nch, 0)),
                  pl.BlockSpec((1, tr, c), lambda s: (s // nch, nch + s % nch, 0))],
        out_specs=[pl.BlockSpec((1, tr, c), lambda s: (s // nch, s % nch, 0))] * 2,
        out_shape=[jax.ShapeDtypeStruct((NSH, h, c), F32), jax.ShapeDtypeStruct((NSH, h, c), BF)],
        scratch_shapes=[pltpu.VMEM((steps, tr, c), F32), pltpu.SemaphoreType.DMA((steps,)), pltpu.SemaphoreType.DMA((steps,))],
    )(g, g)


def _chip_sum_share(tag, b, own):
    _, h, c = b.shape
    tr = _chunk_rows(h, c)
    steps = h // tr

    def body(b_ref, own_ref, out_ref, stage, land, send, recv):
        s = pl.program_id(0)
        x, y, core = _place()
        me = 2 * x + y
        v = [jnp.where(me == j, own_ref[j], b_ref[j].astype(F32)) for j in range(NSH)]
        total = ((v[0] + v[1]) + v[2]) + v[3]
        stage[...] = total
        cp = pltpu.make_async_remote_copy(src_ref=stage, dst_ref=land.at[s], send_sem=send.at[s], recv_sem=recv.at[s],
                                          device_id=(x, y, 1 - core), device_id_type=MESH)
        cp.start()
        cp.wait()
        out_ref[core] = total
        out_ref[1 - core] = land[s]

    return pl.pallas_call(
        body, name=f"chip_sum_share_{tag}", grid=(steps,),
        in_specs=[pl.BlockSpec((NSH, tr, c), lambda s: (0, s, 0))] * 2,
        out_specs=pl.BlockSpec((2, tr, c), lambda s: (0, s, 0)),
        out_shape=jax.ShapeDtypeStruct((2, h, c), F32),
        scratch_shapes=[pltpu.VMEM((tr, c), F32), pltpu.VMEM((steps, tr, c), F32), pltpu.SemaphoreType.DMA((steps,)),
                        pltpu.SemaphoreType.DMA((steps,))],
    )(b, own)


NDEV = 8


def _allreduce_small(pack):
    r, w = pack.shape
    rel = [(dx, dy, dc) for dx in (0, 1) for dy in (0, 1) for dc in (0, 1) if (dx, dy, dc) != (0, 0, 0)]

    def body(in_ref, out_ref, buf, send, recv):
        x, y, c = _place()
        me = 4 * x + 2 * y + c
        buf[me] = in_ref[...]
        peers = [((x + dx) % 2, (y + dy) % 2, (c + dc) % 2) for dx, dy, dc in rel]
        sent = []
        for k, peer in enumerate(peers):
            cp = pltpu.make_async_remote_copy(src_ref=in_ref, dst_ref=buf.at[me], send_sem=send.at[k], recv_sem=recv.at[k],
                                              device_id=peer, device_id_type=MESH)
            cp.start()
            sent.append(cp)
        for k, (px, py, pc) in enumerate(peers):
            pltpu.make_async_remote_copy(src_ref=in_ref, dst_ref=buf.at[4 * px + 2 * py + pc], send_sem=send.at[k], recv_sem=recv.at[k],
                                         device_id=(px, py, pc), device_id_type=MESH).wait_recv()
        for cp in sent:
            cp.wait_send()
        acc = buf[0]
        for k in range(1, NDEV):
            acc = acc + buf[k]
        out_ref[...] = acc

    vm = pl.BlockSpec(memory_space=pltpu.VMEM)
    return pl.pallas_call(
        body, name="allreduce_small", in_specs=[vm], out_specs=vm, out_shape=jax.ShapeDtypeStruct((r, w), F32),
        scratch_shapes=[pltpu.VMEM((NDEV, r, w), F32), pltpu.SemaphoreType.DMA((NDEV - 1,)), pltpu.SemaphoreType.DMA((NDEV - 1,))],
    )(pack)


def _adamw(tag, w, g, m, v):
    r, c = w.shape
    tr = _pick(r, max(8, 262144 // c // 8 * 8))

    def fn(w_, g_, m_, v_):
        m2 = B1 * m_ + (1.0 - B1) * g_
        v2 = B2 * v_ + (1.0 - B2) * (g_ * g_)
        m_hat = m2 / (1.0 - B1 ** STEP)
        v_hat = v2 / (1.0 - B2 ** STEP)
        return -LR * (m_hat / (jnp.sqrt(v_hat) + EPS) + WD * w_), m2, v2

    spec = ((tr, c), lambda i: (i, 0))
    return _rows(f"adamw_{tag}", fn, r // tr, [(a,) + spec for a in (w, g, m, v)], [((r, c), F32) + spec] * 3)


BIG = ("ffn1_w_in", "ffn1_w_out", "w_mix_in", "w_branch_sb", "w_branch_gdn", "w_mix_out", "ffn2_w_in", "ffn2_w_out",
       "w_ple_gate", "w_ple")
FIRST = ("ffn1_w_in", "ffn1_w_out")
SOON = ("w_mix_in",)
LATER = tuple(n for n in BIG if n not in FIRST + SOON)
EARLY = ("ffn2_w_in", "ffn2_w_out", "w_ple_gate", "w_ple")
MID = ("w_mix_in", "w_branch_sb", "w_branch_gdn", "w_mix_out")
LAST = ("ffn1_w_in", "ffn1_w_out")
SMALL = ("ln1_g", "ln1_b", "b_gate", "conv_w", "a_log", "dt_bias", "gdn_norm_w", "ln2_g", "ln2_b", "ln3_g", "ln3_b",
         "b_ple_gate", "ln4_g", "ln4_b")
ORDER = ("ffn1_w_in", "ffn1_w_out", "ln1_g", "ln1_b", "w_mix_in", "b_gate", "conv_w", "a_log", "dt_bias", "gdn_norm_w",
         "w_branch_sb", "w_branch_gdn", "w_mix_out", "ln2_g", "ln2_b", "ffn2_w_in", "ffn2_w_out", "ln3_g", "ln3_b",
         "w_ple_gate", "b_ple_gate", "w_ple", "ln4_g", "ln4_b")
PACK_W = 2304


def _lane_row(v, lanes=128, at=HEADS):
    return jnp.pad(v[None, :], ((0, 0), (at, lanes - at - v.shape[0])))


def _col_join(w4):
    return jnp.transpose(w4, (1, 0, 2)).reshape(w4.shape[1], NSH * w4.shape[2])


def kernel(x, p, ffn1_w_in, ffn1_w_out, ln1_g, ln1_b, w_mix_in, b_gate, conv_w, a_log, dt_bias, gdn_norm_w, w_branch_sb, w_branch_gdn, w_mix_out, ln2_g, ln2_b, ffn2_w_in, ffn2_w_out, ln3_g, ln3_b, w_ple_gate, b_ple_gate, w_ple, ln4_g, ln4_b, loss_target, m_ffn1_w_in, m_ffn1_w_out, m_ln1_g, m_ln1_b, m_w_mix_in, m_b_gate, m_conv_w, m_a_log, m_dt_bias, m_gdn_norm_w, m_w_branch_sb, m_w_branch_gdn, m_w_mix_out, m_ln2_g, m_ln2_b, m_ffn2_w_in, m_ffn2_w_out, m_ln3_g, m_ln3_b, m_w_ple_gate, m_b_ple_gate, m_w_ple, m_ln4_g, m_ln4_b, v_ffn1_w_in, v_ffn1_w_out, v_ln1_g, v_ln1_b, v_w_mix_in, v_b_gate, v_conv_w, v_a_log, v_dt_bias, v_gdn_norm_w, v_w_branch_sb, v_w_branch_gdn, v_w_mix_out, v_ln2_g, v_ln2_b, v_ffn2_w_in, v_ffn2_w_out, v_ln3_g, v_ln3_b, v_w_ple_gate, v_b_ple_gate, v_w_ple, v_ln4_g, v_ln4_b):
    args = dict(locals())
    w = {n: args[n][0] for n in ORDER}
    mom = {n: args["m_" + n][0] for n in ORDER}
    var = {n: args["v_" + n][0] for n in ORDER}

    cast = {n: _cast_bf16(n, w[n]) for n in BIG}
    full = dict(zip(FIRST + ("conv_w",), _gather_shards([cast[n] for n in FIRST] + [w["conv_w"]])))
    W = dict(
        ffn1_in=full["ffn1_w_in"], ffn1_out=full["ffn1_w_out"].reshape(DFF, D),
        conv_w=_col_join(full["conv_w"]), b_gate=w["b_gate"][None], alog=_lane_row(w["a_log"]), dtb=_lane_row(w["dt_bias"]),
        normw=w["gdn_norm_w"][None], b_pg=w["b_ple_gate"][None],
        **{f"ln{i}_{s}": w[f"ln{i}_{s}"][None] for i in (1, 2, 3, 4) for s in ("g", "b")},
    )

    def fill(W_, arrived):
        got = dict(zip(LATER, arrived))
        W_.update(w_sb=_col_join(got["w_branch_sb"]), w_gdn=_col_join(got["w_branch_gdn"]), w_mo=got["w_mix_out"].reshape(D, D),
                  ffn2_in=got["ffn2_w_in"], ffn2_out=got["ffn2_w_out"].reshape(DFF, D),
                  w_pg=got["w_ple_gate"].reshape(D, D), w_ple=_col_join(got["w_ple"]))

    def by_shard(G_, names):
        forms = dict(
            ffn1_w_in=lambda: G_["ffn1_in"], ffn1_w_out=lambda: G_["ffn1_out"].reshape(NSH, DFF // NSH, D),
            w_mix_in=lambda: _unpack_wp(G_["wp"]), w_branch_sb=lambda: G_["w_sb"], w_branch_gdn=lambda: G_["w_gdn"],
            w_mix_out=lambda: G_["w_mo"].reshape(NSH, D // NSH, D), ffn2_w_in=lambda: G_["ffn2_in"],
            ffn2_w_out=lambda: G_["ffn2_out"].reshape(NSH, DFF // NSH, D),
            w_ple_gate=lambda: G_["w_pg"].reshape(NSH, D // NSH, D), w_ple=lambda: G_["w_ple"])
        return [_sibling_sum(n, forms[n]()) for n in names]

    early_sums, mid_sums = [], []

    def early(G_):
        early_sums.extend(by_shard(G_, EARLY))
        return _Ride(True, [narrow for _, narrow in early_sums])

    def mid(G_):
        mid_sums.extend(by_shard(G_, MID))
        return _Ride(True, [narrow for _, narrow in mid_sums])

    loss_row, grad_x, G = _local_step(
        x[0], p[0, 0], loss_target[0], W,
        soon=(_Ride(False, [cast[n] for n in SOON]), lambda W_, arrived: W_.update(wp=_pack_wp(arrived[0]))),
        late=(_Ride(False, [cast[n] for n in LATER]), fill), early=early, mid=mid)
    loss = lax.psum(0.5 * jnp.sum(loss_row) / D, ("x", "y", "c"))

    last_sums = by_shard(G, LAST)
    landed = list(G["early"]) + list(G["mid"]) + list(_reduce_chips([narrow for _, narrow in last_sums]))
    grad = {n: _chip_sum_share(n, b, own).reshape(w[n].shape)
            for n, b, (own, _) in zip(EARLY + MID + LAST, landed, early_sums + mid_sums + last_sums)}

    pieces = [G["ln1_g"], G["ln1_b"], G["b_gate"], G["conv_w"].reshape(1, 4 * CW), G["alog"], G["dtb"], G["normw"],
              G["ln2_g"], G["ln2_b"], G["ln3_g"], G["ln3_b"], G["b_pg"], G["ln4_g"], G["ln4_b"]]
    flat = jnp.concatenate(pieces, axis=1)
    flat = jnp.pad(flat, ((0, 0), (0, NDEV * PACK_W - flat.shape[1])))
    total = _allreduce_small(flat.reshape(NDEV, PACK_W)).reshape(1, NDEV * PACK_W)
    off = 0
    for n, piece in zip(SMALL, pieces):
        grad[n] = total[0, off:off + piece.shape[1]]
        off += piece.shape[1]
    chip = 2 * lax.axis_index("x") + lax.axis_index("y")
    grad["conv_w"] = lax.dynamic_slice_in_dim(grad["conv_w"].reshape(4, CW), chip * (CW // NSH), CW // NSH, axis=1)
    grad["a_log"] = grad["a_log"][HEADS:2 * HEADS]
    grad["dt_bias"] = grad["dt_bias"][HEADS:2 * HEADS]

    delta, new_m, new_v = {}, {}, {}
    for n in ORDER:
        shape2 = w[n].shape if w[n].ndim == 2 else (1, w[n].shape[0])
        d_, m_, v_ = _adamw(n, *[a.reshape(shape2) for a in (w[n], grad[n], mom[n], var[n])])
        delta[n], new_m[n], new_v[n] = (a.reshape(args[n].shape) for a in (d_, m_, v_))
    outs = [loss, grad_x[None]]
    outs += [grad[n].reshape(args[n].shape) for n in ORDER]
    for group in (delta, new_m, new_v):
        outs += [group[n] for n in ORDER]
    return tuple(outs)
```

```python
import jax
import jax.numpy as jnp
from jax import lax
from jax.experimental import pallas as pl
from jax.experimental.pallas import tpu as pltpu

F32 = jnp.float32
BF = jnp.bfloat16
I32 = jnp.int32
MESH = pl.DeviceIdType.MESH
ANY = pl.BlockSpec(memory_space=pl.ANY)

D = 1024
DFF = 2816
NSH = 4
FSH = 2 * DFF // NSH
NIN = 5648
MSH = NIN // NSH
NP = 6144
CUT = 3600
GATE0 = 4096
HEADS = 8
HD = 64
CH = 64
KB = 128
BIG_ROWS = 1024
ALPHA = 2.0 ** 0.25
LN_EPS = 1e-5
RMS_EPS = 1e-6
B1, B2, LR, EPS, WD, STEP = 0.9, 0.999, 0.001, 1e-08, 0.01, 10


def _sigmoid(x):
    return 0.5 * jnp.tanh(0.5 * x) + 0.5


def _softplus(x):
    return jnp.maximum(x, 0.0) + jnp.log1p(jnp.exp(-jnp.abs(x)))


def _layer_norm(r, g, b):
    mu = jnp.mean(r, axis=-1, keepdims=True)
    xc = r - mu
    var = jnp.mean(xc * xc, axis=-1, keepdims=True)
    return xc * lax.rsqrt(var + LN_EPS) * g + b


def _layer_norm_bwd(r, g, dh):
    mu = jnp.mean(r, axis=-1, keepdims=True)
    xc = r - mu
    var = jnp.mean(xc * xc, axis=-1, keepdims=True)
    xhat = xc * lax.rsqrt(var + LN_EPS)
    dxh = dh * g
    dr = lax.rsqrt(var + LN_EPS) * (dxh - jnp.mean(dxh, axis=-1, keepdims=True) - xhat * jnp.mean(dxh * xhat, axis=-1, keepdims=True))
    return dr, jnp.sum(dh * xhat, axis=0, keepdims=True), jnp.sum(dh, axis=0, keepdims=True)


def _pick(n, cap):
    if n <= cap:
        return n
    for t in range(cap - cap % 8, 7, -8):
        if n % t == 0:
            return t
    raise ValueError((n, cap))


def _mm(name, a, b, M, N, K, *, tm, tn, tk, ta=False, tb=False, a_spec=None, b_spec=None, order="ij",
        extras=(), epilogue=None, outs, n_acc=0, ride=None):
    ni, nj, nk = M // tm, N // tn, K // tk
    nr = ride.n if ride else 0
    assert M % tm == 0 and N % tn == 0 and K % tk == 0, (name, M, N, K, tm, tn, tk)
    assert n_acc == 0 or nj == 1

    def wrap(fn):
        if order == "ij":
            return lambda g0, g1, g2: fn(g0, g1, g2)
        return lambda g0, g1, g2: fn(g1, g0, g2)

    if a_spec is None:
        a_spec = ((tk, tm), lambda i, j, k: (k, i)) if ta else ((tm, tk), lambda i, j, k: (i, k))
    if b_spec is None:
        b_spec = ((tn, tk), lambda i, j, k: (j, k)) if tb else ((tk, tn), lambda i, j, k: (k, j))
    dims = (((0 if ta else 1,), (1 if tb else 0,)), ((), ()))
    ne, no = len(extras), len(outs)
    grid = (ni, nj, nk) if order == "ij" else (nj, ni, nk)

    def body(*refs):
        a_ref, b_ref = refs[0], refs[1]
        ex = refs[2:2 + ne]
        ride_in = refs[2 + ne:2 + ne + nr]
        o = refs[2 + ne + nr:2 + ne + nr + no]
        ride_out = refs[2 + ne + nr + no:2 + ne + 2 * nr + no]
        scratch = refs[2 + ne + 2 * nr + no:]
        g0, g1, k = pl.program_id(0), pl.program_id(1), pl.program_id(2)
        first = jnp.logical_and(g0 == 0, g1 == 0)
        if ride:
            @pl.when(jnp.logical_and(first, k == 0))
            def _():
                ride.run("start", ride_in, ride_out, scratch[-3:])

        p = lax.dot_general(a_ref[...].astype(BF), b_ref[...].astype(BF), dims, preferred_element_type=F32)

        def finish(acc):
            vals = (acc,) if epilogue is None else epilogue(acc, *[e[...] for e in ex])
            for idx, (ref, val) in enumerate(zip(o, vals)):
                if idx < no - n_acc:
                    ref[...] = val.astype(ref.dtype)
                else:
                    @pl.when(first)
                    def _(ref=ref, val=val):
                        ref[...] = val

                    @pl.when(jnp.logical_not(first))
                    def _(ref=ref, val=val):
                        ref[...] += val

        if nk == 1:
            finish(p)
        else:
            acc_ref = scratch[0]

            @pl.when(k == 0)
            def _():
                acc_ref[...] = p

            @pl.when(k > 0)
            def _():
                acc_ref[...] += p

            @pl.when(k == nk - 1)
            def _():
                finish(acc_ref[...])

        if ride:
            @pl.when(jnp.logical_and(jnp.logical_and(g0 == grid[0] - 1, g1 == grid[1] - 1), k == nk - 1))
            def _():
                ride.run("wait", ride_in, ride_out, scratch[-3:])

    in_specs = [pl.BlockSpec(a_spec[0], wrap(a_spec[1])), pl.BlockSpec(b_spec[0], wrap(b_spec[1]))]
    in_specs += [pl.BlockSpec(blk, wrap(fn)) for _, blk, fn in extras] + [ANY] * nr
    res = pl.pallas_call(
        body, name=name, grid=grid, in_specs=in_specs,
        out_specs=[pl.BlockSpec(blk, wrap(fn)) for _, _, blk, fn in outs] + [ANY] * nr,
        out_shape=[jax.ShapeDtypeStruct(shape, dt) for shape, dt, _, _ in outs] + (ride.out_shape if ride else []),
        scratch_shapes=([pltpu.VMEM((tm, tn), F32)] if nk > 1 else []) + (_exchange_sems(nr) if ride else []),
    )(a, b, *[e[0] for e in extras], *(ride.arrays if ride else []))
    return res


def _row(i, j, k):
    return (i, 0)


def _tile(i, j, k):
    return (i, j)


def _const(i, j, k):
    return (0, 0)


def _rows(name, fn, n_steps, ins, outs, n_acc=0):
    ni, no = len(ins), len(outs)

    def body(*refs):
        i = pl.program_id(0)
        vals = fn(*[r[...] for r in refs[:ni]])
        for idx, (ref, val) in enumerate(zip(refs[ni:ni + no], vals)):
            if idx < no - n_acc:
                ref[...] = val.astype(ref.dtype)
            else:
                @pl.when(i == 0)
                def _(ref=ref, val=val):
                    ref[...] = val

                @pl.when(i > 0)
                def _(ref=ref, val=val):
                    ref[...] += val

    return pl.pallas_call(
        body, name=name, grid=(n_steps,),
        in_specs=[pl.BlockSpec(blk, fn_) for _, blk, fn_ in ins],
        out_specs=[pl.BlockSpec(blk, fn_) for _, _, blk, fn_ in outs],
        out_shape=[jax.ShapeDtypeStruct(shape, dt) for shape, dt, _, _ in outs],
    )(*[a for a, _, _ in ins])


def _ffn_fwd(tag, x, w_in, w_out, g, b, ride=None, w_out_ride=None):
    S = x.shape[0]
    tm = min(BIG_ROWS, S)
    gate, *came = _mm(f"{tag}_gate", x, w_in, S, DFF, D, tm=tm, tn=FSH, tk=D, order="ji",
                      b_spec=((None, D, FSH), lambda i, j, k: (j, 0, 0)), ride=w_out_ride,
                      outs=[((S, DFF), F32, (tm, FSH), _tile)])
    if w_out_ride:
        w_out = came[0].reshape(DFF, D)

    def up_epi(acc, gt):
        return acc, gt * _sigmoid(gt) * acc

    up, s, *arrived = _mm(f"{tag}_up", x, w_in, S, DFF, D, tm=tm, tn=FSH, tk=D, order="ji",
                          b_spec=((None, D, FSH), lambda i, j, k: (j + 2, 0, 0)),
                          extras=[(gate, (tm, FSH), _tile)], epilogue=up_epi, ride=ride,
                          outs=[((S, DFF), F32, (tm, FSH), _tile), ((S, DFF), BF, (tm, FSH), _tile)])

    def out_epi(acc, xin, gg, bb):
        r = ALPHA * xin + 0.5 * acc
        return _layer_norm(r, gg, bb), r

    h, r = _mm(f"{tag}_out", s, w_out, S, D, DFF, tm=tm, tn=D, tk=DFF,
               extras=[(x, (tm, D), _row), (g, (1, D), _const), (b, (1, D), _const)], epilogue=out_epi,
               outs=[((S, D), F32, (tm, D), _row), ((S, D), F32, (tm, D), _row)])
    return h, (x, gate, up, s, r), arrived, w_out


def _ln_bwd(tag, r, g, dh):
    S = r.shape[0]
    tm = min(512, S)
    return _rows(f"{tag}_lnbwd", _layer_norm_bwd, S // tm,
                 [(r, (tm, D), lambda i: (i, 0)), (g, (1, D), lambda i: (0, 0)), (dh, (tm, D), lambda i: (i, 0))],
                 [((S, D), F32, (tm, D), lambda i: (i, 0)), ((1, D), F32, (1, D), lambda i: (0, 0)),
                  ((1, D), F32, (1, D), lambda i: (0, 0))], n_acc=2)


def _ffn_bwd(tag, saved, w_in, w_out, g, dh, ride=None):
    x, gate, up, s, r = saved
    S = x.shape[0]
    tm = min(BIG_ROWS, S)
    dr, dg, db = _ln_bwd(tag, r, g, dh)

    def act_epi(acc, gt, u):
        ds = 0.5 * acc
        sg = _sigmoid(gt)
        return (jnp.stack([ds * u * (sg * (1.0 + gt * (1.0 - sg))), ds * (gt * sg)]),)

    da, = _mm(f"{tag}_dact", dr, w_out, S, DFF, D, tm=tm, tn=FSH, tk=D, tb=True, order="ji",
              extras=[(gate, (tm, FSH), _tile), (up, (tm, FSH), _tile)], epilogue=act_epi,
              outs=[((2, S, DFF), BF, (2, tm, FSH), lambda i, j, k: (0, i, j))])
    d_w_out, = _mm(f"{tag}_dwout", s, dr, DFF, D, S, tm=FSH, tn=D, tk=tm, ta=True,
                   epilogue=lambda acc: (0.5 * acc,), outs=[((DFF, D), F32, (FSH, D), _tile)])
    tb_ = min(BIG_ROWS, S)
    d_in, *arrived = _mm(f"{tag}_dx", da, w_in, S, D, 2 * DFF, tm=tb_, tn=D, tk=FSH, tb=True,
                         a_spec=((None, tb_, FSH), lambda i, j, k: (k // 2, i, k % 2)),
                         b_spec=((None, D, FSH), lambda i, j, k: (k, 0, 0)),
                         extras=[(dr, (tb_, D), _row)], epilogue=lambda acc, d: (acc + ALPHA * d,), ride=ride,
                         outs=[((S, D), F32, (tb_, D), _row)])
    d_w_in, = _mm(f"{tag}_dwin", x, da, D, 2 * DFF, S, tm=D, tn=FSH, tk=tb_, ta=True, order="ji",
                  b_spec=((None, tb_, FSH), lambda i, j, k: (j // 2, k, j % 2)),
                  outs=[((NSH, D, FSH), F32, (None, D, FSH), lambda i, j, k: (j, 0, 0))])
    return d_in, d_w_in, d_w_out, dg, db, arrived


SB = 256
NKC = SB // KB
HPF = 8
HPS = 4
GW = HPS * HD
LOG2E = 1.4426950408889634


def _split(vals):
    hi = vals.astype(BF)
    return hi, (vals - hi.astype(F32)).astype(BF)


def _chunk_sums(tri2, vals):
    hi, lo = _split(vals)
    return [jnp.dot(tri2, jnp.concatenate([hi[c * KB:(c + 1) * KB], lo[c * KB:(c + 1) * KB]], axis=0), preferred_element_type=F32)
            for c in range(NKC)]


def _head_halves(t, axis):
    idx = lax.broadcasted_iota(I32, t.shape, axis)
    return [jnp.where(idx < HD, t, 0.0).astype(BF), jnp.where(idx >= HD, t, 0.0).astype(BF)]


QT = 512


def _diag_masks(sq):
    krow, qcol = lax.broadcasted_iota(I32, (SB, sq), 0), lax.broadcasted_iota(I32, (SB, sq), 1)
    return [krow + d * SB < qcol for d in range(sq // SB)]


def _softplus2(z):
    return jnp.maximum(z, 0.0) + jnp.log2(1.0 + jnp.exp2(jnp.minimum(z, -z)))


def _attn_prep(m):
    S = m.shape[0]
    tm = min(512, S)
    n = HEADS * HD
    return _rows("attn_prep", lambda k, v: (k, v, k.T, v.T), S // tm,
                 [(m, (tm, n), lambda i: (i, 1)), (m, (tm, n), lambda i: (i, 2))],
                 [((S, n), BF, (tm, n), lambda i: (i, 0)), ((S, n), BF, (tm, n), lambda i: (i, 0)),
                  ((n, S), BF, (n, tm), lambda i: (0, i)), ((n, S), BF, (n, tm), lambda i: (0, i))])


def _attn_fwd_t(m, kb_all, vt_all, ride=None):
    S = m.shape[0]
    SQ = min(QT, S)
    assert S % SQ == 0 and SQ % SB == 0
    HPS, GW = HPF, HPF * HD
    nkc, nqb, nh, nq = S // KB, SQ // SB, HEADS // HPS, S // SQ
    nr = ride.n if ride else 0

    def body(*refs):
        q_ref, kb_hbm, vt_hbm = refs[:3]
        ride_in, (o_ref, r_ref), ride_out = refs[3:3 + nr], refs[3 + nr:5 + nr], refs[5 + nr:5 + 2 * nr]
        kb, vt, acc = refs[5 + 2 * nr:8 + 2 * nr]
        sems = refs[8 + 2 * nr:]
        h, i = pl.program_id(0), pl.program_id(1)
        if ride:
            @pl.when(jnp.logical_and(h == 0, i == 0))
            def _():
                ride.run("start", ride_in, ride_out, sems)

        @pl.when(i == 0)
        def _():
            cols = pl.ds(pl.multiple_of(h * GW, GW), GW)
            pltpu.sync_copy(kb_hbm.at[:, cols], kb)
            pltpu.sync_copy(vt_hbm.at[cols, :], vt)

        qt = (q_ref[...] * (0.125 * LOG2E)).T
        qtm = [t for g in range(HPS // 2) for t in _head_halves(qt[g * KB:(g + 1) * KB], 0)]
        dmasks = _diag_masks(SQ)
        upper = (lax.broadcasted_iota(I32, (KB, KB), 1) >= lax.broadcasted_iota(I32, (KB, KB), 0)).astype(BF)
        tri2 = jnp.concatenate([upper, upper], axis=1)
        acc[...] = jnp.zeros_like(acc)
        r_ref[...] = jnp.zeros_like(r_ref)

        def block(jb, runs, dmask):
            masked = dmask is not None
            off = pl.multiple_of(jb * SB, SB)
            groups = [slice(g * KB, (g + 1) * KB) for g in range(HPS // 2)]
            kblk = [kb[pl.ds(off, SB), s] for s in groups]
            vtb = [vt[s, pl.ds(off, SB)] for s in groups]
            old = acc[...]
            zs = [jnp.dot(kblk[hh // 2], qtm[hh], preferred_element_type=F32) for hh in range(HPS)]
            sps = [_softplus2(z) for z in zs]
            if masked:
                sps = [jnp.where(dmask, sp, 0.0) for sp in sps]
            css = [_chunk_sums(tri2, sp) for sp in sps]
            run0s = [run + cs[1][0:1, :] for run, cs in zip(runs, css)]
            aa = [jnp.exp2(z - jnp.concatenate([run0 + cs[0], run + cs[1]], axis=0)) for z, run, run0, cs in zip(zs, runs, run0s, css)]
            if masked:
                aa = [jnp.where(dmask, a, 0.0) for a in aa]
            parts = [jnp.dot(vtb[hh // 2], aa[hh].astype(BF), preferred_element_type=F32) for hh in range(HPS)]
            upd = jnp.concatenate([parts[hh][(hh % 2) * HD:(hh % 2 + 1) * HD, :] for hh in range(HPS)], axis=0)
            for hh in range(HPS):
                r_ref[hh, pl.ds(NKC * jb, 1), :] = run0s[hh]
                r_ref[hh, pl.ds(NKC * jb + 1, 1), :] = runs[hh]
            acc[...] = old + upd
            return tuple(run0 + cs[0][0:1, :] for run0, cs in zip(run0s, css))

        runs = (jnp.zeros((1, SQ), F32),) * HPS
        for d in reversed(range(nqb)):
            runs = block(i * nqb + d, runs, dmasks[d])
        lax.fori_loop(0, i * nqb, lambda t, c: block(i * nqb - 1 - t, c, None), runs)
        o_ref[...] = acc[...].T
        if ride:
            @pl.when(jnp.logical_and(h == nh - 1, i == nq - 1))
            def _():
                ride.run("wait", ride_in, ride_out, sems)

    return pl.pallas_call(
        body, name="attn_fwd", grid=(nh, nq),
        in_specs=[pl.BlockSpec((SQ, GW), lambda h, i: (i, h)), ANY, ANY] + [ANY] * nr,
        out_specs=[pl.BlockSpec((SQ, GW), lambda h, i: (i, h)), pl.BlockSpec((HPS, nkc, SQ), lambda h, i: (h, 0, i))] + [ANY] * nr,
        out_shape=[jax.ShapeDtypeStruct((S, HEADS * HD), F32), jax.ShapeDtypeStruct((HEADS, nkc, S), F32)]
        + (ride.out_shape if ride else []),
        scratch_shapes=[pltpu.VMEM((S, GW), BF), pltpu.VMEM((GW, S), BF), pltpu.VMEM((GW, SQ), F32)]
        + (_exchange_sems(nr) if ride else []),
    )(m, kb_all, vt_all, *(ride.arrays if ride else []))


def _attn_bwd_t(m, kb_all, vb_all, kt_all, runs, dy, ride=None):
    S = m.shape[0]
    SQ = min(QT, S)
    nkc, nqb, nh, nq = S // KB, SQ // SB, HEADS // HPS, S // SQ
    nr = ride.n if ride else 0

    def body(*refs):
        q_ref, kb_hbm, vb_hbm, kt_hbm, r_ref, dy_ref = refs[:6]
        ride_in, (dq_ref, dk_hbm, dv_hbm), ride_out = refs[6:6 + nr], refs[6 + nr:9 + nr], refs[9 + nr:9 + 2 * nr]
        kb, vb, kt, dqt, dka, dva = refs[9 + 2 * nr:15 + 2 * nr]
        sems = refs[15 + 2 * nr:]
        h, i = pl.program_id(0), pl.program_id(1)
        cols = pl.ds(pl.multiple_of(h * GW, GW), GW)
        if ride:
            @pl.when(jnp.logical_and(h == 0, i == 0))
            def _():
                ride.run("start", ride_in, ride_out, sems)

        @pl.when(i == 0)
        def _():
            pltpu.sync_copy(kb_hbm.at[:, cols], kb)
            pltpu.sync_copy(vb_hbm.at[:, cols], vb)
            pltpu.sync_copy(kt_hbm.at[cols, :], kt)
            dka[...] = jnp.zeros_like(dka)
            dva[...] = jnp.zeros_like(dva)

        q8 = q_ref[...] * 0.125
        dyf = dy_ref[...]
        q8t, dyt = (q8 * LOG2E).T, dyf.T
        groups = [slice(g * KB, (g + 1) * KB) for g in range(HPS // 2)]
        qtm = [t for s in groups for t in _head_halves(q8t[s], 0)]
        dytm = [t for s in groups for t in _head_halves(dyt[s], 0)]
        qlm = [t for s in groups for t in _head_halves(q8[:, s], 1)]
        dylm = [t for s in groups for t in _head_halves(dyf[:, s], 1)]
        dmasks = _diag_masks(SQ)
        ri, ci = lax.broadcasted_iota(I32, (KB, KB), 0), lax.broadcasted_iota(I32, (KB, KB), 1)
        upper, lower = (ci >= ri).astype(BF), (ci <= ri).astype(BF)
        rev2 = jnp.concatenate([upper, upper], axis=1)
        dqt[...] = jnp.zeros_like(dqt)

        def block(jb, pres, dmask):
            masked = dmask is not None
            off = pl.multiple_of(jb * SB, SB)
            heads = range(HPS)
            kblk = [kb[pl.ds(off, SB), s] for s in groups]
            vblk = [vb[pl.ds(off, SB), s] for s in groups]
            ktb = [kt[s, pl.ds(off, SB)] for s in groups]
            run0s = [r_ref[hh, pl.ds(NKC * jb, 1), :] for hh in heads]
            run1s = [r_ref[hh, pl.ds(NKC * jb + 1, 1), :] for hh in heads]
            old_dq, old_dk, old_dv = dqt[...], dka[pl.ds(off, SB), :], dva[pl.ds(off, SB), :]
            dks, dvs, parts, new = [], [], [], []
            for g in range(HPS // 2):
                hs = (2 * g, 2 * g + 1)
                zs = [jnp.dot(kblk[g], qtm[hh], preferred_element_type=F32) for hh in hs]
                das = [jnp.dot(vblk[g], dytm[hh], preferred_element_type=F32) for hh in hs]
                sps = [_softplus2(z) for z in zs]
                sigs = [jnp.exp2(z - sp) for z, sp in zip(zs, sps)]
                if masked:
                    sps = [jnp.where(dmask, sp, 0.0) for sp in sps]
                css = [_chunk_sums(rev2, sp) for sp in sps]
                aa = [jnp.exp2(z - jnp.concatenate([run0s[hh] + cs[0], run1s[hh] + cs[1]], axis=0)) for z, hh, cs in zip(zs, hs, css)]
                if masked:
                    aa = [jnp.where(dmask, a, 0.0) for a in aa]
                gs = [a * da for a, da in zip(aa, das)]
                pgs = [[jnp.dot(lower, gg[c * KB:(c + 1) * KB].astype(BF), preferred_element_type=F32) for c in range(NKC)] for gg in gs]
                pre1s = [pres[hh] + pg[0][KB - 1:KB, :] for hh, pg in zip(hs, pgs)]
                dzs = [gg - sig * jnp.concatenate([pres[hh] + pg[0], pre1 + pg[1]], axis=0)
                       for gg, sig, hh, pre1, pg in zip(gs, sigs, hs, pre1s, pgs)]
                if masked:
                    dzs = [jnp.where(dmask, dz, 0.0) for dz in dzs]
                dzb, ab = [dz.astype(BF) for dz in dzs], [a.astype(BF) for a in aa]
                dks.append(sum(jnp.dot(dzb[t], qlm[hh], preferred_element_type=F32) for t, hh in enumerate(hs)))
                dvs.append(sum(jnp.dot(ab[t], dylm[hh], preferred_element_type=F32) for t, hh in enumerate(hs)))
                parts += [jnp.dot(ktb[g], dzb[t], preferred_element_type=F32)[t * HD:(t + 1) * HD, :] for t in range(2)]
                new += [pre1 + pg[1][KB - 1:KB, :] for pre1, pg in zip(pre1s, pgs)]
            dqt[...] = old_dq + jnp.concatenate(parts, axis=0)
            dka[pl.ds(off, SB), :] = old_dk + jnp.concatenate(dks, axis=1)
            dva[pl.ds(off, SB), :] = old_dv + jnp.concatenate(dvs, axis=1)
            return tuple(new)

        pres = lax.fori_loop(0, i * nqb, lambda jb, c: block(jb, c, None), (jnp.zeros((1, SQ), F32),) * HPS)
        for d in range(nqb):
            pres = block(i * nqb + d, pres, dmasks[d])
        dq_ref[...] = (dqt[...].T * 0.125).astype(dq_ref.dtype)

        @pl.when(i == nq - 1)
        def _():
            pltpu.sync_copy(dka, dk_hbm.at[:, cols])
            pltpu.sync_copy(dva, dv_hbm.at[:, cols])

        if ride:
            @pl.when(jnp.logical_and(h == nh - 1, i == nq - 1))
            def _():
                ride.run("wait", ride_in, ride_out, sems)

    n = HEADS * HD
    return pl.pallas_call(
        body, name="attn_bwd", grid=(nh, nq),
        in_specs=[pl.BlockSpec((SQ, GW), lambda h, i: (i, h)), ANY, ANY, ANY,
                  pl.BlockSpec((HPS, nkc, SQ), lambda h, i: (h, 0, i)), pl.BlockSpec((SQ, GW), lambda h, i: (i, h))] + [ANY] * nr,
        out_specs=[pl.BlockSpec((SQ, GW), lambda h, i: (i, h)), ANY, ANY] + [ANY] * nr,
        out_shape=[jax.ShapeDtypeStruct((S, n), BF), jax.ShapeDtypeStruct((S, n), F32), jax.ShapeDtypeStruct((S, n), F32)]
        + (ride.out_shape if ride else []),
        scratch_shapes=[pltpu.VMEM((S, GW), BF), pltpu.VMEM((S, GW), BF), pltpu.VMEM((GW, S), BF), pltpu.VMEM((GW, SQ), F32),
                        pltpu.VMEM((S, GW), F32), pltpu.VMEM((S, GW), F32)] + (_exchange_sems(nr) if ride else []),
    )(m, kb_all, vb_all, kt_all, runs, dy, *(ride.arrays if ride else []))


CW = 3 * HEADS * HD


def _shift_down(cur, prev8, s):
    if s == 0:
        return cur
    r = pltpu.roll(cur, s, 0)
    first = jnp.where(lax.broadcasted_iota(I32, (8, cur.shape[1]), 0) < s, pltpu.roll(prev8, s, 0), r[:8])
    return jnp.concatenate([first, r[8:]], axis=0)


def _shift_up(cur, next8, s):
    if s == 0:
        return cur
    n = cur.shape[0]
    r = pltpu.roll(cur, n - s, 0)
    last = jnp.where(lax.broadcasted_iota(I32, (8, cur.shape[1]), 0) >= 8 - s, pltpu.roll(next8, 8 - s, 0), r[n - 8:])
    return jnp.concatenate([r[:n - 8], last], axis=0)


def _conv_fwd(m, conv_w):
    S = m.shape[0]
    tm = min(512, S)
    hb = tm // 8

    def body(x_ref, p_ref, w_ref, o_ref):
        i = pl.program_id(0)
        cur = x_ref[...]
        prev = jnp.where(i > 0, p_ref[...], 0.0)
        w = w_ref[...]
        acc = cur * w[3:4]
        for jk in range(3):
            acc = acc + _shift_down(cur, prev, 3 - jk) * w[jk:jk + 1]
        o_ref[...] = acc

    return pl.pallas_call(
        body, name="conv_fwd", grid=(S // tm,),
        in_specs=[pl.BlockSpec((tm, CW), lambda i: (i, 1)), pl.BlockSpec((8, CW), lambda i: (jnp.maximum(i * hb - 1, 0), 1)),
                  pl.BlockSpec((4, CW), lambda i: (0, 0))],
        out_specs=pl.BlockSpec((tm, CW), lambda i: (i, 0)),
        out_shape=jax.ShapeDtypeStruct((S, CW), F32),
    )(m, m, conv_w)


def _conv_bwd(m, dyc, conv_w):
    S = m.shape[0]
    tm = min(512, S)
    hb = tm // 8
    nt = S // tm

    def body(x_ref, p_ref, d_ref, n_ref, w_ref, dx_ref, dw_ref):
        i = pl.program_id(0)
        cur = x_ref[...]
        prev = jnp.where(i > 0, p_ref[...], 0.0)
        d = d_ref[...]
        nxt = jnp.where(i < nt - 1, n_ref[...], 0.0)
        w = w_ref[...]
        acc = d * w[3:4]
        dws = []
        for jk in range(3):
            acc = acc + _shift_up(d, nxt, 3 - jk) * w[jk:jk + 1]
            dws.append(jnp.sum(d * _shift_down(cur, prev, 3 - jk), axis=0, keepdims=True))
        dws.append(jnp.sum(d * cur, axis=0, keepdims=True))
        dx_ref[...] = acc.astype(dx_ref.dtype)
        dw = jnp.concatenate(dws, axis=0)

        @pl.when(i == 0)
        def _():
            dw_ref[...] = dw

        @pl.when(i > 0)
        def _():
            dw_ref[...] += dw

    return pl.pallas_call(
        body, name="conv_bwd", grid=(nt,),
        in_specs=[pl.BlockSpec((tm, CW), lambda i: (i, 1)), pl.BlockSpec((8, CW), lambda i: (jnp.maximum(i * hb - 1, 0), 1)),
                  pl.BlockSpec((tm, CW), lambda i: (i, 0)),
                  pl.BlockSpec((8, CW), lambda i: (jnp.minimum((i + 1) * hb, S // 8 - 1), 0)),
                  pl.BlockSpec((4, CW), lambda i: (0, 0))],
        out_specs=[pl.BlockSpec((tm, CW), lambda i: (i, 0)), pl.BlockSpec((4, CW), lambda i: (0, 0))],
        out_shape=[jax.ShapeDtypeStruct((S, CW), BF), jax.ShapeDtypeStruct((4, CW), F32)],
    )(m, m, dyc, dyc, conv_w)


def _t(a):
    return jnp.swapaxes(a, 1, 2)


def _bdot(a, b):
    return jnp.einsum("hik,hkj->hij", a, b, preferred_element_type=F32)


@jax.custom_vjp
def _mm1(a, b):
    return _bdot(a.astype(BF), b.astype(BF))


def _bdot_nt(a, b):
    return jnp.einsum("hij,hkj->hik", a, b, preferred_element_type=F32)


def _bdot_tn(a, b):
    return jnp.einsum("hki,hkj->hij", a, b, preferred_element_type=F32)


_mm1.defvjp(lambda a, b: (_mm1(a, b), (a.astype(BF), b.astype(BF))),
            lambda res, dc: (_bdot_nt(dc.astype(BF), res[1]), _bdot_tn(res[0], dc.astype(BF))))


@jax.custom_vjp
def _mm1_nt(a, b):
    return _bdot_nt(a.astype(BF), b.astype(BF))


_mm1_nt.defvjp(lambda a, b: (_mm1_nt(a, b), (a.astype(BF), b.astype(BF))),
               lambda res, dc: (_bdot(dc.astype(BF), res[1]), _bdot_tn(dc.astype(BF), res[0])))


@jax.custom_vjp
def _mm1_tn(a, b):
    return _bdot_tn(a.astype(BF), b.astype(BF))


_mm1_tn.defvjp(lambda a, b: (_mm1_tn(a, b), (a.astype(BF), b.astype(BF))),
               lambda res, dc: (_bdot_nt(res[1], dc.astype(BF)), _bdot(res[0], dc.astype(BF))))


def _stack_rows(hi, lo):
    return jnp.concatenate([hi, lo], axis=1)


@jax.custom_vjp
def _mm3(a, b):
    (ah, al), (bh, bl) = _split(a), _split(b)
    n = a.shape[1]
    two = _bdot(_stack_rows(ah, al), bh)
    return two[:, :n] + two[:, n:] + _bdot(ah, bl)


def _mm3_fwd(a, b):
    return _mm3(a, b), (_split(a), _split(b))


def _mm3_bwd(res, dc):
    (ah, al), (bh, bl) = res
    dh, dl = _split(dc)
    n = dc.shape[1]
    two = _bdot_nt(_stack_rows(dh, dl), bh)
    da = two[:, :n] + two[:, n:] + _bdot_nt(dh, bl)
    db = _bdot_tn(jnp.concatenate([ah, ah, al], axis=1), jnp.concatenate([dh, dl, dh], axis=1))
    return da, db


_mm3.defvjp(_mm3_fwd, _mm3_bwd)


def _mm_exact(c3, b):
    hi, lo = _split(b)
    lo2 = (b - hi.astype(F32) - lo.astype(F32)).astype(BF)
    return _bdot(c3, jnp.concatenate([hi, lo, lo2], axis=-2))


@jax.custom_vjp
def _cumsum_rows(b):
    return _mm_exact(_tri3(True), b)


def _tri3(lower):
    ri = lax.broadcasted_iota(I32, (HEADS, CH, CH), 1)
    ci = lax.broadcasted_iota(I32, (HEADS, CH, CH), 2)
    tri = (ri >= ci if lower else ri <= ci).astype(BF)
    return jnp.concatenate([tri, tri, tri], axis=-1)


_cumsum_rows.defvjp(lambda b: (_cumsum_rows(b), None), lambda _, dc: (_mm_exact(_tri3(False), dc),))


CPS = 4


def _gdn_chunk(yc, gz, gba, alog, dtb):
    def heads(t, off):
        return jnp.stack([t[:, off + h * HD: off + (h + 1) * HD] for h in range(HEADS)])

    def cols(t, off):
        return jnp.stack([jnp.broadcast_to(t[:, off + h: off + h + 1], (CH, CH)) for h in range(HEADS)])

    c = yc * _sigmoid(yc)
    q, k, v, zg = heads(c, 0), heads(c, HEADS * HD), heads(c, 2 * HEADS * HD), heads(gz, 0)
    q = q * lax.rsqrt(jnp.sum(q * q, axis=-1, keepdims=True) + RMS_EPS) * (HD ** -0.5)
    k = k * lax.rsqrt(jnp.sum(k * k, axis=-1, keepdims=True) + RMS_EPS)
    beta = cols(_sigmoid(gba), 0)
    g = cols(-jnp.exp(alog) * _softplus(gba + dtb), HEADS)
    ri = lax.broadcasted_iota(I32, (HEADS, CH, CH), 1)
    ci = lax.broadcasted_iota(I32, (HEADS, CH, CH), 2)
    causal, strict = ri >= ci, ri > ci
    eye = (ri == ci).astype(F32)
    gc = _cumsum_rows(g)
    gr = _t(gc)
    decay = jnp.where(causal, jnp.exp(jnp.where(causal, gc - gr, 0.0)), 0.0)
    lower = jnp.where(strict, beta * _mm1_nt(k, k) * decay, 0.0)
    pw = -lower
    inv = eye + pw
    pw = _mm3(pw, pw)
    for _ in range(4):
        both = _mm3(jnp.concatenate([inv, pw], axis=1), pw)
        inv, pw = inv + both[:, :CH], both[:, CH:]
    inv = inv + _mm3(inv, pw)
    eg = jnp.exp(gc)
    uw = _mm3(inv, jnp.concatenate([v * beta, k * (beta * eg)], axis=2))
    u, w = uw[:, :, :HD], uw[:, :, HD:]
    qk = jnp.where(causal, _mm1_nt(q, k) * decay, 0.0)
    g_last = gc[:, CH - 1:CH, :]
    return u, jnp.concatenate([w, q * eg], axis=1), qk, k * jnp.exp(g_last - gc), jnp.exp(g_last), zg * _sigmoid(zg)


def _gdn_advance(state, pre, normw):
    u, wq, qk, kd, last, gate = pre
    ws = _mm1(wq, state)
    v_new = u - ws[:, :CH]
    o = ws[:, CH:] + _mm1(qk, v_new)
    new_state = state * last + _mm1_tn(kd, v_new)
    o = o * lax.rsqrt(jnp.mean(o * o, axis=-1, keepdims=True) + RMS_EPS) * normw * gate
    return jnp.concatenate([o[h] for h in range(HEADS)], axis=1), new_state


def _gdn_chunks(state, yc, gz, gba, alog, dtb, normw):
    rows = [slice(c * CH, (c + 1) * CH) for c in range(yc.shape[0] // CH)]
    pres = [_gdn_chunk(yc[r], gz[r], gba[r], alog, dtb) for r in rows]
    outs = []
    for pre in pres:
        o, state = _gdn_advance(state, pre, normw)
        outs.append(o)
    return jnp.concatenate(outs, axis=0), state


def _gdn_fwd(yc, m, alog, dtb, normw):
    S = yc.shape[0]
    RS = CPS * CH
    nch = S // RS

    def body(y_ref, gz_ref, gba_ref, al_ref, dt_ref, nw_ref, o_ref, st_ref, st):
        @pl.when(pl.program_id(0) == 0)
        def _():
            st[...] = jnp.zeros_like(st)

        cur = st[...]
        st_ref[0] = cur
        o, new = _gdn_chunks(cur, y_ref[...], gz_ref[...], gba_ref[...], al_ref[...], dt_ref[...], nw_ref[...])
        o_ref[...] = o
        st[...] = new

    return pl.pallas_call(
        body, name="gdn_fwd", grid=(nch,),
        in_specs=[pl.BlockSpec((RS, CW), lambda n: (n, 0)), pl.BlockSpec((RS, HEADS * HD), lambda n: (n, 6)),
                  pl.BlockSpec((RS, 128), lambda n: (n, 28)), pl.BlockSpec((1, 128), lambda n: (0, 0)),
                  pl.BlockSpec((1, 128), lambda n: (0, 0)), pl.BlockSpec((1, HD), lambda n: (0, 0))],
        out_specs=[pl.BlockSpec((RS, HEADS * HD), lambda n: (n, 0)), pl.BlockSpec((1, HEADS, HD, HD), lambda n: (n, 0, 0, 0))],
        out_shape=[jax.ShapeDtypeStruct((S, HEADS * HD), F32), jax.ShapeDtypeStruct((nch, HEADS, HD, HD), F32)],
        scratch_shapes=[pltpu.VMEM((HEADS, HD, HD), F32)],
    )(yc, m, m, alog, dtb, normw)


def _gdn_bwd(yc, m, alog, dtb, normw, states, dog):
    S = yc.shape[0]
    RS = CPS * CH
    nch = S // RS

    def body(y_ref, gz_ref, gba_ref, al_ref, dt_ref, nw_ref, st_ref, do_ref, dy_ref, dgz_ref, dgba_ref, dal_ref, ddt_ref, dnw_ref, dst):
        n = pl.program_id(0)

        @pl.when(n == 0)
        def _():
            dst[...] = jnp.zeros_like(dst)

        _, vjp = jax.vjp(_gdn_chunks, st_ref[0], y_ref[...], gz_ref[...], gba_ref[...], al_ref[...], dt_ref[...], nw_ref[...])
        d_state, d_y, d_gz, d_gba, d_al, d_dt, d_nw = vjp((do_ref[...], dst[...]))
        dst[...] = d_state
        dy_ref[...] = d_y
        dgz_ref[...] = d_gz.astype(dgz_ref.dtype)
        dgba_ref[...] = d_gba.astype(dgba_ref.dtype)
        for ref, val in ((dal_ref, d_al), (ddt_ref, d_dt), (dnw_ref, d_nw)):
            @pl.when(n == 0)
            def _(ref=ref, val=val):
                ref[...] = val

            @pl.when(n > 0)
            def _(ref=ref, val=val):
                ref[...] += val

    rev = lambda n: nch - 1 - n
    return pl.pallas_call(
        body, name="gdn_bwd", grid=(nch,),
        in_specs=[pl.BlockSpec((RS, CW), lambda n: (rev(n), 0)), pl.BlockSpec((RS, HEADS * HD), lambda n: (rev(n), 6)),
                  pl.BlockSpec((RS, 128), lambda n: (rev(n), 28)), pl.BlockSpec((1, 128), lambda n: (0, 0)),
                  pl.BlockSpec((1, 128), lambda n: (0, 0)), pl.BlockSpec((1, HD), lambda n: (0, 0)),
                  pl.BlockSpec((1, HEADS, HD, HD), lambda n: (rev(n), 0, 0, 0)),
                  pl.BlockSpec((RS, HEADS * HD), lambda n: (rev(n), 0))],
        out_specs=[pl.BlockSpec((RS, CW), lambda n: (rev(n), 0)), pl.BlockSpec((RS, HEADS * HD), lambda n: (rev(n), 0)),
                   pl.BlockSpec((RS, 128), lambda n: (rev(n), 0)), pl.BlockSpec((1, 128), lambda n: (0, 0)),
                   pl.BlockSpec((1, 128), lambda n: (0, 0)), pl.BlockSpec((1, HD), lambda n: (0, 0))],
        out_shape=[jax.ShapeDtypeStruct((S, CW), F32), jax.ShapeDtypeStruct((S, HEADS * HD), BF),
                   jax.ShapeDtypeStruct((S, 128), BF), jax.ShapeDtypeStruct((1, 128), F32),
                   jax.ShapeDtypeStruct((1, 128), F32), jax.ShapeDtypeStruct((1, HD), F32)],
        scratch_shapes=[pltpu.VMEM((HEADS, HD, HD), F32)],
    )(yc, m, m, alog, dtb, normw, states, dog)


def _mixer_fwd(h1, W, late=None):
    S = h1.shape[0]
    tm = min(512, S)
    n = HEADS * HD
    tb_ = min(BIG_ROWS, S)
    m, = _mm("mix_in", h1, W["wp"], S, NP, D, tm=tb_, tn=1536, tk=D, order="ji", outs=[((S, NP), F32, (tb_, 1536), _tile)])
    kb_all, vb_all, kt_all, vt_all = _attn_prep(m)
    ya, runs, *arrived = _attn_fwd_t(m, kb_all, vt_all, ride=late[0] if late else None)
    if late:
        late[1](W, arrived)
    runs = (kb_all, vb_all, kt_all, runs)
    b_gate, conv_w, alog, dtb, normw, w_sb, w_gdn, w_mo, g, b = (
        W[k] for k in ("b_gate", "conv_w", "alog", "dtb", "normw", "w_sb", "w_gdn", "w_mo", "ln2_g", "ln2_b"))
    yc = _conv_fwd(m, conv_w)
    og, states = _gdn_fwd(yc, m, alog, dtb, normw)
    ysb, = _mm("mix_sb", ya, w_sb, S, D, n, tm=tm, tn=D, tk=n, outs=[((S, D), F32, (tm, D), _row)])

    def merge_epi(acc, ys, gs, gg, bg):
        return _sigmoid(gs + bg[:, :D]) * ys + _sigmoid(gg + bg[:, D:]) * acc, acc

    u, ygdn = _mm("mix_gdn", og, w_gdn, S, D, n, tm=tm, tn=D, tk=n,
                  extras=[(ysb, (tm, D), _row), (m, (tm, D), lambda i, j, k: (i, GATE0 // D)),
                          (m, (tm, D), lambda i, j, k: (i, GATE0 // D + 1)), (b_gate, (1, 2 * D), _const)],
                  epilogue=merge_epi, outs=[((S, D), BF, (tm, D), _row), ((S, D), F32, (tm, D), _row)])

    def out_epi(acc, xin, gg, bb):
        r = ALPHA * xin + acc
        return _layer_norm(r, gg, bb), r

    h2, r2 = _mm("mix_out", u, w_mo, S, D, D, tm=tm, tn=D, tk=D,
                 extras=[(h1, (tm, D), _row), (g, (1, D), _const), (b, (1, D), _const)], epilogue=out_epi,
                 outs=[((S, D), F32, (tm, D), _row), ((S, D), F32, (tm, D), _row)])
    return h2, (h1, m, ya, runs, yc, og, states, ysb, ygdn, u, r2)


def _mixer_bwd(saved, W, dh, ride=None):
    h1, m, ya, runs, yc, og, states, ysb, ygdn, u, r2 = saved
    wp, b_gate, conv_w, alog, dtb, normw, w_sb, w_gdn, w_mo, g = (
        W[k] for k in ("wp", "b_gate", "conv_w", "alog", "dtb", "normw", "w_sb", "w_gdn", "w_mo", "ln2_g"))
    S = h1.shape[0]
    tm = min(512, S)
    n = HEADS * HD
    dr, dg, db = _ln_bwd("mix", r2, g, dh)

    def merge_epi(du, ys, yg, gs, gg, bg):
        s1, s2 = _sigmoid(gs + bg[:, :D]), _sigmoid(gg + bg[:, D:])
        dgate = jnp.concatenate([du * ys * s1 * (1.0 - s1), du * yg * s2 * (1.0 - s2)], axis=1)
        return du * s1, du * s2, dgate, jnp.sum(dgate, axis=0, keepdims=True)

    dysb, dygdn, dgate, d_bg = _mm(
        "mix_dmerge", dr, w_mo, S, D, D, tm=tm, tn=D, tk=D, tb=True,
        extras=[(ysb, (tm, D), _row), (ygdn, (tm, D), _row), (m, (tm, D), lambda i, j, k: (i, GATE0 // D)),
                (m, (tm, D), lambda i, j, k: (i, GATE0 // D + 1)), (b_gate, (1, 2 * D), _const)],
        epilogue=merge_epi, n_acc=1,
        outs=[((S, D), BF, (tm, D), _row), ((S, D), BF, (tm, D), _row), ((S, 2 * D), BF, (tm, 2 * D), _row),
              ((1, 2 * D), F32, (1, 2 * D), _const)])
    d_w_mo, = _mm("mix_dwmo", u, dr, D, D, S, tm=D, tn=D, tk=min(BIG_ROWS, S), ta=True, outs=[((D, D), F32, (D, D), _tile)])
    dya, = _mm("mix_dya", dysb, w_sb, S, n, D, tm=tm, tn=n, tk=D, tb=True, outs=[((S, n), F32, (tm, n), _row)])
    col_shards = [((NSH, n, D // NSH), F32, (None, n, D // NSH), lambda i, j, k: (j, 0, 0))]
    d_w_sb, = _mm("mix_dwsb", ya, dysb, n, D, S, tm=n, tn=D // NSH, tk=min(BIG_ROWS, S), ta=True, order="ji", outs=col_shards)
    dog, = _mm("mix_dog", dygdn, w_gdn, S, n, D, tm=tm, tn=n, tk=D, tb=True, outs=[((S, n), F32, (tm, n), _row)])
    d_w_gdn, = _mm("mix_dwgdn", og, dygdn, n, D, S, tm=n, tn=D // NSH, tk=min(BIG_ROWS, S), ta=True, order="ji", outs=col_shards)
    dq, dk, dv, *arrived = _attn_bwd_t(m, *runs, dya, ride=ride)
    dyc, dgz, dgba, d_alog, d_dtb, d_normw = _gdn_bwd(yc, m, alog, dtb, normw, states, dog)
    dxc, d_conv = _conv_bwd(m, dyc, conv_w)
    dm = jnp.concatenate([dq, dk.astype(BF), dv.astype(BF), dxc, dgz, dgba, jnp.zeros((S, GATE0 - 3712), BF), dgate], axis=1)
    tb_ = min(BIG_ROWS, S)
    d_h1, = _mm("mix_dh", dm, wp, S, D, NP, tm=tb_, tn=D, tk=1536, tb=True,
                extras=[(dr, (tb_, D), _row)], epilogue=lambda acc, d: (acc + ALPHA * d,),
                outs=[((S, D), F32, (tb_, D), _row)])
    d_wp, = _mm("mix_dwp", h1, dm, D, NP, S, tm=D, tn=1536, tk=tb_, ta=True, order="ji",
                outs=[((D, NP), F32, (D, 1536), _tile)])
    return d_h1, dict(wp=d_wp, b_gate=d_bg, conv_w=d_conv, alog=d_alog, dtb=d_dtb, normw=d_normw,
                      w_sb=d_w_sb, w_gdn=d_w_gdn, w_mo=d_w_mo, g=dg, b=db), arrived


def _ple_fwd(h3, p, w_pg, b_pg, w_ple, g, b, target):
    S = h3.shape[0]
    tm = min(512, S)
    pd = p.shape[1]
    pe, = _mm("ple_emb", p, w_ple, S, D, pd, tm=tm, tn=D, tk=pd, outs=[((S, D), F32, (tm, D), _row)])

    def epi(acc, e, xin, tgt, bp, gg, bb):
        gt = _sigmoid(acc + bp)
        r = ALPHA * xin + gt * e
        diff = _layer_norm(r, gg, bb) - tgt
        return gt, r, diff * (1.0 / D), jnp.sum(diff * diff, axis=0, keepdims=True)

    gt, r4, dh4, loss_row = _mm(
        "ple_gate", h3, w_pg, S, D, D, tm=tm, tn=D, tk=D,
        extras=[(pe, (tm, D), _row), (h3, (tm, D), _row), (target, (tm, D), _row), (b_pg, (1, D), _const),
                (g, (1, D), _const), (b, (1, D), _const)], epilogue=epi, n_acc=1,
        outs=[((S, D), F32, (tm, D), _row), ((S, D), F32, (tm, D), _row), ((S, D), F32, (tm, D), _row),
              ((1, D), F32, (1, D), _const)])
    return dh4, loss_row, (h3, p, pe, gt, r4)


def _ple_bwd(saved, w_pg, g, dh4):
    h3, p, pe, gt, r4 = saved
    S = h3.shape[0]
    tm = min(512, S)
    pd = p.shape[1]

    def fn(r, gg, dh, e, t):
        dr, dg, db = _layer_norm_bwd(r, gg, dh)
        dpre = dr * e * t * (1.0 - t)
        return dr, dpre, dr * t, dg, db, jnp.sum(dpre, axis=0, keepdims=True)

    row, one = (lambda i: (i, 0)), (lambda i: (0, 0))
    dr, dpre, dpe, dg, db, d_bpg = _rows(
        "ple_lnbwd", fn, S // tm,
        [(r4, (tm, D), row), (g, (1, D), one), (dh4, (tm, D), row), (pe, (tm, D), row), (gt, (tm, D), row)],
        [((S, D), F32, (tm, D), row), ((S, D), BF, (tm, D), row), ((S, D), BF, (tm, D), row),
         ((1, D), F32, (1, D), one), ((1, D), F32, (1, D), one), ((1, D), F32, (1, D), one)], n_acc=3)
    d_w_pg, = _mm("ple_dwpg", h3, dpre, D, D, S, tm=D, tn=D, tk=min(BIG_ROWS, S), ta=True, outs=[((D, D), F32, (D, D), _tile)])
    d_w_ple, = _mm("ple_dwple", p, dpe, pd, D, S, tm=pd, tn=D // NSH, tk=min(BIG_ROWS, S), ta=True, order="ji",
                   outs=[((NSH, pd, D // NSH), F32, (None, pd, D // NSH), lambda i, j, k: (j, 0, 0))])
    d_h3, = _mm("ple_dh", dpre, w_pg, S, D, D, tm=tm, tn=D, tk=D, tb=True,
                extras=[(dr, (tm, D), _row)], epilogue=lambda acc, d: (acc + ALPHA * d,),
                outs=[((S, D), F32, (tm, D), _row)])
    return d_h3, d_w_pg, d_bpg, d_w_ple, dg, db


def _local_step(x, p, target, W, now=None, soon=None, late=None, early=None, mid=None):
    W = dict(W)
    h1, sv1, arrived, W["ffn1_out"] = _ffn_fwd("ffn1", x, W["ffn1_in"], W.get("ffn1_out"), W["ln1_g"], W["ln1_b"],
                                               ride=soon[0] if soon else None, w_out_ride=now)
    if soon:
        soon[1](W, arrived)
    h2, sv2 = _mixer_fwd(h1, W, late)
    h3, sv3, _, _ = _ffn_fwd("ffn2", h2, W["ffn2_in"], W["ffn2_out"], W["ln3_g"], W["ln3_b"])
    dh4, loss_row, sv4 = _ple_fwd(h3, p, W["w_pg"], W["b_pg"], W["w_ple"], W["ln4_g"], W["ln4_b"], target)
    G = {}
    dh3, G["w_pg"], G["b_pg"], G["w_ple"], G["ln4_g"], G["ln4_b"] = _ple_bwd(sv4, W["w_pg"], W["ln4_g"], dh4)
    dh2, G["ffn2_in"], G["ffn2_out"], G["ln3_g"], G["ln3_b"], _ = _ffn_bwd("ffn2", sv3, W["ffn2_in"], W["ffn2_out"], W["ln3_g"], dh3)
    dh1, gm, G["early"] = _mixer_bwd(sv2, W, dh2, ride=early(G) if early else None)
    G.update({k: v for k, v in gm.items() if k not in ("g", "b")})
    G["ln2_g"], G["ln2_b"] = gm["g"], gm["b"]
    dx, G["ffn1_in"], G["ffn1_out"], G["ln1_g"], G["ln1_b"], G["mid"] = _ffn_bwd(
        "ffn1", sv1, W["ffn1_in"], W["ffn1_out"], W["ln1_g"], dh1, ride=mid(G) if mid else None)
    return loss_row, dx, G


S2 = CUT - 2 * MSH


def _pack_wp(w4):
    tr = 256

    def fn(w):
        s = [w[j].astype(F32) for j in range(NSH)]
        full = jnp.concatenate([s[0], s[1], s[2][:, :S2], jnp.zeros((tr, GATE0 - CUT), F32), s[2][:, S2:], s[3]], axis=1)
        return (full,)

    return _rows("pack_wp", fn, D // tr, [(w4, (NSH, tr, MSH), lambda i: (0, i, 0))],
                 [((D, NP), BF, (tr, NP), lambda i: (i, 0))])[0]


def _unpack_wp(d):
    tr = 256
    g2 = GATE0 + MSH - S2

    def fn(v):
        return (jnp.stack([v[:, :MSH], v[:, MSH:2 * MSH], jnp.concatenate([v[:, 2 * MSH:CUT], v[:, GATE0:g2]], axis=1), v[:, g2:]]),)

    return _rows("unpack_wp", fn, D // tr, [(d, (tr, NP), lambda i: (i, 0))],
                 [((NSH, D, MSH), F32, (NSH, tr, MSH), lambda i: (0, i, 0))])[0]


def _cast_bf16(tag, w):
    r, c = w.shape
    tr = _pick(r, 256)
    return _rows(f"cast_{tag}", lambda v: (v,), r // tr, [(w, (tr, c), lambda i: (i, 0))],
                 [((r, c), BF, (tr, c), lambda i: (i, 0))])[0]


def _place():
    return lax.axis_index("x"), lax.axis_index("y"), lax.axis_index("c")


def _chip_exchange(phase, scatter, ins, outs, send, recv, loc):
    x, y, c = _place()
    me = 2 * x + y
    chips = [(1 - x, y), (x, 1 - y), (1 - x, 1 - y)]
    for t in range(len(ins)):
        own = pltpu.make_async_copy(ins[t].at[me] if scatter else ins[t], outs[t].at[me], loc.at[t])
        out_going, in_coming = [], []
        for q, (px, py) in enumerate(chips):
            src = ins[t].at[2 * px + py] if scatter else ins[t]
            sems = dict(send_sem=send.at[3 * t + q], recv_sem=recv.at[3 * t + q], device_id=(px, py, c), device_id_type=MESH)
            out_going.append(pltpu.make_async_remote_copy(src_ref=src, dst_ref=outs[t].at[me], **sems))
            in_coming.append(pltpu.make_async_remote_copy(src_ref=src, dst_ref=outs[t].at[2 * px + py], **sems))
        if phase == "start":
            own.start()
            for cp in out_going:
                cp.start()
        else:
            for cp in in_coming:
                cp.wait_recv()
            own.wait()
            for cp in out_going:
                cp.wait_send()


def _exchange_sems(n):
    return [pltpu.SemaphoreType.DMA((3 * n,)), pltpu.SemaphoreType.DMA((3 * n,)), pltpu.SemaphoreType.DMA((n,))]


class _Ride:
    def __init__(self, scatter, arrays):
        self.scatter, self.arrays, self.n = scatter, list(arrays), len(arrays)
        self.out_shape = [jax.ShapeDtypeStruct(a.shape if scatter else (NSH,) + a.shape, a.dtype) for a in self.arrays]

    def run(self, phase, refs_in, refs_out, sems):
        _chip_exchange(phase, self.scatter, refs_in, refs_out, *sems)


def _gather_shards(shards):
    n = len(shards)

    def body(*refs):
        _chip_exchange("start", False, refs[:n], refs[n:2 * n], *refs[2 * n:])
        _chip_exchange("wait", False, refs[:n], refs[n:2 * n], *refs[2 * n:])

    return pl.pallas_call(
        body, name="gather_weights", in_specs=[ANY] * n, out_specs=[ANY] * n,
        out_shape=[jax.ShapeDtypeStruct((NSH,) + s.shape, s.dtype) for s in shards], scratch_shapes=_exchange_sems(n),
    )(*shards)


def _reduce_chips(ps):
    n = len(ps)

    def body(*refs):
        _chip_exchange("start", True, refs[:n], refs[n:2 * n], *refs[2 * n:])
        _chip_exchange("wait", True, refs[:n], refs[n:2 * n], *refs[2 * n:])

    return pl.pallas_call(
        body, name="reduce_chips", in_specs=[ANY] * n, out_specs=[ANY] * n,
        out_shape=[jax.ShapeDtypeStruct(p_.shape, p_.dtype) for p_ in ps], scratch_shapes=_exchange_sems(n),
    )(*ps)


def _chunk_rows(h, c):
    return _pick(h, max(8, 524288 // c // 8 * 8))


def _sibling_sum(tag, g):
    _, r, c = g.shape
    h = r // 2
    tr = _chunk_rows(h, c)
    nch = h // tr
    steps = NSH * nch

    def body(top_ref, bot_ref, out_ref, narrow_ref, land, send, recv):
        s = pl.program_id(0)
        x, y, core = _place()

        def exchange(keep_ref, give_ref):
            cp = pltpu.make_async_remote_copy(src_ref=give_ref.at[0], dst_ref=land.at[s], send_sem=send.at[s], recv_sem=recv.at[s],
                                              device_id=(x, y, 1 - core), device_id_type=MESH)
            cp.start()
            cp.wait()
            total = keep_ref[0] + land[s]
            out_ref[0] = total
            narrow_ref[0] = total.astype(BF)

        @pl.when(core == 0)
        def _():
            exchange(top_ref, bot_ref)

        @pl.when(core == 1)
        def _():
            exchange(bot_ref, top_ref)

    return pl.pallas_call(
        body, name=f"sibling_sum_{tag}", grid=(steps,),
        in_specs=[pl.BlockSpec((1, tr, c), lambda s: (s // nch, s % nch, 0)),
                  pl.BlockSpec((1, tr, c), lambda s: (s // nch, nch + s % nch, 0))],
        out_specs=[pl.BlockSpec((1, tr, c), lambda s: (s // nch, s % nch, 0))] * 2,
        out_shape=[jax.ShapeDtypeStruct((NSH, h, c), F32), jax.ShapeDtypeStruct((NSH, h, c), BF)],
        scratch_shapes=[pltpu.VMEM((steps, tr, c), F32), pltpu.SemaphoreType.DMA((steps,)), pltpu.SemaphoreType.DMA((steps,))],
    )(g, g)


def _chip_sum_share(tag, b, own):
    _, h, c = b.shape
    tr = _chunk_rows(h, c)
    steps = h // tr

    def body(b_ref, own_ref, out_ref, stage, land, send, recv):
        s = pl.program_id(0)
        x, y, core = _place()
        me = 2 * x + y
        v = [jnp.where(me == j, own_ref[j], b_ref[j].astype(F32)) for j in range(NSH)]
        total = ((v[0] + v[1]) + v[2]) + v[3]
        stage[...] = total
        cp = pltpu.make_async_remote_copy(src_ref=stage, dst_ref=land.at[s], send_sem=send.at[s], recv_sem=recv.at[s],
                                          device_id=(x, y, 1 - core), device_id_type=MESH)
        cp.start()
        cp.wait()
        out_ref[core] = total
        out_ref[1 - core] = land[s]

    return pl.pallas_call(
        body, name=f"chip_sum_share_{tag}", grid=(steps,),
        in_specs=[pl.BlockSpec((NSH, tr, c), lambda s: (0, s, 0))] * 2,
        out_specs=pl.BlockSpec((2, tr, c), lambda s: (0, s, 0)),
        out_shape=jax.ShapeDtypeStruct((2, h, c), F32),
        scratch_shapes=[pltpu.VMEM((tr, c), F32), pltpu.VMEM((steps, tr, c), F32), pltpu.SemaphoreType.DMA((steps,)),
                        pltpu.SemaphoreType.DMA((steps,))],
    )(b, own)


NDEV = 8


def _allreduce_small(pack):
    r, w = pack.shape
    rel = [(dx, dy, dc) for dx in (0, 1) for dy in (0, 1) for dc in (0, 1) if (dx, dy, dc) != (0, 0, 0)]

    def body(in_ref, out_ref, buf, send, recv):
        x, y, c = _place()
        me = 4 * x + 2 * y + c
        buf[me] = in_ref[...]
        peers = [((x + dx) % 2, (y + dy) % 2, (c + dc) % 2) for dx, dy, dc in rel]
        sent = []
        for k, peer in enumerate(peers):
            cp = pltpu.make_async_remote_copy(src_ref=in_ref, dst_ref=buf.at[me], send_sem=send.at[k], recv_sem=recv.at[k],
                                              device_id=peer, device_id_type=MESH)
            cp.start()
            sent.append(cp)
        for k, (px, py, pc) in enumerate(peers):
            pltpu.make_async_remote_copy(src_ref=in_ref, dst_ref=buf.at[4 * px + 2 * py + pc], send_sem=send.at[k], recv_sem=recv.at[k],
                                         device_id=(px, py, pc), device_id_type=MESH).wait_recv()
        for cp in sent:
            cp.wait_send()
        acc = buf[0]
        for k in range(1, NDEV):
            acc = acc + buf[k]
        out_ref[...] = acc

    vm = pl.BlockSpec(memory_space=pltpu.VMEM)
    return pl.pallas_call(
        body, name="allreduce_small", in_specs=[vm], out_specs=vm, out_shape=jax.ShapeDtypeStruct((r, w), F32),
        scratch_shapes=[pltpu.VMEM((NDEV, r, w), F32), pltpu.SemaphoreType.DMA((NDEV - 1,)), pltpu.SemaphoreType.DMA((NDEV - 1,))],
    )(pack)


def _adamw(tag, w, g, m, v):
    r, c = w.shape
    tr = _pick(r, max(8, 262144 // c // 8 * 8))

    def fn(w_, g_, m_, v_):
        m2 = B1 * m_ + (1.0 - B1) * g_
        v2 = B2 * v_ + (1.0 - B2) * (g_ * g_)
        m_hat = m2 / (1.0 - B1 ** STEP)
        v_hat = v2 / (1.0 - B2 ** STEP)
        return -LR * (m_hat / (jnp.sqrt(v_hat) + EPS) + WD * w_), m2, v2

    spec = ((tr, c), lambda i: (i, 0))
    return _rows(f"adamw_{tag}", fn, r // tr, [(a,) + spec for a in (w, g, m, v)], [((r, c), F32) + spec] * 3)


BIG = ("ffn1_w_in", "ffn1_w_out", "w_mix_in", "w_branch_sb", "w_branch_gdn", "w_mix_out", "ffn2_w_in", "ffn2_w_out",
       "w_ple_gate", "w_ple")
FIRST = ("ffn1_w_in",)
SOON = ("w_mix_in",)
LATER = tuple(n for n in BIG if n not in FIRST + SOON + ("ffn1_w_out",))
EARLY = ("ffn2_w_in", "ffn2_w_out", "w_ple_gate", "w_ple")
MID = ("w_mix_in", "w_branch_sb", "w_branch_gdn", "w_mix_out")
LAST = ("ffn1_w_in", "ffn1_w_out")
SMALL = ("ln1_g", "ln1_b", "b_gate", "conv_w", "a_log", "dt_bias", "gdn_norm_w", "ln2_g", "ln2_b", "ln3_g", "ln3_b",
         "b_ple_gate", "ln4_g", "ln4_b")
ORDER = ("ffn1_w_in", "ffn1_w_out", "ln1_g", "ln1_b", "w_mix_in", "b_gate", "conv_w", "a_log", "dt_bias", "gdn_norm_w",
         "w_branch_sb", "w_branch_gdn", "w_mix_out", "ln2_g", "ln2_b", "ffn2_w_in", "ffn2_w_out", "ln3_g", "ln3_b",
         "w_ple_gate", "b_ple_gate", "w_ple", "ln4_g", "ln4_b")
PACK_W = 2304


def _lane_row(v, lanes=128, at=HEADS):
    return jnp.pad(v[None, :], ((0, 0), (at, lanes - at - v.shape[0])))


def _col_join(w4):
    return jnp.transpose(w4, (1, 0, 2)).reshape(w4.shape[1], NSH * w4.shape[2])


def kernel(x, p, ffn1_w_in, ffn1_w_out, ln1_g, ln1_b, w_mix_in, b_gate, conv_w, a_log, dt_bias, gdn_norm_w, w_branch_sb, w_branch_gdn, w_mix_out, ln2_g, ln2_b, ffn2_w_in, ffn2_w_out, ln3_g, ln3_b, w_ple_gate, b_ple_gate, w_ple, ln4_g, ln4_b, loss_target, m_ffn1_w_in, m_ffn1_w_out, m_ln1_g, m_ln1_b, m_w_mix_in, m_b_gate, m_conv_w, m_a_log, m_dt_bias, m_gdn_norm_w, m_w_branch_sb, m_w_branch_gdn, m_w_mix_out, m_ln2_g, m_ln2_b, m_ffn2_w_in, m_ffn2_w_out, m_ln3_g, m_ln3_b, m_w_ple_gate, m_b_ple_gate, m_w_ple, m_ln4_g, m_ln4_b, v_ffn1_w_in, v_ffn1_w_out, v_ln1_g, v_ln1_b, v_w_mix_in, v_b_gate, v_conv_w, v_a_log, v_dt_bias, v_gdn_norm_w, v_w_branch_sb, v_w_branch_gdn, v_w_mix_out, v_ln2_g, v_ln2_b, v_ffn2_w_in, v_ffn2_w_out, v_ln3_g, v_ln3_b, v_w_ple_gate, v_b_ple_gate, v_w_ple, v_ln4_g, v_ln4_b):
    args = dict(locals())
    w = {n: args[n][0] for n in ORDER}
    mom = {n: args["m_" + n][0] for n in ORDER}
    var = {n: args["v_" + n][0] for n in ORDER}

    cast = {n: _cast_bf16(n, w[n]) for n in BIG}
    full = dict(zip(FIRST + ("conv_w",), _gather_shards([cast[n] for n in FIRST] + [w["conv_w"]])))
    W = dict(
        ffn1_in=full["ffn1_w_in"], conv_w=_col_join(full["conv_w"]), b_gate=w["b_gate"][None], alog=_lane_row(w["a_log"]), dtb=_lane_row(w["dt_bias"]),
        normw=w["gdn_norm_w"][None], b_pg=w["b_ple_gate"][None],
        **{f"ln{i}_{s}": w[f"ln{i}_{s}"][None] for i in (1, 2, 3, 4) for s in ("g", "b")},
    )

    def fill(W_, arrived):
        got = dict(zip(LATER, arrived))
        W_.update(w_sb=_col_join(got["w_branch_sb"]), w_gdn=_col_join(got["w_branch_gdn"]), w_mo=got["w_mix_out"].reshape(D, D),
                  ffn2_in=got["ffn2_w_in"], ffn2_out=got["ffn2_w_out"].reshape(DFF, D),
                  w_pg=got["w_ple_gate"].reshape(D, D), w_ple=_col_join(got["w_ple"]))

    def by_shard(G_, names):
        forms = dict(
            ffn1_w_in=lambda: G_["ffn1_in"], ffn1_w_out=lambda: G_["ffn1_out"].reshape(NSH, DFF // NSH, D),
            w_mix_in=lambda: _unpack_wp(G_["wp"]), w_branch_sb=lambda: G_["w_sb"], w_branch_gdn=lambda: G_["w_gdn"],
            w_mix_out=lambda: G_["w_mo"].reshape(NSH, D // NSH, D), ffn2_w_in=lambda: G_["ffn2_in"],
            ffn2_w_out=lambda: G_["ffn2_out"].reshape(NSH, DFF // NSH, D),
            w_ple_gate=lambda: G_["w_pg"].reshape(NSH, D // NSH, D), w_ple=lambda: G_["w_ple"])
        return [_sibling_sum(n, forms[n]()) for n in names]

    early_sums, mid_sums = [], []

    def early(G_):
        early_sums.extend(by_shard(G_, EARLY))
        return _Ride(True, [narrow for _, narrow in early_sums])

    def mid(G_):
        mid_sums.extend(by_shard(G_, MID))
        return _Ride(True, [narrow for _, narrow in mid_sums])

    loss_row, grad_x, G = _local_step(
        x[0], p[0, 0], loss_target[0], W, now=_Ride(False, [cast["ffn1_w_out"]]),
        soon=(_Ride(False, [cast[n] for n in SOON]), lambda W_, arrived: W_.update(wp=_pack_wp(arrived[0]))),
        late=(_Ride(False, [cast[n] for n in LATER]), fill), early=early, mid=mid)
    loss = lax.psum(0.5 * jnp.sum(loss_row) / D, ("x", "y", "c"))

    last_sums = by_shard(G, LAST)
    landed = list(G["early"]) + list(G["mid"]) + list(_reduce_chips([narrow for _, narrow in last_sums]))
    grad = {n: _chip_sum_share(n, b, own).reshape(w[n].shape)
            for n, b, (own, _) in zip(EARLY + MID + LAST, landed, early_sums + mid_sums + last_sums)}

    pieces = [G["ln1_g"], G["ln1_b"], G["b_gate"], G["conv_w"].reshape(1, 4 * CW), G["alog"], G["dtb"], G["normw"],
              G["ln2_g"], G["ln2_b"], G["ln3_g"], G["ln3_b"], G["b_pg"], G["ln4_g"], G["ln4_b"]]
    flat = jnp.concatenate(pieces, axis=1)
    flat = jnp.pad(flat, ((0, 0), (0, NDEV * PACK_W - flat.shape[1])))
    total = _allreduce_small(flat.reshape(NDEV, PACK_W)).reshape(1, NDEV * PACK_W)
    off = 0
    for n, piece in zip(SMALL, pieces):
        grad[n] = total[0, off:off + piece.shape[1]]
        off += piece.shape[1]
    chip = 2 * lax.axis_index("x") + lax.axis_index("y")
    grad["conv_w"] = lax.dynamic_slice_in_dim(grad["conv_w"].reshape(4, CW), chip * (CW // NSH), CW // NSH, axis=1)
    grad["a_log"] = grad["a_log"][HEADS:2 * HEADS]
    grad["dt_bias"] = grad["dt_bias"][HEADS:2 * HEADS]

    delta, new_m, new_v = {}, {}, {}
    for n in ORDER:
        shape2 = w[n].shape if w[n].ndim == 2 else (1, w[n].shape[0])
        d_, m_, v_ = _adamw(n, *[a.reshape(shape2) for a in (w[n], grad[n], mom[n], var[n])])
        delta[n], new_m[n], new_v[n] = (a.reshape(args[n].shape) for a in (d_, m_, v_))
    outs = [loss, grad_x[None]]
    outs += [grad[n].reshape(args[n].shape) for n in ORDER]
    for group in (delta, new_m, new_v):
        outs += [group[n] for n in ORDER]
    return tuple(outs)
```

```python
import jax
import jax.numpy as jnp
from jax import lax
from jax.experimental import pallas as pl
from jax.experimental.pallas import tpu as pltpu

F32 = jnp.float32
BF = jnp.bfloat16
I32 = jnp.int32
MESH = pl.DeviceIdType.MESH
ANY = pl.BlockSpec(memory_space=pl.ANY)

D = 1024
DFF = 2816
NSH = 4
FSH = 2 * DFF // NSH
NIN = 5648
MSH = NIN // NSH
NP = 6144
CUT = 3600
GATE0 = 4096
HEADS = 8
HD = 64
CH = 64
KB = 128
BIG_ROWS = 1024
ALPHA = 2.0 ** 0.25
LN_EPS = 1e-5
RMS_EPS = 1e-6
B1, B2, LR, EPS, WD, STEP = 0.9, 0.999, 0.001, 1e-08, 0.01, 10


def _sigmoid(x):
    return 0.5 * jnp.tanh(0.5 * x) + 0.5


def _softplus(x):
    return jnp.maximum(x, 0.0) + jnp.log1p(jnp.exp(-jnp.abs(x)))


def _layer_norm(r, g, b):
    mu = jnp.mean(r, axis=-1, keepdims=True)
    xc = r - mu
    var = jnp.mean(xc * xc, axis=-1, keepdims=True)
    return xc * lax.rsqrt(var + LN_EPS) * g + b


def _layer_norm_bwd(r, g, dh):
    mu = jnp.mean(r, axis=-1, keepdims=True)
    xc = r - mu
    var = jnp.mean(xc * xc, axis=-1, keepdims=True)
    xhat = xc * lax.rsqrt(var + LN_EPS)
    dxh = dh * g
    dr = lax.rsqrt(var + LN_EPS) * (dxh - jnp.mean(dxh, axis=-1, keepdims=True) - xhat * jnp.mean(dxh * xhat, axis=-1, keepdims=True))
    return dr, jnp.sum(dh * xhat, axis=0, keepdims=True), jnp.sum(dh, axis=0, keepdims=True)


def _pick(n, cap):
    if n <= cap:
        return n
    for t in range(cap - cap % 8, 7, -8):
        if n % t == 0:
            return t
    raise ValueError((n, cap))


def _mm(name, a, b, M, N, K, *, tm, tn, tk, ta=False, tb=False, a_spec=None, b_spec=None, order="ij",
        extras=(), epilogue=None, outs, n_acc=0, ride=None):
    ni, nj, nk = M // tm, N // tn, K // tk
    nr = ride.n if ride else 0
    assert M % tm == 0 and N % tn == 0 and K % tk == 0, (name, M, N, K, tm, tn, tk)
    assert n_acc == 0 or nj == 1

    def wrap(fn):
        if order == "ij":
            return lambda g0, g1, g2: fn(g0, g1, g2)
        return lambda g0, g1, g2: fn(g1, g0, g2)

    if a_spec is None:
        a_spec = ((tk, tm), lambda i, j, k: (k, i)) if ta else ((tm, tk), lambda i, j, k: (i, k))
    if b_spec is None:
        b_spec = ((tn, tk), lambda i, j, k: (j, k)) if tb else ((tk, tn), lambda i, j, k: (k, j))
    dims = (((0 if ta else 1,), (1 if tb else 0,)), ((), ()))
    ne, no = len(extras), len(outs)
    grid = (ni, nj, nk) if order == "ij" else (nj, ni, nk)

    def body(*refs):
        a_ref, b_ref = refs[0], refs[1]
        ex = refs[2:2 + ne]
        ride_in = refs[2 + ne:2 + ne + nr]
        o = refs[2 + ne + nr:2 + ne + nr + no]
        ride_out = refs[2 + ne + nr + no:2 + ne + 2 * nr + no]
        scratch = refs[2 + ne + 2 * nr + no:]
        g0, g1, k = pl.program_id(0), pl.program_id(1), pl.program_id(2)
        first = jnp.logical_and(g0 == 0, g1 == 0)
        if ride:
            @pl.when(jnp.logical_and(first, k == 0))
            def _():
                ride.run("start", ride_in, ride_out, scratch[-3:])

        p = lax.dot_general(a_ref[...].astype(BF), b_ref[...].astype(BF), dims, preferred_element_type=F32)

        def finish(acc):
            vals = (acc,) if epilogue is None else epilogue(acc, *[e[...] for e in ex])
            for idx, (ref, val) in enumerate(zip(o, vals)):
                if idx < no - n_acc:
                    ref[...] = val.astype(ref.dtype)
                else:
                    @pl.when(first)
                    def _(ref=ref, val=val):
                        ref[...] = val

                    @pl.when(jnp.logical_not(first))
                    def _(ref=ref, val=val):
                        ref[...] += val

        if nk == 1:
            finish(p)
        else:
            acc_ref = scratch[0]

            @pl.when(k == 0)
            def _():
                acc_ref[...] = p

            @pl.when(k > 0)
            def _():
                acc_ref[...] += p

            @pl.when(k == nk - 1)
            def _():
                finish(acc_ref[...])

        if ride:
            @pl.when(jnp.logical_and(jnp.logical_and(g0 == grid[0] - 1, g1 == grid[1] - 1), k == nk - 1))
            def _():
                ride.run("wait", ride_in, ride_out, scratch[-3:])

    in_specs = [pl.BlockSpec(a_spec[0], wrap(a_spec[1])), pl.BlockSpec(b_spec[0], wrap(b_spec[1]))]
    in_specs += [pl.BlockSpec(blk, wrap(fn)) for _, blk, fn in extras] + [ANY] * nr
    res = pl.pallas_call(
        body, name=name, grid=grid, in_specs=in_specs,
        out_specs=[pl.BlockSpec(blk, wrap(fn)) for _, _, blk, fn in outs] + [ANY] * nr,
        out_shape=[jax.ShapeDtypeStruct(shape, dt) for shape, dt, _, _ in outs] + (ride.out_shape if ride else []),
        scratch_shapes=([pltpu.VMEM((tm, tn), F32)] if nk > 1 else []) + (_exchange_sems(nr) if ride else []),
    )(a, b, *[e[0] for e in extras], *(ride.arrays if ride else []))
    return res


def _row(i, j, k):
    return (i, 0)


def _tile(i, j, k):
    return (i, j)


def _const(i, j, k):
    return (0, 0)


def _rows(name, fn, n_steps, ins, outs, n_acc=0):
    ni, no = len(ins), len(outs)

    def body(*refs):
        i = pl.program_id(0)
        vals = fn(*[r[...] for r in refs[:ni]])
        for idx, (ref, val) in enumerate(zip(refs[ni:ni + no], vals)):
            if idx < no - n_acc:
                ref[...] = val.astype(ref.dtype)
            else:
                @pl.when(i == 0)
                def _(ref=ref, val=val):
                    ref[...] = val

                @pl.when(i > 0)
                def _(ref=ref, val=val):
                    ref[...] += val

    return pl.pallas_call(
        body, name=name, grid=(n_steps,),
        in_specs=[pl.BlockSpec(blk, fn_) for _, blk, fn_ in ins],
        out_specs=[pl.BlockSpec(blk, fn_) for _, _, blk, fn_ in outs],
        out_shape=[jax.ShapeDtypeStruct(shape, dt) for shape, dt, _, _ in outs],
    )(*[a for a, _, _ in ins])


def _ffn_fwd(tag, x, w_in, w_out, g, b, ride=None, ride2=None, w_out_ride=None):
    S = x.shape[0]
    tm = min(BIG_ROWS, S)
    gate, *came = _mm(f"{tag}_gate", x, w_in, S, DFF, D, tm=tm, tn=FSH, tk=D, order="ji",
                      b_spec=((None, D, FSH), lambda i, j, k: (j, 0, 0)), ride=w_out_ride,
                      outs=[((S, DFF), F32, (tm, FSH), _tile)])
    if w_out_ride:
        w_out = came[0].reshape(DFF, D)

    def up_epi(acc, gt):
        return acc, gt * _sigmoid(gt) * acc

    up, s, *arrived = _mm(f"{tag}_up", x, w_in, S, DFF, D, tm=tm, tn=FSH, tk=D, order="ji",
                          b_spec=((None, D, FSH), lambda i, j, k: (j + 2, 0, 0)),
                          extras=[(gate, (tm, FSH), _tile)], epilogue=up_epi, ride=ride,
                          outs=[((S, DFF), F32, (tm, FSH), _tile), ((S, DFF), BF, (tm, FSH), _tile)])

    def out_epi(acc, xin, gg, bb):
        r = ALPHA * xin + 0.5 * acc
        return _layer_norm(r, gg, bb), r

    h, r, *more = _mm(f"{tag}_out", s, w_out, S, D, DFF, tm=tm, tn=D, tk=DFF,
                      extras=[(x, (tm, D), _row), (g, (1, D), _const), (b, (1, D), _const)], epilogue=out_epi, ride=ride2,
                      outs=[((S, D), F32, (tm, D), _row), ((S, D), F32, (tm, D), _row)])
    return h, (x, gate, up, s, r), arrived + more, w_out


def _ln_bwd(tag, r, g, dh):
    S = r.shape[0]
    tm = min(512, S)
    return _rows(f"{tag}_lnbwd", _layer_norm_bwd, S // tm,
                 [(r, (tm, D), lambda i: (i, 0)), (g, (1, D), lambda i: (0, 0)), (dh, (tm, D), lambda i: (i, 0))],
                 [((S, D), F32, (tm, D), lambda i: (i, 0)), ((1, D), F32, (1, D), lambda i: (0, 0)),
                  ((1, D), F32, (1, D), lambda i: (0, 0))], n_acc=2)


def _ffn_bwd(tag, saved, w_in, w_out, g, dh, ride=None):
    x, gate, up, s, r = saved
    S = x.shape[0]
    tm = min(BIG_ROWS, S)
    dr, dg, db = _ln_bwd(tag, r, g, dh)

    def act_epi(acc, gt, u):
        ds = 0.5 * acc
        sg = _sigmoid(gt)
        return (jnp.stack([ds * u * (sg * (1.0 + gt * (1.0 - sg))), ds * (gt * sg)]),)

    da, = _mm(f"{tag}_dact", dr, w_out, S, DFF, D, tm=tm, tn=FSH, tk=D, tb=True, order="ji",
              extras=[(gate, (tm, FSH), _tile), (up, (tm, FSH), _tile)], epilogue=act_epi,
              outs=[((2, S, DFF), BF, (2, tm, FSH), lambda i, j, k: (0, i, j))])
    d_w_out, = _mm(f"{tag}_dwout", s, dr, DFF, D, S, tm=FSH, tn=D, tk=tm, ta=True,
                   epilogue=lambda acc: (0.5 * acc,), outs=[((DFF, D), F32, (FSH, D), _tile)])
    tb_ = min(BIG_ROWS, S)
    d_in, *arrived = _mm(f"{tag}_dx", da, w_in, S, D, 2 * DFF, tm=tb_, tn=D, tk=FSH, tb=True,
                         a_spec=((None, tb_, FSH), lambda i, j, k: (k // 2, i, k % 2)),
                         b_spec=((None, D, FSH), lambda i, j, k: (k, 0, 0)),
                         extras=[(dr, (tb_, D), _row)], epilogue=lambda acc, d: (acc + ALPHA * d,), ride=ride,
                         outs=[((S, D), F32, (tb_, D), _row)])
    d_w_in, = _mm(f"{tag}_dwin", x, da, D, 2 * DFF, S, tm=D, tn=FSH, tk=tb_, ta=True, order="ji",
                  b_spec=((None, tb_, FSH), lambda i, j, k: (j // 2, k, j % 2)),
                  outs=[((NSH, D, FSH), F32, (None, D, FSH), lambda i, j, k: (j, 0, 0))])
    return d_in, d_w_in, d_w_out, dg, db, arrived


SB = 256
NKC = SB // KB
HPF = 8
HPS = 4
GW = HPS * HD
LOG2E = 1.4426950408889634


def _split(vals):
    hi = vals.astype(BF)
    return hi, (vals - hi.astype(F32)).astype(BF)


def _chunk_sums(tri2, vals):
    hi, lo = _split(vals)
    return [jnp.dot(tri2, jnp.concatenate([hi[c * KB:(c + 1) * KB], lo[c * KB:(c + 1) * KB]], axis=0), preferred_element_type=F32)
            for c in range(NKC)]


def _head_halves(t, axis):
    idx = lax.broadcasted_iota(I32, t.shape, axis)
    return [jnp.where(idx < HD, t, 0.0).astype(BF), jnp.where(idx >= HD, t, 0.0).astype(BF)]


QT = 512


def _diag_masks(sq):
    krow, qcol = lax.broadcasted_iota(I32, (SB, sq), 0), lax.broadcasted_iota(I32, (SB, sq), 1)
    return [krow + d * SB < qcol for d in range(sq // SB)]


def _softplus2(z):
    return jnp.maximum(z, 0.0) + jnp.log2(1.0 + jnp.exp2(jnp.minimum(z, -z)))


def _attn_prep(m):
    S = m.shape[0]
    tm = min(512, S)
    n = HEADS * HD
    return _rows("attn_prep", lambda k, v: (k, v, k.T, v.T), S // tm,
                 [(m, (tm, n), lambda i: (i, 1)), (m, (tm, n), lambda i: (i, 2))],
                 [((S, n), BF, (tm, n), lambda i: (i, 0)), ((S, n), BF, (tm, n), lambda i: (i, 0)),
                  ((n, S), BF, (n, tm), lambda i: (0, i)), ((n, S), BF, (n, tm), lambda i: (0, i))])


def _attn_fwd_t(m, kb_all, vt_all, ride=None):
    S = m.shape[0]
    SQ = min(QT, S)
    assert S % SQ == 0 and SQ % SB == 0
    HPS, GW = HPF, HPF * HD
    nkc, nqb, nh, nq = S // KB, SQ // SB, HEADS // HPS, S // SQ
    nr = ride.n if ride else 0

    def body(*refs):
        q_ref, kb_hbm, vt_hbm = refs[:3]
        ride_in, (o_ref, r_ref), ride_out = refs[3:3 + nr], refs[3 + nr:5 + nr], refs[5 + nr:5 + 2 * nr]
        kb, vt, acc = refs[5 + 2 * nr:8 + 2 * nr]
        sems = refs[8 + 2 * nr:]
        h, i = pl.program_id(0), pl.program_id(1)
        if ride:
            @pl.when(jnp.logical_and(h == 0, i == 0))
            def _():
                ride.run("start", ride_in, ride_out, sems)

        @pl.when(i == 0)
        def _():
            cols = pl.ds(pl.multiple_of(h * GW, GW), GW)
            pltpu.sync_copy(kb_hbm.at[:, cols], kb)
            pltpu.sync_copy(vt_hbm.at[cols, :], vt)

        qt = (q_ref[...] * (0.125 * LOG2E)).T
        qtm = [t for g in range(HPS // 2) for t in _head_halves(qt[g * KB:(g + 1) * KB], 0)]
        dmasks = _diag_masks(SQ)
        upper = (lax.broadcasted_iota(I32, (KB, KB), 1) >= lax.broadcasted_iota(I32, (KB, KB), 0)).astype(BF)
        tri2 = jnp.concatenate([upper, upper], axis=1)
        acc[...] = jnp.zeros_like(acc)
        r_ref[...] = jnp.zeros_like(r_ref)

        def block(jb, runs, dmask):
            masked = dmask is not None
            off = pl.multiple_of(jb * SB, SB)
            groups = [slice(g * KB, (g + 1) * KB) for g in range(HPS // 2)]
            kblk = [kb[pl.ds(off, SB), s] for s in groups]
            vtb = [vt[s, pl.ds(off, SB)] for s in groups]
            old = acc[...]
            zs = [jnp.dot(kblk[hh // 2], qtm[hh], preferred_element_type=F32) for hh in range(HPS)]
            sps = [_softplus2(z) for z in zs]
            if masked:
                sps = [jnp.where(dmask, sp, 0.0) for sp in sps]
            css = [_chunk_sums(tri2, sp) for sp in sps]
            run0s = [run + cs[1][0:1, :] for run, cs in zip(runs, css)]
            aa = [jnp.exp2(z - jnp.concatenate([run0 + cs[0], run + cs[1]], axis=0)) for z, run, run0, cs in zip(zs, runs, run0s, css)]
            if masked:
                aa = [jnp.where(dmask, a, 0.0) for a in aa]
            parts = [jnp.dot(vtb[hh // 2], aa[hh].astype(BF), preferred_element_type=F32) for hh in range(HPS)]
            upd = jnp.concatenate([parts[hh][(hh % 2) * HD:(hh % 2 + 1) * HD, :] for hh in range(HPS)], axis=0)
            for hh in range(HPS):
                r_ref[hh, pl.ds(NKC * jb, 1), :] = run0s[hh]
                r_ref[hh, pl.ds(NKC * jb + 1, 1), :] = runs[hh]
            acc[...] = old + upd
            return tuple(run0 + cs[0][0:1, :] for run0, cs in zip(run0s, css))

        runs = (jnp.zeros((1, SQ), F32),) * HPS
        for d in reversed(range(nqb)):
            runs = block(i * nqb + d, runs, dmasks[d])
        lax.fori_loop(0, i * nqb, lambda t, c: block(i * nqb - 1 - t, c, None), runs)
        o_ref[...] = acc[...].T
        if ride:
            @pl.when(jnp.logical_and(h == nh - 1, i == nq - 1))
            def _():
                ride.run("wait", ride_in, ride_out, sems)

    return pl.pallas_call(
        body, name="attn_fwd", grid=(nh, nq),
        in_specs=[pl.BlockSpec((SQ, GW), lambda h, i: (i, h)), ANY, ANY] + [ANY] * nr,
        out_specs=[pl.BlockSpec((SQ, GW), lambda h, i: (i, h)), pl.BlockSpec((HPS, nkc, SQ), lambda h, i: (h, 0, i))] + [ANY] * nr,
        out_shape=[jax.ShapeDtypeStruct((S, HEADS * HD), F32), jax.ShapeDtypeStruct((HEADS, nkc, S), F32)]
        + (ride.out_shape if ride else []),
        scratch_shapes=[pltpu.VMEM((S, GW), BF), pltpu.VMEM((GW, S), BF), pltpu.VMEM((GW, SQ), F32)]
        + (_exchange_sems(nr) if ride else []),
    )(m, kb_all, vt_all, *(ride.arrays if ride else []))


def _attn_bwd_t(m, kb_all, vb_all, kt_all, runs, dy, ride=None):
    S = m.shape[0]
    SQ = min(QT, S)
    nkc, nqb, nh, nq = S // KB, SQ // SB, HEADS // HPS, S // SQ
    nr = ride.n if ride else 0

    def body(*refs):
        q_ref, kb_hbm, vb_hbm, kt_hbm, r_ref, dy_ref = refs[:6]
        ride_in, (dq_ref, dk_hbm, dv_hbm), ride_out = refs[6:6 + nr], refs[6 + nr:9 + nr], refs[9 + nr:9 + 2 * nr]
        kb, vb, kt, dqt, dka, dva = refs[9 + 2 * nr:15 + 2 * nr]
        sems = refs[15 + 2 * nr:]
        h, i = pl.program_id(0), pl.program_id(1)
        cols = pl.ds(pl.multiple_of(h * GW, GW), GW)
        if ride:
            @pl.when(jnp.logical_and(h == 0, i == 0))
            def _():
                ride.run("start", ride_in, ride_out, sems)

        @pl.when(i == 0)
        def _():
            pltpu.sync_copy(kb_hbm.at[:, cols], kb)
            pltpu.sync_copy(vb_hbm.at[:, cols], vb)
            pltpu.sync_copy(kt_hbm.at[cols, :], kt)
            dka[...] = jnp.zeros_like(dka)
            dva[...] = jnp.zeros_like(dva)

        q8 = q_ref[...] * 0.125
        dyf = dy_ref[...]
        q8t, dyt = (q8 * LOG2E).T, dyf.T
        groups = [slice(g * KB, (g + 1) * KB) for g in range(HPS // 2)]
        qtm = [t for s in groups for t in _head_halves(q8t[s], 0)]
        dytm = [t for s in groups for t in _head_halves(dyt[s], 0)]
        qlm = [t for s in groups for t in _head_halves(q8[:, s], 1)]
        dylm = [t for s in groups for t in _head_halves(dyf[:, s], 1)]
        dmasks = _diag_masks(SQ)
        ri, ci = lax.broadcasted_iota(I32, (KB, KB), 0), lax.broadcasted_iota(I32, (KB, KB), 1)
        upper, lower = (ci >= ri).astype(BF), (ci <= ri).astype(BF)
        rev2 = jnp.concatenate([upper, upper], axis=1)
        dqt[...] = jnp.zeros_like(dqt)

        def block(jb, pres, dmask):
            masked = dmask is not None
            off = pl.multiple_of(jb * SB, SB)
            heads = range(HPS)
            kblk = [kb[pl.ds(off, SB), s] for s in groups]
            vblk = [vb[pl.ds(off, SB), s] for s in groups]
            ktb = [kt[s, pl.ds(off, SB)] for s in groups]
            run0s = [r_ref[hh, pl.ds(NKC * jb, 1), :] for hh in heads]
            run1s = [r_ref[hh, pl.ds(NKC * jb + 1, 1), :] for hh in heads]
            old_dq, old_dk, old_dv = dqt[...], dka[pl.ds(off, SB), :], dva[pl.ds(off, SB), :]
            dks, dvs, parts, new = [], [], [], []
            for g in range(HPS // 2):
                hs = (2 * g, 2 * g + 1)
                zs = [jnp.dot(kblk[g], qtm[hh], preferred_element_type=F32) for hh in hs]
                das = [jnp.dot(vblk[g], dytm[hh], preferred_element_type=F32) for hh in hs]
                sps = [_softplus2(z) for z in zs]
                sigs = [jnp.exp2(z - sp) for z, sp in zip(zs, sps)]
                if masked:
                    sps = [jnp.where(dmask, sp, 0.0) for sp in sps]
                css = [_chunk_sums(rev2, sp) for sp in sps]
                aa = [jnp.exp2(z - jnp.concatenate([run0s[hh] + cs[0], run1s[hh] + cs[1]], axis=0)) for z, hh, cs in zip(zs, hs, css)]
                if masked:
                    aa = [jnp.where(dmask, a, 0.0) for a in aa]
                gs = [a * da for a, da in zip(aa, das)]
                pgs = [[jnp.dot(lower, gg[c * KB:(c + 1) * KB].astype(BF), preferred_element_type=F32) for c in range(NKC)] for gg in gs]
                pre1s = [pres[hh] + pg[0][KB - 1:KB, :] for hh, pg in zip(hs, pgs)]
                dzs = [gg - sig * jnp.concatenate([pres[hh] + pg[0], pre1 + pg[1]], axis=0)
                       for gg, sig, hh, pre1, pg in zip(gs, sigs, hs, pre1s, pgs)]
                if masked:
                    dzs = [jnp.where(dmask, dz, 0.0) for dz in dzs]
                dzb, ab = [dz.astype(BF) for dz in dzs], [a.astype(BF) for a in aa]
                dks.append(sum(jnp.dot(dzb[t], qlm[hh], preferred_element_type=F32) for t, hh in enumerate(hs)))
                dvs.append(sum(jnp.dot(ab[t], dylm[hh], preferred_element_type=F32) for t, hh in enumerate(hs)))
                parts += [jnp.dot(ktb[g], dzb[t], preferred_element_type=F32)[t * HD:(t + 1) * HD, :] for t in range(2)]
                new += [pre1 + pg[1][KB - 1:KB, :] for pre1, pg in zip(pre1s, pgs)]
            dqt[...] = old_dq + jnp.concatenate(parts, axis=0)
            dka[pl.ds(off, SB), :] = old_dk + jnp.concatenate(dks, axis=1)
            dva[pl.ds(off, SB), :] = old_dv + jnp.concatenate(dvs, axis=1)
            return tuple(new)

        pres = lax.fori_loop(0, i * nqb, lambda jb, c: block(jb, c, None), (jnp.zeros((1, SQ), F32),) * HPS)
        for d in range(nqb):
            pres = block(i * nqb + d, pres, dmasks[d])
        dq_ref[...] = (dqt[...].T * 0.125).astype(dq_ref.dtype)

        @pl.when(i == nq - 1)
        def _():
            pltpu.sync_copy(dka, dk_hbm.at[:, cols])
            pltpu.sync_copy(dva, dv_hbm.at[:, cols])

        if ride:
            @pl.when(jnp.logical_and(h == nh - 1, i == nq - 1))
            def _():
                ride.run("wait", ride_in, ride_out, sems)

    n = HEADS * HD
    return pl.pallas_call(
        body, name="attn_bwd", grid=(nh, nq),
        in_specs=[pl.BlockSpec((SQ, GW), lambda h, i: (i, h)), ANY, ANY, ANY,
                  pl.BlockSpec((HPS, nkc, SQ), lambda h, i: (h, 0, i)), pl.BlockSpec((SQ, GW), lambda h, i: (i, h))] + [ANY] * nr,
        out_specs=[pl.BlockSpec((SQ, GW), lambda h, i: (i, h)), ANY, ANY] + [ANY] * nr,
        out_shape=[jax.ShapeDtypeStruct((S, n), BF), jax.ShapeDtypeStruct((S, n), F32), jax.ShapeDtypeStruct((S, n), F32)]
        + (ride.out_shape if ride else []),
        scratch_shapes=[pltpu.VMEM((S, GW), BF), pltpu.VMEM((S, GW), BF), pltpu.VMEM((GW, S), BF), pltpu.VMEM((GW, SQ), F32),
                        pltpu.VMEM((S, GW), F32), pltpu.VMEM((S, GW), F32)] + (_exchange_sems(nr) if ride else []),
    )(m, kb_all, vb_all, kt_all, runs, dy, *(ride.arrays if ride else []))


CW = 3 * HEADS * HD


def _shift_down(cur, prev8, s):
    if s == 0:
        return cur
    r = pltpu.roll(cur, s, 0)
    first = jnp.where(lax.broadcasted_iota(I32, (8, cur.shape[1]), 0) < s, pltpu.roll(prev8, s, 0), r[:8])
    return jnp.concatenate([first, r[8:]], axis=0)


def _shift_up(cur, next8, s):
    if s == 0:
        return cur
    n = cur.shape[0]
    r = pltpu.roll(cur, n - s, 0)
    last = jnp.where(lax.broadcasted_iota(I32, (8, cur.shape[1]), 0) >= 8 - s, pltpu.roll(next8, 8 - s, 0), r[n - 8:])
    return jnp.concatenate([r[:n - 8], last], axis=0)


def _conv_fwd(m, conv_w):
    S = m.shape[0]
    tm = min(512, S)
    hb = tm // 8

    def body(x_ref, p_ref, w_ref, o_ref):
        i = pl.program_id(0)
        cur = x_ref[...]
        prev = jnp.where(i > 0, p_ref[...], 0.0)
        w = w_ref[...]
        acc = cur * w[3:4]
        for jk in range(3):
            acc = acc + _shift_down(cur, prev, 3 - jk) * w[jk:jk + 1]
        o_ref[...] = acc

    return pl.pallas_call(
        body, name="conv_fwd", grid=(S // tm,),
        in_specs=[pl.BlockSpec((tm, CW), lambda i: (i, 1)), pl.BlockSpec((8, CW), lambda i: (jnp.maximum(i * hb - 1, 0), 1)),
                  pl.BlockSpec((4, CW), lambda i: (0, 0))],
        out_specs=pl.BlockSpec((tm, CW), lambda i: (i, 0)),
        out_shape=jax.ShapeDtypeStruct((S, CW), F32),
    )(m, m, conv_w)


def _conv_bwd(m, dyc, conv_w):
    S = m.shape[0]
    tm = min(512, S)
    hb = tm // 8
    nt = S // tm

    def body(x_ref, p_ref, d_ref, n_ref, w_ref, dx_ref, dw_ref):
        i = pl.program_id(0)
        cur = x_ref[...]
        prev = jnp.where(i > 0, p_ref[...], 0.0)
        d = d_ref[...]
        nxt = jnp.where(i < nt - 1, n_ref[...], 0.0)
        w = w_ref[...]
        acc = d * w[3:4]
        dws = []
        for jk in range(3):
            acc = acc + _shift_up(d, nxt, 3 - jk) * w[jk:jk + 1]
            dws.append(jnp.sum(d * _shift_down(cur, prev, 3 - jk), axis=0, keepdims=True))
        dws.append(jnp.sum(d * cur, axis=0, keepdims=True))
        dx_ref[...] = acc.astype(dx_ref.dtype)
        dw = jnp.concatenate(dws, axis=0)

        @pl.when(i == 0)
        def _():
            dw_ref[...] = dw

        @pl.when(i > 0)
        def _():
            dw_ref[...] += dw

    return pl.pallas_call(
        body, name="conv_bwd", grid=(nt,),
        in_specs=[pl.BlockSpec((tm, CW), lambda i: (i, 1)), pl.BlockSpec((8, CW), lambda i: (jnp.maximum(i * hb - 1, 0), 1)),
                  pl.BlockSpec((tm, CW), lambda i: (i, 0)),
                  pl.BlockSpec((8, CW), lambda i: (jnp.minimum((i + 1) * hb, S // 8 - 1), 0)),
                  pl.BlockSpec((4, CW), lambda i: (0, 0))],
        out_specs=[pl.BlockSpec((tm, CW), lambda i: (i, 0)), pl.BlockSpec((4, CW), lambda i: (0, 0))],
        out_shape=[jax.ShapeDtypeStruct((S, CW), BF), jax.ShapeDtypeStruct((4, CW), F32)],
    )(m, m, dyc, dyc, conv_w)


def _t(a):
    return jnp.swapaxes(a, 1, 2)


def _bdot(a, b):
    return jnp.einsum("hik,hkj->hij", a, b, preferred_element_type=F32)


@jax.custom_vjp
def _mm1(a, b):
    return _bdot(a.astype(BF), b.astype(BF))


def _bdot_nt(a, b):
    return jnp.einsum("hij,hkj->hik", a, b, preferred_element_type=F32)


def _bdot_tn(a, b):
    return jnp.einsum("hki,hkj->hij", a, b, preferred_element_type=F32)


_mm1.defvjp(lambda a, b: (_mm1(a, b), (a.astype(BF), b.astype(BF))),
            lambda res, dc: (_bdot_nt(dc.astype(BF), res[1]), _bdot_tn(res[0], dc.astype(BF))))


@jax.custom_vjp
def _mm1_nt(a, b):
    return _bdot_nt(a.astype(BF), b.astype(BF))


_mm1_nt.defvjp(lambda a, b: (_mm1_nt(a, b), (a.astype(BF), b.astype(BF))),
               lambda res, dc: (_bdot(dc.astype(BF), res[1]), _bdot_tn(dc.astype(BF), res[0])))


@jax.custom_vjp
def _mm1_tn(a, b):
    return _bdot_tn(a.astype(BF), b.astype(BF))


_mm1_tn.defvjp(lambda a, b: (_mm1_tn(a, b), (a.astype(BF), b.astype(BF))),
               lambda res, dc: (_bdot_nt(res[1], dc.astype(BF)), _bdot(res[0], dc.astype(BF))))


def _stack_rows(hi, lo):
    return jnp.concatenate([hi, lo], axis=1)


@jax.custom_vjp
def _mm3(a, b):
    (ah, al), (bh, bl) = _split(a), _split(b)
    n = a.shape[1]
    two = _bdot(_stack_rows(ah, al), bh)
    return two[:, :n] + two[:, n:] + _bdot(ah, bl)


def _mm3_fwd(a, b):
    return _mm3(a, b), (_split(a), _split(b))


def _mm3_bwd(res, dc):
    (ah, al), (bh, bl) = res
    dh, dl = _split(dc)
    n = dc.shape[1]
    two = _bdot_nt(_stack_rows(dh, dl), bh)
    da = two[:, :n] + two[:, n:] + _bdot_nt(dh, bl)
    db = _bdot_tn(jnp.concatenate([ah, ah, al], axis=1), jnp.concatenate([dh, dl, dh], axis=1))
    return da, db


_mm3.defvjp(_mm3_fwd, _mm3_bwd)


def _mm_exact(c3, b):
    hi, lo = _split(b)
    lo2 = (b - hi.astype(F32) - lo.astype(F32)).astype(BF)
    return _bdot(c3, jnp.concatenate([hi, lo, lo2], axis=-2))


@jax.custom_vjp
def _cumsum_rows(b):
    return _mm_exact(_tri3(True), b)


def _tri3(lower):
    ri = lax.broadcasted_iota(I32, (HEADS, CH, CH), 1)
    ci = lax.broadcasted_iota(I32, (HEADS, CH, CH), 2)
    tri = (ri >= ci if lower else ri <= ci).astype(BF)
    return jnp.concatenate([tri, tri, tri], axis=-1)


_cumsum_rows.defvjp(lambda b: (_cumsum_rows(b), None), lambda _, dc: (_mm_exact(_tri3(False), dc),))


CPS = 4


def _gdn_chunk(yc, gz, gba, alog, dtb):
    def heads(t, off):
        return jnp.stack([t[:, off + h * HD: off + (h + 1) * HD] for h in range(HEADS)])

    def cols(t, off):
        return jnp.stack([jnp.broadcast_to(t[:, off + h: off + h + 1], (CH, CH)) for h in range(HEADS)])

    c = yc * _sigmoid(yc)
    q, k, v, zg = heads(c, 0), heads(c, HEADS * HD), heads(c, 2 * HEADS * HD), heads(gz, 0)
    q = q * lax.rsqrt(jnp.sum(q * q, axis=-1, keepdims=True) + RMS_EPS) * (HD ** -0.5)
    k = k * lax.rsqrt(jnp.sum(k * k, axis=-1, keepdims=True) + RMS_EPS)
    beta = cols(_sigmoid(gba), 0)
    g = cols(-jnp.exp(alog) * _softplus(gba + dtb), HEADS)
    ri = lax.broadcasted_iota(I32, (HEADS, CH, CH), 1)
    ci = lax.broadcasted_iota(I32, (HEADS, CH, CH), 2)
    causal, strict = ri >= ci, ri > ci
    eye = (ri == ci).astype(F32)
    gc = _cumsum_rows(g)
    gr = _t(gc)
    decay = jnp.where(causal, jnp.exp(jnp.where(causal, gc - gr, 0.0)), 0.0)
    lower = jnp.where(strict, beta * _mm1_nt(k, k) * decay, 0.0)
    pw = -lower
    inv = eye + pw
    pw = _mm3(pw, pw)
    for _ in range(4):
        both = _mm3(jnp.concatenate([inv, pw], axis=1), pw)
        inv, pw = inv + both[:, :CH], both[:, CH:]
    inv = inv + _mm3(inv, pw)
    eg = jnp.exp(gc)
    uw = _mm3(inv, jnp.concatenate([v * beta, k * (beta * eg)], axis=2))
    u, w = uw[:, :, :HD], uw[:, :, HD:]
    qk = jnp.where(causal, _mm1_nt(q, k) * decay, 0.0)
    g_last = gc[:, CH - 1:CH, :]
    return u, jnp.concatenate([w, q * eg], axis=1), qk, k * jnp.exp(g_last - gc), jnp.exp(g_last), zg * _sigmoid(zg)


def _gdn_advance(state, pre, normw):
    u, wq, qk, kd, last, gate = pre
    ws = _mm1(wq, state)
    v_new = u - ws[:, :CH]
    o = ws[:, CH:] + _mm1(qk, v_new)
    new_state = state * last + _mm1_tn(kd, v_new)
    o = o * lax.rsqrt(jnp.mean(o * o, axis=-1, keepdims=True) + RMS_EPS) * normw * gate
    return jnp.concatenate([o[h] for h in range(HEADS)], axis=1), new_state


def _gdn_chunks(state, yc, gz, gba, alog, dtb, normw):
    rows = [slice(c * CH, (c + 1) * CH) for c in range(yc.shape[0] // CH)]
    pres = [_gdn_chunk(yc[r], gz[r], gba[r], alog, dtb) for r in rows]
    outs = []
    for pre in pres:
        o, state = _gdn_advance(state, pre, normw)
        outs.append(o)
    return jnp.concatenate(outs, axis=0), state


def _gdn_fwd(yc, m, alog, dtb, normw):
    S = yc.shape[0]
    RS = CPS * CH
    nch = S // RS

    def body(y_ref, gz_ref, gba_ref, al_ref, dt_ref, nw_ref, o_ref, st_ref, st):
        @pl.when(pl.program_id(0) == 0)
        def _():
            st[...] = jnp.zeros_like(st)

        cur = st[...]
        st_ref[0] = cur
        o, new = _gdn_chunks(cur, y_ref[...], gz_ref[...], gba_ref[...], al_ref[...], dt_ref[...], nw_ref[...])
        o_ref[...] = o
        st[...] = new

    return pl.pallas_call(
        body, name="gdn_fwd", grid=(nch,),
        in_specs=[pl.BlockSpec((RS, CW), lambda n: (n, 0)), pl.BlockSpec((RS, HEADS * HD), lambda n: (n, 6)),
                  pl.BlockSpec((RS, 128), lambda n: (n, 28)), pl.BlockSpec((1, 128), lambda n: (0, 0)),
                  pl.BlockSpec((1, 128), lambda n: (0, 0)), pl.BlockSpec((1, HD), lambda n: (0, 0))],
        out_specs=[pl.BlockSpec((RS, HEADS * HD), lambda n: (n, 0)), pl.BlockSpec((1, HEADS, HD, HD), lambda n: (n, 0, 0, 0))],
        out_shape=[jax.ShapeDtypeStruct((S, HEADS * HD), F32), jax.ShapeDtypeStruct((nch, HEADS, HD, HD), F32)],
        scratch_shapes=[pltpu.VMEM((HEADS, HD, HD), F32)],
    )(yc, m, m, alog, dtb, normw)


def _gdn_bwd(yc, m, alog, dtb, normw, states, dog):
    S = yc.shape[0]
    RS = CPS * CH
    nch = S // RS

    def body(y_ref, gz_ref, gba_ref, al_ref, dt_ref, nw_ref, st_ref, do_ref, dy_ref, dgz_ref, dgba_ref, dal_ref, ddt_ref, dnw_ref, dst):
        n = pl.program_id(0)

        @pl.when(n == 0)
        def _():
            dst[...] = jnp.zeros_like(dst)

        _, vjp = jax.vjp(_gdn_chunks, st_ref[0], y_ref[...], gz_ref[...], gba_ref[...], al_ref[...], dt_ref[...], nw_ref[...])
        d_state, d_y, d_gz, d_gba, d_al, d_dt, d_nw = vjp((do_ref[...], dst[...]))
        dst[...] = d_state
        dy_ref[...] = d_y
        dgz_ref[...] = d_gz.astype(dgz_ref.dtype)
        dgba_ref[...] = d_gba.astype(dgba_ref.dtype)
        for ref, val in ((dal_ref, d_al), (ddt_ref, d_dt), (dnw_ref, d_nw)):
            @pl.when(n == 0)
            def _(ref=ref, val=val):
                ref[...] = val

            @pl.when(n > 0)
            def _(ref=ref, val=val):
                ref[...] += val

    rev = lambda n: nch - 1 - n
    return pl.pallas_call(
        body, name="gdn_bwd", grid=(nch,),
        in_specs=[pl.BlockSpec((RS, CW), lambda n: (rev(n), 0)), pl.BlockSpec((RS, HEADS * HD), lambda n: (rev(n), 6)),
                  pl.BlockSpec((RS, 128), lambda n: (rev(n), 28)), pl.BlockSpec((1, 128), lambda n: (0, 0)),
                  pl.BlockSpec((1, 128), lambda n: (0, 0)), pl.BlockSpec((1, HD), lambda n: (0, 0)),
                  pl.BlockSpec((1, HEADS, HD, HD), lambda n: (rev(n), 0, 0, 0)),
                  pl.BlockSpec((RS, HEADS * HD), lambda n: (rev(n), 0))],
        out_specs=[pl.BlockSpec((RS, CW), lambda n: (rev(n), 0)), pl.BlockSpec((RS, HEADS * HD), lambda n: (rev(n), 0)),
                   pl.BlockSpec((RS, 128), lambda n: (rev(n), 0)), pl.BlockSpec((1, 128), lambda n: (0, 0)),
                   pl.BlockSpec((1, 128), lambda n: (0, 0)), pl.BlockSpec((1, HD), lambda n: (0, 0))],
        out_shape=[jax.ShapeDtypeStruct((S, CW), F32), jax.ShapeDtypeStruct((S, HEADS * HD), BF),
                   jax.ShapeDtypeStruct((S, 128), BF), jax.ShapeDtypeStruct((1, 128), F32),
                   jax.ShapeDtypeStruct((1, 128), F32), jax.ShapeDtypeStruct((1, HD), F32)],
        scratch_shapes=[pltpu.VMEM((HEADS, HD, HD), F32)],
    )(yc, m, m, alog, dtb, normw, states, dog)


def _mixer_fwd(h1, W, late=None):
    S = h1.shape[0]
    tm = min(512, S)
    n = HEADS * HD
    tb_ = min(BIG_ROWS, S)
    m, = _mm("mix_in", h1, W["wp"], S, NP, D, tm=tb_, tn=1536, tk=D, order="ji", outs=[((S, NP), F32, (tb_, 1536), _tile)])
    kb_all, vb_all, kt_all, vt_all = _attn_prep(m)
    ya, runs, *arrived = _attn_fwd_t(m, kb_all, vt_all, ride=late[0] if late else None)
    if late:
        late[1](W, arrived)
    runs = (kb_all, vb_all, kt_all, runs)
    b_gate, conv_w, alog, dtb, normw, w_sb, w_gdn, w_mo, g, b = (
        W[k] for k in ("b_gate", "conv_w", "alog", "dtb", "normw", "w_sb", "w_gdn", "w_mo", "ln2_g", "ln2_b"))
    yc = _conv_fwd(m, conv_w)
    og, states = _gdn_fwd(yc, m, alog, dtb, normw)
    ysb, = _mm("mix_sb", ya, w_sb, S, D, n, tm=tm, tn=D, tk=n, outs=[((S, D), F32, (tm, D), _row)])

    def merge_epi(acc, ys, gs, gg, bg):
        return _sigmoid(gs + bg[:, :D]) * ys + _sigmoid(gg + bg[:, D:]) * acc, acc

    u, ygdn = _mm("mix_gdn", og, w_gdn, S, D, n, tm=tm, tn=D, tk=n,
                  extras=[(ysb, (tm, D), _row), (m, (tm, D), lambda i, j, k: (i, GATE0 // D)),
                          (m, (tm, D), lambda i, j, k: (i, GATE0 // D + 1)), (b_gate, (1, 2 * D), _const)],
                  epilogue=merge_epi, outs=[((S, D), BF, (tm, D), _row), ((S, D), F32, (tm, D), _row)])

    def out_epi(acc, xin, gg, bb):
        r = ALPHA * xin + acc
        return _layer_norm(r, gg, bb), r

    h2, r2 = _mm("mix_out", u, w_mo, S, D, D, tm=tm, tn=D, tk=D,
                 extras=[(h1, (tm, D), _row), (g, (1, D), _const), (b, (1, D), _const)], epilogue=out_epi,
                 outs=[((S, D), F32, (tm, D), _row), ((S, D), F32, (tm, D), _row)])
    return h2, (h1, m, ya, runs, yc, og, states, ysb, ygdn, u, r2)


def _mixer_bwd(saved, W, dh, ride=None):
    h1, m, ya, runs, yc, og, states, ysb, ygdn, u, r2 = saved
    wp, b_gate, conv_w, alog, dtb, normw, w_sb, w_gdn, w_mo, g = (
        W[k] for k in ("wp", "b_gate", "conv_w", "alog", "dtb", "normw", "w_sb", "w_gdn", "w_mo", "ln2_g"))
    S = h1.shape[0]
    tm = min(512, S)
    n = HEADS * HD
    dr, dg, db = _ln_bwd("mix", r2, g, dh)

    def merge_epi(du, ys, yg, gs, gg, bg):
        s1, s2 = _sigmoid(gs + bg[:, :D]), _sigmoid(gg + bg[:, D:])
        dgate = jnp.concatenate([du * ys * s1 * (1.0 - s1), du * yg * s2 * (1.0 - s2)], axis=1)
        return du * s1, du * s2, dgate, jnp.sum(dgate, axis=0, keepdims=True)

    dysb, dygdn, dgate, d_bg = _mm(
        "mix_dmerge", dr, w_mo, S, D, D, tm=tm, tn=D, tk=D, tb=True,
        extras=[(ysb, (tm, D), _row), (ygdn, (tm, D), _row), (m, (tm, D), lambda i, j, k: (i, GATE0 // D)),
                (m, (tm, D), lambda i, j, k: (i, GATE0 // D + 1)), (b_gate, (1, 2 * D), _const)],
        epilogue=merge_epi, n_acc=1,
        outs=[((S, D), BF, (tm, D), _row), ((S, D), BF, (tm, D), _row), ((S, 2 * D), BF, (tm, 2 * D), _row),
              ((1, 2 * D), F32, (1, 2 * D), _const)])
    d_w_mo, = _mm("mix_dwmo", u, dr, D, D, S, tm=D, tn=D, tk=min(BIG_ROWS, S), ta=True, outs=[((D, D), F32, (D, D), _tile)])
    dya, = _mm("mix_dya", dysb, w_sb, S, n, D, tm=tm, tn=n, tk=D, tb=True, outs=[((S, n), F32, (tm, n), _row)])
    col_shards = [((NSH, n, D // NSH), F32, (None, n, D // NSH), lambda i, j, k: (j, 0, 0))]
    d_w_sb, = _mm("mix_dwsb", ya, dysb, n, D, S, tm=n, tn=D // NSH, tk=min(BIG_ROWS, S), ta=True, order="ji", outs=col_shards)
    dog, = _mm("mix_dog", dygdn, w_gdn, S, n, D, tm=tm, tn=n, tk=D, tb=True, outs=[((S, n), F32, (tm, n), _row)])
    d_w_gdn, = _mm("mix_dwgdn", og, dygdn, n, D, S, tm=n, tn=D // NSH, tk=min(BIG_ROWS, S), ta=True, order="ji", outs=col_shards)
    dq, dk, dv, *arrived = _attn_bwd_t(m, *runs, dya, ride=ride)
    dyc, dgz, dgba, d_alog, d_dtb, d_normw = _gdn_bwd(yc, m, alog, dtb, normw, states, dog)
    dxc, d_conv = _conv_bwd(m, dyc, conv_w)
    dm = jnp.concatenate([dq, dk.astype(BF), dv.astype(BF), dxc, dgz, dgba, jnp.zeros((S, GATE0 - 3712), BF), dgate], axis=1)
    tb_ = min(BIG_ROWS, S)
    d_h1, = _mm("mix_dh", dm, wp, S, D, NP, tm=tb_, tn=D, tk=1536, tb=True,
                extras=[(dr, (tb_, D), _row)], epilogue=lambda acc, d: (acc + ALPHA * d,),
                outs=[((S, D), F32, (tb_, D), _row)])
    d_wp, = _mm("mix_dwp", h1, dm, D, NP, S, tm=D, tn=1536, tk=tb_, ta=True, order="ji",
                outs=[((D, NP), F32, (D, 1536), _tile)])
    return d_h1, dict(wp=d_wp, b_gate=d_bg, conv_w=d_conv, alog=d_alog, dtb=d_dtb, normw=d_normw,
                      w_sb=d_w_sb, w_gdn=d_w_gdn, w_mo=d_w_mo, g=dg, b=db), arrived


def _ple_fwd(h3, p, w_pg, b_pg, w_ple, g, b, target):
    S = h3.shape[0]
    tm = min(512, S)
    pd = p.shape[1]
    pe, = _mm("ple_emb", p, w_ple, S, D, pd, tm=tm, tn=D, tk=pd, outs=[((S, D), F32, (tm, D), _row)])

    def epi(acc, e, xin, tgt, bp, gg, bb):
        gt = _sigmoid(acc + bp)
        r = ALPHA * xin + gt * e
        diff = _layer_norm(r, gg, bb) - tgt
        return gt, r, diff * (1.0 / D), jnp.sum(diff * diff, axis=0, keepdims=True)

    gt, r4, dh4, loss_row = _mm(
        "ple_gate", h3, w_pg, S, D, D, tm=tm, tn=D, tk=D,
        extras=[(pe, (tm, D), _row), (h3, (tm, D), _row), (target, (tm, D), _row), (b_pg, (1, D), _const),
                (g, (1, D), _const), (b, (1, D), _const)], epilogue=epi, n_acc=1,
        outs=[((S, D), F32, (tm, D), _row), ((S, D), F32, (tm, D), _row), ((S, D), F32, (tm, D), _row),
              ((1, D), F32, (1, D), _const)])
    return dh4, loss_row, (h3, p, pe, gt, r4)


def _ple_bwd(saved, w_pg, g, dh4):
    h3, p, pe, gt, r4 = saved
    S = h3.shape[0]
    tm = min(512, S)
    pd = p.shape[1]

    def fn(r, gg, dh, e, t):
        dr, dg, db = _layer_norm_bwd(r, gg, dh)
        dpre = dr * e * t * (1.0 - t)
        return dr, dpre, dr * t, dg, db, jnp.sum(dpre, axis=0, keepdims=True)

    row, one = (lambda i: (i, 0)), (lambda i: (0, 0))
    dr, dpre, dpe, dg, db, d_bpg = _rows(
        "ple_lnbwd", fn, S // tm,
        [(r4, (tm, D), row), (g, (1, D), one), (dh4, (tm, D), row), (pe, (tm, D), row), (gt, (tm, D), row)],
        [((S, D), F32, (tm, D), row), ((S, D), BF, (tm, D), row), ((S, D), BF, (tm, D), row),
         ((1, D), F32, (1, D), one), ((1, D), F32, (1, D), one), ((1, D), F32, (1, D), one)], n_acc=3)
    d_w_pg, = _mm("ple_dwpg", h3, dpre, D, D, S, tm=D, tn=D, tk=min(BIG_ROWS, S), ta=True, outs=[((D, D), F32, (D, D), _tile)])
    d_w_ple, = _mm("ple_dwple", p, dpe, pd, D, S, tm=pd, tn=D // NSH, tk=min(BIG_ROWS, S), ta=True, order="ji",
                   outs=[((NSH, pd, D // NSH), F32, (None, pd, D // NSH), lambda i, j, k: (j, 0, 0))])
    d_h3, = _mm("ple_dh", dpre, w_pg, S, D, D, tm=tm, tn=D, tk=D, tb=True,
                extras=[(dr, (tm, D), _row)], epilogue=lambda acc, d: (acc + ALPHA * d,),
                outs=[((S, D), F32, (tm, D), _row)])
    return d_h3, d_w_pg, d_bpg, d_w_ple, dg, db


def _local_step(x, p, target, W, now=None, soon=None, late=None, early=None, mid=None):
    W = dict(W)
    h1, sv1, arrived, W["ffn1_out"] = _ffn_fwd("ffn1", x, W["ffn1_in"], W.get("ffn1_out"), W["ln1_g"], W["ln1_b"],
                                               ride=soon[0] if soon else None, ride2=soon[1] if soon else None,
                                               w_out_ride=now)
    if soon:
        soon[2](W, arrived)
    h2, sv2 = _mixer_fwd(h1, W, late)
    h3, sv3, _, _ = _ffn_fwd("ffn2", h2, W["ffn2_in"], W["ffn2_out"], W["ln3_g"], W["ln3_b"])
    dh4, loss_row, sv4 = _ple_fwd(h3, p, W["w_pg"], W["b_pg"], W["w_ple"], W["ln4_g"], W["ln4_b"], target)
    G = {}
    dh3, G["w_pg"], G["b_pg"], G["w_ple"], G["ln4_g"], G["ln4_b"] = _ple_bwd(sv4, W["w_pg"], W["ln4_g"], dh4)
    dh2, G["ffn2_in"], G["ffn2_out"], G["ln3_g"], G["ln3_b"], _ = _ffn_bwd("ffn2", sv3, W["ffn2_in"], W["ffn2_out"], W["ln3_g"], dh3)
    dh1, gm, G["early"] = _mixer_bwd(sv2, W, dh2, ride=early(G) if early else None)
    G.update({k: v for k, v in gm.items() if k not in ("g", "b")})
    G["ln2_g"], G["ln2_b"] = gm["g"], gm["b"]
    dx, G["ffn1_in"], G["ffn1_out"], G["ln1_g"], G["ln1_b"], G["mid"] = _ffn_bwd(
        "ffn1", sv1, W["ffn1_in"], W["ffn1_out"], W["ln1_g"], dh1, ride=mid(G) if mid else None)
    return loss_row, dx, G


S2 = CUT - 2 * MSH


def _pack_wp(w4):
    tr = 256

    def fn(w):
        s = [w[j].astype(F32) for j in range(NSH)]
        full = jnp.concatenate([s[0], s[1], s[2][:, :S2], jnp.zeros((tr, GATE0 - CUT), F32), s[2][:, S2:], s[3]], axis=1)
        return (full,)

    return _rows("pack_wp", fn, D // tr, [(w4, (NSH, tr, MSH), lambda i: (0, i, 0))],
                 [((D, NP), BF, (tr, NP), lambda i: (i, 0))])[0]


def _unpack_wp(d):
    tr = 256
    g2 = GATE0 + MSH - S2

    def fn(v):
        return (jnp.stack([v[:, :MSH], v[:, MSH:2 * MSH], jnp.concatenate([v[:, 2 * MSH:CUT], v[:, GATE0:g2]], axis=1), v[:, g2:]]),)

    return _rows("unpack_wp", fn, D // tr, [(d, (tr, NP), lambda i: (i, 0))],
                 [((NSH, D, MSH), F32, (NSH, tr, MSH), lambda i: (0, i, 0))])[0]


def _cast_bf16(tag, w):
    r, c = w.shape
    tr = _pick(r, 256)
    return _rows(f"cast_{tag}", lambda v: (v,), r // tr, [(w, (tr, c), lambda i: (i, 0))],
                 [((r, c), BF, (tr, c), lambda i: (i, 0))])[0]


def _place():
    return lax.axis_index("x"), lax.axis_index("y"), lax.axis_index("c")


def _chip_exchange(phase, scatter, ins, outs, send, recv, loc):
    x, y, c = _place()
    me = 2 * x + y
    chips = [(1 - x, y), (x, 1 - y), (1 - x, 1 - y)]
    for t in range(len(ins)):
        own = pltpu.make_async_copy(ins[t].at[me] if scatter else ins[t], outs[t].at[me], loc.at[t])
        out_going, in_coming = [], []
        for q, (px, py) in enumerate(chips):
            src = ins[t].at[2 * px + py] if scatter else ins[t]
            sems = dict(send_sem=send.at[3 * t + q], recv_sem=recv.at[3 * t + q], device_id=(px, py, c), device_id_type=MESH)
            out_going.append(pltpu.make_async_remote_copy(src_ref=src, dst_ref=outs[t].at[me], **sems))
            in_coming.append(pltpu.make_async_remote_copy(src_ref=src, dst_ref=outs[t].at[2 * px + py], **sems))
        if phase == "start":
            own.start()
            for cp in out_going:
                cp.start()
        else:
            for cp in in_coming:
                cp.wait_recv()
            own.wait()
            for cp in out_going:
                cp.wait_send()


def _exchange_sems(n):
    return [pltpu.SemaphoreType.DMA((3 * n,)), pltpu.SemaphoreType.DMA((3 * n,)), pltpu.SemaphoreType.DMA((n,))]


class _Ride:
    def __init__(self, scatter, arrays):
        self.scatter, self.arrays, self.n = scatter, list(arrays), len(arrays)
        self.out_shape = [jax.ShapeDtypeStruct(a.shape if scatter else (NSH,) + a.shape, a.dtype) for a in self.arrays]

    def run(self, phase, refs_in, refs_out, sems):
        _chip_exchange(phase, self.scatter, refs_in, refs_out, *sems)


def _gather_shards(shards):
    n = len(shards)

    def body(*refs):
        _chip_exchange("start", False, refs[:n], refs[n:2 * n], *refs[2 * n:])
        _chip_exchange("wait", False, refs[:n], refs[n:2 * n], *refs[2 * n:])

    return pl.pallas_call(
        body, name="gather_weights", in_specs=[ANY] * n, out_specs=[ANY] * n,
        out_shape=[jax.ShapeDtypeStruct((NSH,) + s.shape, s.dtype) for s in shards], scratch_shapes=_exchange_sems(n),
    )(*shards)


def _reduce_chips(ps):
    n = len(ps)

    def body(*refs):
        _chip_exchange("start", True, refs[:n], refs[n:2 * n], *refs[2 * n:])
        _chip_exchange("wait", True, refs[:n], refs[n:2 * n], *refs[2 * n:])

    return pl.pallas_call(
        body, name="reduce_chips", in_specs=[ANY] * n, out_specs=[ANY] * n,
        out_shape=[jax.ShapeDtypeStruct(p_.shape, p_.dtype) for p_ in ps], scratch_shapes=_exchange_sems(n),
    )(*ps)


def _chunk_rows(h, c):
    return _pick(h, max(8, 524288 // c // 8 * 8))


def _sibling_sum(tag, g):
    _, r, c = g.shape
    h = r // 2
    tr = _chunk_rows(h, c)
    nch = h // tr
    steps = NSH * nch

    def body(top_ref, bot_ref, out_ref, narrow_ref, stage, land, send, recv):
        s = pl.program_id(0)
        x, y, core = _place()

        def exchange(keep_ref, give_ref):
            stage[...] = give_ref[0].astype(BF)
            cp = pltpu.make_async_remote_copy(src_ref=stage, dst_ref=land.at[s], send_sem=send.at[s], recv_sem=recv.at[s],
                                              device_id=(x, y, 1 - core), device_id_type=MESH)
            cp.start()
            cp.wait()
            total = keep_ref[0] + land[s].astype(F32)
            out_ref[0] = total
            narrow_ref[0] = total.astype(BF)

        @pl.when(core == 0)
        def _():
            exchange(top_ref, bot_ref)

        @pl.when(core == 1)
        def _():
            exchange(bot_ref, top_ref)

    return pl.pallas_call(
        body, name=f"sibling_sum_{tag}", grid=(steps,),
        in_specs=[pl.BlockSpec((1, tr, c), lambda s: (s // nch, s % nch, 0)),
                  pl.BlockSpec((1, tr, c), lambda s: (s // nch, nch + s % nch, 0))],
        out_specs=[pl.BlockSpec((1, tr, c), lambda s: (s // nch, s % nch, 0))] * 2,
        out_shape=[jax.ShapeDtypeStruct((NSH, h, c), F32), jax.ShapeDtypeStruct((NSH, h, c), BF)],
        scratch_shapes=[pltpu.VMEM((tr, c), BF), pltpu.VMEM((steps, tr, c), BF), pltpu.SemaphoreType.DMA((steps,)),
                        pltpu.SemaphoreType.DMA((steps,))],
    )(g, g)


def _chip_sum_share(tag, b, own):
    _, h, c = b.shape
    tr = _chunk_rows(h, c)
    steps = h // tr

    def body(b_ref, own_ref, out_ref, stage, land, send, recv):
        s = pl.program_id(0)
        x, y, core = _place()
        me = 2 * x + y
        v = [jnp.where(me == j, own_ref[j], b_ref[j].astype(F32)) for j in range(NSH)]
        total = ((v[0] + v[1]) + v[2]) + v[3]
        stage[...] = total
        cp = pltpu.make_async_remote_copy(src_ref=stage, dst_ref=land.at[s], send_sem=send.at[s], recv_sem=recv.at[s],
                                          device_id=(x, y, 1 - core), device_id_type=MESH)
        cp.start()
        cp.wait()
        out_ref[core] = total
        out_ref[1 - core] = land[s]

    return pl.pallas_call(
        body, name=f"chip_sum_share_{tag}", grid=(steps,),
        in_specs=[pl.BlockSpec((NSH, tr, c), lambda s: (0, s, 0))] * 2,
        out_specs=pl.BlockSpec((2, tr, c), lambda s: (0, s, 0)),
        out_shape=jax.ShapeDtypeStruct((2, h, c), F32),
        scratch_shapes=[pltpu.VMEM((tr, c), F32), pltpu.VMEM((steps, tr, c), F32), pltpu.SemaphoreType.DMA((steps,)),
                        pltpu.SemaphoreType.DMA((steps,))],
    )(b, own)


NDEV = 8


def _allreduce_small(pack):
    r, w = pack.shape
    rel = [(dx, dy, dc) for dx in (0, 1) for dy in (0, 1) for dc in (0, 1) if (dx, dy, dc) != (0, 0, 0)]

    def body(in_ref, out_ref, buf, send, recv):
        x, y, c = _place()
        me = 4 * x + 2 * y + c
        buf[me] = in_ref[...]
        peers = [((x + dx) % 2, (y + dy) % 2, (c + dc) % 2) for dx, dy, dc in rel]
        sent = []
        for k, peer in enumerate(peers):
            cp = pltpu.make_async_remote_copy(src_ref=in_ref, dst_ref=buf.at[me], send_sem=send.at[k], recv_sem=recv.at[k],
                                              device_id=peer, device_id_type=MESH)
            cp.start()
            sent.append(cp)
        for k, (px, py, pc) in enumerate(peers):
            pltpu.make_async_remote_copy(src_ref=in_ref, dst_ref=buf.at[4 * px + 2 * py + pc], send_sem=send.at[k], recv_sem=recv.at[k],
                                         device_id=(px, py, pc), device_id_type=MESH).wait_recv()
        for cp in sent:
            cp.wait_send()
        acc = buf[0]
        for k in range(1, NDEV):
            acc = acc + buf[k]
        out_ref[...] = acc

    vm = pl.BlockSpec(memory_space=pltpu.VMEM)
    return pl.pallas_call(
        body, name="allreduce_small", in_specs=[vm], out_specs=vm, out_shape=jax.ShapeDtypeStruct((r, w), F32),
        scratch_shapes=[pltpu.VMEM((NDEV, r, w), F32), pltpu.SemaphoreType.DMA((NDEV - 1,)), pltpu.SemaphoreType.DMA((NDEV - 1,))],
    )(pack)


def _adamw(tag, w, g, m, v):
    r, c = w.shape
    tr = _pick(r, max(8, 262144 // c // 8 * 8))

    def fn(w_, g_, m_, v_):
        m2 = B1 * m_ + (1.0 - B1) * g_
        v2 = B2 * v_ + (1.0 - B2) * (g_ * g_)
        m_hat = m2 / (1.0 - B1 ** STEP)
        v_hat = v2 / (1.0 - B2 ** STEP)
        return -LR * (m_hat / (jnp.sqrt(v_hat) + EPS) + WD * w_), m2, v2

    spec = ((tr, c), lambda i: (i, 0))
    return _rows(f"adamw_{tag}", fn, r // tr, [(a,) + spec for a in (w, g, m, v)], [((r, c), F32) + spec] * 3)


BIG = ("ffn1_w_in", "ffn1_w_out", "w_mix_in", "w_branch_sb", "w_branch_gdn", "w_mix_out", "ffn2_w_in", "ffn2_w_out",
       "w_ple_gate", "w_ple")
FIRST = ("ffn1_w_in",)
SOON = ("w_mix_in",)
LATER = tuple(n for n in BIG if n not in FIRST + SOON + ("ffn1_w_out",))
EARLY = ("ffn2_w_in", "ffn2_w_out", "w_ple_gate", "w_ple")
MID = ("w_mix_in", "w_branch_sb", "w_branch_gdn", "w_mix_out")
LAST = ("ffn1_w_in", "ffn1_w_out")
SMALL = ("ln1_g", "ln1_b", "b_gate", "conv_w", "a_log", "dt_bias", "gdn_norm_w", "ln2_g", "ln2_b", "ln3_g", "ln3_b",
         "b_ple_gate", "ln4_g", "ln4_b")
ORDER = ("ffn1_w_in", "ffn1_w_out", "ln1_g", "ln1_b", "w_mix_in", "b_gate", "conv_w", "a_log", "dt_bias", "gdn_norm_w",
         "w_branch_sb", "w_branch_gdn", "w_mix_out", "ln2_g", "ln2_b", "ffn2_w_in", "ffn2_w_out", "ln3_g", "ln3_b",
         "w_ple_gate", "b_ple_gate", "w_ple", "ln4_g", "ln4_b")
PACK_W = 2304


def _lane_row(v, lanes=128, at=HEADS):
    return jnp.pad(v[None, :], ((0, 0), (at, lanes - at - v.shape[0])))


def _col_join(w4):
    return jnp.transpose(w4, (1, 0, 2)).reshape(w4.shape[1], NSH * w4.shape[2])


def kernel(x, p, ffn1_w_in, ffn1_w_out, ln1_g, ln1_b, w_mix_in, b_gate, conv_w, a_log, dt_bias, gdn_norm_w, w_branch_sb, w_branch_gdn, w_mix_out, ln2_g, ln2_b, ffn2_w_in, ffn2_w_out, ln3_g, ln3_b, w_ple_gate, b_ple_gate, w_ple, ln4_g, ln4_b, loss_target, m_ffn1_w_in, m_ffn1_w_out, m_ln1_g, m_ln1_b, m_w_mix_in, m_b_gate, m_conv_w, m_a_log, m_dt_bias, m_gdn_norm_w, m_w_branch_sb, m_w_branch_gdn, m_w_mix_out, m_ln2_g, m_ln2_b, m_ffn2_w_in, m_ffn2_w_out, m_ln3_g, m_ln3_b, m_w_ple_gate, m_b_ple_gate, m_w_ple, m_ln4_g, m_ln4_b, v_ffn1_w_in, v_ffn1_w_out, v_ln1_g, v_ln1_b, v_w_mix_in, v_b_gate, v_conv_w, v_a_log, v_dt_bias, v_gdn_norm_w, v_w_branch_sb, v_w_branch_gdn, v_w_mix_out, v_ln2_g, v_ln2_b, v_ffn2_w_in, v_ffn2_w_out, v_ln3_g, v_ln3_b, v_w_ple_gate, v_b_ple_gate, v_w_ple, v_ln4_g, v_ln4_b):
    args = dict(locals())
    w = {n: args[n][0] for n in ORDER}
    mom = {n: args["m_" + n][0] for n in ORDER}
    var = {n: args["v_" + n][0] for n in ORDER}

    cast = {n: _cast_bf16(n, w[n]) for n in BIG if n not in SOON}
    mix_halves = [_cast_bf16(f"w_mix_in_{t}", w["w_mix_in"][t * (D // 2):(t + 1) * (D // 2)]) for t in range(2)]
    full = dict(zip(FIRST + ("conv_w",), _gather_shards([cast[n] for n in FIRST] + [w["conv_w"]])))
    W = dict(
        ffn1_in=full["ffn1_w_in"], conv_w=_col_join(full["conv_w"]), b_gate=w["b_gate"][None], alog=_lane_row(w["a_log"]), dtb=_lane_row(w["dt_bias"]),
        normw=w["gdn_norm_w"][None], b_pg=w["b_ple_gate"][None],
        **{f"ln{i}_{s}": w[f"ln{i}_{s}"][None] for i in (1, 2, 3, 4) for s in ("g", "b")},
    )

    def fill(W_, arrived):
        got = dict(zip(LATER, arrived))
        W_.update(w_sb=_col_join(got["w_branch_sb"]), w_gdn=_col_join(got["w_branch_gdn"]), w_mo=got["w_mix_out"].reshape(D, D),
                  ffn2_in=got["ffn2_w_in"], ffn2_out=got["ffn2_w_out"].reshape(DFF, D),
                  w_pg=got["w_ple_gate"].reshape(D, D), w_ple=_col_join(got["w_ple"]))

    def by_shard(G_, names):
        forms = dict(
            ffn1_w_in=lambda: G_["ffn1_in"], ffn1_w_out=lambda: G_["ffn1_out"].reshape(NSH, DFF // NSH, D),
            w_mix_in=lambda: _unpack_wp(G_["wp"]), w_branch_sb=lambda: G_["w_sb"], w_branch_gdn=lambda: G_["w_gdn"],
            w_mix_out=lambda: G_["w_mo"].reshape(NSH, D // NSH, D), ffn2_w_in=lambda: G_["ffn2_in"],
            ffn2_w_out=lambda: G_["ffn2_out"].reshape(NSH, DFF // NSH, D),
            w_ple_gate=lambda: G_["w_pg"].reshape(NSH, D // NSH, D), w_ple=lambda: G_["w_ple"])
        return [_sibling_sum(n, forms[n]()) for n in names]

    early_sums, mid_sums = [], []

    def early(G_):
        early_sums.extend(by_shard(G_, EARLY))
        return _Ride(True, [narrow for _, narrow in early_sums])

    def mid(G_):
        mid_sums.extend(by_shard(G_, MID))
        return _Ride(True, [narrow for _, narrow in mid_sums])

    loss_row, grad_x, G = _local_step(
        x[0], p[0, 0], loss_target[0], W, now=_Ride(False, [cast["ffn1_w_out"]]),
        soon=(_Ride(False, mix_halves[:1]), _Ride(False, mix_halves[1:]),
              lambda W_, arrived: W_.update(wp=_pack_wp(jnp.concatenate(arrived, axis=1)))),
        late=(_Ride(False, [cast[n] for n in LATER]), fill), early=early, mid=mid)
    loss = lax.psum(0.5 * jnp.sum(loss_row) / D, ("x", "y", "c"))

    last_sums = by_shard(G, LAST)
    landed = list(G["early"]) + list(G["mid"]) + list(_reduce_chips([narrow for _, narrow in last_sums]))
    grad = {n: _chip_sum_share(n, b, own).reshape(w[n].shape)
            for n, b, (own, _) in zip(EARLY + MID + LAST, landed, early_sums + mid_sums + last_sums)}

    pieces = [G["ln1_g"], G["ln1_b"], G["b_gate"], G["conv_w"].reshape(1, 4 * CW), G["alog"], G["dtb"], G["normw"],
              G["ln2_g"], G["ln2_b"], G["ln3_g"], G["ln3_b"], G["b_pg"], G["ln4_g"], G["ln4_b"]]
    flat = jnp.concatenate(pieces, axis=1)
    flat = jnp.pad(flat, ((0, 0), (0, NDEV * PACK_W - flat.shape[1])))
    total = _allreduce_small(flat.reshape(NDEV, PACK_W)).reshape(1, NDEV * PACK_W)
    off = 0
    for n, piece in zip(SMALL, pieces):
        grad[n] = total[0, off:off + piece.shape[1]]
        off += piece.shape[1]
    chip = 2 * lax.axis_index("x") + lax.axis_index("y")
    grad["conv_w"] = lax.dynamic_slice_in_dim(grad["conv_w"].reshape(4, CW), chip * (CW // NSH), CW // NSH, axis=1)
    grad["a_log"] = grad["a_log"][HEADS:2 * HEADS]
    grad["dt_bias"] = grad["dt_bias"][HEADS:2 * HEADS]

    delta, new_m, new_v = {}, {}, {}
    for n in ORDER:
        shape2 = w[n].shape if w[n].ndim == 2 else (1, w[n].shape[0])
        d_, m_, v_ = _adamw(n, *[a.reshape(shape2) for a in (w[n], grad[n], mom[n], var[n])])
        delta[n], new_m[n], new_v[n] = (a.reshape(args[n].shape) for a in (d_, m_, v_))
    outs = [loss, grad_x[None]]
    outs += [grad[n].reshape(args[n].shape) for n in ORDER]
    for group in (delta, new_m, new_v):
        outs += [group[n] for n in ORDER]
    return tuple(outs)
```

```python
import jax
import jax.numpy as jnp
from jax import lax
from jax.experimental import pallas as pl
from jax.experimental.pallas import tpu as pltpu

F32 = jnp.float32
BF = jnp.bfloat16
I32 = jnp.int32
MESH = pl.DeviceIdType.MESH
ANY = pl.BlockSpec(memory_space=pl.ANY)

D = 1024
DFF = 2816
NSH = 4
FSH = 2 * DFF // NSH
NIN = 5648
MSH = NIN // NSH
NP = 6144
CUT = 3600
GATE0 = 4096
HEADS = 8
HD = 64
CH = 64
KB = 128
BIG_ROWS = 1024
ALPHA = 2.0 ** 0.25
LN_EPS = 1e-5
RMS_EPS = 1e-6
B1, B2, LR, EPS, WD, STEP = 0.9, 0.999, 0.001, 1e-08, 0.01, 10


def _sigmoid(x):
    return 0.5 * jnp.tanh(0.5 * x) + 0.5


def _softplus(x):
    return jnp.maximum(x, 0.0) + jnp.log1p(jnp.exp(-jnp.abs(x)))


def _layer_norm(r, g, b):
    mu = jnp.mean(r, axis=-1, keepdims=True)
    xc = r - mu
    var = jnp.mean(xc * xc, axis=-1, keepdims=True)
    return xc * lax.rsqrt(var + LN_EPS) * g + b


def _layer_norm_bwd(r, g, dh):
    mu = jnp.mean(r, axis=-1, keepdims=True)
    xc = r - mu
    var = jnp.mean(xc * xc, axis=-1, keepdims=True)
    xhat = xc * lax.rsqrt(var + LN_EPS)
    dxh = dh * g
    dr = lax.rsqrt(var + LN_EPS) * (dxh - jnp.mean(dxh, axis=-1, keepdims=True) - xhat * jnp.mean(dxh * xhat, axis=-1, keepdims=True))
    return dr, jnp.sum(dh * xhat, axis=0, keepdims=True), jnp.sum(dh, axis=0, keepdims=True)


def _pick(n, cap):
    if n <= cap:
        return n
    for t in range(cap - cap % 8, 7, -8):
        if n % t == 0:
            return t
    raise ValueError((n, cap))


def _mm(name, a, b, M, N, K, *, tm, tn, tk, ta=False, tb=False, a_spec=None, b_spec=None, order="ij",
        extras=(), epilogue=None, outs, n_acc=0, ride=None):
    ni, nj, nk = M // tm, N // tn, K // tk
    nr = ride.n if ride else 0
    assert M % tm == 0 and N % tn == 0 and K % tk == 0, (name, M, N, K, tm, tn, tk)
    assert n_acc == 0 or nj == 1

    def wrap(fn):
        if order == "ij":
            return lambda g0, g1, g2: fn(g0, g1, g2)
        return lambda g0, g1, g2: fn(g1, g0, g2)

    if a_spec is None:
        a_spec = ((tk, tm), lambda i, j, k: (k, i)) if ta else ((tm, tk), lambda i, j, k: (i, k))
    if b_spec is None:
        b_spec = ((tn, tk), lambda i, j, k: (j, k)) if tb else ((tk, tn), lambda i, j, k: (k, j))
    dims = (((0 if ta else 1,), (1 if tb else 0,)), ((), ()))
    ne, no = len(extras), len(outs)
    grid = (ni, nj, nk) if order == "ij" else (nj, ni, nk)

    def body(*refs):
        a_ref, b_ref = refs[0], refs[1]
        ex = refs[2:2 + ne]
        ride_in = refs[2 + ne:2 + ne + nr]
        o = refs[2 + ne + nr:2 + ne + nr + no]
        ride_out = refs[2 + ne + nr + no:2 + ne + 2 * nr + no]
        scratch = refs[2 + ne + 2 * nr + no:]
        g0, g1, k = pl.program_id(0), pl.program_id(1), pl.program_id(2)
        first = jnp.logical_and(g0 == 0, g1 == 0)
        if ride:
            @pl.when(jnp.logical_and(first, k == 0))
            def _():
                ride.run("start", ride_in, ride_out, scratch[-3:])

        p = lax.dot_general(a_ref[...].astype(BF), b_ref[...].astype(BF), dims, preferred_element_type=F32)

        def finish(acc):
            vals = (acc,) if epilogue is None else epilogue(acc, *[e[...] for e in ex])
            for idx, (ref, val) in enumerate(zip(o, vals)):
                if idx < no - n_acc:
                    ref[...] = val.astype(ref.dtype)
                else:
                    @pl.when(first)
                    def _(ref=ref, val=val):
                        ref[...] = val

                    @pl.when(jnp.logical_not(first))
                    def _(ref=ref, val=val):
                        ref[...] += val

        if nk == 1:
            finish(p)
        else:
            acc_ref = scratch[0]

            @pl.when(k == 0)
            def _():
                acc_ref[...] = p

            @pl.when(k > 0)
            def _():
                acc_ref[...] += p

            @pl.when(k == nk - 1)
            def _():
                finish(acc_ref[...])

        if ride:
            @pl.when(jnp.logical_and(jnp.logical_and(g0 == grid[0] - 1, g1 == grid[1] - 1), k == nk - 1))
            def _():
                ride.run("wait", ride_in, ride_out, scratch[-3:])

    in_specs = [pl.BlockSpec(a_spec[0], wrap(a_spec[1])), pl.BlockSpec(b_spec[0], wrap(b_spec[1]))]
    in_specs += [pl.BlockSpec(blk, wrap(fn)) for _, blk, fn in extras] + [ANY] * nr
    res = pl.pallas_call(
        body, name=name, grid=grid, in_specs=in_specs,
        out_specs=[pl.BlockSpec(blk, wrap(fn)) for _, _, blk, fn in outs] + [ANY] * nr,
        out_shape=[jax.ShapeDtypeStruct(shape, dt) for shape, dt, _, _ in outs] + (ride.out_shape if ride else []),
        scratch_shapes=([pltpu.VMEM((tm, tn), F32)] if nk > 1 else []) + (_exchange_sems(nr) if ride else []),
    )(a, b, *[e[0] for e in extras], *(ride.arrays if ride else []))
    return res


def _row(i, j, k):
    return (i, 0)


def _tile(i, j, k):
    return (i, j)


def _const(i, j, k):
    return (0, 0)


def _rows(name, fn, n_steps, ins, outs, n_acc=0):
    ni, no = len(ins), len(outs)

    def body(*refs):
        i = pl.program_id(0)
        vals = fn(*[r[...] for r in refs[:ni]])
        for idx, (ref, val) in enumerate(zip(refs[ni:ni + no], vals)):
            if idx < no - n_acc:
                ref[...] = val.astype(ref.dtype)
            else:
                @pl.when(i == 0)
                def _(ref=ref, val=val):
                    ref[...] = val

                @pl.when(i > 0)
                def _(ref=ref, val=val):
                    ref[...] += val

    return pl.pallas_call(
        body, name=name, grid=(n_steps,),
        in_specs=[pl.BlockSpec(blk, fn_) for _, blk, fn_ in ins],
        out_specs=[pl.BlockSpec(blk, fn_) for _, _, blk, fn_ in outs],
        out_shape=[jax.ShapeDtypeStruct(shape, dt) for shape, dt, _, _ in outs],
    )(*[a for a, _, _ in ins])


def _ffn_fwd(tag, x, w_in, w_out, g, b, ride=None, ride2=None, w_out_ride=None):
    S = x.shape[0]
    tm = min(BIG_ROWS, S)
    gate, *came = _mm(f"{tag}_gate", x, w_in, S, DFF, D, tm=tm, tn=FSH, tk=D, order="ji",
                      b_spec=((None, D, FSH), lambda i, j, k: (j, 0, 0)), ride=w_out_ride,
                      outs=[((S, DFF), F32, (tm, FSH), _tile)])
    if w_out_ride:
        w_out = came[0].reshape(DFF, D)

    def up_epi(acc, gt):
        return acc, gt * _sigmoid(gt) * acc

    up, s, *arrived = _mm(f"{tag}_up", x, w_in, S, DFF, D, tm=tm, tn=FSH, tk=D, order="ji",
                          b_spec=((None, D, FSH), lambda i, j, k: (j + 2, 0, 0)),
                          extras=[(gate, (tm, FSH), _tile)], epilogue=up_epi, ride=ride,
                          outs=[((S, DFF), F32, (tm, FSH), _tile), ((S, DFF), BF, (tm, FSH), _tile)])

    def out_epi(acc, xin, gg, bb):
        r = ALPHA * xin + 0.5 * acc
        return _layer_norm(r, gg, bb), r

    h, r, *more = _mm(f"{tag}_out", s, w_out, S, D, DFF, tm=tm, tn=D, tk=DFF,
                      extras=[(x, (tm, D), _row), (g, (1, D), _const), (b, (1, D), _const)], epilogue=out_epi, ride=ride2,
                      outs=[((S, D), F32, (tm, D), _row), ((S, D), F32, (tm, D), _row)])
    return h, (x, gate, up, s, r), arrived + more, w_out


def _ln_bwd(tag, r, g, dh):
    S = r.shape[0]
    tm = min(512, S)
    return _rows(f"{tag}_lnbwd", _layer_norm_bwd, S // tm,
                 [(r, (tm, D), lambda i: (i, 0)), (g, (1, D), lambda i: (0, 0)), (dh, (tm, D), lambda i: (i, 0))],
                 [((S, D), F32, (tm, D), lambda i: (i, 0)), ((1, D), F32, (1, D), lambda i: (0, 0)),
                  ((1, D), F32, (1, D), lambda i: (0, 0))], n_acc=2)


def _ffn_bwd(tag, saved, w_in, w_out, g, dh, ride=None):
    x, gate, up, s, r = saved
    S = x.shape[0]
    tm = min(BIG_ROWS, S)
    dr, dg, db = _ln_bwd(tag, r, g, dh)

    def act_epi(acc, gt, u):
        ds = 0.5 * acc
        sg = _sigmoid(gt)
        return (jnp.stack([ds * u * (sg * (1.0 + gt * (1.0 - sg))), ds * (gt * sg)]),)

    da, = _mm(f"{tag}_dact", dr, w_out, S, DFF, D, tm=tm, tn=FSH, tk=D, tb=True, order="ji",
              extras=[(gate, (tm, FSH), _tile), (up, (tm, FSH), _tile)], epilogue=act_epi,
              outs=[((2, S, DFF), BF, (2, tm, FSH), lambda i, j, k: (0, i, j))])
    d_w_out, = _mm(f"{tag}_dwout", s, dr, DFF, D, S, tm=FSH, tn=D, tk=tm, ta=True,
                   epilogue=lambda acc: (0.5 * acc,), outs=[((DFF, D), F32, (FSH, D), _tile)])
    tb_ = min(BIG_ROWS, S)
    d_in, *arrived = _mm(f"{tag}_dx", da, w_in, S, D, 2 * DFF, tm=tb_, tn=D, tk=FSH, tb=True,
                         a_spec=((None, tb_, FSH), lambda i, j, k: (k // 2, i, k % 2)),
                         b_spec=((None, D, FSH), lambda i, j, k: (k, 0, 0)),
                         extras=[(dr, (tb_, D), _row)], epilogue=lambda acc, d: (acc + ALPHA * d,), ride=ride,
                         outs=[((S, D), F32, (tb_, D), _row)])
    d_w_in, = _mm(f"{tag}_dwin", x, da, D, 2 * DFF, S, tm=D, tn=FSH, tk=tb_, ta=True, order="ji",
                  b_spec=((None, tb_, FSH), lambda i, j, k: (j // 2, k, j % 2)),
                  outs=[((NSH, D, FSH), F32, (None, D, FSH), lambda i, j, k: (j, 0, 0))])
    return d_in, d_w_in, d_w_out, dg, db, arrived


SB = 256
NKC = SB // KB
HPF = 8
HPS = 4
GW = HPS * HD
LOG2E = 1.4426950408889634


def _split(vals):
    hi = vals.astype(BF)
    return hi, (vals - hi.astype(F32)).astype(BF)


def _chunk_sums(tri2, vals):
    hi, lo = _split(vals)
    return [jnp.dot(tri2, jnp.concatenate([hi[c * KB:(c + 1) * KB], lo[c * KB:(c + 1) * KB]], axis=0), preferred_element_type=F32)
            for c in range(NKC)]


def _head_halves(t, axis):
    idx = lax.broadcasted_iota(I32, t.shape, axis)
    return [jnp.where(idx < HD, t, 0.0).astype(BF), jnp.where(idx >= HD, t, 0.0).astype(BF)]


QT = 512


def _diag_masks(sq):
    krow, qcol = lax.broadcasted_iota(I32, (SB, sq), 0), lax.broadcasted_iota(I32, (SB, sq), 1)
    return [krow + d * SB < qcol for d in range(sq // SB)]


def _softplus2(z):
    return jnp.maximum(z, 0.0) + jnp.log2(1.0 + jnp.exp2(jnp.minimum(z, -z)))


def _attn_prep(m):
    S = m.shape[0]
    tm = min(512, S)
    n = HEADS * HD
    return _rows("attn_prep", lambda k, v: (k, v, k.T, v.T), S // tm,
                 [(m, (tm, n), lambda i: (i, 1)), (m, (tm, n), lambda i: (i, 2))],
                 [((S, n), BF, (tm, n), lambda i: (i, 0)), ((S, n), BF, (tm, n), lambda i: (i, 0)),
                  ((n, S), BF, (n, tm), lambda i: (0, i)), ((n, S), BF, (n, tm), lambda i: (0, i))])


def _attn_fwd_t(m, kb_all, vt_all, ride=None):
    S = m.shape[0]
    SQ = min(QT, S)
    assert S % SQ == 0 and SQ % SB == 0
    HPS, GW = HPF, HPF * HD
    nkc, nqb, nh, nq = S // KB, SQ // SB, HEADS // HPS, S // SQ
    nr = ride.n if ride else 0

    def body(*refs):
        q_ref, kb_hbm, vt_hbm = refs[:3]
        ride_in, (o_ref, r_ref), ride_out = refs[3:3 + nr], refs[3 + nr:5 + nr], refs[5 + nr:5 + 2 * nr]
        kb, vt, acc = refs[5 + 2 * nr:8 + 2 * nr]
        sems = refs[8 + 2 * nr:]
        h, i = pl.program_id(0), pl.program_id(1)
        if ride:
            @pl.when(jnp.logical_and(h == 0, i == 0))
            def _():
                ride.run("start", ride_in, ride_out, sems)

        @pl.when(i == 0)
        def _():
            cols = pl.ds(pl.multiple_of(h * GW, GW), GW)
            pltpu.sync_copy(kb_hbm.at[:, cols], kb)
            pltpu.sync_copy(vt_hbm.at[cols, :], vt)

        qt = (q_ref[...] * (0.125 * LOG2E)).T
        qtm = [t for g in range(HPS // 2) for t in _head_halves(qt[g * KB:(g + 1) * KB], 0)]
        dmasks = _diag_masks(SQ)
        upper = (lax.broadcasted_iota(I32, (KB, KB), 1) >= lax.broadcasted_iota(I32, (KB, KB), 0)).astype(BF)
        tri2 = jnp.concatenate([upper, upper], axis=1)
        acc[...] = jnp.zeros_like(acc)
        r_ref[...] = jnp.zeros_like(r_ref)

        def block(jb, runs, dmask):
            masked = dmask is not None
            off = pl.multiple_of(jb * SB, SB)
            groups = [slice(g * KB, (g + 1) * KB) for g in range(HPS // 2)]
            kblk = [kb[pl.ds(off, SB), s] for s in groups]
            vtb = [vt[s, pl.ds(off, SB)] for s in groups]
            old = acc[...]
            zs = [jnp.dot(kblk[hh // 2], qtm[hh], preferred_element_type=F32) for hh in range(HPS)]
            sps = [_softplus2(z) for z in zs]
            if masked:
                sps = [jnp.where(dmask, sp, 0.0) for sp in sps]
            css = [_chunk_sums(tri2, sp) for sp in sps]
            run0s = [run + cs[1][0:1, :] for run, cs in zip(runs, css)]
            aa = [jnp.exp2(z - jnp.concatenate([run0 + cs[0], run + cs[1]], axis=0)) for z, run, run0, cs in zip(zs, runs, run0s, css)]
            if masked:
                aa = [jnp.where(dmask, a, 0.0) for a in aa]
            parts = [jnp.dot(vtb[hh // 2], aa[hh].astype(BF), preferred_element_type=F32) for hh in range(HPS)]
            upd = jnp.concatenate([parts[hh][(hh % 2) * HD:(hh % 2 + 1) * HD, :] for hh in range(HPS)], axis=0)
            for hh in range(HPS):
                r_ref[hh, pl.ds(NKC * jb, 1), :] = run0s[hh]
                r_ref[hh, pl.ds(NKC * jb + 1, 1), :] = runs[hh]
            acc[...] = old + upd
            return tuple(run0 + cs[0][0:1, :] for run0, cs in zip(run0s, css))

        runs = (jnp.zeros((1, SQ), F32),) * HPS
        for d in reversed(range(nqb)):
            runs = block(i * nqb + d, runs, dmasks[d])
        lax.fori_loop(0, i * nqb, lambda t, c: block(i * nqb - 1 - t, c, None), runs)
        o_ref[...] = acc[...].T
        if ride:
            @pl.when(jnp.logical_and(h == nh - 1, i == nq - 1))
            def _():
                ride.run("wait", ride_in, ride_out, sems)

    return pl.pallas_call(
        body, name="attn_fwd", grid=(nh, nq),
        in_specs=[pl.BlockSpec((SQ, GW), lambda h, i: (i, h)), ANY, ANY] + [ANY] * nr,
        out_specs=[pl.BlockSpec((SQ, GW), lambda h, i: (i, h)), pl.BlockSpec((HPS, nkc, SQ), lambda h, i: (h, 0, i))] + [ANY] * nr,
        out_shape=[jax.ShapeDtypeStruct((S, HEADS * HD), F32), jax.ShapeDtypeStruct((HEADS, nkc, S), F32)]
        + (ride.out_shape if ride else []),
        scratch_shapes=[pltpu.VMEM((S, GW), BF), pltpu.VMEM((GW, S), BF), pltpu.VMEM((GW, SQ), F32)]
        + (_exchange_sems(nr) if ride else []),
    )(m, kb_all, vt_all, *(ride.arrays if ride else []))


def _attn_bwd_t(m, kb_all, vb_all, kt_all, runs, dy, ride=None):
    S = m.shape[0]
    SQ = min(QT, S)
    nkc, nqb, nh, nq = S // KB, SQ // SB, HEADS // HPS, S // SQ
    nr = ride.n if ride else 0

    def body(*refs):
        q_ref, kb_hbm, vb_hbm, kt_hbm, r_ref, dy_ref = refs[:6]
        ride_in, (dq_ref, dk_hbm, dv_hbm), ride_out = refs[6:6 + nr], refs[6 + nr:9 + nr], refs[9 + nr:9 + 2 * nr]
        kb, vb, kt, dqt, dka, dva = refs[9 + 2 * nr:15 + 2 * nr]
        sems = refs[15 + 2 * nr:]
        h, i = pl.program_id(0), pl.program_id(1)
        cols = pl.ds(pl.multiple_of(h * GW, GW), GW)
        if ride:
            @pl.when(jnp.logical_and(h == 0, i == 0))
            def _():
                ride.run("start", ride_in, ride_out, sems)

        @pl.when(i == 0)
        def _():
            pltpu.sync_copy(kb_hbm.at[:, cols], kb)
            pltpu.sync_copy(vb_hbm.at[:, cols], vb)
            pltpu.sync_copy(kt_hbm.at[cols, :], kt)
            dka[...] = jnp.zeros_like(dka)
            dva[...] = jnp.zeros_like(dva)

        q8 = q_ref[...] * 0.125
        dyf = dy_ref[...]
        q8t, dyt = (q8 * LOG2E).T, dyf.T
        groups = [slice(g * KB, (g + 1) * KB) for g in range(HPS // 2)]
        qtm = [t for s in groups for t in _head_halves(q8t[s], 0)]
        dytm = [t for s in groups for t in _head_halves(dyt[s], 0)]
        qlm = [t for s in groups for t in _head_halves(q8[:, s], 1)]
        dylm = [t for s in groups for t in _head_halves(dyf[:, s], 1)]
        dmasks = _diag_masks(SQ)
        ri, ci = lax.broadcasted_iota(I32, (KB, KB), 0), lax.broadcasted_iota(I32, (KB, KB), 1)
        upper, lower = (ci >= ri).astype(BF), (ci <= ri).astype(BF)
        rev2 = jnp.concatenate([upper, upper], axis=1)
        dqt[...] = jnp.zeros_like(dqt)

        def block(jb, pres, dmask):
            masked = dmask is not None
            off = pl.multiple_of(jb * SB, SB)
            heads = range(HPS)
            kblk = [kb[pl.ds(off, SB), s] for s in groups]
            vblk = [vb[pl.ds(off, SB), s] for s in groups]
            ktb = [kt[s, pl.ds(off, SB)] for s in groups]
            run0s = [r_ref[hh, pl.ds(NKC * jb, 1), :] for hh in heads]
            run1s = [r_ref[hh, pl.ds(NKC * jb + 1, 1), :] for hh in heads]
            old_dq, old_dk, old_dv = dqt[...], dka[pl.ds(off, SB), :], dva[pl.ds(off, SB), :]
            dks, dvs, parts, new = [], [], [], []
            for g in range(HPS // 2):
                hs = (2 * g, 2 * g + 1)
                zs = [jnp.dot(kblk[g], qtm[hh], preferred_element_type=F32) for hh in hs]
                das = [jnp.dot(vblk[g], dytm[hh], preferred_element_type=F32) for hh in hs]
                sps = [_softplus2(z) for z in zs]
                sigs = [jnp.exp2(z - sp) for z, sp in zip(zs, sps)]
                if masked:
                    sps = [jnp.where(dmask, sp, 0.0) for sp in sps]
                css = [_chunk_sums(rev2, sp) for sp in sps]
                aa = [jnp.exp2(z - jnp.concatenate([run0s[hh] + cs[0], run1s[hh] + cs[1]], axis=0)) for z, hh, cs in zip(zs, hs, css)]
                if masked:
                    aa = [jnp.where(dmask, a, 0.0) for a in aa]
                gs = [a * da for a, da in zip(aa, das)]
                pgs = [[jnp.dot(lower, gg[c * KB:(c + 1) * KB].astype(BF), preferred_element_type=F32) for c in range(NKC)] for gg in gs]
                pre1s = [pres[hh] + pg[0][KB - 1:KB, :] for hh, pg in zip(hs, pgs)]
                dzs = [gg - sig * jnp.concatenate([pres[hh] + pg[0], pre1 + pg[1]], axis=0)
                       for gg, sig, hh, pre1, pg in zip(gs, sigs, hs, pre1s, pgs)]
                if masked:
                    dzs = [jnp.where(dmask, dz, 0.0) for dz in dzs]
                dzb, ab = [dz.astype(BF) for dz in dzs], [a.astype(BF) for a in aa]
                dks.append(sum(jnp.dot(dzb[t], qlm[hh], preferred_element_type=F32) for t, hh in enumerate(hs)))
                dvs.append(sum(jnp.dot(ab[t], dylm[hh], preferred_element_type=F32) for t, hh in enumerate(hs)))
                parts += [jnp.dot(ktb[g], dzb[t], preferred_element_type=F32)[t * HD:(t + 1) * HD, :] for t in range(2)]
                new += [pre1 + pg[1][KB - 1:KB, :] for pre1, pg in zip(pre1s, pgs)]
            dqt[...] = old_dq + jnp.concatenate(parts, axis=0)
            dka[pl.ds(off, SB), :] = old_dk + jnp.concatenate(dks, axis=1)
            dva[pl.ds(off, SB), :] = old_dv + jnp.concatenate(dvs, axis=1)
            return tuple(new)

        pres = lax.fori_loop(0, i * nqb, lambda jb, c: block(jb, c, None), (jnp.zeros((1, SQ), F32),) * HPS)
        for d in range(nqb):
            pres = block(i * nqb + d, pres, dmasks[d])
        dq_ref[...] = (dqt[...].T * 0.125).astype(dq_ref.dtype)

        @pl.when(i == nq - 1)
        def _():
            pltpu.sync_copy(dka, dk_hbm.at[:, cols])
            pltpu.sync_copy(dva, dv_hbm.at[:, cols])

        if ride:
            @pl.when(jnp.logical_and(h == nh - 1, i == nq - 1))
            def _():
                ride.run("wait", ride_in, ride_out, sems)

    n = HEADS * HD
    return pl.pallas_call(
        body, name="attn_bwd", grid=(nh, nq),
        in_specs=[pl.BlockSpec((SQ, GW), lambda h, i: (i, h)), ANY, ANY, ANY,
                  pl.BlockSpec((HPS, nkc, SQ), lambda h, i: (h, 0, i)), pl.BlockSpec((SQ, GW), lambda h, i: (i, h))] + [ANY] * nr,
        out_specs=[pl.BlockSpec((SQ, GW), lambda h, i: (i, h)), ANY, ANY] + [ANY] * nr,
        out_shape=[jax.ShapeDtypeStruct((S, n), BF), jax.ShapeDtypeStruct((S, n), F32), jax.ShapeDtypeStruct((S, n), F32)]
        + (ride.out_shape if ride else []),
        scratch_shapes=[pltpu.VMEM((S, GW), BF), pltpu.VMEM((S, GW), BF), pltpu.VMEM((GW, S), BF), pltpu.VMEM((GW, SQ), F32),
                        pltpu.VMEM((S, GW), F32), pltpu.VMEM((S, GW), F32)] + (_exchange_sems(nr) if ride else []),
    )(m, kb_all, vb_all, kt_all, runs, dy, *(ride.arrays if ride else []))


CW = 3 * HEADS * HD


def _shift_down(cur, prev8, s):
    if s == 0:
        return cur
    r = pltpu.roll(cur, s, 0)
    first = jnp.where(lax.broadcasted_iota(I32, (8, cur.shape[1]), 0) < s, pltpu.roll(prev8, s, 0), r[:8])
    return jnp.concatenate([first, r[8:]], axis=0)


def _shift_up(cur, next8, s):
    if s == 0:
        return cur
    n = cur.shape[0]
    r = pltpu.roll(cur, n - s, 0)
    last = jnp.where(lax.broadcasted_iota(I32, (8, cur.shape[1]), 0) >= 8 - s, pltpu.roll(next8, 8 - s, 0), r[n - 8:])
    return jnp.concatenate([r[:n - 8], last], axis=0)


def _conv_fwd(m, conv_w):
    S = m.shape[0]
    tm = min(512, S)
    hb = tm // 8

    def body(x_ref, p_ref, w_ref, o_ref):
        i = pl.program_id(0)
        cur = x_ref[...]
        prev = jnp.where(i > 0, p_ref[...], 0.0)
        w = w_ref[...]
        acc = cur * w[3:4]
        for jk in range(3):
            acc = acc + _shift_down(cur, prev, 3 - jk) * w[jk:jk + 1]
        o_ref[...] = acc

    return pl.pallas_call(
        body, name="conv_fwd", grid=(S // tm,),
        in_specs=[pl.BlockSpec((tm, CW), lambda i: (i, 1)), pl.BlockSpec((8, CW), lambda i: (jnp.maximum(i * hb - 1, 0), 1)),
                  pl.BlockSpec((4, CW), lambda i: (0, 0))],
        out_specs=pl.BlockSpec((tm, CW), lambda i: (i, 0)),
        out_shape=jax.ShapeDtypeStruct((S, CW), F32),
    )(m, m, conv_w)


def _conv_bwd(m, dyc, conv_w):
    S = m.shape[0]
    tm = min(512, S)
    hb = tm // 8
    nt = S // tm

    def body(x_ref, p_ref, d_ref, n_ref, w_ref, dx_ref, dw_ref):
        i = pl.program_id(0)
        cur = x_ref[...]
        prev = jnp.where(i > 0, p_ref[...], 0.0)
        d = d_ref[...]
        nxt = jnp.where(i < nt - 1, n_ref[...], 0.0)
        w = w_ref[...]
        acc = d * w[3:4]
        dws = []
        for jk in range(3):
            acc = acc + _shift_up(d, nxt, 3 - jk) * w[jk:jk + 1]
            dws.append(jnp.sum(d * _shift_down(cur, prev, 3 - jk), axis=0, keepdims=True))
        dws.append(jnp.sum(d * cur, axis=0, keepdims=True))
        dx_ref[...] = acc.astype(dx_ref.dtype)
        dw = jnp.concatenate(dws, axis=0)

        @pl.when(i == 0)
        def _():
            dw_ref[...] = dw

        @pl.when(i > 0)
        def _():
            dw_ref[...] += dw

    return pl.pallas_call(
        body, name="conv_bwd", grid=(nt,),
        in_specs=[pl.BlockSpec((tm, CW), lambda i: (i, 1)), pl.BlockSpec((8, CW), lambda i: (jnp.maximum(i * hb - 1, 0), 1)),
                  pl.BlockSpec((tm, CW), lambda i: (i, 0)),
                  pl.BlockSpec((8, CW), lambda i: (jnp.minimum((i + 1) * hb, S // 8 - 1), 0)),
                  pl.BlockSpec((4, CW), lambda i: (0, 0))],
        out_specs=[pl.BlockSpec((tm, CW), lambda i: (i, 0)), pl.BlockSpec((4, CW), lambda i: (0, 0))],
        out_shape=[jax.ShapeDtypeStruct((S, CW), BF), jax.ShapeDtypeStruct((4, CW), F32)],
    )(m, m, dyc, dyc, conv_w)


def _t(a):
    return jnp.swapaxes(a, 1, 2)


def _bdot(a, b):
    return jnp.einsum("hik,hkj->hij", a, b, preferred_element_type=F32)


@jax.custom_vjp
def _mm1(a, b):
    return _bdot(a.astype(BF), b.astype(BF))


def _bdot_nt(a, b):
    return jnp.einsum("hij,hkj->hik", a, b, preferred_element_type=F32)


def _bdot_tn(a, b):
    return jnp.einsum("hki,hkj->hij", a, b, preferred_element_type=F32)


_mm1.defvjp(lambda a, b: (_mm1(a, b), (a.astype(BF), b.astype(BF))),
            lambda res, dc: (_bdot_nt(dc.astype(BF), res[1]), _bdot_tn(res[0], dc.astype(BF))))


@jax.custom_vjp
def _mm1_nt(a, b):
    return _bdot_nt(a.astype(BF), b.astype(BF))


_mm1_nt.defvjp(lambda a, b: (_mm1_nt(a, b), (a.astype(BF), b.astype(BF))),
               lambda res, dc: (_bdot(dc.astype(BF), res[1]), _bdot_tn(dc.astype(BF), res[0])))


@jax.custom_vjp
def _mm1_tn(a, b):
    return _bdot_tn(a.astype(BF), b.astype(BF))


_mm1_tn.defvjp(lambda a, b: (_mm1_tn(a, b), (a.astype(BF), b.astype(BF))),
               lambda res, dc: (_bdot_nt(res[1], dc.astype(BF)), _bdot(res[0], dc.astype(BF))))


def _stack_rows(hi, lo):
    return jnp.concatenate([hi, lo], axis=1)


@jax.custom_vjp
def _mm3(a, b):
    (ah, al), (bh, bl) = _split(a), _split(b)
    n = a.shape[1]
    two = _bdot(_stack_rows(ah, al), bh)
    return two[:, :n] + two[:, n:] + _bdot(ah, bl)


def _mm3_fwd(a, b):
    return _mm3(a, b), (_split(a), _split(b))


def _mm3_bwd(res, dc):
    (ah, al), (bh, bl) = res
    dh, dl = _split(dc)
    n = dc.shape[1]
    two = _bdot_nt(_stack_rows(dh, dl), bh)
    da = two[:, :n] + two[:, n:] + _bdot_nt(dh, bl)
    db = _bdot_tn(jnp.concatenate([ah, ah, al], axis=1), jnp.concatenate([dh, dl, dh], axis=1))
    return da, db


_mm3.defvjp(_mm3_fwd, _mm3_bwd)


@jax.custom_vjp
def _unit_lower_inverse(lower):
    eye = (lax.broadcasted_iota(I32, lower.shape, 1) == lax.broadcasted_iota(I32, lower.shape, 2)).astype(F32)
    pw = -lower
    inv = eye + pw
    pw = _mm3(pw, pw)
    for _ in range(4):
        both = _mm3(jnp.concatenate([inv, pw], axis=1), pw)
        inv, pw = inv + both[:, :CH], both[:, CH:]
    return inv + _mm3(inv, pw)


def _unit_lower_inverse_bwd(inv, d):
    t = _t(inv)
    return (-_mm3(_mm3(t, d), t),)


_unit_lower_inverse.defvjp(lambda lower: (_unit_lower_inverse(lower),) * 2, _unit_lower_inverse_bwd)


def _mm_exact(c3, b):
    hi, lo = _split(b)
    lo2 = (b - hi.astype(F32) - lo.astype(F32)).astype(BF)
    return _bdot(c3, jnp.concatenate([hi, lo, lo2], axis=-2))


@jax.custom_vjp
def _cumsum_rows(b):
    return _mm_exact(_tri3(True), b)


def _tri3(lower):
    ri = lax.broadcasted_iota(I32, (HEADS, CH, CH), 1)
    ci = lax.broadcasted_iota(I32, (HEADS, CH, CH), 2)
    tri = (ri >= ci if lower else ri <= ci).astype(BF)
    return jnp.concatenate([tri, tri, tri], axis=-1)


_cumsum_rows.defvjp(lambda b: (_cumsum_rows(b), None), lambda _, dc: (_mm_exact(_tri3(False), dc),))


CPS = 4


def _gdn_chunk(yc, gz, gba, alog, dtb):
    def heads(t, off):
        return jnp.stack([t[:, off + h * HD: off + (h + 1) * HD] for h in range(HEADS)])

    def cols(t, off):
        return jnp.stack([jnp.broadcast_to(t[:, off + h: off + h + 1], (CH, CH)) for h in range(HEADS)])

    c = yc * _sigmoid(yc)
    q, k, v, zg = heads(c, 0), heads(c, HEADS * HD), heads(c, 2 * HEADS * HD), heads(gz, 0)
    q = q * lax.rsqrt(jnp.sum(q * q, axis=-1, keepdims=True) + RMS_EPS) * (HD ** -0.5)
    k = k * lax.rsqrt(jnp.sum(k * k, axis=-1, keepdims=True) + RMS_EPS)
    beta = cols(_sigmoid(gba), 0)
    g = cols(-jnp.exp(alog) * _softplus(gba + dtb), HEADS)
    ri = lax.broadcasted_iota(I32, (HEADS, CH, CH), 1)
    ci = lax.broadcasted_iota(I32, (HEADS, CH, CH), 2)
    causal, strict = ri >= ci, ri > ci
    gc = _cumsum_rows(g)
    gr = _t(gc)
    decay = jnp.where(causal, jnp.exp(jnp.where(causal, gc - gr, 0.0)), 0.0)
    lower = jnp.where(strict, beta * _mm1_nt(k, k) * decay, 0.0)
    inv = _unit_lower_inverse(lower)
    eg = jnp.exp(gc)
    uw = _mm3(inv, jnp.concatenate([v * beta, k * (beta * eg)], axis=2))
    u, w = uw[:, :, :HD], uw[:, :, HD:]
    qk = jnp.where(causal, _mm1_nt(q, k) * decay, 0.0)
    g_last = gc[:, CH - 1:CH, :]
    return u, jnp.concatenate([w, q * eg], axis=1), qk, k * jnp.exp(g_last - gc), jnp.exp(g_last), zg * _sigmoid(zg)


def _gdn_advance(state, pre, normw):
    u, wq, qk, kd, last, gate = pre
    ws = _mm1(wq, state)
    v_new = u - ws[:, :CH]
    o = ws[:, CH:] + _mm1(qk, v_new)
    new_state = state * last + _mm1_tn(kd, v_new)
    o = o * lax.rsqrt(jnp.mean(o * o, axis=-1, keepdims=True) + RMS_EPS) * normw * gate
    return jnp.concatenate([o[h] for h in range(HEADS)], axis=1), new_state


def _gdn_chunks(state, yc, gz, gba, alog, dtb, normw):
    rows = [slice(c * CH, (c + 1) * CH) for c in range(yc.shape[0] // CH)]
    pres = [_gdn_chunk(yc[r], gz[r], gba[r], alog, dtb) for r in rows]
    outs = []
    for pre in pres:
        o, state = _gdn_advance(state, pre, normw)
        outs.append(o)
    return jnp.concatenate(outs, axis=0), state


def _gdn_fwd(yc, m, alog, dtb, normw):
    S = yc.shape[0]
    RS = CPS * CH
    nch = S // RS

    def body(y_ref, gz_ref, gba_ref, al_ref, dt_ref, nw_ref, o_ref, st_ref, st):
        @pl.when(pl.program_id(0) == 0)
        def _():
            st[...] = jnp.zeros_like(st)

        cur = st[...]
        st_ref[0] = cur
        o, new = _gdn_chunks(cur, y_ref[...], gz_ref[...], gba_ref[...], al_ref[...], dt_ref[...], nw_ref[...])
        o_ref[...] = o
        st[...] = new

    return pl.pallas_call(
        body, name="gdn_fwd", grid=(nch,),
        in_specs=[pl.BlockSpec((RS, CW), lambda n: (n, 0)), pl.BlockSpec((RS, HEADS * HD), lambda n: (n, 6)),
                  pl.BlockSpec((RS, 128), lambda n: (n, 28)), pl.BlockSpec((1, 128), lambda n: (0, 0)),
                  pl.BlockSpec((1, 128), lambda n: (0, 0)), pl.BlockSpec((1, HD), lambda n: (0, 0))],
        out_specs=[pl.BlockSpec((RS, HEADS * HD), lambda n: (n, 0)), pl.BlockSpec((1, HEADS, HD, HD), lambda n: (n, 0, 0, 0))],
        out_shape=[jax.ShapeDtypeStruct((S, HEADS * HD), F32), jax.ShapeDtypeStruct((nch, HEADS, HD, HD), F32)],
        scratch_shapes=[pltpu.VMEM((HEADS, HD, HD), F32)],
    )(yc, m, m, alog, dtb, normw)


def _gdn_bwd(yc, m, alog, dtb, normw, states, dog):
    S = yc.shape[0]
    RS = CPS * CH
    nch = S // RS

    def body(y_ref, gz_ref, gba_ref, al_ref, dt_ref, nw_ref, st_ref, do_ref, dy_ref, dgz_ref, dgba_ref, dal_ref, ddt_ref, dnw_ref, dst):
        n = pl.program_id(0)

        @pl.when(n == 0)
        def _():
            dst[...] = jnp.zeros_like(dst)

        _, vjp = jax.vjp(_gdn_chunks, st_ref[0], y_ref[...], gz_ref[...], gba_ref[...], al_ref[...], dt_ref[...], nw_ref[...])
        d_state, d_y, d_gz, d_gba, d_al, d_dt, d_nw = vjp((do_ref[...], dst[...]))
        dst[...] = d_state
        dy_ref[...] = d_y
        dgz_ref[...] = d_gz.astype(dgz_ref.dtype)
        dgba_ref[...] = d_gba.astype(dgba_ref.dtype)
        for ref, val in ((dal_ref, d_al), (ddt_ref, d_dt), (dnw_ref, d_nw)):
            @pl.when(n == 0)
            def _(ref=ref, val=val):
                ref[...] = val

            @pl.when(n > 0)
            def _(ref=ref, val=val):
                ref[...] += val

    rev = lambda n: nch - 1 - n
    return pl.pallas_call(
        body, name="gdn_bwd", grid=(nch,),
        in_specs=[pl.BlockSpec((RS, CW), lambda n: (rev(n), 0)), pl.BlockSpec((RS, HEADS * HD), lambda n: (rev(n), 6)),
                  pl.BlockSpec((RS, 128), lambda n: (rev(n), 28)), pl.BlockSpec((1, 128), lambda n: (0, 0)),
                  pl.BlockSpec((1, 128), lambda n: (0, 0)), pl.BlockSpec((1, HD), lambda n: (0, 0)),
                  pl.BlockSpec((1, HEADS, HD, HD), lambda n: (rev(n), 0, 0, 0)),
                  pl.BlockSpec((RS, HEADS * HD), lambda n: (rev(n), 0))],
        out_specs=[pl.BlockSpec((RS, CW), lambda n: (rev(n), 0)), pl.BlockSpec((RS, HEADS * HD), lambda n: (rev(n), 0)),
                   pl.BlockSpec((RS, 128), lambda n: (rev(n), 0)), pl.BlockSpec((1, 128), lambda n: (0, 0)),
                   pl.BlockSpec((1, 128), lambda n: (0, 0)), pl.BlockSpec((1, HD), lambda n: (0, 0))],
        out_shape=[jax.ShapeDtypeStruct((S, CW), F32), jax.ShapeDtypeStruct((S, HEADS * HD), BF),
                   jax.ShapeDtypeStruct((S, 128), BF), jax.ShapeDtypeStruct((1, 128), F32),
                   jax.ShapeDtypeStruct((1, 128), F32), jax.ShapeDtypeStruct((1, HD), F32)],
        scratch_shapes=[pltpu.VMEM((HEADS, HD, HD), F32)],
    )(yc, m, m, alog, dtb, normw, states, dog)


def _mixer_fwd(h1, W, late=None):
    S = h1.shape[0]
    tm = min(512, S)
    n = HEADS * HD
    tb_ = min(BIG_ROWS, S)
    m, = _mm("mix_in", h1, W["wp"], S, NP, D, tm=tb_, tn=1536, tk=D, order="ji", outs=[((S, NP), F32, (tb_, 1536), _tile)])
    kb_all, vb_all, kt_all, vt_all = _attn_prep(m)
    ya, runs, *arrived = _attn_fwd_t(m, kb_all, vt_all, ride=late[0] if late else None)
    if late:
        late[1](W, arrived)
    runs = (kb_all, vb_all, kt_all, runs)
    b_gate, conv_w, alog, dtb, normw, w_sb, w_gdn, w_mo, g, b = (
        W[k] for k in ("b_gate", "conv_w", "alog", "dtb", "normw", "w_sb", "w_gdn", "w_mo", "ln2_g", "ln2_b"))
    yc = _conv_fwd(m, conv_w)
    og, states = _gdn_fwd(yc, m, alog, dtb, normw)
    ysb, = _mm("mix_sb", ya, w_sb, S, D, n, tm=tm, tn=D, tk=n, outs=[((S, D), F32, (tm, D), _row)])

    def merge_epi(acc, ys, gs, gg, bg):
        return _sigmoid(gs + bg[:, :D]) * ys + _sigmoid(gg + bg[:, D:]) * acc, acc

    u, ygdn = _mm("mix_gdn", og, w_gdn, S, D, n, tm=tm, tn=D, tk=n,
                  extras=[(ysb, (tm, D), _row), (m, (tm, D), lambda i, j, k: (i, GATE0 // D)),
                          (m, (tm, D), lambda i, j, k: (i, GATE0 // D + 1)), (b_gate, (1, 2 * D), _const)],
                  epilogue=merge_epi, outs=[((S, D), BF, (tm, D), _row), ((S, D), F32, (tm, D), _row)])

    def out_epi(acc, xin, gg, bb):
        r = ALPHA * xin + acc
        return _layer_norm(r, gg, bb), r

    h2, r2 = _mm("mix_out", u, w_mo, S, D, D, tm=tm, tn=D, tk=D,
                 extras=[(h1, (tm, D), _row), (g, (1, D), _const), (b, (1, D), _const)], epilogue=out_epi,
                 outs=[((S, D), F32, (tm, D), _row), ((S, D), F32, (tm, D), _row)])
    return h2, (h1, m, ya, runs, yc, og, states, ysb, ygdn, u, r2)


def _mixer_bwd(saved, W, dh, ride=None):
    h1, m, ya, runs, yc, og, states, ysb, ygdn, u, r2 = saved
    wp, b_gate, conv_w, alog, dtb, normw, w_sb, w_gdn, w_mo, g = (
        W[k] for k in ("wp", "b_gate", "conv_w", "alog", "dtb", "normw", "w_sb", "w_gdn", "w_mo", "ln2_g"))
    S = h1.shape[0]
    tm = min(512, S)
    n = HEADS * HD
    dr, dg, db = _ln_bwd("mix", r2, g, dh)

    def merge_epi(du, ys, yg, gs, gg, bg):
        s1, s2 = _sigmoid(gs + bg[:, :D]), _sigmoid(gg + bg[:, D:])
        dgate = jnp.concatenate([du * ys * s1 * (1.0 - s1), du * yg * s2 * (1.0 - s2)], axis=1)
        return du * s1, du * s2, dgate, jnp.sum(dgate, axis=0, keepdims=True)

    dysb, dygdn, dgate, d_bg = _mm(
        "mix_dmerge", dr, w_mo, S, D, D, tm=tm, tn=D, tk=D, tb=True,
        extras=[(ysb, (tm, D), _row), (ygdn, (tm, D), _row), (m, (tm, D), lambda i, j, k: (i, GATE0 // D)),
                (m, (tm, D), lambda i, j, k: (i, GATE0 // D + 1)), (b_gate, (1, 2 * D), _const)],
        epilogue=merge_epi, n_acc=1,
        outs=[((S, D), BF, (tm, D), _row), ((S, D), BF, (tm, D), _row), ((S, 2 * D), BF, (tm, 2 * D), _row),
              ((1, 2 * D), F32, (1, 2 * D), _const)])
    d_w_mo, = _mm("mix_dwmo", u, dr, D, D, S, tm=D, tn=D, tk=min(BIG_ROWS, S), ta=True, outs=[((D, D), F32, (D, D), _tile)])
    dya, = _mm("mix_dya", dysb, w_sb, S, n, D, tm=tm, tn=n, tk=D, tb=True, outs=[((S, n), F32, (tm, n), _row)])
    col_shards = [((NSH, n, D // NSH), F32, (None, n, D // NSH), lambda i, j, k: (j, 0, 0))]
    d_w_sb, = _mm("mix_dwsb", ya, dysb, n, D, S, tm=n, tn=D // NSH, tk=min(BIG_ROWS, S), ta=True, order="ji", outs=col_shards)
    dog, = _mm("mix_dog", dygdn, w_gdn, S, n, D, tm=tm, tn=n, tk=D, tb=True, outs=[((S, n), F32, (tm, n), _row)])
    d_w_gdn, = _mm("mix_dwgdn", og, dygdn, n, D, S, tm=n, tn=D // NSH, tk=min(BIG_ROWS, S), ta=True, order="ji", outs=col_shards)
    dq, dk, dv, *arrived = _attn_bwd_t(m, *runs, dya, ride=ride)
    dyc, dgz, dgba, d_alog, d_dtb, d_normw = _gdn_bwd(yc, m, alog, dtb, normw, states, dog)
    dxc, d_conv = _conv_bwd(m, dyc, conv_w)
    dm = jnp.concatenate([dq, dk.astype(BF), dv.astype(BF), dxc, dgz, dgba, jnp.zeros((S, GATE0 - 3712), BF), dgate], axis=1)
    tb_ = min(BIG_ROWS, S)
    d_h1, = _mm("mix_dh", dm, wp, S, D, NP, tm=tb_, tn=D, tk=1536, tb=True,
                extras=[(dr, (tb_, D), _row)], epilogue=lambda acc, d: (acc + ALPHA * d,),
                outs=[((S, D), F32, (tb_, D), _row)])
    d_wp, = _mm("mix_dwp", h1, dm, D, NP, S, tm=D, tn=1536, tk=tb_, ta=True, order="ji",
                outs=[((D, NP), F32, (D, 1536), _tile)])
    return d_h1, dict(wp=d_wp, b_gate=d_bg, conv_w=d_conv, alog=d_alog, dtb=d_dtb, normw=d_normw,
                      w_sb=d_w_sb, w_gdn=d_w_gdn, w_mo=d_w_mo, g=dg, b=db), arrived


def _ple_fwd(h3, p, w_pg, b_pg, w_ple, g, b, target):
    S = h3.shape[0]
    tm = min(512, S)
    pd = p.shape[1]
    pe, = _mm("ple_emb", p, w_ple, S, D, pd, tm=tm, tn=D, tk=pd, outs=[((S, D), F32, (tm, D), _row)])

    def epi(acc, e, xin, tgt, bp, gg, bb):
        gt = _sigmoid(acc + bp)
        r = ALPHA * xin + gt * e
        diff = _layer_norm(r, gg, bb) - tgt
        return gt, r, diff * (1.0 / D), jnp.sum(diff * diff, axis=0, keepdims=True)

    gt, r4, dh4, loss_row = _mm(
        "ple_gate", h3, w_pg, S, D, D, tm=tm, tn=D, tk=D,
        extras=[(pe, (tm, D), _row), (h3, (tm, D), _row), (target, (tm, D), _row), (b_pg, (1, D), _const),
                (g, (1, D), _const), (b, (1, D), _const)], epilogue=epi, n_acc=1,
        outs=[((S, D), F32, (tm, D), _row), ((S, D), F32, (tm, D), _row), ((S, D), F32, (tm, D), _row),
              ((1, D), F32, (1, D), _const)])
    return dh4, loss_row, (h3, p, pe, gt, r4)


def _ple_bwd(saved, w_pg, g, dh4):
    h3, p, pe, gt, r4 = saved
    S = h3.shape[0]
    tm = min(512, S)
    pd = p.shape[1]

    def fn(r, gg, dh, e, t):
        dr, dg, db = _layer_norm_bwd(r, gg, dh)
        dpre = dr * e * t * (1.0 - t)
        return dr, dpre, dr * t, dg, db, jnp.sum(dpre, axis=0, keepdims=True)

    row, one = (lambda i: (i, 0)), (lambda i: (0, 0))
    dr, dpre, dpe, dg, db, d_bpg = _rows(
        "ple_lnbwd", fn, S // tm,
        [(r4, (tm, D), row), (g, (1, D), one), (dh4, (tm, D), row), (pe, (tm, D), row), (gt, (tm, D), row)],
        [((S, D), F32, (tm, D), row), ((S, D), BF, (tm, D), row), ((S, D), BF, (tm, D), row),
         ((1, D), F32, (1, D), one), ((1, D), F32, (1, D), one), ((1, D), F32, (1, D), one)], n_acc=3)
    d_w_pg, = _mm("ple_dwpg", h3, dpre, D, D, S, tm=D, tn=D, tk=min(BIG_ROWS, S), ta=True, outs=[((D, D), F32, (D, D), _tile)])
    d_w_ple, = _mm("ple_dwple", p, dpe, pd, D, S, tm=pd, tn=D // NSH, tk=min(BIG_ROWS, S), ta=True, order="ji",
                   outs=[((NSH, pd, D // NSH), F32, (None, pd, D // NSH), lambda i, j, k: (j, 0, 0))])
    d_h3, = _mm("ple_dh", dpre, w_pg, S, D, D, tm=tm, tn=D, tk=D, tb=True,
                extras=[(dr, (tm, D), _row)], epilogue=lambda acc, d: (acc + ALPHA * d,),
                outs=[((S, D), F32, (tm, D), _row)])
    return d_h3, d_w_pg, d_bpg, d_w_ple, dg, db


def _local_step(x, p, target, W, now=None, soon=None, late=None, early=None, mid=None):
    W = dict(W)
    h1, sv1, arrived, W["ffn1_out"] = _ffn_fwd("ffn1", x, W["ffn1_in"], W.get("ffn1_out"), W["ln1_g"], W["ln1_b"],
                                               ride=soon[0] if soon else None, ride2=soon[1] if soon else None,
                                               w_out_ride=now)
    if soon:
        soon[2](W, arrived)
    h2, sv2 = _mixer_fwd(h1, W, late)
    h3, sv3, _, _ = _ffn_fwd("ffn2", h2, W["ffn2_in"], W["ffn2_out"], W["ln3_g"], W["ln3_b"])
    dh4, loss_row, sv4 = _ple_fwd(h3, p, W["w_pg"], W["b_pg"], W["w_ple"], W["ln4_g"], W["ln4_b"], target)
    G = {}
    dh3, G["w_pg"], G["b_pg"], G["w_ple"], G["ln4_g"], G["ln4_b"] = _ple_bwd(sv4, W["w_pg"], W["ln4_g"], dh4)
    dh2, G["ffn2_in"], G["ffn2_out"], G["ln3_g"], G["ln3_b"], _ = _ffn_bwd("ffn2", sv3, W["ffn2_in"], W["ffn2_out"], W["ln3_g"], dh3)
    dh1, gm, G["early"] = _mixer_bwd(sv2, W, dh2, ride=early(G) if early else None)
    G.update({k: v for k, v in gm.items() if k not in ("g", "b")})
    G["ln2_g"], G["ln2_b"] = gm["g"], gm["b"]
    dx, G["ffn1_in"], G["ffn1_out"], G["ln1_g"], G["ln1_b"], G["mid"] = _ffn_bwd(
        "ffn1", sv1, W["ffn1_in"], W["ffn1_out"], W["ln1_g"], dh1, ride=mid(G) if mid else None)
    return loss_row, dx, G


S2 = CUT - 2 * MSH


def _pack_wp(w4):
    tr = 256

    def fn(w):
        s = [w[j].astype(F32) for j in range(NSH)]
        full = jnp.concatenate([s[0], s[1], s[2][:, :S2], jnp.zeros((tr, GATE0 - CUT), F32), s[2][:, S2:], s[3]], axis=1)
        return (full,)

    return _rows("pack_wp", fn, D // tr, [(w4, (NSH, tr, MSH), lambda i: (0, i, 0))],
                 [((D, NP), BF, (tr, NP), lambda i: (i, 0))])[0]


def _unpack_wp(d):
    tr = 256
    g2 = GATE0 + MSH - S2

    def fn(v):
        return (jnp.stack([v[:, :MSH], v[:, MSH:2 * MSH], jnp.concatenate([v[:, 2 * MSH:CUT], v[:, GATE0:g2]], axis=1), v[:, g2:]]),)

    return _rows("unpack_wp", fn, D // tr, [(d, (tr, NP), lambda i: (i, 0))],
                 [((NSH, D, MSH), F32, (NSH, tr, MSH), lambda i: (0, i, 0))])[0]


def _cast_bf16(tag, w):
    r, c = w.shape
    tr = _pick(r, 256)
    return _rows(f"cast_{tag}", lambda v: (v,), r // tr, [(w, (tr, c), lambda i: (i, 0))],
                 [((r, c), BF, (tr, c), lambda i: (i, 0))])[0]


def _place():
    return lax.axis_index("x"), lax.axis_index("y"), lax.axis_index("c")


def _chip_exchange(phase, scatter, ins, outs, send, recv, loc):
    x, y, c = _place()
    me = 2 * x + y
    chips = [(1 - x, y), (x, 1 - y), (1 - x, 1 - y)]
    for t in range(len(ins)):
        own = pltpu.make_async_copy(ins[t].at[me] if scatter else ins[t], outs[t].at[me], loc.at[t])
        out_going, in_coming = [], []
        for q, (px, py) in enumerate(chips):
            src = ins[t].at[2 * px + py] if scatter else ins[t]
            sems = dict(send_sem=send.at[3 * t + q], recv_sem=recv.at[3 * t + q], device_id=(px, py, c), device_id_type=MESH)
            out_going.append(pltpu.make_async_remote_copy(src_ref=src, dst_ref=outs[t].at[me], **sems))
            in_coming.append(pltpu.make_async_remote_copy(src_ref=src, dst_ref=outs[t].at[2 * px + py], **sems))
        if phase == "start":
            own.start()
            for cp in out_going:
                cp.start()
        else:
            for cp in in_coming:
                cp.wait_recv()
            own.wait()
            for cp in out_going:
                cp.wait_send()


def _exchange_sems(n):
    return [pltpu.SemaphoreType.DMA((3 * n,)), pltpu.SemaphoreType.DMA((3 * n,)), pltpu.SemaphoreType.DMA((n,))]


class _Ride:
    def __init__(self, scatter, arrays):
        self.scatter, self.arrays, self.n = scatter, list(arrays), len(arrays)
        self.out_shape = [jax.ShapeDtypeStruct(a.shape if scatter else (NSH,) + a.shape, a.dtype) for a in self.arrays]

    def run(self, phase, refs_in, refs_out, sems):
        _chip_exchange(phase, self.scatter, refs_in, refs_out, *sems)


def _gather_shards(shards):
    n = len(shards)

    def body(*refs):
        _chip_exchange("start", False, refs[:n], refs[n:2 * n], *refs[2 * n:])
        _chip_exchange("wait", False, refs[:n], refs[n:2 * n], *refs[2 * n:])

    return pl.pallas_call(
        body, name="gather_weights", in_specs=[ANY] * n, out_specs=[ANY] * n,
        out_shape=[jax.ShapeDtypeStruct((NSH,) + s.shape, s.dtype) for s in shards], scratch_shapes=_exchange_sems(n),
    )(*shards)


def _reduce_chips(ps):
    n = len(ps)

    def body(*refs):
        _chip_exchange("start", True, refs[:n], refs[n:2 * n], *refs[2 * n:])
        _chip_exchange("wait", True, refs[:n], refs[n:2 * n], *refs[2 * n:])

    return pl.pallas_call(
        body, name="reduce_chips", in_specs=[ANY] * n, out_specs=[ANY] * n,
        out_shape=[jax.ShapeDtypeStruct(p_.shape, p_.dtype) for p_ in ps], scratch_shapes=_exchange_sems(n),
    )(*ps)


def _chunk_rows(h, c):
    return _pick(h, max(8, 524288 // c // 8 * 8))


def _sibling_sum(tag, g):
    _, r, c = g.shape
    h = r // 2
    tr = _chunk_rows(h, c)
    nch = h // tr
    steps = NSH * nch

    def body(top_ref, bot_ref, out_ref, narrow_ref, stage, land, send, recv):
        s = pl.program_id(0)
        x, y, core = _place()

        def exchange(keep_ref, give_ref):
            stage[...] = give_ref[0].astype(BF)
            cp = pltpu.make_async_remote_copy(src_ref=stage, dst_ref=land.at[s], send_sem=send.at[s], recv_sem=recv.at[s],
                                              device_id=(x, y, 1 - core), device_id_type=MESH)
            cp.start()
            cp.wait()
            total = keep_ref[0] + land[s].astype(F32)
            out_ref[0] = total
            narrow_ref[0] = total.astype(BF)

        @pl.when(core == 0)
        def _():
            exchange(top_ref, bot_ref)

        @pl.when(core == 1)
        def _():
            exchange(bot_ref, top_ref)

    return pl.pallas_call(
        body, name=f"sibling_sum_{tag}", grid=(steps,),
        in_specs=[pl.BlockSpec((1, tr, c), lambda s: (s // nch, s % nch, 0)),
                  pl.BlockSpec((1, tr, c), lambda s: (s // nch, nch + s % nch, 0))],
        out_specs=[pl.BlockSpec((1, tr, c), lambda s: (s // nch, s % nch, 0))] * 2,
        out_shape=[jax.ShapeDtypeStruct((NSH, h, c), F32), jax.ShapeDtypeStruct((NSH, h, c), BF)],
        scratch_shapes=[pltpu.VMEM((tr, c), BF), pltpu.VMEM((steps, tr, c), BF), pltpu.SemaphoreType.DMA((steps,)),
                        pltpu.SemaphoreType.DMA((steps,))],
    )(g, g)


def _chip_sum_share(tag, b, own):
    _, h, c = b.shape
    tr = _chunk_rows(h, c)
    steps = h // tr

    def body(b_ref, own_ref, out_ref, stage, land, send, recv):
        s = pl.program_id(0)
        x, y, core = _place()
        me = 2 * x + y
        v = [jnp.where(me == j, own_ref[j], b_ref[j].astype(F32)) for j in range(NSH)]
        total = ((v[0] + v[1]) + v[2]) + v[3]
        stage[...] = total
        cp = pltpu.make_async_remote_copy(src_ref=stage, dst_ref=land.at[s], send_sem=send.at[s], recv_sem=recv.at[s],
                                          device_id=(x, y, 1 - core), device_id_type=MESH)
        cp.start()
        cp.wait()
        out_ref[core] = total
        out_ref[1 - core] = land[s]

    return pl.pallas_call(
        body, name=f"chip_sum_share_{tag}", grid=(steps,),
        in_specs=[pl.BlockSpec((NSH, tr, c), lambda s: (0, s, 0))] * 2,
        out_specs=pl.BlockSpec((2, tr, c), lambda s: (0, s, 0)),
        out_shape=jax.ShapeDtypeStruct((2, h, c), F32),
        scratch_shapes=[pltpu.VMEM((tr, c), F32), pltpu.VMEM((steps, tr, c), F32), pltpu.SemaphoreType.DMA((steps,)),
                        pltpu.SemaphoreType.DMA((steps,))],
    )(b, own)


NDEV = 8


def _allreduce_small(pack):
    r, w = pack.shape
    rel = [(dx, dy, dc) for dx in (0, 1) for dy in (0, 1) for dc in (0, 1) if (dx, dy, dc) != (0, 0, 0)]

    def body(in_ref, out_ref, buf, send, recv):
        x, y, c = _place()
        me = 4 * x + 2 * y + c
        buf[me] = in_ref[...]
        peers = [((x + dx) % 2, (y + dy) % 2, (c + dc) % 2) for dx, dy, dc in rel]
        sent = []
        for k, peer in enumerate(peers):
            cp = pltpu.make_async_remote_copy(src_ref=in_ref, dst_ref=buf.at[me], send_sem=send.at[k], recv_sem=recv.at[k],
                                              device_id=peer, device_id_type=MESH)
            cp.start()
            sent.append(cp)
        for k, (px, py, pc) in enumerate(peers):
            pltpu.make_async_remote_copy(src_ref=in_ref, dst_ref=buf.at[4 * px + 2 * py + pc], send_sem=send.at[k], recv_sem=recv.at[k],
                                         device_id=(px, py, pc), device_id_type=MESH).wait_recv()
        for cp in sent:
            cp.wait_send()
        acc = buf[0]
        for k in range(1, NDEV):
            acc = acc + buf[k]
        out_ref[...] = acc

    vm = pl.BlockSpec(memory_space=pltpu.VMEM)
    return pl.pallas_call(
        body, name="allreduce_small", in_specs=[vm], out_specs=vm, out_shape=jax.ShapeDtypeStruct((r, w), F32),
        scratch_shapes=[pltpu.VMEM((NDEV, r, w), F32), pltpu.SemaphoreType.DMA((NDEV - 1,)), pltpu.SemaphoreType.DMA((NDEV - 1,))],
    )(pack)


def _adamw(tag, w, g, m, v):
    r, c = w.shape
    tr = _pick(r, max(8, 262144 // c // 8 * 8))

    def fn(w_, g_, m_, v_):
        m2 = B1 * m_ + (1.0 - B1) * g_
        v2 = B2 * v_ + (1.0 - B2) * (g_ * g_)
        m_hat = m2 / (1.0 - B1 ** STEP)
        v_hat = v2 / (1.0 - B2 ** STEP)
        return -LR * (m_hat / (jnp.sqrt(v_hat) + EPS) + WD * w_), m2, v2

    spec = ((tr, c), lambda i: (i, 0))
    return _rows(f"adamw_{tag}", fn, r // tr, [(a,) + spec for a in (w, g, m, v)], [((r, c), F32) + spec] * 3)


BIG = ("ffn1_w_in", "ffn1_w_out", "w_mix_in", "w_branch_sb", "w_branch_gdn", "w_mix_out", "ffn2_w_in", "ffn2_w_out",
       "w_ple_gate", "w_ple")
FIRST = ("ffn1_w_in",)
SOON = ("w_mix_in",)
LATER = tuple(n for n in BIG if n not in FIRST + SOON + ("ffn1_w_out",))
EARLY = ("ffn2_w_in", "ffn2_w_out", "w_ple_gate", "w_ple")
MID = ("w_mix_in", "w_branch_sb", "w_branch_gdn", "w_mix_out")
LAST = ("ffn1_w_in", "ffn1_w_out")
SMALL = ("ln1_g", "ln1_b", "b_gate", "conv_w", "a_log", "dt_bias", "gdn_norm_w", "ln2_g", "ln2_b", "ln3_g", "ln3_b",
         "b_ple_gate", "ln4_g", "ln4_b")
ORDER = ("ffn1_w_in", "ffn1_w_out", "ln1_g", "ln1_b", "w_mix_in", "b_gate", "conv_w", "a_log", "dt_bias", "gdn_norm_w",
         "w_branch_sb", "w_branch_gdn", "w_mix_out", "ln2_g", "ln2_b", "ffn2_w_in", "ffn2_w_out", "ln3_g", "ln3_b",
         "w_ple_gate", "b_ple_gate", "w_ple", "ln4_g", "ln4_b")
PACK_W = 2304


def _lane_row(v, lanes=128, at=HEADS):
    return jnp.pad(v[None, :], ((0, 0), (at, lanes - at - v.shape[0])))


def _col_join(w4):
    return jnp.transpose(w4, (1, 0, 2)).reshape(w4.shape[1], NSH * w4.shape[2])


def kernel(x, p, ffn1_w_in, ffn1_w_out, ln1_g, ln1_b, w_mix_in, b_gate, conv_w, a_log, dt_bias, gdn_norm_w, w_branch_sb, w_branch_gdn, w_mix_out, ln2_g, ln2_b, ffn2_w_in, ffn2_w_out, ln3_g, ln3_b, w_ple_gate, b_ple_gate, w_ple, ln4_g, ln4_b, loss_target, m_ffn1_w_in, m_ffn1_w_out, m_ln1_g, m_ln1_b, m_w_mix_in, m_b_gate, m_conv_w, m_a_log, m_dt_bias, m_gdn_norm_w, m_w_branch_sb, m_w_branch_gdn, m_w_mix_out, m_ln2_g, m_ln2_b, m_ffn2_w_in, m_ffn2_w_out, m_ln3_g, m_ln3_b, m_w_ple_gate, m_b_ple_gate, m_w_ple, m_ln4_g, m_ln4_b, v_ffn1_w_in, v_ffn1_w_out, v_ln1_g, v_ln1_b, v_w_mix_in, v_b_gate, v_conv_w, v_a_log, v_dt_bias, v_gdn_norm_w, v_w_branch_sb, v_w_branch_gdn, v_w_mix_out, v_ln2_g, v_ln2_b, v_ffn2_w_in, v_ffn2_w_out, v_ln3_g, v_ln3_b, v_w_ple_gate, v_b_ple_gate, v_w_ple, v_ln4_g, v_ln4_b):
    args = dict(locals())
    w = {n: args[n][0] for n in ORDER}
    mom = {n: args["m_" + n][0] for n in ORDER}
    var = {n: args["v_" + n][0] for n in ORDER}

    cast = {n: _cast_bf16(n, w[n]) for n in BIG if n not in SOON}
    mix_halves = [_cast_bf16(f"w_mix_in_{t}", w["w_mix_in"][t * (D // 2):(t + 1) * (D // 2)]) for t in range(2)]
    full = dict(zip(FIRST + ("conv_w",), _gather_shards([cast[n] for n in FIRST] + [w["conv_w"]])))
    W = dict(
        ffn1_in=full["ffn1_w_in"], conv_w=_col_join(full["conv_w"]), b_gate=w["b_gate"][None], alog=_lane_row(w["a_log"]), dtb=_lane_row(w["dt_bias"]),
        normw=w["gdn_norm_w"][None], b_pg=w["b_ple_gate"][None],
        **{f"ln{i}_{s}": w[f"ln{i}_{s}"][None] for i in (1, 2, 3, 4) for s in ("g", "b")},
    )

    def fill(W_, arrived):
        got = dict(zip(LATER, arrived))
        W_.update(w_sb=_col_join(got["w_branch_sb"]), w_gdn=_col_join(got["w_branch_gdn"]), w_mo=got["w_mix_out"].reshape(D, D),
                  ffn2_in=got["ffn2_w_in"], ffn2_out=got["ffn2_w_out"].reshape(DFF, D),
                  w_pg=got["w_ple_gate"].reshape(D, D), w_ple=_col_join(got["w_ple"]))

    def by_shard(G_, names):
        forms = dict(
            ffn1_w_in=lambda: G_["ffn1_in"], ffn1_w_out=lambda: G_["ffn1_out"].reshape(NSH, DFF // NSH, D),
            w_mix_in=lambda: _unpack_wp(G_["wp"]), w_branch_sb=lambda: G_["w_sb"], w_branch_gdn=lambda: G_["w_gdn"],
            w_mix_out=lambda: G_["w_mo"].reshape(NSH, D // NSH, D), ffn2_w_in=lambda: G_["ffn2_in"],
            ffn2_w_out=lambda: G_["ffn2_out"].reshape(NSH, DFF // NSH, D),
            w_ple_gate=lambda: G_["w_pg"].reshape(NSH, D // NSH, D), w_ple=lambda: G_["w_ple"])
        return [_sibling_sum(n, forms[n]()) for n in names]

    early_sums, mid_sums = [], []

    def early(G_):
        early_sums.extend(by_shard(G_, EARLY))
        return _Ride(True, [narrow for _, narrow in early_sums])

    def mid(G_):
        mid_sums.extend(by_shard(G_, MID))
        return _Ride(True, [narrow for _, narrow in mid_sums])

    loss_row, grad_x, G = _local_step(
        x[0], p[0, 0], loss_target[0], W, now=_Ride(False, [cast["ffn1_w_out"]]),
        soon=(_Ride(False, mix_halves[:1]), _Ride(False, mix_halves[1:]),
              lambda W_, arrived: W_.update(wp=_pack_wp(jnp.concatenate(arrived, axis=1)))),
        late=(_Ride(False, [cast[n] for n in LATER]), fill), early=early, mid=mid)
    loss = lax.psum(0.5 * jnp.sum(loss_row) / D, ("x", "y", "c"))

    last_sums = by_shard(G, LAST)
    landed = list(G["early"]) + list(G["mid"]) + list(_reduce_chips([narrow for _, narrow in last_sums]))
    grad = {n: _chip_sum_share(n, b, own).reshape(w[n].shape)
            for n, b, (own, _) in zip(EARLY + MID + LAST, landed, early_sums + mid_sums + last_sums)}

    pieces = [G["ln1_g"], G["ln1_b"], G["b_gate"], G["conv_w"].reshape(1, 4 * CW), G["alog"], G["dtb"], G["normw"],
              G["ln2_g"], G["ln2_b"], G["ln3_g"], G["ln3_b"], G["b_pg"], G["ln4_g"], G["ln4_b"]]
    flat = jnp.concatenate(pieces, axis=1)
    flat = jnp.pad(flat, ((0, 0), (0, NDEV * PACK_W - flat.shape[1])))
    total = _allreduce_small(flat.reshape(NDEV, PACK_W)).reshape(1, NDEV * PACK_W)
    off = 0
    for n, piece in zip(SMALL, pieces):
        grad[n] = total[0, off:off + piece.shape[1]]
        off += piece.shape[1]
    chip = 2 * lax.axis_index("x") + lax.axis_index("y")
    grad["conv_w"] = lax.dynamic_slice_in_dim(grad["conv_w"].reshape(4, CW), chip * (CW // NSH), CW // NSH, axis=1)
    grad["a_log"] = grad["a_log"][HEADS:2 * HEADS]
    grad["dt_bias"] = grad["dt_bias"][HEADS:2 * HEADS]

    delta, new_m, new_v = {}, {}, {}
    for n in ORDER:
        shape2 = w[n].shape if w[n].ndim == 2 else (1, w[n].shape[0])
        d_, m_, v_ = _adamw(n, *[a.reshape(shape2) for a in (w[n], grad[n], mom[n], var[n])])
        delta[n], new_m[n], new_v[n] = (a.reshape(args[n].shape) for a in (d_, m_, v_))
    outs = [loss, grad_x[None]]
    outs += [grad[n].reshape(args[n].shape) for n in ORDER]
    for group in (delta, new_m, new_v):
        outs += [group[n] for n in ORDER]
    return tuple(outs)
```

```python
import jax
import jax.numpy as jnp
from jax import lax
from jax.experimental import pallas as pl
from jax.experimental.pallas import tpu as pltpu

F32 = jnp.float32
BF = jnp.bfloat16
I32 = jnp.int32
MESH = pl.DeviceIdType.MESH
ANY = pl.BlockSpec(memory_space=pl.ANY)

D = 1024
DFF = 2816
NSH = 4
FSH = 2 * DFF // NSH
NIN = 5648
MSH = NIN // NSH
NP = 6144
CUT = 3600
GATE0 = 4096
HEADS = 8
HD = 64
CH = 64
KB = 128
BIG_ROWS = 1024
ALPHA = 2.0 ** 0.25
LN_EPS = 1e-5
RMS_EPS = 1e-6
B1, B2, LR, EPS, WD, STEP = 0.9, 0.999, 0.001, 1e-08, 0.01, 10


def _sigmoid(x):
    return 0.5 * jnp.tanh(0.5 * x) + 0.5


def _softplus(x):
    return jnp.maximum(x, 0.0) + jnp.log1p(jnp.exp(-jnp.abs(x)))


def _layer_norm(r, g, b):
    mu = jnp.mean(r, axis=-1, keepdims=True)
    xc = r - mu
    var = jnp.mean(xc * xc, axis=-1, keepdims=True)
    return xc * lax.rsqrt(var + LN_EPS) * g + b


def _layer_norm_bwd(r, g, dh):
    mu = jnp.mean(r, axis=-1, keepdims=True)
    xc = r - mu
    var = jnp.mean(xc * xc, axis=-1, keepdims=True)
    xhat = xc * lax.rsqrt(var + LN_EPS)
    dxh = dh * g
    dr = lax.rsqrt(var + LN_EPS) * (dxh - jnp.mean(dxh, axis=-1, keepdims=True) - xhat * jnp.mean(dxh * xhat, axis=-1, keepdims=True))
    return dr, jnp.sum(dh * xhat, axis=0, keepdims=True), jnp.sum(dh, axis=0, keepdims=True)


def _pick(n, cap):
    if n <= cap:
        return n
    for t in range(cap - cap % 8, 7, -8):
        if n % t == 0:
            return t
    raise ValueError((n, cap))


def _mm(name, a, b, M, N, K, *, tm, tn, tk, ta=False, tb=False, a_spec=None, b_spec=None, order="ij",
        extras=(), epilogue=None, outs, n_acc=0, ride=None):
    ni, nj, nk = M // tm, N // tn, K // tk
    nr = ride.n if ride else 0
    assert M % tm == 0 and N % tn == 0 and K % tk == 0, (name, M, N, K, tm, tn, tk)
    assert n_acc == 0 or nj == 1

    def wrap(fn):
        if order == "ij":
            return lambda g0, g1, g2: fn(g0, g1, g2)
        return lambda g0, g1, g2: fn(g1, g0, g2)

    if a_spec is None:
        a_spec = ((tk, tm), lambda i, j, k: (k, i)) if ta else ((tm, tk), lambda i, j, k: (i, k))
    if b_spec is None:
        b_spec = ((tn, tk), lambda i, j, k: (j, k)) if tb else ((tk, tn), lambda i, j, k: (k, j))
    dims = (((0 if ta else 1,), (1 if tb else 0,)), ((), ()))
    ne, no = len(extras), len(outs)
    grid = (ni, nj, nk) if order == "ij" else (nj, ni, nk)

    def body(*refs):
        a_ref, b_ref = refs[0], refs[1]
        ex = refs[2:2 + ne]
        ride_in = refs[2 + ne:2 + ne + nr]
        o = refs[2 + ne + nr:2 + ne + nr + no]
        ride_out = refs[2 + ne + nr + no:2 + ne + 2 * nr + no]
        scratch = refs[2 + ne + 2 * nr + no:]
        g0, g1, k = pl.program_id(0), pl.program_id(1), pl.program_id(2)
        first = jnp.logical_and(g0 == 0, g1 == 0)
        if ride:
            @pl.when(jnp.logical_and(first, k == 0))
            def _():
                ride.run("start", ride_in, ride_out, scratch[-3:])

        p = lax.dot_general(a_ref[...].astype(BF), b_ref[...].astype(BF), dims, preferred_element_type=F32)

        def finish(acc):
            vals = (acc,) if epilogue is None else epilogue(acc, *[e[...] for e in ex])
            for idx, (ref, val) in enumerate(zip(o, vals)):
                if idx < no - n_acc:
                    ref[...] = val.astype(ref.dtype)
                else:
                    @pl.when(first)
                    def _(ref=ref, val=val):
                        ref[...] = val

                    @pl.when(jnp.logical_not(first))
                    def _(ref=ref, val=val):
                        ref[...] += val

        if nk == 1:
            finish(p)
        else:
            acc_ref = scratch[0]

            @pl.when(k == 0)
            def _():
                acc_ref[...] = p

            @pl.when(k > 0)
            def _():
                acc_ref[...] += p

            @pl.when(k == nk - 1)
            def _():
                finish(acc_ref[...])

        if ride:
            @pl.when(jnp.logical_and(jnp.logical_and(g0 == grid[0] - 1, g1 == grid[1] - 1), k == nk - 1))
            def _():
                ride.run("wait", ride_in, ride_out, scratch[-3:])

    in_specs = [pl.BlockSpec(a_spec[0], wrap(a_spec[1])), pl.BlockSpec(b_spec[0], wrap(b_spec[1]))]
    in_specs += [pl.BlockSpec(blk, wrap(fn)) for _, blk, fn in extras] + [ANY] * nr
    res = pl.pallas_call(
        body, name=name, grid=grid, in_specs=in_specs,
        out_specs=[pl.BlockSpec(blk, wrap(fn)) for _, _, blk, fn in outs] + [ANY] * nr,
        out_shape=[jax.ShapeDtypeStruct(shape, dt) for shape, dt, _, _ in outs] + (ride.out_shape if ride else []),
        scratch_shapes=([pltpu.VMEM((tm, tn), F32)] if nk > 1 else []) + (_exchange_sems(nr) if ride else []),
    )(a, b, *[e[0] for e in extras], *(ride.arrays if ride else []))
    return res


def _row(i, j, k):
    return (i, 0)


def _tile(i, j, k):
    return (i, j)


def _const(i, j, k):
    return (0, 0)


def _rows(name, fn, n_steps, ins, outs, n_acc=0):
    ni, no = len(ins), len(outs)

    def body(*refs):
        i = pl.program_id(0)
        vals = fn(*[r[...] for r in refs[:ni]])
        for idx, (ref, val) in enumerate(zip(refs[ni:ni + no], vals)):
            if idx < no - n_acc:
                ref[...] = val.astype(ref.dtype)
            else:
                @pl.when(i == 0)
                def _(ref=ref, val=val):
                    ref[...] = val

                @pl.when(i > 0)
                def _(ref=ref, val=val):
                    ref[...] += val

    return pl.pallas_call(
        body, name=name, grid=(n_steps,),
        in_specs=[pl.BlockSpec(blk, fn_) for _, blk, fn_ in ins],
        out_specs=[pl.BlockSpec(blk, fn_) for _, _, blk, fn_ in outs],
        out_shape=[jax.ShapeDtypeStruct(shape, dt) for shape, dt, _, _ in outs],
    )(*[a for a, _, _ in ins])


def _ffn_fwd(tag, x, w_in, w_out, g, b, ride=None, ride2=None, w_out_ride=None):
    S = x.shape[0]
    tm = min(BIG_ROWS, S)
    gate, *came = _mm(f"{tag}_gate", x, w_in, S, DFF, D, tm=tm, tn=FSH, tk=D, order="ji",
                      b_spec=((None, D, FSH), lambda i, j, k: (j, 0, 0)), ride=w_out_ride,
                      outs=[((S, DFF), F32, (tm, FSH), _tile)])
    if w_out_ride:
        w_out = came[0].reshape(DFF, D)

    def up_epi(acc, gt):
        return acc, gt * _sigmoid(gt) * acc

    up, s, *arrived = _mm(f"{tag}_up", x, w_in, S, DFF, D, tm=tm, tn=FSH, tk=D, order="ji",
                          b_spec=((None, D, FSH), lambda i, j, k: (j + 2, 0, 0)),
                          extras=[(gate, (tm, FSH), _tile)], epilogue=up_epi, ride=ride,
                          outs=[((S, DFF), F32, (tm, FSH), _tile), ((S, DFF), BF, (tm, FSH), _tile)])

    def out_epi(acc, xin, gg, bb):
        r = ALPHA * xin + 0.5 * acc
        return _layer_norm(r, gg, bb), r

    h, r, *more = _mm(f"{tag}_out", s, w_out, S, D, DFF, tm=tm, tn=D, tk=DFF,
                      extras=[(x, (tm, D), _row), (g, (1, D), _const), (b, (1, D), _const)], epilogue=out_epi, ride=ride2,
                      outs=[((S, D), F32, (tm, D), _row), ((S, D), F32, (tm, D), _row)])
    return h, (x, gate, up, s, r), arrived + more, w_out


def _ln_bwd(tag, r, g, dh):
    S = r.shape[0]
    tm = min(512, S)
    return _rows(f"{tag}_lnbwd", _layer_norm_bwd, S // tm,
                 [(r, (tm, D), lambda i: (i, 0)), (g, (1, D), lambda i: (0, 0)), (dh, (tm, D), lambda i: (i, 0))],
                 [((S, D), F32, (tm, D), lambda i: (i, 0)), ((1, D), F32, (1, D), lambda i: (0, 0)),
                  ((1, D), F32, (1, D), lambda i: (0, 0))], n_acc=2)


def _ffn_bwd(tag, saved, w_in, w_out, g, dh, ride=None):
    x, gate, up, s, r = saved
    S = x.shape[0]
    tm = min(BIG_ROWS, S)
    dr, dg, db = _ln_bwd(tag, r, g, dh)

    def act_epi(acc, gt, u):
        ds = 0.5 * acc
        sg = _sigmoid(gt)
        return (jnp.stack([ds * u * (sg * (1.0 + gt * (1.0 - sg))), ds * (gt * sg)]),)

    da, = _mm(f"{tag}_dact", dr, w_out, S, DFF, D, tm=tm, tn=FSH, tk=D, tb=True, order="ji",
              extras=[(gate, (tm, FSH), _tile), (up, (tm, FSH), _tile)], epilogue=act_epi,
              outs=[((2, S, DFF), BF, (2, tm, FSH), lambda i, j, k: (0, i, j))])
    d_w_out, = _mm(f"{tag}_dwout", s, dr, DFF, D, S, tm=FSH, tn=D, tk=tm, ta=True,
                   epilogue=lambda acc: (0.5 * acc,), outs=[((DFF, D), F32, (FSH, D), _tile)])
    tb_ = min(BIG_ROWS, S)
    d_in, *arrived = _mm(f"{tag}_dx", da, w_in, S, D, 2 * DFF, tm=tb_, tn=D, tk=FSH, tb=True,
                         a_spec=((None, tb_, FSH), lambda i, j, k: (k // 2, i, k % 2)),
                         b_spec=((None, D, FSH), lambda i, j, k: (k, 0, 0)),
                         extras=[(dr, (tb_, D), _row)], epilogue=lambda acc, d: (acc + ALPHA * d,), ride=ride,
                         outs=[((S, D), F32, (tb_, D), _row)])
    d_w_in, = _mm(f"{tag}_dwin", x, da, D, 2 * DFF, S, tm=D, tn=FSH, tk=tb_, ta=True, order="ji",
                  b_spec=((None, tb_, FSH), lambda i, j, k: (j // 2, k, j % 2)),
                  outs=[((NSH, D, FSH), F32, (None, D, FSH), lambda i, j, k: (j, 0, 0))])
    return d_in, d_w_in, d_w_out, dg, db, arrived


SB = 256
NKC = SB // KB
HPF = 8
HPS = 4
GW = HPS * HD
LOG2E = 1.4426950408889634


def _split(vals):
    hi = vals.astype(BF)
    return hi, (vals - hi.astype(F32)).astype(BF)


def _chunk_sums(tri2, vals):
    hi, lo = _split(vals)
    return [jnp.dot(tri2, jnp.concatenate([hi[c * KB:(c + 1) * KB], lo[c * KB:(c + 1) * KB]], axis=0), preferred_element_type=F32)
            for c in range(NKC)]


def _head_halves(t, axis):
    idx = lax.broadcasted_iota(I32, t.shape, axis)
    return [jnp.where(idx < HD, t, 0.0).astype(BF), jnp.where(idx >= HD, t, 0.0).astype(BF)]


QT = 512


def _diag_masks(sq):
    krow, qcol = lax.broadcasted_iota(I32, (SB, sq), 0), lax.broadcasted_iota(I32, (SB, sq), 1)
    return [krow + d * SB < qcol for d in range(sq // SB)]


def _softplus2(z):
    return jnp.maximum(z, 0.0) + jnp.log2(1.0 + jnp.exp2(jnp.minimum(z, -z)))


def _attn_prep(m):
    S = m.shape[0]
    tm = min(512, S)
    n = HEADS * HD
    return _rows("attn_prep", lambda k, v: (k, v, k.T, v.T), S // tm,
                 [(m, (tm, n), lambda i: (i, 1)), (m, (tm, n), lambda i: (i, 2))],
                 [((S, n), BF, (tm, n), lambda i: (i, 0)), ((S, n), BF, (tm, n), lambda i: (i, 0)),
                  ((n, S), BF, (n, tm), lambda i: (0, i)), ((n, S), BF, (n, tm), lambda i: (0, i))])


def _attn_fwd_t(m, kb_all, vt_all, ride=None):
    S = m.shape[0]
    SQ = min(QT, S)
    assert S % SQ == 0 and SQ % SB == 0
    HPS, GW = HPF, HPF * HD
    nkc, nqb, nh, nq = S // KB, SQ // SB, HEADS // HPS, S // SQ
    nr = ride.n if ride else 0

    def body(*refs):
        q_ref, kb_hbm, vt_hbm = refs[:3]
        ride_in, (o_ref, r_ref), ride_out = refs[3:3 + nr], refs[3 + nr:5 + nr], refs[5 + nr:5 + 2 * nr]
        kb, vt, acc = refs[5 + 2 * nr:8 + 2 * nr]
        sems = refs[8 + 2 * nr:]
        h, i = pl.program_id(0), pl.program_id(1)
        if ride:
            @pl.when(jnp.logical_and(h == 0, i == 0))
            def _():
                ride.run("start", ride_in, ride_out, sems)

        @pl.when(i == 0)
        def _():
            cols = pl.ds(pl.multiple_of(h * GW, GW), GW)
            pltpu.sync_copy(kb_hbm.at[:, cols], kb)
            pltpu.sync_copy(vt_hbm.at[cols, :], vt)

        qt = (q_ref[...] * (0.125 * LOG2E)).T
        qtm = [t for g in range(HPS // 2) for t in _head_halves(qt[g * KB:(g + 1) * KB], 0)]
        dmasks = _diag_masks(SQ)
        upper = (lax.broadcasted_iota(I32, (KB, KB), 1) >= lax.broadcasted_iota(I32, (KB, KB), 0)).astype(BF)
        tri2 = jnp.concatenate([upper, upper], axis=1)
        acc[...] = jnp.zeros_like(acc)
        r_ref[...] = jnp.zeros_like(r_ref)

        def block(jb, runs, dmask):
            masked = dmask is not None
            off = pl.multiple_of(jb * SB, SB)
            groups = [slice(g * KB, (g + 1) * KB) for g in range(HPS // 2)]
            kblk = [kb[pl.ds(off, SB), s] for s in groups]
            vtb = [vt[s, pl.ds(off, SB)] for s in groups]
            old = acc[...]
            zs = [jnp.dot(kblk[hh // 2], qtm[hh], preferred_element_type=F32) for hh in range(HPS)]
            sps = [_softplus2(z) for z in zs]
            if masked:
                sps = [jnp.where(dmask, sp, 0.0) for sp in sps]
            css = [_chunk_sums(tri2, sp) for sp in sps]
            run0s = [run + cs[1][0:1, :] for run, cs in zip(runs, css)]
            aa = [jnp.exp2(z - jnp.concatenate([run0 + cs[0], run + cs[1]], axis=0)) for z, run, run0, cs in zip(zs, runs, run0s, css)]
            if masked:
                aa = [jnp.where(dmask, a, 0.0) for a in aa]
            parts = [jnp.dot(vtb[hh // 2], aa[hh].astype(BF), preferred_element_type=F32) for hh in range(HPS)]
            upd = jnp.concatenate([parts[hh][(hh % 2) * HD:(hh % 2 + 1) * HD, :] for hh in range(HPS)], axis=0)
            for hh in range(HPS):
                r_ref[hh, pl.ds(NKC * jb, 1), :] = run0s[hh]
                r_ref[hh, pl.ds(NKC * jb + 1, 1), :] = runs[hh]
            acc[...] = old + upd
            return tuple(run0 + cs[0][0:1, :] for run0, cs in zip(run0s, css))

        runs = (jnp.zeros((1, SQ), F32),) * HPS
        for d in reversed(range(nqb)):
            runs = block(i * nqb + d, runs, dmasks[d])
        lax.fori_loop(0, i * nqb, lambda t, c: block(i * nqb - 1 - t, c, None), runs)
        o_ref[...] = acc[...].T
        if ride:
            @pl.when(jnp.logical_and(h == nh - 1, i == nq - 1))
            def _():
                ride.run("wait", ride_in, ride_out, sems)

    return pl.pallas_call(
        body, name="attn_fwd", grid=(nh, nq),
        in_specs=[pl.BlockSpec((SQ, GW), lambda h, i: (i, h)), ANY, ANY] + [ANY] * nr,
        out_specs=[pl.BlockSpec((SQ, GW), lambda h, i: (i, h)), pl.BlockSpec((HPS, nkc, SQ), lambda h, i: (h, 0, i))] + [ANY] * nr,
        out_shape=[jax.ShapeDtypeStruct((S, HEADS * HD), F32), jax.ShapeDtypeStruct((HEADS, nkc, S), F32)]
        + (ride.out_shape if ride else []),
        scratch_shapes=[pltpu.VMEM((S, GW), BF), pltpu.VMEM((GW, S), BF), pltpu.VMEM((GW, SQ), F32)]
        + (_exchange_sems(nr) if ride else []),
    )(m, kb_all, vt_all, *(ride.arrays if ride else []))


def _attn_bwd_t(m, kb_all, vb_all, kt_all, runs, dy, ride=None):
    S = m.shape[0]
    SQ = min(QT, S)
    nkc, nqb, nh, nq = S // KB, SQ // SB, HEADS // HPS, S // SQ
    nr = ride.n if ride else 0

    def body(*refs):
        q_ref, kb_hbm, vb_hbm, kt_hbm, r_ref, dy_ref = refs[:6]
        ride_in, (dq_ref, dk_hbm, dv_hbm), ride_out = refs[6:6 + nr], refs[6 + nr:9 + nr], refs[9 + nr:9 + 2 * nr]
        kb, vb, kt, dqt, dka, dva = refs[9 + 2 * nr:15 + 2 * nr]
        sems = refs[15 + 2 * nr:]
        h, i = pl.program_id(0), pl.program_id(1)
        cols = pl.ds(pl.multiple_of(h * GW, GW), GW)
        if ride:
            @pl.when(jnp.logical_and(h == 0, i == 0))
            def _():
                ride.run("start", ride_in, ride_out, sems)

        @pl.when(i == 0)
        def _():
            pltpu.sync_copy(kb_hbm.at[:, cols], kb)
            pltpu.sync_copy(vb_hbm.at[:, cols], vb)
            pltpu.sync_copy(kt_hbm.at[cols, :], kt)
            dka[...] = jnp.zeros_like(dka)
            dva[...] = jnp.zeros_like(dva)

        q8 = q_ref[...] * 0.125
        dyf = dy_ref[...]
        q8t, dyt = (q8 * LOG2E).T, dyf.T
        groups = [slice(g * KB, (g + 1) * KB) for g in range(HPS // 2)]
        qtm = [t for s in groups for t in _head_halves(q8t[s], 0)]
        dytm = [t for s in groups for t in _head_halves(dyt[s], 0)]
        qlm = [t for s in groups for t in _head_halves(q8[:, s], 1)]
        dylm = [t for s in groups for t in _head_halves(dyf[:, s], 1)]
        dmasks = _diag_masks(SQ)
        ri, ci = lax.broadcasted_iota(I32, (KB, KB), 0), lax.broadcasted_iota(I32, (KB, KB), 1)
        upper, lower = (ci >= ri).astype(BF), (ci <= ri).astype(BF)
        rev2 = jnp.concatenate([upper, upper], axis=1)
        dqt[...] = jnp.zeros_like(dqt)

        def block(jb, pres, dmask):
            masked = dmask is not None
            off = pl.multiple_of(jb * SB, SB)
            heads = range(HPS)
            kblk = [kb[pl.ds(off, SB), s] for s in groups]
            vblk = [vb[pl.ds(off, SB), s] for s in groups]
            ktb = [kt[s, pl.ds(off, SB)] for s in groups]
            run0s = [r_ref[hh, pl.ds(NKC * jb, 1), :] for hh in heads]
            run1s = [r_ref[hh, pl.ds(NKC * jb + 1, 1), :] for hh in heads]
            old_dq, old_dk, old_dv = dqt[...], dka[pl.ds(off, SB), :], dva[pl.ds(off, SB), :]
            dks, dvs, parts, new = [], [], [], []
            for g in range(HPS // 2):
                hs = (2 * g, 2 * g + 1)
                zs = [jnp.dot(kblk[g], qtm[hh], preferred_element_type=F32) for hh in hs]
                das = [jnp.dot(vblk[g], dytm[hh], preferred_element_type=F32) for hh in hs]
                sps = [_softplus2(z) for z in zs]
                sigs = [jnp.exp2(z - sp) for z, sp in zip(zs, sps)]
                if masked:
                    sps = [jnp.where(dmask, sp, 0.0) for sp in sps]
                css = [_chunk_sums(rev2, sp) for sp in sps]
                aa = [jnp.exp2(z - jnp.concatenate([run0s[hh] + cs[0], run1s[hh] + cs[1]], axis=0)) for z, hh, cs in zip(zs, hs, css)]
                if masked:
                    aa = [jnp.where(dmask, a, 0.0) for a in aa]
                gs = [a * da for a, da in zip(aa, das)]
                pgs = [[jnp.dot(lower, gg[c * KB:(c + 1) * KB].astype(BF), preferred_element_type=F32) for c in range(NKC)] for gg in gs]
                pre1s = [pres[hh] + pg[0][KB - 1:KB, :] for hh, pg in zip(hs, pgs)]
                dzs = [gg - sig * jnp.concatenate([pres[hh] + pg[0], pre1 + pg[1]], axis=0)
                       for gg, sig, hh, pre1, pg in zip(gs, sigs, hs, pre1s, pgs)]
                if masked:
                    dzs = [jnp.where(dmask, dz, 0.0) for dz in dzs]
                dzb, ab = [dz.astype(BF) for dz in dzs], [a.astype(BF) for a in aa]
                dks.append(sum(jnp.dot(dzb[t], qlm[hh], preferred_element_type=F32) for t, hh in enumerate(hs)))
                dvs.append(sum(jnp.dot(ab[t], dylm[hh], preferred_element_type=F32) for t, hh in enumerate(hs)))
                parts += [jnp.dot(ktb[g], dzb[t], preferred_element_type=F32)[t * HD:(t + 1) * HD, :] for t in range(2)]
                new += [pre1 + pg[1][KB - 1:KB, :] for pre1, pg in zip(pre1s, pgs)]
            dqt[...] = old_dq + jnp.concatenate(parts, axis=0)
            dka[pl.ds(off, SB), :] = old_dk + jnp.concatenate(dks, axis=1)
            dva[pl.ds(off, SB), :] = old_dv + jnp.concatenate(dvs, axis=1)
            return tuple(new)

        pres = lax.fori_loop(0, i * nqb, lambda jb, c: block(jb, c, None), (jnp.zeros((1, SQ), F32),) * HPS)
        for d in range(nqb):
            pres = block(i * nqb + d, pres, dmasks[d])
        dq_ref[...] = (dqt[...].T * 0.125).astype(dq_ref.dtype)

        @pl.when(i == nq - 1)
        def _():
            pltpu.sync_copy(dka, dk_hbm.at[:, cols])
            pltpu.sync_copy(dva, dv_hbm.at[:, cols])

        if ride:
            @pl.when(jnp.logical_and(h == nh - 1, i == nq - 1))
            def _():
                ride.run("wait", ride_in, ride_out, sems)

    n = HEADS * HD
    return pl.pallas_call(
        body, name="attn_bwd", grid=(nh, nq),
        in_specs=[pl.BlockSpec((SQ, GW), lambda h, i: (i, h)), ANY, ANY, ANY,
                  pl.BlockSpec((HPS, nkc, SQ), lambda h, i: (h, 0, i)), pl.BlockSpec((SQ, GW), lambda h, i: (i, h))] + [ANY] * nr,
        out_specs=[pl.BlockSpec((SQ, GW), lambda h, i: (i, h)), ANY, ANY] + [ANY] * nr,
        out_shape=[jax.ShapeDtypeStruct((S, n), BF), jax.ShapeDtypeStruct((S, n), F32), jax.ShapeDtypeStruct((S, n), F32)]
        + (ride.out_shape if ride else []),
        scratch_shapes=[pltpu.VMEM((S, GW), BF), pltpu.VMEM((S, GW), BF), pltpu.VMEM((GW, S), BF), pltpu.VMEM((GW, SQ), F32),
                        pltpu.VMEM((S, GW), F32), pltpu.VMEM((S, GW), F32)] + (_exchange_sems(nr) if ride else []),
    )(m, kb_all, vb_all, kt_all, runs, dy, *(ride.arrays if ride else []))


CW = 3 * HEADS * HD


def _shift_down(cur, prev8, s):
    if s == 0:
        return cur
    r = pltpu.roll(cur, s, 0)
    first = jnp.where(lax.broadcasted_iota(I32, (8, cur.shape[1]), 0) < s, pltpu.roll(prev8, s, 0), r[:8])
    return jnp.concatenate([first, r[8:]], axis=0)


def _shift_up(cur, next8, s):
    if s == 0:
        return cur
    n = cur.shape[0]
    r = pltpu.roll(cur, n - s, 0)
    last = jnp.where(lax.broadcasted_iota(I32, (8, cur.shape[1]), 0) >= 8 - s, pltpu.roll(next8, 8 - s, 0), r[n - 8:])
    return jnp.concatenate([r[:n - 8], last], axis=0)


def _conv_fwd(m, conv_w):
    S = m.shape[0]
    tm = min(512, S)
    hb = tm // 8

    def body(x_ref, p_ref, w_ref, o_ref):
        i = pl.program_id(0)
        cur = x_ref[...]
        prev = jnp.where(i > 0, p_ref[...], 0.0)
        w = w_ref[...]
        acc = cur * w[3:4]
        for jk in range(3):
            acc = acc + _shift_down(cur, prev, 3 - jk) * w[jk:jk + 1]
        o_ref[...] = acc

    return pl.pallas_call(
        body, name="conv_fwd", grid=(S // tm,),
        in_specs=[pl.BlockSpec((tm, CW), lambda i: (i, 1)), pl.BlockSpec((8, CW), lambda i: (jnp.maximum(i * hb - 1, 0), 1)),
                  pl.BlockSpec((4, CW), lambda i: (0, 0))],
        out_specs=pl.BlockSpec((tm, CW), lambda i: (i, 0)),
        out_shape=jax.ShapeDtypeStruct((S, CW), F32),
    )(m, m, conv_w)


def _conv_bwd(m, dyc, conv_w):
    S = m.shape[0]
    tm = min(512, S)
    hb = tm // 8
    nt = S // tm

    def body(x_ref, p_ref, d_ref, n_ref, w_ref, dx_ref, dw_ref):
        i = pl.program_id(0)
        cur = x_ref[...]
        prev = jnp.where(i > 0, p_ref[...], 0.0)
        d = d_ref[...]
        nxt = jnp.where(i < nt - 1, n_ref[...], 0.0)
        w = w_ref[...]
        acc = d * w[3:4]
        dws = []
        for jk in range(3):
            acc = acc + _shift_up(d, nxt, 3 - jk) * w[jk:jk + 1]
            dws.append(jnp.sum(d * _shift_down(cur, prev, 3 - jk), axis=0, keepdims=True))
        dws.append(jnp.sum(d * cur, axis=0, keepdims=True))
        dx_ref[...] = acc.astype(dx_ref.dtype)
        dw = jnp.concatenate(dws, axis=0)

        @pl.when(i == 0)
        def _():
            dw_ref[...] = dw

        @pl.when(i > 0)
        def _():
            dw_ref[...] += dw

    return pl.pallas_call(
        body, name="conv_bwd", grid=(nt,),
        in_specs=[pl.BlockSpec((tm, CW), lambda i: (i, 1)), pl.BlockSpec((8, CW), lambda i: (jnp.maximum(i * hb - 1, 0), 1)),
                  pl.BlockSpec((tm, CW), lambda i: (i, 0)),
                  pl.BlockSpec((8, CW), lambda i: (jnp.minimum((i + 1) * hb, S // 8 - 1), 0)),
                  pl.BlockSpec((4, CW), lambda i: (0, 0))],
        out_specs=[pl.BlockSpec((tm, CW), lambda i: (i, 0)), pl.BlockSpec((4, CW), lambda i: (0, 0))],
        out_shape=[jax.ShapeDtypeStruct((S, CW), BF), jax.ShapeDtypeStruct((4, CW), F32)],
    )(m, m, dyc, dyc, conv_w)


def _t(a):
    return jnp.swapaxes(a, 1, 2)


def _bdot(a, b):
    return jnp.einsum("hik,hkj->hij", a, b, preferred_element_type=F32)


@jax.custom_vjp
def _mm1(a, b):
    return _bdot(a.astype(BF), b.astype(BF))


def _bdot_nt(a, b):
    return jnp.einsum("hij,hkj->hik", a, b, preferred_element_type=F32)


def _bdot_tn(a, b):
    return jnp.einsum("hki,hkj->hij", a, b, preferred_element_type=F32)


_mm1.defvjp(lambda a, b: (_mm1(a, b), (a.astype(BF), b.astype(BF))),
            lambda res, dc: (_bdot_nt(dc.astype(BF), res[1]), _bdot_tn(res[0], dc.astype(BF))))


@jax.custom_vjp
def _mm1_nt(a, b):
    return _bdot_nt(a.astype(BF), b.astype(BF))


_mm1_nt.defvjp(lambda a, b: (_mm1_nt(a, b), (a.astype(BF), b.astype(BF))),
               lambda res, dc: (_bdot(dc.astype(BF), res[1]), _bdot_tn(dc.astype(BF), res[0])))


@jax.custom_vjp
def _mm1_tn(a, b):
    return _bdot_tn(a.astype(BF), b.astype(BF))


_mm1_tn.defvjp(lambda a, b: (_mm1_tn(a, b), (a.astype(BF), b.astype(BF))),
               lambda res, dc: (_bdot_nt(res[1], dc.astype(BF)), _bdot(res[0], dc.astype(BF))))


def _stack_rows(hi, lo):
    return jnp.concatenate([hi, lo], axis=1)


@jax.custom_vjp
def _mm3(a, b):
    (ah, al), (bh, bl) = _split(a), _split(b)
    n = a.shape[1]
    two = _bdot(_stack_rows(ah, al), bh)
    return two[:, :n] + two[:, n:] + _bdot(ah, bl)


def _mm3_fwd(a, b):
    return _mm3(a, b), (_split(a), _split(b))


def _mm3_bwd(res, dc):
    (ah, al), (bh, bl) = res
    dh, dl = _split(dc)
    n = dc.shape[1]
    two = _bdot_nt(_stack_rows(dh, dl), bh)
    da = two[:, :n] + two[:, n:] + _bdot_nt(dh, bl)
    db = _bdot_tn(jnp.concatenate([ah, ah, al], axis=1), jnp.concatenate([dh, dl, dh], axis=1))
    return da, db


_mm3.defvjp(_mm3_fwd, _mm3_bwd)


@jax.custom_vjp
def _unit_lower_inverse(lower):
    eye = (lax.broadcasted_iota(I32, lower.shape, 1) == lax.broadcasted_iota(I32, lower.shape, 2)).astype(F32)
    pw = -lower
    inv = eye + pw
    pw = _mm3(pw, pw)
    for _ in range(4):
        both = _mm3(jnp.concatenate([inv, pw], axis=1), pw)
        inv, pw = inv + both[:, :CH], both[:, CH:]
    return inv + _mm3(inv, pw)


def _unit_lower_inverse_bwd(inv, d):
    t = _t(inv)
    return (-_mm3(_mm3(t, d), t),)


_unit_lower_inverse.defvjp(lambda lower: (_unit_lower_inverse(lower),) * 2, _unit_lower_inverse_bwd)


@jax.custom_vjp
def _saved_inverse(lower, inv):
    return inv


_saved_inverse.defvjp(lambda lower, inv: (inv, inv),
                      lambda inv, d: _unit_lower_inverse_bwd(inv, d) + (jnp.zeros_like(inv),))


def _mm_exact(c3, b):
    hi, lo = _split(b)
    lo2 = (b - hi.astype(F32) - lo.astype(F32)).astype(BF)
    return _bdot(c3, jnp.concatenate([hi, lo, lo2], axis=-2))


@jax.custom_vjp
def _cumsum_rows(b):
    return _mm_exact(_tri3(True), b)


def _tri3(lower):
    ri = lax.broadcasted_iota(I32, (HEADS, CH, CH), 1)
    ci = lax.broadcasted_iota(I32, (HEADS, CH, CH), 2)
    tri = (ri >= ci if lower else ri <= ci).astype(BF)
    return jnp.concatenate([tri, tri, tri], axis=-1)


_cumsum_rows.defvjp(lambda b: (_cumsum_rows(b), None), lambda _, dc: (_mm_exact(_tri3(False), dc),))


CPS = 4


def _gdn_chunk(yc, gz, gba, alog, dtb, inv_saved=None):
    def heads(t, off):
        return jnp.stack([t[:, off + h * HD: off + (h + 1) * HD] for h in range(HEADS)])

    def cols(t, off):
        return jnp.stack([jnp.broadcast_to(t[:, off + h: off + h + 1], (CH, CH)) for h in range(HEADS)])

    c = yc * _sigmoid(yc)
    q, k, v, zg = heads(c, 0), heads(c, HEADS * HD), heads(c, 2 * HEADS * HD), heads(gz, 0)
    q = q * lax.rsqrt(jnp.sum(q * q, axis=-1, keepdims=True) + RMS_EPS) * (HD ** -0.5)
    k = k * lax.rsqrt(jnp.sum(k * k, axis=-1, keepdims=True) + RMS_EPS)
    beta = cols(_sigmoid(gba), 0)
    g = cols(-jnp.exp(alog) * _softplus(gba + dtb), HEADS)
    ri = lax.broadcasted_iota(I32, (HEADS, CH, CH), 1)
    ci = lax.broadcasted_iota(I32, (HEADS, CH, CH), 2)
    causal, strict = ri >= ci, ri > ci
    gc = _cumsum_rows(g)
    gr = _t(gc)
    decay = jnp.where(causal, jnp.exp(jnp.where(causal, gc - gr, 0.0)), 0.0)
    lower = jnp.where(strict, beta * _mm1_nt(k, k) * decay, 0.0)
    inv = _unit_lower_inverse(lower) if inv_saved is None else _saved_inverse(lower, inv_saved)
    eg = jnp.exp(gc)
    uw = _mm3(inv, jnp.concatenate([v * beta, k * (beta * eg)], axis=2))
    u, w = uw[:, :, :HD], uw[:, :, HD:]
    qk = jnp.where(causal, _mm1_nt(q, k) * decay, 0.0)
    g_last = gc[:, CH - 1:CH, :]
    return (u, jnp.concatenate([w, q * eg], axis=1), qk, k * jnp.exp(g_last - gc), jnp.exp(g_last), zg * _sigmoid(zg)), inv


def _gdn_advance(state, pre, normw):
    u, wq, qk, kd, last, gate = pre
    ws = _mm1(wq, state)
    v_new = u - ws[:, :CH]
    o = ws[:, CH:] + _mm1(qk, v_new)
    new_state = state * last + _mm1_tn(kd, v_new)
    o = o * lax.rsqrt(jnp.mean(o * o, axis=-1, keepdims=True) + RMS_EPS) * normw * gate
    return jnp.concatenate([o[h] for h in range(HEADS)], axis=1), new_state


def _gdn_chunks(state, yc, gz, gba, alog, dtb, normw, invs=None):
    rows = [slice(c * CH, (c + 1) * CH) for c in range(yc.shape[0] // CH)]
    pres = [_gdn_chunk(yc[r], gz[r], gba[r], alog, dtb, None if invs is None else invs[c]) for c, r in enumerate(rows)]
    outs = []
    for pre, _ in pres:
        o, state = _gdn_advance(state, pre, normw)
        outs.append(o)
    return jnp.concatenate(outs, axis=0), state, jnp.stack([inv for _, inv in pres])


def _gdn_fwd(yc, m, alog, dtb, normw):
    S = yc.shape[0]
    RS = CPS * CH
    nch = S // RS

    def body(y_ref, gz_ref, gba_ref, al_ref, dt_ref, nw_ref, o_ref, st_ref, inv_ref, st):
        @pl.when(pl.program_id(0) == 0)
        def _():
            st[...] = jnp.zeros_like(st)

        cur = st[...]
        st_ref[0] = cur
        o, new, invs = _gdn_chunks(cur, y_ref[...], gz_ref[...], gba_ref[...], al_ref[...], dt_ref[...], nw_ref[...])
        o_ref[...] = o
        inv_ref[...] = invs
        st[...] = new

    og, states, invs = pl.pallas_call(
        body, name="gdn_fwd", grid=(nch,),
        in_specs=[pl.BlockSpec((RS, CW), lambda n: (n, 0)), pl.BlockSpec((RS, HEADS * HD), lambda n: (n, 6)),
                  pl.BlockSpec((RS, 128), lambda n: (n, 28)), pl.BlockSpec((1, 128), lambda n: (0, 0)),
                  pl.BlockSpec((1, 128), lambda n: (0, 0)), pl.BlockSpec((1, HD), lambda n: (0, 0))],
        out_specs=[pl.BlockSpec((RS, HEADS * HD), lambda n: (n, 0)), pl.BlockSpec((1, HEADS, HD, HD), lambda n: (n, 0, 0, 0)),
                   pl.BlockSpec((CPS, HEADS, HD, HD), lambda n: (n, 0, 0, 0))],
        out_shape=[jax.ShapeDtypeStruct((S, HEADS * HD), F32), jax.ShapeDtypeStruct((nch, HEADS, HD, HD), F32),
                   jax.ShapeDtypeStruct((S // CH, HEADS, HD, HD), F32)],
        scratch_shapes=[pltpu.VMEM((HEADS, HD, HD), F32)],
    )(yc, m, m, alog, dtb, normw)
    return og, (states, invs)


def _gdn_bwd(yc, m, alog, dtb, normw, states, dog):
    S = yc.shape[0]
    RS = CPS * CH
    nch = S // RS
    states, invs = states

    def body(y_ref, gz_ref, gba_ref, al_ref, dt_ref, nw_ref, st_ref, inv_ref, do_ref, dy_ref, dgz_ref, dgba_ref, dal_ref, ddt_ref,
             dnw_ref, dst):
        n = pl.program_id(0)

        @pl.when(n == 0)
        def _():
            dst[...] = jnp.zeros_like(dst)

        kept = inv_ref[...]
        _, vjp = jax.vjp(lambda *a: _gdn_chunks(*a, invs=kept)[:2],
                         st_ref[0], y_ref[...], gz_ref[...], gba_ref[...], al_ref[...], dt_ref[...], nw_ref[...])
        d_state, d_y, d_gz, d_gba, d_al, d_dt, d_nw = vjp((do_ref[...], dst[...]))
        dst[...] = d_state
        dy_ref[...] = d_y
        dgz_ref[...] = d_gz.astype(dgz_ref.dtype)
        dgba_ref[...] = d_gba.astype(dgba_ref.dtype)
        for ref, val in ((dal_ref, d_al), (ddt_ref, d_dt), (dnw_ref, d_nw)):
            @pl.when(n == 0)
            def _(ref=ref, val=val):
                ref[...] = val

            @pl.when(n > 0)
            def _(ref=ref, val=val):
                ref[...] += val

    rev = lambda n: nch - 1 - n
    return pl.pallas_call(
        body, name="gdn_bwd", grid=(nch,),
        in_specs=[pl.BlockSpec((RS, CW), lambda n: (rev(n), 0)), pl.BlockSpec((RS, HEADS * HD), lambda n: (rev(n), 6)),
                  pl.BlockSpec((RS, 128), lambda n: (rev(n), 28)), pl.BlockSpec((1, 128), lambda n: (0, 0)),
                  pl.BlockSpec((1, 128), lambda n: (0, 0)), pl.BlockSpec((1, HD), lambda n: (0, 0)),
                  pl.BlockSpec((1, HEADS, HD, HD), lambda n: (rev(n), 0, 0, 0)),
                  pl.BlockSpec((CPS, HEADS, HD, HD), lambda n: (rev(n), 0, 0, 0)),
                  pl.BlockSpec((RS, HEADS * HD), lambda n: (rev(n), 0))],
        out_specs=[pl.BlockSpec((RS, CW), lambda n: (rev(n), 0)), pl.BlockSpec((RS, HEADS * HD), lambda n: (rev(n), 0)),
                   pl.BlockSpec((RS, 128), lambda n: (rev(n), 0)), pl.BlockSpec((1, 128), lambda n: (0, 0)),
                   pl.BlockSpec((1, 128), lambda n: (0, 0)), pl.BlockSpec((1, HD), lambda n: (0, 0))],
        out_shape=[jax.ShapeDtypeStruct((S, CW), F32), jax.ShapeDtypeStruct((S, HEADS * HD), BF),
                   jax.ShapeDtypeStruct((S, 128), BF), jax.ShapeDtypeStruct((1, 128), F32),
                   jax.ShapeDtypeStruct((1, 128), F32), jax.ShapeDtypeStruct((1, HD), F32)],
        scratch_shapes=[pltpu.VMEM((HEADS, HD, HD), F32)],
    )(yc, m, m, alog, dtb, normw, states, invs, dog)


def _mixer_fwd(h1, W, late=None):
    S = h1.shape[0]
    tm = min(512, S)
    n = HEADS * HD
    tb_ = min(BIG_ROWS, S)
    m, = _mm("mix_in", h1, W["wp"], S, NP, D, tm=tb_, tn=1536, tk=D, order="ji", outs=[((S, NP), F32, (tb_, 1536), _tile)])
    kb_all, vb_all, kt_all, vt_all = _attn_prep(m)
    ya, runs, *arrived = _attn_fwd_t(m, kb_all, vt_all, ride=late[0] if late else None)
    if late:
        late[1](W, arrived)
    runs = (kb_all, vb_all, kt_all, runs)
    b_gate, conv_w, alog, dtb, normw, w_sb, w_gdn, w_mo, g, b = (
        W[k] for k in ("b_gate", "conv_w", "alog", "dtb", "normw", "w_sb", "w_gdn", "w_mo", "ln2_g", "ln2_b"))
    yc = _conv_fwd(m, conv_w)
    og, states = _gdn_fwd(yc, m, alog, dtb, normw)
    ysb, = _mm("mix_sb", ya, w_sb, S, D, n, tm=tm, tn=D, tk=n, outs=[((S, D), F32, (tm, D), _row)])

    def merge_epi(acc, ys, gs, gg, bg):
        return _sigmoid(gs + bg[:, :D]) * ys + _sigmoid(gg + bg[:, D:]) * acc, acc

    u, ygdn = _mm("mix_gdn", og, w_gdn, S, D, n, tm=tm, tn=D, tk=n,
                  extras=[(ysb, (tm, D), _row), (m, (tm, D), lambda i, j, k: (i, GATE0 // D)),
                          (m, (tm, D), lambda i, j, k: (i, GATE0 // D + 1)), (b_gate, (1, 2 * D), _const)],
                  epilogue=merge_epi, outs=[((S, D), BF, (tm, D), _row), ((S, D), F32, (tm, D), _row)])

    def out_epi(acc, xin, gg, bb):
        r = ALPHA * xin + acc
        return _layer_norm(r, gg, bb), r

    h2, r2 = _mm("mix_out", u, w_mo, S, D, D, tm=tm, tn=D, tk=D,
                 extras=[(h1, (tm, D), _row), (g, (1, D), _const), (b, (1, D), _const)], epilogue=out_epi,
                 outs=[((S, D), F32, (tm, D), _row), ((S, D), F32, (tm, D), _row)])
    return h2, (h1, m, ya, runs, yc, og, states, ysb, ygdn, u, r2)


def _mixer_bwd(saved, W, dh, ride=None):
    h1, m, ya, runs, yc, og, states, ysb, ygdn, u, r2 = saved
    wp, b_gate, conv_w, alog, dtb, normw, w_sb, w_gdn, w_mo, g = (
        W[k] for k in ("wp", "b_gate", "conv_w", "alog", "dtb", "normw", "w_sb", "w_gdn", "w_mo", "ln2_g"))
    S = h1.shape[0]
    tm = min(512, S)
    n = HEADS * HD
    dr, dg, db = _ln_bwd("mix", r2, g, dh)

    def merge_epi(du, ys, yg, gs, gg, bg):
        s1, s2 = _sigmoid(gs + bg[:, :D]), _sigmoid(gg + bg[:, D:])
        dgate = jnp.concatenate([du * ys * s1 * (1.0 - s1), du * yg * s2 * (1.0 - s2)], axis=1)
        return du * s1, du * s2, dgate, jnp.sum(dgate, axis=0, keepdims=True)

    dysb, dygdn, dgate, d_bg = _mm(
        "mix_dmerge", dr, w_mo, S, D, D, tm=tm, tn=D, tk=D, tb=True,
        extras=[(ysb, (tm, D), _row), (ygdn, (tm, D), _row), (m, (tm, D), lambda i, j, k: (i, GATE0 // D)),
                (m, (tm, D), lambda i, j, k: (i, GATE0 // D + 1)), (b_gate, (1, 2 * D), _const)],
        epilogue=merge_epi, n_acc=1,
        outs=[((S, D), BF, (tm, D), _row), ((S, D), BF, (tm, D), _row), ((S, 2 * D), BF, (tm, 2 * D), _row),
              ((1, 2 * D), F32, (1, 2 * D), _const)])
    d_w_mo, = _mm("mix_dwmo", u, dr, D, D, S, tm=D, tn=D, tk=min(BIG_ROWS, S), ta=True, outs=[((D, D), F32, (D, D), _tile)])
    dya, = _mm("mix_dya", dysb, w_sb, S, n, D, tm=tm, tn=n, tk=D, tb=True, outs=[((S, n), F32, (tm, n), _row)])
    col_shards = [((NSH, n, D // NSH), F32, (None, n, D // NSH), lambda i, j, k: (j, 0, 0))]
    d_w_sb, = _mm("mix_dwsb", ya, dysb, n, D, S, tm=n, tn=D // NSH, tk=min(BIG_ROWS, S), ta=True, order="ji", outs=col_shards)
    dog, = _mm("mix_dog", dygdn, w_gdn, S, n, D, tm=tm, tn=n, tk=D, tb=True, outs=[((S, n), F32, (tm, n), _row)])
    d_w_gdn, = _mm("mix_dwgdn", og, dygdn, n, D, S, tm=n, tn=D // NSH, tk=min(BIG_ROWS, S), ta=True, order="ji", outs=col_shards)
    dq, dk, dv, *arrived = _attn_bwd_t(m, *runs, dya, ride=ride)
    dyc, dgz, dgba, d_alog, d_dtb, d_normw = _gdn_bwd(yc, m, alog, dtb, normw, states, dog)
    dxc, d_conv = _conv_bwd(m, dyc, conv_w)
    dm = jnp.concatenate([dq, dk.astype(BF), dv.astype(BF), dxc, dgz, dgba, jnp.zeros((S, GATE0 - 3712), BF), dgate], axis=1)
    tb_ = min(BIG_ROWS, S)
    d_h1, = _mm("mix_dh", dm, wp, S, D, NP, tm=tb_, tn=D, tk=1536, tb=True,
                extras=[(dr, (tb_, D), _row)], epilogue=lambda acc, d: (acc + ALPHA * d,),
                outs=[((S, D), F32, (tb_, D), _row)])
    d_wp, = _mm("mix_dwp", h1, dm, D, NP, S, tm=D, tn=1536, tk=tb_, ta=True, order="ji",
                outs=[((D, NP), F32, (D, 1536), _tile)])
    return d_h1, dict(wp=d_wp, b_gate=d_bg, conv_w=d_conv, alog=d_alog, dtb=d_dtb, normw=d_normw,
                      w_sb=d_w_sb, w_gdn=d_w_gdn, w_mo=d_w_mo, g=dg, b=db), arrived


def _ple_fwd(h3, p, w_pg, b_pg, w_ple, g, b, target):
    S = h3.shape[0]
    tm = min(512, S)
    pd = p.shape[1]
    pe, = _mm("ple_emb", p, w_ple, S, D, pd, tm=tm, tn=D, tk=pd, outs=[((S, D), F32, (tm, D), _row)])

    def epi(acc, e, xin, tgt, bp, gg, bb):
        gt = _sigmoid(acc + bp)
        r = ALPHA * xin + gt * e
        diff = _layer_norm(r, gg, bb) - tgt
        return gt, r, diff * (1.0 / D), jnp.sum(diff * diff, axis=0, keepdims=True)

    gt, r4, dh4, loss_row = _mm(
        "ple_gate", h3, w_pg, S, D, D, tm=tm, tn=D, tk=D,
        extras=[(pe, (tm, D), _row), (h3, (tm, D), _row), (target, (tm, D), _row), (b_pg, (1, D), _const),
                (g, (1, D), _const), (b, (1, D), _const)], epilogue=epi, n_acc=1,
        outs=[((S, D), F32, (tm, D), _row), ((S, D), F32, (tm, D), _row), ((S, D), F32, (tm, D), _row),
              ((1, D), F32, (1, D), _const)])
    return dh4, loss_row, (h3, p, pe, gt, r4)


def _ple_bwd(saved, w_pg, g, dh4):
    h3, p, pe, gt, r4 = saved
    S = h3.shape[0]
    tm = min(512, S)
    pd = p.shape[1]

    def fn(r, gg, dh, e, t):
        dr, dg, db = _layer_norm_bwd(r, gg, dh)
        dpre = dr * e * t * (1.0 - t)
        return dr, dpre, dr * t, dg, db, jnp.sum(dpre, axis=0, keepdims=True)

    row, one = (lambda i: (i, 0)), (lambda i: (0, 0))
    dr, dpre, dpe, dg, db, d_bpg = _rows(
        "ple_lnbwd", fn, S // tm,
        [(r4, (tm, D), row), (g, (1, D), one), (dh4, (tm, D), row), (pe, (tm, D), row), (gt, (tm, D), row)],
        [((S, D), F32, (tm, D), row), ((S, D), BF, (tm, D), row), ((S, D), BF, (tm, D), row),
         ((1, D), F32, (1, D), one), ((1, D), F32, (1, D), one), ((1, D), F32, (1, D), one)], n_acc=3)
    d_w_pg, = _mm("ple_dwpg", h3, dpre, D, D, S, tm=D, tn=D, tk=min(BIG_ROWS, S), ta=True, outs=[((D, D), F32, (D, D), _tile)])
    d_w_ple, = _mm("ple_dwple", p, dpe, pd, D, S, tm=pd, tn=D // NSH, tk=min(BIG_ROWS, S), ta=True, order="ji",
                   outs=[((NSH, pd, D // NSH), F32, (None, pd, D // NSH), lambda i, j, k: (j, 0, 0))])
    d_h3, = _mm("ple_dh", dpre, w_pg, S, D, D, tm=tm, tn=D, tk=D, tb=True,
                extras=[(dr, (tm, D), _row)], epilogue=lambda acc, d: (acc + ALPHA * d,),
                outs=[((S, D), F32, (tm, D), _row)])
    return d_h3, d_w_pg, d_bpg, d_w_ple, dg, db


def _local_step(x, p, target, W, now=None, soon=None, late=None, early=None, mid=None):
    W = dict(W)
    h1, sv1, arrived, W["ffn1_out"] = _ffn_fwd("ffn1", x, W["ffn1_in"], W.get("ffn1_out"), W["ln1_g"], W["ln1_b"],
                                               ride=soon[0] if soon else None, ride2=soon[1] if soon else None,
                                               w_out_ride=now)
    if soon:
        soon[2](W, arrived)
    h2, sv2 = _mixer_fwd(h1, W, late)
    h3, sv3, _, _ = _ffn_fwd("ffn2", h2, W["ffn2_in"], W["ffn2_out"], W["ln3_g"], W["ln3_b"])
    dh4, loss_row, sv4 = _ple_fwd(h3, p, W["w_pg"], W["b_pg"], W["w_ple"], W["ln4_g"], W["ln4_b"], target)
    G = {}
    dh3, G["w_pg"], G["b_pg"], G["w_ple"], G["ln4_g"], G["ln4_b"] = _ple_bwd(sv4, W["w_pg"], W["ln4_g"], dh4)
    dh2, G["ffn2_in"], G["ffn2_out"], G["ln3_g"], G["ln3_b"], _ = _ffn_bwd("ffn2", sv3, W["ffn2_in"], W["ffn2_out"], W["ln3_g"], dh3)
    dh1, gm, G["early"] = _mixer_bwd(sv2, W, dh2, ride=early(G) if early else None)
    G.update({k: v for k, v in gm.items() if k not in ("g", "b")})
    G["ln2_g"], G["ln2_b"] = gm["g"], gm["b"]
    dx, G["ffn1_in"], G["ffn1_out"], G["ln1_g"], G["ln1_b"], G["mid"] = _ffn_bwd(
        "ffn1", sv1, W["ffn1_in"], W["ffn1_out"], W["ln1_g"], dh1, ride=mid(G) if mid else None)
    return loss_row, dx, G


S2 = CUT - 2 * MSH


def _pack_wp(w4):
    tr = 256

    def fn(w):
        s = [w[j].astype(F32) for j in range(NSH)]
        full = jnp.concatenate([s[0], s[1], s[2][:, :S2], jnp.zeros((tr, GATE0 - CUT), F32), s[2][:, S2:], s[3]], axis=1)
        return (full,)

    return _rows("pack_wp", fn, D // tr, [(w4, (NSH, tr, MSH), lambda i: (0, i, 0))],
                 [((D, NP), BF, (tr, NP), lambda i: (i, 0))])[0]


def _unpack_wp(d):
    tr = 256
    g2 = GATE0 + MSH - S2

    def fn(v):
        return (jnp.stack([v[:, :MSH], v[:, MSH:2 * MSH], jnp.concatenate([v[:, 2 * MSH:CUT], v[:, GATE0:g2]], axis=1), v[:, g2:]]),)

    return _rows("unpack_wp", fn, D // tr, [(d, (tr, NP), lambda i: (i, 0))],
                 [((NSH, D, MSH), F32, (NSH, tr, MSH), lambda i: (0, i, 0))])[0]


def _cast_bf16(tag, w):
    r, c = w.shape
    tr = _pick(r, 256)
    return _rows(f"cast_{tag}", lambda v: (v,), r // tr, [(w, (tr, c), lambda i: (i, 0))],
                 [((r, c), BF, (tr, c), lambda i: (i, 0))])[0]


def _place():
    return lax.axis_index("x"), lax.axis_index("y"), lax.axis_index("c")


def _chip_exchange(phase, scatter, ins, outs, send, recv, loc):
    x, y, c = _place()
    me = 2 * x + y
    chips = [(1 - x, y), (x, 1 - y), (1 - x, 1 - y)]
    for t in range(len(ins)):
        own = pltpu.make_async_copy(ins[t].at[me] if scatter else ins[t], outs[t].at[me], loc.at[t])
        out_going, in_coming = [], []
        for q, (px, py) in enumerate(chips):
            src = ins[t].at[2 * px + py] if scatter else ins[t]
            sems = dict(send_sem=send.at[3 * t + q], recv_sem=recv.at[3 * t + q], device_id=(px, py, c), device_id_type=MESH)
            out_going.append(pltpu.make_async_remote_copy(src_ref=src, dst_ref=outs[t].at[me], **sems))
            in_coming.append(pltpu.make_async_remote_copy(src_ref=src, dst_ref=outs[t].at[2 * px + py], **sems))
        if phase == "start":
            own.start()
            for cp in out_going:
                cp.start()
        else:
            for cp in in_coming:
                cp.wait_recv()
            own.wait()
            for cp in out_going:
                cp.wait_send()


def _exchange_sems(n):
    return [pltpu.SemaphoreType.DMA((3 * n,)), pltpu.SemaphoreType.DMA((3 * n,)), pltpu.SemaphoreType.DMA((n,))]


class _Ride:
    def __init__(self, scatter, arrays):
        self.scatter, self.arrays, self.n = scatter, list(arrays), len(arrays)
        self.out_shape = [jax.ShapeDtypeStruct(a.shape if scatter else (NSH,) + a.shape, a.dtype) for a in self.arrays]

    def run(self, phase, refs_in, refs_out, sems):
        _chip_exchange(phase, self.scatter, refs_in, refs_out, *sems)


def _gather_shards(shards):
    n = len(shards)

    def body(*refs):
        _chip_exchange("start", False, refs[:n], refs[n:2 * n], *refs[2 * n:])
        _chip_exchange("wait", False, refs[:n], refs[n:2 * n], *refs[2 * n:])

    return pl.pallas_call(
        body, name="gather_weights", in_specs=[ANY] * n, out_specs=[ANY] * n,
        out_shape=[jax.ShapeDtypeStruct((NSH,) + s.shape, s.dtype) for s in shards], scratch_shapes=_exchange_sems(n),
    )(*shards)


def _reduce_chips(ps):
    n = len(ps)

    def body(*refs):
        _chip_exchange("start", True, refs[:n], refs[n:2 * n], *refs[2 * n:])
        _chip_exchange("wait", True, refs[:n], refs[n:2 * n], *refs[2 * n:])

    return pl.pallas_call(
        body, name="reduce_chips", in_specs=[ANY] * n, out_specs=[ANY] * n,
        out_shape=[jax.ShapeDtypeStruct(p_.shape, p_.dtype) for p_ in ps], scratch_shapes=_exchange_sems(n),
    )(*ps)


def _chunk_rows(h, c):
    return _pick(h, max(8, 524288 // c // 8 * 8))


def _sibling_sum(tag, g):
    _, r, c = g.shape
    h = r // 2
    tr = _chunk_rows(h, c)
    nch = h // tr
    steps = NSH * nch

    def body(top_ref, bot_ref, out_ref, narrow_ref, stage, land, send, recv):
        s = pl.program_id(0)
        x, y, core = _place()

        def exchange(keep_ref, give_ref):
            stage[...] = give_ref[0].astype(BF)
            cp = pltpu.make_async_remote_copy(src_ref=stage, dst_ref=land.at[s], send_sem=send.at[s], recv_sem=recv.at[s],
                                              device_id=(x, y, 1 - core), device_id_type=MESH)
            cp.start()
            cp.wait()
            total = keep_ref[0] + land[s].astype(F32)
            out_ref[0] = total
            narrow_ref[0] = total.astype(BF)

        @pl.when(core == 0)
        def _():
            exchange(top_ref, bot_ref)

        @pl.when(core == 1)
        def _():
            exchange(bot_ref, top_ref)

    return pl.pallas_call(
        body, name=f"sibling_sum_{tag}", grid=(steps,),
        in_specs=[pl.BlockSpec((1, tr, c), lambda s: (s // nch, s % nch, 0)),
                  pl.BlockSpec((1, tr, c), lambda s: (s // nch, nch + s % nch, 0))],
        out_specs=[pl.BlockSpec((1, tr, c), lambda s: (s // nch, s % nch, 0))] * 2,
        out_shape=[jax.ShapeDtypeStruct((NSH, h, c), F32), jax.ShapeDtypeStruct((NSH, h, c), BF)],
        scratch_shapes=[pltpu.VMEM((tr, c), BF), pltpu.VMEM((steps, tr, c), BF), pltpu.SemaphoreType.DMA((steps,)),
                        pltpu.SemaphoreType.DMA((steps,))],
    )(g, g)


def _chip_sum_share(tag, b, own):
    _, h, c = b.shape
    tr = _chunk_rows(h, c)
    steps = h // tr

    def body(b_ref, own_ref, out_ref, stage, land, send, recv):
        s = pl.program_id(0)
        x, y, core = _place()
        me = 2 * x + y
        v = [jnp.where(me == j, own_ref[j], b_ref[j].astype(F32)) for j in range(NSH)]
        total = ((v[0] + v[1]) + v[2]) + v[3]
        stage[...] = total
        cp = pltpu.make_async_remote_copy(src_ref=stage, dst_ref=land.at[s], send_sem=send.at[s], recv_sem=recv.at[s],
                                          device_id=(x, y, 1 - core), device_id_type=MESH)
        cp.start()
        cp.wait()
        out_ref[core] = total
        out_ref[1 - core] = land[s]

    return pl.pallas_call(
        body, name=f"chip_sum_share_{tag}", grid=(steps,),
        in_specs=[pl.BlockSpec((NSH, tr, c), lambda s: (0, s, 0))] * 2,
        out_specs=pl.BlockSpec((2, tr, c), lambda s: (0, s, 0)),
        out_shape=jax.ShapeDtypeStruct((2, h, c), F32),
        scratch_shapes=[pltpu.VMEM((tr, c), F32), pltpu.VMEM((steps, tr, c), F32), pltpu.SemaphoreType.DMA((steps,)),
                        pltpu.SemaphoreType.DMA((steps,))],
    )(b, own)


NDEV = 8


def _allreduce_small(pack):
    r, w = pack.shape
    rel = [(dx, dy, dc) for dx in (0, 1) for dy in (0, 1) for dc in (0, 1) if (dx, dy, dc) != (0, 0, 0)]

    def body(in_ref, out_ref, buf, send, recv):
        x, y, c = _place()
        me = 4 * x + 2 * y + c
        buf[me] = in_ref[...]
        peers = [((x + dx) % 2, (y + dy) % 2, (c + dc) % 2) for dx, dy, dc in rel]
        sent = []
        for k, peer in enumerate(peers):
            cp = pltpu.make_async_remote_copy(src_ref=in_ref, dst_ref=buf.at[me], send_sem=send.at[k], recv_sem=recv.at[k],
                                              device_id=peer, device_id_type=MESH)
            cp.start()
            sent.append(cp)
        for k, (px, py, pc) in enumerate(peers):
            pltpu.make_async_remote_copy(src_ref=in_ref, dst_ref=buf.at[4 * px + 2 * py + pc], send_sem=send.at[k], recv_sem=recv.at[k],
                                         device_id=(px, py, pc), device_id_type=MESH).wait_recv()
        for cp in sent:
            cp.wait_send()
        acc = buf[0]
        for k in range(1, NDEV):
            acc = acc + buf[k]
        out_ref[...] = acc

    vm = pl.BlockSpec(memory_space=pltpu.VMEM)
    return pl.pallas_call(
        body, name="allreduce_small", in_specs=[vm], out_specs=vm, out_shape=jax.ShapeDtypeStruct((r, w), F32),
        scratch_shapes=[pltpu.VMEM((NDEV, r, w), F32), pltpu.SemaphoreType.DMA((NDEV - 1,)), pltpu.SemaphoreType.DMA((NDEV - 1,))],
    )(pack)


def _adamw(tag, w, g, m, v):
    r, c = w.shape
    tr = _pick(r, max(8, 262144 // c // 8 * 8))

    def fn(w_, g_, m_, v_):
        m2 = B1 * m_ + (1.0 - B1) * g_
        v2 = B2 * v_ + (1.0 - B2) * (g_ * g_)
        m_hat = m2 / (1.0 - B1 ** STEP)
        v_hat = v2 / (1.0 - B2 ** STEP)
        return -LR * (m_hat / (jnp.sqrt(v_hat) + EPS) + WD * w_), m2, v2

    spec = ((tr, c), lambda i: (i, 0))
    return _rows(f"adamw_{tag}", fn, r // tr, [(a,) + spec for a in (w, g, m, v)], [((r, c), F32) + spec] * 3)


BIG = ("ffn1_w_in", "ffn1_w_out", "w_mix_in", "w_branch_sb", "w_branch_gdn", "w_mix_out", "ffn2_w_in", "ffn2_w_out",
       "w_ple_gate", "w_ple")
FIRST = ("ffn1_w_in",)
SOON = ("w_mix_in",)
LATER = tuple(n for n in BIG if n not in FIRST + SOON + ("ffn1_w_out",))
EARLY = ("ffn2_w_in", "ffn2_w_out", "w_ple_gate", "w_ple")
MID = ("w_mix_in", "w_branch_sb", "w_branch_gdn", "w_mix_out")
LAST = ("ffn1_w_in", "ffn1_w_out")
SMALL = ("ln1_g", "ln1_b", "b_gate", "conv_w", "a_log", "dt_bias", "gdn_norm_w", "ln2_g", "ln2_b", "ln3_g", "ln3_b",
         "b_ple_gate", "ln4_g", "ln4_b")
ORDER = ("ffn1_w_in", "ffn1_w_out", "ln1_g", "ln1_b", "w_mix_in", "b_gate", "conv_w", "a_log", "dt_bias", "gdn_norm_w",
         "w_branch_sb", "w_branch_gdn", "w_mix_out", "ln2_g", "ln2_b", "ffn2_w_in", "ffn2_w_out", "ln3_g", "ln3_b",
         "w_ple_gate", "b_ple_gate", "w_ple", "ln4_g", "ln4_b")
PACK_W = 2304


def _lane_row(v, lanes=128, at=HEADS):
    return jnp.pad(v[None, :], ((0, 0), (at, lanes - at - v.shape[0])))


def _col_join(w4):
    return jnp.transpose(w4, (1, 0, 2)).reshape(w4.shape[1], NSH * w4.shape[2])


def kernel(x, p, ffn1_w_in, ffn1_w_out, ln1_g, ln1_b, w_mix_in, b_gate, conv_w, a_log, dt_bias, gdn_norm_w, w_branch_sb, w_branch_gdn, w_mix_out, ln2_g, ln2_b, ffn2_w_in, ffn2_w_out, ln3_g, ln3_b, w_ple_gate, b_ple_gate, w_ple, ln4_g, ln4_b, loss_target, m_ffn1_w_in, m_ffn1_w_out, m_ln1_g, m_ln1_b, m_w_mix_in, m_b_gate, m_conv_w, m_a_log, m_dt_bias, m_gdn_norm_w, m_w_branch_sb, m_w_branch_gdn, m_w_mix_out, m_ln2_g, m_ln2_b, m_ffn2_w_in, m_ffn2_w_out, m_ln3_g, m_ln3_b, m_w_ple_gate, m_b_ple_gate, m_w_ple, m_ln4_g, m_ln4_b, v_ffn1_w_in, v_ffn1_w_out, v_ln1_g, v_ln1_b, v_w_mix_in, v_b_gate, v_conv_w, v_a_log, v_dt_bias, v_gdn_norm_w, v_w_branch_sb, v_w_branch_gdn, v_w_mix_out, v_ln2_g, v_ln2_b, v_ffn2_w_in, v_ffn2_w_out, v_ln3_g, v_ln3_b, v_w_ple_gate, v_b_ple_gate, v_w_ple, v_ln4_g, v_ln4_b):
    args = dict(locals())
    w = {n: args[n][0] for n in ORDER}
    mom = {n: args["m_" + n][0] for n in ORDER}
    var = {n: args["v_" + n][0] for n in ORDER}

    cast = {n: _cast_bf16(n, w[n]) for n in BIG if n not in SOON}
    mix_halves = [_cast_bf16(f"w_mix_in_{t}", w["w_mix_in"][t * (D // 2):(t + 1) * (D // 2)]) for t in range(2)]
    full = dict(zip(FIRST + ("conv_w",), _gather_shards([cast[n] for n in FIRST] + [w["conv_w"]])))
    W = dict(
        ffn1_in=full["ffn1_w_in"], conv_w=_col_join(full["conv_w"]), b_gate=w["b_gate"][None], alog=_lane_row(w["a_log"]), dtb=_lane_row(w["dt_bias"]),
        normw=w["gdn_norm_w"][None], b_pg=w["b_ple_gate"][None],
        **{f"ln{i}_{s}": w[f"ln{i}_{s}"][None] for i in (1, 2, 3, 4) for s in ("g", "b")},
    )

    def fill(W_, arrived):
        got = dict(zip(LATER, arrived))
        W_.update(w_sb=_col_join(got["w_branch_sb"]), w_gdn=_col_join(got["w_branch_gdn"]), w_mo=got["w_mix_out"].reshape(D, D),
                  ffn2_in=got["ffn2_w_in"], ffn2_out=got["ffn2_w_out"].reshape(DFF, D),
                  w_pg=got["w_ple_gate"].reshape(D, D), w_ple=_col_join(got["w_ple"]))

    def by_shard(G_, names):
        forms = dict(
            ffn1_w_in=lambda: G_["ffn1_in"], ffn1_w_out=lambda: G_["ffn1_out"].reshape(NSH, DFF // NSH, D),
            w_mix_in=lambda: _unpack_wp(G_["wp"]), w_branch_sb=lambda: G_["w_sb"], w_branch_gdn=lambda: G_["w_gdn"],
            w_mix_out=lambda: G_["w_mo"].reshape(NSH, D // NSH, D), ffn2_w_in=lambda: G_["ffn2_in"],
            ffn2_w_out=lambda: G_["ffn2_out"].reshape(NSH, DFF // NSH, D),
            w_ple_gate=lambda: G_["w_pg"].reshape(NSH, D // NSH, D), w_ple=lambda: G_["w_ple"])
        return [_sibling_sum(n, forms[n]()) for n in names]

    early_sums, mid_sums = [], []

    def early(G_):
        early_sums.extend(by_shard(G_, EARLY))
        return _Ride(True, [narrow for _, narrow in early_sums])

    def mid(G_):
        mid_sums.extend(by_shard(G_, MID))
        return _Ride(True, [narrow for _, narrow in mid_sums])

    loss_row, grad_x, G = _local_step(
        x[0], p[0, 0], loss_target[0], W, now=_Ride(False, [cast["ffn1_w_out"]]),
        soon=(_Ride(False, mix_halves[:1]), _Ride(False, mix_halves[1:]),
              lambda W_, arrived: W_.update(wp=_pack_wp(jnp.concatenate(arrived, axis=1)))),
        late=(_Ride(False, [cast[n] for n in LATER]), fill), early=early, mid=mid)
    loss = lax.psum(0.5 * jnp.sum(loss_row) / D, ("x", "y", "c"))

    last_sums = by_shard(G, LAST)
    landed = list(G["early"]) + list(G["mid"]) + list(_reduce_chips([narrow for _, narrow in last_sums]))
    grad = {n: _chip_sum_share(n, b, own).reshape(w[n].shape)
            for n, b, (own, _) in zip(EARLY + MID + LAST, landed, early_sums + mid_sums + last_sums)}

    pieces = [G["ln1_g"], G["ln1_b"], G["b_gate"], G["conv_w"].reshape(1, 4 * CW), G["alog"], G["dtb"], G["normw"],
              G["ln2_g"], G["ln2_b"], G["ln3_g"], G["ln3_b"], G["b_pg"], G["ln4_g"], G["ln4_b"]]
    flat = jnp.concatenate(pieces, axis=1)
    flat = jnp.pad(flat, ((0, 0), (0, NDEV * PACK_W - flat.shape[1])))
    total = _allreduce_small(flat.reshape(NDEV, PACK_W)).reshape(1, NDEV * PACK_W)
    off = 0
    for n, piece in zip(SMALL, pieces):
        grad[n] = total[0, off:off + piece.shape[1]]
        off += piece.shape[1]
    chip = 2 * lax.axis_index("x") + lax.axis_index("y")
    grad["conv_w"] = lax.dynamic_slice_in_dim(grad["conv_w"].reshape(4, CW), chip * (CW // NSH), CW // NSH, axis=1)
    grad["a_log"] = grad["a_log"][HEADS:2 * HEADS]
    grad["dt_bias"] = grad["dt_bias"][HEADS:2 * HEADS]

    delta, new_m, new_v = {}, {}, {}
    for n in ORDER:
        shape2 = w[n].shape if w[n].ndim == 2 else (1, w[n].shape[0])
        d_, m_, v_ = _adamw(n, *[a.reshape(shape2) for a in (w[n], grad[n], mom[n], var[n])])
        delta[n], new_m[n], new_v[n] = (a.reshape(args[n].shape) for a in (d_, m_, v_))
    outs = [loss, grad_x[None]]
    outs += [grad[n].reshape(args[n].shape) for n in ORDER]
    for group in (delta, new_m, new_v):
        outs += [group[n] for n in ORDER]
    return tuple(outs)
```

```python
import jax
import jax.numpy as jnp
from jax import lax
from jax.experimental import pallas as pl
from jax.experimental.pallas import tpu as pltpu

F32 = jnp.float32
BF = jnp.bfloat16
I32 = jnp.int32
MESH = pl.DeviceIdType.MESH
ANY = pl.BlockSpec(memory_space=pl.ANY)

D = 1024
DFF = 2816
NSH = 4
FSH = 2 * DFF // NSH
NIN = 5648
MSH = NIN // NSH
NP = 6144
CUT = 3600
GATE0 = 4096
HEADS = 8
HD = 64
CH = 64
KB = 128
BIG_ROWS = 1024
ALPHA = 2.0 ** 0.25
LN_EPS = 1e-5
RMS_EPS = 1e-6
B1, B2, LR, EPS, WD, STEP = 0.9, 0.999, 0.001, 1e-08, 0.01, 10


def _sigmoid(x):
    return 0.5 * jnp.tanh(0.5 * x) + 0.5


def _softplus(x):
    return jnp.maximum(x, 0.0) + jnp.log1p(jnp.exp(-jnp.abs(x)))


def _layer_norm(r, g, b):
    mu = jnp.mean(r, axis=-1, keepdims=True)
    xc = r - mu
    var = jnp.mean(xc * xc, axis=-1, keepdims=True)
    return xc * lax.rsqrt(var + LN_EPS) * g + b


def _layer_norm_bwd(r, g, dh):
    mu = jnp.mean(r, axis=-1, keepdims=True)
    xc = r - mu
    var = jnp.mean(xc * xc, axis=-1, keepdims=True)
    xhat = xc * lax.rsqrt(var + LN_EPS)
    dxh = dh * g
    dr = lax.rsqrt(var + LN_EPS) * (dxh - jnp.mean(dxh, axis=-1, keepdims=True) - xhat * jnp.mean(dxh * xhat, axis=-1, keepdims=True))
    return dr, jnp.sum(dh * xhat, axis=0, keepdims=True), jnp.sum(dh, axis=0, keepdims=True)


def _pick(n, cap):
    if n <= cap:
        return n
    for t in range(cap - cap % 8, 7, -8):
        if n % t == 0:
            return t
    raise ValueError((n, cap))


def _mm(name, a, b, M, N, K, *, tm, tn, tk, ta=False, tb=False, a_spec=None, b_spec=None, order="ij",
        extras=(), epilogue=None, outs, n_acc=0, ride=None):
    ni, nj, nk = M // tm, N // tn, K // tk
    nr = ride.n if ride else 0
    assert M % tm == 0 and N % tn == 0 and K % tk == 0, (name, M, N, K, tm, tn, tk)
    assert n_acc == 0 or nj == 1

    def wrap(fn):
        if order == "ij":
            return lambda g0, g1, g2: fn(g0, g1, g2)
        return lambda g0, g1, g2: fn(g1, g0, g2)

    if a_spec is None:
        a_spec = ((tk, tm), lambda i, j, k: (k, i)) if ta else ((tm, tk), lambda i, j, k: (i, k))
    if b_spec is None:
        b_spec = ((tn, tk), lambda i, j, k: (j, k)) if tb else ((tk, tn), lambda i, j, k: (k, j))
    dims = (((0 if ta else 1,), (1 if tb else 0,)), ((), ()))
    ne, no = len(extras), len(outs)
    grid = (ni, nj, nk) if order == "ij" else (nj, ni, nk)

    def body(*refs):
        a_ref, b_ref = refs[0], refs[1]
        ex = refs[2:2 + ne]
        ride_in = refs[2 + ne:2 + ne + nr]
        o = refs[2 + ne + nr:2 + ne + nr + no]
        ride_out = refs[2 + ne + nr + no:2 + ne + 2 * nr + no]
        scratch = refs[2 + ne + 2 * nr + no:]
        g0, g1, k = pl.program_id(0), pl.program_id(1), pl.program_id(2)
        first = jnp.logical_and(g0 == 0, g1 == 0)
        if ride:
            @pl.when(jnp.logical_and(first, k == 0))
            def _():
                ride.run("start", ride_in, ride_out, scratch[-3:])

        p = lax.dot_general(a_ref[...].astype(BF), b_ref[...].astype(BF), dims, preferred_element_type=F32)

        def finish(acc):
            vals = (acc,) if epilogue is None else epilogue(acc, *[e[...] for e in ex])
            for idx, (ref, val) in enumerate(zip(o, vals)):
                if idx < no - n_acc:
                    ref[...] = val.astype(ref.dtype)
                else:
                    @pl.when(first)
                    def _(ref=ref, val=val):
                        ref[...] = val

                    @pl.when(jnp.logical_not(first))
                    def _(ref=ref, val=val):
                        ref[...] += val

        if nk == 1:
            finish(p)
        else:
            acc_ref = scratch[0]

            @pl.when(k == 0)
            def _():
                acc_ref[...] = p

            @pl.when(k > 0)
            def _():
                acc_ref[...] += p

            @pl.when(k == nk - 1)
            def _():
                finish(acc_ref[...])

        if ride:
            @pl.when(jnp.logical_and(jnp.logical_and(g0 == grid[0] - 1, g1 == grid[1] - 1), k == nk - 1))
            def _():
                ride.run("wait", ride_in, ride_out, scratch[-3:])

    in_specs = [pl.BlockSpec(a_spec[0], wrap(a_spec[1])), pl.BlockSpec(b_spec[0], wrap(b_spec[1]))]
    in_specs += [pl.BlockSpec(blk, wrap(fn)) for _, blk, fn in extras] + [ANY] * nr
    res = pl.pallas_call(
        body, name=name, grid=grid, in_specs=in_specs,
        out_specs=[pl.BlockSpec(blk, wrap(fn)) for _, _, blk, fn in outs] + [ANY] * nr,
        out_shape=[jax.ShapeDtypeStruct(shape, dt) for shape, dt, _, _ in outs] + (ride.out_shape if ride else []),
        scratch_shapes=([pltpu.VMEM((tm, tn), F32)] if nk > 1 else []) + (_exchange_sems(nr) if ride else []),
    )(a, b, *[e[0] for e in extras], *(ride.arrays if ride else []))
    return res


def _row(i, j, k):
    return (i, 0)


def _tile(i, j, k):
    return (i, j)


def _const(i, j, k):
    return (0, 0)


def _rows(name, fn, n_steps, ins, outs, n_acc=0):
    ni, no = len(ins), len(outs)

    def body(*refs):
        i = pl.program_id(0)
        vals = fn(*[r[...] for r in refs[:ni]])
        for idx, (ref, val) in enumerate(zip(refs[ni:ni + no], vals)):
            if idx < no - n_acc:
                ref[...] = val.astype(ref.dtype)
            else:
                @pl.when(i == 0)
                def _(ref=ref, val=val):
                    ref[...] = val

                @pl.when(i > 0)
                def _(ref=ref, val=val):
                    ref[...] += val

    return pl.pallas_call(
        body, name=name, grid=(n_steps,),
        in_specs=[pl.BlockSpec(blk, fn_) for _, blk, fn_ in ins],
        out_specs=[pl.BlockSpec(blk, fn_) for _, _, blk, fn_ in outs],
        out_shape=[jax.ShapeDtypeStruct(shape, dt) for shape, dt, _, _ in outs],
    )(*[a for a, _, _ in ins])


def _ffn_fwd(tag, x, w_in, w_out, g, b, ride=None, ride2=None, w_out_ride=None):
    S = x.shape[0]
    tm = min(BIG_ROWS, S)
    gate, *came = _mm(f"{tag}_gate", x, w_in, S, DFF, D, tm=tm, tn=FSH, tk=D, order="ji",
                      b_spec=((None, D, FSH), lambda i, j, k: (j, 0, 0)), ride=w_out_ride,
                      outs=[((S, DFF), F32, (tm, FSH), _tile)])
    if w_out_ride:
        w_out = came[0].reshape(DFF, D)

    def up_epi(acc, gt):
        return acc, gt * _sigmoid(gt) * acc

    up, s, *arrived = _mm(f"{tag}_up", x, w_in, S, DFF, D, tm=tm, tn=FSH, tk=D, order="ji",
                          b_spec=((None, D, FSH), lambda i, j, k: (j + 2, 0, 0)),
                          extras=[(gate, (tm, FSH), _tile)], epilogue=up_epi, ride=ride,
                          outs=[((S, DFF), F32, (tm, FSH), _tile), ((S, DFF), BF, (tm, FSH), _tile)])

    def out_epi(acc, xin, gg, bb):
        r = ALPHA * xin + 0.5 * acc
        return _layer_norm(r, gg, bb), r

    h, r, *more = _mm(f"{tag}_out", s, w_out, S, D, DFF, tm=tm, tn=D, tk=DFF,
                      extras=[(x, (tm, D), _row), (g, (1, D), _const), (b, (1, D), _const)], epilogue=out_epi, ride=ride2,
                      outs=[((S, D), F32, (tm, D), _row), ((S, D), F32, (tm, D), _row)])
    return h, (x, gate, up, s, r), arrived + more, w_out


def _ln_bwd(tag, r, g, dh):
    S = r.shape[0]
    tm = min(512, S)
    return _rows(f"{tag}_lnbwd", _layer_norm_bwd, S // tm,
                 [(r, (tm, D), lambda i: (i, 0)), (g, (1, D), lambda i: (0, 0)), (dh, (tm, D), lambda i: (i, 0))],
                 [((S, D), F32, (tm, D), lambda i: (i, 0)), ((1, D), F32, (1, D), lambda i: (0, 0)),
                  ((1, D), F32, (1, D), lambda i: (0, 0))], n_acc=2)


def _grad_in(S, tm, dr, before):
    extras, outs = [(dr, (tm, D), _row)], [((S, D), F32, (tm, D), _row)]
    if before is None:
        return dict(extras=extras, epilogue=lambda acc, d: (acc + ALPHA * d,), outs=outs)
    extras += [(before[0], (tm, D), _row), (before[1], (1, D), _const)]
    outs += [((1, D), F32, (1, D), _const)] * 2
    return dict(extras=extras, epilogue=lambda acc, d, r_, g_: _layer_norm_bwd(r_, g_, acc + ALPHA * d), outs=outs, n_acc=2)


def _ln_bwd_or(tag, r, g, dh):
    return dh if isinstance(dh, (tuple, list)) else _ln_bwd(tag, r, g, dh)


def _ffn_bwd(tag, saved, w_in, w_out, g, dh, ride=None, before=None):
    x, gate, up, s, r = saved
    S = x.shape[0]
    tm = min(BIG_ROWS, S)
    dr, dg, db = _ln_bwd_or(tag, r, g, dh)

    def act_epi(acc, gt, u):
        ds = 0.5 * acc
        sg = _sigmoid(gt)
        return (jnp.stack([ds * u * (sg * (1.0 + gt * (1.0 - sg))), ds * (gt * sg)]),)

    da, = _mm(f"{tag}_dact", dr, w_out, S, DFF, D, tm=tm, tn=FSH, tk=D, tb=True, order="ji",
              extras=[(gate, (tm, FSH), _tile), (up, (tm, FSH), _tile)], epilogue=act_epi,
              outs=[((2, S, DFF), BF, (2, tm, FSH), lambda i, j, k: (0, i, j))])
    d_w_out, = _mm(f"{tag}_dwout", s, dr, DFF, D, S, tm=FSH, tn=D, tk=tm, ta=True,
                   epilogue=lambda acc: (0.5 * acc,), outs=[((DFF, D), F32, (FSH, D), _tile)])
    tb_ = min(BIG_ROWS, S)
    res = _mm(f"{tag}_dx", da, w_in, S, D, 2 * DFF, tm=tb_, tn=D, tk=FSH, tb=True,
              a_spec=((None, tb_, FSH), lambda i, j, k: (k // 2, i, k % 2)),
              b_spec=((None, D, FSH), lambda i, j, k: (k, 0, 0)), ride=ride, **_grad_in(S, tb_, dr, before))
    d_in, arrived = (res[0], res[1:]) if before is None else (tuple(res[:3]), res[3:])
    d_w_in, = _mm(f"{tag}_dwin", x, da, D, 2 * DFF, S, tm=D, tn=FSH, tk=tb_, ta=True, order="ji",
                  b_spec=((None, tb_, FSH), lambda i, j, k: (j // 2, k, j % 2)),
                  outs=[((NSH, D, FSH), F32, (None, D, FSH), lambda i, j, k: (j, 0, 0))])
    return d_in, d_w_in, d_w_out, dg, db, arrived


SB = 256
NKC = SB // KB
HPF = 8
HPS = 4
GW = HPS * HD
LOG2E = 1.4426950408889634


def _split(vals):
    hi = vals.astype(BF)
    return hi, (vals - hi.astype(F32)).astype(BF)


def _chunk_sums(tri2, vals):
    hi, lo = _split(vals)
    return [jnp.dot(tri2, jnp.concatenate([hi[c * KB:(c + 1) * KB], lo[c * KB:(c + 1) * KB]], axis=0), preferred_element_type=F32)
            for c in range(NKC)]


def _head_halves(t, axis):
    idx = lax.broadcasted_iota(I32, t.shape, axis)
    return [jnp.where(idx < HD, t, 0.0).astype(BF), jnp.where(idx >= HD, t, 0.0).astype(BF)]


QT = 512


def _diag_masks(sq):
    krow, qcol = lax.broadcasted_iota(I32, (SB, sq), 0), lax.broadcasted_iota(I32, (SB, sq), 1)
    return [krow + d * SB < qcol for d in range(sq // SB)]


def _softplus2(z):
    return jnp.maximum(z, 0.0) + jnp.log2(1.0 + jnp.exp2(jnp.minimum(z, -z)))


def _attn_prep(m):
    S = m.shape[0]
    tm = min(512, S)
    n = HEADS * HD
    return _rows("attn_prep", lambda k, v: (k, v, k.T, v.T), S // tm,
                 [(m, (tm, n), lambda i: (i, 1)), (m, (tm, n), lambda i: (i, 2))],
                 [((S, n), BF, (tm, n), lambda i: (i, 0)), ((S, n), BF, (tm, n), lambda i: (i, 0)),
                  ((n, S), BF, (n, tm), lambda i: (0, i)), ((n, S), BF, (n, tm), lambda i: (0, i))])


def _attn_fwd_t(m, kb_all, vt_all, ride=None):
    S = m.shape[0]
    SQ = min(QT, S)
    assert S % SQ == 0 and SQ % SB == 0
    HPS, GW = HPF, HPF * HD
    nkc, nqb, nh, nq = S // KB, SQ // SB, HEADS // HPS, S // SQ
    nr = ride.n if ride else 0

    def body(*refs):
        q_ref, kb_hbm, vt_hbm = refs[:3]
        ride_in, (o_ref, r_ref), ride_out = refs[3:3 + nr], refs[3 + nr:5 + nr], refs[5 + nr:5 + 2 * nr]
        kb, vt, acc = refs[5 + 2 * nr:8 + 2 * nr]
        sems = refs[8 + 2 * nr:]
        h, i = pl.program_id(0), pl.program_id(1)
        if ride:
            @pl.when(jnp.logical_and(h == 0, i == 0))
            def _():
                ride.run("start", ride_in, ride_out, sems)

        @pl.when(i == 0)
        def _():
            cols = pl.ds(pl.multiple_of(h * GW, GW), GW)
            pltpu.sync_copy(kb_hbm.at[:, cols], kb)
            pltpu.sync_copy(vt_hbm.at[cols, :], vt)

        qt = (q_ref[...] * (0.125 * LOG2E)).T
        qtm = [t for g in range(HPS // 2) for t in _head_halves(qt[g * KB:(g + 1) * KB], 0)]
        dmasks = _diag_masks(SQ)
        upper = (lax.broadcasted_iota(I32, (KB, KB), 1) >= lax.broadcasted_iota(I32, (KB, KB), 0)).astype(BF)
        tri2 = jnp.concatenate([upper, upper], axis=1)
        acc[...] = jnp.zeros_like(acc)
        r_ref[...] = jnp.zeros_like(r_ref)

        def block(jb, runs, dmask):
            masked = dmask is not None
            off = pl.multiple_of(jb * SB, SB)
            groups = [slice(g * KB, (g + 1) * KB) for g in range(HPS // 2)]
            kblk = [kb[pl.ds(off, SB), s] for s in groups]
            vtb = [vt[s, pl.ds(off, SB)] for s in groups]
            old = acc[...]
            zs = [jnp.dot(kblk[hh // 2], qtm[hh], preferred_element_type=F32) for hh in range(HPS)]
            sps = [_softplus2(z) for z in zs]
            if masked:
                sps = [jnp.where(dmask, sp, 0.0) for sp in sps]
            css = [_chunk_sums(tri2, sp) for sp in sps]
            run0s = [run + cs[1][0:1, :] for run, cs in zip(runs, css)]
            aa = [jnp.exp2(z - jnp.concatenate([run0 + cs[0], run + cs[1]], axis=0)) for z, run, run0, cs in zip(zs, runs, run0s, css)]
            if masked:
                aa = [jnp.where(dmask, a, 0.0) for a in aa]
            parts = [jnp.dot(vtb[hh // 2], aa[hh].astype(BF), preferred_element_type=F32) for hh in range(HPS)]
            upd = jnp.concatenate([parts[hh][(hh % 2) * HD:(hh % 2 + 1) * HD, :] for hh in range(HPS)], axis=0)
            for hh in range(HPS):
                r_ref[hh, pl.ds(NKC * jb, 1), :] = run0s[hh]
                r_ref[hh, pl.ds(NKC * jb + 1, 1), :] = runs[hh]
            acc[...] = old + upd
            return tuple(run0 + cs[0][0:1, :] for run0, cs in zip(run0s, css))

        runs = (jnp.zeros((1, SQ), F32),) * HPS
        for d in reversed(range(nqb)):
            runs = block(i * nqb + d, runs, dmasks[d])
        lax.fori_loop(0, i * nqb, lambda t, c: block(i * nqb - 1 - t, c, None), runs)
        o_ref[...] = acc[...].T
        if ride:
            @pl.when(jnp.logical_and(h == nh - 1, i == nq - 1))
            def _():
                ride.run("wait", ride_in, ride_out, sems)

    return pl.pallas_call(
        body, name="attn_fwd", grid=(nh, nq),
        in_specs=[pl.BlockSpec((SQ, GW), lambda h, i: (i, h)), ANY, ANY] + [ANY] * nr,
        out_specs=[pl.BlockSpec((SQ, GW), lambda h, i: (i, h)), pl.BlockSpec((HPS, nkc, SQ), lambda h, i: (h, 0, i))] + [ANY] * nr,
        out_shape=[jax.ShapeDtypeStruct((S, HEADS * HD), F32), jax.ShapeDtypeStruct((HEADS, nkc, S), F32)]
        + (ride.out_shape if ride else []),
        scratch_shapes=[pltpu.VMEM((S, GW), BF), pltpu.VMEM((GW, S), BF), pltpu.VMEM((GW, SQ), F32)]
        + (_exchange_sems(nr) if ride else []),
    )(m, kb_all, vt_all, *(ride.arrays if ride else []))


def _attn_bwd_t(m, kb_all, vb_all, kt_all, runs, dy, ride=None):
    S = m.shape[0]
    SQ = min(QT, S)
    nkc, nqb, nh, nq = S // KB, SQ // SB, HEADS // HPS, S // SQ
    nr = ride.n if ride else 0

    def body(*refs):
        q_ref, kb_hbm, vb_hbm, kt_hbm, r_ref, dy_ref = refs[:6]
        ride_in, (dq_ref, dk_hbm, dv_hbm), ride_out = refs[6:6 + nr], refs[6 + nr:9 + nr], refs[9 + nr:9 + 2 * nr]
        kb, vb, kt, dqt, dka, dva = refs[9 + 2 * nr:15 + 2 * nr]
        sems = refs[15 + 2 * nr:]
        h, i = pl.program_id(0), pl.program_id(1)
        cols = pl.ds(pl.multiple_of(h * GW, GW), GW)
        if ride:
            @pl.when(jnp.logical_and(h == 0, i == 0))
            def _():
                ride.run("start", ride_in, ride_out, sems)

        @pl.when(i == 0)
        def _():
            pltpu.sync_copy(kb_hbm.at[:, cols], kb)
            pltpu.sync_copy(vb_hbm.at[:, cols], vb)
            pltpu.sync_copy(kt_hbm.at[cols, :], kt)
            dka[...] = jnp.zeros_like(dka)
            dva[...] = jnp.zeros_like(dva)

        q8 = q_ref[...] * 0.125
        dyf = dy_ref[...]
        q8t, dyt = (q8 * LOG2E).T, dyf.T
        groups = [slice(g * KB, (g + 1) * KB) for g in range(HPS // 2)]
        qtm = [t for s in groups for t in _head_halves(q8t[s], 0)]
        dytm = [t for s in groups for t in _head_halves(dyt[s], 0)]
        qlm = [t for s in groups for t in _head_halves(q8[:, s], 1)]
        dylm = [t for s in groups for t in _head_halves(dyf[:, s], 1)]
        dmasks = _diag_masks(SQ)
        ri, ci = lax.broadcasted_iota(I32, (KB, KB), 0), lax.broadcasted_iota(I32, (KB, KB), 1)
        upper, lower = (ci >= ri).astype(BF), (ci <= ri).astype(BF)
        rev2 = jnp.concatenate([upper, upper], axis=1)
        dqt[...] = jnp.zeros_like(dqt)

        def block(jb, pres, dmask):
            masked = dmask is not None
            off = pl.multiple_of(jb * SB, SB)
            heads = range(HPS)
            kblk = [kb[pl.ds(off, SB), s] for s in groups]
            vblk = [vb[pl.ds(off, SB), s] for s in groups]
            ktb = [kt[s, pl.ds(off, SB)] for s in groups]
            run0s = [r_ref[hh, pl.ds(NKC * jb, 1), :] for hh in heads]
            run1s = [r_ref[hh, pl.ds(NKC * jb + 1, 1), :] for hh in heads]
            old_dq, old_dk, old_dv = dqt[...], dka[pl.ds(off, SB), :], dva[pl.ds(off, SB), :]
            dks, dvs, parts, new = [], [], [], []
            for g in range(HPS // 2):
                hs = (2 * g, 2 * g + 1)
                zs = [jnp.dot(kblk[g], qtm[hh], preferred_element_type=F32) for hh in hs]
                das = [jnp.dot(vblk[g], dytm[hh], preferred_element_type=F32) for hh in hs]
                sps = [_softplus2(z) for z in zs]
                sigs = [jnp.exp2(z - sp) for z, sp in zip(zs, sps)]
                if masked:
                    sps = [jnp.where(dmask, sp, 0.0) for sp in sps]
                css = [_chunk_sums(rev2, sp) for sp in sps]
                aa = [jnp.exp2(z - jnp.concatenate([run0s[hh] + cs[0], run1s[hh] + cs[1]], axis=0)) for z, hh, cs in zip(zs, hs, css)]
                if masked:
                    aa = [jnp.where(dmask, a, 0.0) for a in aa]
                gs = [a * da for a, da in zip(aa, das)]
                pgs = [[jnp.dot(lower, gg[c * KB:(c + 1) * KB].astype(BF), preferred_element_type=F32) for c in range(NKC)] for gg in gs]
                pre1s = [pres[hh] + pg[0][KB - 1:KB, :] for hh, pg in zip(hs, pgs)]
                dzs = [gg - sig * jnp.concatenate([pres[hh] + pg[0], pre1 + pg[1]], axis=0)
                       for gg, sig, hh, pre1, pg in zip(gs, sigs, hs, pre1s, pgs)]
                if masked:
                    dzs = [jnp.where(dmask, dz, 0.0) for dz in dzs]
                dzb, ab = [dz.astype(BF) for dz in dzs], [a.astype(BF) for a in aa]
                dks.append(sum(jnp.dot(dzb[t], qlm[hh], preferred_element_type=F32) for t, hh in enumerate(hs)))
                dvs.append(sum(jnp.dot(ab[t], dylm[hh], preferred_element_type=F32) for t, hh in enumerate(hs)))
                parts += [jnp.dot(ktb[g], dzb[t], preferred_element_type=F32)[t * HD:(t + 1) * HD, :] for t in range(2)]
                new += [pre1 + pg[1][KB - 1:KB, :] for pre1, pg in zip(pre1s, pgs)]
            dqt[...] = old_dq + jnp.concatenate(parts, axis=0)
            dka[pl.ds(off, SB), :] = old_dk + jnp.concatenate(dks, axis=1)
            dva[pl.ds(off, SB), :] = old_dv + jnp.concatenate(dvs, axis=1)
            return tuple(new)

        pres = lax.fori_loop(0, i * nqb, lambda jb, c: block(jb, c, None), (jnp.zeros((1, SQ), F32),) * HPS)
        for d in range(nqb):
            pres = block(i * nqb + d, pres, dmasks[d])
        dq_ref[...] = (dqt[...].T * 0.125).astype(dq_ref.dtype)

        @pl.when(i == nq - 1)
        def _():
            pltpu.sync_copy(dka, dk_hbm.at[:, cols])
            pltpu.sync_copy(dva, dv_hbm.at[:, cols])

        if ride:
            @pl.when(jnp.logical_and(h == nh - 1, i == nq - 1))
            def _():
                ride.run("wait", ride_in, ride_out, sems)

    n = HEADS * HD
    return pl.pallas_call(
        body, name="attn_bwd", grid=(nh, nq),
        in_specs=[pl.BlockSpec((SQ, GW), lambda h, i: (i, h)), ANY, ANY, ANY,
                  pl.BlockSpec((HPS, nkc, SQ), lambda h, i: (h, 0, i)), pl.BlockSpec((SQ, GW), lambda h, i: (i, h))] + [ANY] * nr,
        out_specs=[pl.BlockSpec((SQ, GW), lambda h, i: (i, h)), ANY, ANY] + [ANY] * nr,
        out_shape=[jax.ShapeDtypeStruct((S, n), BF), jax.ShapeDtypeStruct((S, n), F32), jax.ShapeDtypeStruct((S, n), F32)]
        + (ride.out_shape if ride else []),
        scratch_shapes=[pltpu.VMEM((S, GW), BF), pltpu.VMEM((S, GW), BF), pltpu.VMEM((GW, S), BF), pltpu.VMEM((GW, SQ), F32),
                        pltpu.VMEM((S, GW), F32), pltpu.VMEM((S, GW), F32)] + (_exchange_sems(nr) if ride else []),
    )(m, kb_all, vb_all, kt_all, runs, dy, *(ride.arrays if ride else []))


CW = 3 * HEADS * HD


def _shift_down(cur, prev8, s):
    if s == 0:
        return cur
    r = pltpu.roll(cur, s, 0)
    first = jnp.where(lax.broadcasted_iota(I32, (8, cur.shape[1]), 0) < s, pltpu.roll(prev8, s, 0), r[:8])
    return jnp.concatenate([first, r[8:]], axis=0)


def _shift_up(cur, next8, s):
    if s == 0:
        return cur
    n = cur.shape[0]
    r = pltpu.roll(cur, n - s, 0)
    last = jnp.where(lax.broadcasted_iota(I32, (8, cur.shape[1]), 0) >= 8 - s, pltpu.roll(next8, 8 - s, 0), r[n - 8:])
    return jnp.concatenate([r[:n - 8], last], axis=0)


def _conv_fwd(m, conv_w):
    S = m.shape[0]
    tm = min(512, S)
    hb = tm // 8

    def body(x_ref, p_ref, w_ref, o_ref):
        i = pl.program_id(0)
        cur = x_ref[...]
        prev = jnp.where(i > 0, p_ref[...], 0.0)
        w = w_ref[...]
        acc = cur * w[3:4]
        for jk in range(3):
            acc = acc + _shift_down(cur, prev, 3 - jk) * w[jk:jk + 1]
        o_ref[...] = acc

    return pl.pallas_call(
        body, name="conv_fwd", grid=(S // tm,),
        in_specs=[pl.BlockSpec((tm, CW), lambda i: (i, 1)), pl.BlockSpec((8, CW), lambda i: (jnp.maximum(i * hb - 1, 0), 1)),
                  pl.BlockSpec((4, CW), lambda i: (0, 0))],
        out_specs=pl.BlockSpec((tm, CW), lambda i: (i, 0)),
        out_shape=jax.ShapeDtypeStruct((S, CW), F32),
    )(m, m, conv_w)


def _conv_bwd(m, dyc, conv_w):
    S = m.shape[0]
    tm = min(512, S)
    hb = tm // 8
    nt = S // tm

    def body(x_ref, p_ref, d_ref, n_ref, w_ref, dx_ref, dw_ref):
        i = pl.program_id(0)
        cur = x_ref[...]
        prev = jnp.where(i > 0, p_ref[...], 0.0)
        d = d_ref[...]
        nxt = jnp.where(i < nt - 1, n_ref[...], 0.0)
        w = w_ref[...]
        acc = d * w[3:4]
        dws = []
        for jk in range(3):
            acc = acc + _shift_up(d, nxt, 3 - jk) * w[jk:jk + 1]
            dws.append(jnp.sum(d * _shift_down(cur, prev, 3 - jk), axis=0, keepdims=True))
        dws.append(jnp.sum(d * cur, axis=0, keepdims=True))
        dx_ref[...] = acc.astype(dx_ref.dtype)
        dw = jnp.concatenate(dws, axis=0)

        @pl.when(i == 0)
        def _():
            dw_ref[...] = dw

        @pl.when(i > 0)
        def _():
            dw_ref[...] += dw

    return pl.pallas_call(
        body, name="conv_bwd", grid=(nt,),
        in_specs=[pl.BlockSpec((tm, CW), lambda i: (i, 1)), pl.BlockSpec((8, CW), lambda i: (jnp.maximum(i * hb - 1, 0), 1)),
                  pl.BlockSpec((tm, CW), lambda i: (i, 0)),
                  pl.BlockSpec((8, CW), lambda i: (jnp.minimum((i + 1) * hb, S // 8 - 1), 0)),
                  pl.BlockSpec((4, CW), lambda i: (0, 0))],
        out_specs=[pl.BlockSpec((tm, CW), lambda i: (i, 0)), pl.BlockSpec((4, CW), lambda i: (0, 0))],
        out_shape=[jax.ShapeDtypeStruct((S, CW), BF), jax.ShapeDtypeStruct((4, CW), F32)],
    )(m, m, dyc, dyc, conv_w)


def _t(a):
    return jnp.swapaxes(a, 1, 2)


def _bdot(a, b):
    return jnp.einsum("hik,hkj->hij", a, b, preferred_element_type=F32)


@jax.custom_vjp
def _mm1(a, b):
    return _bdot(a.astype(BF), b.astype(BF))


def _bdot_nt(a, b):
    return jnp.einsum("hij,hkj->hik", a, b, preferred_element_type=F32)


def _bdot_tn(a, b):
    return jnp.einsum("hki,hkj->hij", a, b, preferred_element_type=F32)


_mm1.defvjp(lambda a, b: (_mm1(a, b), (a.astype(BF), b.astype(BF))),
            lambda res, dc: (_bdot_nt(dc.astype(BF), res[1]), _bdot_tn(res[0], dc.astype(BF))))


@jax.custom_vjp
def _mm1_nt(a, b):
    return _bdot_nt(a.astype(BF), b.astype(BF))


_mm1_nt.defvjp(lambda a, b: (_mm1_nt(a, b), (a.astype(BF), b.astype(BF))),
               lambda res, dc: (_bdot(dc.astype(BF), res[1]), _bdot_tn(dc.astype(BF), res[0])))


@jax.custom_vjp
def _mm1_tn(a, b):
    return _bdot_tn(a.astype(BF), b.astype(BF))


_mm1_tn.defvjp(lambda a, b: (_mm1_tn(a, b), (a.astype(BF), b.astype(BF))),
               lambda res, dc: (_bdot_nt(res[1], dc.astype(BF)), _bdot(res[0], dc.astype(BF))))


def _stack_rows(hi, lo):
    return jnp.concatenate([hi, lo], axis=1)


@jax.custom_vjp
def _mm3(a, b):
    (ah, al), (bh, bl) = _split(a), _split(b)
    n = a.shape[1]
    two = _bdot(_stack_rows(ah, al), bh)
    return two[:, :n] + two[:, n:] + _bdot(ah, bl)


def _mm3_fwd(a, b):
    return _mm3(a, b), (_split(a), _split(b))


def _mm3_bwd(res, dc):
    (ah, al), (bh, bl) = res
    dh, dl = _split(dc)
    n = dc.shape[1]
    two = _bdot_nt(_stack_rows(dh, dl), bh)
    da = two[:, :n] + two[:, n:] + _bdot_nt(dh, bl)
    db = _bdot_tn(jnp.concatenate([ah, ah, al], axis=1), jnp.concatenate([dh, dl, dh], axis=1))
    return da, db


_mm3.defvjp(_mm3_fwd, _mm3_bwd)


@jax.custom_vjp
def _unit_lower_inverse(lower):
    eye = (lax.broadcasted_iota(I32, lower.shape, 1) == lax.broadcasted_iota(I32, lower.shape, 2)).astype(F32)
    pw = -lower
    inv = eye + pw
    pw = _mm3(pw, pw)
    for _ in range(4):
        both = _mm3(jnp.concatenate([inv, pw], axis=1), pw)
        inv, pw = inv + both[:, :CH], both[:, CH:]
    return inv + _mm3(inv, pw)


def _unit_lower_inverse_bwd(inv, d):
    t = _t(inv)
    return (-_mm3(_mm3(t, d), t),)


_unit_lower_inverse.defvjp(lambda lower: (_unit_lower_inverse(lower),) * 2, _unit_lower_inverse_bwd)


@jax.custom_vjp
def _saved_inverse(lower, inv):
    return inv


_saved_inverse.defvjp(lambda lower, inv: (inv, inv),
                      lambda inv, d: _unit_lower_inverse_bwd(inv, d) + (jnp.zeros_like(inv),))


def _mm_exact(c3, b):
    hi, lo = _split(b)
    lo2 = (b - hi.astype(F32) - lo.astype(F32)).astype(BF)
    return _bdot(c3, jnp.concatenate([hi, lo, lo2], axis=-2))


@jax.custom_vjp
def _cumsum_rows(b):
    return _mm_exact(_tri3(True), b)


def _tri3(lower):
    ri = lax.broadcasted_iota(I32, (HEADS, CH, CH), 1)
    ci = lax.broadcasted_iota(I32, (HEADS, CH, CH), 2)
    tri = (ri >= ci if lower else ri <= ci).astype(BF)
    return jnp.concatenate([tri, tri, tri], axis=-1)


_cumsum_rows.defvjp(lambda b: (_cumsum_rows(b), None), lambda _, dc: (_mm_exact(_tri3(False), dc),))


CPS = 4


def _gdn_chunk(yc, gz, gba, alog, dtb, inv_saved=None):
    def heads(t, off):
        return jnp.stack([t[:, off + h * HD: off + (h + 1) * HD] for h in range(HEADS)])

    def cols(t, off):
        return jnp.stack([jnp.broadcast_to(t[:, off + h: off + h + 1], (CH, CH)) for h in range(HEADS)])

    c = yc * _sigmoid(yc)
    q, k, v, zg = heads(c, 0), heads(c, HEADS * HD), heads(c, 2 * HEADS * HD), heads(gz, 0)
    q = q * lax.rsqrt(jnp.sum(q * q, axis=-1, keepdims=True) + RMS_EPS) * (HD ** -0.5)
    k = k * lax.rsqrt(jnp.sum(k * k, axis=-1, keepdims=True) + RMS_EPS)
    beta = cols(_sigmoid(gba), 0)
    g = cols(-jnp.exp(alog) * _softplus(gba + dtb), HEADS)
    ri = lax.broadcasted_iota(I32, (HEADS, CH, CH), 1)
    ci = lax.broadcasted_iota(I32, (HEADS, CH, CH), 2)
    causal, strict = ri >= ci, ri > ci
    gc = _cumsum_rows(g)
    gr = _t(gc)
    decay = jnp.where(causal, jnp.exp(jnp.where(causal, gc - gr, 0.0)), 0.0)
    lower = jnp.where(strict, beta * _mm1_nt(k, k) * decay, 0.0)
    inv = _unit_lower_inverse(lower) if inv_saved is None else _saved_inverse(lower, inv_saved)
    eg = jnp.exp(gc)
    uw = _mm3(inv, jnp.concatenate([v * beta, k * (beta * eg)], axis=2))
    u, w = uw[:, :, :HD], uw[:, :, HD:]
    qk = jnp.where(causal, _mm1_nt(q, k) * decay, 0.0)
    g_last = gc[:, CH - 1:CH, :]
    return (u, jnp.concatenate([w, q * eg], axis=1), qk, k * jnp.exp(g_last - gc), jnp.exp(g_last), zg * _sigmoid(zg)), inv


def _gdn_advance(state, pre, normw):
    u, wq, qk, kd, last, gate = pre
    ws = _mm1(wq, state)
    v_new = u - ws[:, :CH]
    o = ws[:, CH:] + _mm1(qk, v_new)
    new_state = state * last + _mm1_tn(kd, v_new)
    o = o * lax.rsqrt(jnp.mean(o * o, axis=-1, keepdims=True) + RMS_EPS) * normw * gate
    return jnp.concatenate([o[h] for h in range(HEADS)], axis=1), new_state


def _gdn_chunks(state, yc, gz, gba, alog, dtb, normw, invs=None):
    rows = [slice(c * CH, (c + 1) * CH) for c in range(yc.shape[0] // CH)]
    pres = [_gdn_chunk(yc[r], gz[r], gba[r], alog, dtb, None if invs is None else invs[c]) for c, r in enumerate(rows)]
    outs = []
    for pre, _ in pres:
        o, state = _gdn_advance(state, pre, normw)
        outs.append(o)
    return jnp.concatenate(outs, axis=0), state, jnp.stack([inv for _, inv in pres])


def _gdn_fwd(yc, m, alog, dtb, normw):
    S = yc.shape[0]
    RS = CPS * CH
    nch = S // RS

    def body(y_ref, gz_ref, gba_ref, al_ref, dt_ref, nw_ref, o_ref, st_ref, inv_ref, st):
        @pl.when(pl.program_id(0) == 0)
        def _():
            st[...] = jnp.zeros_like(st)

        cur = st[...]
        st_ref[0] = cur
        o, new, invs = _gdn_chunks(cur, y_ref[...], gz_ref[...], gba_ref[...], al_ref[...], dt_ref[...], nw_ref[...])
        o_ref[...] = o
        inv_ref[...] = invs
        st[...] = new

    og, states, invs = pl.pallas_call(
        body, name="gdn_fwd", grid=(nch,),
        in_specs=[pl.BlockSpec((RS, CW), lambda n: (n, 0)), pl.BlockSpec((RS, HEADS * HD), lambda n: (n, 6)),
                  pl.BlockSpec((RS, 128), lambda n: (n, 28)), pl.BlockSpec((1, 128), lambda n: (0, 0)),
                  pl.BlockSpec((1, 128), lambda n: (0, 0)), pl.BlockSpec((1, HD), lambda n: (0, 0))],
        out_specs=[pl.BlockSpec((RS, HEADS * HD), lambda n: (n, 0)), pl.BlockSpec((1, HEADS, HD, HD), lambda n: (n, 0, 0, 0)),
                   pl.BlockSpec((CPS, HEADS, HD, HD), lambda n: (n, 0, 0, 0))],
        out_shape=[jax.ShapeDtypeStruct((S, HEADS * HD), F32), jax.ShapeDtypeStruct((nch, HEADS, HD, HD), F32),
                   jax.ShapeDtypeStruct((S // CH, HEADS, HD, HD), F32)],
        scratch_shapes=[pltpu.VMEM((HEADS, HD, HD), F32)],
    )(yc, m, m, alog, dtb, normw)
    return og, (states, invs)


def _gdn_bwd(yc, m, alog, dtb, normw, states, dog):
    S = yc.shape[0]
    RS = CPS * CH
    nch = S // RS
    states, invs = states

    def body(y_ref, gz_ref, gba_ref, al_ref, dt_ref, nw_ref, st_ref, inv_ref, do_ref, dy_ref, dgz_ref, dgba_ref, dal_ref, ddt_ref,
             dnw_ref, dst):
        n = pl.program_id(0)

        @pl.when(n == 0)
        def _():
            dst[...] = jnp.zeros_like(dst)

        kept = inv_ref[...]
        _, vjp = jax.vjp(lambda *a: _gdn_chunks(*a, invs=kept)[:2],
                         st_ref[0], y_ref[...], gz_ref[...], gba_ref[...], al_ref[...], dt_ref[...], nw_ref[...])
        d_state, d_y, d_gz, d_gba, d_al, d_dt, d_nw = vjp((do_ref[...], dst[...]))
        dst[...] = d_state
        dy_ref[...] = d_y
        dgz_ref[...] = d_gz.astype(dgz_ref.dtype)
        dgba_ref[...] = d_gba.astype(dgba_ref.dtype)
        for ref, val in ((dal_ref, d_al), (ddt_ref, d_dt), (dnw_ref, d_nw)):
            @pl.when(n == 0)
            def _(ref=ref, val=val):
                ref[...] = val

            @pl.when(n > 0)
            def _(ref=ref, val=val):
                ref[...] += val

    rev = lambda n: nch - 1 - n
    return pl.pallas_call(
        body, name="gdn_bwd", grid=(nch,),
        in_specs=[pl.BlockSpec((RS, CW), lambda n: (rev(n), 0)), pl.BlockSpec((RS, HEADS * HD), lambda n: (rev(n), 6)),
                  pl.BlockSpec((RS, 128), lambda n: (rev(n), 28)), pl.BlockSpec((1, 128), lambda n: (0, 0)),
                  pl.BlockSpec((1, 128), lambda n: (0, 0)), pl.BlockSpec((1, HD), lambda n: (0, 0)),
                  pl.BlockSpec((1, HEADS, HD, HD), lambda n: (rev(n), 0, 0, 0)),
                  pl.BlockSpec((CPS, HEADS, HD, HD), lambda n: (rev(n), 0, 0, 0)),
                  pl.BlockSpec((RS, HEADS * HD), lambda n: (rev(n), 0))],
        out_specs=[pl.BlockSpec((RS, CW), lambda n: (rev(n), 0)), pl.BlockSpec((RS, HEADS * HD), lambda n: (rev(n), 0)),
                   pl.BlockSpec((RS, 128), lambda n: (rev(n), 0)), pl.BlockSpec((1, 128), lambda n: (0, 0)),
                   pl.BlockSpec((1, 128), lambda n: (0, 0)), pl.BlockSpec((1, HD), lambda n: (0, 0))],
        out_shape=[jax.ShapeDtypeStruct((S, CW), F32), jax.ShapeDtypeStruct((S, HEADS * HD), BF),
                   jax.ShapeDtypeStruct((S, 128), BF), jax.ShapeDtypeStruct((1, 128), F32),
                   jax.ShapeDtypeStruct((1, 128), F32), jax.ShapeDtypeStruct((1, HD), F32)],
        scratch_shapes=[pltpu.VMEM((HEADS, HD, HD), F32)],
    )(yc, m, m, alog, dtb, normw, states, invs, dog)


def _mixer_fwd(h1, W, late=None):
    S = h1.shape[0]
    tm = min(512, S)
    n = HEADS * HD
    tb_ = min(BIG_ROWS, S)
    m, = _mm("mix_in", h1, W["wp"], S, NP, D, tm=tb_, tn=1536, tk=D, order="ji", outs=[((S, NP), F32, (tb_, 1536), _tile)])
    kb_all, vb_all, kt_all, vt_all = _attn_prep(m)
    ya, runs, *arrived = _attn_fwd_t(m, kb_all, vt_all, ride=late[0] if late else None)
    if late:
        late[1](W, arrived)
    runs = (kb_all, vb_all, kt_all, runs)
    b_gate, conv_w, alog, dtb, normw, w_sb, w_gdn, w_mo, g, b = (
        W[k] for k in ("b_gate", "conv_w", "alog", "dtb", "normw", "w_sb", "w_gdn", "w_mo", "ln2_g", "ln2_b"))
    yc = _conv_fwd(m, conv_w)
    og, states = _gdn_fwd(yc, m, alog, dtb, normw)
    ysb, = _mm("mix_sb", ya, w_sb, S, D, n, tm=tm, tn=D, tk=n, outs=[((S, D), F32, (tm, D), _row)])

    def merge_epi(acc, ys, gs, gg, bg):
        return _sigmoid(gs + bg[:, :D]) * ys + _sigmoid(gg + bg[:, D:]) * acc, acc

    u, ygdn = _mm("mix_gdn", og, w_gdn, S, D, n, tm=tm, tn=D, tk=n,
                  extras=[(ysb, (tm, D), _row), (m, (tm, D), lambda i, j, k: (i, GATE0 // D)),
                          (m, (tm, D), lambda i, j, k: (i, GATE0 // D + 1)), (b_gate, (1, 2 * D), _const)],
                  epilogue=merge_epi, outs=[((S, D), BF, (tm, D), _row), ((S, D), F32, (tm, D), _row)])

    def out_epi(acc, xin, gg, bb):
        r = ALPHA * xin + acc
        return _layer_norm(r, gg, bb), r

    h2, r2 = _mm("mix_out", u, w_mo, S, D, D, tm=tm, tn=D, tk=D,
                 extras=[(h1, (tm, D), _row), (g, (1, D), _const), (b, (1, D), _const)], epilogue=out_epi,
                 outs=[((S, D), F32, (tm, D), _row), ((S, D), F32, (tm, D), _row)])
    return h2, (h1, m, ya, runs, yc, og, states, ysb, ygdn, u, r2)


def _mixer_bwd(saved, W, dh, ride=None, before=None):
    h1, m, ya, runs, yc, og, states, ysb, ygdn, u, r2 = saved
    wp, b_gate, conv_w, alog, dtb, normw, w_sb, w_gdn, w_mo, g = (
        W[k] for k in ("wp", "b_gate", "conv_w", "alog", "dtb", "normw", "w_sb", "w_gdn", "w_mo", "ln2_g"))
    S = h1.shape[0]
    tm = min(512, S)
    n = HEADS * HD
    dr, dg, db = _ln_bwd_or("mix", r2, g, dh)

    def merge_epi(du, ys, yg, gs, gg, bg):
        s1, s2 = _sigmoid(gs + bg[:, :D]), _sigmoid(gg + bg[:, D:])
        dgate = jnp.concatenate([du * ys * s1 * (1.0 - s1), du * yg * s2 * (1.0 - s2)], axis=1)
        return du * s1, du * s2, dgate, jnp.sum(dgate, axis=0, keepdims=True)

    dysb, dygdn, dgate, d_bg = _mm(
        "mix_dmerge", dr, w_mo, S, D, D, tm=tm, tn=D, tk=D, tb=True,
        extras=[(ysb, (tm, D), _row), (ygdn, (tm, D), _row), (m, (tm, D), lambda i, j, k: (i, GATE0 // D)),
                (m, (tm, D), lambda i, j, k: (i, GATE0 // D + 1)), (b_gate, (1, 2 * D), _const)],
        epilogue=merge_epi, n_acc=1,
        outs=[((S, D), BF, (tm, D), _row), ((S, D), BF, (tm, D), _row), ((S, 2 * D), BF, (tm, 2 * D), _row),
              ((1, 2 * D), F32, (1, 2 * D), _const)])
    d_w_mo, = _mm("mix_dwmo", u, dr, D, D, S, tm=D, tn=D, tk=min(BIG_ROWS, S), ta=True, outs=[((D, D), F32, (D, D), _tile)])
    dya, = _mm("mix_dya", dysb, w_sb, S, n, D, tm=tm, tn=n, tk=D, tb=True, outs=[((S, n), F32, (tm, n), _row)])
    col_shards = [((NSH, n, D // NSH), F32, (None, n, D // NSH), lambda i, j, k: (j, 0, 0))]
    d_w_sb, = _mm("mix_dwsb", ya, dysb, n, D, S, tm=n, tn=D // NSH, tk=min(BIG_ROWS, S), ta=True, order="ji", outs=col_shards)
    dog, = _mm("mix_dog", dygdn, w_gdn, S, n, D, tm=tm, tn=n, tk=D, tb=True, outs=[((S, n), F32, (tm, n), _row)])
    d_w_gdn, = _mm("mix_dwgdn", og, dygdn, n, D, S, tm=n, tn=D // NSH, tk=min(BIG_ROWS, S), ta=True, order="ji", outs=col_shards)
    dq, dk, dv, *arrived = _attn_bwd_t(m, *runs, dya, ride=ride)
    dyc, dgz, dgba, d_alog, d_dtb, d_normw = _gdn_bwd(yc, m, alog, dtb, normw, states, dog)
    dxc, d_conv = _conv_bwd(m, dyc, conv_w)
    dm = jnp.concatenate([dq, dk.astype(BF), dv.astype(BF), dxc, dgz, dgba, jnp.zeros((S, GATE0 - 3712), BF), dgate], axis=1)
    tb_ = min(BIG_ROWS, S)
    res = _mm("mix_dh", dm, wp, S, D, NP, tm=tb_, tn=D, tk=1536, tb=True, **_grad_in(S, tb_, dr, before))
    d_h1 = res[0] if before is None else tuple(res)
    d_wp, = _mm("mix_dwp", h1, dm, D, NP, S, tm=D, tn=1536, tk=tb_, ta=True, order="ji",
                outs=[((D, NP), F32, (D, 1536), _tile)])
    return d_h1, dict(wp=d_wp, b_gate=d_bg, conv_w=d_conv, alog=d_alog, dtb=d_dtb, normw=d_normw,
                      w_sb=d_w_sb, w_gdn=d_w_gdn, w_mo=d_w_mo, g=dg, b=db), arrived


def _ple_fwd(h3, p, w_pg, b_pg, w_ple, g, b, target):
    S = h3.shape[0]
    tm = min(512, S)
    pd = p.shape[1]
    pe, = _mm("ple_emb", p, w_ple, S, D, pd, tm=tm, tn=D, tk=pd, outs=[((S, D), F32, (tm, D), _row)])

    def epi(acc, e, xin, tgt, bp, gg, bb):
        gt = _sigmoid(acc + bp)
        r = ALPHA * xin + gt * e
        diff = _layer_norm(r, gg, bb) - tgt
        return gt, r, diff * (1.0 / D), jnp.sum(diff * diff, axis=0, keepdims=True)

    gt, r4, dh4, loss_row = _mm(
        "ple_gate", h3, w_pg, S, D, D, tm=tm, tn=D, tk=D,
        extras=[(pe, (tm, D), _row), (h3, (tm, D), _row), (target, (tm, D), _row), (b_pg, (1, D), _const),
                (g, (1, D), _const), (b, (1, D), _const)], epilogue=epi, n_acc=1,
        outs=[((S, D), F32, (tm, D), _row), ((S, D), F32, (tm, D), _row), ((S, D), F32, (tm, D), _row),
              ((1, D), F32, (1, D), _const)])
    return dh4, loss_row, (h3, p, pe, gt, r4)


def _ple_bwd(saved, w_pg, g, dh4, before=None):
    h3, p, pe, gt, r4 = saved
    S = h3.shape[0]
    tm = min(512, S)
    pd = p.shape[1]

    def fn(r, gg, dh, e, t):
        dr, dg, db = _layer_norm_bwd(r, gg, dh)
        dpre = dr * e * t * (1.0 - t)
        return dr, dpre, dr * t, dg, db, jnp.sum(dpre, axis=0, keepdims=True)

    row, one = (lambda i: (i, 0)), (lambda i: (0, 0))
    dr, dpre, dpe, dg, db, d_bpg = _rows(
        "ple_lnbwd", fn, S // tm,
        [(r4, (tm, D), row), (g, (1, D), one), (dh4, (tm, D), row), (pe, (tm, D), row), (gt, (tm, D), row)],
        [((S, D), F32, (tm, D), row), ((S, D), BF, (tm, D), row), ((S, D), BF, (tm, D), row),
         ((1, D), F32, (1, D), one), ((1, D), F32, (1, D), one), ((1, D), F32, (1, D), one)], n_acc=3)
    d_w_pg, = _mm("ple_dwpg", h3, dpre, D, D, S, tm=D, tn=D, tk=min(BIG_ROWS, S), ta=True, outs=[((D, D), F32, (D, D), _tile)])
    d_w_ple, = _mm("ple_dwple", p, dpe, pd, D, S, tm=pd, tn=D // NSH, tk=min(BIG_ROWS, S), ta=True, order="ji",
                   outs=[((NSH, pd, D // NSH), F32, (None, pd, D // NSH), lambda i, j, k: (j, 0, 0))])
    res = _mm("ple_dh", dpre, w_pg, S, D, D, tm=tm, tn=D, tk=D, tb=True, **_grad_in(S, tm, dr, before))
    return (res[0] if before is None else tuple(res)), d_w_pg, d_bpg, d_w_ple, dg, db


def _local_step(x, p, target, W, now=None, soon=None, late=None, early=None, mid=None):
    W = dict(W)
    h1, sv1, arrived, W["ffn1_out"] = _ffn_fwd("ffn1", x, W["ffn1_in"], W.get("ffn1_out"), W["ln1_g"], W["ln1_b"],
                                               ride=soon[0] if soon else None, ride2=soon[1] if soon else None,
                                               w_out_ride=now)
    if soon:
        soon[2](W, arrived)
    h2, sv2 = _mixer_fwd(h1, W, late)
    h3, sv3, _, _ = _ffn_fwd("ffn2", h2, W["ffn2_in"], W["ffn2_out"], W["ln3_g"], W["ln3_b"])
    dh4, loss_row, sv4 = _ple_fwd(h3, p, W["w_pg"], W["b_pg"], W["w_ple"], W["ln4_g"], W["ln4_b"], target)
    G = {}
    dh3, G["w_pg"], G["b_pg"], G["w_ple"], G["ln4_g"], G["ln4_b"] = _ple_bwd(sv4, W["w_pg"], W["ln4_g"], dh4,
                                                                              before=(sv3[4], W["ln3_g"]))
    dh2, G["ffn2_in"], G["ffn2_out"], G["ln3_g"], G["ln3_b"], _ = _ffn_bwd(
        "ffn2", sv3, W["ffn2_in"], W["ffn2_out"], W["ln3_g"], dh3, before=(sv2[-1], W["ln2_g"]))
    dh1, gm, G["early"] = _mixer_bwd(sv2, W, dh2, ride=early(G) if early else None, before=(sv1[4], W["ln1_g"]))
    G.update({k: v for k, v in gm.items() if k not in ("g", "b")})
    G["ln2_g"], G["ln2_b"] = gm["g"], gm["b"]
    dx, G["ffn1_in"], G["ffn1_out"], G["ln1_g"], G["ln1_b"], G["mid"] = _ffn_bwd(
        "ffn1", sv1, W["ffn1_in"], W["ffn1_out"], W["ln1_g"], dh1, ride=mid(G) if mid else None)
    return loss_row, dx, G


S2 = CUT - 2 * MSH


def _pack_wp(w4):
    tr = 256

    def fn(w):
        s = [w[j].astype(F32) for j in range(NSH)]
        full = jnp.concatenate([s[0], s[1], s[2][:, :S2], jnp.zeros((tr, GATE0 - CUT), F32), s[2][:, S2:], s[3]], axis=1)
        return (full,)

    return _rows("pack_wp", fn, D // tr, [(w4, (NSH, tr, MSH), lambda i: (0, i, 0))],
                 [((D, NP), BF, (tr, NP), lambda i: (i, 0))])[0]


def _unpack_wp(d):
    tr = 256
    g2 = GATE0 + MSH - S2

    def fn(v):
        return (jnp.stack([v[:, :MSH], v[:, MSH:2 * MSH], jnp.concatenate([v[:, 2 * MSH:CUT], v[:, GATE0:g2]], axis=1), v[:, g2:]]),)

    return _rows("unpack_wp", fn, D // tr, [(d, (tr, NP), lambda i: (i, 0))],
                 [((NSH, D, MSH), F32, (NSH, tr, MSH), lambda i: (0, i, 0))])[0]


def _cast_bf16(tag, w):
    r, c = w.shape
    tr = _pick(r, 256)
    return _rows(f"cast_{tag}", lambda v: (v,), r // tr, [(w, (tr, c), lambda i: (i, 0))],
                 [((r, c), BF, (tr, c), lambda i: (i, 0))])[0]


def _place():
    return lax.axis_index("x"), lax.axis_index("y"), lax.axis_index("c")


def _chip_exchange(phase, scatter, ins, outs, send, recv, loc):
    x, y, c = _place()
    me = 2 * x + y
    chips = [(1 - x, y), (x, 1 - y), (1 - x, 1 - y)]
    for t in range(len(ins)):
        own = pltpu.make_async_copy(ins[t].at[me] if scatter else ins[t], outs[t].at[me], loc.at[t])
        out_going, in_coming = [], []
        for q, (px, py) in enumerate(chips):
            src = ins[t].at[2 * px + py] if scatter else ins[t]
            sems = dict(send_sem=send.at[3 * t + q], recv_sem=recv.at[3 * t + q], device_id=(px, py, c), device_id_type=MESH)
            out_going.append(pltpu.make_async_remote_copy(src_ref=src, dst_ref=outs[t].at[me], **sems))
            in_coming.append(pltpu.make_async_remote_copy(src_ref=src, dst_ref=outs[t].at[2 * px + py], **sems))
        if phase == "start":
            own.start()
            for cp in out_going:
                cp.start()
        else:
            for cp in in_coming:
                cp.wait_recv()
            own.wait()
            for cp in out_going:
                cp.wait_send()


def _exchange_sems(n):
    return [pltpu.SemaphoreType.DMA((3 * n,)), pltpu.SemaphoreType.DMA((3 * n,)), pltpu.SemaphoreType.DMA((n,))]


class _Ride:
    def __init__(self, scatter, arrays):
        self.scatter, self.arrays, self.n = scatter, list(arrays), len(arrays)
        self.out_shape = [jax.ShapeDtypeStruct(a.shape if scatter else (NSH,) + a.shape, a.dtype) for a in self.arrays]

    def run(self, phase, refs_in, refs_out, sems):
        _chip_exchange(phase, self.scatter, refs_in, refs_out, *sems)


def _gather_shards(shards):
    n = len(shards)

    def body(*refs):
        _chip_exchange("start", False, refs[:n], refs[n:2 * n], *refs[2 * n:])
        _chip_exchange("wait", False, refs[:n], refs[n:2 * n], *refs[2 * n:])

    return pl.pallas_call(
        body, name="gather_weights", in_specs=[ANY] * n, out_specs=[ANY] * n,
        out_shape=[jax.ShapeDtypeStruct((NSH,) + s.shape, s.dtype) for s in shards], scratch_shapes=_exchange_sems(n),
    )(*shards)


def _reduce_chips(ps):
    n = len(ps)

    def body(*refs):
        _chip_exchange("start", True, refs[:n], refs[n:2 * n], *refs[2 * n:])
        _chip_exchange("wait", True, refs[:n], refs[n:2 * n], *refs[2 * n:])

    return pl.pallas_call(
        body, name="reduce_chips", in_specs=[ANY] * n, out_specs=[ANY] * n,
        out_shape=[jax.ShapeDtypeStruct(p_.shape, p_.dtype) for p_ in ps], scratch_shapes=_exchange_sems(n),
    )(*ps)


def _chunk_rows(h, c):
    return _pick(h, max(8, 524288 // c // 8 * 8))


def _sibling_sum(tag, g):
    _, r, c = g.shape
    h = r // 2
    tr = _chunk_rows(h, c)
    nch = h // tr
    steps = NSH * nch

    def body(top_ref, bot_ref, out_ref, narrow_ref, stage, land, send, recv):
        s = pl.program_id(0)
        x, y, core = _place()

        def exchange(keep_ref, give_ref):
            stage[...] = give_ref[0].astype(BF)
            cp = pltpu.make_async_remote_copy(src_ref=stage, dst_ref=land.at[s], send_sem=send.at[s], recv_sem=recv.at[s],
                                              device_id=(x, y, 1 - core), device_id_type=MESH)
            cp.start()
            cp.wait()
            total = keep_ref[0] + land[s].astype(F32)
            out_ref[0] = total
            narrow_ref[0] = total.astype(BF)

        @pl.when(core == 0)
        def _():
            exchange(top_ref, bot_ref)

        @pl.when(core == 1)
        def _():
            exchange(bot_ref, top_ref)

    return pl.pallas_call(
        body, name=f"sibling_sum_{tag}", grid=(steps,),
        in_specs=[pl.BlockSpec((1, tr, c), lambda s: (s // nch, s % nch, 0)),
                  pl.BlockSpec((1, tr, c), lambda s: (s // nch, nch + s % nch, 0))],
        out_specs=[pl.BlockSpec((1, tr, c), lambda s: (s // nch, s % nch, 0))] * 2,
        out_shape=[jax.ShapeDtypeStruct((NSH, h, c), F32), jax.ShapeDtypeStruct((NSH, h, c), BF)],
        scratch_shapes=[pltpu.VMEM((tr, c), BF), pltpu.VMEM((steps, tr, c), BF), pltpu.SemaphoreType.DMA((steps,)),
                        pltpu.SemaphoreType.DMA((steps,))],
    )(g, g)


def _chip_sum_share(tag, b, own):
    _, h, c = b.shape
    tr = _chunk_rows(h, c)
    steps = h // tr

    def body(b_ref, own_ref, out_ref, stage, land, send, recv):
        s = pl.program_id(0)
        x, y, core = _place()
        me = 2 * x + y
        v = [jnp.where(me == j, own_ref[j], b_ref[j].astype(F32)) for j in range(NSH)]
        total = ((v[0] + v[1]) + v[2]) + v[3]
        stage[...] = total
        cp = pltpu.make_async_remote_copy(src_ref=stage, dst_ref=land.at[s], send_sem=send.at[s], recv_sem=recv.at[s],
                                          device_id=(x, y, 1 - core), device_id_type=MESH)
        cp.start()
        cp.wait()
        out_ref[core] = total
        out_ref[1 - core] = land[s]

    return pl.pallas_call(
        body, name=f"chip_sum_share_{tag}", grid=(steps,),
        in_specs=[pl.BlockSpec((NSH, tr, c), lambda s: (0, s, 0))] * 2,
        out_specs=pl.BlockSpec((2, tr, c), lambda s: (0, s, 0)),
        out_shape=jax.ShapeDtypeStruct((2, h, c), F32),
        scratch_shapes=[pltpu.VMEM((tr, c), F32), pltpu.VMEM((steps, tr, c), F32), pltpu.SemaphoreType.DMA((steps,)),
                        pltpu.SemaphoreType.DMA((steps,))],
    )(b, own)


NDEV = 8


def _allreduce_small(pack):
    r, w = pack.shape
    rel = [(dx, dy, dc) for dx in (0, 1) for dy in (0, 1) for dc in (0, 1) if (dx, dy, dc) != (0, 0, 0)]

    def body(in_ref, out_ref, buf, send, recv):
        x, y, c = _place()
        me = 4 * x + 2 * y + c
        buf[me] = in_ref[...]
        peers = [((x + dx) % 2, (y + dy) % 2, (c + dc) % 2) for dx, dy, dc in rel]
        sent = []
        for k, peer in enumerate(peers):
            cp = pltpu.make_async_remote_copy(src_ref=in_ref, dst_ref=buf.at[me], send_sem=send.at[k], recv_sem=recv.at[k],
                                              device_id=peer, device_id_type=MESH)
            cp.start()
            sent.append(cp)
        for k, (px, py, pc) in enumerate(peers):
            pltpu.make_async_remote_copy(src_ref=in_ref, dst_ref=buf.at[4 * px + 2 * py + pc], send_sem=send.at[k], recv_sem=recv.at[k],
                                         device_id=(px, py, pc), device_id_type=MESH).wait_recv()
        for cp in sent:
            cp.wait_send()
        acc = buf[0]
        for k in range(1, NDEV):
            acc = acc + buf[k]
        out_ref[...] = acc

    vm = pl.BlockSpec(memory_space=pltpu.VMEM)
    return pl.pallas_call(
        body, name="allreduce_small", in_specs=[vm], out_specs=vm, out_shape=jax.ShapeDtypeStruct((r, w), F32),
        scratch_shapes=[pltpu.VMEM((NDEV, r, w), F32), pltpu.SemaphoreType.DMA((NDEV - 1,)), pltpu.SemaphoreType.DMA((NDEV - 1,))],
    )(pack)


def _adamw(tag, w, g, m, v):
    r, c = w.shape
    tr = _pick(r, max(8, 262144 // c // 8 * 8))

    def fn(w_, g_, m_, v_):
        m2 = B1 * m_ + (1.0 - B1) * g_
        v2 = B2 * v_ + (1.0 - B2) * (g_ * g_)
        m_hat = m2 / (1.0 - B1 ** STEP)
        v_hat = v2 / (1.0 - B2 ** STEP)
        return -LR * (m_hat / (jnp.sqrt(v_hat) + EPS) + WD * w_), m2, v2

    spec = ((tr, c), lambda i: (i, 0))
    return _rows(f"adamw_{tag}", fn, r // tr, [(a,) + spec for a in (w, g, m, v)], [((r, c), F32) + spec] * 3)


BIG = ("ffn1_w_in", "ffn1_w_out", "w_mix_in", "w_branch_sb", "w_branch_gdn", "w_mix_out", "ffn2_w_in", "ffn2_w_out",
       "w_ple_gate", "w_ple")
FIRST = ("ffn1_w_in",)
SOON = ("w_mix_in",)
LATER = tuple(n for n in BIG if n not in FIRST + SOON + ("ffn1_w_out",))
EARLY = ("ffn2_w_in", "ffn2_w_out", "w_ple_gate", "w_ple")
MID = ("w_mix_in", "w_branch_sb", "w_branch_gdn", "w_mix_out")
LAST = ("ffn1_w_in", "ffn1_w_out")
SMALL = ("ln1_g", "ln1_b", "b_gate", "conv_w", "a_log", "dt_bias", "gdn_norm_w", "ln2_g", "ln2_b", "ln3_g", "ln3_b",
         "b_ple_gate", "ln4_g", "ln4_b")
ORDER = ("ffn1_w_in", "ffn1_w_out", "ln1_g", "ln1_b", "w_mix_in", "b_gate", "conv_w", "a_log", "dt_bias", "gdn_norm_w",
         "w_branch_sb", "w_branch_gdn", "w_mix_out", "ln2_g", "ln2_b", "ffn2_w_in", "ffn2_w_out", "ln3_g", "ln3_b",
         "w_ple_gate", "b_ple_gate", "w_ple", "ln4_g", "ln4_b")
PACK_W = 2304


def _lane_row(v, lanes=128, at=HEADS):
    return jnp.pad(v[None, :], ((0, 0), (at, lanes - at - v.shape[0])))


def _col_join(w4):
    return jnp.transpose(w4, (1, 0, 2)).reshape(w4.shape[1], NSH * w4.shape[2])


def kernel(x, p, ffn1_w_in, ffn1_w_out, ln1_g, ln1_b, w_mix_in, b_gate, conv_w, a_log, dt_bias, gdn_norm_w, w_branch_sb, w_branch_gdn, w_mix_out, ln2_g, ln2_b, ffn2_w_in, ffn2_w_out, ln3_g, ln3_b, w_ple_gate, b_ple_gate, w_ple, ln4_g, ln4_b, loss_target, m_ffn1_w_in, m_ffn1_w_out, m_ln1_g, m_ln1_b, m_w_mix_in, m_b_gate, m_conv_w, m_a_log, m_dt_bias, m_gdn_norm_w, m_w_branch_sb, m_w_branch_gdn, m_w_mix_out, m_ln2_g, m_ln2_b, m_ffn2_w_in, m_ffn2_w_out, m_ln3_g, m_ln3_b, m_w_ple_gate, m_b_ple_gate, m_w_ple, m_ln4_g, m_ln4_b, v_ffn1_w_in, v_ffn1_w_out, v_ln1_g, v_ln1_b, v_w_mix_in, v_b_gate, v_conv_w, v_a_log, v_dt_bias, v_gdn_norm_w, v_w_branch_sb, v_w_branch_gdn, v_w_mix_out, v_ln2_g, v_ln2_b, v_ffn2_w_in, v_ffn2_w_out, v_ln3_g, v_ln3_b, v_w_ple_gate, v_b_ple_gate, v_w_ple, v_ln4_g, v_ln4_b):
    args = dict(locals())
    w = {n: args[n][0] for n in ORDER}
    mom = {n: args["m_" + n][0] for n in ORDER}
    var = {n: args["v_" + n][0] for n in ORDER}

    cast = {n: _cast_bf16(n, w[n]) for n in BIG if n not in SOON}
    mix_halves = [_cast_bf16(f"w_mix_in_{t}", w["w_mix_in"][t * (D // 2):(t + 1) * (D // 2)]) for t in range(2)]
    full = dict(zip(FIRST + ("conv_w",), _gather_shards([cast[n] for n in FIRST] + [w["conv_w"]])))
    W = dict(
        ffn1_in=full["ffn1_w_in"], conv_w=_col_join(full["conv_w"]), b_gate=w["b_gate"][None], alog=_lane_row(w["a_log"]), dtb=_lane_row(w["dt_bias"]),
        normw=w["gdn_norm_w"][None], b_pg=w["b_ple_gate"][None],
        **{f"ln{i}_{s}": w[f"ln{i}_{s}"][None] for i in (1, 2, 3, 4) for s in ("g", "b")},
    )

    def fill(W_, arrived):
        got = dict(zip(LATER, arrived))
        W_.update(w_sb=_col_join(got["w_branch_sb"]), w_gdn=_col_join(got["w_branch_gdn"]), w_mo=got["w_mix_out"].reshape(D, D),
                  ffn2_in=got["ffn2_w_in"], ffn2_out=got["ffn2_w_out"].reshape(DFF, D),
                  w_pg=got["w_ple_gate"].reshape(D, D), w_ple=_col_join(got["w_ple"]))

    def by_shard(G_, names):
        forms = dict(
            ffn1_w_in=lambda: G_["ffn1_in"], ffn1_w_out=lambda: G_["ffn1_out"].reshape(NSH, DFF // NSH, D),
            w_mix_in=lambda: _unpack_wp(G_["wp"]), w_branch_sb=lambda: G_["w_sb"], w_branch_gdn=lambda: G_["w_gdn"],
            w_mix_out=lambda: G_["w_mo"].reshape(NSH, D // NSH, D), ffn2_w_in=lambda: G_["ffn2_in"],
            ffn2_w_out=lambda: G_["ffn2_out"].reshape(NSH, DFF // NSH, D),
            w_ple_gate=lambda: G_["w_pg"].reshape(NSH, D // NSH, D), w_ple=lambda: G_["w_ple"])
        return [_sibling_sum(n, forms[n]()) for n in names]

    early_sums, mid_sums = [], []

    def early(G_):
        early_sums.extend(by_shard(G_, EARLY))
        return _Ride(True, [narrow for _, narrow in early_sums])

    def mid(G_):
        mid_sums.extend(by_shard(G_, MID))
        return _Ride(True, [narrow for _, narrow in mid_sums])

    loss_row, grad_x, G = _local_step(
        x[0], p[0, 0], loss_target[0], W, now=_Ride(False, [cast["ffn1_w_out"]]),
        soon=(_Ride(False, mix_halves[:1]), _Ride(False, mix_halves[1:]),
              lambda W_, arrived: W_.update(wp=_pack_wp(jnp.concatenate(arrived, axis=1)))),
        late=(_Ride(False, [cast[n] for n in LATER]), fill), early=early, mid=mid)
    loss = lax.psum(0.5 * jnp.sum(loss_row) / D, ("x", "y", "c"))

    last_sums = by_shard(G, LAST)
    landed = list(G["early"]) + list(G["mid"]) + list(_reduce_chips([narrow for _, narrow in last_sums]))
    grad = {n: _chip_sum_share(n, b, own).reshape(w[n].shape)
            for n, b, (own, _) in zip(EARLY + MID + LAST, landed, early_sums + mid_sums + last_sums)}

    pieces = [G["ln1_g"], G["ln1_b"], G["b_gate"], G["conv_w"].reshape(1, 4 * CW), G["alog"], G["dtb"], G["normw"],
              G["ln2_g"], G["ln2_b"], G["ln3_g"], G["ln3_b"], G["b_pg"], G["ln4_g"], G["ln4_b"]]
    flat = jnp.concatenate(pieces, axis=1)
    flat = jnp.pad(flat, ((0, 0), (0, NDEV * PACK_W - flat.shape[1])))
    total = _allreduce_small(flat.reshape(NDEV, PACK_W)).reshape(1, NDEV * PACK_W)
    off = 0
    for n, piece in zip(SMALL, pieces):
        grad[n] = total[0, off:off + piece.shape[1]]
        off += piece.shape[1]
    chip = 2 * lax.axis_index("x") + lax.axis_index("y")
    grad["conv_w"] = lax.dynamic_slice_in_dim(grad["conv_w"].reshape(4, CW), chip * (CW // NSH), CW // NSH, axis=1)
    grad["a_log"] = grad["a_log"][HEADS:2 * HEADS]
    grad["dt_bias"] = grad["dt_bias"][HEADS:2 * HEADS]

    delta, new_m, new_v = {}, {}, {}
    for n in ORDER:
        shape2 = w[n].shape if w[n].ndim == 2 else (1, w[n].shape[0])
        d_, m_, v_ = _adamw(n, *[a.reshape(shape2) for a in (w[n], grad[n], mom[n], var[n])])
        delta[n], new_m[n], new_v[n] = (a.reshape(args[n].shape) for a in (d_, m_, v_))
    outs = [loss, grad_x[None]]
    outs += [grad[n].reshape(args[n].shape) for n in ORDER]
    for group in (delta, new_m, new_v):
        outs += [group[n] for n in ORDER]
    return tuple(outs)
```

```python
import jax
import jax.numpy as jnp
from jax import lax
from jax.experimental import pallas as pl
from jax.experimental.pallas import tpu as pltpu

F32 = jnp.float32
BF = jnp.bfloat16
I32 = jnp.int32
MESH = pl.DeviceIdType.MESH
ANY = pl.BlockSpec(memory_space=pl.ANY)

D = 1024
DFF = 2816
NSH = 4
FSH = 2 * DFF // NSH
NIN = 5648
MSH = NIN // NSH
NP = 6144
CUT = 3600
GATE0 = 4096
HEADS = 8
HD = 64
CH = 64
KB = 128
BIG_ROWS = 1024
ALPHA = 2.0 ** 0.25
LN_EPS = 1e-5
RMS_EPS = 1e-6
B1, B2, LR, EPS, WD, STEP = 0.9, 0.999, 0.001, 1e-08, 0.01, 10


def _sigmoid(x):
    return 0.5 * jnp.tanh(0.5 * x) + 0.5


def _softplus(x):
    return jnp.maximum(x, 0.0) + jnp.log1p(jnp.exp(-jnp.abs(x)))


def _layer_norm(r, g, b):
    mu = jnp.mean(r, axis=-1, keepdims=True)
    xc = r - mu
    var = jnp.mean(xc * xc, axis=-1, keepdims=True)
    return xc * lax.rsqrt(var + LN_EPS) * g + b


def _layer_norm_bwd(r, g, dh):
    mu = jnp.mean(r, axis=-1, keepdims=True)
    xc = r - mu
    var = jnp.mean(xc * xc, axis=-1, keepdims=True)
    xhat = xc * lax.rsqrt(var + LN_EPS)
    dxh = dh * g
    dr = lax.rsqrt(var + LN_EPS) * (dxh - jnp.mean(dxh, axis=-1, keepdims=True) - xhat * jnp.mean(dxh * xhat, axis=-1, keepdims=True))
    return dr, jnp.sum(dh * xhat, axis=0, keepdims=True), jnp.sum(dh, axis=0, keepdims=True)


def _pick(n, cap):
    if n <= cap:
        return n
    for t in range(cap - cap % 8, 7, -8):
        if n % t == 0:
            return t
    raise ValueError((n, cap))


def _mm(name, a, b, M, N, K, *, tm, tn, tk, ta=False, tb=False, a_spec=None, b_spec=None, order="ij",
        extras=(), epilogue=None, outs, n_acc=0, ride=None):
    ni, nj, nk = M // tm, N // tn, K // tk
    nr = ride.n if ride else 0
    assert M % tm == 0 and N % tn == 0 and K % tk == 0, (name, M, N, K, tm, tn, tk)
    assert n_acc == 0 or nj == 1

    def wrap(fn):
        if order == "ij":
            return lambda g0, g1, g2: fn(g0, g1, g2)
        return lambda g0, g1, g2: fn(g1, g0, g2)

    if a_spec is None:
        a_spec = ((tk, tm), lambda i, j, k: (k, i)) if ta else ((tm, tk), lambda i, j, k: (i, k))
    if b_spec is None:
        b_spec = ((tn, tk), lambda i, j, k: (j, k)) if tb else ((tk, tn), lambda i, j, k: (k, j))
    dims = (((0 if ta else 1,), (1 if tb else 0,)), ((), ()))
    ne, no = len(extras), len(outs)
    grid = (ni, nj, nk) if order == "ij" else (nj, ni, nk)

    def body(*refs):
        a_ref, b_ref = refs[0], refs[1]
        ex = refs[2:2 + ne]
        ride_in = refs[2 + ne:2 + ne + nr]
        o = refs[2 + ne + nr:2 + ne + nr + no]
        ride_out = refs[2 + ne + nr + no:2 + ne + 2 * nr + no]
        scratch = refs[2 + ne + 2 * nr + no:]
        g0, g1, k = pl.program_id(0), pl.program_id(1), pl.program_id(2)
        first = jnp.logical_and(g0 == 0, g1 == 0)
        if ride:
            @pl.when(jnp.logical_and(first, k == 0))
            def _():
                ride.run("start", ride_in, ride_out, scratch[-3:])

        p = lax.dot_general(a_ref[...].astype(BF), b_ref[...].astype(BF), dims, preferred_element_type=F32)

        def finish(acc):
            vals = (acc,) if epilogue is None else epilogue(acc, *[e[...] for e in ex])
            for idx, (ref, val) in enumerate(zip(o, vals)):
                if idx < no - n_acc:
                    ref[...] = val.astype(ref.dtype)
                else:
                    @pl.when(first)
                    def _(ref=ref, val=val):
                        ref[...] = val

                    @pl.when(jnp.logical_not(first))
                    def _(ref=ref, val=val):
                        ref[...] += val

        if nk == 1:
            finish(p)
        else:
            acc_ref = scratch[0]

            @pl.when(k == 0)
            def _():
                acc_ref[...] = p

            @pl.when(k > 0)
            def _():
                acc_ref[...] += p

            @pl.when(k == nk - 1)
            def _():
                finish(acc_ref[...])

        if ride:
            @pl.when(jnp.logical_and(jnp.logical_and(g0 == grid[0] - 1, g1 == grid[1] - 1), k == nk - 1))
            def _():
                ride.run("wait", ride_in, ride_out, scratch[-3:])

    in_specs = [pl.BlockSpec(a_spec[0], wrap(a_spec[1])), pl.BlockSpec(b_spec[0], wrap(b_spec[1]))]
    in_specs += [pl.BlockSpec(blk, wrap(fn)) for _, blk, fn in extras] + [ANY] * nr
    res = pl.pallas_call(
        body, name=name, grid=grid, in_specs=in_specs,
        out_specs=[pl.BlockSpec(blk, wrap(fn)) for _, _, blk, fn in outs] + [ANY] * nr,
        out_shape=[jax.ShapeDtypeStruct(shape, dt) for shape, dt, _, _ in outs] + (ride.out_shape if ride else []),
        scratch_shapes=([pltpu.VMEM((tm, tn), F32)] if nk > 1 else []) + (_exchange_sems(nr) if ride else []),
    )(a, b, *[e[0] for e in extras], *(ride.arrays if ride else []))
    return res


def _row(i, j, k):
    return (i, 0)


def _tile(i, j, k):
    return (i, j)


def _const(i, j, k):
    return (0, 0)


def _rows(name, fn, n_steps, ins, outs, n_acc=0):
    ni, no = len(ins), len(outs)

    def body(*refs):
        i = pl.program_id(0)
        vals = fn(*[r[...] for r in refs[:ni]])
        for idx, (ref, val) in enumerate(zip(refs[ni:ni + no], vals)):
            if idx < no - n_acc:
                ref[...] = val.astype(ref.dtype)
            else:
                @pl.when(i == 0)
                def _(ref=ref, val=val):
                    ref[...] = val

                @pl.when(i > 0)
                def _(ref=ref, val=val):
                    ref[...] += val

    return pl.pallas_call(
        body, name=name, grid=(n_steps,),
        in_specs=[pl.BlockSpec(blk, fn_) for _, blk, fn_ in ins],
        out_specs=[pl.BlockSpec(blk, fn_) for _, _, blk, fn_ in outs],
        out_shape=[jax.ShapeDtypeStruct(shape, dt) for shape, dt, _, _ in outs],
    )(*[a for a, _, _ in ins])


def _ffn_fwd(tag, x, w_in, w_out, g, b, ride=None, ride2=None, w_out_ride=None):
    S = x.shape[0]
    tm = min(BIG_ROWS, S)
    gate, *came = _mm(f"{tag}_gate", x, w_in, S, DFF, D, tm=tm, tn=FSH, tk=D, order="ji",
                      b_spec=((None, D, FSH), lambda i, j, k: (j, 0, 0)), ride=w_out_ride,
                      outs=[((S, DFF), F32, (tm, FSH), _tile)])
    if w_out_ride:
        w_out = came[0].reshape(DFF, D)

    def up_epi(acc, gt):
        return acc, gt * _sigmoid(gt) * acc

    up, s, *arrived = _mm(f"{tag}_up", x, w_in, S, DFF, D, tm=tm, tn=FSH, tk=D, order="ji",
                          b_spec=((None, D, FSH), lambda i, j, k: (j + 2, 0, 0)),
                          extras=[(gate, (tm, FSH), _tile)], epilogue=up_epi, ride=ride,
                          outs=[((S, DFF), F32, (tm, FSH), _tile), ((S, DFF), BF, (tm, FSH), _tile)])

    def out_epi(acc, xin, gg, bb):
        r = ALPHA * xin + 0.5 * acc
        h_ = _layer_norm(r, gg, bb)
        return h_, r, h_

    h, r, hb, *more = _mm(f"{tag}_out", s, w_out, S, D, DFF, tm=tm, tn=D, tk=DFF,
                          extras=[(x, (tm, D), _row), (g, (1, D), _const), (b, (1, D), _const)], epilogue=out_epi, ride=ride2,
                          outs=[((S, D), F32, (tm, D), _row), ((S, D), F32, (tm, D), _row), ((S, D), BF, (tm, D), _row)])
    return h, (x, gate, up, s, r, hb), arrived + more, w_out


def _ln_bwd(tag, r, g, dh):
    S = r.shape[0]
    tm = min(512, S)
    return _rows(f"{tag}_lnbwd", _layer_norm_bwd, S // tm,
                 [(r, (tm, D), lambda i: (i, 0)), (g, (1, D), lambda i: (0, 0)), (dh, (tm, D), lambda i: (i, 0))],
                 [((S, D), F32, (tm, D), lambda i: (i, 0)), ((1, D), F32, (1, D), lambda i: (0, 0)),
                  ((1, D), F32, (1, D), lambda i: (0, 0))], n_acc=2)


def _grad_in(S, tm, dr, before):
    extras, outs = [(dr, (tm, D), _row)], [((S, D), F32, (tm, D), _row)]
    if before is None:
        return dict(extras=extras, epilogue=lambda acc, d: (acc + ALPHA * d,), outs=outs)
    extras += [(before[0], (tm, D), _row), (before[1], (1, D), _const)]
    outs += [((1, D), F32, (1, D), _const)] * 2
    return dict(extras=extras, epilogue=lambda acc, d, r_, g_: _layer_norm_bwd(r_, g_, acc + ALPHA * d), outs=outs, n_acc=2)


def _ln_bwd_or(tag, r, g, dh):
    return dh if isinstance(dh, (tuple, list)) else _ln_bwd(tag, r, g, dh)


def _ffn_bwd(tag, saved, w_in, w_out, g, dh, ride=None, before=None):
    x, gate, up, s, r = saved[:5]
    S = x.shape[0]
    tm = min(BIG_ROWS, S)
    dr, dg, db = _ln_bwd_or(tag, r, g, dh)

    def act_epi(acc, gt, u):
        ds = 0.5 * acc
        sg = _sigmoid(gt)
        return (jnp.stack([ds * u * (sg * (1.0 + gt * (1.0 - sg))), ds * (gt * sg)]),)

    da, = _mm(f"{tag}_dact", dr, w_out, S, DFF, D, tm=tm, tn=FSH, tk=D, tb=True, order="ji",
              extras=[(gate, (tm, FSH), _tile), (up, (tm, FSH), _tile)], epilogue=act_epi,
              outs=[((2, S, DFF), BF, (2, tm, FSH), lambda i, j, k: (0, i, j))])
    d_w_out, = _mm(f"{tag}_dwout", s, dr, DFF, D, S, tm=FSH, tn=D, tk=tm, ta=True,
                   epilogue=lambda acc: (0.5 * acc,), outs=[((DFF, D), F32, (FSH, D), _tile)])
    tb_ = min(BIG_ROWS, S)
    res = _mm(f"{tag}_dx", da, w_in, S, D, 2 * DFF, tm=tb_, tn=D, tk=FSH, tb=True,
              a_spec=((None, tb_, FSH), lambda i, j, k: (k // 2, i, k % 2)),
              b_spec=((None, D, FSH), lambda i, j, k: (k, 0, 0)), ride=ride, **_grad_in(S, tb_, dr, before))
    d_in, arrived = (res[0], res[1:]) if before is None else (tuple(res[:3]), res[3:])
    d_w_in, = _mm(f"{tag}_dwin", x, da, D, 2 * DFF, S, tm=D, tn=FSH, tk=tb_, ta=True, order="ji",
                  b_spec=((None, tb_, FSH), lambda i, j, k: (j // 2, k, j % 2)),
                  outs=[((NSH, D, FSH), F32, (None, D, FSH), lambda i, j, k: (j, 0, 0))])
    return d_in, d_w_in, d_w_out, dg, db, arrived


SB = 256
NKC = SB // KB
HPF = 8
HPS = 4
GW = HPS * HD
LOG2E = 1.4426950408889634


def _split(vals):
    hi = vals.astype(BF)
    return hi, (vals - hi.astype(F32)).astype(BF)


def _chunk_sums(tri2, vals):
    hi, lo = _split(vals)
    return [jnp.dot(tri2, jnp.concatenate([hi[c * KB:(c + 1) * KB], lo[c * KB:(c + 1) * KB]], axis=0), preferred_element_type=F32)
            for c in range(NKC)]


def _head_halves(t, axis):
    idx = lax.broadcasted_iota(I32, t.shape, axis)
    return [jnp.where(idx < HD, t, 0.0).astype(BF), jnp.where(idx >= HD, t, 0.0).astype(BF)]


QT = 512


def _diag_masks(sq):
    krow, qcol = lax.broadcasted_iota(I32, (SB, sq), 0), lax.broadcasted_iota(I32, (SB, sq), 1)
    return [krow + d * SB < qcol for d in range(sq // SB)]


def _softplus2(z):
    return jnp.maximum(z, 0.0) + jnp.log2(1.0 + jnp.exp2(jnp.minimum(z, -z)))


def _attn_prep(m):
    S = m.shape[0]
    tm = min(512, S)
    n = HEADS * HD
    return _rows("attn_prep", lambda k, v: (k, v, k.T, v.T), S // tm,
                 [(m, (tm, n), lambda i: (i, 1)), (m, (tm, n), lambda i: (i, 2))],
                 [((S, n), BF, (tm, n), lambda i: (i, 0)), ((S, n), BF, (tm, n), lambda i: (i, 0)),
                  ((n, S), BF, (n, tm), lambda i: (0, i)), ((n, S), BF, (n, tm), lambda i: (0, i))])


def _attn_fwd_t(m, kb_all, vt_all, ride=None):
    S = m.shape[0]
    SQ = min(QT, S)
    assert S % SQ == 0 and SQ % SB == 0
    HPS, GW = HPF, HPF * HD
    nkc, nqb, nh, nq = S // KB, SQ // SB, HEADS // HPS, S // SQ
    nr = ride.n if ride else 0

    def body(*refs):
        q_ref, kb_hbm, vt_hbm = refs[:3]
        ride_in, (o_ref, r_ref), ride_out = refs[3:3 + nr], refs[3 + nr:5 + nr], refs[5 + nr:5 + 2 * nr]
        kb, vt, acc = refs[5 + 2 * nr:8 + 2 * nr]
        sems = refs[8 + 2 * nr:]
        h, i = pl.program_id(0), pl.program_id(1)
        if ride:
            @pl.when(jnp.logical_and(h == 0, i == 0))
            def _():
                ride.run("start", ride_in, ride_out, sems)

        @pl.when(i == 0)
        def _():
            cols = pl.ds(pl.multiple_of(h * GW, GW), GW)
            pltpu.sync_copy(kb_hbm.at[:, cols], kb)
            pltpu.sync_copy(vt_hbm.at[cols, :], vt)

        qt = (q_ref[...] * (0.125 * LOG2E)).T
        qtm = [t for g in range(HPS // 2) for t in _head_halves(qt[g * KB:(g + 1) * KB], 0)]
        dmasks = _diag_masks(SQ)
        upper = (lax.broadcasted_iota(I32, (KB, KB), 1) >= lax.broadcasted_iota(I32, (KB, KB), 0)).astype(BF)
        tri2 = jnp.concatenate([upper, upper], axis=1)
        acc[...] = jnp.zeros_like(acc)
        r_ref[...] = jnp.zeros_like(r_ref)

        def block(jb, runs, dmask):
            masked = dmask is not None
            off = pl.multiple_of(jb * SB, SB)
            groups = [slice(g * KB, (g + 1) * KB) for g in range(HPS // 2)]
            kblk = [kb[pl.ds(off, SB), s] for s in groups]
            vtb = [vt[s, pl.ds(off, SB)] for s in groups]
            old = acc[...]
            zs = [jnp.dot(kblk[hh // 2], qtm[hh], preferred_element_type=F32) for hh in range(HPS)]
            sps = [_softplus2(z) for z in zs]
            if masked:
                sps = [jnp.where(dmask, sp, 0.0) for sp in sps]
            css = [_chunk_sums(tri2, sp) for sp in sps]
            run0s = [run + cs[1][0:1, :] for run, cs in zip(runs, css)]
            aa = [jnp.exp2(z - jnp.concatenate([run0 + cs[0], run + cs[1]], axis=0)) for z, run, run0, cs in zip(zs, runs, run0s, css)]
            if masked:
                aa = [jnp.where(dmask, a, 0.0) for a in aa]
            parts = [jnp.dot(vtb[hh // 2], aa[hh].astype(BF), preferred_element_type=F32) for hh in range(HPS)]
            upd = jnp.concatenate([parts[hh][(hh % 2) * HD:(hh % 2 + 1) * HD, :] for hh in range(HPS)], axis=0)
            for hh in range(HPS):
                r_ref[hh, pl.ds(NKC * jb, 1), :] = run0s[hh]
                r_ref[hh, pl.ds(NKC * jb + 1, 1), :] = runs[hh]
            acc[...] = old + upd
            return tuple(run0 + cs[0][0:1, :] for run0, cs in zip(run0s, css))

        runs = (jnp.zeros((1, SQ), F32),) * HPS
        for d in reversed(range(nqb)):
            runs = block(i * nqb + d, runs, dmasks[d])
        lax.fori_loop(0, i * nqb, lambda t, c: block(i * nqb - 1 - t, c, None), runs)
        o_ref[...] = acc[...].T
        if ride:
            @pl.when(jnp.logical_and(h == nh - 1, i == nq - 1))
            def _():
                ride.run("wait", ride_in, ride_out, sems)

    return pl.pallas_call(
        body, name="attn_fwd", grid=(nh, nq),
        in_specs=[pl.BlockSpec((SQ, GW), lambda h, i: (i, h)), ANY, ANY] + [ANY] * nr,
        out_specs=[pl.BlockSpec((SQ, GW), lambda h, i: (i, h)), pl.BlockSpec((HPS, nkc, SQ), lambda h, i: (h, 0, i))] + [ANY] * nr,
        out_shape=[jax.ShapeDtypeStruct((S, HEADS * HD), F32), jax.ShapeDtypeStruct((HEADS, nkc, S), F32)]
        + (ride.out_shape if ride else []),
        scratch_shapes=[pltpu.VMEM((S, GW), BF), pltpu.VMEM((GW, S), BF), pltpu.VMEM((GW, SQ), F32)]
        + (_exchange_sems(nr) if ride else []),
    )(m, kb_all, vt_all, *(ride.arrays if ride else []))


def _attn_bwd_t(m, kb_all, vb_all, kt_all, runs, dy, ride=None):
    S = m.shape[0]
    SQ = min(QT, S)
    nkc, nqb, nh, nq = S // KB, SQ // SB, HEADS // HPS, S // SQ
    nr = ride.n if ride else 0

    def body(*refs):
        q_ref, kb_hbm, vb_hbm, kt_hbm, r_ref, dy_ref = refs[:6]
        ride_in, (dq_ref, dk_hbm, dv_hbm), ride_out = refs[6:6 + nr], refs[6 + nr:9 + nr], refs[9 + nr:9 + 2 * nr]
        kb, vb, kt, dqt, dka, dva = refs[9 + 2 * nr:15 + 2 * nr]
        sems = refs[15 + 2 * nr:]
        h, i = pl.program_id(0), pl.program_id(1)
        cols = pl.ds(pl.multiple_of(h * GW, GW), GW)
        if ride:
            @pl.when(jnp.logical_and(h == 0, i == 0))
            def _():
                ride.run("start", ride_in, ride_out, sems)

        @pl.when(i == 0)
        def _():
            pltpu.sync_copy(kb_hbm.at[:, cols], kb)
            pltpu.sync_copy(vb_hbm.at[:, cols], vb)
            pltpu.sync_copy(kt_hbm.at[cols, :], kt)
            dka[...] = jnp.zeros_like(dka)
            dva[...] = jnp.zeros_like(dva)

        q8 = q_ref[...] * 0.125
        dyf = dy_ref[...]
        q8t, dyt = (q8 * LOG2E).T, dyf.T
        groups = [slice(g * KB, (g + 1) * KB) for g in range(HPS // 2)]
        qtm = [t for s in groups for t in _head_halves(q8t[s], 0)]
        dytm = [t for s in groups for t in _head_halves(dyt[s], 0)]
        qlm = [t for s in groups for t in _head_halves(q8[:, s], 1)]
        dylm = [t for s in groups for t in _head_halves(dyf[:, s], 1)]
        dmasks = _diag_masks(SQ)
        ri, ci = lax.broadcasted_iota(I32, (KB, KB), 0), lax.broadcasted_iota(I32, (KB, KB), 1)
        upper, lower = (ci >= ri).astype(BF), (ci <= ri).astype(BF)
        rev2 = jnp.concatenate([upper, upper], axis=1)
        dqt[...] = jnp.zeros_like(dqt)

        def block(jb, pres, dmask):
            masked = dmask is not None
            off = pl.multiple_of(jb * SB, SB)
            heads = range(HPS)
            kblk = [kb[pl.ds(off, SB), s] for s in groups]
            vblk = [vb[pl.ds(off, SB), s] for s in groups]
            ktb = [kt[s, pl.ds(off, SB)] for s in groups]
            run0s = [r_ref[hh, pl.ds(NKC * jb, 1), :] for hh in heads]
            run1s = [r_ref[hh, pl.ds(NKC * jb + 1, 1), :] for hh in heads]
            old_dq, old_dk, old_dv = dqt[...], dka[pl.ds(off, SB), :], dva[pl.ds(off, SB), :]
            dks, dvs, parts, new = [], [], [], []
            for g in range(HPS // 2):
                hs = (2 * g, 2 * g + 1)
                zs = [jnp.dot(kblk[g], qtm[hh], preferred_element_type=F32) for hh in hs]
                das = [jnp.dot(vblk[g], dytm[hh], preferred_element_type=F32) for hh in hs]
                sps = [_softplus2(z) for z in zs]
                sigs = [jnp.exp2(z - sp) for z, sp in zip(zs, sps)]
                if masked:
                    sps = [jnp.where(dmask, sp, 0.0) for sp in sps]
                css = [_chunk_sums(rev2, sp) for sp in sps]
                aa = [jnp.exp2(z - jnp.concatenate([run0s[hh] + cs[0], run1s[hh] + cs[1]], axis=0)) for z, hh, cs in zip(zs, hs, css)]
                if masked:
                    aa = [jnp.where(dmask, a, 0.0) for a in aa]
                gs = [a * da for a, da in zip(aa, das)]
                pgs = [[jnp.dot(lower, gg[c * KB:(c + 1) * KB].astype(BF), preferred_element_type=F32) for c in range(NKC)] for gg in gs]
                pre1s = [pres[hh] + pg[0][KB - 1:KB, :] for hh, pg in zip(hs, pgs)]
                dzs = [gg - sig * jnp.concatenate([pres[hh] + pg[0], pre1 + pg[1]], axis=0)
                       for gg, sig, hh, pre1, pg in zip(gs, sigs, hs, pre1s, pgs)]
                if masked:
                    dzs = [jnp.where(dmask, dz, 0.0) for dz in dzs]
                dzb, ab = [dz.astype(BF) for dz in dzs], [a.astype(BF) for a in aa]
                dks.append(sum(jnp.dot(dzb[t], qlm[hh], preferred_element_type=F32) for t, hh in enumerate(hs)))
                dvs.append(sum(jnp.dot(ab[t], dylm[hh], preferred_element_type=F32) for t, hh in enumerate(hs)))
                parts += [jnp.dot(ktb[g], dzb[t], preferred_element_type=F32)[t * HD:(t + 1) * HD, :] for t in range(2)]
                new += [pre1 + pg[1][KB - 1:KB, :] for pre1, pg in zip(pre1s, pgs)]
            dqt[...] = old_dq + jnp.concatenate(parts, axis=0)
            dka[pl.ds(off, SB), :] = old_dk + jnp.concatenate(dks, axis=1)
            dva[pl.ds(off, SB), :] = old_dv + jnp.concatenate(dvs, axis=1)
            return tuple(new)

        pres = lax.fori_loop(0, i * nqb, lambda jb, c: block(jb, c, None), (jnp.zeros((1, SQ), F32),) * HPS)
        for d in range(nqb):
            pres = block(i * nqb + d, pres, dmasks[d])
        dq_ref[...] = (dqt[...].T * 0.125).astype(dq_ref.dtype)

        @pl.when(i == nq - 1)
        def _():
            pltpu.sync_copy(dka, dk_hbm.at[:, cols])
            pltpu.sync_copy(dva, dv_hbm.at[:, cols])

        if ride:
            @pl.when(jnp.logical_and(h == nh - 1, i == nq - 1))
            def _():
                ride.run("wait", ride_in, ride_out, sems)

    n = HEADS * HD
    return pl.pallas_call(
        body, name="attn_bwd", grid=(nh, nq),
        in_specs=[pl.BlockSpec((SQ, GW), lambda h, i: (i, h)), ANY, ANY, ANY,
                  pl.BlockSpec((HPS, nkc, SQ), lambda h, i: (h, 0, i)), pl.BlockSpec((SQ, GW), lambda h, i: (i, h))] + [ANY] * nr,
        out_specs=[pl.BlockSpec((SQ, GW), lambda h, i: (i, h)), ANY, ANY] + [ANY] * nr,
        out_shape=[jax.ShapeDtypeStruct((S, n), BF), jax.ShapeDtypeStruct((S, n), F32), jax.ShapeDtypeStruct((S, n), F32)]
        + (ride.out_shape if ride else []),
        scratch_shapes=[pltpu.VMEM((S, GW), BF), pltpu.VMEM((S, GW), BF), pltpu.VMEM((GW, S), BF), pltpu.VMEM((GW, SQ), F32),
                        pltpu.VMEM((S, GW), F32), pltpu.VMEM((S, GW), F32)] + (_exchange_sems(nr) if ride else []),
    )(m, kb_all, vb_all, kt_all, runs, dy, *(ride.arrays if ride else []))


CW = 3 * HEADS * HD


def _shift_down(cur, prev8, s):
    if s == 0:
        return cur
    r = pltpu.roll(cur, s, 0)
    first = jnp.where(lax.broadcasted_iota(I32, (8, cur.shape[1]), 0) < s, pltpu.roll(prev8, s, 0), r[:8])
    return jnp.concatenate([first, r[8:]], axis=0)


def _shift_up(cur, next8, s):
    if s == 0:
        return cur
    n = cur.shape[0]
    r = pltpu.roll(cur, n - s, 0)
    last = jnp.where(lax.broadcasted_iota(I32, (8, cur.shape[1]), 0) >= 8 - s, pltpu.roll(next8, 8 - s, 0), r[n - 8:])
    return jnp.concatenate([r[:n - 8], last], axis=0)


def _conv_fwd(m, conv_w):
    S = m.shape[0]
    tm = min(512, S)
    hb = tm // 8

    def body(x_ref, p_ref, w_ref, o_ref):
        i = pl.program_id(0)
        cur = x_ref[...]
        prev = jnp.where(i > 0, p_ref[...], 0.0)
        w = w_ref[...]
        acc = cur * w[3:4]
        for jk in range(3):
            acc = acc + _shift_down(cur, prev, 3 - jk) * w[jk:jk + 1]
        o_ref[...] = acc

    return pl.pallas_call(
        body, name="conv_fwd", grid=(S // tm,),
        in_specs=[pl.BlockSpec((tm, CW), lambda i: (i, 1)), pl.BlockSpec((8, CW), lambda i: (jnp.maximum(i * hb - 1, 0), 1)),
                  pl.BlockSpec((4, CW), lambda i: (0, 0))],
        out_specs=pl.BlockSpec((tm, CW), lambda i: (i, 0)),
        out_shape=jax.ShapeDtypeStruct((S, CW), F32),
    )(m, m, conv_w)


def _conv_bwd(m, dyc, conv_w):
    S = m.shape[0]
    tm = min(512, S)
    hb = tm // 8
    nt = S // tm

    def body(x_ref, p_ref, d_ref, n_ref, w_ref, dx_ref, dw_ref):
        i = pl.program_id(0)
        cur = x_ref[...]
        prev = jnp.where(i > 0, p_ref[...], 0.0)
        d = d_ref[...]
        nxt = jnp.where(i < nt - 1, n_ref[...], 0.0)
        w = w_ref[...]
        acc = d * w[3:4]
        dws = []
        for jk in range(3):
            acc = acc + _shift_up(d, nxt, 3 - jk) * w[jk:jk + 1]
            dws.append(jnp.sum(d * _shift_down(cur, prev, 3 - jk), axis=0, keepdims=True))
        dws.append(jnp.sum(d * cur, axis=0, keepdims=True))
        dx_ref[...] = acc.astype(dx_ref.dtype)
        dw = jnp.concatenate(dws, axis=0)

        @pl.when(i == 0)
        def _():
            dw_ref[...] = dw

        @pl.when(i > 0)
        def _():
            dw_ref[...] += dw

    return pl.pallas_call(
        body, name="conv_bwd", grid=(nt,),
        in_specs=[pl.BlockSpec((tm, CW), lambda i: (i, 1)), pl.BlockSpec((8, CW), lambda i: (jnp.maximum(i * hb - 1, 0), 1)),
                  pl.BlockSpec((tm, CW), lambda i: (i, 0)),
                  pl.BlockSpec((8, CW), lambda i: (jnp.minimum((i + 1) * hb, S // 8 - 1), 0)),
                  pl.BlockSpec((4, CW), lambda i: (0, 0))],
        out_specs=[pl.BlockSpec((tm, CW), lambda i: (i, 0)), pl.BlockSpec((4, CW), lambda i: (0, 0))],
        out_shape=[jax.ShapeDtypeStruct((S, CW), BF), jax.ShapeDtypeStruct((4, CW), F32)],
    )(m, m, dyc, dyc, conv_w)


def _t(a):
    return jnp.swapaxes(a, 1, 2)


def _bdot(a, b):
    return jnp.einsum("hik,hkj->hij", a, b, preferred_element_type=F32)


@jax.custom_vjp
def _mm1(a, b):
    return _bdot(a.astype(BF), b.astype(BF))


def _bdot_nt(a, b):
    return jnp.einsum("hij,hkj->hik", a, b, preferred_element_type=F32)


def _bdot_tn(a, b):
    return jnp.einsum("hki,hkj->hij", a, b, preferred_element_type=F32)


_mm1.defvjp(lambda a, b: (_mm1(a, b), (a.astype(BF), b.astype(BF))),
            lambda res, dc: (_bdot_nt(dc.astype(BF), res[1]), _bdot_tn(res[0], dc.astype(BF))))


@jax.custom_vjp
def _mm1_nt(a, b):
    return _bdot_nt(a.astype(BF), b.astype(BF))


_mm1_nt.defvjp(lambda a, b: (_mm1_nt(a, b), (a.astype(BF), b.astype(BF))),
               lambda res, dc: (_bdot(dc.astype(BF), res[1]), _bdot_tn(dc.astype(BF), res[0])))


@jax.custom_vjp
def _mm1_tn(a, b):
    return _bdot_tn(a.astype(BF), b.astype(BF))


_mm1_tn.defvjp(lambda a, b: (_mm1_tn(a, b), (a.astype(BF), b.astype(BF))),
               lambda res, dc: (_bdot_nt(res[1], dc.astype(BF)), _bdot(res[0], dc.astype(BF))))


def _stack_rows(hi, lo):
    return jnp.concatenate([hi, lo], axis=1)


@jax.custom_vjp
def _mm3(a, b):
    (ah, al), (bh, bl) = _split(a), _split(b)
    n = a.shape[1]
    two = _bdot(_stack_rows(ah, al), bh)
    return two[:, :n] + two[:, n:] + _bdot(ah, bl)


def _mm3_fwd(a, b):
    return _mm3(a, b), (_split(a), _split(b))


def _mm3_bwd(res, dc):
    (ah, al), (bh, bl) = res
    dh, dl = _split(dc)
    n = dc.shape[1]
    two = _bdot_nt(_stack_rows(dh, dl), bh)
    da = two[:, :n] + two[:, n:] + _bdot_nt(dh, bl)
    db = _bdot_tn(jnp.concatenate([ah, ah, al], axis=1), jnp.concatenate([dh, dl, dh], axis=1))
    return da, db


_mm3.defvjp(_mm3_fwd, _mm3_bwd)


@jax.custom_vjp
def _unit_lower_inverse(lower):
    eye = (lax.broadcasted_iota(I32, lower.shape, 1) == lax.broadcasted_iota(I32, lower.shape, 2)).astype(F32)
    pw = -lower
    inv = eye + pw
    pw = _mm3(pw, pw)
    for _ in range(4):
        both = _mm3(jnp.concatenate([inv, pw], axis=1), pw)
        inv, pw = inv + both[:, :CH], both[:, CH:]
    return inv + _mm3(inv, pw)


def _unit_lower_inverse_bwd(inv, d):
    t = _t(inv)
    return (-_mm3(_mm3(t, d), t),)


_unit_lower_inverse.defvjp(lambda lower: (_unit_lower_inverse(lower),) * 2, _unit_lower_inverse_bwd)


@jax.custom_vjp
def _saved_inverse(lower, inv):
    return inv


_saved_inverse.defvjp(lambda lower, inv: (inv, inv),
                      lambda inv, d: _unit_lower_inverse_bwd(inv, d) + (jnp.zeros_like(inv),))


def _mm_exact(c3, b):
    hi, lo = _split(b)
    lo2 = (b - hi.astype(F32) - lo.astype(F32)).astype(BF)
    return _bdot(c3, jnp.concatenate([hi, lo, lo2], axis=-2))


@jax.custom_vjp
def _cumsum_rows(b):
    return _mm_exact(_tri3(True), b)


def _tri3(lower):
    ri = lax.broadcasted_iota(I32, (HEADS, CH, CH), 1)
    ci = lax.broadcasted_iota(I32, (HEADS, CH, CH), 2)
    tri = (ri >= ci if lower else ri <= ci).astype(BF)
    return jnp.concatenate([tri, tri, tri], axis=-1)


_cumsum_rows.defvjp(lambda b: (_cumsum_rows(b), None), lambda _, dc: (_mm_exact(_tri3(False), dc),))


CPS = 4


def _gdn_chunk(yc, gz, gba, alog, dtb, inv_saved=None):
    def heads(t, off):
        return jnp.stack([t[:, off + h * HD: off + (h + 1) * HD] for h in range(HEADS)])

    def cols(t, off):
        return jnp.stack([jnp.broadcast_to(t[:, off + h: off + h + 1], (CH, CH)) for h in range(HEADS)])

    c = yc * _sigmoid(yc)
    q, k, v, zg = heads(c, 0), heads(c, HEADS * HD), heads(c, 2 * HEADS * HD), heads(gz, 0)
    q = q * lax.rsqrt(jnp.sum(q * q, axis=-1, keepdims=True) + RMS_EPS) * (HD ** -0.5)
    k = k * lax.rsqrt(jnp.sum(k * k, axis=-1, keepdims=True) + RMS_EPS)
    beta = cols(_sigmoid(gba), 0)
    g = cols(-jnp.exp(alog) * _softplus(gba + dtb), HEADS)
    ri = lax.broadcasted_iota(I32, (HEADS, CH, CH), 1)
    ci = lax.broadcasted_iota(I32, (HEADS, CH, CH), 2)
    causal, strict = ri >= ci, ri > ci
    gc = _cumsum_rows(g)
    gr = _t(gc)
    decay = jnp.where(causal, jnp.exp(jnp.where(causal, gc - gr, 0.0)), 0.0)
    lower = jnp.where(strict, beta * _mm1_nt(k, k) * decay, 0.0)
    inv = _unit_lower_inverse(lower) if inv_saved is None else _saved_inverse(lower, inv_saved)
    eg = jnp.exp(gc)
    uw = _mm3(inv, jnp.concatenate([v * beta, k * (beta * eg)], axis=2))
    u, w = uw[:, :, :HD], uw[:, :, HD:]
    qk = jnp.where(causal, _mm1_nt(q, k) * decay, 0.0)
    g_last = gc[:, CH - 1:CH, :]
    return (u, jnp.concatenate([w, q * eg], axis=1), qk, k * jnp.exp(g_last - gc), jnp.exp(g_last), zg * _sigmoid(zg)), inv


def _gdn_advance(state, pre, normw):
    u, wq, qk, kd, last, gate = pre
    ws = _mm1(wq, state)
    v_new = u - ws[:, :CH]
    o = ws[:, CH:] + _mm1(qk, v_new)
    new_state = state * last + _mm1_tn(kd, v_new)
    o = o * lax.rsqrt(jnp.mean(o * o, axis=-1, keepdims=True) + RMS_EPS) * normw * gate
    return jnp.concatenate([o[h] for h in range(HEADS)], axis=1), new_state


def _gdn_chunks(state, yc, gz, gba, alog, dtb, normw, invs=None):
    rows = [slice(c * CH, (c + 1) * CH) for c in range(yc.shape[0] // CH)]
    pres = [_gdn_chunk(yc[r], gz[r], gba[r], alog, dtb, None if invs is None else invs[c]) for c, r in enumerate(rows)]
    outs = []
    for pre, _ in pres:
        o, state = _gdn_advance(state, pre, normw)
        outs.append(o)
    return jnp.concatenate(outs, axis=0), state, jnp.stack([inv for _, inv in pres])


def _gdn_fwd(yc, m, alog, dtb, normw):
    S = yc.shape[0]
    RS = CPS * CH
    nch = S // RS

    def body(y_ref, gz_ref, gba_ref, al_ref, dt_ref, nw_ref, o_ref, st_ref, inv_ref, st):
        @pl.when(pl.program_id(0) == 0)
        def _():
            st[...] = jnp.zeros_like(st)

        cur = st[...]
        st_ref[0] = cur
        o, new, invs = _gdn_chunks(cur, y_ref[...], gz_ref[...], gba_ref[...], al_ref[...], dt_ref[...], nw_ref[...])
        o_ref[...] = o
        inv_ref[...] = invs
        st[...] = new

    og, states, invs = pl.pallas_call(
        body, name="gdn_fwd", grid=(nch,),
        in_specs=[pl.BlockSpec((RS, CW), lambda n: (n, 0)), pl.BlockSpec((RS, HEADS * HD), lambda n: (n, 6)),
                  pl.BlockSpec((RS, 128), lambda n: (n, 28)), pl.BlockSpec((1, 128), lambda n: (0, 0)),
                  pl.BlockSpec((1, 128), lambda n: (0, 0)), pl.BlockSpec((1, HD), lambda n: (0, 0))],
        out_specs=[pl.BlockSpec((RS, HEADS * HD), lambda n: (n, 0)), pl.BlockSpec((1, HEADS, HD, HD), lambda n: (n, 0, 0, 0)),
                   pl.BlockSpec((CPS, HEADS, HD, HD), lambda n: (n, 0, 0, 0))],
        out_shape=[jax.ShapeDtypeStruct((S, HEADS * HD), F32), jax.ShapeDtypeStruct((nch, HEADS, HD, HD), F32),
                   jax.ShapeDtypeStruct((S // CH, HEADS, HD, HD), F32)],
        scratch_shapes=[pltpu.VMEM((HEADS, HD, HD), F32)],
    )(yc, m, m, alog, dtb, normw)
    return og, (states, invs)


def _gdn_bwd(yc, m, alog, dtb, normw, states, dog):
    S = yc.shape[0]
    RS = CPS * CH
    nch = S // RS
    states, invs = states

    def body(y_ref, gz_ref, gba_ref, al_ref, dt_ref, nw_ref, st_ref, inv_ref, do_ref, dy_ref, dgz_ref, dgba_ref, dal_ref, ddt_ref,
             dnw_ref, dst):
        n = pl.program_id(0)

        @pl.when(n == 0)
        def _():
            dst[...] = jnp.zeros_like(dst)

        kept = inv_ref[...]
        _, vjp = jax.vjp(lambda *a: _gdn_chunks(*a, invs=kept)[:2],
                         st_ref[0], y_ref[...], gz_ref[...], gba_ref[...], al_ref[...], dt_ref[...], nw_ref[...])
        d_state, d_y, d_gz, d_gba, d_al, d_dt, d_nw = vjp((do_ref[...], dst[...]))
        dst[...] = d_state
        dy_ref[...] = d_y
        dgz_ref[...] = d_gz.astype(dgz_ref.dtype)
        dgba_ref[...] = d_gba.astype(dgba_ref.dtype)
        for ref, val in ((dal_ref, d_al), (ddt_ref, d_dt), (dnw_ref, d_nw)):
            @pl.when(n == 0)
            def _(ref=ref, val=val):
                ref[...] = val

            @pl.when(n > 0)
            def _(ref=ref, val=val):
                ref[...] += val

    rev = lambda n: nch - 1 - n
    return pl.pallas_call(
        body, name="gdn_bwd", grid=(nch,),
        in_specs=[pl.BlockSpec((RS, CW), lambda n: (rev(n), 0)), pl.BlockSpec((RS, HEADS * HD), lambda n: (rev(n), 6)),
                  pl.BlockSpec((RS, 128), lambda n: (rev(n), 28)), pl.BlockSpec((1, 128), lambda n: (0, 0)),
                  pl.BlockSpec((1, 128), lambda n: (0, 0)), pl.BlockSpec((1, HD), lambda n: (0, 0)),
                  pl.BlockSpec((1, HEADS, HD, HD), lambda n: (rev(n), 0, 0, 0)),
                  pl.BlockSpec((CPS, HEADS, HD, HD), lambda n: (rev(n), 0, 0, 0)),
                  pl.BlockSpec((RS, HEADS * HD), lambda n: (rev(n), 0))],
        out_specs=[pl.BlockSpec((RS, CW), lambda n: (rev(n), 0)), pl.BlockSpec((RS, HEADS * HD), lambda n: (rev(n), 0)),
                   pl.BlockSpec((RS, 128), lambda n: (rev(n), 0)), pl.BlockSpec((1, 128), lambda n: (0, 0)),
                   pl.BlockSpec((1, 128), lambda n: (0, 0)), pl.BlockSpec((1, HD), lambda n: (0, 0))],
        out_shape=[jax.ShapeDtypeStruct((S, CW), F32), jax.ShapeDtypeStruct((S, HEADS * HD), BF),
                   jax.ShapeDtypeStruct((S, 128), BF), jax.ShapeDtypeStruct((1, 128), F32),
                   jax.ShapeDtypeStruct((1, 128), F32), jax.ShapeDtypeStruct((1, HD), F32)],
        scratch_shapes=[pltpu.VMEM((HEADS, HD, HD), F32)],
    )(yc, m, m, alog, dtb, normw, states, invs, dog)


def _mixer_fwd(h1, W, late=None, h1b=None):
    S = h1.shape[0]
    tm = min(512, S)
    n = HEADS * HD
    tb_ = min(BIG_ROWS, S)
    m, = _mm("mix_in", h1 if h1b is None else h1b, W["wp"], S, NP, D, tm=tb_, tn=1536, tk=D, order="ji", outs=[((S, NP), F32, (tb_, 1536), _tile)])
    kb_all, vb_all, kt_all, vt_all = _attn_prep(m)
    ya, runs, *arrived = _attn_fwd_t(m, kb_all, vt_all, ride=late[0] if late else None)
    if late:
        late[1](W, arrived)
    runs = (kb_all, vb_all, kt_all, runs)
    b_gate, conv_w, alog, dtb, normw, w_sb, w_gdn, w_mo, g, b = (
        W[k] for k in ("b_gate", "conv_w", "alog", "dtb", "normw", "w_sb", "w_gdn", "w_mo", "ln2_g", "ln2_b"))
    yc = _conv_fwd(m, conv_w)
    og, states = _gdn_fwd(yc, m, alog, dtb, normw)
    ysb, = _mm("mix_sb", ya, w_sb, S, D, n, tm=tm, tn=D, tk=n, outs=[((S, D), F32, (tm, D), _row)])

    def merge_epi(acc, ys, gs, gg, bg):
        return _sigmoid(gs + bg[:, :D]) * ys + _sigmoid(gg + bg[:, D:]) * acc, acc

    u, ygdn = _mm("mix_gdn", og, w_gdn, S, D, n, tm=tm, tn=D, tk=n,
                  extras=[(ysb, (tm, D), _row), (m, (tm, D), lambda i, j, k: (i, GATE0 // D)),
                          (m, (tm, D), lambda i, j, k: (i, GATE0 // D + 1)), (b_gate, (1, 2 * D), _const)],
                  epilogue=merge_epi, outs=[((S, D), BF, (tm, D), _row), ((S, D), F32, (tm, D), _row)])

    def out_epi(acc, xin, gg, bb):
        r = ALPHA * xin + acc
        return _layer_norm(r, gg, bb), r

    h2, r2 = _mm("mix_out", u, w_mo, S, D, D, tm=tm, tn=D, tk=D,
                 extras=[(h1, (tm, D), _row), (g, (1, D), _const), (b, (1, D), _const)], epilogue=out_epi,
                 outs=[((S, D), F32, (tm, D), _row), ((S, D), F32, (tm, D), _row)])
    return h2, (h1 if h1b is None else h1b, m, ya, runs, yc, og, states, ysb, ygdn, u, r2)


def _mixer_bwd(saved, W, dh, ride=None, before=None):
    h1, m, ya, runs, yc, og, states, ysb, ygdn, u, r2 = saved
    wp, b_gate, conv_w, alog, dtb, normw, w_sb, w_gdn, w_mo, g = (
        W[k] for k in ("wp", "b_gate", "conv_w", "alog", "dtb", "normw", "w_sb", "w_gdn", "w_mo", "ln2_g"))
    S = h1.shape[0]
    tm = min(512, S)
    n = HEADS * HD
    dr, dg, db = _ln_bwd_or("mix", r2, g, dh)

    def merge_epi(du, ys, yg, gs, gg, bg):
        s1, s2 = _sigmoid(gs + bg[:, :D]), _sigmoid(gg + bg[:, D:])
        dgate = jnp.concatenate([du * ys * s1 * (1.0 - s1), du * yg * s2 * (1.0 - s2)], axis=1)
        return du * s1, du * s2, dgate, jnp.sum(dgate, axis=0, keepdims=True)

    dysb, dygdn, dgate, d_bg = _mm(
        "mix_dmerge", dr, w_mo, S, D, D, tm=tm, tn=D, tk=D, tb=True,
        extras=[(ysb, (tm, D), _row), (ygdn, (tm, D), _row), (m, (tm, D), lambda i, j, k: (i, GATE0 // D)),
                (m, (tm, D), lambda i, j, k: (i, GATE0 // D + 1)), (b_gate, (1, 2 * D), _const)],
        epilogue=merge_epi, n_acc=1,
        outs=[((S, D), BF, (tm, D), _row), ((S, D), BF, (tm, D), _row), ((S, 2 * D), BF, (tm, 2 * D), _row),
              ((1, 2 * D), F32, (1, 2 * D), _const)])
    d_w_mo, = _mm("mix_dwmo", u, dr, D, D, S, tm=D, tn=D, tk=min(BIG_ROWS, S), ta=True, outs=[((D, D), F32, (D, D), _tile)])
    dya, = _mm("mix_dya", dysb, w_sb, S, n, D, tm=tm, tn=n, tk=D, tb=True, outs=[((S, n), F32, (tm, n), _row)])
    col_shards = [((NSH, n, D // NSH), F32, (None, n, D // NSH), lambda i, j, k: (j, 0, 0))]
    d_w_sb, = _mm("mix_dwsb", ya, dysb, n, D, S, tm=n, tn=D // NSH, tk=min(BIG_ROWS, S), ta=True, order="ji", outs=col_shards)
    dog, = _mm("mix_dog", dygdn, w_gdn, S, n, D, tm=tm, tn=n, tk=D, tb=True, outs=[((S, n), F32, (tm, n), _row)])
    d_w_gdn, = _mm("mix_dwgdn", og, dygdn, n, D, S, tm=n, tn=D // NSH, tk=min(BIG_ROWS, S), ta=True, order="ji", outs=col_shards)
    dq, dk, dv, *arrived = _attn_bwd_t(m, *runs, dya, ride=ride)
    dyc, dgz, dgba, d_alog, d_dtb, d_normw = _gdn_bwd(yc, m, alog, dtb, normw, states, dog)
    dxc, d_conv = _conv_bwd(m, dyc, conv_w)
    dm = jnp.concatenate([dq, dk.astype(BF), dv.astype(BF), dxc, dgz, dgba, jnp.zeros((S, GATE0 - 3712), BF), dgate], axis=1)
    tb_ = min(BIG_ROWS, S)
    res = _mm("mix_dh", dm, wp, S, D, NP, tm=tb_, tn=D, tk=1536, tb=True, **_grad_in(S, tb_, dr, before))
    d_h1 = res[0] if before is None else tuple(res)
    d_wp, = _mm("mix_dwp", h1, dm, D, NP, S, tm=D, tn=1536, tk=tb_, ta=True, order="ji",
                outs=[((D, NP), F32, (D, 1536), _tile)])
    return d_h1, dict(wp=d_wp, b_gate=d_bg, conv_w=d_conv, alog=d_alog, dtb=d_dtb, normw=d_normw,
                      w_sb=d_w_sb, w_gdn=d_w_gdn, w_mo=d_w_mo, g=dg, b=db), arrived


def _ple_fwd(h3, p, w_pg, b_pg, w_ple, g, b, target):
    S = h3.shape[0]
    tm = min(512, S)
    pd = p.shape[1]
    pe, = _mm("ple_emb", p, w_ple, S, D, pd, tm=tm, tn=D, tk=pd, outs=[((S, D), F32, (tm, D), _row)])

    def epi(acc, e, xin, tgt, bp, gg, bb):
        gt = _sigmoid(acc + bp)
        r = ALPHA * xin + gt * e
        diff = _layer_norm(r, gg, bb) - tgt
        return gt, r, diff * (1.0 / D), jnp.sum(diff * diff, axis=0, keepdims=True)

    gt, r4, dh4, loss_row = _mm(
        "ple_gate", h3, w_pg, S, D, D, tm=tm, tn=D, tk=D,
        extras=[(pe, (tm, D), _row), (h3, (tm, D), _row), (target, (tm, D), _row), (b_pg, (1, D), _const),
                (g, (1, D), _const), (b, (1, D), _const)], epilogue=epi, n_acc=1,
        outs=[((S, D), F32, (tm, D), _row), ((S, D), F32, (tm, D), _row), ((S, D), F32, (tm, D), _row),
              ((1, D), F32, (1, D), _const)])
    return dh4, loss_row, (h3, p, pe, gt, r4)


def _ple_bwd(saved, w_pg, g, dh4, before=None):
    h3, p, pe, gt, r4 = saved
    S = h3.shape[0]
    tm = min(512, S)
    pd = p.shape[1]

    def fn(r, gg, dh, e, t):
        dr, dg, db = _layer_norm_bwd(r, gg, dh)
        dpre = dr * e * t * (1.0 - t)
        return dr, dpre, dr * t, dg, db, jnp.sum(dpre, axis=0, keepdims=True)

    row, one = (lambda i: (i, 0)), (lambda i: (0, 0))
    dr, dpre, dpe, dg, db, d_bpg = _rows(
        "ple_lnbwd", fn, S // tm,
        [(r4, (tm, D), row), (g, (1, D), one), (dh4, (tm, D), row), (pe, (tm, D), row), (gt, (tm, D), row)],
        [((S, D), F32, (tm, D), row), ((S, D), BF, (tm, D), row), ((S, D), BF, (tm, D), row),
         ((1, D), F32, (1, D), one), ((1, D), F32, (1, D), one), ((1, D), F32, (1, D), one)], n_acc=3)
    d_w_pg, = _mm("ple_dwpg", h3, dpre, D, D, S, tm=D, tn=D, tk=min(BIG_ROWS, S), ta=True, outs=[((D, D), F32, (D, D), _tile)])
    d_w_ple, = _mm("ple_dwple", p, dpe, pd, D, S, tm=pd, tn=D // NSH, tk=min(BIG_ROWS, S), ta=True, order="ji",
                   outs=[((NSH, pd, D // NSH), F32, (None, pd, D // NSH), lambda i, j, k: (j, 0, 0))])
    res = _mm("ple_dh", dpre, w_pg, S, D, D, tm=tm, tn=D, tk=D, tb=True, **_grad_in(S, tm, dr, before))
    return (res[0] if before is None else tuple(res)), d_w_pg, d_bpg, d_w_ple, dg, db


def _local_step(x, p, target, W, now=None, soon=None, late=None, early=None, mid=None):
    W = dict(W)
    h1, sv1, arrived, W["ffn1_out"] = _ffn_fwd("ffn1", x, W["ffn1_in"], W.get("ffn1_out"), W["ln1_g"], W["ln1_b"],
                                               ride=soon[0] if soon else None, ride2=soon[1] if soon else None,
                                               w_out_ride=now)
    if soon:
        soon[2](W, arrived)
    h2, sv2 = _mixer_fwd(h1, W, late, h1b=sv1[5])
    h3, sv3, _, _ = _ffn_fwd("ffn2", h2, W["ffn2_in"], W["ffn2_out"], W["ln3_g"], W["ln3_b"])
    dh4, loss_row, sv4 = _ple_fwd(h3, p, W["w_pg"], W["b_pg"], W["w_ple"], W["ln4_g"], W["ln4_b"], target)
    G = {}
    dh3, G["w_pg"], G["b_pg"], G["w_ple"], G["ln4_g"], G["ln4_b"] = _ple_bwd(sv4, W["w_pg"], W["ln4_g"], dh4,
                                                                              before=(sv3[4], W["ln3_g"]))
    dh2, G["ffn2_in"], G["ffn2_out"], G["ln3_g"], G["ln3_b"], _ = _ffn_bwd(
        "ffn2", sv3, W["ffn2_in"], W["ffn2_out"], W["ln3_g"], dh3, before=(sv2[-1], W["ln2_g"]))
    dh1, gm, G["early"] = _mixer_bwd(sv2, W, dh2, ride=early(G) if early else None, before=(sv1[4], W["ln1_g"]))
    G.update({k: v for k, v in gm.items() if k not in ("g", "b")})
    G["ln2_g"], G["ln2_b"] = gm["g"], gm["b"]
    dx, G["ffn1_in"], G["ffn1_out"], G["ln1_g"], G["ln1_b"], G["mid"] = _ffn_bwd(
        "ffn1", sv1, W["ffn1_in"], W["ffn1_out"], W["ln1_g"], dh1, ride=mid(G) if mid else None)
    return loss_row, dx, G


S2 = CUT - 2 * MSH


def _pack_wp(w4):
    tr = 256

    def fn(w):
        s = [w[j].astype(F32) for j in range(NSH)]
        full = jnp.concatenate([s[0], s[1], s[2][:, :S2], jnp.zeros((tr, GATE0 - CUT), F32), s[2][:, S2:], s[3]], axis=1)
        return (full,)

    return _rows("pack_wp", fn, D // tr, [(w4, (NSH, tr, MSH), lambda i: (0, i, 0))],
                 [((D, NP), BF, (tr, NP), lambda i: (i, 0))])[0]


def _unpack_wp(d):
    tr = 256
    g2 = GATE0 + MSH - S2

    def fn(v):
        return (jnp.stack([v[:, :MSH], v[:, MSH:2 * MSH], jnp.concatenate([v[:, 2 * MSH:CUT], v[:, GATE0:g2]], axis=1), v[:, g2:]]),)

    return _rows("unpack_wp", fn, D // tr, [(d, (tr, NP), lambda i: (i, 0))],
                 [((NSH, D, MSH), F32, (NSH, tr, MSH), lambda i: (0, i, 0))])[0]


def _cast_bf16(tag, w):
    r, c = w.shape
    tr = _pick(r, 256)
    return _rows(f"cast_{tag}", lambda v: (v,), r // tr, [(w, (tr, c), lambda i: (i, 0))],
                 [((r, c), BF, (tr, c), lambda i: (i, 0))])[0]


def _place():
    return lax.axis_index("x"), lax.axis_index("y"), lax.axis_index("c")


def _chip_exchange(phase, scatter, ins, outs, send, recv, loc):
    x, y, c = _place()
    me = 2 * x + y
    chips = [(1 - x, y), (x, 1 - y), (1 - x, 1 - y)]
    for t in range(len(ins)):
        own = pltpu.make_async_copy(ins[t].at[me] if scatter else ins[t], outs[t].at[me], loc.at[t])
        out_going, in_coming = [], []
        for q, (px, py) in enumerate(chips):
            src = ins[t].at[2 * px + py] if scatter else ins[t]
            sems = dict(send_sem=send.at[3 * t + q], recv_sem=recv.at[3 * t + q], device_id=(px, py, c), device_id_type=MESH)
            out_going.append(pltpu.make_async_remote_copy(src_ref=src, dst_ref=outs[t].at[me], **sems))
            in_coming.append(pltpu.make_async_remote_copy(src_ref=src, dst_ref=outs[t].at[2 * px + py], **sems))
        if phase == "start":
            own.start()
            for cp in out_going:
                cp.start()
        else:
            for cp in in_coming:
                cp.wait_recv()
            own.wait()
            for cp in out_going:
                cp.wait_send()


def _exchange_sems(n):
    return [pltpu.SemaphoreType.DMA((3 * n,)), pltpu.SemaphoreType.DMA((3 * n,)), pltpu.SemaphoreType.DMA((n,))]


class _Ride:
    def __init__(self, scatter, arrays):
        self.scatter, self.arrays, self.n = scatter, list(arrays), len(arrays)
        self.out_shape = [jax.ShapeDtypeStruct(a.shape if scatter else (NSH,) + a.shape, a.dtype) for a in self.arrays]

    def run(self, phase, refs_in, refs_out, sems):
        _chip_exchange(phase, self.scatter, refs_in, refs_out, *sems)


def _gather_shards(shards):
    n = len(shards)

    def body(*refs):
        _chip_exchange("start", False, refs[:n], refs[n:2 * n], *refs[2 * n:])
        _chip_exchange("wait", False, refs[:n], refs[n:2 * n], *refs[2 * n:])

    return pl.pallas_call(
        body, name="gather_weights", in_specs=[ANY] * n, out_specs=[ANY] * n,
        out_shape=[jax.ShapeDtypeStruct((NSH,) + s.shape, s.dtype) for s in shards], scratch_shapes=_exchange_sems(n),
    )(*shards)


def _reduce_chips(ps):
    n = len(ps)

    def body(*refs):
        _chip_exchange("start", True, refs[:n], refs[n:2 * n], *refs[2 * n:])
        _chip_exchange("wait", True, refs[:n], refs[n:2 * n], *refs[2 * n:])

    return pl.pallas_call(
        body, name="reduce_chips", in_specs=[ANY] * n, out_specs=[ANY] * n,
        out_shape=[jax.ShapeDtypeStruct(p_.shape, p_.dtype) for p_ in ps], scratch_shapes=_exchange_sems(n),
    )(*ps)


def _chunk_rows(h, c):
    return _pick(h, max(8, 524288 // c // 8 * 8))


def _sibling_sum(tag, g):
    _, r, c = g.shape
    h = r // 2
    tr = _chunk_rows(h, c)
    nch = h // tr
    steps = NSH * nch

    def body(top_ref, bot_ref, out_ref, narrow_ref, stage, land, send, recv):
        s = pl.program_id(0)
        x, y, core = _place()

        def exchange(keep_ref, give_ref):
            stage[...] = give_ref[0].astype(BF)
            cp = pltpu.make_async_remote_copy(src_ref=stage, dst_ref=land.at[s], send_sem=send.at[s], recv_sem=recv.at[s],
                                              device_id=(x, y, 1 - core), device_id_type=MESH)
            cp.start()
            cp.wait()
            total = keep_ref[0] + land[s].astype(F32)
            out_ref[0] = total
            narrow_ref[0] = total.astype(BF)

        @pl.when(core == 0)
        def _():
            exchange(top_ref, bot_ref)

        @pl.when(core == 1)
        def _():
            exchange(bot_ref, top_ref)

    return pl.pallas_call(
        body, name=f"sibling_sum_{tag}", grid=(steps,),
        in_specs=[pl.BlockSpec((1, tr, c), lambda s: (s // nch, s % nch, 0)),
                  pl.BlockSpec((1, tr, c), lambda s: (s // nch, nch + s % nch, 0))],
        out_specs=[pl.BlockSpec((1, tr, c), lambda s: (s // nch, s % nch, 0))] * 2,
        out_shape=[jax.ShapeDtypeStruct((NSH, h, c), F32), jax.ShapeDtypeStruct((NSH, h, c), BF)],
        scratch_shapes=[pltpu.VMEM((tr, c), BF), pltpu.VMEM((steps, tr, c), BF), pltpu.SemaphoreType.DMA((steps,)),
                        pltpu.SemaphoreType.DMA((steps,))],
    )(g, g)


def _chip_sum_share(tag, b, own):
    _, h, c = b.shape
    tr = _chunk_rows(h, c)
    steps = h // tr

    def body(b_ref, own_ref, out_ref, stage, land, send, recv):
        s = pl.program_id(0)
        x, y, core = _place()
        me = 2 * x + y
        v = [jnp.where(me == j, own_ref[j], b_ref[j].astype(F32)) for j in range(NSH)]
        total = ((v[0] + v[1]) + v[2]) + v[3]
        stage[...] = total
        cp = pltpu.make_async_remote_copy(src_ref=stage, dst_ref=land.at[s], send_sem=send.at[s], recv_sem=recv.at[s],
                                          device_id=(x, y, 1 - core), device_id_type=MESH)
        cp.start()
        cp.wait()
        out_ref[core] = total
        out_ref[1 - core] = land[s]

    return pl.pallas_call(
        body, name=f"chip_sum_share_{tag}", grid=(steps,),
        in_specs=[pl.BlockSpec((NSH, tr, c), lambda s: (0, s, 0))] * 2,
        out_specs=pl.BlockSpec((2, tr, c), lambda s: (0, s, 0)),
        out_shape=jax.ShapeDtypeStruct((2, h, c), F32),
        scratch_shapes=[pltpu.VMEM((tr, c), F32), pltpu.VMEM((steps, tr, c), F32), pltpu.SemaphoreType.DMA((steps,)),
                        pltpu.SemaphoreType.DMA((steps,))],
    )(b, own)


NDEV = 8


def _allreduce_small(pack):
    r, w = pack.shape
    rel = [(dx, dy, dc) for dx in (0, 1) for dy in (0, 1) for dc in (0, 1) if (dx, dy, dc) != (0, 0, 0)]

    def body(in_ref, out_ref, buf, send, recv):
        x, y, c = _place()
        me = 4 * x + 2 * y + c
        buf[me] = in_ref[...]
        peers = [((x + dx) % 2, (y + dy) % 2, (c + dc) % 2) for dx, dy, dc in rel]
        sent = []
        for k, peer in enumerate(peers):
            cp = pltpu.make_async_remote_copy(src_ref=in_ref, dst_ref=buf.at[me], send_sem=send.at[k], recv_sem=recv.at[k],
                                              device_id=peer, device_id_type=MESH)
            cp.start()
            sent.append(cp)
        for k, (px, py, pc) in enumerate(peers):
            pltpu.make_async_remote_copy(src_ref=in_ref, dst_ref=buf.at[4 * px + 2 * py + pc], send_sem=send.at[k], recv_sem=recv.at[k],
                                         device_id=(px, py, pc), device_id_type=MESH).wait_recv()
        for cp in sent:
            cp.wait_send()
        acc = buf[0]
        for k in range(1, NDEV):
            acc = acc + buf[k]
        out_ref[...] = acc

    vm = pl.BlockSpec(memory_space=pltpu.VMEM)
    return pl.pallas_call(
        body, name="allreduce_small", in_specs=[vm], out_specs=vm, out_shape=jax.ShapeDtypeStruct((r, w), F32),
        scratch_shapes=[pltpu.VMEM((NDEV, r, w), F32), pltpu.SemaphoreType.DMA((NDEV - 1,)), pltpu.SemaphoreType.DMA((NDEV - 1,))],
    )(pack)


def _adamw(tag, w, g, m, v):
    r, c = w.shape
    tr = _pick(r, max(8, 262144 // c // 8 * 8))

    def fn(w_, g_, m_, v_):
        m2 = B1 * m_ + (1.0 - B1) * g_
        v2 = B2 * v_ + (1.0 - B2) * (g_ * g_)
        m_hat = m2 / (1.0 - B1 ** STEP)
        v_hat = v2 / (1.0 - B2 ** STEP)
        return -LR * (m_hat / (jnp.sqrt(v_hat) + EPS) + WD * w_), m2, v2

    spec = ((tr, c), lambda i: (i, 0))
    return _rows(f"adamw_{tag}", fn, r // tr, [(a,) + spec for a in (w, g, m, v)], [((r, c), F32) + spec] * 3)


BIG = ("ffn1_w_in", "ffn1_w_out", "w_mix_in", "w_branch_sb", "w_branch_gdn", "w_mix_out", "ffn2_w_in", "ffn2_w_out",
       "w_ple_gate", "w_ple")
FIRST = ("ffn1_w_in",)
SOON = ("w_mix_in",)
LATER = tuple(n for n in BIG if n not in FIRST + SOON + ("ffn1_w_out",))
EARLY = ("ffn2_w_in", "ffn2_w_out", "w_ple_gate", "w_ple")
MID = ("w_mix_in", "w_branch_sb", "w_branch_gdn", "w_mix_out")
LAST = ("ffn1_w_in", "ffn1_w_out")
SMALL = ("ln1_g", "ln1_b", "b_gate", "conv_w", "a_log", "dt_bias", "gdn_norm_w", "ln2_g", "ln2_b", "ln3_g", "ln3_b",
         "b_ple_gate", "ln4_g", "ln4_b")
ORDER = ("ffn1_w_in", "ffn1_w_out", "ln1_g", "ln1_b", "w_mix_in", "b_gate", "conv_w", "a_log", "dt_bias", "gdn_norm_w",
         "w_branch_sb", "w_branch_gdn", "w_mix_out", "ln2_g", "ln2_b", "ffn2_w_in", "ffn2_w_out", "ln3_g", "ln3_b",
         "w_ple_gate", "b_ple_gate", "w_ple", "ln4_g", "ln4_b")
PACK_W = 2304


def _lane_row(v, lanes=128, at=HEADS):
    return jnp.pad(v[None, :], ((0, 0), (at, lanes - at - v.shape[0])))


def _col_join(w4):
    return jnp.transpose(w4, (1, 0, 2)).reshape(w4.shape[1], NSH * w4.shape[2])


def kernel(x, p, ffn1_w_in, ffn1_w_out, ln1_g, ln1_b, w_mix_in, b_gate, conv_w, a_log, dt_bias, gdn_norm_w, w_branch_sb, w_branch_gdn, w_mix_out, ln2_g, ln2_b, ffn2_w_in, ffn2_w_out, ln3_g, ln3_b, w_ple_gate, b_ple_gate, w_ple, ln4_g, ln4_b, loss_target, m_ffn1_w_in, m_ffn1_w_out, m_ln1_g, m_ln1_b, m_w_mix_in, m_b_gate, m_conv_w, m_a_log, m_dt_bias, m_gdn_norm_w, m_w_branch_sb, m_w_branch_gdn, m_w_mix_out, m_ln2_g, m_ln2_b, m_ffn2_w_in, m_ffn2_w_out, m_ln3_g, m_ln3_b, m_w_ple_gate, m_b_ple_gate, m_w_ple, m_ln4_g, m_ln4_b, v_ffn1_w_in, v_ffn1_w_out, v_ln1_g, v_ln1_b, v_w_mix_in, v_b_gate, v_conv_w, v_a_log, v_dt_bias, v_gdn_norm_w, v_w_branch_sb, v_w_branch_gdn, v_w_mix_out, v_ln2_g, v_ln2_b, v_ffn2_w_in, v_ffn2_w_out, v_ln3_g, v_ln3_b, v_w_ple_gate, v_b_ple_gate, v_w_ple, v_ln4_g, v_ln4_b):
    args = dict(locals())
    w = {n: args[n][0] for n in ORDER}
    mom = {n: args["m_" + n][0] for n in ORDER}
    var = {n: args["v_" + n][0] for n in ORDER}

    cast = {n: _cast_bf16(n, w[n]) for n in BIG if n not in SOON}
    mix_halves = [_cast_bf16(f"w_mix_in_{t}", w["w_mix_in"][t * (D // 2):(t + 1) * (D // 2)]) for t in range(2)]
    full = dict(zip(FIRST + ("conv_w",), _gather_shards([cast[n] for n in FIRST] + [w["conv_w"]])))
    W = dict(
        ffn1_in=full["ffn1_w_in"], conv_w=_col_join(full["conv_w"]), b_gate=w["b_gate"][None], alog=_lane_row(w["a_log"]), dtb=_lane_row(w["dt_bias"]),
        normw=w["gdn_norm_w"][None], b_pg=w["b_ple_gate"][None],
        **{f"ln{i}_{s}": w[f"ln{i}_{s}"][None] for i in (1, 2, 3, 4) for s in ("g", "b")},
    )

    def fill(W_, arrived):
        got = dict(zip(LATER, arrived))
        W_.update(w_sb=_col_join(got["w_branch_sb"]), w_gdn=_col_join(got["w_branch_gdn"]), w_mo=got["w_mix_out"].reshape(D, D),
                  ffn2_in=got["ffn2_w_in"], ffn2_out=got["ffn2_w_out"].reshape(DFF, D),
                  w_pg=got["w_ple_gate"].reshape(D, D), w_ple=_col_join(got["w_ple"]))

    def by_shard(G_, names):
        forms = dict(
            ffn1_w_in=lambda: G_["ffn1_in"], ffn1_w_out=lambda: G_["ffn1_out"].reshape(NSH, DFF // NSH, D),
            w_mix_in=lambda: _unpack_wp(G_["wp"]), w_branch_sb=lambda: G_["w_sb"], w_branch_gdn=lambda: G_["w_gdn"],
            w_mix_out=lambda: G_["w_mo"].reshape(NSH, D // NSH, D), ffn2_w_in=lambda: G_["ffn2_in"],
            ffn2_w_out=lambda: G_["ffn2_out"].reshape(NSH, DFF // NSH, D),
            w_ple_gate=lambda: G_["w_pg"].reshape(NSH, D // NSH, D), w_ple=lambda: G_["w_ple"])
        return [_sibling_sum(n, forms[n]()) for n in names]

    early_sums, mid_sums = [], []

    def early(G_):
        early_sums.extend(by_shard(G_, EARLY))
        return _Ride(True, [narrow for _, narrow in early_sums])

    def mid(G_):
        mid_sums.extend(by_shard(G_, MID))
        return _Ride(True, [narrow for _, narrow in mid_sums])

    loss_row, grad_x, G = _local_step(
        x[0], p[0, 0], loss_target[0], W, now=_Ride(False, [cast["ffn1_w_out"]]),
        soon=(_Ride(False, mix_halves[:1]), _Ride(False, mix_halves[1:]),
              lambda W_, arrived: W_.update(wp=_pack_wp(jnp.concatenate(arrived, axis=1)))),
        late=(_Ride(False, [cast[n] for n in LATER]), fill), early=early, mid=mid)
    loss = lax.psum(0.5 * jnp.sum(loss_row) / D, ("x", "y", "c"))

    last_sums = by_shard(G, LAST)
    landed = list(G["early"]) + list(G["mid"]) + list(_reduce_chips([narrow for _, narrow in last_sums]))
    grad = {n: _chip_sum_share(n, b, own).reshape(w[n].shape)
            for n, b, (own, _) in zip(EARLY + MID + LAST, landed, early_sums + mid_sums + last_sums)}

    pieces = [G["ln1_g"], G["ln1_b"], G["b_gate"], G["conv_w"].reshape(1, 4 * CW), G["alog"], G["dtb"], G["normw"],
              G["ln2_g"], G["ln2_b"], G["ln3_g"], G["ln3_b"], G["b_pg"], G["ln4_g"], G["ln4_b"]]
    flat = jnp.concatenate(pieces, axis=1)
    flat = jnp.pad(flat, ((0, 0), (0, NDEV * PACK_W - flat.shape[1])))
    total = _allreduce_small(flat.reshape(NDEV, PACK_W)).reshape(1, NDEV * PACK_W)
    off = 0
    for n, piece in zip(SMALL, pieces):
        grad[n] = total[0, off:off + piece.shape[1]]
        off += piece.shape[1]
    chip = 2 * lax.axis_index("x") + lax.axis_index("y")
    grad["conv_w"] = lax.dynamic_slice_in_dim(grad["conv_w"].reshape(4, CW), chip * (CW // NSH), CW // NSH, axis=1)
    grad["a_log"] = grad["a_log"][HEADS:2 * HEADS]
    grad["dt_bias"] = grad["dt_bias"][HEADS:2 * HEADS]

    delta, new_m, new_v = {}, {}, {}
    for n in ORDER:
        shape2 = w[n].shape if w[n].ndim == 2 else (1, w[n].shape[0])
        d_, m_, v_ = _adamw(n, *[a.reshape(shape2) for a in (w[n], grad[n], mom[n], var[n])])
        delta[n], new_m[n], new_v[n] = (a.reshape(args[n].shape) for a in (d_, m_, v_))
    outs = [loss, grad_x[None]]
    outs += [grad[n].reshape(args[n].shape) for n in ORDER]
    for group in (delta, new_m, new_v):
        outs += [group[n] for n in ORDER]
    return tuple(outs)
```
